```python
import math
import jax, jax.numpy as jnp
from jax import lax
import numpy as np

D_MODEL = 1024
BATCH = 8
SEQ = 2048
DEPTH = 1

ATTN_HEADS = 8
ATTN_HEAD_DIM = 64
ATTN_WIDTH = ATTN_HEADS * ATTN_HEAD_DIM
MOBA_BLOCK = 256
MOBA_TOPK = 3
MOBA_QUERY_BLOCK = 16
ROPE_THETA = 10000.0
MLSTM_HEADS = 4
MLSTM_HEAD_DIM = 128
MLSTM_WIDTH = MLSTM_HEADS * MLSTM_HEAD_DIM
MLSTM_CONV = 4
MLSTM_CHUNK = 64
MOE_GROUPS = 8
MOE_EXPERTS_PER_GROUP = 8
MOE_EXPERTS = MOE_GROUPS * MOE_EXPERTS_PER_GROUP
MOE_TOPK = 2
MOE_D_FF = 512
MOE_BLOCK = 128
LN_EPS = 1e-5
GN_EPS = 1e-6
DEEPNORM_ALPHA = (2 * DEPTH) ** 0.25
DEEPNORM_BETA = (8 * DEPTH) ** -0.25
IN_SPLIT_SIZES = (ATTN_WIDTH, ATTN_WIDTH, ATTN_WIDTH, MLSTM_WIDTH, MLSTM_WIDTH, MLSTM_WIDTH, MLSTM_HEADS, MLSTM_HEADS, D_MODEL, D_MODEL)
IN_WIDTH = 3 * ATTN_WIDTH + 3 * MLSTM_WIDTH + 2 * MLSTM_HEADS + 2 * D_MODEL

kernel_name = 'hybrid_moba_mlstm_hmoe_deepnorm'


def layer_norm(x, g, b):
    xf = x.astype(jnp.float32)
    mu = jnp.mean(xf, axis=-1, keepdims=True)
    var = jnp.mean(jnp.square(xf - mu), axis=-1, keepdims=True)
    return ((xf - mu) * lax.rsqrt(var + LN_EPS) * g + b).astype(x.dtype)


def split_cols(z):
    outs = []
    off = 0
    for s in IN_SPLIT_SIZES:
        outs.append(z[..., off:off + s])
        off += s
    return outs


def apply_rope(x, pos):
    half = x.shape[-1] // 2
    inv_freq = ROPE_THETA ** (-jnp.arange(half, dtype=jnp.float32) / half)
    ang = pos.astype(jnp.float32)[:, None] * inv_freq[None, :]
    cos = jnp.cos(ang)[None, :, None, :]
    sin = jnp.sin(ang)[None, :, None, :]
    xf = x.astype(jnp.float32)
    x1, x2 = xf[..., :half], xf[..., half:]
    return jnp.concatenate([x1 * cos - x2 * sin, x2 * cos + x1 * sin], axis=-1).astype(x.dtype)


def moba_attention(q, k, v):
    B, S, H, hd = q.shape
    nb = -(-S // MOBA_BLOCK)
    sp = nb * MOBA_BLOCK
    G = B * H
    qb_len = MOBA_QUERY_BLOCK

    def to_groups(t):
        t = jnp.transpose(t, (0, 2, 1, 3)).reshape(G, S, hd)
        return jnp.pad(t, ((0, 0), (0, sp - S), (0, 0)))

    qg = to_groups(q) * (hd ** -0.5)
    kg = to_groups(k)
    vg = to_groups(v)
    kb = kg.reshape(G, nb, MOBA_BLOCK, hd)
    vb = vg.reshape(G, nb, MOBA_BLOCK, hd)
    kmean = jnp.mean(kb.astype(jnp.float32), axis=2)
    gate = jnp.einsum('gtd,gnd->gtn', qg.astype(jnp.float32), kmean)
    q_blk = jnp.arange(sp) // MOBA_BLOCK
    fully_past = jnp.arange(nb)[None, :] < q_blk[:, None]
    gate = jnp.where(fully_past[None], gate, -jnp.inf)
    n_sel = min(MOBA_TOPK, nb)
    g_val, g_idx = lax.top_k(gate, n_sel)
    g_valid = jnp.isfinite(g_val)

    nq = sp // qb_len

    def chunks(t):
        return jnp.moveaxis(t.reshape((G, nq, qb_len) + t.shape[2:]), 1, 0)

    xs = (chunks(qg), chunks(g_idx), chunks(g_valid), jnp.arange(nq, dtype=jnp.int32) * qb_len)
    g_ar = jnp.arange(G)[:, None, None]

    def query_block(args):
        qc, ic, mc, start = args
        qpos = start + jnp.arange(qb_len)
        own = (start // MOBA_BLOCK) * MOBA_BLOCK
        k_own = lax.dynamic_slice_in_dim(kg, own, MOBA_BLOCK, axis=1)
        v_own = lax.dynamic_slice_in_dim(vg, own, MOBA_BLOCK, axis=1)
        kpos = own + jnp.arange(MOBA_BLOCK)
        s_own = jnp.einsum('gqd,gkd->gqk', qc, k_own).astype(jnp.float32)
        s_own = jnp.where((kpos[None, :] <= qpos[:, None])[None], s_own, -jnp.inf)
        k_sel = kb[g_ar, ic]
        v_sel = vb[g_ar, ic]
        s_sel = jnp.einsum('gqd,gqnkd->gqnk', qc, k_sel).astype(jnp.float32)
        s_sel = jnp.where(mc[..., None], s_sel, -jnp.inf).reshape(G, qb_len, n_sel * MOBA_BLOCK)
        p = jax.nn.softmax(jnp.concatenate([s_sel, s_own], axis=-1), axis=-1)
        p_sel = p[..., :n_sel * MOBA_BLOCK].reshape(G, qb_len, n_sel, MOBA_BLOCK).astype(vg.dtype)
        p_own = p[..., n_sel * MOBA_BLOCK:].astype(vg.dtype)
        return jnp.einsum('gqnk,gqnkd->gqd', p_sel, v_sel) + jnp.einsum('gqk,gkd->gqd', p_own, v_own)

    out = lax.map(query_block, xs)
    out = jnp.moveaxis(out, 0, 1).reshape(G, sp, hd)[:, :S]
    return out.reshape(B, H, S, hd).transpose(0, 2, 1, 3).reshape(B, S, H * hd)


def causal_depthwise_conv(x, w, b):
    K, C = w.shape
    y = lax.conv_general_dilated(x, w[:, None, :], window_strides=(1,), padding=[(K - 1, 0)],
                                 dimension_numbers=('NWC', 'WIO', 'NWC'), feature_group_count=C)
    return y + b


def mlstm_chunk_step(carry, inp):
    c_state, n_state, m_state = carry
    q, k, v, ig, lf = inp
    L = q.shape[-2]
    causal = jnp.tril(jnp.ones((L, L), dtype=bool))
    b = jnp.cumsum(lf, axis=-1)
    d_log = b[..., :, None] - b[..., None, :] + ig[..., None, :]
    d_log = jnp.where(causal, d_log, -jnp.inf)
    inter_log = b + m_state[..., None]
    m_t = jnp.maximum(inter_log, jnp.max(d_log, axis=-1))
    w_intra = jnp.exp(d_log - m_t[..., None])
    w_inter = jnp.exp(inter_log - m_t)
    s = jnp.einsum('bhtd,bhsd->bhts', q, k) * w_intra
    num = w_inter[..., None] * jnp.einsum('bhtd,bhde->bhte', q, c_state) + jnp.einsum('bhts,bhse->bhte', s, v)
    den = w_inter * jnp.einsum('bhtd,bhd->bht', q, n_state) + jnp.sum(s, axis=-1)
    h = num / jnp.maximum(jnp.abs(den), jnp.exp(-m_t))[..., None]
    b_end = b[..., -1]
    w_log = b_end[..., None] - b + ig
    m_new = jnp.maximum(b_end + m_state, jnp.max(w_log, axis=-1))
    decay = jnp.exp(b_end + m_state - m_new)
    w_state = jnp.exp(w_log - m_new[..., None])
    c_new = decay[..., None, None] * c_state + jnp.einsum('bhs,bhsd,bhse->bhde', w_state, k, v)
    n_new = decay[..., None] * n_state + jnp.einsum('bhs,bhsd->bhd', w_state, k)
    return (c_new, n_new, m_new), h


def mlstm_chunkwise(q, k, v, ig, lf):
    B, H, S, d = q.shape
    nc = S // MLSTM_CHUNK

    def chunks(t):
        return jnp.moveaxis(t.reshape((B, H, nc, MLSTM_CHUNK) + t.shape[3:]), 2, 0)

    init = (jnp.zeros((B, H, d, d), q.dtype), jnp.zeros((B, H, d), q.dtype), jnp.zeros((B, H), q.dtype))
    _, hs = lax.scan(mlstm_chunk_step, init, (chunks(q), chunks(k), chunks(v), chunks(ig), chunks(lf)))
    return jnp.moveaxis(hs, 0, 2).reshape(B, H, S, d)


def mlstm_branch(u, v, o_pre, i_pre, f_pre, conv_w, conv_b, w_mq, w_mk, b_i, b_f, gn_g, skip):
    B, S, _ = u.shape
    f32 = jnp.float32
    u_c = jax.nn.silu(causal_depthwise_conv(u, conv_w, conv_b))
    uh = u_c.reshape(B, S, MLSTM_HEADS, MLSTM_HEAD_DIM)
    q = jnp.einsum('bshd,hde->bhse', uh, w_mq).astype(f32)
    k = (jnp.einsum('bshd,hde->bhse', uh, w_mk) * (MLSTM_HEAD_DIM ** -0.5)).astype(f32)
    vh = v.reshape(B, S, MLSTM_HEADS, MLSTM_HEAD_DIM).transpose(0, 2, 1, 3).astype(f32)
    ig = jnp.transpose((i_pre + b_i).astype(f32), (0, 2, 1))
    lf = jnp.transpose(jax.nn.log_sigmoid((f_pre + b_f).astype(f32)), (0, 2, 1))
    h = mlstm_chunkwise(q, k, vh, ig, lf).transpose(0, 2, 1, 3)
    o = jax.nn.sigmoid(o_pre.astype(f32)).reshape(B, S, MLSTM_HEADS, MLSTM_HEAD_DIM)
    h = o * h
    mu = jnp.mean(h, axis=-1, keepdims=True)
    var = jnp.mean(jnp.square(h - mu), axis=-1, keepdims=True)
    h = ((h - mu) * lax.rsqrt(var + GN_EPS)).reshape(B, S, MLSTM_WIDTH) * gn_g + skip * u_c
    return h.astype(u.dtype)


def hierarchical_moe(x, w_rg, b_rg, w_re, b_re, w_gate, w_up, w_down):
    B, S, D = x.shape
    N = B * S
    xf = x.reshape(N, D)
    g_prob = jax.nn.softmax((xf @ w_rg + b_rg).astype(jnp.float32), axis=-1)
    g_w, g_idx = lax.top_k(g_prob, 1)
    e_logits = (xf @ w_re + b_re).astype(jnp.float32).reshape(N, MOE_GROUPS, MOE_EXPERTS_PER_GROUP)
    e_logits = jnp.take_along_axis(e_logits, g_idx[:, :, None], axis=1)[:, 0]
    e_val, e_idx = lax.top_k(e_logits, MOE_TOPK)
    weights = g_w * jax.nn.softmax(e_val, axis=-1)
    expert = g_idx * MOE_EXPERTS_PER_GROUP + e_idx
    A = N * MOE_TOPK
    eid = expert.reshape(A)
    tok = jnp.repeat(jnp.arange(N, dtype=jnp.int32), MOE_TOPK)
    wt = weights.reshape(A)
    order = jnp.argsort(eid)
    e_s, t_s, w_s = eid[order], tok[order], wt[order]
    counts = jnp.bincount(eid, length=MOE_EXPERTS)
    padded = ((counts + MOE_BLOCK - 1) // MOE_BLOCK) * MOE_BLOCK
    pad_end = jnp.cumsum(padded)
    pad_start = pad_end - padded
    start = jnp.cumsum(counts) - counts
    dest = pad_start[e_s] + jnp.arange(A) - start[e_s]
    nblk = -(-A // MOE_BLOCK) + MOE_EXPERTS
    P = nblk * MOE_BLOCK
    tok_pad = jnp.zeros((P,), jnp.int32).at[dest].set(t_s)
    w_pad = jnp.zeros((P,), w_s.dtype).at[dest].set(w_s)
    blk_expert = jnp.minimum(jnp.searchsorted(pad_end, jnp.arange(nblk) * MOE_BLOCK, side='right'), MOE_EXPERTS - 1)

    def expert_block(args):
        tb, e = args
        xb = xf[tb]
        hb = jax.nn.silu(xb @ w_gate[e]) * (xb @ w_up[e])
        return hb @ w_down[e]

    yb = lax.map(expert_block, (tok_pad.reshape(nblk, MOE_BLOCK), blk_expert))
    y = jnp.zeros((N, D), yb.dtype).at[tok_pad].add(yb.reshape(P, D) * w_pad[:, None].astype(yb.dtype))
    return y.reshape(B, S, D)


def setup_inputs(seed: int = 0) -> dict:
    key = jax.random.key(seed)
    ks = jax.random.split(key, 26)
    f32 = jnp.float32
    L = DEPTH

    def nrm(k, shape, scale):
        return jax.random.normal(k, shape, f32) * scale

    return {
        'x': nrm(ks[0], (BATCH, SEQ, D_MODEL), 1.0),
        'ln0_g': 1.0 + nrm(ks[1], (D_MODEL,), 0.01),
        'ln0_b': nrm(ks[2], (D_MODEL,), 0.01),
        'w_in': nrm(ks[3], (L, D_MODEL, IN_WIDTH), D_MODEL ** -0.5),
        'conv_w': nrm(ks[4], (L, MLSTM_CONV, MLSTM_WIDTH), MLSTM_CONV ** -0.5),
        'conv_b': nrm(ks[5], (L, MLSTM_WIDTH), 0.01),
        'w_mq': nrm(ks[6], (L, MLSTM_HEADS, MLSTM_HEAD_DIM, MLSTM_HEAD_DIM), MLSTM_HEAD_DIM ** -0.5),
        'w_mk': nrm(ks[7], (L, MLSTM_HEADS, MLSTM_HEAD_DIM, MLSTM_HEAD_DIM), MLSTM_HEAD_DIM ** -0.5),
        'b_i': nrm(ks[8], (L, MLSTM_HEADS), 0.1),
        'b_f': jnp.linspace(3.0, 6.0, MLSTM_HEADS, dtype=f32)[None, :] + nrm(ks[9], (L, MLSTM_HEADS), 0.01),
        'gn_g': 1.0 + nrm(ks[10], (L, MLSTM_WIDTH), 0.01),
        'skip': 1.0 + nrm(ks[11], (L, MLSTM_WIDTH), 0.01),
        'w_attn_up': nrm(ks[12], (L, ATTN_WIDTH, D_MODEL), ATTN_WIDTH ** -0.5),
        'w_mlstm_up': nrm(ks[13], (L, MLSTM_WIDTH, D_MODEL), MLSTM_WIDTH ** -0.5),
        'w_out': nrm(ks[14], (L, D_MODEL, D_MODEL), DEEPNORM_BETA * D_MODEL ** -0.5),
        'ln1_g': 1.0 + nrm(ks[15], (L, D_MODEL), 0.01),
        'ln1_b': nrm(ks[16], (L, D_MODEL), 0.01),
        'w_router_group': nrm(ks[17], (L, D_MODEL, MOE_GROUPS), D_MODEL ** -0.5),
        'b_router_group': nrm(ks[18], (L, MOE_GROUPS), 0.01),
        'w_router_expert': nrm(ks[19], (L, D_MODEL, MOE_EXPERTS), D_MODEL ** -0.5),
        'b_router_expert': nrm(ks[20], (L, MOE_EXPERTS), 0.01),
        'w_gate': nrm(ks[21], (L, MOE_EXPERTS, D_MODEL, MOE_D_FF), D_MODEL ** -0.5),
        'w_up': nrm(ks[22], (L, MOE_EXPERTS, D_MODEL, MOE_D_FF), D_MODEL ** -0.5),
        'w_down': nrm(ks[23], (L, MOE_EXPERTS, MOE_D_FF, D_MODEL), DEEPNORM_BETA * MOE_D_FF ** -0.5),
        'ln2_g': 1.0 + nrm(ks[24], (L, D_MODEL), 0.01),
        'ln2_b': nrm(ks[25], (L, D_MODEL), 0.01),
    }


def reference(x, ln0_g, ln0_b, w_in, conv_w, conv_b, w_mq, w_mk, b_i, b_f, gn_g, skip,
              w_attn_up, w_mlstm_up, w_out, ln1_g, ln1_b, w_router_group, b_router_group,
              w_router_expert, b_router_expert, w_gate, w_up, w_down, ln2_g, ln2_b):
    B, S, _ = x.shape
    pos = jnp.arange(S, dtype=jnp.int32)
    x = layer_norm(x, ln0_g, ln0_b)
    for l in range(DEPTH):
        z = x @ w_in[l]
        q_a, k_a, v_a, u_m, v_m, o_m, i_m, f_m, g_a, g_m = split_cols(z)
        q_a = apply_rope(q_a.reshape(B, S, ATTN_HEADS, ATTN_HEAD_DIM), pos)
        k_a = apply_rope(k_a.reshape(B, S, ATTN_HEADS, ATTN_HEAD_DIM), pos)
        v_a = v_a.reshape(B, S, ATTN_HEADS, ATTN_HEAD_DIM)
        y_a = moba_attention(q_a, k_a, v_a)
        y_m = mlstm_branch(u_m, v_m, o_m, i_m, f_m, conv_w[l], conv_b[l], w_mq[l], w_mk[l],
                           b_i[l], b_f[l], gn_g[l], skip[l])
        mix = jax.nn.sigmoid(g_a) * (y_a @ w_attn_up[l]) + jax.nn.sigmoid(g_m) * (y_m @ w_mlstm_up[l])
        x = layer_norm(DEEPNORM_ALPHA * x + mix @ w_out[l], ln1_g[l], ln1_b[l])
        ffn = hierarchical_moe(x, w_router_group[l], b_router_group[l], w_router_expert[l],
                               b_router_expert[l], w_gate[l], w_up[l], w_down[l])
        x = layer_norm(DEEPNORM_ALPHA * x + ffn, ln2_g[l], ln2_b[l])
    return x
```

```python
import functools
import math

import jax
import jax.numpy as jnp
from jax import lax
from jax.experimental import pallas as pl
from jax.experimental.pallas import tpu as pltpu

F32 = jnp.float32
BF16 = jnp.bfloat16

ATTN_HEADS = 8
ATTN_HEAD_DIM = 64
ATTN_WIDTH = ATTN_HEADS * ATTN_HEAD_DIM
MOBA_BLOCK = 256
MOBA_TOPK = 3
ROPE_THETA = 10000.0
MLSTM_HEADS = 4
MLSTM_HEAD_DIM = 128
MLSTM_WIDTH = MLSTM_HEADS * MLSTM_HEAD_DIM
MLSTM_CONV = 4
MOE_GROUPS = 8
MOE_EXPERTS_PER_GROUP = 8
MOE_EXPERTS = MOE_GROUPS * MOE_EXPERTS_PER_GROUP
MOE_D_FF = 512
LN_EPS = 1e-5
GN_EPS = 1e-6
DEPTH = 1
DEEPNORM_ALPHA = (2 * DEPTH) ** 0.25

LANES = 128
SUBLANES = 8
ROW_TILE = 256
EXPERT_TILE = 256
VMEM_LIMIT = 48 * 1024 * 1024

NEG_INF = float("-inf")


def _params(*sem):
    return pltpu.CompilerParams(dimension_semantics=sem, vmem_limit_bytes=VMEM_LIMIT)


def _dot(a, b):
    return jnp.dot(a, b, preferred_element_type=F32)


def _dot_nt(a, b):
    return lax.dot_general(a, b, (((1,), (1,)), ((), ())), preferred_element_type=F32)


def _dot_tn(a, b):
    return lax.dot_general(a, b, (((0,), (0,)), ((), ())), preferred_element_type=F32)


def _split3(x):
    x1 = x.astype(BF16)
    r1 = x - x1.astype(F32)
    x2 = r1.astype(BF16)
    r2 = r1 - x2.astype(F32)
    return x1, x2, r2.astype(BF16)


def _layer_norm(x, g, b):
    mu = jnp.mean(x, axis=-1, keepdims=True)
    xc = x - mu
    var = jnp.mean(xc * xc, axis=-1, keepdims=True)
    return xc * lax.rsqrt(var + LN_EPS) * g + b


def _log_sigmoid(x):
    return jnp.minimum(x, 0.0) - jnp.log1p(jnp.exp(-jnp.abs(x)))


def _full(shape):
    nd = len(shape)
    return pl.BlockSpec(shape, lambda *_: (0,) * nd)


def _inproj_kernel(x_ref, g_ref, b_ref, wqkv_ref, wuvo_ref, wif_ref, wift_ref, wg_ref, cos_ref, sin_ref,
                   q_ref, k_ref, v_ref, km_ref, u_ref, vm_ref, o_ref, ifc_ref, ift_ref, ga_ref, gm_ref):
    xn = _layer_norm(x_ref[...], g_ref[...], b_ref[...])
    xb = xn.astype(BF16)

    cos = cos_ref[...]
    sin = sin_ref[...]
    lane = lax.broadcasted_iota(jnp.int32, cos.shape, 1)
    first_half = (lane % ATTN_HEAD_DIM) < (ATTN_HEAD_DIM // 2)

    def rope(t):
        fwd = pltpu.roll(t, ATTN_WIDTH - ATTN_HEAD_DIM // 2, axis=1)
        bwd = pltpu.roll(t, ATTN_HEAD_DIM // 2, axis=1)
        return t * cos + jnp.where(first_half, fwd, bwd) * sin

    zqkv = _dot(xb, wqkv_ref[...])
    q = rope(zqkv[:, :ATTN_WIDTH]) * (ATTN_HEAD_DIM ** -0.5)
    k = rope(zqkv[:, ATTN_WIDTH:2 * ATTN_WIDTH])
    v = zqkv[:, 2 * ATTN_WIDTH:]
    km_ref[0] = jnp.mean(k, axis=0, keepdims=True)
    for h in range(ATTN_HEADS):
        sl = slice(h * ATTN_HEAD_DIM, (h + 1) * ATTN_HEAD_DIM)
        q_ref[0, h] = q[:, sl].astype(BF16)
        k_ref[0, h] = k[:, sl].astype(BF16)
        v_ref[0, h] = v[:, sl].astype(BF16)

    zuvo = _dot(xb, wuvo_ref[...])
    u_ref[...] = zuvo[:, :MLSTM_WIDTH]
    vm_ref[...] = zuvo[:, MLSTM_WIDTH:2 * MLSTM_WIDTH].astype(BF16)
    o_ref[...] = zuvo[:, 2 * MLSTM_WIDTH:]

    ifc_ref[...] = _dot(xb, wif_ref[...])
    ift_ref[...] = _dot_nt(wift_ref[...], xb)

    zg = _dot(xb, wg_ref[...])
    d = ga_ref.shape[1]
    ga_ref[...] = zg[:, :d]
    gm_ref[...] = zg[:, d:]


def _inproj(x2, ln_g, ln_b, wqkv, wuvo, wif, wift, wg, cos, sin, batch, seq):
    n, d = x2.shape
    tm = ROW_TILE
    nsb = seq // tm
    hd = ATTN_HEAD_DIM
    row = lambda w: pl.BlockSpec((tm, w), lambda i: (i, 0))
    head = pl.BlockSpec((1, ATTN_HEADS, tm, hd), lambda i: (i // nsb, 0, i % nsb, 0))
    tab = pl.BlockSpec((tm, ATTN_WIDTH), lambda i: (i % nsb, 0))
    head_shape = jax.ShapeDtypeStruct((batch, ATTN_HEADS, seq, hd), BF16)
    out_shape = (
        head_shape, head_shape, head_shape,
        jax.ShapeDtypeStruct((n // tm, 1, ATTN_WIDTH), F32),
        jax.ShapeDtypeStruct((n, MLSTM_WIDTH), F32),
        jax.ShapeDtypeStruct((n, MLSTM_WIDTH), BF16),
        jax.ShapeDtypeStruct((n, MLSTM_WIDTH), F32),
        jax.ShapeDtypeStruct((n, LANES), F32),
        jax.ShapeDtypeStruct((SUBLANES, n), F32),
        jax.ShapeDtypeStruct((n, d), F32),
        jax.ShapeDtypeStruct((n, d), F32),
    )
    out_specs = (
        head, head, head,
        pl.BlockSpec((1, 1, ATTN_WIDTH), lambda i: (i, 0, 0)),
        row(MLSTM_WIDTH), row(MLSTM_WIDTH), row(MLSTM_WIDTH),
        row(LANES),
        pl.BlockSpec((SUBLANES, tm), lambda i: (0, i)),
        row(d), row(d),
    )
    in_specs = [row(d), _full(ln_g.shape), _full(ln_b.shape), _full(wqkv.shape), _full(wuvo.shape),
                _full(wif.shape), _full(wift.shape), _full(wg.shape), tab, tab]
    return pl.pallas_call(
        _inproj_kernel, grid=(n // tm,), in_specs=in_specs, out_specs=out_specs, out_shape=out_shape,
        compiler_params=_params("parallel"), name="inproj",
    )(x2, ln_g, ln_b, wqkv, wuvo, wif, wift, wg, cos, sin)


def _moba_kernel(q_ref, k_ref, v_ref, kmt_ref, o_ref, bias_ref):
    i = pl.program_id(1)
    blk = MOBA_BLOCK
    nb = k_ref.shape[2] // blk
    lane = lax.broadcasted_iota(jnp.int32, (blk, LANES), 1)
    rows = lax.broadcasted_iota(jnp.int32, (blk, blk), 0)
    cols = lax.broadcasted_iota(jnp.int32, (blk, blk), 1)
    causal = cols <= rows

    outs = []
    for h in range(ATTN_HEADS):
        q = q_ref[0, h]
        km = kmt_ref[0, h]
        km_hi = km.astype(BF16)
        km_lo = (km - km_hi.astype(F32)).astype(BF16)
        gate = _dot(q, km_hi) + _dot(q, km_lo)
        gate = jnp.where(lane < i, gate, NEG_INF)
        for j in range(nb - 1):
            col = gate[:, j:j + 1]
            beats = (gate > col) | ((gate == col) & (lane < j))
            cnt = jnp.sum(jnp.where(beats, 1.0, 0.0), axis=1, keepdims=True)
            sel = (cnt < float(MOBA_TOPK)) & (col > NEG_INF)
            bias_ref[j] = jnp.broadcast_to(jnp.where(sel, 0.0, NEG_INF), (blk, LANES))

        own = pl.multiple_of(i * blk, blk)
        s = _dot_nt(q, k_ref[0, h, pl.ds(own, blk), :])
        s = jnp.where(causal, s, NEG_INF)
        m0 = jnp.max(s, axis=1, keepdims=True)
        p = jnp.exp(s - m0)
        l0 = jnp.sum(p, axis=1, keepdims=True)
        acc0 = _dot(p.astype(BF16), v_ref[0, h, pl.ds(own, blk), :])

        def body(j, carry, q=q, h=h):
            m, l, acc = carry
            off = pl.multiple_of(j * blk, blk)
            bias = bias_ref[j]
            s = _dot_nt(q, k_ref[0, h, pl.ds(off, blk), :]) + jnp.concatenate([bias] * (blk // LANES), axis=1)
            m_new = jnp.maximum(m, jnp.max(s, axis=1, keepdims=True))
            a = jnp.exp(m - m_new)
            p = jnp.exp(s - m_new)
            l = a * l + jnp.sum(p, axis=1, keepdims=True)
            acc = a * acc + _dot(p.astype(BF16), v_ref[0, h, pl.ds(off, blk), :])
            return m_new, l, acc

        _, l, acc = lax.fori_loop(0, i, body, (m0, l0, acc0))
        outs.append(acc / l)
    o_ref[0] = jnp.concatenate(outs, axis=1).astype(BF16)


def _moba(q, k, v, kmt):
    batch, heads, seq, hd = q.shape
    blk = MOBA_BLOCK
    nb = seq // blk
    return pl.pallas_call(
        _moba_kernel, grid=(batch, nb),
        in_specs=[
            pl.BlockSpec((1, heads, blk, hd), lambda b, i: (b, 0, i, 0)),
            pl.BlockSpec((1, heads, seq, hd), lambda b, i: (b, 0, 0, 0)),
            pl.BlockSpec((1, heads, seq, hd), lambda b, i: (b, 0, 0, 0)),
            pl.BlockSpec((1, heads, hd, LANES), lambda b, i: (b, 0, 0, 0)),
        ],
        out_specs=pl.BlockSpec((1, blk, heads * hd), lambda b, i: (b, i, 0)),
        out_shape=jax.ShapeDtypeStruct((batch, seq, heads * hd), BF16),
        scratch_shapes=[pltpu.VMEM((nb - 1, blk, LANES), F32)],
        compiler_params=_params("parallel", "arbitrary"), name="moba",
    )(q, k, v, kmt)


def _mlstm_kernel(u_ref, vm_ref, o_ref, ifc_ref, ift_ref, cw_ref, cb_ref, wq_ref, wk_ref, brow_ref, bcol_ref,
                  gn_ref, skip_ref, y_ref, ext_ref, c_ref, n_ref, m_ref):
    tm = u_ref.shape[0]
    hd = MLSTM_HEAD_DIM
    halo = SUBLANES

    @pl.when(pl.program_id(1) == 0)
    def _():
        ext_ref[0:halo, :] = jnp.zeros((halo, MLSTM_WIDTH), F32)
        c_ref[...] = jnp.zeros_like(c_ref)
        n_ref[...] = jnp.zeros_like(n_ref)
        m_ref[...] = jnp.zeros_like(m_ref)

    u = u_ref[...]
    ext_ref[halo:halo + tm, :] = u
    acc = jnp.broadcast_to(cb_ref[...], u.shape)
    for j in range(MLSTM_CONV):
        acc = acc + cw_ref[j:j + 1, :] * ext_ref[halo - (MLSTM_CONV - 1) + j:halo - (MLSTM_CONV - 1) + j + tm, :]
    ext_ref[0:halo, :] = u[tm - halo:, :]
    uc = acc * jax.nn.sigmoid(acc)

    gc = ifc_ref[...] + brow_ref[...]
    gr = ift_ref[...] + bcol_ref[...]
    rows = lax.broadcasted_iota(jnp.int32, (tm, tm), 0)
    cols = lax.broadcasted_iota(jnp.int32, (tm, tm), 1)
    causal = cols <= rows
    tril = jnp.where(causal, 1.0, 0.0).astype(BF16)
    triu = jnp.where(rows <= cols, 1.0, 0.0).astype(BF16)
    c1, c2, c3 = _split3(_log_sigmoid(gc))
    bcum_c = _dot(tril, c1) + _dot(tril, c2) + _dot(tril, c3)
    r1, r2, r3 = _split3(_log_sigmoid(gr))
    bcum_r = _dot(r1, triu) + _dot(r2, triu) + _dot(r3, triu)

    for h in range(MLSTM_HEADS):
        hs = slice(h * hd, (h + 1) * hd)
        fl = MLSTM_HEADS + h
        bt = bcum_c[:, fl:fl + 1]
        ig_c = gc[:, h:h + 1]
        row_t = gr[h:h + 1, :] - bcum_r[fl:fl + 1, :]
        m_prev = m_ref[h][:, 0:1]
        dlog = jnp.where(causal, bt + row_t, NEG_INF)
        inter = bt + m_prev
        m_t = jnp.maximum(inter, jnp.max(dlog, axis=1, keepdims=True))
        w_intra = jnp.exp(dlog - m_t)
        w_inter = jnp.exp(inter - m_t)

        ucb = uc[:, hs].astype(BF16)
        q = _dot(ucb, wq_ref[h])
        k = _dot(ucb, wk_ref[h]) * (hd ** -0.5)
        qb = q.astype(BF16)
        kb = k.astype(BF16)
        vb = vm_ref[:, hs]
        s = _dot_nt(qb, kb) * w_intra
        c_prev = c_ref[h]
        n_prev = n_ref[h]
        num = w_inter * _dot(qb, c_prev.astype(BF16)) + _dot(s.astype(BF16), vb)
        den = w_inter * jnp.sum(q * n_prev, axis=1, keepdims=True) + jnp.sum(s, axis=1, keepdims=True)
        hh = num / jnp.maximum(jnp.abs(den), jnp.exp(-m_t))

        b_end = bt[tm - 1:tm, :]
        w_log = b_end - bt + ig_c
        m_new = jnp.maximum(b_end + m_prev, jnp.max(w_log, axis=0, keepdims=True))
        decay = jnp.exp(b_end + m_prev - m_new)
        kw = k * jnp.exp(w_log - m_new)
        c_ref[h] = decay * c_prev + _dot_tn(kw.astype(BF16), vb)
        n_ref[h] = decay * n_prev + jnp.sum(kw, axis=0, keepdims=True)
        m_ref[h] = jnp.broadcast_to(m_new, (1, LANES))

        hh = jax.nn.sigmoid(o_ref[:, hs]) * hh
        mu = jnp.mean(hh, axis=1, keepdims=True)
        hc = hh - mu
        var = jnp.mean(hc * hc, axis=1, keepdims=True)
        y = hc * lax.rsqrt(var + GN_EPS) * gn_ref[:, hs] + skip_ref[:, hs] * uc[:, hs]
        y_ref[:, hs] = y.astype(BF16)


def _mlstm(u, vm, o, ifc, ift, conv_w, conv_b, wq, wk, brow, bcol, gn_g, skip, batch, seq):
    n = u.shape[0]
    tm = ROW_TILE
    nc = seq // tm
    row = lambda w: pl.BlockSpec((tm, w), lambda b, c: (b * nc + c, 0))
    in_specs = [row(MLSTM_WIDTH), row(MLSTM_WIDTH), row(MLSTM_WIDTH), row(LANES),
                pl.BlockSpec((SUBLANES, tm), lambda b, c: (0, b * nc + c)),
                _full(conv_w.shape), _full(conv_b.shape), _full(wq.shape), _full(wk.shape),
                _full(brow.shape), _full(bcol.shape), _full(gn_g.shape), _full(skip.shape)]
    return pl.pallas_call(
        _mlstm_kernel, grid=(batch, nc), in_specs=in_specs, out_specs=row(MLSTM_WIDTH),
        out_shape=jax.ShapeDtypeStruct((n, MLSTM_WIDTH), BF16),
        scratch_shapes=[pltpu.VMEM((SUBLANES + tm, MLSTM_WIDTH), F32),
                        pltpu.VMEM((MLSTM_HEADS, MLSTM_HEAD_DIM, MLSTM_HEAD_DIM), F32),
                        pltpu.VMEM((MLSTM_HEADS, 1, MLSTM_HEAD_DIM), F32),
                        pltpu.VMEM((MLSTM_HEADS, 1, LANES), F32)],
        compiler_params=_params("parallel", "arbitrary"), name="mlstm",
    )(u, vm, o, ifc, ift, conv_w, conv_b, wq, wk, brow, bcol, gn_g, skip)


def _mix_kernel(x_ref, g0_ref, b0_ref, ya_ref, ym_ref, ga_ref, gm_ref, wau_ref, wmu_ref, wout_ref,
                g1_ref, b1_ref, wrh_ref, wrl_ref, br_ref,
                x1_ref, ri_ref, rw_ref, cnt_out_ref, cnt_ref):
    tm = x_ref.shape[0]

    @pl.when(pl.program_id(0) == 0)
    def _():
        cnt_ref[...] = jnp.zeros_like(cnt_ref)

    xn = _layer_norm(x_ref[...], g0_ref[...], b0_ref[...])
    a_up = _dot(ya_ref[...], wau_ref[...])
    m_up = _dot(ym_ref[...], wmu_ref[...])
    mix = jax.nn.sigmoid(ga_ref[...]) * a_up + jax.nn.sigmoid(gm_ref[...]) * m_up
    x1 = _layer_norm(DEEPNORM_ALPHA * xn + _dot(mix.astype(BF16), wout_ref[...]), g1_ref[...], b1_ref[...])
    x1_ref[...] = x1

    x_hi = x1.astype(BF16)
    x_lo = (x1 - x_hi.astype(F32)).astype(BF16)
    w_hi = wrh_ref[...]
    logits = _dot(x_hi, w_hi) + _dot(x_lo, w_hi) + _dot(x_hi, wrl_ref[...]) + br_ref[...]
    lane = lax.broadcasted_iota(jnp.int32, (tm, LANES), 1).astype(F32)
    big = float(4 * LANES)
    is_g = (lane >= float(MOE_EXPERTS)) & (lane < float(MOE_EXPERTS + MOE_GROUPS))
    gl = jnp.where(is_g, logits, NEG_INF)
    ge = jnp.exp(gl - jnp.max(gl, axis=1, keepdims=True))
    gp = ge / jnp.sum(ge, axis=1, keepdims=True)
    g_w = jnp.max(gp, axis=1, keepdims=True)
    g_idx = jnp.min(jnp.where((gp == g_w) & is_g, lane - float(MOE_EXPERTS), big), axis=1, keepdims=True)
    lo = g_idx * float(MOE_EXPERTS_PER_GROUP)
    in_grp = (lane >= lo) & (lane < lo + float(MOE_EXPERTS_PER_GROUP))
    el = jnp.where(in_grp, logits, NEG_INF)
    v1 = jnp.max(el, axis=1, keepdims=True)
    i1 = jnp.min(jnp.where((el == v1) & in_grp, lane, big), axis=1, keepdims=True)
    el2 = jnp.where(lane == i1, NEG_INF, el)
    v2 = jnp.max(el2, axis=1, keepdims=True)
    i2 = jnp.min(jnp.where((el2 == v2) & in_grp & (lane != i1), lane, big), axis=1, keepdims=True)
    e2 = jnp.exp(v2 - v1)
    w0 = g_w / (1.0 + e2)
    w1 = g_w * e2 / (1.0 + e2)

    is1 = lane == i1
    is2 = lane == i2
    onehot = jnp.where(is1 | is2, 1.0, 0.0)
    rows = lax.broadcasted_iota(jnp.int32, (tm, tm), 0)
    cols = lax.broadcasted_iota(jnp.int32, (tm, tm), 1)
    strict = jnp.where(cols < rows, 1.0, 0.0).astype(BF16)
    before = _dot(strict, onehot.astype(BF16)) + cnt_ref[...]
    r0 = jnp.sum(jnp.where(is1, before, 0.0), axis=1, keepdims=True)
    r1 = jnp.sum(jnp.where(is2, before, 0.0), axis=1, keepdims=True)
    total = cnt_ref[...] + jnp.sum(onehot, axis=0, keepdims=True)
    cnt_ref[...] = total
    cnt_out_ref[...] = total

    ri = jnp.where(lane == 0.0, i1, jnp.where(lane == 1.0, i2, jnp.where(lane == 2.0, r0, jnp.where(lane == 3.0, r1, 0.0))))
    ri_ref[...] = ri.astype(jnp.int32)
    rw_ref[...] = jnp.where(lane == 0.0, w0, jnp.where(lane == 1.0, w1, 0.0))


def _mix(x2, g0, b0, ya, ym, ga, gm, wau, wmu, wout, g1, b1, wrh, wrl, br):
    n, d = x2.shape
    tm = ROW_TILE
    row = lambda w: pl.BlockSpec((tm, w), lambda i: (i, 0))
    in_specs = [row(d), _full(g0.shape), _full(b0.shape), row(ATTN_WIDTH), row(MLSTM_WIDTH), row(d), row(d),
                _full(wau.shape), _full(wmu.shape), _full(wout.shape), _full(g1.shape), _full(b1.shape),
                _full(wrh.shape), _full(wrl.shape), _full(br.shape)]
    out_shape = (jax.ShapeDtypeStruct((n, d), F32), jax.ShapeDtypeStruct((n, LANES), jnp.int32),
                 jax.ShapeDtypeStruct((n, LANES), F32), jax.ShapeDtypeStruct((1, LANES), F32))
    out_specs = (row(d), row(LANES), row(LANES), _full((1, LANES)))
    return pl.pallas_call(
        _mix_kernel, grid=(n // tm,), in_specs=in_specs, out_specs=out_specs, out_shape=out_shape,
        scratch_shapes=[pltpu.VMEM((1, LANES), F32)],
        compiler_params=_params("arbitrary"), name="mix",
    )(x2, g0, b0, ya, ym, ga, gm, wau, wmu, wout, g1, b1, wrh, wrl, br)


def _row_copy(src, src_row, dst, dst_row, sem):
    return pltpu.make_async_copy(src.at[pl.ds(src_row, 1), :], dst.at[pl.ds(dst_row, 1), :], sem)


def _dispatch_kernel(dest_ref, x_ref, xs_in_ref, xs_ref, sem):
    del xs_in_ref
    tm = x_ref.shape[0]

    def start(r, _):
        _row_copy(x_ref, r, xs_ref, dest_ref[2 * r], sem).start()
        _row_copy(x_ref, r, xs_ref, dest_ref[2 * r + 1], sem).start()
        return 0

    def wait(r, _):
        _row_copy(x_ref, r, xs_ref, dest_ref[2 * r], sem).wait()
        _row_copy(x_ref, r, xs_ref, dest_ref[2 * r + 1], sem).wait()
        return 0

    lax.fori_loop(0, tm, start, 0)
    lax.fori_loop(0, tm, wait, 0)


def _dispatch(dest, x1, xs_init):
    n, d = x1.shape
    tm = ROW_TILE
    return pl.pallas_call(
        _dispatch_kernel, grid=(n // tm,),
        in_specs=[pl.BlockSpec((2 * tm,), lambda i: (i,), memory_space=pltpu.SMEM),
                  pl.BlockSpec((tm, d), lambda i: (i, 0)),
                  pl.BlockSpec(memory_space=pl.ANY)],
        out_specs=pl.BlockSpec(memory_space=pl.ANY),
        out_shape=jax.ShapeDtypeStruct(xs_init.shape, xs_init.dtype),
        scratch_shapes=[pltpu.SemaphoreType.DMA(())],
        input_output_aliases={2: 0},
        compiler_params=_params("arbitrary"), name="dispatch",
    )(dest, x1, xs_init)


def _expert_kernel(be_ref, nused_ref, xs_ref, wg_ref, wu_ref, wd_ref, ys_ref):
    del be_ref
    used = pl.program_id(0) < nused_ref[0]

    @pl.when(used)
    def _():
        xb = xs_ref[...].astype(BF16)
        g = _dot(xb, wg_ref[0].astype(BF16))
        u = _dot(xb, wu_ref[0].astype(BF16))
        hmid = g * jax.nn.sigmoid(g) * u
        ys_ref[...] = _dot(hmid.astype(BF16), wd_ref[0].astype(BF16))

    @pl.when(jnp.logical_not(used))
    def _():
        ys_ref[...] = jnp.zeros_like(ys_ref)


def _experts(blk_expert, nused, xs, w_gate, w_up, w_down):
    p, d = xs.shape
    tb = EXPERT_TILE
    nblk = p // tb
    dff = w_gate.shape[2]
    blk_row = lambda i, be, nu: (jnp.minimum(i, nu[0] - 1), 0)
    grid_spec = pltpu.PrefetchScalarGridSpec(
        num_scalar_prefetch=2, grid=(nblk,),
        in_specs=[pl.BlockSpec((tb, d), blk_row),
                  pl.BlockSpec((1, d, dff), lambda i, be, nu: (be[i], 0, 0)),
                  pl.BlockSpec((1, d, dff), lambda i, be, nu: (be[i], 0, 0)),
                  pl.BlockSpec((1, dff, d), lambda i, be, nu: (be[i], 0, 0))],
        out_specs=pl.BlockSpec((tb, d), lambda i, be, nu: (i, 0)),
    )
    return pl.pallas_call(
        _expert_kernel, grid_spec=grid_spec, out_shape=jax.ShapeDtypeStruct((p, d), F32),
        compiler_params=_params("arbitrary"), name="experts",
    )(blk_expert, nused, xs, w_gate, w_up, w_down)


def _combine_kernel(dest_ref, x1_ref, rw_ref, g_ref, b_ref, ys_ref, o_ref, buf_ref, sem):
    tm = x1_ref.shape[0]

    def start(r, _):
        _row_copy(ys_ref, dest_ref[2 * r], buf_ref.at[0], r, sem).start()
        _row_copy(ys_ref, dest_ref[2 * r + 1], buf_ref.at[1], r, sem).start()
        return 0

    def wait(r, _):
        _row_copy(ys_ref, dest_ref[2 * r], buf_ref.at[0], r, sem).wait()
        _row_copy(ys_ref, dest_ref[2 * r + 1], buf_ref.at[1], r, sem).wait()
        return 0

    lax.fori_loop(0, tm, start, 0)
    lax.fori_loop(0, tm, wait, 0)
    rw = rw_ref[...]
    ffn = rw[:, 0:1] * buf_ref[0] + rw[:, 1:2] * buf_ref[1]
    o_ref[...] = _layer_norm(DEEPNORM_ALPHA * x1_ref[...] + ffn, g_ref[...], b_ref[...])


def _combine(dest, x1, rw, ln_g, ln_b, ys):
    n, d = x1.shape
    tm = ROW_TILE
    row = lambda w: pl.BlockSpec((tm, w), lambda i: (i, 0))
    return pl.pallas_call(
        _combine_kernel, grid=(n // tm,),
        in_specs=[pl.BlockSpec((2 * tm,), lambda i: (i,), memory_space=pltpu.SMEM),
                  row(d), row(LANES), _full(ln_g.shape), _full(ln_b.shape),
                  pl.BlockSpec(memory_space=pl.ANY)],
        out_specs=row(d),
        out_shape=jax.ShapeDtypeStruct((n, d), F32),
        scratch_shapes=[pltpu.VMEM((2, tm, d), F32), pltpu.SemaphoreType.DMA(())],
        compiler_params=_params("arbitrary"), name="combine",
    )(dest, x1, rw, ln_g, ln_b, ys)


def _rope_tables(seq):
    half = ATTN_HEAD_DIM // 2
    inv_freq = ROPE_THETA ** (-jnp.arange(half, dtype=F32) / half)
    ang = jnp.arange(seq, dtype=F32)[:, None] * inv_freq[None, :]
    cos = jnp.cos(ang)
    sin = jnp.sin(ang)
    cos_h = jnp.concatenate([cos, cos], axis=1)
    sin_h = jnp.concatenate([-sin, sin], axis=1)
    return jnp.tile(cos_h, (1, ATTN_HEADS)), jnp.tile(sin_h, (1, ATTN_HEADS))


def _pad_lanes(a, width=LANES):
    return jnp.pad(a, ((0, 0), (0, width - a.shape[1])))


def kernel(x, ln0_g, ln0_b, w_in, conv_w, conv_b, w_mq, w_mk, b_i, b_f, gn_g, skip, w_attn_up, w_mlstm_up, w_out,
           ln1_g, ln1_b, w_router_group, b_router_group, w_router_expert, b_router_expert, w_gate, w_up, w_down,
           ln2_g, ln2_b):
    batch, seq, d = x.shape
    n = batch * seq
    assert seq % ROW_TILE == 0 and ROW_TILE == MOBA_BLOCK and w_in.shape[0] == DEPTH
    x2 = x.reshape(n, d)
    vec = lambda a: a.reshape(1, -1).astype(F32)

    w = w_in[0]
    c_if = 3 * ATTN_WIDTH + 3 * MLSTM_WIDTH
    c_g = c_if + 2 * MLSTM_HEADS
    wqkv = w[:, :3 * ATTN_WIDTH].astype(BF16)
    wuvo = w[:, 3 * ATTN_WIDTH:c_if].astype(BF16)
    w_if = w[:, c_if:c_g]
    wif = _pad_lanes(w_if).astype(BF16)
    wift = w_if.T.astype(BF16)
    wg = w[:, c_g:].astype(BF16)
    cos, sin = _rope_tables(seq)

    q, k, v, kmean, u, vm, o, ifc, ift, ga, gm = _inproj(
        x2, vec(ln0_g), vec(ln0_b), wqkv, wuvo, wif, wift, wg, cos, sin, batch, seq)

    nb = seq // MOBA_BLOCK
    kmt = kmean.reshape(batch, nb, ATTN_HEADS, ATTN_HEAD_DIM).transpose(0, 2, 3, 1)
    kmt = jnp.pad(kmt, ((0, 0), (0, 0), (0, 0), (0, LANES - nb)))
    ya = _moba(q, k, v, kmt).reshape(n, ATTN_WIDTH)

    b_if = jnp.concatenate([b_i[0], b_f[0]]).astype(F32)
    ym = _mlstm(u, vm, o, ifc, ift, conv_w[0], vec(conv_b[0]), w_mq[0].astype(BF16), w_mk[0].astype(BF16),
                _pad_lanes(b_if[None, :]), b_if[:, None], vec(gn_g[0]), vec(skip[0]), batch, seq)

    w_r = _pad_lanes(jnp.concatenate([w_router_expert[0], w_router_group[0]], axis=1))
    w_r_hi = w_r.astype(BF16)
    w_r_lo = (w_r - w_r_hi.astype(F32)).astype(BF16)
    b_r = _pad_lanes(jnp.concatenate([b_router_expert[0], b_router_group[0]])[None, :])
    x1, ri, rw, counts = _mix(
        x2, vec(ln0_g), vec(ln0_b), ya, ym, ga, gm, w_attn_up[0].astype(BF16), w_mlstm_up[0].astype(BF16),
        w_out[0].astype(BF16), vec(ln1_g[0]), vec(ln1_b[0]), w_r_hi, w_r_lo, b_r)

    tb = EXPERT_TILE
    nblk = (2 * n) // tb + MOE_EXPERTS
    cnt = counts[0, :MOE_EXPERTS].astype(jnp.int32)
    nblk_e = (cnt + tb - 1) // tb
    blk_end = jnp.cumsum(nblk_e)
    pad_start = (blk_end - nblk_e) * tb
    nused = blk_end[-1:]
    blk_ids = jnp.minimum(jnp.arange(nblk, dtype=jnp.int32), nused[0] - 1)
    blk_expert = jnp.sum((blk_ids[:, None] >= blk_end[None, :]).astype(jnp.int32), axis=1)
    blk_expert = jnp.minimum(blk_expert, MOE_EXPERTS - 1)
    dest = (pad_start[ri[:, 0:2]] + ri[:, 2:4]).reshape(2 * n).astype(jnp.int32)

    xs = _dispatch(dest, x1, jnp.zeros((nblk * tb, d), F32))
    ys = _experts(blk_expert, nused.astype(jnp.int32), xs, w_gate[0], w_up[0], w_down[0])
    out = _combine(dest, x1, rw, vec(ln2_g[0]), vec(ln2_b[0]), ys)
    return out.reshape(batch, seq, d)
```

```python
import functools
import math

import jax
import jax.numpy as jnp
from jax import lax
from jax.experimental import pallas as pl
from jax.experimental.pallas import tpu as pltpu

F32 = jnp.float32
BF16 = jnp.bfloat16

ATTN_HEADS = 8
ATTN_HEAD_DIM = 64
ATTN_WIDTH = ATTN_HEADS * ATTN_HEAD_DIM
MOBA_BLOCK = 256
MOBA_TOPK = 3
ROPE_THETA = 10000.0
MLSTM_HEADS = 4
MLSTM_HEAD_DIM = 128
MLSTM_WIDTH = MLSTM_HEADS * MLSTM_HEAD_DIM
MLSTM_CONV = 4
MOE_GROUPS = 8
MOE_EXPERTS_PER_GROUP = 8
MOE_EXPERTS = MOE_GROUPS * MOE_EXPERTS_PER_GROUP
MOE_D_FF = 512
LN_EPS = 1e-5
GN_EPS = 1e-6
DEPTH = 1
DEEPNORM_ALPHA = (2 * DEPTH) ** 0.25

LANES = 128
SUBLANES = 8
ROW_TILE = 256
EXPERT_TILE = 256
VMEM_LIMIT = 48 * 1024 * 1024

PAST_UNROLL = 4
LOG2_E = math.log2(math.e)

NEG_INF = float("-inf")


def _params(*sem):
    return pltpu.CompilerParams(dimension_semantics=sem, vmem_limit_bytes=VMEM_LIMIT)


def _dot(a, b):
    return jnp.dot(a, b, preferred_element_type=F32)


def _dot_nt(a, b):
    return lax.dot_general(a, b, (((1,), (1,)), ((), ())), preferred_element_type=F32)


def _dot_tn(a, b):
    return lax.dot_general(a, b, (((0,), (0,)), ((), ())), preferred_element_type=F32)


def _split3(x):
    x1 = x.astype(BF16)
    r1 = x - x1.astype(F32)
    x2 = r1.astype(BF16)
    r2 = r1 - x2.astype(F32)
    return x1, x2, r2.astype(BF16)


def _layer_norm(x, g, b):
    mu = jnp.mean(x, axis=-1, keepdims=True)
    xc = x - mu
    var = jnp.mean(xc * xc, axis=-1, keepdims=True)
    return xc * lax.rsqrt(var + LN_EPS) * g + b


def _log_sigmoid(x):
    return jnp.minimum(x, 0.0) - jnp.log1p(jnp.exp(-jnp.abs(x)))


def _full(shape):
    nd = len(shape)
    return pl.BlockSpec(shape, lambda *_: (0,) * nd)


def _inproj_kernel(x_ref, g_ref, b_ref, wqkv_ref, wuvo_ref, wif_ref, wift_ref, wg_ref, cos_ref, sin_ref,
                   q_ref, k_ref, v_ref, km_ref, u_ref, vm_ref, o_ref, ifc_ref, ift_ref, ga_ref, gm_ref):
    xn = _layer_norm(x_ref[...], g_ref[...], b_ref[...])
    xb = xn.astype(BF16)

    cos = cos_ref[...]
    sin = sin_ref[...]
    lane = lax.broadcasted_iota(jnp.int32, cos.shape, 1)
    first_half = (lane % ATTN_HEAD_DIM) < (ATTN_HEAD_DIM // 2)

    def rope(t):
        fwd = pltpu.roll(t, ATTN_WIDTH - ATTN_HEAD_DIM // 2, axis=1)
        bwd = pltpu.roll(t, ATTN_HEAD_DIM // 2, axis=1)
        return t * cos + jnp.where(first_half, fwd, bwd) * sin

    zqkv = _dot(xb, wqkv_ref[...])
    q = rope(zqkv[:, :ATTN_WIDTH]) * (ATTN_HEAD_DIM ** -0.5 * LOG2_E)
    k = rope(zqkv[:, ATTN_WIDTH:2 * ATTN_WIDTH])
    v = zqkv[:, 2 * ATTN_WIDTH:]
    km_ref[0] = jnp.mean(k, axis=0, keepdims=True)
    qt = q.T
    vt = v.T
    for h in range(ATTN_HEADS):
        sl = slice(h * ATTN_HEAD_DIM, (h + 1) * ATTN_HEAD_DIM)
        q_ref[0, h] = qt[sl, :].astype(BF16)
        k_ref[0, h] = k[:, sl].astype(BF16)
        v_ref[0, h] = vt[sl, :].astype(BF16)

    zuvo = _dot(xb, wuvo_ref[...])
    u_ref[...] = zuvo[:, :MLSTM_WIDTH]
    vm_ref[...] = zuvo[:, MLSTM_WIDTH:2 * MLSTM_WIDTH].astype(BF16)
    o_ref[...] = zuvo[:, 2 * MLSTM_WIDTH:]

    ifc_ref[...] = _dot(xb, wif_ref[...])
    ift_ref[...] = _dot_nt(wift_ref[...], xb)

    zg = _dot(xb, wg_ref[...])
    d = ga_ref.shape[1]
    ga_ref[...] = zg[:, :d]
    gm_ref[...] = zg[:, d:]


def _inproj(x2, ln_g, ln_b, wqkv, wuvo, wif, wift, wg, cos, sin, batch, seq):
    n, d = x2.shape
    tm = ROW_TILE
    nsb = seq // tm
    hd = ATTN_HEAD_DIM
    row = lambda w: pl.BlockSpec((tm, w), lambda i: (i, 0))
    head = pl.BlockSpec((1, ATTN_HEADS, tm, hd), lambda i: (i // nsb, 0, i % nsb, 0))
    head_t = pl.BlockSpec((1, ATTN_HEADS, hd, tm), lambda i: (i // nsb, 0, 0, i % nsb))
    tab = pl.BlockSpec((tm, ATTN_WIDTH), lambda i: (i % nsb, 0))
    head_shape = jax.ShapeDtypeStruct((batch, ATTN_HEADS, seq, hd), BF16)
    head_t_shape = jax.ShapeDtypeStruct((batch, ATTN_HEADS, hd, seq), BF16)
    out_shape = (
        head_t_shape, head_shape, head_t_shape,
        jax.ShapeDtypeStruct((n // tm, 1, ATTN_WIDTH), F32),
        jax.ShapeDtypeStruct((n, MLSTM_WIDTH), F32),
        jax.ShapeDtypeStruct((n, MLSTM_WIDTH), BF16),
        jax.ShapeDtypeStruct((n, MLSTM_WIDTH), F32),
        jax.ShapeDtypeStruct((n, LANES), F32),
        jax.ShapeDtypeStruct((SUBLANES, n), F32),
        jax.ShapeDtypeStruct((n, d), F32),
        jax.ShapeDtypeStruct((n, d), F32),
    )
    out_specs = (
        head_t, head, head_t,
        pl.BlockSpec((1, 1, ATTN_WIDTH), lambda i: (i, 0, 0)),
        row(MLSTM_WIDTH), row(MLSTM_WIDTH), row(MLSTM_WIDTH),
        row(LANES),
        pl.BlockSpec((SUBLANES, tm), lambda i: (0, i)),
        row(d), row(d),
    )
    in_specs = [row(d), _full(ln_g.shape), _full(ln_b.shape), _full(wqkv.shape), _full(wuvo.shape),
                _full(wif.shape), _full(wift.shape), _full(wg.shape), tab, tab]
    return pl.pallas_call(
        _inproj_kernel, grid=(n // tm,), in_specs=in_specs, out_specs=out_specs, out_shape=out_shape,
        compiler_params=_params("parallel"), name="inproj",
    )(x2, ln_g, ln_b, wqkv, wuvo, wif, wift, wg, cos, sin)


def _moba_kernel(qt_ref, k_ref, vt_ref, km_ref, o_ref, bias_ref, m_ref, l_ref, acc_ref, s0_ref, s1_ref, mt_ref):
    i = pl.program_id(1)
    blk = MOBA_BLOCK
    hd = ATTN_HEAD_DIM
    nb = k_ref.shape[2] // blk
    blk_id = lax.broadcasted_iota(jnp.int32, (nb, blk), 0)
    key_pos = lax.broadcasted_iota(jnp.int32, (blk, blk), 0)
    qry_pos = lax.broadcasted_iota(jnp.int32, (blk, blk), 1)
    causal = key_pos <= qry_pos
    own = pl.multiple_of(i * blk, blk)

    for h in range(ATTN_HEADS):
        qt = qt_ref[0, h]
        km = km_ref[0, h]
        km_hi = km.astype(BF16)
        km_lo = (km - km_hi.astype(F32)).astype(BF16)
        gate = _dot(km_hi, qt) + _dot(km_lo, qt)
        gate = jnp.where(blk_id < i, gate, NEG_INF)
        for j in range(nb - 1):
            row = gate[j:j + 1, :]
            beats = (gate > row) | ((gate == row) & (blk_id < j))
            cnt = jnp.sum(jnp.where(beats, 1.0, 0.0), axis=0, keepdims=True)
            sel = (cnt < float(MOBA_TOPK)) & (row > NEG_INF)
            bias_ref[j * ATTN_HEADS + h] = jnp.where(sel, 0.0, NEG_INF)
    m_ref[...] = jnp.full(m_ref.shape, NEG_INF, F32)
    l_ref[...] = jnp.zeros_like(l_ref)
    acc_ref[...] = jnp.zeros_like(acc_ref)

    bufs = ((s0_ref, mt_ref.at[0]), (s1_ref, mt_ref.at[1]))

    def scores(h, off, buf, own_block):
        dst_ref, mt_dst = buf
        qt = qt_ref[0, h]
        half = blk // 2
        m_tile = None
        for c in range(2):
            rows = slice(c * half, (c + 1) * half)
            s = _dot(k_ref[0, h, pl.ds(pl.multiple_of(off + c * half, half), half), :], qt)
            if own_block:
                s = jnp.where(causal[rows], s, NEG_INF)
            dst_ref[rows, :] = s
            m_c = jnp.max(s, axis=0, keepdims=True)
            m_tile = m_c if m_tile is None else jnp.maximum(m_tile, m_c)
        mt_dst[...] = m_tile

    def update(h, off, buf, bias):
        src_ref, mt_src = buf
        m = m_ref[h]
        m_tile = mt_src[...]
        if bias is not None:
            m_tile = m_tile + bias
        m_new = jnp.maximum(m, m_tile)
        m_ref[h] = m_new
        a = jnp.exp2(m - m_new)
        shift = m_new if bias is None else m_new - bias
        p = jnp.exp2(src_ref[...] - shift)
        l_ref[h] = a * l_ref[h] + jnp.sum(p, axis=0, keepdims=True)
        acc_ref[h] = a * acc_ref[h] + _dot(vt_ref[0, h, :, pl.ds(off, blk)], p.astype(BF16))

    def run_tiles(t0, count, last, tile, own_block):
        for d in range(count):
            h_cur, off_cur = tile(t0 + d)
            t_nxt = t0 + d + 1
            t_nxt = min(t_nxt, last) if isinstance(t_nxt, int) and isinstance(last, int) else jnp.minimum(t_nxt, last)
            h_nxt, off_nxt = tile(t_nxt)
            scores(h_nxt, off_nxt, bufs[(d + 1) % 2], own_block)
            update(h_cur, off_cur, bufs[d % 2], None if own_block else bias_ref[t0 + d])

    own_tile = lambda t: (t, own)
    scores(0, own, bufs[0], True)
    run_tiles(0, ATTN_HEADS, ATTN_HEADS - 1, own_tile, True)

    @pl.when(i > 0)
    def _():
        past_tile = lambda t: (t % ATTN_HEADS, pl.multiple_of((t // ATTN_HEADS) * blk, blk))
        n_tiles = i * ATTN_HEADS
        scores(0, 0, bufs[0], False)

        def body(u, _):
            run_tiles(u * PAST_UNROLL, PAST_UNROLL, n_tiles - 1, past_tile, False)
            return 0

        lax.fori_loop(0, n_tiles // PAST_UNROLL, body, 0)

    yt = acc_ref[...] / l_ref[...]
    o_ref[0] = yt.reshape(ATTN_HEADS * hd, blk).T.astype(BF16)


def _moba(qt, k, vt, km):
    batch, heads, seq, hd = k.shape
    blk = MOBA_BLOCK
    nb = seq // blk
    return pl.pallas_call(
        _moba_kernel, grid=(batch, nb),
        in_specs=[
            pl.BlockSpec((1, heads, hd, blk), lambda b, i: (b, 0, 0, i)),
            pl.BlockSpec((1, heads, seq, hd), lambda b, i: (b, 0, 0, 0)),
            pl.BlockSpec((1, heads, hd, seq), lambda b, i: (b, 0, 0, 0)),
            pl.BlockSpec((1, heads, nb, hd), lambda b, i: (b, 0, 0, 0)),
        ],
        out_specs=pl.BlockSpec((1, blk, heads * hd), lambda b, i: (b, i, 0)),
        out_shape=jax.ShapeDtypeStruct((batch, seq, heads * hd), BF16),
        scratch_shapes=[pltpu.VMEM(((nb - 1) * heads, 1, blk), F32), pltpu.VMEM((heads, 1, blk), F32),
                        pltpu.VMEM((heads, 1, blk), F32), pltpu.VMEM((heads, hd, blk), F32),
                        pltpu.VMEM((blk, blk), F32), pltpu.VMEM((blk, blk), F32), pltpu.VMEM((2, 1, blk), F32)],
        compiler_params=_params("parallel", "arbitrary"), name="moba",
    )(qt, k, vt, km)


def _mlstm_kernel(u_ref, vm_ref, o_ref, ifc_ref, ift_ref, cw_ref, cb_ref, wq_ref, wk_ref, brow_ref, bcol_ref,
                  gn_ref, skip_ref, y_ref, ext_ref, c_ref, n_ref, m_ref):
    tm = u_ref.shape[0]
    hd = MLSTM_HEAD_DIM
    halo = SUBLANES

    @pl.when(pl.program_id(1) == 0)
    def _():
        ext_ref[0:halo, :] = jnp.zeros((halo, MLSTM_WIDTH), F32)
        c_ref[...] = jnp.zeros_like(c_ref)
        n_ref[...] = jnp.zeros_like(n_ref)
        m_ref[...] = jnp.zeros_like(m_ref)

    u = u_ref[...]
    ext_ref[halo:halo + tm, :] = u
    acc = jnp.broadcast_to(cb_ref[...], u.shape)
    for j in range(MLSTM_CONV):
        acc = acc + cw_ref[j:j + 1, :] * ext_ref[halo - (MLSTM_CONV - 1) + j:halo - (MLSTM_CONV - 1) + j + tm, :]
    ext_ref[0:halo, :] = u[tm - halo:, :]
    uc = acc * jax.nn.sigmoid(acc)

    gc = ifc_ref[...] + brow_ref[...]
    gr = ift_ref[...] + bcol_ref[...]
    rows = lax.broadcasted_iota(jnp.int32, (tm, tm), 0)
    cols = lax.broadcasted_iota(jnp.int32, (tm, tm), 1)
    causal = cols <= rows
    tril = jnp.where(causal, 1.0, 0.0).astype(BF16)
    triu = jnp.where(rows <= cols, 1.0, 0.0).astype(BF16)
    c1, c2, c3 = _split3(_log_sigmoid(gc))
    bcum_c = _dot(tril, c1) + _dot(tril, c2) + _dot(tril, c3)
    r1, r2, r3 = _split3(_log_sigmoid(gr))
    bcum_r = _dot(r1, triu) + _dot(r2, triu) + _dot(r3, triu)

    for h in range(MLSTM_HEADS):
        hs = slice(h * hd, (h + 1) * hd)
        fl = MLSTM_HEADS + h
        bt = bcum_c[:, fl:fl + 1]
        ig_c = gc[:, h:h + 1]
        row_t = gr[h:h + 1, :] - bcum_r[fl:fl + 1, :]
        m_prev = m_ref[h][:, 0:1]
        dlog = jnp.where(causal, bt + row_t, NEG_INF)
        inter = bt + m_prev
        m_t = jnp.maximum(inter, jnp.max(dlog, axis=1, keepdims=True))
        w_intra = jnp.exp(dlog - m_t)
        w_inter = jnp.exp(inter - m_t)

        ucb = uc[:, hs].astype(BF16)
        q = _dot(ucb, wq_ref[h])
        k = _dot(ucb, wk_ref[h]) * (hd ** -0.5)
        qb = q.astype(BF16)
        kb = k.astype(BF16)
        vb = vm_ref[:, hs]
        s = _dot_nt(qb, kb) * w_intra
        c_prev = c_ref[h]
        n_prev = n_ref[h]
        num = w_inter * _dot(qb, c_prev.astype(BF16)) + _dot(s.astype(BF16), vb)
        den = w_inter * jnp.sum(q * n_prev, axis=1, keepdims=True) + jnp.sum(s, axis=1, keepdims=True)
        hh = num / jnp.maximum(jnp.abs(den), jnp.exp(-m_t))

        b_end = bt[tm - 1:tm, :]
        w_log = b_end - bt + ig_c
        m_new = jnp.maximum(b_end + m_prev, jnp.max(w_log, axis=0, keepdims=True))
        decay = jnp.exp(b_end + m_prev - m_new)
        kw = k * jnp.exp(w_log - m_new)
        c_ref[h] = decay * c_prev + _dot_tn(kw.astype(BF16), vb)
        n_ref[h] = decay * n_prev + jnp.sum(kw, axis=0, keepdims=True)
        m_ref[h] = jnp.broadcast_to(m_new, (1, LANES))

        hh = jax.nn.sigmoid(o_ref[:, hs]) * hh
        mu = jnp.mean(hh, axis=1, keepdims=True)
        hc = hh - mu
        var = jnp.mean(hc * hc, axis=1, keepdims=True)
        y = hc * lax.rsqrt(var + GN_EPS) * gn_ref[:, hs] + skip_ref[:, hs] * uc[:, hs]
        y_ref[:, hs] = y.astype(BF16)


def _mlstm(u, vm, o, ifc, ift, conv_w, conv_b, wq, wk, brow, bcol, gn_g, skip, batch, seq):
    n = u.shape[0]
    tm = ROW_TILE
    nc = seq // tm
    row = lambda w: pl.BlockSpec((tm, w), lambda b, c: (b * nc + c, 0))
    in_specs = [row(MLSTM_WIDTH), row(MLSTM_WIDTH), row(MLSTM_WIDTH), row(LANES),
                pl.BlockSpec((SUBLANES, tm), lambda b, c: (0, b * nc + c)),
                _full(conv_w.shape), _full(conv_b.shape), _full(wq.shape), _full(wk.shape),
                _full(brow.shape), _full(bcol.shape), _full(gn_g.shape), _full(skip.shape)]
    return pl.pallas_call(
        _mlstm_kernel, grid=(batch, nc), in_specs=in_specs, out_specs=row(MLSTM_WIDTH),
        out_shape=jax.ShapeDtypeStruct((n, MLSTM_WIDTH), BF16),
        scratch_shapes=[pltpu.VMEM((SUBLANES + tm, MLSTM_WIDTH), F32),
                        pltpu.VMEM((MLSTM_HEADS, MLSTM_HEAD_DIM, MLSTM_HEAD_DIM), F32),
                        pltpu.VMEM((MLSTM_HEADS, 1, MLSTM_HEAD_DIM), F32),
                        pltpu.VMEM((MLSTM_HEADS, 1, LANES), F32)],
        compiler_params=_params("parallel", "arbitrary"), name="mlstm",
    )(u, vm, o, ifc, ift, conv_w, conv_b, wq, wk, brow, bcol, gn_g, skip)


def _mix_kernel(x_ref, g0_ref, b0_ref, ya_ref, ym_ref, ga_ref, gm_ref, wau_ref, wmu_ref, wout_ref,
                g1_ref, b1_ref, wrh_ref, wrl_ref, br_ref,
                x1_ref, ri_ref, rw_ref, cnt_out_ref, cnt_ref):
    tm = x_ref.shape[0]

    @pl.when(pl.program_id(0) == 0)
    def _():
        cnt_ref[...] = jnp.zeros_like(cnt_ref)

    xn = _layer_norm(x_ref[...], g0_ref[...], b0_ref[...])
    a_up = _dot(ya_ref[...], wau_ref[...])
    m_up = _dot(ym_ref[...], wmu_ref[...])
    mix = jax.nn.sigmoid(ga_ref[...]) * a_up + jax.nn.sigmoid(gm_ref[...]) * m_up
    x1 = _layer_norm(DEEPNORM_ALPHA * xn + _dot(mix.astype(BF16), wout_ref[...]), g1_ref[...], b1_ref[...])
    x1_ref[...] = x1

    x_hi = x1.astype(BF16)
    x_lo = (x1 - x_hi.astype(F32)).astype(BF16)
    w_hi = wrh_ref[...]
    logits = _dot(x_hi, w_hi) + _dot(x_lo, w_hi) + _dot(x_hi, wrl_ref[...]) + br_ref[...]
    lane = lax.broadcasted_iota(jnp.int32, (tm, LANES), 1).astype(F32)
    big = float(4 * LANES)
    is_g = (lane >= float(MOE_EXPERTS)) & (lane < float(MOE_EXPERTS + MOE_GROUPS))
    gl = jnp.where(is_g, logits, NEG_INF)
    ge = jnp.exp(gl - jnp.max(gl, axis=1, keepdims=True))
    gp = ge / jnp.sum(ge, axis=1, keepdims=True)
    g_w = jnp.max(gp, axis=1, keepdims=True)
    g_idx = jnp.min(jnp.where((gp == g_w) & is_g, lane - float(MOE_EXPERTS), big), axis=1, keepdims=True)
    lo = g_idx * float(MOE_EXPERTS_PER_GROUP)
    in_grp = (lane >= lo) & (lane < lo + float(MOE_EXPERTS_PER_GROUP))
    el = jnp.where(in_grp, logits, NEG_INF)
    v1 = jnp.max(el, axis=1, keepdims=True)
    i1 = jnp.min(jnp.where((el == v1) & in_grp, lane, big), axis=1, keepdims=True)
    el2 = jnp.where(lane == i1, NEG_INF, el)
    v2 = jnp.max(el2, axis=1, keepdims=True)
    i2 = jnp.min(jnp.where((el2 == v2) & in_grp & (lane != i1), lane, big), axis=1, keepdims=True)
    e2 = jnp.exp(v2 - v1)
    w0 = g_w / (1.0 + e2)
    w1 = g_w * e2 / (1.0 + e2)

    is1 = lane == i1
    is2 = lane == i2
    onehot = jnp.where(is1 | is2, 1.0, 0.0)
    rows = lax.broadcasted_iota(jnp.int32, (tm, tm), 0)
    cols = lax.broadcasted_iota(jnp.int32, (tm, tm), 1)
    strict = jnp.where(cols < rows, 1.0, 0.0).astype(BF16)
    before = _dot(strict, onehot.astype(BF16)) + cnt_ref[...]
    r0 = jnp.sum(jnp.where(is1, before, 0.0), axis=1, keepdims=True)
    r1 = jnp.sum(jnp.where(is2, before, 0.0), axis=1, keepdims=True)
    total = cnt_ref[...] + jnp.sum(onehot, axis=0, keepdims=True)
    cnt_ref[...] = total
    cnt_out_ref[...] = total

    ri = jnp.where(lane == 0.0, i1, jnp.where(lane == 1.0, i2, jnp.where(lane == 2.0, r0, jnp.where(lane == 3.0, r1, 0.0))))
    ri_ref[...] = ri.astype(jnp.int32)
    rw_ref[...] = jnp.where(lane == 0.0, w0, jnp.where(lane == 1.0, w1, 0.0))


def _mix(x2, g0, b0, ya, ym, ga, gm, wau, wmu, wout, g1, b1, wrh, wrl, br):
    n, d = x2.shape
    tm = ROW_TILE
    row = lambda w: pl.BlockSpec((tm, w), lambda i: (i, 0))
    in_specs = [row(d), _full(g0.shape), _full(b0.shape), row(ATTN_WIDTH), row(MLSTM_WIDTH), row(d), row(d),
                _full(wau.shape), _full(wmu.shape), _full(wout.shape), _full(g1.shape), _full(b1.shape),
                _full(wrh.shape), _full(wrl.shape), _full(br.shape)]
    out_shape = (jax.ShapeDtypeStruct((n, d), F32), jax.ShapeDtypeStruct((n, LANES), jnp.int32),
                 jax.ShapeDtypeStruct((n, LANES), F32), jax.ShapeDtypeStruct((1, LANES), F32))
    out_specs = (row(d), row(LANES), row(LANES), _full((1, LANES)))
    return pl.pallas_call(
        _mix_kernel, grid=(n // tm,), in_specs=in_specs, out_specs=out_specs, out_shape=out_shape,
        scratch_shapes=[pltpu.VMEM((1, LANES), F32)],
        compiler_params=_params("arbitrary"), name="mix",
    )(x2, g0, b0, ya, ym, ga, gm, wau, wmu, wout, g1, b1, wrh, wrl, br)


def _row_copy(src, src_row, dst, dst_row, sem):
    return pltpu.make_async_copy(src.at[pl.ds(src_row, 1), :], dst.at[pl.ds(dst_row, 1), :], sem)


def _dispatch_kernel(dest_ref, x_ref, xs_in_ref, xs_ref, sem):
    del xs_in_ref
    tm = x_ref.shape[0]

    def start(r, _):
        _row_copy(x_ref, r, xs_ref, dest_ref[2 * r], sem).start()
        _row_copy(x_ref, r, xs_ref, dest_ref[2 * r + 1], sem).start()
        return 0

    def wait(r, _):
        _row_copy(x_ref, r, xs_ref, dest_ref[2 * r], sem).wait()
        _row_copy(x_ref, r, xs_ref, dest_ref[2 * r + 1], sem).wait()
        return 0

    lax.fori_loop(0, tm, start, 0)
    lax.fori_loop(0, tm, wait, 0)


def _dispatch(dest, x1, xs_init):
    n, d = x1.shape
    tm = ROW_TILE
    return pl.pallas_call(
        _dispatch_kernel, grid=(n // tm,),
        in_specs=[pl.BlockSpec((2 * tm,), lambda i: (i,), memory_space=pltpu.SMEM),
                  pl.BlockSpec((tm, d), lambda i: (i, 0)),
                  pl.BlockSpec(memory_space=pl.ANY)],
        out_specs=pl.BlockSpec(memory_space=pl.ANY),
        out_shape=jax.ShapeDtypeStruct(xs_init.shape, xs_init.dtype),
        scratch_shapes=[pltpu.SemaphoreType.DMA(())],
        input_output_aliases={2: 0},
        compiler_params=_params("arbitrary"), name="dispatch",
    )(dest, x1, xs_init)


def _expert_kernel(be_ref, nused_ref, xs_ref, wg_ref, wu_ref, wd_ref, ys_ref):
    del be_ref
    used = pl.program_id(0) < nused_ref[0]

    @pl.when(used)
    def _():
        xb = xs_ref[...].astype(BF16)
        g = _dot(xb, wg_ref[0].astype(BF16))
        u = _dot(xb, wu_ref[0].astype(BF16))
        hmid = g * jax.nn.sigmoid(g) * u
        ys_ref[...] = _dot(hmid.astype(BF16), wd_ref[0].astype(BF16))

    @pl.when(jnp.logical_not(used))
    def _():
        ys_ref[...] = jnp.zeros_like(ys_ref)


def _experts(blk_expert, nused, xs, w_gate, w_up, w_down):
    p, d = xs.shape
    tb = EXPERT_TILE
    nblk = p // tb
    dff = w_gate.shape[2]
    blk_row = lambda i, be, nu: (jnp.minimum(i, nu[0] - 1), 0)
    grid_spec = pltpu.PrefetchScalarGridSpec(
        num_scalar_prefetch=2, grid=(nblk,),
        in_specs=[pl.BlockSpec((tb, d), blk_row),
                  pl.BlockSpec((1, d, dff), lambda i, be, nu: (be[i], 0, 0)),
                  pl.BlockSpec((1, d, dff), lambda i, be, nu: (be[i], 0, 0)),
                  pl.BlockSpec((1, dff, d), lambda i, be, nu: (be[i], 0, 0))],
        out_specs=pl.BlockSpec((tb, d), lambda i, be, nu: (i, 0)),
    )
    return pl.pallas_call(
        _expert_kernel, grid_spec=grid_spec, out_shape=jax.ShapeDtypeStruct((p, d), F32),
        compiler_params=_params("arbitrary"), name="experts",
    )(blk_expert, nused, xs, w_gate, w_up, w_down)


def _combine_kernel(dest_ref, x1_ref, rw_ref, g_ref, b_ref, ys_ref, o_ref, buf_ref, sem):
    tm = x1_ref.shape[0]

    def start(r, _):
        _row_copy(ys_ref, dest_ref[2 * r], buf_ref.at[0], r, sem).start()
        _row_copy(ys_ref, dest_ref[2 * r + 1], buf_ref.at[1], r, sem).start()
        return 0

    def wait(r, _):
        _row_copy(ys_ref, dest_ref[2 * r], buf_ref.at[0], r, sem).wait()
        _row_copy(ys_ref, dest_ref[2 * r + 1], buf_ref.at[1], r, sem).wait()
        return 0

    lax.fori_loop(0, tm, start, 0)
    lax.fori_loop(0, tm, wait, 0)
    rw = rw_ref[...]
    ffn = rw[:, 0:1] * buf_ref[0] + rw[:, 1:2] * buf_ref[1]
    o_ref[...] = _layer_norm(DEEPNORM_ALPHA * x1_ref[...] + ffn, g_ref[...], b_ref[...])


def _combine(dest, x1, rw, ln_g, ln_b, ys):
    n, d = x1.shape
    tm = ROW_TILE
    row = lambda w: pl.BlockSpec((tm, w), lambda i: (i, 0))
    return pl.pallas_call(
        _combine_kernel, grid=(n // tm,),
        in_specs=[pl.BlockSpec((2 * tm,), lambda i: (i,), memory_space=pltpu.SMEM),
                  row(d), row(LANES), _full(ln_g.shape), _full(ln_b.shape),
                  pl.BlockSpec(memory_space=pl.ANY)],
        out_specs=row(d),
        out_shape=jax.ShapeDtypeStruct((n, d), F32),
        scratch_shapes=[pltpu.VMEM((2, tm, d), F32), pltpu.SemaphoreType.DMA(())],
        compiler_params=_params("arbitrary"), name="combine",
    )(dest, x1, rw, ln_g, ln_b, ys)


def _rope_tables(seq):
    half = ATTN_HEAD_DIM // 2
    inv_freq = ROPE_THETA ** (-jnp.arange(half, dtype=F32) / half)
    ang = jnp.arange(seq, dtype=F32)[:, None] * inv_freq[None, :]
    cos = jnp.cos(ang)
    sin = jnp.sin(ang)
    cos_h = jnp.concatenate([cos, cos], axis=1)
    sin_h = jnp.concatenate([-sin, sin], axis=1)
    return jnp.tile(cos_h, (1, ATTN_HEADS)), jnp.tile(sin_h, (1, ATTN_HEADS))


def _pad_lanes(a, width=LANES):
    return jnp.pad(a, ((0, 0), (0, width - a.shape[1])))


def kernel(x, ln0_g, ln0_b, w_in, conv_w, conv_b, w_mq, w_mk, b_i, b_f, gn_g, skip, w_attn_up, w_mlstm_up, w_out,
           ln1_g, ln1_b, w_router_group, b_router_group, w_router_expert, b_router_expert, w_gate, w_up, w_down,
           ln2_g, ln2_b):
    batch, seq, d = x.shape
    n = batch * seq
    assert seq % ROW_TILE == 0 and ROW_TILE == MOBA_BLOCK and w_in.shape[0] == DEPTH
    x2 = x.reshape(n, d)
    vec = lambda a: a.reshape(1, -1).astype(F32)

    w = w_in[0]
    c_if = 3 * ATTN_WIDTH + 3 * MLSTM_WIDTH
    c_g = c_if + 2 * MLSTM_HEADS
    wqkv = w[:, :3 * ATTN_WIDTH].astype(BF16)
    wuvo = w[:, 3 * ATTN_WIDTH:c_if].astype(BF16)
    w_if = w[:, c_if:c_g]
    wif = _pad_lanes(w_if).astype(BF16)
    wift = w_if.T.astype(BF16)
    wg = w[:, c_g:].astype(BF16)
    cos, sin = _rope_tables(seq)

    q, k, v, kmean, u, vm, o, ifc, ift, ga, gm = _inproj(
        x2, vec(ln0_g), vec(ln0_b), wqkv, wuvo, wif, wift, wg, cos, sin, batch, seq)

    nb = seq // MOBA_BLOCK
    km = kmean.reshape(batch, nb, ATTN_HEADS, ATTN_HEAD_DIM).transpose(0, 2, 1, 3)
    ya = _moba(q, k, v, km).reshape(n, ATTN_WIDTH)

    b_if = jnp.concatenate([b_i[0], b_f[0]]).astype(F32)
    ym = _mlstm(u, vm, o, ifc, ift, conv_w[0], vec(conv_b[0]), w_mq[0].astype(BF16), w_mk[0].astype(BF16),
                _pad_lanes(b_if[None, :]), b_if[:, None], vec(gn_g[0]), vec(skip[0]), batch, seq)

    w_r = _pad_lanes(jnp.concatenate([w_router_expert[0], w_router_group[0]], axis=1))
    w_r_hi = w_r.astype(BF16)
    w_r_lo = (w_r - w_r_hi.astype(F32)).astype(BF16)
    b_r = _pad_lanes(jnp.concatenate([b_router_expert[0], b_router_group[0]])[None, :])
    x1, ri, rw, counts = _mix(
        x2, vec(ln0_g), vec(ln0_b), ya, ym, ga, gm, w_attn_up[0].astype(BF16), w_mlstm_up[0].astype(BF16),
        w_out[0].astype(BF16), vec(ln1_g[0]), vec(ln1_b[0]), w_r_hi, w_r_lo, b_r)

    tb = EXPERT_TILE
    nblk = (2 * n) // tb + MOE_EXPERTS
    cnt = counts[0, :MOE_EXPERTS].astype(jnp.int32)
    nblk_e = (cnt + tb - 1) // tb
    blk_end = jnp.cumsum(nblk_e)
    pad_start = (blk_end - nblk_e) * tb
    nused = blk_end[-1:]
    blk_ids = jnp.minimum(jnp.arange(nblk, dtype=jnp.int32), nused[0] - 1)
    blk_expert = jnp.sum((blk_ids[:, None] >= blk_end[None, :]).astype(jnp.int32), axis=1)
    blk_expert = jnp.minimum(blk_expert, MOE_EXPERTS - 1)
    dest = (pad_start[ri[:, 0:2]] + ri[:, 2:4]).reshape(2 * n).astype(jnp.int32)

    xs = _dispatch(dest, x1, jnp.zeros((nblk * tb, d), F32))
    ys = _experts(blk_expert, nused.astype(jnp.int32), xs, w_gate[0], w_up[0], w_down[0])
    out = _combine(dest, x1, rw, vec(ln2_g[0]), vec(ln2_b[0]), ys)
    return out.reshape(batch, seq, d)
```

```python
import functools
import math

import jax
import jax.numpy as jnp
from jax import lax
from jax.experimental import pallas as pl
from jax.experimental.pallas import tpu as pltpu

F32 = jnp.float32
BF16 = jnp.bfloat16

ATTN_HEADS = 8
ATTN_HEAD_DIM = 64
ATTN_WIDTH = ATTN_HEADS * ATTN_HEAD_DIM
MOBA_BLOCK = 256
MOBA_TOPK = 3
ROPE_THETA = 10000.0
MLSTM_HEADS = 4
MLSTM_HEAD_DIM = 128
MLSTM_WIDTH = MLSTM_HEADS * MLSTM_HEAD_DIM
MLSTM_CONV = 4
MOE_GROUPS = 8
MOE_EXPERTS_PER_GROUP = 8
MOE_EXPERTS = MOE_GROUPS * MOE_EXPERTS_PER_GROUP
MOE_D_FF = 512
LN_EPS = 1e-5
GN_EPS = 1e-6
DEPTH = 1
DEEPNORM_ALPHA = (2 * DEPTH) ** 0.25

LANES = 128
SUBLANES = 8
ROW_TILE = 256
EXPERT_TILE = 256
VMEM_LIMIT = 48 * 1024 * 1024

PAST_UNROLL = 4
LOG2_E = math.log2(math.e)

NEG_INF = float("-inf")


def _params(*sem):
    return pltpu.CompilerParams(dimension_semantics=sem, vmem_limit_bytes=VMEM_LIMIT)


def _dot(a, b):
    return jnp.dot(a, b, preferred_element_type=F32)


def _dot_nt(a, b):
    return lax.dot_general(a, b, (((1,), (1,)), ((), ())), preferred_element_type=F32)


def _dot_tn(a, b):
    return lax.dot_general(a, b, (((0,), (0,)), ((), ())), preferred_element_type=F32)


def _split3(x):
    x1 = x.astype(BF16)
    r1 = x - x1.astype(F32)
    x2 = r1.astype(BF16)
    r2 = r1 - x2.astype(F32)
    return x1, x2, r2.astype(BF16)


def _layer_norm(x, g, b):
    mu = jnp.mean(x, axis=-1, keepdims=True)
    xc = x - mu
    var = jnp.mean(xc * xc, axis=-1, keepdims=True)
    return xc * lax.rsqrt(var + LN_EPS) * g + b


def _log_sigmoid(x):
    return jnp.minimum(x, 0.0) - jnp.log1p(jnp.exp(-jnp.abs(x)))


def _full(shape):
    nd = len(shape)
    return pl.BlockSpec(shape, lambda *_: (0,) * nd)


def _inproj_kernel(x_ref, g_ref, b_ref, wqkv_ref, wuvo_ref, wif_ref, wift_ref, wg_ref, cos_ref, sin_ref,
                   q_ref, k_ref, v_ref, km_ref, u_ref, vm_ref, o_ref, ifc_ref, ift_ref, ga_ref, gm_ref):
    xn = _layer_norm(x_ref[...], g_ref[...], b_ref[...])
    xb = xn.astype(BF16)

    cos = cos_ref[...]
    sin = sin_ref[...]
    lane = lax.broadcasted_iota(jnp.int32, cos.shape, 1)
    first_half = (lane % ATTN_HEAD_DIM) < (ATTN_HEAD_DIM // 2)

    def rope(t):
        fwd = pltpu.roll(t, ATTN_WIDTH - ATTN_HEAD_DIM // 2, axis=1)
        bwd = pltpu.roll(t, ATTN_HEAD_DIM // 2, axis=1)
        return t * cos + jnp.where(first_half, fwd, bwd) * sin

    zqkv = _dot(xb, wqkv_ref[...])
    q = rope(zqkv[:, :ATTN_WIDTH]) * (ATTN_HEAD_DIM ** -0.5 * LOG2_E)
    k = rope(zqkv[:, ATTN_WIDTH:2 * ATTN_WIDTH])
    v = zqkv[:, 2 * ATTN_WIDTH:]
    km_ref[0] = jnp.mean(k, axis=0, keepdims=True)
    qt = q.T
    vt = v.T
    for h in range(ATTN_HEADS):
        sl = slice(h * ATTN_HEAD_DIM, (h + 1) * ATTN_HEAD_DIM)
        q_ref[0, h] = qt[sl, :].astype(BF16)
        k_ref[0, h] = k[:, sl].astype(BF16)
        v_ref[0, h] = vt[sl, :].astype(BF16)

    zuvo = _dot(xb, wuvo_ref[...])
    u_ref[...] = zuvo[:, :MLSTM_WIDTH]
    vm_ref[...] = zuvo[:, MLSTM_WIDTH:2 * MLSTM_WIDTH].astype(BF16)
    o_ref[...] = zuvo[:, 2 * MLSTM_WIDTH:]

    ifc_ref[...] = _dot(xb, wif_ref[...])
    ift_ref[...] = _dot_nt(wift_ref[...], xb)

    zg = _dot(xb, wg_ref[...])
    d = ga_ref.shape[1]
    ga_ref[...] = zg[:, :d]
    gm_ref[...] = zg[:, d:]


def _inproj(x2, ln_g, ln_b, wqkv, wuvo, wif, wift, wg, cos, sin, batch, seq):
    n, d = x2.shape
    tm = ROW_TILE
    nsb = seq // tm
    hd = ATTN_HEAD_DIM
    row = lambda w: pl.BlockSpec((tm, w), lambda i: (i, 0))
    head = pl.BlockSpec((1, ATTN_HEADS, tm, hd), lambda i: (i // nsb, 0, i % nsb, 0))
    head_t = pl.BlockSpec((1, ATTN_HEADS, hd, tm), lambda i: (i // nsb, 0, 0, i % nsb))
    tab = pl.BlockSpec((tm, ATTN_WIDTH), lambda i: (i % nsb, 0))
    head_shape = jax.ShapeDtypeStruct((batch, ATTN_HEADS, seq, hd), BF16)
    head_t_shape = jax.ShapeDtypeStruct((batch, ATTN_HEADS, hd, seq), BF16)
    out_shape = (
        head_t_shape, head_shape, head_t_shape,
        jax.ShapeDtypeStruct((n // tm, 1, ATTN_WIDTH), F32),
        jax.ShapeDtypeStruct((n, MLSTM_WIDTH), F32),
        jax.ShapeDtypeStruct((n, MLSTM_WIDTH), BF16),
        jax.ShapeDtypeStruct((n, MLSTM_WIDTH), F32),
        jax.ShapeDtypeStruct((n, LANES), F32),
        jax.ShapeDtypeStruct((SUBLANES, n), F32),
        jax.ShapeDtypeStruct((n, d), F32),
        jax.ShapeDtypeStruct((n, d), F32),
    )
    out_specs = (
        head_t, head, head_t,
        pl.BlockSpec((1, 1, ATTN_WIDTH), lambda i: (i, 0, 0)),
        row(MLSTM_WIDTH), row(MLSTM_WIDTH), row(MLSTM_WIDTH),
        row(LANES),
        pl.BlockSpec((SUBLANES, tm), lambda i: (0, i)),
        row(d), row(d),
    )
    in_specs = [row(d), _full(ln_g.shape), _full(ln_b.shape), _full(wqkv.shape), _full(wuvo.shape),
                _full(wif.shape), _full(wift.shape), _full(wg.shape), tab, tab]
    return pl.pallas_call(
        _inproj_kernel, grid=(n // tm,), in_specs=in_specs, out_specs=out_specs, out_shape=out_shape,
        compiler_params=_params("parallel"), name="inproj",
    )(x2, ln_g, ln_b, wqkv, wuvo, wif, wift, wg, cos, sin)


def _moba_kernel(qt_ref, k_ref, vt_ref, km_ref, o_ref, bias_ref, m_ref, l_ref, acc_ref, s0_ref, s1_ref, mt_ref):
    i = pl.program_id(1)
    blk = MOBA_BLOCK
    hd = ATTN_HEAD_DIM
    nb = k_ref.shape[2] // blk
    blk_id = lax.broadcasted_iota(jnp.int32, (nb, blk), 0)
    key_pos = lax.broadcasted_iota(jnp.int32, (blk, blk), 0)
    qry_pos = lax.broadcasted_iota(jnp.int32, (blk, blk), 1)
    causal = key_pos <= qry_pos
    own = pl.multiple_of(i * blk, blk)

    for h in range(ATTN_HEADS):
        qt = qt_ref[0, h]
        km = km_ref[0, h]
        km_hi = km.astype(BF16)
        km_lo = (km - km_hi.astype(F32)).astype(BF16)
        gate = _dot(km_hi, qt) + _dot(km_lo, qt)
        gate = jnp.where(blk_id < i, gate, NEG_INF)
        for j in range(nb - 1):
            row = gate[j:j + 1, :]
            beats = (gate > row) | ((gate == row) & (blk_id < j))
            cnt = jnp.sum(jnp.where(beats, 1.0, 0.0), axis=0, keepdims=True)
            sel = (cnt < float(MOBA_TOPK)) & (row > NEG_INF)
            bias_ref[j * ATTN_HEADS + h] = jnp.where(sel, 0.0, NEG_INF)
    m_ref[...] = jnp.full(m_ref.shape, NEG_INF, F32)
    l_ref[...] = jnp.zeros_like(l_ref)
    acc_ref[...] = jnp.zeros_like(acc_ref)

    bufs = ((s0_ref, mt_ref.at[0]), (s1_ref, mt_ref.at[1]))

    def scores(h, off, buf, own_block):
        dst_ref, mt_dst = buf
        qt = qt_ref[0, h]
        half = blk // 2
        m_tile = None
        for c in range(2):
            rows = slice(c * half, (c + 1) * half)
            s = _dot(k_ref[0, h, pl.ds(pl.multiple_of(off + c * half, half), half), :], qt)
            if own_block:
                s = jnp.where(causal[rows], s, NEG_INF)
            dst_ref[rows, :] = s
            m_c = jnp.max(s, axis=0, keepdims=True)
            m_tile = m_c if m_tile is None else jnp.maximum(m_tile, m_c)
        mt_dst[...] = m_tile

    def update(h, off, buf, bias):
        src_ref, mt_src = buf
        m = m_ref[h]
        m_tile = mt_src[...]
        if bias is not None:
            m_tile = m_tile + bias
        m_new = jnp.maximum(m, m_tile)
        m_ref[h] = m_new
        a = jnp.exp2(m - m_new)
        shift = m_new if bias is None else m_new - bias
        p = jnp.exp2(src_ref[...] - shift)
        l_ref[h] = a * l_ref[h] + jnp.sum(p, axis=0, keepdims=True)
        acc_ref[h] = a * acc_ref[h] + _dot(vt_ref[0, h, :, pl.ds(off, blk)], p.astype(BF16))

    def run_tiles(t0, count, last, tile, own_block):
        for d in range(count):
            h_cur, off_cur = tile(t0 + d)
            t_nxt = t0 + d + 1
            t_nxt = min(t_nxt, last) if isinstance(t_nxt, int) and isinstance(last, int) else jnp.minimum(t_nxt, last)
            h_nxt, off_nxt = tile(t_nxt)
            scores(h_nxt, off_nxt, bufs[(d + 1) % 2], own_block)
            update(h_cur, off_cur, bufs[d % 2], None if own_block else bias_ref[t0 + d])

    own_tile = lambda t: (t, own)
    scores(0, own, bufs[0], True)
    run_tiles(0, ATTN_HEADS, ATTN_HEADS - 1, own_tile, True)

    @pl.when(i > 0)
    def _():
        past_tile = lambda t: (t % ATTN_HEADS, pl.multiple_of((t // ATTN_HEADS) * blk, blk))
        n_tiles = i * ATTN_HEADS
        scores(0, 0, bufs[0], False)

        def body(u, _):
            run_tiles(u * PAST_UNROLL, PAST_UNROLL, n_tiles - 1, past_tile, False)
            return 0

        lax.fori_loop(0, n_tiles // PAST_UNROLL, body, 0)

    yt = acc_ref[...] / l_ref[...]
    o_ref[0] = yt.reshape(ATTN_HEADS * hd, blk).T.astype(BF16)


def _moba(qt, k, vt, km):
    batch, heads, seq, hd = k.shape
    blk = MOBA_BLOCK
    nb = seq // blk
    return pl.pallas_call(
        _moba_kernel, grid=(batch, nb),
        in_specs=[
            pl.BlockSpec((1, heads, hd, blk), lambda b, i: (b, 0, 0, i)),
            pl.BlockSpec((1, heads, seq, hd), lambda b, i: (b, 0, 0, 0)),
            pl.BlockSpec((1, heads, hd, seq), lambda b, i: (b, 0, 0, 0)),
            pl.BlockSpec((1, heads, nb, hd), lambda b, i: (b, 0, 0, 0)),
        ],
        out_specs=pl.BlockSpec((1, blk, heads * hd), lambda b, i: (b, i, 0)),
        out_shape=jax.ShapeDtypeStruct((batch, seq, heads * hd), BF16),
        scratch_shapes=[pltpu.VMEM(((nb - 1) * heads, 1, blk), F32), pltpu.VMEM((heads, 1, blk), F32),
                        pltpu.VMEM((heads, 1, blk), F32), pltpu.VMEM((heads, hd, blk), F32),
                        pltpu.VMEM((blk, blk), F32), pltpu.VMEM((blk, blk), F32), pltpu.VMEM((2, 1, blk), F32)],
        compiler_params=_params("parallel", "arbitrary"), name="moba",
    )(qt, k, vt, km)


def _mlstm_kernel(u_ref, vm_ref, o_ref, ifc_ref, ift_ref, cw_ref, cb_ref, wq_ref, wk_ref, brow_ref, bcol_ref,
                  gn_ref, skip_ref, y_ref, ext_ref, c_ref, n_ref, m_ref):
    tm = u_ref.shape[0]
    hd = MLSTM_HEAD_DIM
    halo = SUBLANES

    @pl.when(pl.program_id(1) == 0)
    def _():
        ext_ref[0:halo, :] = jnp.zeros((halo, MLSTM_WIDTH), F32)
        c_ref[...] = jnp.zeros_like(c_ref)
        n_ref[...] = jnp.zeros_like(n_ref)
        m_ref[...] = jnp.zeros_like(m_ref)

    u = u_ref[...]
    ext_ref[halo:halo + tm, :] = u
    acc = jnp.broadcast_to(cb_ref[...], u.shape)
    for j in range(MLSTM_CONV):
        acc = acc + cw_ref[j:j + 1, :] * ext_ref[halo - (MLSTM_CONV - 1) + j:halo - (MLSTM_CONV - 1) + j + tm, :]
    ext_ref[0:halo, :] = u[tm - halo:, :]
    uc = acc * jax.nn.sigmoid(acc)

    gc = ifc_ref[...] + brow_ref[...]
    gr = ift_ref[...] + bcol_ref[...]
    rows = lax.broadcasted_iota(jnp.int32, (tm, tm), 0)
    cols = lax.broadcasted_iota(jnp.int32, (tm, tm), 1)
    causal = cols <= rows
    tril = jnp.where(causal, 1.0, 0.0).astype(BF16)
    triu = jnp.where(rows <= cols, 1.0, 0.0).astype(BF16)
    c1, c2, c3 = _split3(_log_sigmoid(gc))
    bcum_c = _dot(tril, c1) + _dot(tril, c2) + _dot(tril, c3)
    r1, r2, r3 = _split3(_log_sigmoid(gr))
    bcum_r = _dot(r1, triu) + _dot(r2, triu) + _dot(r3, triu)

    for h in range(MLSTM_HEADS):
        hs = slice(h * hd, (h + 1) * hd)
        fl = MLSTM_HEADS + h
        bt = bcum_c[:, fl:fl + 1]
        ig_c = gc[:, h:h + 1]
        row_t = gr[h:h + 1, :] - bcum_r[fl:fl + 1, :]
        m_prev = m_ref[h][:, 0:1]
        dlog = jnp.where(causal, bt + row_t, NEG_INF)
        inter = bt + m_prev
        m_t = jnp.maximum(inter, jnp.max(dlog, axis=1, keepdims=True))
        w_intra = jnp.exp(dlog - m_t)
        w_inter = jnp.exp(inter - m_t)

        ucb = uc[:, hs].astype(BF16)
        q = _dot(ucb, wq_ref[h])
        k = _dot(ucb, wk_ref[h]) * (hd ** -0.5)
        qb = q.astype(BF16)
        kb = k.astype(BF16)
        vb = vm_ref[:, hs]
        s = _dot_nt(qb, kb) * w_intra
        c_prev = c_ref[h]
        n_prev = n_ref[h]
        num = w_inter * _dot(qb, c_prev.astype(BF16)) + _dot(s.astype(BF16), vb)
        den = w_inter * jnp.sum(q * n_prev, axis=1, keepdims=True) + jnp.sum(s, axis=1, keepdims=True)
        hh = num / jnp.maximum(jnp.abs(den), jnp.exp(-m_t))

        b_end = bt[tm - 1:tm, :]
        w_log = b_end - bt + ig_c
        m_new = jnp.maximum(b_end + m_prev, jnp.max(w_log, axis=0, keepdims=True))
        decay = jnp.exp(b_end + m_prev - m_new)
        kw = k * jnp.exp(w_log - m_new)
        c_ref[h] = decay * c_prev + _dot_tn(kw.astype(BF16), vb)
        n_ref[h] = decay * n_prev + jnp.sum(kw, axis=0, keepdims=True)
        m_ref[h] = jnp.broadcast_to(m_new, (1, LANES))

        hh = jax.nn.sigmoid(o_ref[:, hs]) * hh
        mu = jnp.mean(hh, axis=1, keepdims=True)
        hc = hh - mu
        var = jnp.mean(hc * hc, axis=1, keepdims=True)
        y = hc * lax.rsqrt(var + GN_EPS) * gn_ref[:, hs] + skip_ref[:, hs] * uc[:, hs]
        y_ref[:, hs] = y.astype(BF16)


def _mlstm(u, vm, o, ifc, ift, conv_w, conv_b, wq, wk, brow, bcol, gn_g, skip, batch, seq):
    n = u.shape[0]
    tm = ROW_TILE
    nc = seq // tm
    row = lambda w: pl.BlockSpec((tm, w), lambda b, c: (b * nc + c, 0))
    in_specs = [row(MLSTM_WIDTH), row(MLSTM_WIDTH), row(MLSTM_WIDTH), row(LANES),
                pl.BlockSpec((SUBLANES, tm), lambda b, c: (0, b * nc + c)),
                _full(conv_w.shape), _full(conv_b.shape), _full(wq.shape), _full(wk.shape),
                _full(brow.shape), _full(bcol.shape), _full(gn_g.shape), _full(skip.shape)]
    return pl.pallas_call(
        _mlstm_kernel, grid=(batch, nc), in_specs=in_specs, out_specs=row(MLSTM_WIDTH),
        out_shape=jax.ShapeDtypeStruct((n, MLSTM_WIDTH), BF16),
        scratch_shapes=[pltpu.VMEM((SUBLANES + tm, MLSTM_WIDTH), F32),
                        pltpu.VMEM((MLSTM_HEADS, MLSTM_HEAD_DIM, MLSTM_HEAD_DIM), F32),
                        pltpu.VMEM((MLSTM_HEADS, 1, MLSTM_HEAD_DIM), F32),
                        pltpu.VMEM((MLSTM_HEADS, 1, LANES), F32)],
        compiler_params=_params("parallel", "arbitrary"), name="mlstm",
    )(u, vm, o, ifc, ift, conv_w, conv_b, wq, wk, brow, bcol, gn_g, skip)


def _mix_kernel(x_ref, g0_ref, b0_ref, ya_ref, ym_ref, ga_ref, gm_ref, wau_ref, wmu_ref, wout_ref,
                g1_ref, b1_ref, wrh_ref, wrl_ref, br_ref,
                x1_ref, ri_ref, rw_ref, cnt_out_ref, cnt_ref):
    tm = x_ref.shape[0]

    @pl.when(pl.program_id(0) == 0)
    def _():
        cnt_ref[...] = jnp.zeros_like(cnt_ref)

    xn = _layer_norm(x_ref[...], g0_ref[...], b0_ref[...])
    a_up = _dot(ya_ref[...], wau_ref[...])
    m_up = _dot(ym_ref[...], wmu_ref[...])
    mix = jax.nn.sigmoid(ga_ref[...]) * a_up + jax.nn.sigmoid(gm_ref[...]) * m_up
    x1 = _layer_norm(DEEPNORM_ALPHA * xn + _dot(mix.astype(BF16), wout_ref[...]), g1_ref[...], b1_ref[...])
    x1_ref[...] = x1

    x_hi = x1.astype(BF16)
    x_lo = (x1 - x_hi.astype(F32)).astype(BF16)
    w_hi = wrh_ref[...]
    logits = _dot(x_hi, w_hi) + _dot(x_lo, w_hi) + _dot(x_hi, wrl_ref[...]) + br_ref[...]
    lane = lax.broadcasted_iota(jnp.int32, (tm, LANES), 1).astype(F32)
    big = float(4 * LANES)
    is_g = (lane >= float(MOE_EXPERTS)) & (lane < float(MOE_EXPERTS + MOE_GROUPS))
    gl = jnp.where(is_g, logits, NEG_INF)
    ge = jnp.exp(gl - jnp.max(gl, axis=1, keepdims=True))
    gp = ge / jnp.sum(ge, axis=1, keepdims=True)
    g_w = jnp.max(gp, axis=1, keepdims=True)
    g_idx = jnp.min(jnp.where((gp == g_w) & is_g, lane - float(MOE_EXPERTS), big), axis=1, keepdims=True)
    lo = g_idx * float(MOE_EXPERTS_PER_GROUP)
    in_grp = (lane >= lo) & (lane < lo + float(MOE_EXPERTS_PER_GROUP))
    el = jnp.where(in_grp, logits, NEG_INF)
    v1 = jnp.max(el, axis=1, keepdims=True)
    i1 = jnp.min(jnp.where((el == v1) & in_grp, lane, big), axis=1, keepdims=True)
    el2 = jnp.where(lane == i1, NEG_INF, el)
    v2 = jnp.max(el2, axis=1, keepdims=True)
    i2 = jnp.min(jnp.where((el2 == v2) & in_grp & (lane != i1), lane, big), axis=1, keepdims=True)
    e2 = jnp.exp(v2 - v1)
    w0 = g_w / (1.0 + e2)
    w1 = g_w * e2 / (1.0 + e2)

    is1 = lane == i1
    is2 = lane == i2
    onehot = jnp.where(is1 | is2, 1.0, 0.0)
    rows = lax.broadcasted_iota(jnp.int32, (tm, tm), 0)
    cols = lax.broadcasted_iota(jnp.int32, (tm, tm), 1)
    strict = jnp.where(cols < rows, 1.0, 0.0).astype(BF16)
    before = _dot(strict, onehot.astype(BF16)) + cnt_ref[...]
    r0 = jnp.sum(jnp.where(is1, before, 0.0), axis=1, keepdims=True)
    r1 = jnp.sum(jnp.where(is2, before, 0.0), axis=1, keepdims=True)
    total = cnt_ref[...] + jnp.sum(onehot, axis=0, keepdims=True)
    cnt_ref[...] = total
    cnt_out_ref[...] = total

    ri = jnp.where(lane == 0.0, i1, jnp.where(lane == 1.0, i2, jnp.where(lane == 2.0, r0, jnp.where(lane == 3.0, r1, 0.0))))
    ri_ref[...] = ri.astype(jnp.int32)
    rw_ref[...] = jnp.where(lane == 0.0, w0, jnp.where(lane == 1.0, w1, 0.0))


def _mix(x2, g0, b0, ya, ym, ga, gm, wau, wmu, wout, g1, b1, wrh, wrl, br):
    n, d = x2.shape
    tm = ROW_TILE
    row = lambda w: pl.BlockSpec((tm, w), lambda i: (i, 0))
    in_specs = [row(d), _full(g0.shape), _full(b0.shape), row(ATTN_WIDTH), row(MLSTM_WIDTH), row(d), row(d),
                _full(wau.shape), _full(wmu.shape), _full(wout.shape), _full(g1.shape), _full(b1.shape),
                _full(wrh.shape), _full(wrl.shape), _full(br.shape)]
    out_shape = (jax.ShapeDtypeStruct((n, d), F32), jax.ShapeDtypeStruct((n, LANES), jnp.int32),
                 jax.ShapeDtypeStruct((n, LANES), F32), jax.ShapeDtypeStruct((1, LANES), F32))
    out_specs = (row(d), row(LANES), row(LANES), _full((1, LANES)))
    return pl.pallas_call(
        _mix_kernel, grid=(n // tm,), in_specs=in_specs, out_specs=out_specs, out_shape=out_shape,
        scratch_shapes=[pltpu.VMEM((1, LANES), F32)],
        compiler_params=_params("arbitrary"), name="mix",
    )(x2, g0, b0, ya, ym, ga, gm, wau, wmu, wout, g1, b1, wrh, wrl, br)


def _to_token_tiles(dst_ref, x):
    nch = x.shape[1] // LANES
    for c in range(nch):
        dst_ref[pl.ds(c, x.shape[0], stride=nch), :] = x[:, c * LANES:(c + 1) * LANES]


def _from_token_tiles(src_ref, rows, nch):
    return [src_ref[pl.ds(c, rows, stride=nch), :] for c in range(nch)]


def _token_copy(src, src_tok, dst, dst_tok, nch, sem):
    s0 = pl.multiple_of(src_tok * nch, nch)
    d0 = pl.multiple_of(dst_tok * nch, nch)
    return pltpu.make_async_copy(src.at[pl.ds(s0, nch), :], dst.at[pl.ds(d0, nch), :], sem)


def _slot(er_ref, ps_ref, r, k):
    return ps_ref[er_ref[4 * r + k]] + er_ref[4 * r + 2 + k]


def _dispatch_kernel(er_ref, ps_ref, last_ref, x_ref, xs_ref, scr_ref, zero_ref, sem, zsem):
    tm, d = x_ref.shape
    nch = d // LANES
    tb = zero_ref.shape[0] // nch

    @pl.when(pl.program_id(0) == 0)
    def _():
        zero_ref[...] = jnp.zeros_like(zero_ref)

        def desc(tok):
            off = pl.multiple_of(jnp.maximum(tok, 0) * nch, nch)
            return pltpu.make_async_copy(zero_ref, xs_ref.at[pl.ds(off, tb * nch), :], zsem)

        def zstart(e, _):
            @pl.when(last_ref[e] >= 0)
            def _():
                desc(last_ref[e]).start()
            return 0

        def zwait(e, _):
            @pl.when(last_ref[e] >= 0)
            def _():
                desc(last_ref[e]).wait()
            return 0

        lax.fori_loop(0, MOE_EXPERTS, zstart, 0)
        nused = last_ref[MOE_EXPERTS]
        nblk = xs_ref.shape[0] // (tb * nch)
        lax.fori_loop(nused, nblk, lambda b, _: (desc(b * tb).start(), 0)[1], 0)
        lax.fori_loop(0, MOE_EXPERTS, zwait, 0)
        lax.fori_loop(nused, nblk, lambda b, _: (desc(b * tb).wait(), 0)[1], 0)

    _to_token_tiles(scr_ref, x_ref[...])

    def start(r, _):
        for k in range(2):
            _token_copy(scr_ref, r, xs_ref, _slot(er_ref, ps_ref, r, k), nch, sem).start(priority=k)
        return 0

    def wait(r, _):
        for k in range(2):
            _token_copy(scr_ref, r, xs_ref, _slot(er_ref, ps_ref, r, k), nch, sem).wait()
        return 0

    lax.fori_loop(0, tm, start, 0, unroll=8)
    lax.fori_loop(0, tm, wait, 0, unroll=8)


def _dispatch(er, pad_start, last_blk, x1, n_rows):
    n, d = x1.shape
    tm = ROW_TILE
    nch = d // LANES
    smem = lambda: pl.BlockSpec(memory_space=pltpu.SMEM)
    return pl.pallas_call(
        _dispatch_kernel, grid=(n // tm,),
        in_specs=[pl.BlockSpec((4 * tm,), lambda i: (i,), memory_space=pltpu.SMEM), smem(), smem(),
                  pl.BlockSpec((tm, d), lambda i: (i, 0))],
        out_specs=pl.BlockSpec(memory_space=pl.ANY),
        out_shape=jax.ShapeDtypeStruct((n_rows * nch, LANES), F32),
        scratch_shapes=[pltpu.VMEM((tm * nch, LANES), F32), pltpu.VMEM((EXPERT_TILE * nch, LANES), F32),
                        pltpu.SemaphoreType.DMA(()), pltpu.SemaphoreType.DMA(())],
        compiler_params=_params("arbitrary"), name="dispatch",
    )(er, pad_start, last_blk, x1)


def _expert_kernel(be_ref, nused_ref, xs_ref, wg_ref, wu_ref, wd_ref, ys_ref):
    del be_ref
    used = pl.program_id(0) < nused_ref[0]
    d = wg_ref.shape[1]
    nch = d // LANES
    tb = xs_ref.shape[0] // nch

    @pl.when(used)
    def _():
        xb = jnp.concatenate([c.astype(BF16) for c in _from_token_tiles(xs_ref, tb, nch)], axis=1)
        g = _dot(xb, wg_ref[0].astype(BF16))
        u = _dot(xb, wu_ref[0].astype(BF16))
        hmid = g * jax.nn.sigmoid(g) * u
        _to_token_tiles(ys_ref, _dot(hmid.astype(BF16), wd_ref[0].astype(BF16)))

    @pl.when(jnp.logical_not(used))
    def _():
        ys_ref[...] = jnp.zeros_like(ys_ref)


def _experts(blk_expert, nused, xs, w_gate, w_up, w_down):
    d, dff = w_gate.shape[1:]
    nch = d // LANES
    tb = EXPERT_TILE
    nblk = xs.shape[0] // (tb * nch)
    grid_spec = pltpu.PrefetchScalarGridSpec(
        num_scalar_prefetch=2, grid=(nblk,),
        in_specs=[pl.BlockSpec((tb * nch, LANES), lambda i, be, nu: (jnp.minimum(i, nu[0] - 1), 0)),
                  pl.BlockSpec((1, d, dff), lambda i, be, nu: (be[i], 0, 0)),
                  pl.BlockSpec((1, d, dff), lambda i, be, nu: (be[i], 0, 0)),
                  pl.BlockSpec((1, dff, d), lambda i, be, nu: (be[i], 0, 0))],
        out_specs=pl.BlockSpec((tb * nch, LANES), lambda i, be, nu: (i, 0)),
    )
    return pl.pallas_call(
        _expert_kernel, grid_spec=grid_spec, out_shape=jax.ShapeDtypeStruct(xs.shape, F32),
        compiler_params=_params("arbitrary"), name="experts",
    )(blk_expert, nused, xs, w_gate, w_up, w_down)


def _combine_kernel(er_ref, ps_ref, x1_ref, rw_ref, g_ref, b_ref, ys_ref, o_ref, buf0_ref, buf1_ref, sem):
    tm, d = x1_ref.shape
    nch = d // LANES
    bufs = (buf0_ref, buf1_ref)

    def start(r, _):
        for k in range(2):
            _token_copy(ys_ref, _slot(er_ref, ps_ref, r, k), bufs[k], r, nch, sem).start(priority=k)
        return 0

    def wait(r, _):
        for k in range(2):
            _token_copy(ys_ref, _slot(er_ref, ps_ref, r, k), bufs[k], r, nch, sem).wait()
        return 0

    lax.fori_loop(0, tm, start, 0, unroll=8)
    lax.fori_loop(0, tm, wait, 0, unroll=8)
    rw = rw_ref[...]
    y0 = jnp.concatenate(_from_token_tiles(buf0_ref, tm, nch), axis=1)
    y1 = jnp.concatenate(_from_token_tiles(buf1_ref, tm, nch), axis=1)
    ffn = rw[:, 0:1] * y0 + rw[:, 1:2] * y1
    o_ref[...] = _layer_norm(DEEPNORM_ALPHA * x1_ref[...] + ffn, g_ref[...], b_ref[...])


def _combine(er, pad_start, x1, rw, ln_g, ln_b, ys):
    n, d = x1.shape
    tm = ROW_TILE
    nch = d // LANES
    row = lambda w: pl.BlockSpec((tm, w), lambda i: (i, 0))
    return pl.pallas_call(
        _combine_kernel, grid=(n // tm,),
        in_specs=[pl.BlockSpec((4 * tm,), lambda i: (i,), memory_space=pltpu.SMEM),
                  pl.BlockSpec(memory_space=pltpu.SMEM),
                  row(d), row(LANES), _full(ln_g.shape), _full(ln_b.shape),
                  pl.BlockSpec(memory_space=pl.ANY)],
        out_specs=row(d),
        out_shape=jax.ShapeDtypeStruct((n, d), F32),
        scratch_shapes=[pltpu.VMEM((tm * nch, LANES), F32), pltpu.VMEM((tm * nch, LANES), F32),
                        pltpu.SemaphoreType.DMA(())],
        compiler_params=_params("arbitrary"), name="combine",
    )(er, pad_start, x1, rw, ln_g, ln_b, ys)


def _rope_tables(seq):
    half = ATTN_HEAD_DIM // 2
    inv_freq = ROPE_THETA ** (-jnp.arange(half, dtype=F32) / half)
    ang = jnp.arange(seq, dtype=F32)[:, None] * inv_freq[None, :]
    cos = jnp.cos(ang)
    sin = jnp.sin(ang)
    cos_h = jnp.concatenate([cos, cos], axis=1)
    sin_h = jnp.concatenate([-sin, sin], axis=1)
    return jnp.tile(cos_h, (1, ATTN_HEADS)), jnp.tile(sin_h, (1, ATTN_HEADS))


def _pad_lanes(a, width=LANES):
    return jnp.pad(a, ((0, 0), (0, width - a.shape[1])))


def kernel(x, ln0_g, ln0_b, w_in, conv_w, conv_b, w_mq, w_mk, b_i, b_f, gn_g, skip, w_attn_up, w_mlstm_up, w_out,
           ln1_g, ln1_b, w_router_group, b_router_group, w_router_expert, b_router_expert, w_gate, w_up, w_down,
           ln2_g, ln2_b):
    batch, seq, d = x.shape
    n = batch * seq
    assert seq % ROW_TILE == 0 and ROW_TILE == MOBA_BLOCK and w_in.shape[0] == DEPTH
    x2 = x.reshape(n, d)
    vec = lambda a: a.reshape(1, -1).astype(F32)

    w = w_in[0]
    c_if = 3 * ATTN_WIDTH + 3 * MLSTM_WIDTH
    c_g = c_if + 2 * MLSTM_HEADS
    wqkv = w[:, :3 * ATTN_WIDTH].astype(BF16)
    wuvo = w[:, 3 * ATTN_WIDTH:c_if].astype(BF16)
    w_if = w[:, c_if:c_g]
    wif = _pad_lanes(w_if).astype(BF16)
    wift = w_if.T.astype(BF16)
    wg = w[:, c_g:].astype(BF16)
    cos, sin = _rope_tables(seq)

    q, k, v, kmean, u, vm, o, ifc, ift, ga, gm = _inproj(
        x2, vec(ln0_g), vec(ln0_b), wqkv, wuvo, wif, wift, wg, cos, sin, batch, seq)

    nb = seq // MOBA_BLOCK
    km = kmean.reshape(batch, nb, ATTN_HEADS, ATTN_HEAD_DIM).transpose(0, 2, 1, 3)
    ya = _moba(q, k, v, km).reshape(n, ATTN_WIDTH)

    b_if = jnp.concatenate([b_i[0], b_f[0]]).astype(F32)
    ym = _mlstm(u, vm, o, ifc, ift, conv_w[0], vec(conv_b[0]), w_mq[0].astype(BF16), w_mk[0].astype(BF16),
                _pad_lanes(b_if[None, :]), b_if[:, None], vec(gn_g[0]), vec(skip[0]), batch, seq)

    w_r = _pad_lanes(jnp.concatenate([w_router_expert[0], w_router_group[0]], axis=1))
    w_r_hi = w_r.astype(BF16)
    w_r_lo = (w_r - w_r_hi.astype(F32)).astype(BF16)
    b_r = _pad_lanes(jnp.concatenate([b_router_expert[0], b_router_group[0]])[None, :])
    x1, ri, rw, counts = _mix(
        x2, vec(ln0_g), vec(ln0_b), ya, ym, ga, gm, w_attn_up[0].astype(BF16), w_mlstm_up[0].astype(BF16),
        w_out[0].astype(BF16), vec(ln1_g[0]), vec(ln1_b[0]), w_r_hi, w_r_lo, b_r)

    tb = EXPERT_TILE
    nblk = (2 * n) // tb + MOE_EXPERTS
    cnt = counts[0, :MOE_EXPERTS].astype(jnp.int32)
    nblk_e = (cnt + tb - 1) // tb
    blk_end = jnp.cumsum(nblk_e)
    pad_start = (blk_end - nblk_e) * tb
    nused = blk_end[-1:]
    blk_ids = jnp.minimum(jnp.arange(nblk, dtype=jnp.int32), nused[0] - 1)
    blk_expert = jnp.sum((blk_ids[:, None] >= blk_end[None, :]).astype(jnp.int32), axis=1)
    blk_expert = jnp.minimum(blk_expert, MOE_EXPERTS - 1)
    last_blk = jnp.where(nblk_e > 0, (blk_end - 1) * tb, -1)
    last_blk = jnp.concatenate([last_blk, nused]).astype(jnp.int32)
    er = ri[:, :4].reshape(4 * n)

    xs = _dispatch(er, pad_start, last_blk, x1, nblk * tb)
    ys = _experts(blk_expert, nused.astype(jnp.int32), xs, w_gate[0], w_up[0], w_down[0])
    out = _combine(er, pad_start, x1, rw, vec(ln2_g[0]), vec(ln2_b[0]), ys)
    return out.reshape(batch, seq, d)
```

```python
import functools
import math

import jax
import jax.numpy as jnp
from jax import lax
from jax.experimental import pallas as pl
from jax.experimental.pallas import tpu as pltpu

F32 = jnp.float32
BF16 = jnp.bfloat16

ATTN_HEADS = 8
ATTN_HEAD_DIM = 64
ATTN_WIDTH = ATTN_HEADS * ATTN_HEAD_DIM
MOBA_BLOCK = 256
MOBA_TOPK = 3
ROPE_THETA = 10000.0
MLSTM_HEADS = 4
MLSTM_HEAD_DIM = 128
MLSTM_WIDTH = MLSTM_HEADS * MLSTM_HEAD_DIM
MLSTM_CONV = 4
MOE_GROUPS = 8
MOE_EXPERTS_PER_GROUP = 8
MOE_EXPERTS = MOE_GROUPS * MOE_EXPERTS_PER_GROUP
MOE_D_FF = 512
LN_EPS = 1e-5
GN_EPS = 1e-6
DEPTH = 1
DEEPNORM_ALPHA = (2 * DEPTH) ** 0.25

LANES = 128
SUBLANES = 8
ROW_TILE = 256
EXPERT_TILE = 256
VMEM_LIMIT = 48 * 1024 * 1024

PAST_UNROLL = 4
LOG2_E = math.log2(math.e)

NEG_INF = float("-inf")


def _params(*sem):
    return pltpu.CompilerParams(dimension_semantics=sem, vmem_limit_bytes=VMEM_LIMIT)


def _dot(a, b):
    return jnp.dot(a, b, preferred_element_type=F32)


def _dot_nt(a, b):
    return lax.dot_general(a, b, (((1,), (1,)), ((), ())), preferred_element_type=F32)


def _dot_tn(a, b):
    return lax.dot_general(a, b, (((0,), (0,)), ((), ())), preferred_element_type=F32)


def _split3(x):
    x1 = x.astype(BF16)
    r1 = x - x1.astype(F32)
    x2 = r1.astype(BF16)
    r2 = r1 - x2.astype(F32)
    return x1, x2, r2.astype(BF16)


def _layer_norm(x, g, b):
    mu = jnp.mean(x, axis=-1, keepdims=True)
    xc = x - mu
    var = jnp.mean(xc * xc, axis=-1, keepdims=True)
    return xc * lax.rsqrt(var + LN_EPS) * g + b


def _log_sigmoid(x):
    return jnp.minimum(x, 0.0) - jnp.log1p(jnp.exp(-jnp.abs(x)))


def _full(shape):
    nd = len(shape)
    return pl.BlockSpec(shape, lambda *_: (0,) * nd)


def _inproj_kernel(x_ref, g_ref, b_ref, wqkv_ref, wuvo_ref, wif_ref, wift_ref, wg_ref, cos_ref, sin_ref,
                   q_ref, k_ref, v_ref, km_ref, u_ref, vm_ref, o_ref, ifc_ref, ift_ref, ga_ref, gm_ref):
    xn = _layer_norm(x_ref[...], g_ref[...], b_ref[...])
    xb = xn.astype(BF16)

    cos = cos_ref[...]
    sin = sin_ref[...]
    lane = lax.broadcasted_iota(jnp.int32, cos.shape, 1)
    first_half = (lane % ATTN_HEAD_DIM) < (ATTN_HEAD_DIM // 2)

    def rope(t):
        fwd = pltpu.roll(t, ATTN_WIDTH - ATTN_HEAD_DIM // 2, axis=1)
        bwd = pltpu.roll(t, ATTN_HEAD_DIM // 2, axis=1)
        return t * cos + jnp.where(first_half, fwd, bwd) * sin

    zqkv = _dot(xb, wqkv_ref[...])
    q = rope(zqkv[:, :ATTN_WIDTH]) * (ATTN_HEAD_DIM ** -0.5 * LOG2_E)
    k = rope(zqkv[:, ATTN_WIDTH:2 * ATTN_WIDTH])
    v = zqkv[:, 2 * ATTN_WIDTH:]
    km_ref[0] = jnp.mean(k, axis=0, keepdims=True)
    qt = q.T
    vt = v.T
    for h in range(ATTN_HEADS):
        sl = slice(h * ATTN_HEAD_DIM, (h + 1) * ATTN_HEAD_DIM)
        q_ref[0, h] = qt[sl, :].astype(BF16)
        k_ref[0, h] = k[:, sl].astype(BF16)
        v_ref[0, h] = vt[sl, :].astype(BF16)

    zuvo = _dot(xb, wuvo_ref[...])
    u_ref[...] = zuvo[:, :MLSTM_WIDTH]
    vm_ref[...] = zuvo[:, MLSTM_WIDTH:2 * MLSTM_WIDTH].astype(BF16)
    o_ref[...] = zuvo[:, 2 * MLSTM_WIDTH:]

    ifc_ref[...] = _dot(xb, wif_ref[...])
    ift_ref[...] = _dot_nt(wift_ref[...], xb)

    zg = _dot(xb, wg_ref[...])
    d = ga_ref.shape[1]
    ga_ref[...] = zg[:, :d]
    gm_ref[...] = zg[:, d:]


def _inproj(x2, ln_g, ln_b, wqkv, wuvo, wif, wift, wg, cos, sin, batch, seq):
    n, d = x2.shape
    tm = ROW_TILE
    nsb = seq // tm
    hd = ATTN_HEAD_DIM
    row = lambda w: pl.BlockSpec((tm, w), lambda i: (i, 0))
    head = pl.BlockSpec((1, ATTN_HEADS, tm, hd), lambda i: (i // nsb, 0, i % nsb, 0))
    head_t = pl.BlockSpec((1, ATTN_HEADS, hd, tm), lambda i: (i // nsb, 0, 0, i % nsb))
    tab = pl.BlockSpec((tm, ATTN_WIDTH), lambda i: (i % nsb, 0))
    head_shape = jax.ShapeDtypeStruct((batch, ATTN_HEADS, seq, hd), BF16)
    head_t_shape = jax.ShapeDtypeStruct((batch, ATTN_HEADS, hd, seq), BF16)
    out_shape = (
        head_t_shape, head_shape, head_t_shape,
        jax.ShapeDtypeStruct((n // tm, 1, ATTN_WIDTH), F32),
        jax.ShapeDtypeStruct((n, MLSTM_WIDTH), F32),
        jax.ShapeDtypeStruct((n, MLSTM_WIDTH), BF16),
        jax.ShapeDtypeStruct((n, MLSTM_WIDTH), F32),
        jax.ShapeDtypeStruct((n, LANES), F32),
        jax.ShapeDtypeStruct((SUBLANES, n), F32),
        jax.ShapeDtypeStruct((n, d), F32),
        jax.ShapeDtypeStruct((n, d), F32),
    )
    out_specs = (
        head_t, head, head_t,
        pl.BlockSpec((1, 1, ATTN_WIDTH), lambda i: (i, 0, 0)),
        row(MLSTM_WIDTH), row(MLSTM_WIDTH), row(MLSTM_WIDTH),
        row(LANES),
        pl.BlockSpec((SUBLANES, tm), lambda i: (0, i)),
        row(d), row(d),
    )
    in_specs = [row(d), _full(ln_g.shape), _full(ln_b.shape), _full(wqkv.shape), _full(wuvo.shape),
                _full(wif.shape), _full(wift.shape), _full(wg.shape), tab, tab]
    return pl.pallas_call(
        _inproj_kernel, grid=(n // tm,), in_specs=in_specs, out_specs=out_specs, out_shape=out_shape,
        compiler_params=_params("parallel"), name="inproj",
    )(x2, ln_g, ln_b, wqkv, wuvo, wif, wift, wg, cos, sin)


def _moba_kernel(qt_ref, k_ref, vt_ref, km_ref, o_ref, bias_ref, m_ref, l_ref, acc_ref, s0_ref, s1_ref, mt_ref):
    i = pl.program_id(1)
    blk = MOBA_BLOCK
    hd = ATTN_HEAD_DIM
    nb = k_ref.shape[2] // blk
    blk_id = lax.broadcasted_iota(jnp.int32, (nb, blk), 0)
    key_pos = lax.broadcasted_iota(jnp.int32, (blk, blk), 0)
    qry_pos = lax.broadcasted_iota(jnp.int32, (blk, blk), 1)
    causal = key_pos <= qry_pos
    own = pl.multiple_of(i * blk, blk)

    for h in range(ATTN_HEADS):
        qt = qt_ref[0, h]
        km = km_ref[0, h]
        km_hi = km.astype(BF16)
        km_lo = (km - km_hi.astype(F32)).astype(BF16)
        gate = _dot(km_hi, qt) + _dot(km_lo, qt)
        gate = jnp.where(blk_id < i, gate, NEG_INF)
        for j in range(nb - 1):
            row = gate[j:j + 1, :]
            beats = (gate > row) | ((gate == row) & (blk_id < j))
            cnt = jnp.sum(jnp.where(beats, 1.0, 0.0), axis=0, keepdims=True)
            sel = (cnt < float(MOBA_TOPK)) & (row > NEG_INF)
            bias_ref[j * ATTN_HEADS + h] = jnp.where(sel, 0.0, NEG_INF)
    m_ref[...] = jnp.full(m_ref.shape, NEG_INF, F32)
    l_ref[...] = jnp.zeros_like(l_ref)
    acc_ref[...] = jnp.zeros_like(acc_ref)

    bufs = ((s0_ref, mt_ref.at[0]), (s1_ref, mt_ref.at[1]))

    def scores(h, off, buf, own_block):
        dst_ref, mt_dst = buf
        qt = qt_ref[0, h]
        half = blk // 2
        m_tile = None
        for c in range(2):
            rows = slice(c * half, (c + 1) * half)
            s = _dot(k_ref[0, h, pl.ds(pl.multiple_of(off + c * half, half), half), :], qt)
            if own_block:
                s = jnp.where(causal[rows], s, NEG_INF)
            dst_ref[rows, :] = s
            m_c = jnp.max(s, axis=0, keepdims=True)
            m_tile = m_c if m_tile is None else jnp.maximum(m_tile, m_c)
        mt_dst[...] = m_tile

    def update(h, off, buf, bias):
        src_ref, mt_src = buf
        m = m_ref[h]
        m_tile = mt_src[...]
        if bias is not None:
            m_tile = m_tile + bias
        m_new = jnp.maximum(m, m_tile)
        m_ref[h] = m_new
        a = jnp.exp2(m - m_new)
        shift = m_new if bias is None else m_new - bias
        p = jnp.exp2(src_ref[...] - shift)
        l_ref[h] = a * l_ref[h] + jnp.sum(p, axis=0, keepdims=True)
        acc_ref[h] = a * acc_ref[h] + _dot(vt_ref[0, h, :, pl.ds(off, blk)], p.astype(BF16))

    def run_tiles(t0, count, last, tile, own_block):
        for d in range(count):
            h_cur, off_cur = tile(t0 + d)
            t_nxt = t0 + d + 1
            t_nxt = min(t_nxt, last) if isinstance(t_nxt, int) and isinstance(last, int) else jnp.minimum(t_nxt, last)
            h_nxt, off_nxt = tile(t_nxt)
            scores(h_nxt, off_nxt, bufs[(d + 1) % 2], own_block)
            update(h_cur, off_cur, bufs[d % 2], None if own_block else bias_ref[t0 + d])

    own_tile = lambda t: (t, own)
    scores(0, own, bufs[0], True)
    run_tiles(0, ATTN_HEADS, ATTN_HEADS - 1, own_tile, True)

    @pl.when(i > 0)
    def _():
        past_tile = lambda t: (t % ATTN_HEADS, pl.multiple_of((t // ATTN_HEADS) * blk, blk))
        n_tiles = i * ATTN_HEADS
        scores(0, 0, bufs[0], False)

        def body(u, _):
            run_tiles(u * PAST_UNROLL, PAST_UNROLL, n_tiles - 1, past_tile, False)
            return 0

        lax.fori_loop(0, n_tiles // PAST_UNROLL, body, 0)

    yt = acc_ref[...] / l_ref[...]
    o_ref[0] = yt.reshape(ATTN_HEADS * hd, blk).T.astype(BF16)


def _moba(qt, k, vt, km):
    batch, heads, seq, hd = k.shape
    blk = MOBA_BLOCK
    nb = seq // blk
    return pl.pallas_call(
        _moba_kernel, grid=(batch, nb),
        in_specs=[
            pl.BlockSpec((1, heads, hd, blk), lambda b, i: (b, 0, 0, i)),
            pl.BlockSpec((1, heads, seq, hd), lambda b, i: (b, 0, 0, 0)),
            pl.BlockSpec((1, heads, hd, seq), lambda b, i: (b, 0, 0, 0)),
            pl.BlockSpec((1, heads, nb, hd), lambda b, i: (b, 0, 0, 0)),
        ],
        out_specs=pl.BlockSpec((1, blk, heads * hd), lambda b, i: (b, i, 0)),
        out_shape=jax.ShapeDtypeStruct((batch, seq, heads * hd), BF16),
        scratch_shapes=[pltpu.VMEM(((nb - 1) * heads, 1, blk), F32), pltpu.VMEM((heads, 1, blk), F32),
                        pltpu.VMEM((heads, 1, blk), F32), pltpu.VMEM((heads, hd, blk), F32),
                        pltpu.VMEM((blk, blk), F32), pltpu.VMEM((blk, blk), F32), pltpu.VMEM((2, 1, blk), F32)],
        compiler_params=_params("parallel", "arbitrary"), name="moba",
    )(qt, k, vt, km)


def _mlstm_kernel(u_ref, vm_ref, o_ref, ifc_ref, ift_ref, cw_ref, cb_ref, wq_ref, wk_ref, brow_ref, bcol_ref,
                  gn_ref, skip_ref, y_ref, ext_ref, c_ref, n_ref, m_ref):
    tm = u_ref.shape[0]
    hd = MLSTM_HEAD_DIM
    halo = SUBLANES

    @pl.when(pl.program_id(1) == 0)
    def _():
        ext_ref[0:halo, :] = jnp.zeros((halo, MLSTM_WIDTH), F32)
        c_ref[...] = jnp.zeros_like(c_ref)
        n_ref[...] = jnp.zeros_like(n_ref)
        m_ref[...] = jnp.zeros_like(m_ref)

    u = u_ref[...]
    ext_ref[halo:halo + tm, :] = u
    acc = jnp.broadcast_to(cb_ref[...], u.shape)
    for j in range(MLSTM_CONV):
        acc = acc + cw_ref[j:j + 1, :] * ext_ref[halo - (MLSTM_CONV - 1) + j:halo - (MLSTM_CONV - 1) + j + tm, :]
    ext_ref[0:halo, :] = u[tm - halo:, :]
    uc = acc * jax.nn.sigmoid(acc)

    gc = ifc_ref[...] + brow_ref[...]
    gr = ift_ref[...] + bcol_ref[...]
    rows = lax.broadcasted_iota(jnp.int32, (tm, tm), 0)
    cols = lax.broadcasted_iota(jnp.int32, (tm, tm), 1)
    causal = cols <= rows
    tril = jnp.where(causal, 1.0, 0.0).astype(BF16)
    triu = jnp.where(rows <= cols, 1.0, 0.0).astype(BF16)
    c1, c2, c3 = _split3(_log_sigmoid(gc))
    bcum_c = _dot(tril, c1) + _dot(tril, c2) + _dot(tril, c3)
    r1, r2, r3 = _split3(_log_sigmoid(gr))
    bcum_r = _dot(r1, triu) + _dot(r2, triu) + _dot(r3, triu)

    for h in range(MLSTM_HEADS):
        hs = slice(h * hd, (h + 1) * hd)
        fl = MLSTM_HEADS + h
        bt = bcum_c[:, fl:fl + 1]
        ig_c = gc[:, h:h + 1]
        row_t = gr[h:h + 1, :] - bcum_r[fl:fl + 1, :]
        m_prev = m_ref[h][:, 0:1]
        dlog = jnp.where(causal, bt + row_t, NEG_INF)
        inter = bt + m_prev
        m_t = jnp.maximum(inter, jnp.max(dlog, axis=1, keepdims=True))
        w_intra = jnp.exp(dlog - m_t)
        w_inter = jnp.exp(inter - m_t)

        ucb = uc[:, hs].astype(BF16)
        q = _dot(ucb, wq_ref[h])
        k = _dot(ucb, wk_ref[h]) * (hd ** -0.5)
        qb = q.astype(BF16)
        kb = k.astype(BF16)
        vb = vm_ref[:, hs]
        s = _dot_nt(qb, kb) * w_intra
        c_prev = c_ref[h]
        n_prev = n_ref[h]
        num = w_inter * _dot(qb, c_prev.astype(BF16)) + _dot(s.astype(BF16), vb)
        den = w_inter * jnp.sum(q * n_prev, axis=1, keepdims=True) + jnp.sum(s, axis=1, keepdims=True)
        hh = num / jnp.maximum(jnp.abs(den), jnp.exp(-m_t))

        b_end = bt[tm - 1:tm, :]
        w_log = b_end - bt + ig_c
        m_new = jnp.maximum(b_end + m_prev, jnp.max(w_log, axis=0, keepdims=True))
        decay = jnp.exp(b_end + m_prev - m_new)
        kw = k * jnp.exp(w_log - m_new)
        c_ref[h] = decay * c_prev + _dot_tn(kw.astype(BF16), vb)
        n_ref[h] = decay * n_prev + jnp.sum(kw, axis=0, keepdims=True)
        m_ref[h] = jnp.broadcast_to(m_new, (1, LANES))

        hh = jax.nn.sigmoid(o_ref[:, hs]) * hh
        mu = jnp.mean(hh, axis=1, keepdims=True)
        hc = hh - mu
        var = jnp.mean(hc * hc, axis=1, keepdims=True)
        y = hc * lax.rsqrt(var + GN_EPS) * gn_ref[:, hs] + skip_ref[:, hs] * uc[:, hs]
        y_ref[:, hs] = y.astype(BF16)


def _mlstm(u, vm, o, ifc, ift, conv_w, conv_b, wq, wk, brow, bcol, gn_g, skip, batch, seq):
    n = u.shape[0]
    tm = ROW_TILE
    nc = seq // tm
    row = lambda w: pl.BlockSpec((tm, w), lambda b, c: (b * nc + c, 0))
    in_specs = [row(MLSTM_WIDTH), row(MLSTM_WIDTH), row(MLSTM_WIDTH), row(LANES),
                pl.BlockSpec((SUBLANES, tm), lambda b, c: (0, b * nc + c)),
                _full(conv_w.shape), _full(conv_b.shape), _full(wq.shape), _full(wk.shape),
                _full(brow.shape), _full(bcol.shape), _full(gn_g.shape), _full(skip.shape)]
    return pl.pallas_call(
        _mlstm_kernel, grid=(batch, nc), in_specs=in_specs, out_specs=row(MLSTM_WIDTH),
        out_shape=jax.ShapeDtypeStruct((n, MLSTM_WIDTH), BF16),
        scratch_shapes=[pltpu.VMEM((SUBLANES + tm, MLSTM_WIDTH), F32),
                        pltpu.VMEM((MLSTM_HEADS, MLSTM_HEAD_DIM, MLSTM_HEAD_DIM), F32),
                        pltpu.VMEM((MLSTM_HEADS, 1, MLSTM_HEAD_DIM), F32),
                        pltpu.VMEM((MLSTM_HEADS, 1, LANES), F32)],
        compiler_params=_params("parallel", "arbitrary"), name="mlstm",
    )(u, vm, o, ifc, ift, conv_w, conv_b, wq, wk, brow, bcol, gn_g, skip)


def _mix_kernel(x_ref, g0_ref, b0_ref, ya_ref, ym_ref, ga_ref, gm_ref, wau_ref, wmu_ref, wout_ref,
                g1_ref, b1_ref, wrh_ref, wrl_ref, br_ref,
                x1_ref, ri_ref, rw_ref, cnt_out_ref, cnt_ref):
    tm = x_ref.shape[0]

    @pl.when(pl.program_id(0) == 0)
    def _():
        cnt_ref[...] = jnp.zeros_like(cnt_ref)

    xn = _layer_norm(x_ref[...], g0_ref[...], b0_ref[...])
    a_up = _dot(ya_ref[...], wau_ref[...])
    m_up = _dot(ym_ref[...], wmu_ref[...])
    mix = jax.nn.sigmoid(ga_ref[...]) * a_up + jax.nn.sigmoid(gm_ref[...]) * m_up
    x1 = _layer_norm(DEEPNORM_ALPHA * xn + _dot(mix.astype(BF16), wout_ref[...]), g1_ref[...], b1_ref[...])
    x1_ref[...] = x1

    x_hi = x1.astype(BF16)
    x_lo = (x1 - x_hi.astype(F32)).astype(BF16)
    w_hi = wrh_ref[...]
    logits = _dot(x_hi, w_hi) + _dot(x_lo, w_hi) + _dot(x_hi, wrl_ref[...]) + br_ref[...]
    lane = lax.broadcasted_iota(jnp.int32, (tm, LANES), 1).astype(F32)
    big = float(4 * LANES)
    is_g = (lane >= float(MOE_EXPERTS)) & (lane < float(MOE_EXPERTS + MOE_GROUPS))
    gl = jnp.where(is_g, logits, NEG_INF)
    ge = jnp.exp(gl - jnp.max(gl, axis=1, keepdims=True))
    gp = ge / jnp.sum(ge, axis=1, keepdims=True)
    g_w = jnp.max(gp, axis=1, keepdims=True)
    g_idx = jnp.min(jnp.where((gp == g_w) & is_g, lane - float(MOE_EXPERTS), big), axis=1, keepdims=True)
    lo = g_idx * float(MOE_EXPERTS_PER_GROUP)
    in_grp = (lane >= lo) & (lane < lo + float(MOE_EXPERTS_PER_GROUP))
    el = jnp.where(in_grp, logits, NEG_INF)
    v1 = jnp.max(el, axis=1, keepdims=True)
    i1 = jnp.min(jnp.where((el == v1) & in_grp, lane, big), axis=1, keepdims=True)
    el2 = jnp.where(lane == i1, NEG_INF, el)
    v2 = jnp.max(el2, axis=1, keepdims=True)
    i2 = jnp.min(jnp.where((el2 == v2) & in_grp & (lane != i1), lane, big), axis=1, keepdims=True)
    e2 = jnp.exp(v2 - v1)
    w0 = g_w / (1.0 + e2)
    w1 = g_w * e2 / (1.0 + e2)

    is1 = lane == i1
    is2 = lane == i2
    onehot = jnp.where(is1 | is2, 1.0, 0.0)
    rows = lax.broadcasted_iota(jnp.int32, (tm, tm), 0)
    cols = lax.broadcasted_iota(jnp.int32, (tm, tm), 1)
    strict = jnp.where(cols < rows, 1.0, 0.0).astype(BF16)
    before = _dot(strict, onehot.astype(BF16)) + cnt_ref[...]
    r0 = jnp.sum(jnp.where(is1, before, 0.0), axis=1, keepdims=True)
    r1 = jnp.sum(jnp.where(is2, before, 0.0), axis=1, keepdims=True)
    total = cnt_ref[...] + jnp.sum(onehot, axis=0, keepdims=True)
    cnt_ref[...] = total
    cnt_out_ref[...] = total

    ri = jnp.where(lane == 0.0, i1, jnp.where(lane == 1.0, i2, jnp.where(lane == 2.0, r0, jnp.where(lane == 3.0, r1, 0.0))))
    ri_ref[...] = ri.astype(jnp.int32)
    rw_ref[...] = jnp.where(lane == 0.0, w0, jnp.where(lane == 1.0, w1, 0.0))


def _mix(x2, g0, b0, ya, ym, ga, gm, wau, wmu, wout, g1, b1, wrh, wrl, br):
    n, d = x2.shape
    tm = ROW_TILE
    row = lambda w: pl.BlockSpec((tm, w), lambda i: (i, 0))
    in_specs = [row(d), _full(g0.shape), _full(b0.shape), row(ATTN_WIDTH), row(MLSTM_WIDTH), row(d), row(d),
                _full(wau.shape), _full(wmu.shape), _full(wout.shape), _full(g1.shape), _full(b1.shape),
                _full(wrh.shape), _full(wrl.shape), _full(br.shape)]
    out_shape = (jax.ShapeDtypeStruct((n, d), F32), jax.ShapeDtypeStruct((n, LANES), jnp.int32),
                 jax.ShapeDtypeStruct((n, LANES), F32), jax.ShapeDtypeStruct((1, LANES), F32))
    out_specs = (row(d), row(LANES), row(LANES), _full((1, LANES)))
    return pl.pallas_call(
        _mix_kernel, grid=(n // tm,), in_specs=in_specs, out_specs=out_specs, out_shape=out_shape,
        scratch_shapes=[pltpu.VMEM((1, LANES), F32)],
        compiler_params=_params("arbitrary"), name="mix",
    )(x2, g0, b0, ya, ym, ga, gm, wau, wmu, wout, g1, b1, wrh, wrl, br)


def _to_token_tiles(dst_ref, x):
    nch = x.shape[1] // LANES
    for c in range(nch):
        dst_ref[pl.ds(c, x.shape[0], stride=nch), :] = x[:, c * LANES:(c + 1) * LANES]


def _from_token_tiles(src_ref, rows, nch):
    return [src_ref[pl.ds(c, rows, stride=nch), :] for c in range(nch)]


def _token_copy(src, src_tok, dst, dst_tok, nch, sem):
    s0 = pl.multiple_of(src_tok * nch, nch)
    d0 = pl.multiple_of(dst_tok * nch, nch)
    return pltpu.make_async_copy(src.at[pl.ds(s0, nch), :], dst.at[pl.ds(d0, nch), :], sem)


def _slot(er_ref, ps_ref, r, k):
    return ps_ref[er_ref[4 * r + k]] + er_ref[4 * r + 2 + k]


def _dispatch_kernel(er_ref, ps_ref, last_ref, x_ref, xs_ref, scr_ref, zero_ref, sem, zsem):
    tm, d = x_ref.shape
    nch = d // LANES
    tb = zero_ref.shape[0] // nch

    @pl.when(pl.program_id(0) == 0)
    def _():
        zero_ref[...] = jnp.zeros_like(zero_ref)

        def desc(tok):
            off = pl.multiple_of(jnp.maximum(tok, 0) * nch, nch)
            return pltpu.make_async_copy(zero_ref, xs_ref.at[pl.ds(off, tb * nch), :], zsem)

        def zstart(e, _):
            @pl.when(last_ref[e] >= 0)
            def _():
                desc(last_ref[e]).start()
            return 0

        def zwait(e, _):
            @pl.when(last_ref[e] >= 0)
            def _():
                desc(last_ref[e]).wait()
            return 0

        lax.fori_loop(0, MOE_EXPERTS, zstart, 0)
        nused = last_ref[MOE_EXPERTS]
        nblk = xs_ref.shape[0] // (tb * nch)
        lax.fori_loop(nused, nblk, lambda b, _: (desc(b * tb).start(), 0)[1], 0)
        lax.fori_loop(0, MOE_EXPERTS, zwait, 0)
        lax.fori_loop(nused, nblk, lambda b, _: (desc(b * tb).wait(), 0)[1], 0)

    _to_token_tiles(scr_ref, x_ref[...])

    def start(r, _):
        for k in range(2):
            _token_copy(scr_ref, r, xs_ref, _slot(er_ref, ps_ref, r, k), nch, sem).start(priority=k)
        return 0

    def wait(r, _):
        for k in range(2):
            _token_copy(scr_ref, r, xs_ref, _slot(er_ref, ps_ref, r, k), nch, sem).wait()
        return 0

    lax.fori_loop(0, tm, start, 0, unroll=8)
    lax.fori_loop(0, tm, wait, 0, unroll=8)


def _dispatch(er, pad_start, last_blk, x1, n_rows):
    n, d = x1.shape
    tm = ROW_TILE
    nch = d // LANES
    smem = lambda: pl.BlockSpec(memory_space=pltpu.SMEM)
    return pl.pallas_call(
        _dispatch_kernel, grid=(n // tm,),
        in_specs=[pl.BlockSpec((4 * tm,), lambda i: (i,), memory_space=pltpu.SMEM), smem(), smem(),
                  pl.BlockSpec((tm, d), lambda i: (i, 0))],
        out_specs=pl.BlockSpec(memory_space=pl.ANY),
        out_shape=jax.ShapeDtypeStruct((n_rows * nch, LANES), F32),
        scratch_shapes=[pltpu.VMEM((tm * nch, LANES), F32), pltpu.VMEM((EXPERT_TILE * nch, LANES), F32),
                        pltpu.SemaphoreType.DMA(()), pltpu.SemaphoreType.DMA(())],
        compiler_params=_params("arbitrary"), name="dispatch",
    )(er, pad_start, last_blk, x1)


def _expert_kernel(first_ref, count_ref, widx_ref, nused_ref, wg_ref, wu_ref, wd_ref, xs_ref, ys_ref,
                   wgb_ref, wub_ref, wdb_ref, xbuf_ref, ybuf_ref, in_sem, out_sem):
    del widx_ref
    e = pl.program_id(0)
    nused = nused_ref[0]
    d = wg_ref.shape[1]
    nch = d // LANES
    rows = xbuf_ref.shape[1]
    tb = rows // nch
    nblk = xs_ref.shape[0] // rows

    def blk(ref, b):
        return ref.at[pl.ds(pl.multiple_of(b * rows, rows), rows), :]

    def in_copy(b, slot):
        return pltpu.make_async_copy(blk(xs_ref, b), xbuf_ref.at[slot], in_sem.at[slot])

    def out_copy(b, slot):
        return pltpu.make_async_copy(ybuf_ref.at[slot], blk(ys_ref, b), out_sem.at[slot])

    @pl.when((e == 0) & (nused > 0))
    def _():
        in_copy(0, 0).start()

    @pl.when(count_ref[e] > 0)
    def _():
        wgb_ref[...] = wg_ref[0].astype(BF16)
        wub_ref[...] = wu_ref[0].astype(BF16)
        wdb_ref[...] = wd_ref[0].astype(BF16)

    def body(b, _):
        slot = b % 2
        in_copy(b, slot).wait()

        @pl.when(b + 1 < nused)
        def _():
            in_copy(b + 1, 1 - slot).start()

        @pl.when(b >= 2)
        def _():
            out_copy(b - 2, slot).wait()

        xb = jnp.concatenate([c.astype(BF16) for c in _from_token_tiles(xbuf_ref.at[slot], tb, nch)], axis=1)
        g = _dot(xb, wgb_ref[...])
        u = _dot(xb, wub_ref[...])
        hmid = g * jax.nn.sigmoid(g) * u
        _to_token_tiles(ybuf_ref.at[slot], _dot(hmid.astype(BF16), wdb_ref[...]))
        out_copy(b, slot).start()
        return 0

    lax.fori_loop(first_ref[e], first_ref[e] + count_ref[e], body, 0)

    @pl.when(e == pl.num_programs(0) - 1)
    def _():
        for back in (2, 1):
            @pl.when(nused >= back)
            def _():
                out_copy(nused - back, (nused - back) % 2).wait()

        ybuf_ref[0] = jnp.zeros(ybuf_ref.shape[1:], F32)
        lax.fori_loop(nused, nblk, lambda b, _: (out_copy(b, 0).start(), 0)[1], 0)
        lax.fori_loop(nused, nblk, lambda b, _: (out_copy(b, 0).wait(), 0)[1], 0)


def _experts(first_blk, blk_count, w_idx, nused, xs, w_gate, w_up, w_down):
    n_exp, d, dff = w_gate.shape
    nch = d // LANES
    rows = EXPERT_TILE * nch
    w_spec = lambda shape: pl.BlockSpec(shape, lambda e, fb, bc, wi, nu: (wi[e], 0, 0))
    any_spec = pl.BlockSpec(memory_space=pl.ANY)
    grid_spec = pltpu.PrefetchScalarGridSpec(
        num_scalar_prefetch=4, grid=(n_exp,),
        in_specs=[w_spec((1, d, dff)), w_spec((1, d, dff)), w_spec((1, dff, d)), any_spec],
        out_specs=any_spec,
        scratch_shapes=[pltpu.VMEM((d, dff), BF16), pltpu.VMEM((d, dff), BF16), pltpu.VMEM((dff, d), BF16),
                        pltpu.VMEM((2, rows, LANES), F32), pltpu.VMEM((2, rows, LANES), F32),
                        pltpu.SemaphoreType.DMA((2,)), pltpu.SemaphoreType.DMA((2,))],
    )
    return pl.pallas_call(
        _expert_kernel, grid_spec=grid_spec, out_shape=jax.ShapeDtypeStruct(xs.shape, F32),
        compiler_params=_params("arbitrary"), name="experts",
    )(first_blk, blk_count, w_idx, nused, w_gate, w_up, w_down, xs)


def _combine_kernel(er_ref, ps_ref, x1_ref, rw_ref, g_ref, b_ref, ys_ref, o_ref, buf0_ref, buf1_ref, sem):
    tm, d = x1_ref.shape
    nch = d // LANES
    bufs = (buf0_ref, buf1_ref)

    def start(r, _):
        for k in range(2):
            _token_copy(ys_ref, _slot(er_ref, ps_ref, r, k), bufs[k], r, nch, sem).start(priority=k)
        return 0

    def wait(r, _):
        for k in range(2):
            _token_copy(ys_ref, _slot(er_ref, ps_ref, r, k), bufs[k], r, nch, sem).wait()
        return 0

    lax.fori_loop(0, tm, start, 0, unroll=8)
    lax.fori_loop(0, tm, wait, 0, unroll=8)
    rw = rw_ref[...]
    y0 = jnp.concatenate(_from_token_tiles(buf0_ref, tm, nch), axis=1)
    y1 = jnp.concatenate(_from_token_tiles(buf1_ref, tm, nch), axis=1)
    ffn = rw[:, 0:1] * y0 + rw[:, 1:2] * y1
    o_ref[...] = _layer_norm(DEEPNORM_ALPHA * x1_ref[...] + ffn, g_ref[...], b_ref[...])


def _combine(er, pad_start, x1, rw, ln_g, ln_b, ys):
    n, d = x1.shape
    tm = ROW_TILE
    nch = d // LANES
    row = lambda w: pl.BlockSpec((tm, w), lambda i: (i, 0))
    return pl.pallas_call(
        _combine_kernel, grid=(n // tm,),
        in_specs=[pl.BlockSpec((4 * tm,), lambda i: (i,), memory_space=pltpu.SMEM),
                  pl.BlockSpec(memory_space=pltpu.SMEM),
                  row(d), row(LANES), _full(ln_g.shape), _full(ln_b.shape),
                  pl.BlockSpec(memory_space=pl.ANY)],
        out_specs=row(d),
        out_shape=jax.ShapeDtypeStruct((n, d), F32),
        scratch_shapes=[pltpu.VMEM((tm * nch, LANES), F32), pltpu.VMEM((tm * nch, LANES), F32),
                        pltpu.SemaphoreType.DMA(())],
        compiler_params=_params("arbitrary"), name="combine",
    )(er, pad_start, x1, rw, ln_g, ln_b, ys)


def _rope_tables(seq):
    half = ATTN_HEAD_DIM // 2
    inv_freq = ROPE_THETA ** (-jnp.arange(half, dtype=F32) / half)
    ang = jnp.arange(seq, dtype=F32)[:, None] * inv_freq[None, :]
    cos = jnp.cos(ang)
    sin = jnp.sin(ang)
    cos_h = jnp.concatenate([cos, cos], axis=1)
    sin_h = jnp.concatenate([-sin, sin], axis=1)
    return jnp.tile(cos_h, (1, ATTN_HEADS)), jnp.tile(sin_h, (1, ATTN_HEADS))


def _pad_lanes(a, width=LANES):
    return jnp.pad(a, ((0, 0), (0, width - a.shape[1])))


def kernel(x, ln0_g, ln0_b, w_in, conv_w, conv_b, w_mq, w_mk, b_i, b_f, gn_g, skip, w_attn_up, w_mlstm_up, w_out,
           ln1_g, ln1_b, w_router_group, b_router_group, w_router_expert, b_router_expert, w_gate, w_up, w_down,
           ln2_g, ln2_b):
    batch, seq, d = x.shape
    n = batch * seq
    assert seq % ROW_TILE == 0 and ROW_TILE == MOBA_BLOCK and w_in.shape[0] == DEPTH
    x2 = x.reshape(n, d)
    vec = lambda a: a.reshape(1, -1).astype(F32)

    w = w_in[0]
    c_if = 3 * ATTN_WIDTH + 3 * MLSTM_WIDTH
    c_g = c_if + 2 * MLSTM_HEADS
    wqkv = w[:, :3 * ATTN_WIDTH].astype(BF16)
    wuvo = w[:, 3 * ATTN_WIDTH:c_if].astype(BF16)
    w_if = w[:, c_if:c_g]
    wif = _pad_lanes(w_if).astype(BF16)
    wift = w_if.T.astype(BF16)
    wg = w[:, c_g:].astype(BF16)
    cos, sin = _rope_tables(seq)

    q, k, v, kmean, u, vm, o, ifc, ift, ga, gm = _inproj(
        x2, vec(ln0_g), vec(ln0_b), wqkv, wuvo, wif, wift, wg, cos, sin, batch, seq)

    nb = seq // MOBA_BLOCK
    km = kmean.reshape(batch, nb, ATTN_HEADS, ATTN_HEAD_DIM).transpose(0, 2, 1, 3)
    ya = _moba(q, k, v, km).reshape(n, ATTN_WIDTH)

    b_if = jnp.concatenate([b_i[0], b_f[0]]).astype(F32)
    ym = _mlstm(u, vm, o, ifc, ift, conv_w[0], vec(conv_b[0]), w_mq[0].astype(BF16), w_mk[0].astype(BF16),
                _pad_lanes(b_if[None, :]), b_if[:, None], vec(gn_g[0]), vec(skip[0]), batch, seq)

    w_r = _pad_lanes(jnp.concatenate([w_router_expert[0], w_router_group[0]], axis=1))
    w_r_hi = w_r.astype(BF16)
    w_r_lo = (w_r - w_r_hi.astype(F32)).astype(BF16)
    b_r = _pad_lanes(jnp.concatenate([b_router_expert[0], b_router_group[0]])[None, :])
    x1, ri, rw, counts = _mix(
        x2, vec(ln0_g), vec(ln0_b), ya, ym, ga, gm, w_attn_up[0].astype(BF16), w_mlstm_up[0].astype(BF16),
        w_out[0].astype(BF16), vec(ln1_g[0]), vec(ln1_b[0]), w_r_hi, w_r_lo, b_r)

    tb = EXPERT_TILE
    nblk = (2 * n) // tb + MOE_EXPERTS
    cnt = counts[0, :MOE_EXPERTS].astype(jnp.int32)
    nblk_e = (cnt + tb - 1) // tb
    blk_end = jnp.cumsum(nblk_e)
    pad_start = (blk_end - nblk_e) * tb
    nused = blk_end[-1:]
    ids = jnp.arange(MOE_EXPERTS, dtype=jnp.int32)
    prev_used = jnp.max(jnp.where((ids[None, :] <= ids[:, None]) & (nblk_e[None, :] > 0), ids[None, :], -1), axis=1)
    first_used = jnp.min(jnp.where(nblk_e > 0, ids, MOE_EXPERTS - 1))
    w_idx = jnp.where(prev_used >= 0, prev_used, first_used).astype(jnp.int32)
    last_blk = jnp.where(nblk_e > 0, (blk_end - 1) * tb, -1)
    last_blk = jnp.concatenate([last_blk, nused]).astype(jnp.int32)
    er = ri[:, :4].reshape(4 * n)

    xs = _dispatch(er, pad_start, last_blk, x1, nblk * tb)
    ys = _experts((blk_end - nblk_e).astype(jnp.int32), nblk_e.astype(jnp.int32), w_idx, nused.astype(jnp.int32),
                  xs, w_gate[0], w_up[0], w_down[0])
    out = _combine(er, pad_start, x1, rw, vec(ln2_g[0]), vec(ln2_b[0]), ys)
    return out.reshape(batch, seq, d)
```

```python
import functools
import math

import jax
import jax.numpy as jnp
from jax import lax
from jax.experimental import pallas as pl
from jax.experimental.pallas import tpu as pltpu

F32 = jnp.float32
BF16 = jnp.bfloat16

ATTN_HEADS = 8
ATTN_HEAD_DIM = 64
ATTN_WIDTH = ATTN_HEADS * ATTN_HEAD_DIM
MOBA_BLOCK = 256
MOBA_TOPK = 3
ROPE_THETA = 10000.0
MLSTM_HEADS = 4
MLSTM_HEAD_DIM = 128
MLSTM_WIDTH = MLSTM_HEADS * MLSTM_HEAD_DIM
MLSTM_CONV = 4
MOE_GROUPS = 8
MOE_EXPERTS_PER_GROUP = 8
MOE_EXPERTS = MOE_GROUPS * MOE_EXPERTS_PER_GROUP
MOE_D_FF = 512
LN_EPS = 1e-5
GN_EPS = 1e-6
DEPTH = 1
DEEPNORM_ALPHA = (2 * DEPTH) ** 0.25

LANES = 128
SUBLANES = 8
ROW_TILE = 256
EXPERT_TILE = 256
VMEM_LIMIT = 48 * 1024 * 1024
LOG2_E = math.log2(math.e)

NEG_INF = float("-inf")


def _params(*sem):
    return pltpu.CompilerParams(dimension_semantics=sem, vmem_limit_bytes=VMEM_LIMIT)


def _dot(a, b):
    return jnp.dot(a, b, preferred_element_type=F32)


def _dot_nt(a, b):
    return lax.dot_general(a, b, (((1,), (1,)), ((), ())), preferred_element_type=F32)


def _dot_tn(a, b):
    return lax.dot_general(a, b, (((0,), (0,)), ((), ())), preferred_element_type=F32)


def _split3(x):
    x1 = x.astype(BF16)
    r1 = x - x1.astype(F32)
    x2 = r1.astype(BF16)
    r2 = r1 - x2.astype(F32)
    return x1, x2, r2.astype(BF16)


def _layer_norm(x, g, b):
    mu = jnp.mean(x, axis=-1, keepdims=True)
    xc = x - mu
    var = jnp.mean(xc * xc, axis=-1, keepdims=True)
    return xc * lax.rsqrt(var + LN_EPS) * g + b


def _log_sigmoid(x):
    return jnp.minimum(x, 0.0) - jnp.log1p(jnp.exp(-jnp.abs(x)))


def _full(shape):
    nd = len(shape)
    return pl.BlockSpec(shape, lambda *_: (0,) * nd)


def _inproj_kernel(x_ref, g_ref, b_ref, wqkv_ref, wuvo_ref, wif_ref, wift_ref, wg_ref, cos_ref, sin_ref,
                   q_ref, k_ref, v_ref, km_ref, u_ref, vm_ref, o_ref, ifc_ref, ift_ref, ga_ref, gm_ref):
    xn = _layer_norm(x_ref[...], g_ref[...], b_ref[...])
    xb = xn.astype(BF16)

    cos = cos_ref[...]
    sin = sin_ref[...]
    lane = lax.broadcasted_iota(jnp.int32, cos.shape, 1)
    first_half = (lane % ATTN_HEAD_DIM) < (ATTN_HEAD_DIM // 2)

    def rope(t):
        fwd = pltpu.roll(t, ATTN_WIDTH - ATTN_HEAD_DIM // 2, axis=1)
        bwd = pltpu.roll(t, ATTN_HEAD_DIM // 2, axis=1)
        return t * cos + jnp.where(first_half, fwd, bwd) * sin

    zqkv = _dot(xb, wqkv_ref[...])
    q = rope(zqkv[:, :ATTN_WIDTH]) * (ATTN_HEAD_DIM ** -0.5 * LOG2_E)
    k = rope(zqkv[:, ATTN_WIDTH:2 * ATTN_WIDTH])
    v = zqkv[:, 2 * ATTN_WIDTH:]
    km_ref[0] = jnp.mean(k, axis=0, keepdims=True)
    qt = q.T
    vt = v.T
    for h in range(ATTN_HEADS):
        sl = slice(h * ATTN_HEAD_DIM, (h + 1) * ATTN_HEAD_DIM)
        q_ref[0, h] = qt[sl, :].astype(BF16)
        k_ref[0, h] = k[:, sl].astype(BF16)
        v_ref[0, h] = vt[sl, :].astype(BF16)

    zuvo = _dot(xb, wuvo_ref[...])
    u_ref[...] = zuvo[:, :MLSTM_WIDTH]
    vm_ref[...] = zuvo[:, MLSTM_WIDTH:2 * MLSTM_WIDTH].astype(BF16)
    o_ref[...] = zuvo[:, 2 * MLSTM_WIDTH:]

    ifc_ref[...] = _dot(xb, wif_ref[...])
    ift_ref[...] = _dot_nt(wift_ref[...], xb)

    zg = _dot(xb, wg_ref[...])
    d = ga_ref.shape[1]
    ga_ref[...] = zg[:, :d]
    gm_ref[...] = zg[:, d:]


def _inproj(x2, ln_g, ln_b, wqkv, wuvo, wif, wift, wg, cos, sin, batch, seq):
    n, d = x2.shape
    tm = ROW_TILE
    nsb = seq // tm
    hd = ATTN_HEAD_DIM
    row = lambda w: pl.BlockSpec((tm, w), lambda i: (i, 0))
    head = pl.BlockSpec((1, ATTN_HEADS, tm, hd), lambda i: (i // nsb, 0, i % nsb, 0))
    head_t = pl.BlockSpec((1, ATTN_HEADS, hd, tm), lambda i: (i // nsb, 0, 0, i % nsb))
    tab = pl.BlockSpec((tm, ATTN_WIDTH), lambda i: (i % nsb, 0))
    head_shape = jax.ShapeDtypeStruct((batch, ATTN_HEADS, seq, hd), BF16)
    head_t_shape = jax.ShapeDtypeStruct((batch, ATTN_HEADS, hd, seq), BF16)
    out_shape = (
        head_t_shape, head_shape, head_t_shape,
        jax.ShapeDtypeStruct((n // tm, 1, ATTN_WIDTH), F32),
        jax.ShapeDtypeStruct((n, MLSTM_WIDTH), F32),
        jax.ShapeDtypeStruct((n, MLSTM_WIDTH), BF16),
        jax.ShapeDtypeStruct((n, MLSTM_WIDTH), F32),
        jax.ShapeDtypeStruct((n, LANES), F32),
        jax.ShapeDtypeStruct((SUBLANES, n), F32),
        jax.ShapeDtypeStruct((n, d), F32),
        jax.ShapeDtypeStruct((n, d), F32),
    )
    out_specs = (
        head_t, head, head_t,
        pl.BlockSpec((1, 1, ATTN_WIDTH), lambda i: (i, 0, 0)),
        row(MLSTM_WIDTH), row(MLSTM_WIDTH), row(MLSTM_WIDTH),
        row(LANES),
        pl.BlockSpec((SUBLANES, tm), lambda i: (0, i)),
        row(d), row(d),
    )
    in_specs = [row(d), _full(ln_g.shape), _full(ln_b.shape), _full(wqkv.shape), _full(wuvo.shape),
                _full(wif.shape), _full(wift.shape), _full(wg.shape), tab, tab]
    return pl.pallas_call(
        _inproj_kernel, grid=(n // tm,), in_specs=in_specs, out_specs=out_specs, out_shape=out_shape,
        compiler_params=_params("parallel"), name="inproj",
    )(x2, ln_g, ln_b, wqkv, wuvo, wif, wift, wg, cos, sin)


def _moba_kernel(qt_ref, k_ref, vt_ref, km_ref, o_ref, bias_ref, m_ref, l_ref, acc_ref, s_ref):
    i = pl.program_id(1)
    blk = MOBA_BLOCK
    hd = ATTN_HEAD_DIM
    heads = ATTN_HEADS
    nb = k_ref.shape[2] // blk
    blk_id = lax.broadcasted_iota(jnp.int32, (nb, blk), 0)
    key_pos = lax.broadcasted_iota(jnp.int32, (blk, blk), 0)
    qry_pos = lax.broadcasted_iota(jnp.int32, (blk, blk), 1)
    causal = key_pos <= qry_pos

    for h in range(heads):
        qt = qt_ref[0, h]
        km = km_ref[0, h]
        km_hi = km.astype(BF16)
        km_lo = (km - km_hi.astype(F32)).astype(BF16)
        gate = _dot(km_hi, qt) + _dot(km_lo, qt)
        gate = jnp.where(blk_id < i, gate, NEG_INF)
        for j in range(nb - 1):
            row = gate[j:j + 1, :]
            beats = (gate > row) | ((gate == row) & (blk_id < j))
            cnt = jnp.sum(jnp.where(beats, 1.0, 0.0), axis=0, keepdims=True)
            sel = (cnt < float(MOBA_TOPK)) & (row > NEG_INF)
            bias_ref[j * heads + h] = jnp.where(sel, 0.0, NEG_INF)
    for h in range(heads):
        bias_ref[i * heads + h] = jnp.zeros((1, blk), F32)

    def scores(h, j, own_block):
        qt = qt_ref[0, h]
        half = blk // 2
        m_tile = None
        for c in range(2):
            rows = slice(c * half, (c + 1) * half)
            s = _dot(k_ref[0, h, pl.ds(pl.multiple_of(j * blk + c * half, half), half), :], qt)
            if own_block:
                s = jnp.where(causal[rows], s, NEG_INF)
            s_ref[j * heads + h, rows, :] = s
            m_c = jnp.max(s, axis=0, keepdims=True)
            m_tile = m_c if m_tile is None else jnp.maximum(m_tile, m_c)
        return m_tile

    for h in range(heads):
        m_ref[h] = scores(h, i, True)

    def past_scores(j, _):
        for h in range(heads):
            m_ref[h] = jnp.maximum(m_ref[h], scores(h, j, False) + bias_ref[j * heads + h])
        return 0

    lax.fori_loop(0, i, past_scores, 0)

    l_ref[...] = jnp.zeros_like(l_ref)
    acc_ref[...] = jnp.zeros_like(acc_ref)

    def accumulate(j, _):
        off = pl.multiple_of(j * blk, blk)
        for h in range(heads):
            p = jnp.exp2(s_ref[j * heads + h] - (m_ref[h] - bias_ref[j * heads + h]))
            l_ref[h] += jnp.sum(p, axis=0, keepdims=True)
            acc_ref[h] += _dot(vt_ref[0, h, :, pl.ds(off, blk)], p.astype(BF16))
        return 0

    lax.fori_loop(0, i + 1, accumulate, 0)
    yt = acc_ref[...] / l_ref[...]
    o_ref[0] = yt.reshape(heads * hd, blk).T.astype(BF16)


def _moba(qt, k, vt, km):
    batch, heads, seq, hd = k.shape
    blk = MOBA_BLOCK
    nb = seq // blk
    return pl.pallas_call(
        _moba_kernel, grid=(batch, nb),
        in_specs=[
            pl.BlockSpec((1, heads, hd, blk), lambda b, i: (b, 0, 0, i)),
            pl.BlockSpec((1, heads, seq, hd), lambda b, i: (b, 0, 0, 0)),
            pl.BlockSpec((1, heads, hd, seq), lambda b, i: (b, 0, 0, 0)),
            pl.BlockSpec((1, heads, nb, hd), lambda b, i: (b, 0, 0, 0)),
        ],
        out_specs=pl.BlockSpec((1, blk, heads * hd), lambda b, i: (b, i, 0)),
        out_shape=jax.ShapeDtypeStruct((batch, seq, heads * hd), BF16),
        scratch_shapes=[pltpu.VMEM((nb * heads, 1, blk), F32), pltpu.VMEM((heads, 1, blk), F32),
                        pltpu.VMEM((heads, 1, blk), F32), pltpu.VMEM((heads, hd, blk), F32),
                        pltpu.VMEM((nb * heads, blk, blk), F32)],
        compiler_params=_params("parallel", "arbitrary"), name="moba",
    )(qt, k, vt, km)


def _mlstm_kernel(u_ref, vm_ref, o_ref, ifc_ref, ift_ref, cw_ref, cb_ref, wq_ref, wk_ref, brow_ref, bcol_ref,
                  gn_ref, skip_ref, y_ref, ext_ref, c_ref, n_ref, m_ref):
    tm = u_ref.shape[0]
    hd = MLSTM_HEAD_DIM
    halo = SUBLANES

    @pl.when(pl.program_id(1) == 0)
    def _():
        ext_ref[0:halo, :] = jnp.zeros((halo, MLSTM_WIDTH), F32)
        c_ref[...] = jnp.zeros_like(c_ref)
        n_ref[...] = jnp.zeros_like(n_ref)
        m_ref[...] = jnp.zeros_like(m_ref)

    u = u_ref[...]
    ext_ref[halo:halo + tm, :] = u
    acc = jnp.broadcast_to(cb_ref[...], u.shape)
    for j in range(MLSTM_CONV):
        acc = acc + cw_ref[j:j + 1, :] * ext_ref[halo - (MLSTM_CONV - 1) + j:halo - (MLSTM_CONV - 1) + j + tm, :]
    ext_ref[0:halo, :] = u[tm - halo:, :]
    uc = acc * jax.nn.sigmoid(acc)

    gc = ifc_ref[...] + brow_ref[...]
    gr = ift_ref[...] + bcol_ref[...]
    rows = lax.broadcasted_iota(jnp.int32, (tm, tm), 0)
    cols = lax.broadcasted_iota(jnp.int32, (tm, tm), 1)
    causal = cols <= rows
    tril = jnp.where(causal, 1.0, 0.0).astype(BF16)
    triu = jnp.where(rows <= cols, 1.0, 0.0).astype(BF16)
    c1, c2, c3 = _split3(_log_sigmoid(gc))
    bcum_c = _dot(tril, c1) + _dot(tril, c2) + _dot(tril, c3)
    r1, r2, r3 = _split3(_log_sigmoid(gr))
    bcum_r = _dot(r1, triu) + _dot(r2, triu) + _dot(r3, triu)

    for h in range(MLSTM_HEADS):
        hs = slice(h * hd, (h + 1) * hd)
        fl = MLSTM_HEADS + h
        bt = bcum_c[:, fl:fl + 1]
        ig_c = gc[:, h:h + 1]
        row_t = gr[h:h + 1, :] - bcum_r[fl:fl + 1, :]
        m_prev = m_ref[h][:, 0:1]
        dlog = jnp.where(causal, bt + row_t, NEG_INF)
        inter = bt + m_prev
        m_t = jnp.maximum(inter, jnp.max(dlog, axis=1, keepdims=True))
        w_intra = jnp.exp(dlog - m_t)
        w_inter = jnp.exp(inter - m_t)

        ucb = uc[:, hs].astype(BF16)
        q = _dot(ucb, wq_ref[h])
        k = _dot(ucb, wk_ref[h]) * (hd ** -0.5)
        qb = q.astype(BF16)
        kb = k.astype(BF16)
        vb = vm_ref[:, hs]
        s = _dot_nt(qb, kb) * w_intra
        c_prev = c_ref[h]
        n_prev = n_ref[h]
        num = w_inter * _dot(qb, c_prev.astype(BF16)) + _dot(s.astype(BF16), vb)
        den = w_inter * jnp.sum(q * n_prev, axis=1, keepdims=True) + jnp.sum(s, axis=1, keepdims=True)
        hh = num / jnp.maximum(jnp.abs(den), jnp.exp(-m_t))

        b_end = bt[tm - 1:tm, :]
        w_log = b_end - bt + ig_c
        m_new = jnp.maximum(b_end + m_prev, jnp.max(w_log, axis=0, keepdims=True))
        decay = jnp.exp(b_end + m_prev - m_new)
        kw = k * jnp.exp(w_log - m_new)
        c_ref[h] = decay * c_prev + _dot_tn(kw.astype(BF16), vb)
        n_ref[h] = decay * n_prev + jnp.sum(kw, axis=0, keepdims=True)
        m_ref[h] = jnp.broadcast_to(m_new, (1, LANES))

        hh = jax.nn.sigmoid(o_ref[:, hs]) * hh
        mu = jnp.mean(hh, axis=1, keepdims=True)
        hc = hh - mu
        var = jnp.mean(hc * hc, axis=1, keepdims=True)
        y = hc * lax.rsqrt(var + GN_EPS) * gn_ref[:, hs] + skip_ref[:, hs] * uc[:, hs]
        y_ref[:, hs] = y.astype(BF16)


def _mlstm(u, vm, o, ifc, ift, conv_w, conv_b, wq, wk, brow, bcol, gn_g, skip, batch, seq):
    n = u.shape[0]
    tm = ROW_TILE
    nc = seq // tm
    row = lambda w: pl.BlockSpec((tm, w), lambda b, c: (b * nc + c, 0))
    in_specs = [row(MLSTM_WIDTH), row(MLSTM_WIDTH), row(MLSTM_WIDTH), row(LANES),
                pl.BlockSpec((SUBLANES, tm), lambda b, c: (0, b * nc + c)),
                _full(conv_w.shape), _full(conv_b.shape), _full(wq.shape), _full(wk.shape),
                _full(brow.shape), _full(bcol.shape), _full(gn_g.shape), _full(skip.shape)]
    return pl.pallas_call(
        _mlstm_kernel, grid=(batch, nc), in_specs=in_specs, out_specs=row(MLSTM_WIDTH),
        out_shape=jax.ShapeDtypeStruct((n, MLSTM_WIDTH), BF16),
        scratch_shapes=[pltpu.VMEM((SUBLANES + tm, MLSTM_WIDTH), F32),
                        pltpu.VMEM((MLSTM_HEADS, MLSTM_HEAD_DIM, MLSTM_HEAD_DIM), F32),
                        pltpu.VMEM((MLSTM_HEADS, 1, MLSTM_HEAD_DIM), F32),
                        pltpu.VMEM((MLSTM_HEADS, 1, LANES), F32)],
        compiler_params=_params("parallel", "arbitrary"), name="mlstm",
    )(u, vm, o, ifc, ift, conv_w, conv_b, wq, wk, brow, bcol, gn_g, skip)


def _mix_kernel(x_ref, g0_ref, b0_ref, ya_ref, ym_ref, ga_ref, gm_ref, wau_ref, wmu_ref, wout_ref,
                g1_ref, b1_ref, wrh_ref, wrl_ref, br_ref,
                x1_ref, ri_ref, rw_ref, cnt_out_ref, cnt_ref):
    tm = x_ref.shape[0]

    @pl.when(pl.program_id(0) == 0)
    def _():
        cnt_ref[...] = jnp.zeros_like(cnt_ref)

    xn = _layer_norm(x_ref[...], g0_ref[...], b0_ref[...])
    a_up = _dot(ya_ref[...], wau_ref[...])
    m_up = _dot(ym_ref[...], wmu_ref[...])
    mix = jax.nn.sigmoid(ga_ref[...]) * a_up + jax.nn.sigmoid(gm_ref[...]) * m_up
    x1 = _layer_norm(DEEPNORM_ALPHA * xn + _dot(mix.astype(BF16), wout_ref[...]), g1_ref[...], b1_ref[...])
    x1_ref[...] = x1

    x_hi = x1.astype(BF16)
    x_lo = (x1 - x_hi.astype(F32)).astype(BF16)
    w_hi = wrh_ref[...]
    logits = _dot(x_hi, w_hi) + _dot(x_lo, w_hi) + _dot(x_hi, wrl_ref[...]) + br_ref[...]
    lane = lax.broadcasted_iota(jnp.int32, (tm, LANES), 1).astype(F32)
    big = float(4 * LANES)
    is_g = (lane >= float(MOE_EXPERTS)) & (lane < float(MOE_EXPERTS + MOE_GROUPS))
    gl = jnp.where(is_g, logits, NEG_INF)
    ge = jnp.exp(gl - jnp.max(gl, axis=1, keepdims=True))
    gp = ge / jnp.sum(ge, axis=1, keepdims=True)
    g_w = jnp.max(gp, axis=1, keepdims=True)
    g_idx = jnp.min(jnp.where((gp == g_w) & is_g, lane - float(MOE_EXPERTS), big), axis=1, keepdims=True)
    lo = g_idx * float(MOE_EXPERTS_PER_GROUP)
    in_grp = (lane >= lo) & (lane < lo + float(MOE_EXPERTS_PER_GROUP))
    el = jnp.where(in_grp, logits, NEG_INF)
    v1 = jnp.max(el, axis=1, keepdims=True)
    i1 = jnp.min(jnp.where((el == v1) & in_grp, lane, big), axis=1, keepdims=True)
    el2 = jnp.where(lane == i1, NEG_INF, el)
    v2 = jnp.max(el2, axis=1, keepdims=True)
    i2 = jnp.min(jnp.where((el2 == v2) & in_grp & (lane != i1), lane, big), axis=1, keepdims=True)
    e2 = jnp.exp(v2 - v1)
    w0 = g_w / (1.0 + e2)
    w1 = g_w * e2 / (1.0 + e2)

    is1 = lane == i1
    is2 = lane == i2
    onehot = jnp.where(is1 | is2, 1.0, 0.0)
    rows = lax.broadcasted_iota(jnp.int32, (tm, tm), 0)
    cols = lax.broadcasted_iota(jnp.int32, (tm, tm), 1)
    strict = jnp.where(cols < rows, 1.0, 0.0).astype(BF16)
    before = _dot(strict, onehot.astype(BF16)) + cnt_ref[...]
    r0 = jnp.sum(jnp.where(is1, before, 0.0), axis=1, keepdims=True)
    r1 = jnp.sum(jnp.where(is2, before, 0.0), axis=1, keepdims=True)
    total = cnt_ref[...] + jnp.sum(onehot, axis=0, keepdims=True)
    cnt_ref[...] = total
    cnt_out_ref[...] = total

    ri = jnp.where(lane == 0.0, i1, jnp.where(lane == 1.0, i2, jnp.where(lane == 2.0, r0, jnp.where(lane == 3.0, r1, 0.0))))
    ri_ref[...] = ri.astype(jnp.int32)
    rw_ref[...] = jnp.where(lane == 0.0, w0, jnp.where(lane == 1.0, w1, 0.0))


def _mix(x2, g0, b0, ya, ym, ga, gm, wau, wmu, wout, g1, b1, wrh, wrl, br):
    n, d = x2.shape
    tm = ROW_TILE
    row = lambda w: pl.BlockSpec((tm, w), lambda i: (i, 0))
    in_specs = [row(d), _full(g0.shape), _full(b0.shape), row(ATTN_WIDTH), row(MLSTM_WIDTH), row(d), row(d),
                _full(wau.shape), _full(wmu.shape), _full(wout.shape), _full(g1.shape), _full(b1.shape),
                _full(wrh.shape), _full(wrl.shape), _full(br.shape)]
    out_shape = (jax.ShapeDtypeStruct((n, d), F32), jax.ShapeDtypeStruct((n, LANES), jnp.int32),
                 jax.ShapeDtypeStruct((n, LANES), F32), jax.ShapeDtypeStruct((1, LANES), F32))
    out_specs = (row(d), row(LANES), row(LANES), _full((1, LANES)))
    return pl.pallas_call(
        _mix_kernel, grid=(n // tm,), in_specs=in_specs, out_specs=out_specs, out_shape=out_shape,
        scratch_shapes=[pltpu.VMEM((1, LANES), F32)],
        compiler_params=_params("arbitrary"), name="mix",
    )(x2, g0, b0, ya, ym, ga, gm, wau, wmu, wout, g1, b1, wrh, wrl, br)


def _to_token_tiles(dst_ref, x):
    nch = x.shape[1] // LANES
    for c in range(nch):
        dst_ref[pl.ds(c, x.shape[0], stride=nch), :] = x[:, c * LANES:(c + 1) * LANES]


def _from_token_tiles(src_ref, rows, nch):
    return [src_ref[pl.ds(c, rows, stride=nch), :] for c in range(nch)]


def _token_copy(src, src_tok, dst, dst_tok, nch, sem):
    s0 = pl.multiple_of(src_tok * nch, nch)
    d0 = pl.multiple_of(dst_tok * nch, nch)
    return pltpu.make_async_copy(src.at[pl.ds(s0, nch), :], dst.at[pl.ds(d0, nch), :], sem)


def _slot(er_ref, ps_ref, r, k):
    return ps_ref[er_ref[4 * r + k]] + er_ref[4 * r + 2 + k]


def _dispatch_kernel(er_ref, ps_ref, last_ref, x_ref, xs_ref, scr_ref, zero_ref, sem, zsem):
    tm, d = x_ref.shape
    nch = d // LANES
    tb = zero_ref.shape[0] // nch

    @pl.when(pl.program_id(0) == 0)
    def _():
        zero_ref[...] = jnp.zeros_like(zero_ref)

        def desc(tok):
            off = pl.multiple_of(jnp.maximum(tok, 0) * nch, nch)
            return pltpu.make_async_copy(zero_ref, xs_ref.at[pl.ds(off, tb * nch), :], zsem)

        def zstart(e, _):
            @pl.when(last_ref[e] >= 0)
            def _():
                desc(last_ref[e]).start()
            return 0

        def zwait(e, _):
            @pl.when(last_ref[e] >= 0)
            def _():
                desc(last_ref[e]).wait()
            return 0

        lax.fori_loop(0, MOE_EXPERTS, zstart, 0)
        nused = last_ref[MOE_EXPERTS]
        nblk = xs_ref.shape[0] // (tb * nch)
        lax.fori_loop(nused, nblk, lambda b, _: (desc(b * tb).start(), 0)[1], 0)
        lax.fori_loop(0, MOE_EXPERTS, zwait, 0)
        lax.fori_loop(nused, nblk, lambda b, _: (desc(b * tb).wait(), 0)[1], 0)

    _to_token_tiles(scr_ref, x_ref[...])

    def start(r, _):
        for k in range(2):
            _token_copy(scr_ref, r, xs_ref, _slot(er_ref, ps_ref, r, k), nch, sem).start(priority=k)
        return 0

    def wait(r, _):
        for k in range(2):
            _token_copy(scr_ref, r, xs_ref, _slot(er_ref, ps_ref, r, k), nch, sem).wait()
        return 0

    lax.fori_loop(0, tm, start, 0, unroll=8)
    lax.fori_loop(0, tm, wait, 0, unroll=8)


def _dispatch(er, pad_start, last_blk, x1, n_rows):
    n, d = x1.shape
    tm = ROW_TILE
    nch = d // LANES
    smem = lambda: pl.BlockSpec(memory_space=pltpu.SMEM)
    return pl.pallas_call(
        _dispatch_kernel, grid=(n // tm,),
        in_specs=[pl.BlockSpec((4 * tm,), lambda i: (i,), memory_space=pltpu.SMEM), smem(), smem(),
                  pl.BlockSpec((tm, d), lambda i: (i, 0))],
        out_specs=pl.BlockSpec(memory_space=pl.ANY),
        out_shape=jax.ShapeDtypeStruct((n_rows * nch, LANES), F32),
        scratch_shapes=[pltpu.VMEM((tm * nch, LANES), F32), pltpu.VMEM((EXPERT_TILE * nch, LANES), F32),
                        pltpu.SemaphoreType.DMA(()), pltpu.SemaphoreType.DMA(())],
        compiler_params=_params("arbitrary"), name="dispatch",
    )(er, pad_start, last_blk, x1)


def _expert_kernel(first_ref, count_ref, widx_ref, nused_ref, wg_ref, wu_ref, wd_ref, xs_ref, ys_ref,
                   wgb_ref, wub_ref, wdb_ref, xbuf_ref, ybuf_ref, in_sem, out_sem):
    del widx_ref
    e = pl.program_id(0)
    nused = nused_ref[0]
    d = wg_ref.shape[1]
    nch = d // LANES
    rows = xbuf_ref.shape[1]
    tb = rows // nch
    nblk = xs_ref.shape[0] // rows

    def blk(ref, b):
        return ref.at[pl.ds(pl.multiple_of(b * rows, rows), rows), :]

    def in_copy(b, slot):
        return pltpu.make_async_copy(blk(xs_ref, b), xbuf_ref.at[slot], in_sem.at[slot])

    def out_copy(b, slot):
        return pltpu.make_async_copy(ybuf_ref.at[slot], blk(ys_ref, b), out_sem.at[slot])

    @pl.when((e == 0) & (nused > 0))
    def _():
        in_copy(0, 0).start()

    @pl.when(count_ref[e] > 0)
    def _():
        wgb_ref[...] = wg_ref[0].astype(BF16)
        wub_ref[...] = wu_ref[0].astype(BF16)
        wdb_ref[...] = wd_ref[0].astype(BF16)

    def body(b, _):
        slot = b % 2
        in_copy(b, slot).wait()

        @pl.when(b + 1 < nused)
        def _():
            in_copy(b + 1, 1 - slot).start()

        @pl.when(b >= 2)
        def _():
            out_copy(b - 2, slot).wait()

        xb = jnp.concatenate([c.astype(BF16) for c in _from_token_tiles(xbuf_ref.at[slot], tb, nch)], axis=1)
        g = _dot(xb, wgb_ref[...])
        u = _dot(xb, wub_ref[...])
        hmid = g * jax.nn.sigmoid(g) * u
        _to_token_tiles(ybuf_ref.at[slot], _dot(hmid.astype(BF16), wdb_ref[...]))
        out_copy(b, slot).start()
        return 0

    lax.fori_loop(first_ref[e], first_ref[e] + count_ref[e], body, 0)

    @pl.when(e == pl.num_programs(0) - 1)
    def _():
        for back in (2, 1):
            @pl.when(nused >= back)
            def _():
                out_copy(nused - back, (nused - back) % 2).wait()

        ybuf_ref[0] = jnp.zeros(ybuf_ref.shape[1:], F32)
        lax.fori_loop(nused, nblk, lambda b, _: (out_copy(b, 0).start(), 0)[1], 0)
        lax.fori_loop(nused, nblk, lambda b, _: (out_copy(b, 0).wait(), 0)[1], 0)


def _experts(first_blk, blk_count, w_idx, nused, xs, w_gate, w_up, w_down):
    n_exp, d, dff = w_gate.shape
    nch = d // LANES
    rows = EXPERT_TILE * nch
    w_spec = lambda shape: pl.BlockSpec(shape, lambda e, fb, bc, wi, nu: (wi[e], 0, 0))
    any_spec = pl.BlockSpec(memory_space=pl.ANY)
    grid_spec = pltpu.PrefetchScalarGridSpec(
        num_scalar_prefetch=4, grid=(n_exp,),
        in_specs=[w_spec((1, d, dff)), w_spec((1, d, dff)), w_spec((1, dff, d)), any_spec],
        out_specs=any_spec,
        scratch_shapes=[pltpu.VMEM((d, dff), BF16), pltpu.VMEM((d, dff), BF16), pltpu.VMEM((dff, d), BF16),
                        pltpu.VMEM((2, rows, LANES), F32), pltpu.VMEM((2, rows, LANES), F32),
                        pltpu.SemaphoreType.DMA((2,)), pltpu.SemaphoreType.DMA((2,))],
    )
    return pl.pallas_call(
        _expert_kernel, grid_spec=grid_spec, out_shape=jax.ShapeDtypeStruct(xs.shape, F32),
        compiler_params=_params("arbitrary"), name="experts",
    )(first_blk, blk_count, w_idx, nused, w_gate, w_up, w_down, xs)


def _combine_kernel(er_ref, ps_ref, x1_ref, rw_ref, g_ref, b_ref, ys_ref, o_ref, buf0_ref, buf1_ref, sem):
    tm, d = x1_ref.shape
    nch = d // LANES
    bufs = (buf0_ref, buf1_ref)

    def start(r, _):
        for k in range(2):
            _token_copy(ys_ref, _slot(er_ref, ps_ref, r, k), bufs[k], r, nch, sem).start(priority=k)
        return 0

    def wait(r, _):
        for k in range(2):
            _token_copy(ys_ref, _slot(er_ref, ps_ref, r, k), bufs[k], r, nch, sem).wait()
        return 0

    lax.fori_loop(0, tm, start, 0, unroll=8)
    lax.fori_loop(0, tm, wait, 0, unroll=8)
    rw = rw_ref[...]
    y0 = jnp.concatenate(_from_token_tiles(buf0_ref, tm, nch), axis=1)
    y1 = jnp.concatenate(_from_token_tiles(buf1_ref, tm, nch), axis=1)
    ffn = rw[:, 0:1] * y0 + rw[:, 1:2] * y1
    o_ref[...] = _layer_norm(DEEPNORM_ALPHA * x1_ref[...] + ffn, g_ref[...], b_ref[...])


def _combine(er, pad_start, x1, rw, ln_g, ln_b, ys):
    n, d = x1.shape
    tm = ROW_TILE
    nch = d // LANES
    row = lambda w: pl.BlockSpec((tm, w), lambda i: (i, 0))
    return pl.pallas_call(
        _combine_kernel, grid=(n // tm,),
        in_specs=[pl.BlockSpec((4 * tm,), lambda i: (i,), memory_space=pltpu.SMEM),
                  pl.BlockSpec(memory_space=pltpu.SMEM),
                  row(d), row(LANES), _full(ln_g.shape), _full(ln_b.shape),
                  pl.BlockSpec(memory_space=pl.ANY)],
        out_specs=row(d),
        out_shape=jax.ShapeDtypeStruct((n, d), F32),
        scratch_shapes=[pltpu.VMEM((tm * nch, LANES), F32), pltpu.VMEM((tm * nch, LANES), F32),
                        pltpu.SemaphoreType.DMA(())],
        compiler_params=_params("arbitrary"), name="combine",
    )(er, pad_start, x1, rw, ln_g, ln_b, ys)


def _rope_tables(seq):
    half = ATTN_HEAD_DIM // 2
    inv_freq = ROPE_THETA ** (-jnp.arange(half, dtype=F32) / half)
    ang = jnp.arange(seq, dtype=F32)[:, None] * inv_freq[None, :]
    cos = jnp.cos(ang)
    sin = jnp.sin(ang)
    cos_h = jnp.concatenate([cos, cos], axis=1)
    sin_h = jnp.concatenate([-sin, sin], axis=1)
    return jnp.tile(cos_h, (1, ATTN_HEADS)), jnp.tile(sin_h, (1, ATTN_HEADS))


def _pad_lanes(a, width=LANES):
    return jnp.pad(a, ((0, 0), (0, width - a.shape[1])))


def kernel(x, ln0_g, ln0_b, w_in, conv_w, conv_b, w_mq, w_mk, b_i, b_f, gn_g, skip, w_attn_up, w_mlstm_up, w_out,
           ln1_g, ln1_b, w_router_group, b_router_group, w_router_expert, b_router_expert, w_gate, w_up, w_down,
           ln2_g, ln2_b):
    batch, seq, d = x.shape
    n = batch * seq
    assert seq % ROW_TILE == 0 and ROW_TILE == MOBA_BLOCK and w_in.shape[0] == DEPTH
    x2 = x.reshape(n, d)
    vec = lambda a: a.reshape(1, -1).astype(F32)

    w = w_in[0]
    c_if = 3 * ATTN_WIDTH + 3 * MLSTM_WIDTH
    c_g = c_if + 2 * MLSTM_HEADS
    wqkv = w[:, :3 * ATTN_WIDTH].astype(BF16)
    wuvo = w[:, 3 * ATTN_WIDTH:c_if].astype(BF16)
    w_if = w[:, c_if:c_g]
    wif = _pad_lanes(w_if).astype(BF16)
    wift = w_if.T.astype(BF16)
    wg = w[:, c_g:].astype(BF16)
    cos, sin = _rope_tables(seq)

    q, k, v, kmean, u, vm, o, ifc, ift, ga, gm = _inproj(
        x2, vec(ln0_g), vec(ln0_b), wqkv, wuvo, wif, wift, wg, cos, sin, batch, seq)

    nb = seq // MOBA_BLOCK
    km = kmean.reshape(batch, nb, ATTN_HEADS, ATTN_HEAD_DIM).transpose(0, 2, 1, 3)
    ya = _moba(q, k, v, km).reshape(n, ATTN_WIDTH)

    b_if = jnp.concatenate([b_i[0], b_f[0]]).astype(F32)
    ym = _mlstm(u, vm, o, ifc, ift, conv_w[0], vec(conv_b[0]), w_mq[0].astype(BF16), w_mk[0].astype(BF16),
                _pad_lanes(b_if[None, :]), b_if[:, None], vec(gn_g[0]), vec(skip[0]), batch, seq)

    w_r = _pad_lanes(jnp.concatenate([w_router_expert[0], w_router_group[0]], axis=1))
    w_r_hi = w_r.astype(BF16)
    w_r_lo = (w_r - w_r_hi.astype(F32)).astype(BF16)
    b_r = _pad_lanes(jnp.concatenate([b_router_expert[0], b_router_group[0]])[None, :])
    x1, ri, rw, counts = _mix(
        x2, vec(ln0_g), vec(ln0_b), ya, ym, ga, gm, w_attn_up[0].astype(BF16), w_mlstm_up[0].astype(BF16),
        w_out[0].astype(BF16), vec(ln1_g[0]), vec(ln1_b[0]), w_r_hi, w_r_lo, b_r)

    tb = EXPERT_TILE
    nblk = (2 * n) // tb + MOE_EXPERTS
    cnt = counts[0, :MOE_EXPERTS].astype(jnp.int32)
    nblk_e = (cnt + tb - 1) // tb
    blk_end = jnp.cumsum(nblk_e)
    pad_start = (blk_end - nblk_e) * tb
    nused = blk_end[-1:]
    ids = jnp.arange(MOE_EXPERTS, dtype=jnp.int32)
    prev_used = jnp.max(jnp.where((ids[None, :] <= ids[:, None]) & (nblk_e[None, :] > 0), ids[None, :], -1), axis=1)
    first_used = jnp.min(jnp.where(nblk_e > 0, ids, MOE_EXPERTS - 1))
    w_idx = jnp.where(prev_used >= 0, prev_used, first_used).astype(jnp.int32)
    last_blk = jnp.where(nblk_e > 0, (blk_end - 1) * tb, -1)
    last_blk = jnp.concatenate([last_blk, nused]).astype(jnp.int32)
    er = ri[:, :4].reshape(4 * n)

    xs = _dispatch(er, pad_start, last_blk, x1, nblk * tb)
    ys = _experts((blk_end - nblk_e).astype(jnp.int32), nblk_e.astype(jnp.int32), w_idx, nused.astype(jnp.int32),
                  xs, w_gate[0], w_up[0], w_down[0])
    out = _combine(er, pad_start, x1, rw, vec(ln2_g[0]), vec(ln2_b[0]), ys)
    return out.reshape(batch, seq, d)
```

```python
import functools
import math

import jax
import jax.numpy as jnp
from jax import lax
from jax.experimental import pallas as pl
from jax.experimental.pallas import tpu as pltpu

F32 = jnp.float32
BF16 = jnp.bfloat16

ATTN_HEADS = 8
ATTN_HEAD_DIM = 64
ATTN_WIDTH = ATTN_HEADS * ATTN_HEAD_DIM
MOBA_BLOCK = 256
MOBA_TOPK = 3
ROPE_THETA = 10000.0
MLSTM_HEADS = 4
MLSTM_HEAD_DIM = 128
MLSTM_WIDTH = MLSTM_HEADS * MLSTM_HEAD_DIM
MLSTM_CONV = 4
MOE_GROUPS = 8
MOE_EXPERTS_PER_GROUP = 8
MOE_EXPERTS = MOE_GROUPS * MOE_EXPERTS_PER_GROUP
MOE_D_FF = 512
LN_EPS = 1e-5
GN_EPS = 1e-6
DEPTH = 1
DEEPNORM_ALPHA = (2 * DEPTH) ** 0.25

LANES = 128
SUBLANES = 8
ROW_TILE = 256
EXPERT_TILE = 256
EXPERT_IN_SLOTS = 4
VMEM_LIMIT = 48 * 1024 * 1024
LOG2_E = math.log2(math.e)

NEG_INF = float("-inf")


def _params(*sem):
    return pltpu.CompilerParams(dimension_semantics=sem, vmem_limit_bytes=VMEM_LIMIT)


def _dot(a, b):
    return jnp.dot(a, b, preferred_element_type=F32)


def _dot_nt(a, b):
    return lax.dot_general(a, b, (((1,), (1,)), ((), ())), preferred_element_type=F32)


def _dot_tn(a, b):
    return lax.dot_general(a, b, (((0,), (0,)), ((), ())), preferred_element_type=F32)


def _split3(x):
    x1 = x.astype(BF16)
    r1 = x - x1.astype(F32)
    x2 = r1.astype(BF16)
    r2 = r1 - x2.astype(F32)
    return x1, x2, r2.astype(BF16)


def _layer_norm(x, g, b):
    mu = jnp.mean(x, axis=-1, keepdims=True)
    xc = x - mu
    var = jnp.mean(xc * xc, axis=-1, keepdims=True)
    return xc * lax.rsqrt(var + LN_EPS) * g + b


def _log_sigmoid(x):
    return jnp.minimum(x, 0.0) - jnp.log1p(jnp.exp(-jnp.abs(x)))


def _full(shape):
    nd = len(shape)
    return pl.BlockSpec(shape, lambda *_: (0,) * nd)


def _inproj_kernel(x_ref, g_ref, b_ref, wqkv_ref, wuvo_ref, wif_ref, wift_ref, wg_ref, cos_ref, sin_ref,
                   q_ref, k_ref, v_ref, km_ref, u_ref, vm_ref, o_ref, ifc_ref, ift_ref, ga_ref, gm_ref):
    xn = _layer_norm(x_ref[...], g_ref[...], b_ref[...])
    xb = xn.astype(BF16)

    cos = cos_ref[...]
    sin = sin_ref[...]
    lane = lax.broadcasted_iota(jnp.int32, cos.shape, 1)
    first_half = (lane % ATTN_HEAD_DIM) < (ATTN_HEAD_DIM // 2)

    def rope(t):
        fwd = pltpu.roll(t, ATTN_WIDTH - ATTN_HEAD_DIM // 2, axis=1)
        bwd = pltpu.roll(t, ATTN_HEAD_DIM // 2, axis=1)
        return t * cos + jnp.where(first_half, fwd, bwd) * sin

    zqkv = _dot(xb, wqkv_ref[...])
    q = rope(zqkv[:, :ATTN_WIDTH]) * (ATTN_HEAD_DIM ** -0.5 * LOG2_E)
    k = rope(zqkv[:, ATTN_WIDTH:2 * ATTN_WIDTH])
    v = zqkv[:, 2 * ATTN_WIDTH:]
    km_ref[0] = jnp.mean(k, axis=0, keepdims=True)
    qt = q.T
    vt = v.T
    for h in range(ATTN_HEADS):
        sl = slice(h * ATTN_HEAD_DIM, (h + 1) * ATTN_HEAD_DIM)
        q_ref[0, h] = qt[sl, :].astype(BF16)
        k_ref[0, h] = k[:, sl].astype(BF16)
        v_ref[0, h] = vt[sl, :].astype(BF16)

    zuvo = _dot(xb, wuvo_ref[...])
    u_ref[...] = zuvo[:, :MLSTM_WIDTH]
    vm_ref[...] = zuvo[:, MLSTM_WIDTH:2 * MLSTM_WIDTH].astype(BF16)
    o_ref[...] = zuvo[:, 2 * MLSTM_WIDTH:]

    ifc_ref[...] = _dot(xb, wif_ref[...])
    ift_ref[...] = _dot_nt(wift_ref[...], xb)

    zg = _dot(xb, wg_ref[...])
    d = ga_ref.shape[1]
    ga_ref[...] = zg[:, :d]
    gm_ref[...] = zg[:, d:]


def _inproj(x2, ln_g, ln_b, wqkv, wuvo, wif, wift, wg, cos, sin, batch, seq):
    n, d = x2.shape
    tm = ROW_TILE
    nsb = seq // tm
    hd = ATTN_HEAD_DIM
    row = lambda w: pl.BlockSpec((tm, w), lambda i: (i, 0))
    head = pl.BlockSpec((1, ATTN_HEADS, tm, hd), lambda i: (i // nsb, 0, i % nsb, 0))
    head_t = pl.BlockSpec((1, ATTN_HEADS, hd, tm), lambda i: (i // nsb, 0, 0, i % nsb))
    tab = pl.BlockSpec((tm, ATTN_WIDTH), lambda i: (i % nsb, 0))
    head_shape = jax.ShapeDtypeStruct((batch, ATTN_HEADS, seq, hd), BF16)
    head_t_shape = jax.ShapeDtypeStruct((batch, ATTN_HEADS, hd, seq), BF16)
    out_shape = (
        head_t_shape, head_shape, head_t_shape,
        jax.ShapeDtypeStruct((n // tm, 1, ATTN_WIDTH), F32),
        jax.ShapeDtypeStruct((n, MLSTM_WIDTH), F32),
        jax.ShapeDtypeStruct((n, MLSTM_WIDTH), BF16),
        jax.ShapeDtypeStruct((n, MLSTM_WIDTH), F32),
        jax.ShapeDtypeStruct((n, LANES), F32),
        jax.ShapeDtypeStruct((SUBLANES, n), F32),
        jax.ShapeDtypeStruct((n, d), F32),
        jax.ShapeDtypeStruct((n, d), F32),
    )
    out_specs = (
        head_t, head, head_t,
        pl.BlockSpec((1, 1, ATTN_WIDTH), lambda i: (i, 0, 0)),
        row(MLSTM_WIDTH), row(MLSTM_WIDTH), row(MLSTM_WIDTH),
        row(LANES),
        pl.BlockSpec((SUBLANES, tm), lambda i: (0, i)),
        row(d), row(d),
    )
    in_specs = [row(d), _full(ln_g.shape), _full(ln_b.shape), _full(wqkv.shape), _full(wuvo.shape),
                _full(wif.shape), _full(wift.shape), _full(wg.shape), tab, tab]
    return pl.pallas_call(
        _inproj_kernel, grid=(n // tm,), in_specs=in_specs, out_specs=out_specs, out_shape=out_shape,
        compiler_params=_params("parallel"), name="inproj",
    )(x2, ln_g, ln_b, wqkv, wuvo, wif, wift, wg, cos, sin)


def _moba_kernel(qt_ref, k_ref, vt_ref, km_ref, o_ref, bias_ref, m_ref, l_ref, acc_ref, s_ref):
    i = pl.program_id(1)
    blk = MOBA_BLOCK
    hd = ATTN_HEAD_DIM
    heads = ATTN_HEADS
    nb = k_ref.shape[2] // blk
    blk_id = lax.broadcasted_iota(jnp.int32, (nb, blk), 0)
    key_pos = lax.broadcasted_iota(jnp.int32, (blk, blk), 0)
    qry_pos = lax.broadcasted_iota(jnp.int32, (blk, blk), 1)
    causal = key_pos <= qry_pos

    for h in range(heads):
        qt = qt_ref[0, h]
        km = km_ref[0, h]
        km_hi = km.astype(BF16)
        km_lo = (km - km_hi.astype(F32)).astype(BF16)
        gate = _dot(km_hi, qt) + _dot(km_lo, qt)
        gate = jnp.where(blk_id < i, gate, NEG_INF)
        for j in range(nb - 1):
            row = gate[j:j + 1, :]
            beats = (gate > row) | ((gate == row) & (blk_id < j))
            cnt = jnp.sum(jnp.where(beats, 1.0, 0.0), axis=0, keepdims=True)
            sel = (cnt < float(MOBA_TOPK)) & (row > NEG_INF)
            bias_ref[j * heads + h] = jnp.where(sel, 0.0, NEG_INF)
    for h in range(heads):
        bias_ref[i * heads + h] = jnp.zeros((1, blk), F32)

    def scores(h, j, own_block):
        qt = qt_ref[0, h]
        half = blk // 2
        m_tile = None
        for c in range(2):
            rows = slice(c * half, (c + 1) * half)
            s = _dot(k_ref[0, h, pl.ds(pl.multiple_of(j * blk + c * half, half), half), :], qt)
            if own_block:
                s = jnp.where(causal[rows], s, NEG_INF)
            s_ref[j * heads + h, rows, :] = s
            m_c = jnp.max(s, axis=0, keepdims=True)
            m_tile = m_c if m_tile is None else jnp.maximum(m_tile, m_c)
        return m_tile

    for h in range(heads):
        m_ref[h] = scores(h, i, True)

    def past_scores(j, _):
        for h in range(heads):
            m_ref[h] = jnp.maximum(m_ref[h], scores(h, j, False) + bias_ref[j * heads + h])
        return 0

    lax.fori_loop(0, i, past_scores, 0)

    l_ref[...] = jnp.zeros_like(l_ref)
    acc_ref[...] = jnp.zeros_like(acc_ref)

    def accumulate(j, _):
        off = pl.multiple_of(j * blk, blk)
        for h in range(heads):
            p = jnp.exp2(s_ref[j * heads + h] - (m_ref[h] - bias_ref[j * heads + h]))
            l_ref[h] += jnp.sum(p, axis=0, keepdims=True)
            acc_ref[h] += _dot(vt_ref[0, h, :, pl.ds(off, blk)], p.astype(BF16))
        return 0

    lax.fori_loop(0, i + 1, accumulate, 0)
    yt = acc_ref[...] / l_ref[...]
    o_ref[0] = yt.reshape(heads * hd, blk).T.astype(BF16)


def _moba(qt, k, vt, km):
    batch, heads, seq, hd = k.shape
    blk = MOBA_BLOCK
    nb = seq // blk
    return pl.pallas_call(
        _moba_kernel, grid=(batch, nb),
        in_specs=[
            pl.BlockSpec((1, heads, hd, blk), lambda b, i: (b, 0, 0, i)),
            pl.BlockSpec((1, heads, seq, hd), lambda b, i: (b, 0, 0, 0)),
            pl.BlockSpec((1, heads, hd, seq), lambda b, i: (b, 0, 0, 0)),
            pl.BlockSpec((1, heads, nb, hd), lambda b, i: (b, 0, 0, 0)),
        ],
        out_specs=pl.BlockSpec((1, blk, heads * hd), lambda b, i: (b, i, 0)),
        out_shape=jax.ShapeDtypeStruct((batch, seq, heads * hd), BF16),
        scratch_shapes=[pltpu.VMEM((nb * heads, 1, blk), F32), pltpu.VMEM((heads, 1, blk), F32),
                        pltpu.VMEM((heads, 1, blk), F32), pltpu.VMEM((heads, hd, blk), F32),
                        pltpu.VMEM((nb * heads, blk, blk), F32)],
        compiler_params=_params("parallel", "arbitrary"), name="moba",
    )(qt, k, vt, km)


def _mlstm_kernel(u_ref, vm_ref, o_ref, ifc_ref, ift_ref, cw_ref, cb_ref, wq_ref, wk_ref, brow_ref, bcol_ref,
                  gn_ref, skip_ref, y_ref, ext_ref, c_ref, n_ref, m_ref):
    tm = u_ref.shape[0]
    hd = MLSTM_HEAD_DIM
    halo = SUBLANES

    @pl.when(pl.program_id(1) == 0)
    def _():
        ext_ref[0:halo, :] = jnp.zeros((halo, MLSTM_WIDTH), F32)
        c_ref[...] = jnp.zeros_like(c_ref)
        n_ref[...] = jnp.zeros_like(n_ref)
        m_ref[...] = jnp.zeros_like(m_ref)

    u = u_ref[...]
    ext_ref[halo:halo + tm, :] = u
    acc = jnp.broadcast_to(cb_ref[...], u.shape)
    for j in range(MLSTM_CONV):
        acc = acc + cw_ref[j:j + 1, :] * ext_ref[halo - (MLSTM_CONV - 1) + j:halo - (MLSTM_CONV - 1) + j + tm, :]
    ext_ref[0:halo, :] = u[tm - halo:, :]
    uc = acc * jax.nn.sigmoid(acc)

    gc = ifc_ref[...] + brow_ref[...]
    gr = ift_ref[...] + bcol_ref[...]
    rows = lax.broadcasted_iota(jnp.int32, (tm, tm), 0)
    cols = lax.broadcasted_iota(jnp.int32, (tm, tm), 1)
    causal = cols <= rows
    tril = jnp.where(causal, 1.0, 0.0).astype(BF16)
    triu = jnp.where(rows <= cols, 1.0, 0.0).astype(BF16)
    c1, c2, c3 = _split3(_log_sigmoid(gc))
    bcum_c = _dot(tril, c1) + _dot(tril, c2) + _dot(tril, c3)
    r1, r2, r3 = _split3(_log_sigmoid(gr))
    bcum_r = _dot(r1, triu) + _dot(r2, triu) + _dot(r3, triu)

    for h in range(MLSTM_HEADS):
        hs = slice(h * hd, (h + 1) * hd)
        fl = MLSTM_HEADS + h
        bt = bcum_c[:, fl:fl + 1]
        ig_c = gc[:, h:h + 1]
        row_t = gr[h:h + 1, :] - bcum_r[fl:fl + 1, :]
        m_prev = m_ref[h][:, 0:1]
        dlog = jnp.where(causal, bt + row_t, NEG_INF)
        inter = bt + m_prev
        m_t = jnp.maximum(inter, jnp.max(dlog, axis=1, keepdims=True))
        w_intra = jnp.exp(dlog - m_t)
        w_inter = jnp.exp(inter - m_t)

        ucb = uc[:, hs].astype(BF16)
        q = _dot(ucb, wq_ref[h])
        k = _dot(ucb, wk_ref[h]) * (hd ** -0.5)
        qb = q.astype(BF16)
        kb = k.astype(BF16)
        vb = vm_ref[:, hs]
        s = _dot_nt(qb, kb) * w_intra
        c_prev = c_ref[h]
        n_prev = n_ref[h]
        num = w_inter * _dot(qb, c_prev.astype(BF16)) + _dot(s.astype(BF16), vb)
        den = w_inter * jnp.sum(q * n_prev, axis=1, keepdims=True) + jnp.sum(s, axis=1, keepdims=True)
        hh = num / jnp.maximum(jnp.abs(den), jnp.exp(-m_t))

        b_end = bt[tm - 1:tm, :]
        w_log = b_end - bt + ig_c
        m_new = jnp.maximum(b_end + m_prev, jnp.max(w_log, axis=0, keepdims=True))
        decay = jnp.exp(b_end + m_prev - m_new)
        kw = k * jnp.exp(w_log - m_new)
        c_ref[h] = decay * c_prev + _dot_tn(kw.astype(BF16), vb)
        n_ref[h] = decay * n_prev + jnp.sum(kw, axis=0, keepdims=True)
        m_ref[h] = jnp.broadcast_to(m_new, (1, LANES))

        hh = jax.nn.sigmoid(o_ref[:, hs]) * hh
        mu = jnp.mean(hh, axis=1, keepdims=True)
        hc = hh - mu
        var = jnp.mean(hc * hc, axis=1, keepdims=True)
        y = hc * lax.rsqrt(var + GN_EPS) * gn_ref[:, hs] + skip_ref[:, hs] * uc[:, hs]
        y_ref[:, hs] = y.astype(BF16)


def _mlstm(u, vm, o, ifc, ift, conv_w, conv_b, wq, wk, brow, bcol, gn_g, skip, batch, seq):
    n = u.shape[0]
    tm = ROW_TILE
    nc = seq // tm
    row = lambda w: pl.BlockSpec((tm, w), lambda b, c: (b * nc + c, 0))
    in_specs = [row(MLSTM_WIDTH), row(MLSTM_WIDTH), row(MLSTM_WIDTH), row(LANES),
                pl.BlockSpec((SUBLANES, tm), lambda b, c: (0, b * nc + c)),
                _full(conv_w.shape), _full(conv_b.shape), _full(wq.shape), _full(wk.shape),
                _full(brow.shape), _full(bcol.shape), _full(gn_g.shape), _full(skip.shape)]
    return pl.pallas_call(
        _mlstm_kernel, grid=(batch, nc), in_specs=in_specs, out_specs=row(MLSTM_WIDTH),
        out_shape=jax.ShapeDtypeStruct((n, MLSTM_WIDTH), BF16),
        scratch_shapes=[pltpu.VMEM((SUBLANES + tm, MLSTM_WIDTH), F32),
                        pltpu.VMEM((MLSTM_HEADS, MLSTM_HEAD_DIM, MLSTM_HEAD_DIM), F32),
                        pltpu.VMEM((MLSTM_HEADS, 1, MLSTM_HEAD_DIM), F32),
                        pltpu.VMEM((MLSTM_HEADS, 1, LANES), F32)],
        compiler_params=_params("parallel", "arbitrary"), name="mlstm",
    )(u, vm, o, ifc, ift, conv_w, conv_b, wq, wk, brow, bcol, gn_g, skip)


def _mix_kernel(x_ref, g0_ref, b0_ref, ya_ref, ym_ref, ga_ref, gm_ref, wau_ref, wmu_ref, wout_ref,
                g1_ref, b1_ref, wrh_ref, wrl_ref, br_ref,
                x1_ref, ri_ref, rw_ref, cnt_out_ref, cnt_ref):
    tm = x_ref.shape[0]

    @pl.when(pl.program_id(0) == 0)
    def _():
        cnt_ref[...] = jnp.zeros_like(cnt_ref)

    xn = _layer_norm(x_ref[...], g0_ref[...], b0_ref[...])
    a_up = _dot(ya_ref[...], wau_ref[...])
    m_up = _dot(ym_ref[...], wmu_ref[...])
    mix = jax.nn.sigmoid(ga_ref[...]) * a_up + jax.nn.sigmoid(gm_ref[...]) * m_up
    x1 = _layer_norm(DEEPNORM_ALPHA * xn + _dot(mix.astype(BF16), wout_ref[...]), g1_ref[...], b1_ref[...])
    x1_ref[...] = x1

    x_hi = x1.astype(BF16)
    x_lo = (x1 - x_hi.astype(F32)).astype(BF16)
    w_hi = wrh_ref[...]
    logits = _dot(x_hi, w_hi) + _dot(x_lo, w_hi) + _dot(x_hi, wrl_ref[...]) + br_ref[...]
    lane = lax.broadcasted_iota(jnp.int32, (tm, LANES), 1).astype(F32)
    big = float(4 * LANES)
    is_g = (lane >= float(MOE_EXPERTS)) & (lane < float(MOE_EXPERTS + MOE_GROUPS))
    gl = jnp.where(is_g, logits, NEG_INF)
    ge = jnp.exp(gl - jnp.max(gl, axis=1, keepdims=True))
    gp = ge / jnp.sum(ge, axis=1, keepdims=True)
    g_w = jnp.max(gp, axis=1, keepdims=True)
    g_idx = jnp.min(jnp.where((gp == g_w) & is_g, lane - float(MOE_EXPERTS), big), axis=1, keepdims=True)
    lo = g_idx * float(MOE_EXPERTS_PER_GROUP)
    in_grp = (lane >= lo) & (lane < lo + float(MOE_EXPERTS_PER_GROUP))
    el = jnp.where(in_grp, logits, NEG_INF)
    v1 = jnp.max(el, axis=1, keepdims=True)
    i1 = jnp.min(jnp.where((el == v1) & in_grp, lane, big), axis=1, keepdims=True)
    el2 = jnp.where(lane == i1, NEG_INF, el)
    v2 = jnp.max(el2, axis=1, keepdims=True)
    i2 = jnp.min(jnp.where((el2 == v2) & in_grp & (lane != i1), lane, big), axis=1, keepdims=True)
    e2 = jnp.exp(v2 - v1)
    w0 = g_w / (1.0 + e2)
    w1 = g_w * e2 / (1.0 + e2)

    is1 = lane == i1
    is2 = lane == i2
    onehot = jnp.where(is1 | is2, 1.0, 0.0)
    rows = lax.broadcasted_iota(jnp.int32, (tm, tm), 0)
    cols = lax.broadcasted_iota(jnp.int32, (tm, tm), 1)
    strict = jnp.where(cols < rows, 1.0, 0.0).astype(BF16)
    before = _dot(strict, onehot.astype(BF16)) + cnt_ref[...]
    r0 = jnp.sum(jnp.where(is1, before, 0.0), axis=1, keepdims=True)
    r1 = jnp.sum(jnp.where(is2, before, 0.0), axis=1, keepdims=True)
    total = cnt_ref[...] + jnp.sum(onehot, axis=0, keepdims=True)
    cnt_ref[...] = total
    cnt_out_ref[...] = total

    ri = jnp.where(lane == 0.0, i1, jnp.where(lane == 1.0, i2, jnp.where(lane == 2.0, r0, jnp.where(lane == 3.0, r1, 0.0))))
    ri_ref[...] = ri.astype(jnp.int32)
    rw_ref[...] = jnp.where(lane == 0.0, w0, jnp.where(lane == 1.0, w1, 0.0))


def _mix(x2, g0, b0, ya, ym, ga, gm, wau, wmu, wout, g1, b1, wrh, wrl, br):
    n, d = x2.shape
    tm = ROW_TILE
    row = lambda w: pl.BlockSpec((tm, w), lambda i: (i, 0))
    in_specs = [row(d), _full(g0.shape), _full(b0.shape), row(ATTN_WIDTH), row(MLSTM_WIDTH), row(d), row(d),
                _full(wau.shape), _full(wmu.shape), _full(wout.shape), _full(g1.shape), _full(b1.shape),
                _full(wrh.shape), _full(wrl.shape), _full(br.shape)]
    out_shape = (jax.ShapeDtypeStruct((n, d), F32), jax.ShapeDtypeStruct((n, LANES), jnp.int32),
                 jax.ShapeDtypeStruct((n, LANES), F32), jax.ShapeDtypeStruct((1, LANES), F32))
    out_specs = (row(d), row(LANES), row(LANES), _full((1, LANES)))
    return pl.pallas_call(
        _mix_kernel, grid=(n // tm,), in_specs=in_specs, out_specs=out_specs, out_shape=out_shape,
        scratch_shapes=[pltpu.VMEM((1, LANES), F32)],
        compiler_params=_params("arbitrary"), name="mix",
    )(x2, g0, b0, ya, ym, ga, gm, wau, wmu, wout, g1, b1, wrh, wrl, br)


def _to_token_tiles(dst_ref, x):
    nch = x.shape[1] // LANES
    for c in range(nch):
        dst_ref[pl.ds(c, x.shape[0], stride=nch), :] = x[:, c * LANES:(c + 1) * LANES]


def _from_token_tiles(src_ref, rows, nch):
    return [src_ref[pl.ds(c, rows, stride=nch), :] for c in range(nch)]


def _token_copy(src, src_tok, dst, dst_tok, nch, sem):
    s0 = pl.multiple_of(src_tok * nch, nch)
    d0 = pl.multiple_of(dst_tok * nch, nch)
    return pltpu.make_async_copy(src.at[pl.ds(s0, nch), :], dst.at[pl.ds(d0, nch), :], sem)


def _slot(er_ref, ps_ref, r, k):
    return ps_ref[er_ref[4 * r + k]] + er_ref[4 * r + 2 + k]


def _dispatch_kernel(er_ref, ps_ref, last_ref, x_ref, xs_ref, scr_ref, zero_ref, sem, zsem):
    tm, d = x_ref.shape
    nch = d // LANES
    tb = zero_ref.shape[0] // nch

    @pl.when(pl.program_id(0) == 0)
    def _():
        zero_ref[...] = jnp.zeros_like(zero_ref)

        def desc(tok):
            off = pl.multiple_of(jnp.maximum(tok, 0) * nch, nch)
            return pltpu.make_async_copy(zero_ref, xs_ref.at[pl.ds(off, tb * nch), :], zsem)

        def zstart(e, _):
            @pl.when(last_ref[e] >= 0)
            def _():
                desc(last_ref[e]).start()
            return 0

        def zwait(e, _):
            @pl.when(last_ref[e] >= 0)
            def _():
                desc(last_ref[e]).wait()
            return 0

        lax.fori_loop(0, MOE_EXPERTS, zstart, 0)
        nused = last_ref[MOE_EXPERTS]
        nblk = xs_ref.shape[0] // (tb * nch)
        lax.fori_loop(nused, nblk, lambda b, _: (desc(b * tb).start(), 0)[1], 0)
        lax.fori_loop(0, MOE_EXPERTS, zwait, 0)
        lax.fori_loop(nused, nblk, lambda b, _: (desc(b * tb).wait(), 0)[1], 0)

    step = pl.program_id(0)
    slot = step % 2
    scr = scr_ref.at[slot]
    _to_token_tiles(scr, x_ref[...])

    def start(r, _):
        for k in range(2):
            _token_copy(scr, r, xs_ref, _slot(er_ref, ps_ref, r, k), nch, sem.at[slot]).start(priority=k)
        return 0

    def drain(which):
        def wait(r, _):
            for k in range(2):
                _token_copy(scr_ref.at[which], 0, xs_ref, 0, nch, sem.at[which]).wait()
            return 0
        lax.fori_loop(0, tm, wait, 0, unroll=8)

    lax.fori_loop(0, tm, start, 0, unroll=8)

    @pl.when(step > 0)
    def _():
        drain(1 - slot)

    @pl.when(step == pl.num_programs(0) - 1)
    def _():
        drain(slot)


def _dispatch(er, pad_start, last_blk, x1, n_rows):
    n, d = x1.shape
    tm = ROW_TILE
    nch = d // LANES
    smem = lambda: pl.BlockSpec(memory_space=pltpu.SMEM)
    return pl.pallas_call(
        _dispatch_kernel, grid=(n // tm,),
        in_specs=[pl.BlockSpec((4 * tm,), lambda i: (i,), memory_space=pltpu.SMEM), smem(), smem(),
                  pl.BlockSpec((tm, d), lambda i: (i, 0))],
        out_specs=pl.BlockSpec(memory_space=pl.ANY),
        out_shape=jax.ShapeDtypeStruct((n_rows * nch, LANES), F32),
        scratch_shapes=[pltpu.VMEM((2, tm * nch, LANES), F32), pltpu.VMEM((EXPERT_TILE * nch, LANES), F32),
                        pltpu.SemaphoreType.DMA((2,)), pltpu.SemaphoreType.DMA(())],
        compiler_params=_params("arbitrary"), name="dispatch",
    )(er, pad_start, last_blk, x1)


def _expert_kernel(first_ref, count_ref, widx_ref, nused_ref, wg_ref, wu_ref, wd_ref, xs_ref, ys_ref,
                   wgb_ref, wub_ref, wdb_ref, xbuf_ref, ybuf_ref, in_sem, out_sem):
    del widx_ref
    e = pl.program_id(0)
    nused = nused_ref[0]
    d = wg_ref.shape[1]
    nch = d // LANES
    rows = xbuf_ref.shape[1]
    tb = rows // nch
    nblk = xs_ref.shape[0] // rows

    def blk(ref, b):
        return ref.at[pl.ds(pl.multiple_of(b * rows, rows), rows), :]

    def in_copy(b, slot):
        return pltpu.make_async_copy(blk(xs_ref, b), xbuf_ref.at[slot], in_sem.at[slot])

    def out_copy(b, slot):
        return pltpu.make_async_copy(ybuf_ref.at[slot], blk(ys_ref, b), out_sem.at[slot])

    n_in = xbuf_ref.shape[0]

    @pl.when(e == 0)
    def _():
        for b0 in range(n_in - 1):
            @pl.when(b0 < nused)
            def _():
                in_copy(b0, b0).start()

    @pl.when(count_ref[e] > 0)
    def _():
        wgb_ref[...] = wg_ref[0].astype(BF16)
        wub_ref[...] = wu_ref[0].astype(BF16)
        wdb_ref[...] = wd_ref[0].astype(BF16)

    def body(b, _):
        slot = b % n_in
        oslot = b % 2
        in_copy(b, slot).wait()

        @pl.when(b + n_in - 1 < nused)
        def _():
            in_copy(b + n_in - 1, (b + n_in - 1) % n_in).start()

        @pl.when(b >= 2)
        def _():
            out_copy(b - 2, oslot).wait()

        xb = jnp.concatenate([c.astype(BF16) for c in _from_token_tiles(xbuf_ref.at[slot], tb, nch)], axis=1)
        g = _dot(xb, wgb_ref[...])
        u = _dot(xb, wub_ref[...])
        hmid = g * jax.nn.sigmoid(g) * u
        _to_token_tiles(ybuf_ref.at[oslot], _dot(hmid.astype(BF16), wdb_ref[...]))
        out_copy(b, oslot).start()
        return 0

    lax.fori_loop(first_ref[e], first_ref[e] + count_ref[e], body, 0)

    @pl.when(e == pl.num_programs(0) - 1)
    def _():
        for back in (2, 1):
            @pl.when(nused >= back)
            def _():
                out_copy(nused - back, (nused - back) % 2).wait()

        ybuf_ref[0] = jnp.zeros(ybuf_ref.shape[1:], F32)
        lax.fori_loop(nused, nblk, lambda b, _: (out_copy(b, 0).start(), 0)[1], 0)
        lax.fori_loop(nused, nblk, lambda b, _: (out_copy(b, 0).wait(), 0)[1], 0)


def _experts(first_blk, blk_count, w_idx, nused, xs, w_gate, w_up, w_down):
    n_exp, d, dff = w_gate.shape
    nch = d // LANES
    rows = EXPERT_TILE * nch
    w_spec = lambda shape: pl.BlockSpec(shape, lambda e, fb, bc, wi, nu: (wi[e], 0, 0))
    any_spec = pl.BlockSpec(memory_space=pl.ANY)
    grid_spec = pltpu.PrefetchScalarGridSpec(
        num_scalar_prefetch=4, grid=(n_exp,),
        in_specs=[w_spec((1, d, dff)), w_spec((1, d, dff)), w_spec((1, dff, d)), any_spec],
        out_specs=any_spec,
        scratch_shapes=[pltpu.VMEM((d, dff), BF16), pltpu.VMEM((d, dff), BF16), pltpu.VMEM((dff, d), BF16),
                        pltpu.VMEM((EXPERT_IN_SLOTS, rows, LANES), F32), pltpu.VMEM((2, rows, LANES), F32),
                        pltpu.SemaphoreType.DMA((EXPERT_IN_SLOTS,)), pltpu.SemaphoreType.DMA((2,))],
    )
    return pl.pallas_call(
        _expert_kernel, grid_spec=grid_spec, out_shape=jax.ShapeDtypeStruct(xs.shape, F32),
        compiler_params=_params("arbitrary"), name="experts",
    )(first_blk, blk_count, w_idx, nused, w_gate, w_up, w_down, xs)


def _combine_kernel(er_ref, er_next_ref, ps_ref, x1_ref, rw_ref, g_ref, b_ref, ys_ref, o_ref, buf_ref, sem):
    tm, d = x1_ref.shape
    nch = d // LANES
    step = pl.program_id(0)
    slot = step % 2

    def gather(idx_ref, which):
        def start(r, _):
            for k in range(2):
                _token_copy(ys_ref, _slot(idx_ref, ps_ref, r, k), buf_ref.at[which, k], r, nch,
                            sem.at[which]).start(priority=k)
            return 0
        lax.fori_loop(0, tm, start, 0, unroll=8)

    @pl.when(step == 0)
    def _():
        gather(er_ref, 0)

    @pl.when(step + 1 < pl.num_programs(0))
    def _():
        gather(er_next_ref, 1 - slot)

    def wait(r, _):
        for k in range(2):
            _token_copy(ys_ref, 0, buf_ref.at[slot, k], 0, nch, sem.at[slot]).wait()
        return 0

    lax.fori_loop(0, tm, wait, 0, unroll=8)
    rw = rw_ref[...]
    y0 = jnp.concatenate(_from_token_tiles(buf_ref.at[slot, 0], tm, nch), axis=1)
    y1 = jnp.concatenate(_from_token_tiles(buf_ref.at[slot, 1], tm, nch), axis=1)
    ffn = rw[:, 0:1] * y0 + rw[:, 1:2] * y1
    o_ref[...] = _layer_norm(DEEPNORM_ALPHA * x1_ref[...] + ffn, g_ref[...], b_ref[...])


def _combine(er, pad_start, x1, rw, ln_g, ln_b, ys):
    n, d = x1.shape
    tm = ROW_TILE
    nch = d // LANES
    last = n // tm - 1
    row = lambda w: pl.BlockSpec((tm, w), lambda i: (i, 0))
    return pl.pallas_call(
        _combine_kernel, grid=(n // tm,),
        in_specs=[pl.BlockSpec((4 * tm,), lambda i: (i,), memory_space=pltpu.SMEM),
                  pl.BlockSpec((4 * tm,), lambda i: (jnp.minimum(i + 1, last),), memory_space=pltpu.SMEM),
                  pl.BlockSpec(memory_space=pltpu.SMEM),
                  row(d), row(LANES), _full(ln_g.shape), _full(ln_b.shape),
                  pl.BlockSpec(memory_space=pl.ANY)],
        out_specs=row(d),
        out_shape=jax.ShapeDtypeStruct((n, d), F32),
        scratch_shapes=[pltpu.VMEM((2, 2, tm * nch, LANES), F32), pltpu.SemaphoreType.DMA((2,))],
        compiler_params=_params("arbitrary"), name="combine",
    )(er, er, pad_start, x1, rw, ln_g, ln_b, ys)


def _rope_tables(seq):
    half = ATTN_HEAD_DIM // 2
    inv_freq = ROPE_THETA ** (-jnp.arange(half, dtype=F32) / half)
    ang = jnp.arange(seq, dtype=F32)[:, None] * inv_freq[None, :]
    cos = jnp.cos(ang)
    sin = jnp.sin(ang)
    cos_h = jnp.concatenate([cos, cos], axis=1)
    sin_h = jnp.concatenate([-sin, sin], axis=1)
    return jnp.tile(cos_h, (1, ATTN_HEADS)), jnp.tile(sin_h, (1, ATTN_HEADS))


def _pad_lanes(a, width=LANES):
    return jnp.pad(a, ((0, 0), (0, width - a.shape[1])))


def kernel(x, ln0_g, ln0_b, w_in, conv_w, conv_b, w_mq, w_mk, b_i, b_f, gn_g, skip, w_attn_up, w_mlstm_up, w_out,
           ln1_g, ln1_b, w_router_group, b_router_group, w_router_expert, b_router_expert, w_gate, w_up, w_down,
           ln2_g, ln2_b):
    batch, seq, d = x.shape
    n = batch * seq
    assert seq % ROW_TILE == 0 and ROW_TILE == MOBA_BLOCK and w_in.shape[0] == DEPTH
    x2 = x.reshape(n, d)
    vec = lambda a: a.reshape(1, -1).astype(F32)

    w = w_in[0]
    c_if = 3 * ATTN_WIDTH + 3 * MLSTM_WIDTH
    c_g = c_if + 2 * MLSTM_HEADS
    wqkv = w[:, :3 * ATTN_WIDTH].astype(BF16)
    wuvo = w[:, 3 * ATTN_WIDTH:c_if].astype(BF16)
    w_if = w[:, c_if:c_g]
    wif = _pad_lanes(w_if).astype(BF16)
    wift = w_if.T.astype(BF16)
    wg = w[:, c_g:].astype(BF16)
    cos, sin = _rope_tables(seq)

    q, k, v, kmean, u, vm, o, ifc, ift, ga, gm = _inproj(
        x2, vec(ln0_g), vec(ln0_b), wqkv, wuvo, wif, wift, wg, cos, sin, batch, seq)

    nb = seq // MOBA_BLOCK
    km = kmean.reshape(batch, nb, ATTN_HEADS, ATTN_HEAD_DIM).transpose(0, 2, 1, 3)
    ya = _moba(q, k, v, km).reshape(n, ATTN_WIDTH)

    b_if = jnp.concatenate([b_i[0], b_f[0]]).astype(F32)
    ym = _mlstm(u, vm, o, ifc, ift, conv_w[0], vec(conv_b[0]), w_mq[0].astype(BF16), w_mk[0].astype(BF16),
                _pad_lanes(b_if[None, :]), b_if[:, None], vec(gn_g[0]), vec(skip[0]), batch, seq)

    w_r = _pad_lanes(jnp.concatenate([w_router_expert[0], w_router_group[0]], axis=1))
    w_r_hi = w_r.astype(BF16)
    w_r_lo = (w_r - w_r_hi.astype(F32)).astype(BF16)
    b_r = _pad_lanes(jnp.concatenate([b_router_expert[0], b_router_group[0]])[None, :])
    x1, ri, rw, counts = _mix(
        x2, vec(ln0_g), vec(ln0_b), ya, ym, ga, gm, w_attn_up[0].astype(BF16), w_mlstm_up[0].astype(BF16),
        w_out[0].astype(BF16), vec(ln1_g[0]), vec(ln1_b[0]), w_r_hi, w_r_lo, b_r)

    tb = EXPERT_TILE
    nblk = (2 * n) // tb + MOE_EXPERTS
    cnt = counts[0, :MOE_EXPERTS].astype(jnp.int32)
    nblk_e = (cnt + tb - 1) // tb
    blk_end = jnp.cumsum(nblk_e)
    pad_start = (blk_end - nblk_e) * tb
    nused = blk_end[-1:]
    ids = jnp.arange(MOE_EXPERTS, dtype=jnp.int32)
    prev_used = jnp.max(jnp.where((ids[None, :] <= ids[:, None]) & (nblk_e[None, :] > 0), ids[None, :], -1), axis=1)
    first_used = jnp.min(jnp.where(nblk_e > 0, ids, MOE_EXPERTS - 1))
    w_idx = jnp.where(prev_used >= 0, prev_used, first_used).astype(jnp.int32)
    last_blk = jnp.where(nblk_e > 0, (blk_end - 1) * tb, -1)
    last_blk = jnp.concatenate([last_blk, nused]).astype(jnp.int32)
    er = ri[:, :4].reshape(4 * n)

    xs = _dispatch(er, pad_start, last_blk, x1, nblk * tb)
    ys = _experts((blk_end - nblk_e).astype(jnp.int32), nblk_e.astype(jnp.int32), w_idx, nused.astype(jnp.int32),
                  xs, w_gate[0], w_up[0], w_down[0])
    out = _combine(er, pad_start, x1, rw, vec(ln2_g[0]), vec(ln2_b[0]), ys)
    return out.reshape(batch, seq, d)
```

```python
import functools
import math

import jax
import jax.numpy as jnp
from jax import lax
from jax.experimental import pallas as pl
from jax.experimental.pallas import tpu as pltpu

F32 = jnp.float32
BF16 = jnp.bfloat16

ATTN_HEADS = 8
ATTN_HEAD_DIM = 64
ATTN_WIDTH = ATTN_HEADS * ATTN_HEAD_DIM
MOBA_BLOCK = 256
MOBA_TOPK = 3
ROPE_THETA = 10000.0
MLSTM_HEADS = 4
MLSTM_HEAD_DIM = 128
MLSTM_WIDTH = MLSTM_HEADS * MLSTM_HEAD_DIM
MLSTM_CONV = 4
MOE_GROUPS = 8
MOE_EXPERTS_PER_GROUP = 8
MOE_EXPERTS = MOE_GROUPS * MOE_EXPERTS_PER_GROUP
MOE_D_FF = 512
LN_EPS = 1e-5
GN_EPS = 1e-6
DEPTH = 1
DEEPNORM_ALPHA = (2 * DEPTH) ** 0.25

LANES = 128
SUBLANES = 8
ROW_TILE = 256
EXPERT_TILE = 256
EXPERT_IN_SLOTS = 4
VMEM_LIMIT = 48 * 1024 * 1024
LOG2_E = math.log2(math.e)

NEG_INF = float("-inf")


def _params(*sem):
    return pltpu.CompilerParams(dimension_semantics=sem, vmem_limit_bytes=VMEM_LIMIT)


def _dot(a, b):
    return jnp.dot(a, b, preferred_element_type=F32)


def _dot_nt(a, b):
    return lax.dot_general(a, b, (((1,), (1,)), ((), ())), preferred_element_type=F32)


def _dot_tn(a, b):
    return lax.dot_general(a, b, (((0,), (0,)), ((), ())), preferred_element_type=F32)


def _split3(x):
    x1 = x.astype(BF16)
    r1 = x - x1.astype(F32)
    x2 = r1.astype(BF16)
    r2 = r1 - x2.astype(F32)
    return x1, x2, r2.astype(BF16)


def _layer_norm(x, g, b):
    mu = jnp.mean(x, axis=-1, keepdims=True)
    xc = x - mu
    var = jnp.mean(xc * xc, axis=-1, keepdims=True)
    return xc * lax.rsqrt(var + LN_EPS) * g + b


def _log_sigmoid(x):
    return jnp.minimum(x, 0.0) - jnp.log1p(jnp.exp(-jnp.abs(x)))


def _full(shape):
    nd = len(shape)
    return pl.BlockSpec(shape, lambda *_: (0,) * nd)


def _inproj_kernel(x_ref, g_ref, b_ref, wqkv_ref, wuvo_ref, wif_ref, wift_ref, wg_ref, cos_ref, sin_ref,
                   q_ref, k_ref, v_ref, km_ref, u_ref, vm_ref, o_ref, ifc_ref, ift_ref, ga_ref, gm_ref):
    xn = _layer_norm(x_ref[...], g_ref[...], b_ref[...])
    xb = xn.astype(BF16)

    cos = cos_ref[...]
    sin = sin_ref[...]
    lane = lax.broadcasted_iota(jnp.int32, cos.shape, 1)
    first_half = (lane % ATTN_HEAD_DIM) < (ATTN_HEAD_DIM // 2)

    def rope(t):
        fwd = pltpu.roll(t, ATTN_WIDTH - ATTN_HEAD_DIM // 2, axis=1)
        bwd = pltpu.roll(t, ATTN_HEAD_DIM // 2, axis=1)
        return t * cos + jnp.where(first_half, fwd, bwd) * sin

    zqkv = _dot(xb, wqkv_ref[...])
    q = rope(zqkv[:, :ATTN_WIDTH]) * (ATTN_HEAD_DIM ** -0.5 * LOG2_E)
    k = rope(zqkv[:, ATTN_WIDTH:2 * ATTN_WIDTH])
    v = zqkv[:, 2 * ATTN_WIDTH:]
    km_ref[0] = jnp.mean(k, axis=0, keepdims=True)
    qt = q.T
    vt = v.T
    for h in range(ATTN_HEADS):
        sl = slice(h * ATTN_HEAD_DIM, (h + 1) * ATTN_HEAD_DIM)
        q_ref[0, h] = qt[sl, :].astype(BF16)
        k_ref[0, h] = k[:, sl].astype(BF16)
        v_ref[0, h] = vt[sl, :].astype(BF16)

    zuvo = _dot(xb, wuvo_ref[...])
    u_ref[...] = zuvo[:, :MLSTM_WIDTH]
    vm_ref[...] = zuvo[:, MLSTM_WIDTH:2 * MLSTM_WIDTH].T.astype(BF16)
    o_ref[...] = zuvo[:, 2 * MLSTM_WIDTH:].T

    ifc_ref[...] = _dot(xb, wif_ref[...])
    ift_ref[...] = _dot_nt(wift_ref[...], xb)

    zg = _dot(xb, wg_ref[...])
    d = ga_ref.shape[1]
    ga_ref[...] = zg[:, :d]
    gm_ref[...] = zg[:, d:]


def _inproj(x2, ln_g, ln_b, wqkv, wuvo, wif, wift, wg, cos, sin, batch, seq):
    n, d = x2.shape
    tm = ROW_TILE
    nsb = seq // tm
    hd = ATTN_HEAD_DIM
    row = lambda w: pl.BlockSpec((tm, w), lambda i: (i, 0))
    col = lambda h: pl.BlockSpec((h, tm), lambda i: (0, i))
    head = pl.BlockSpec((1, ATTN_HEADS, tm, hd), lambda i: (i // nsb, 0, i % nsb, 0))
    head_t = pl.BlockSpec((1, ATTN_HEADS, hd, tm), lambda i: (i // nsb, 0, 0, i % nsb))
    tab = pl.BlockSpec((tm, ATTN_WIDTH), lambda i: (i % nsb, 0))
    head_shape = jax.ShapeDtypeStruct((batch, ATTN_HEADS, seq, hd), BF16)
    head_t_shape = jax.ShapeDtypeStruct((batch, ATTN_HEADS, hd, seq), BF16)
    out_shape = (
        head_t_shape, head_shape, head_t_shape,
        jax.ShapeDtypeStruct((n // tm, 1, ATTN_WIDTH), F32),
        jax.ShapeDtypeStruct((n, MLSTM_WIDTH), F32),
        jax.ShapeDtypeStruct((MLSTM_WIDTH, n), BF16),
        jax.ShapeDtypeStruct((MLSTM_WIDTH, n), F32),
        jax.ShapeDtypeStruct((n, LANES), F32),
        jax.ShapeDtypeStruct((SUBLANES, n), F32),
        jax.ShapeDtypeStruct((n, d), F32),
        jax.ShapeDtypeStruct((n, d), F32),
    )
    out_specs = (
        head_t, head, head_t,
        pl.BlockSpec((1, 1, ATTN_WIDTH), lambda i: (i, 0, 0)),
        row(MLSTM_WIDTH), col(MLSTM_WIDTH), col(MLSTM_WIDTH),
        row(LANES),
        col(SUBLANES),
        row(d), row(d),
    )
    in_specs = [row(d), _full(ln_g.shape), _full(ln_b.shape), _full(wqkv.shape), _full(wuvo.shape),
                _full(wif.shape), _full(wift.shape), _full(wg.shape), tab, tab]
    return pl.pallas_call(
        _inproj_kernel, grid=(n // tm,), in_specs=in_specs, out_specs=out_specs, out_shape=out_shape,
        compiler_params=_params("parallel"), name="inproj",
    )(x2, ln_g, ln_b, wqkv, wuvo, wif, wift, wg, cos, sin)


def _moba_kernel(qt_ref, k_ref, vt_ref, km_ref, o_ref, bias_ref, m_ref, l_ref, acc_ref, s_ref):
    i = pl.program_id(1)
    blk = MOBA_BLOCK
    hd = ATTN_HEAD_DIM
    heads = ATTN_HEADS
    nb = k_ref.shape[2] // blk
    blk_id = lax.broadcasted_iota(jnp.int32, (nb, blk), 0)
    key_pos = lax.broadcasted_iota(jnp.int32, (blk, blk), 0)
    qry_pos = lax.broadcasted_iota(jnp.int32, (blk, blk), 1)
    causal = key_pos <= qry_pos

    for h in range(heads):
        qt = qt_ref[0, h]
        km = km_ref[0, h]
        km_hi = km.astype(BF16)
        km_lo = (km - km_hi.astype(F32)).astype(BF16)
        gate = _dot(km_hi, qt) + _dot(km_lo, qt)
        gate = jnp.where(blk_id < i, gate, NEG_INF)
        for j in range(nb - 1):
            row = gate[j:j + 1, :]
            beats = (gate > row) | ((gate == row) & (blk_id < j))
            cnt = jnp.sum(jnp.where(beats, 1.0, 0.0), axis=0, keepdims=True)
            sel = (cnt < float(MOBA_TOPK)) & (row > NEG_INF)
            bias_ref[j * heads + h] = jnp.where(sel, 0.0, NEG_INF)
    for h in range(heads):
        bias_ref[i * heads + h] = jnp.zeros((1, blk), F32)

    def scores(h, j, own_block):
        qt = qt_ref[0, h]
        half = blk // 2
        m_tile = None
        for c in range(2):
            rows = slice(c * half, (c + 1) * half)
            s = _dot(k_ref[0, h, pl.ds(pl.multiple_of(j * blk + c * half, half), half), :], qt)
            if own_block:
                s = jnp.where(causal[rows], s, NEG_INF)
            s_ref[j * heads + h, rows, :] = s
            m_c = jnp.max(s, axis=0, keepdims=True)
            m_tile = m_c if m_tile is None else jnp.maximum(m_tile, m_c)
        return m_tile

    for h in range(heads):
        m_ref[h] = scores(h, i, True)

    def past_scores(j, _):
        for h in range(heads):
            m_ref[h] = jnp.maximum(m_ref[h], scores(h, j, False) + bias_ref[j * heads + h])
        return 0

    lax.fori_loop(0, i, past_scores, 0)

    l_ref[...] = jnp.zeros_like(l_ref)
    acc_ref[...] = jnp.zeros_like(acc_ref)

    def accumulate(j, _):
        off = pl.multiple_of(j * blk, blk)
        for h in range(heads):
            p = jnp.exp2(s_ref[j * heads + h] - (m_ref[h] - bias_ref[j * heads + h]))
            l_ref[h] += jnp.sum(p, axis=0, keepdims=True)
            acc_ref[h] += _dot(vt_ref[0, h, :, pl.ds(off, blk)], p.astype(BF16))
        return 0

    lax.fori_loop(0, i + 1, accumulate, 0)
    yt = acc_ref[...] / l_ref[...]
    o_ref[0] = yt.reshape(heads * hd, blk).T.astype(BF16)


def _moba(qt, k, vt, km):
    batch, heads, seq, hd = k.shape
    blk = MOBA_BLOCK
    nb = seq // blk
    return pl.pallas_call(
        _moba_kernel, grid=(batch, nb),
        in_specs=[
            pl.BlockSpec((1, heads, hd, blk), lambda b, i: (b, 0, 0, i)),
            pl.BlockSpec((1, heads, seq, hd), lambda b, i: (b, 0, 0, 0)),
            pl.BlockSpec((1, heads, hd, seq), lambda b, i: (b, 0, 0, 0)),
            pl.BlockSpec((1, heads, nb, hd), lambda b, i: (b, 0, 0, 0)),
        ],
        out_specs=pl.BlockSpec((1, blk, heads * hd), lambda b, i: (b, i, 0)),
        out_shape=jax.ShapeDtypeStruct((batch, seq, heads * hd), BF16),
        scratch_shapes=[pltpu.VMEM((nb * heads, 1, blk), F32), pltpu.VMEM((heads, 1, blk), F32),
                        pltpu.VMEM((heads, 1, blk), F32), pltpu.VMEM((heads, hd, blk), F32),
                        pltpu.VMEM((nb * heads, blk, blk), F32)],
        compiler_params=_params("parallel", "arbitrary"), name="moba",
    )(qt, k, vt, km)


def _mlstm_kernel(u_ref, vmt_ref, ot_ref, ifc_ref, ift_ref, cw_ref, cb_ref, wqt_ref, wk_ref, brow_ref, bcol_ref,
                  gn_ref, skip_ref, y_ref, ext_ref, c_ref, n_ref, m_ref, yt_ref):
    tm = u_ref.shape[0]
    hd = MLSTM_HEAD_DIM
    halo = SUBLANES

    @pl.when(pl.program_id(1) == 0)
    def _():
        ext_ref[0:halo, :] = jnp.zeros((halo, MLSTM_WIDTH), F32)
        c_ref[...] = jnp.zeros_like(c_ref)
        n_ref[...] = jnp.zeros_like(n_ref)
        m_ref[...] = jnp.zeros_like(m_ref)

    u = u_ref[...]
    ext_ref[halo:halo + tm, :] = u
    acc = jnp.broadcast_to(cb_ref[...], u.shape)
    for j in range(MLSTM_CONV):
        acc = acc + cw_ref[j:j + 1, :] * ext_ref[halo - (MLSTM_CONV - 1) + j:halo - (MLSTM_CONV - 1) + j + tm, :]
    ext_ref[0:halo, :] = u[tm - halo:, :]
    uc = acc * jax.nn.sigmoid(acc)

    gc = ifc_ref[...] + brow_ref[...]
    gr = ift_ref[...] + bcol_ref[...]
    rows = lax.broadcasted_iota(jnp.int32, (tm, tm), 0)
    cols = lax.broadcasted_iota(jnp.int32, (tm, tm), 1)
    causal_t = rows <= cols
    tril = jnp.where(cols <= rows, 1.0, 0.0).astype(BF16)
    triu = jnp.where(causal_t, 1.0, 0.0).astype(BF16)
    c1, c2, c3 = _split3(_log_sigmoid(gc))
    bcum_c = _dot(tril, c1) + _dot(tril, c2) + _dot(tril, c3)
    r1, r2, r3 = _split3(_log_sigmoid(gr))
    bcum_r = _dot(r1, triu) + _dot(r2, triu) + _dot(r3, triu)

    uct = uc.T
    for h in range(MLSTM_HEADS):
        hs = slice(h * hd, (h + 1) * hd)
        fl = MLSTM_HEADS + h
        b_row = bcum_r[fl:fl + 1, :]
        key_row = gr[h:h + 1, :] - b_row
        key_col = gc[:, h:h + 1] - bcum_c[:, fl:fl + 1]
        m_prev = m_ref[h][:, 0:1]
        dlog = jnp.where(causal_t, key_col + b_row, NEG_INF)
        inter = b_row + m_prev
        m_t = jnp.maximum(inter, jnp.max(dlog, axis=0, keepdims=True))
        w_intra = jnp.exp(dlog - m_t)
        w_inter = jnp.exp(inter - m_t)

        uct_h = uct[hs, :]
        qtb = _dot(wqt_ref[h], uct_h.astype(BF16)).astype(BF16)
        k = _dot(uc[:, hs].astype(BF16), wk_ref[h]) * (hd ** -0.5)
        vtb = vmt_ref[hs, :]
        s = _dot(k.astype(BF16), qtb) * w_intra
        ct_prev = c_ref[h]
        n_prev = n_ref[h]
        n_hi = n_prev.astype(BF16)
        n_lo = (n_prev - n_hi.astype(F32)).astype(BF16)
        qn = (_dot(n_hi, qtb) + _dot(n_lo, qtb))[0:1, :]
        num = w_inter * _dot(ct_prev.astype(BF16), qtb) + _dot(vtb, s.astype(BF16))
        den = w_inter * qn + jnp.sum(s, axis=0, keepdims=True)
        hh = num / jnp.maximum(jnp.abs(den), jnp.exp(-m_t))

        b_end = b_row[:, tm - 1:tm]
        m_new = jnp.maximum(b_end + m_prev, jnp.max(b_end + key_row, axis=1, keepdims=True))
        decay = jnp.exp(b_end + m_prev - m_new)
        kw = k * jnp.exp(b_end + key_col - m_new)
        c_ref[h] = decay * ct_prev + _dot(vtb, kw.astype(BF16))
        n_ref[h] = decay * n_prev + jnp.broadcast_to(jnp.sum(kw, axis=0, keepdims=True), n_prev.shape)
        m_ref[h] = jnp.broadcast_to(m_new, (1, LANES))

        hh = jax.nn.sigmoid(ot_ref[hs, :]) * hh
        mu = jnp.mean(hh, axis=0, keepdims=True)
        hc = hh - mu
        var = jnp.mean(hc * hc, axis=0, keepdims=True)
        yt_ref[hs, :] = hc * lax.rsqrt(var + GN_EPS) * gn_ref[hs, :] + skip_ref[hs, :] * uct_h
    y_ref[...] = yt_ref[...].T.astype(BF16)


def _mlstm(u, vmt, ot, ifc, ift, conv_w, conv_b, wqt, wk, brow, bcol, gn_g, skip, batch, seq):
    n = u.shape[0]
    tm = ROW_TILE
    nc = seq // tm
    row = lambda w: pl.BlockSpec((tm, w), lambda b, c: (b * nc + c, 0))
    col = lambda h: pl.BlockSpec((h, tm), lambda b, c: (0, b * nc + c))
    in_specs = [row(MLSTM_WIDTH), col(MLSTM_WIDTH), col(MLSTM_WIDTH), row(LANES), col(SUBLANES),
                _full(conv_w.shape), _full(conv_b.shape), _full(wqt.shape), _full(wk.shape),
                _full(brow.shape), _full(bcol.shape), _full(gn_g.shape), _full(skip.shape)]
    return pl.pallas_call(
        _mlstm_kernel, grid=(batch, nc), in_specs=in_specs, out_specs=row(MLSTM_WIDTH),
        out_shape=jax.ShapeDtypeStruct((n, MLSTM_WIDTH), BF16),
        scratch_shapes=[pltpu.VMEM((SUBLANES + tm, MLSTM_WIDTH), F32),
                        pltpu.VMEM((MLSTM_HEADS, MLSTM_HEAD_DIM, MLSTM_HEAD_DIM), F32),
                        pltpu.VMEM((MLSTM_HEADS, SUBLANES, MLSTM_HEAD_DIM), F32),
                        pltpu.VMEM((MLSTM_HEADS, 1, LANES), F32),
                        pltpu.VMEM((MLSTM_WIDTH, tm), F32)],
        compiler_params=_params("parallel", "arbitrary"), name="mlstm",
    )(u, vmt, ot, ifc, ift, conv_w, conv_b, wqt, wk, brow, bcol, gn_g, skip)


def _mix_kernel(x_ref, g0_ref, b0_ref, ya_ref, ym_ref, ga_ref, gm_ref, wau_ref, wmu_ref, wout_ref,
                g1_ref, b1_ref, wrh_ref, wrl_ref, br_ref,
                x1_ref, ri_ref, rw_ref, cnt_out_ref, cnt_ref):
    tm = x_ref.shape[0]

    @pl.when(pl.program_id(0) == 0)
    def _():
        cnt_ref[...] = jnp.zeros_like(cnt_ref)

    xn = _layer_norm(x_ref[...], g0_ref[...], b0_ref[...])
    a_up = _dot(ya_ref[...], wau_ref[...])
    m_up = _dot(ym_ref[...], wmu_ref[...])
    mix = jax.nn.sigmoid(ga_ref[...]) * a_up + jax.nn.sigmoid(gm_ref[...]) * m_up
    x1 = _layer_norm(DEEPNORM_ALPHA * xn + _dot(mix.astype(BF16), wout_ref[...]), g1_ref[...], b1_ref[...])
    x1_ref[...] = x1

    x_hi = x1.astype(BF16)
    x_lo = (x1 - x_hi.astype(F32)).astype(BF16)
    w_hi = wrh_ref[...]
    logits = _dot(x_hi, w_hi) + _dot(x_lo, w_hi) + _dot(x_hi, wrl_ref[...]) + br_ref[...]
    lane = lax.broadcasted_iota(jnp.int32, (tm, LANES), 1).astype(F32)
    big = float(4 * LANES)
    is_g = (lane >= float(MOE_EXPERTS)) & (lane < float(MOE_EXPERTS + MOE_GROUPS))
    gl = jnp.where(is_g, logits, NEG_INF)
    ge = jnp.exp(gl - jnp.max(gl, axis=1, keepdims=True))
    gp = ge / jnp.sum(ge, axis=1, keepdims=True)
    g_w = jnp.max(gp, axis=1, keepdims=True)
    g_idx = jnp.min(jnp.where((gp == g_w) & is_g, lane - float(MOE_EXPERTS), big), axis=1, keepdims=True)
    lo = g_idx * float(MOE_EXPERTS_PER_GROUP)
    in_grp = (lane >= lo) & (lane < lo + float(MOE_EXPERTS_PER_GROUP))
    el = jnp.where(in_grp, logits, NEG_INF)
    v1 = jnp.max(el, axis=1, keepdims=True)
    i1 = jnp.min(jnp.where((el == v1) & in_grp, lane, big), axis=1, keepdims=True)
    el2 = jnp.where(lane == i1, NEG_INF, el)
    v2 = jnp.max(el2, axis=1, keepdims=True)
    i2 = jnp.min(jnp.where((el2 == v2) & in_grp & (lane != i1), lane, big), axis=1, keepdims=True)
    e2 = jnp.exp(v2 - v1)
    w0 = g_w / (1.0 + e2)
    w1 = g_w * e2 / (1.0 + e2)

    is1 = lane == i1
    is2 = lane == i2
    onehot = jnp.where(is1 | is2, 1.0, 0.0)
    rows = lax.broadcasted_iota(jnp.int32, (tm, tm), 0)
    cols = lax.broadcasted_iota(jnp.int32, (tm, tm), 1)
    strict = jnp.where(cols < rows, 1.0, 0.0).astype(BF16)
    before = _dot(strict, onehot.astype(BF16)) + cnt_ref[...]
    r0 = jnp.sum(jnp.where(is1, before, 0.0), axis=1, keepdims=True)
    r1 = jnp.sum(jnp.where(is2, before, 0.0), axis=1, keepdims=True)
    total = cnt_ref[...] + jnp.sum(onehot, axis=0, keepdims=True)
    cnt_ref[...] = total
    cnt_out_ref[...] = total

    ri = jnp.where(lane == 0.0, i1, jnp.where(lane == 1.0, i2, jnp.where(lane == 2.0, r0, jnp.where(lane == 3.0, r1, 0.0))))
    ri_ref[...] = ri.astype(jnp.int32)
    rw_ref[...] = jnp.where(lane == 0.0, w0, jnp.where(lane == 1.0, w1, 0.0))


def _mix(x2, g0, b0, ya, ym, ga, gm, wau, wmu, wout, g1, b1, wrh, wrl, br):
    n, d = x2.shape
    tm = ROW_TILE
    row = lambda w: pl.BlockSpec((tm, w), lambda i: (i, 0))
    in_specs = [row(d), _full(g0.shape), _full(b0.shape), row(ATTN_WIDTH), row(MLSTM_WIDTH), row(d), row(d),
                _full(wau.shape), _full(wmu.shape), _full(wout.shape), _full(g1.shape), _full(b1.shape),
                _full(wrh.shape), _full(wrl.shape), _full(br.shape)]
    out_shape = (jax.ShapeDtypeStruct((n, d), F32), jax.ShapeDtypeStruct((n, LANES), jnp.int32),
                 jax.ShapeDtypeStruct((n, LANES), F32), jax.ShapeDtypeStruct((1, LANES), F32))
    out_specs = (row(d), row(LANES), row(LANES), _full((1, LANES)))
    return pl.pallas_call(
        _mix_kernel, grid=(n // tm,), in_specs=in_specs, out_specs=out_specs, out_shape=out_shape,
        scratch_shapes=[pltpu.VMEM((1, LANES), F32)],
        compiler_params=_params("arbitrary"), name="mix",
    )(x2, g0, b0, ya, ym, ga, gm, wau, wmu, wout, g1, b1, wrh, wrl, br)


def _to_token_tiles(dst_ref, x):
    nch = x.shape[1] // LANES
    for c in range(nch):
        dst_ref[pl.ds(c, x.shape[0], stride=nch), :] = x[:, c * LANES:(c + 1) * LANES]


def _from_token_tiles(src_ref, rows, nch):
    return [src_ref[pl.ds(c, rows, stride=nch), :] for c in range(nch)]


def _token_copy(src, src_tok, dst, dst_tok, nch, sem):
    s0 = pl.multiple_of(src_tok * nch, nch)
    d0 = pl.multiple_of(dst_tok * nch, nch)
    return pltpu.make_async_copy(src.at[pl.ds(s0, nch), :], dst.at[pl.ds(d0, nch), :], sem)


def _slot(er_ref, ps_ref, r, k):
    return ps_ref[er_ref[4 * r + k]] + er_ref[4 * r + 2 + k]


def _dispatch_kernel(er_ref, ps_ref, last_ref, x_ref, xs_ref, scr_ref, zero_ref, sem, zsem):
    tm, d = x_ref.shape
    nch = d // LANES
    tb = zero_ref.shape[0] // nch

    @pl.when(pl.program_id(0) == 0)
    def _():
        zero_ref[...] = jnp.zeros_like(zero_ref)

        def desc(tok):
            off = pl.multiple_of(jnp.maximum(tok, 0) * nch, nch)
            return pltpu.make_async_copy(zero_ref, xs_ref.at[pl.ds(off, tb * nch), :], zsem)

        def zstart(e, _):
            @pl.when(last_ref[e] >= 0)
            def _():
                desc(last_ref[e]).start()
            return 0

        def zwait(e, _):
            @pl.when(last_ref[e] >= 0)
            def _():
                desc(last_ref[e]).wait()
            return 0

        lax.fori_loop(0, MOE_EXPERTS, zstart, 0)
        nused = last_ref[MOE_EXPERTS]
        nblk = xs_ref.shape[0] // (tb * nch)
        lax.fori_loop(nused, nblk, lambda b, _: (desc(b * tb).start(), 0)[1], 0)
        lax.fori_loop(0, MOE_EXPERTS, zwait, 0)
        lax.fori_loop(nused, nblk, lambda b, _: (desc(b * tb).wait(), 0)[1], 0)

    step = pl.program_id(0)
    slot = step % 2
    scr = scr_ref.at[slot]
    _to_token_tiles(scr, x_ref[...])

    def start(r, _):
        for k in range(2):
            _token_copy(scr, r, xs_ref, _slot(er_ref, ps_ref, r, k), nch, sem.at[slot]).start(priority=k)
        return 0

    def drain(which):
        def wait(r, _):
            for k in range(2):
                _token_copy(scr_ref.at[which], 0, xs_ref, 0, nch, sem.at[which]).wait()
            return 0
        lax.fori_loop(0, tm, wait, 0, unroll=8)

    lax.fori_loop(0, tm, start, 0, unroll=8)

    @pl.when(step > 0)
    def _():
        drain(1 - slot)

    @pl.when(step == pl.num_programs(0) - 1)
    def _():
        drain(slot)


def _dispatch(er, pad_start, last_blk, x1, n_rows):
    n, d = x1.shape
    tm = ROW_TILE
    nch = d // LANES
    smem = lambda: pl.BlockSpec(memory_space=pltpu.SMEM)
    return pl.pallas_call(
        _dispatch_kernel, grid=(n // tm,),
        in_specs=[pl.BlockSpec((4 * tm,), lambda i: (i,), memory_space=pltpu.SMEM), smem(), smem(),
                  pl.BlockSpec((tm, d), lambda i: (i, 0))],
        out_specs=pl.BlockSpec(memory_space=pl.ANY),
        out_shape=jax.ShapeDtypeStruct((n_rows * nch, LANES), F32),
        scratch_shapes=[pltpu.VMEM((2, tm * nch, LANES), F32), pltpu.VMEM((EXPERT_TILE * nch, LANES), F32),
                        pltpu.SemaphoreType.DMA((2,)), pltpu.SemaphoreType.DMA(())],
        compiler_params=_params("arbitrary"), name="dispatch",
    )(er, pad_start, last_blk, x1)


def _expert_kernel(first_ref, count_ref, widx_ref, nused_ref, wg_ref, wu_ref, wd_ref, xs_ref, ys_ref,
                   wgb_ref, wub_ref, wdb_ref, xbuf_ref, ybuf_ref, in_sem, out_sem):
    del widx_ref
    e = pl.program_id(0)
    nused = nused_ref[0]
    d = wg_ref.shape[1]
    nch = d // LANES
    rows = xbuf_ref.shape[1]
    tb = rows // nch
    nblk = xs_ref.shape[0] // rows

    def blk(ref, b):
        return ref.at[pl.ds(pl.multiple_of(b * rows, rows), rows), :]

    def in_copy(b, slot):
        return pltpu.make_async_copy(blk(xs_ref, b), xbuf_ref.at[slot], in_sem.at[slot])

    def out_copy(b, slot):
        return pltpu.make_async_copy(ybuf_ref.at[slot], blk(ys_ref, b), out_sem.at[slot])

    n_in = xbuf_ref.shape[0]

    @pl.when(e == 0)
    def _():
        for b0 in range(n_in - 1):
            @pl.when(b0 < nused)
            def _():
                in_copy(b0, b0).start()

    @pl.when(count_ref[e] > 0)
    def _():
        wgb_ref[...] = wg_ref[0].astype(BF16)
        wub_ref[...] = wu_ref[0].astype(BF16)
        wdb_ref[...] = wd_ref[0].astype(BF16)

    def body(b, _):
        slot = b % n_in
        oslot = b % 2
        in_copy(b, slot).wait()

        @pl.when(b + n_in - 1 < nused)
        def _():
            in_copy(b + n_in - 1, (b + n_in - 1) % n_in).start()

        @pl.when(b >= 2)
        def _():
            out_copy(b - 2, oslot).wait()

        xb = jnp.concatenate([c.astype(BF16) for c in _from_token_tiles(xbuf_ref.at[slot], tb, nch)], axis=1)
        g = _dot(xb, wgb_ref[...])
        u = _dot(xb, wub_ref[...])
        hmid = g * jax.nn.sigmoid(g) * u
        _to_token_tiles(ybuf_ref.at[oslot], _dot(hmid.astype(BF16), wdb_ref[...]))
        out_copy(b, oslot).start()
        return 0

    lax.fori_loop(first_ref[e], first_ref[e] + count_ref[e], body, 0)

    @pl.when(e == pl.num_programs(0) - 1)
    def _():
        for back in (2, 1):
            @pl.when(nused >= back)
            def _():
                out_copy(nused - back, (nused - back) % 2).wait()

        ybuf_ref[0] = jnp.zeros(ybuf_ref.shape[1:], F32)
        lax.fori_loop(nused, nblk, lambda b, _: (out_copy(b, 0).start(), 0)[1], 0)
        lax.fori_loop(nused, nblk, lambda b, _: (out_copy(b, 0).wait(), 0)[1], 0)


def _experts(first_blk, blk_count, w_idx, nused, xs, w_gate, w_up, w_down):
    n_exp, d, dff = w_gate.shape
    nch = d // LANES
    rows = EXPERT_TILE * nch
    w_spec = lambda shape: pl.BlockSpec(shape, lambda e, fb, bc, wi, nu: (wi[e], 0, 0))
    any_spec = pl.BlockSpec(memory_space=pl.ANY)
    grid_spec = pltpu.PrefetchScalarGridSpec(
        num_scalar_prefetch=4, grid=(n_exp,),
        in_specs=[w_spec((1, d, dff)), w_spec((1, d, dff)), w_spec((1, dff, d)), any_spec],
        out_specs=any_spec,
        scratch_shapes=[pltpu.VMEM((d, dff), BF16), pltpu.VMEM((d, dff), BF16), pltpu.VMEM((dff, d), BF16),
                        pltpu.VMEM((EXPERT_IN_SLOTS, rows, LANES), F32), pltpu.VMEM((2, rows, LANES), F32),
                        pltpu.SemaphoreType.DMA((EXPERT_IN_SLOTS,)), pltpu.SemaphoreType.DMA((2,))],
    )
    return pl.pallas_call(
        _expert_kernel, grid_spec=grid_spec, out_shape=jax.ShapeDtypeStruct(xs.shape, F32),
        compiler_params=_params("arbitrary"), name="experts",
    )(first_blk, blk_count, w_idx, nused, w_gate, w_up, w_down, xs)


def _combine_kernel(er_ref, er_next_ref, ps_ref, x1_ref, rw_ref, g_ref, b_ref, ys_ref, o_ref, buf_ref, sem):
    tm, d = x1_ref.shape
    nch = d // LANES
    step = pl.program_id(0)
    slot = step % 2

    def gather(idx_ref, which):
        def start(r, _):
            for k in range(2):
                _token_copy(ys_ref, _slot(idx_ref, ps_ref, r, k), buf_ref.at[which, k], r, nch,
                            sem.at[which]).start(priority=k)
            return 0
        lax.fori_loop(0, tm, start, 0, unroll=8)

    @pl.when(step == 0)
    def _():
        gather(er_ref, 0)

    @pl.when(step + 1 < pl.num_programs(0))
    def _():
        gather(er_next_ref, 1 - slot)

    def wait(r, _):
        for k in range(2):
            _token_copy(ys_ref, 0, buf_ref.at[slot, k], 0, nch, sem.at[slot]).wait()
        return 0

    lax.fori_loop(0, tm, wait, 0, unroll=8)
    rw = rw_ref[...]
    y0 = jnp.concatenate(_from_token_tiles(buf_ref.at[slot, 0], tm, nch), axis=1)
    y1 = jnp.concatenate(_from_token_tiles(buf_ref.at[slot, 1], tm, nch), axis=1)
    ffn = rw[:, 0:1] * y0 + rw[:, 1:2] * y1
    o_ref[...] = _layer_norm(DEEPNORM_ALPHA * x1_ref[...] + ffn, g_ref[...], b_ref[...])


def _combine(er, pad_start, x1, rw, ln_g, ln_b, ys):
    n, d = x1.shape
    tm = ROW_TILE
    nch = d // LANES
    last = n // tm - 1
    row = lambda w: pl.BlockSpec((tm, w), lambda i: (i, 0))
    return pl.pallas_call(
        _combine_kernel, grid=(n // tm,),
        in_specs=[pl.BlockSpec((4 * tm,), lambda i: (i,), memory_space=pltpu.SMEM),
                  pl.BlockSpec((4 * tm,), lambda i: (jnp.minimum(i + 1, last),), memory_space=pltpu.SMEM),
                  pl.BlockSpec(memory_space=pltpu.SMEM),
                  row(d), row(LANES), _full(ln_g.shape), _full(ln_b.shape),
                  pl.BlockSpec(memory_space=pl.ANY)],
        out_specs=row(d),
        out_shape=jax.ShapeDtypeStruct((n, d), F32),
        scratch_shapes=[pltpu.VMEM((2, 2, tm * nch, LANES), F32), pltpu.SemaphoreType.DMA((2,))],
        compiler_params=_params("arbitrary"), name="combine",
    )(er, er, pad_start, x1, rw, ln_g, ln_b, ys)


def _rope_tables(seq):
    half = ATTN_HEAD_DIM // 2
    inv_freq = ROPE_THETA ** (-jnp.arange(half, dtype=F32) / half)
    ang = jnp.arange(seq, dtype=F32)[:, None] * inv_freq[None, :]
    cos = jnp.cos(ang)
    sin = jnp.sin(ang)
    cos_h = jnp.concatenate([cos, cos], axis=1)
    sin_h = jnp.concatenate([-sin, sin], axis=1)
    return jnp.tile(cos_h, (1, ATTN_HEADS)), jnp.tile(sin_h, (1, ATTN_HEADS))


def _pad_lanes(a, width=LANES):
    return jnp.pad(a, ((0, 0), (0, width - a.shape[1])))


def kernel(x, ln0_g, ln0_b, w_in, conv_w, conv_b, w_mq, w_mk, b_i, b_f, gn_g, skip, w_attn_up, w_mlstm_up, w_out,
           ln1_g, ln1_b, w_router_group, b_router_group, w_router_expert, b_router_expert, w_gate, w_up, w_down,
           ln2_g, ln2_b):
    batch, seq, d = x.shape
    n = batch * seq
    assert seq % ROW_TILE == 0 and ROW_TILE == MOBA_BLOCK and w_in.shape[0] == DEPTH
    x2 = x.reshape(n, d)
    vec = lambda a: a.reshape(1, -1).astype(F32)

    w = w_in[0]
    c_if = 3 * ATTN_WIDTH + 3 * MLSTM_WIDTH
    c_g = c_if + 2 * MLSTM_HEADS
    wqkv = w[:, :3 * ATTN_WIDTH].astype(BF16)
    wuvo = w[:, 3 * ATTN_WIDTH:c_if].astype(BF16)
    w_if = w[:, c_if:c_g]
    wif = _pad_lanes(w_if).astype(BF16)
    wift = w_if.T.astype(BF16)
    wg = w[:, c_g:].astype(BF16)
    cos, sin = _rope_tables(seq)

    q, k, v, kmean, u, vm, o, ifc, ift, ga, gm = _inproj(
        x2, vec(ln0_g), vec(ln0_b), wqkv, wuvo, wif, wift, wg, cos, sin, batch, seq)

    nb = seq // MOBA_BLOCK
    km = kmean.reshape(batch, nb, ATTN_HEADS, ATTN_HEAD_DIM).transpose(0, 2, 1, 3)
    ya = _moba(q, k, v, km).reshape(n, ATTN_WIDTH)

    b_if = jnp.concatenate([b_i[0], b_f[0]]).astype(F32)
    ym = _mlstm(u, vm, o, ifc, ift, conv_w[0], vec(conv_b[0]), w_mq[0].transpose(0, 2, 1).astype(BF16),
                w_mk[0].astype(BF16), _pad_lanes(b_if[None, :]), b_if[:, None],
                gn_g[0].astype(F32)[:, None], skip[0].astype(F32)[:, None], batch, seq)

    w_r = _pad_lanes(jnp.concatenate([w_router_expert[0], w_router_group[0]], axis=1))
    w_r_hi = w_r.astype(BF16)
    w_r_lo = (w_r - w_r_hi.astype(F32)).astype(BF16)
    b_r = _pad_lanes(jnp.concatenate([b_router_expert[0], b_router_group[0]])[None, :])
    x1, ri, rw, counts = _mix(
        x2, vec(ln0_g), vec(ln0_b), ya, ym, ga, gm, w_attn_up[0].astype(BF16), w_mlstm_up[0].astype(BF16),
        w_out[0].astype(BF16), vec(ln1_g[0]), vec(ln1_b[0]), w_r_hi, w_r_lo, b_r)

    tb = EXPERT_TILE
    nblk = (2 * n) // tb + MOE_EXPERTS
    cnt = counts[0, :MOE_EXPERTS].astype(jnp.int32)
    nblk_e = (cnt + tb - 1) // tb
    blk_end = jnp.cumsum(nblk_e)
    pad_start = (blk_end - nblk_e) * tb
    nused = blk_end[-1:]
    ids = jnp.arange(MOE_EXPERTS, dtype=jnp.int32)
    prev_used = jnp.max(jnp.where((ids[None, :] <= ids[:, None]) & (nblk_e[None, :] > 0), ids[None, :], -1), axis=1)
    first_used = jnp.min(jnp.where(nblk_e > 0, ids, MOE_EXPERTS - 1))
    w_idx = jnp.where(prev_used >= 0, prev_used, first_used).astype(jnp.int32)
    last_blk = jnp.where(nblk_e > 0, (blk_end - 1) * tb, -1)
    last_blk = jnp.concatenate([last_blk, nused]).astype(jnp.int32)
    er = ri[:, :4].reshape(4 * n)

    xs = _dispatch(er, pad_start, last_blk, x1, nblk * tb)
    ys = _experts((blk_end - nblk_e).astype(jnp.int32), nblk_e.astype(jnp.int32), w_idx, nused.astype(jnp.int32),
                  xs, w_gate[0], w_up[0], w_down[0])
    out = _combine(er, pad_start, x1, rw, vec(ln2_g[0]), vec(ln2_b[0]), ys)
    return out.reshape(batch, seq, d)
```

```python
import functools
import math

import jax
import jax.numpy as jnp
from jax import lax
from jax.experimental import pallas as pl
from jax.experimental.pallas import tpu as pltpu

F32 = jnp.float32
BF16 = jnp.bfloat16

ATTN_HEADS = 8
ATTN_HEAD_DIM = 64
ATTN_WIDTH = ATTN_HEADS * ATTN_HEAD_DIM
MOBA_BLOCK = 256
MOBA_TOPK = 3
ROPE_THETA = 10000.0
MLSTM_HEADS = 4
MLSTM_HEAD_DIM = 128
MLSTM_WIDTH = MLSTM_HEADS * MLSTM_HEAD_DIM
MLSTM_CONV = 4
MOE_GROUPS = 8
MOE_EXPERTS_PER_GROUP = 8
MOE_EXPERTS = MOE_GROUPS * MOE_EXPERTS_PER_GROUP
MOE_D_FF = 512
LN_EPS = 1e-5
GN_EPS = 1e-6
DEPTH = 1
DEEPNORM_ALPHA = (2 * DEPTH) ** 0.25

LANES = 128
SUBLANES = 8
ROW_TILE = 256
EXPERT_TILE = 256
EXPERT_IN_SLOTS = 4
MIX_CHAINS = 4
VMEM_LIMIT = 48 * 1024 * 1024
LOG2_E = math.log2(math.e)

NEG_INF = float("-inf")


def _params(*sem):
    return pltpu.CompilerParams(dimension_semantics=sem, vmem_limit_bytes=VMEM_LIMIT)


def _dot(a, b):
    return jnp.dot(a, b, preferred_element_type=F32)


def _dot_nt(a, b):
    return lax.dot_general(a, b, (((1,), (1,)), ((), ())), preferred_element_type=F32)


def _dot_tn(a, b):
    return lax.dot_general(a, b, (((0,), (0,)), ((), ())), preferred_element_type=F32)


def _split3(x):
    x1 = x.astype(BF16)
    r1 = x - x1.astype(F32)
    x2 = r1.astype(BF16)
    r2 = r1 - x2.astype(F32)
    return x1, x2, r2.astype(BF16)


def _layer_norm(x, g, b):
    mu = jnp.mean(x, axis=-1, keepdims=True)
    xc = x - mu
    var = jnp.mean(xc * xc, axis=-1, keepdims=True)
    return xc * lax.rsqrt(var + LN_EPS) * g + b


def _log_sigmoid(x):
    return jnp.minimum(x, 0.0) - jnp.log1p(jnp.exp(-jnp.abs(x)))


def _full(shape):
    nd = len(shape)
    return pl.BlockSpec(shape, lambda *_: (0,) * nd)


def _inproj_kernel(x_ref, g_ref, b_ref, wqkv_ref, wuvo_ref, wif_ref, wift_ref, wg_ref, cos_ref, sin_ref,
                   q_ref, k_ref, v_ref, km_ref, u_ref, vm_ref, o_ref, ifc_ref, ift_ref, ga_ref, gm_ref, xn_ref):
    xn = _layer_norm(x_ref[...], g_ref[...], b_ref[...])
    xb = xn.astype(BF16)

    cos = cos_ref[...]
    sin = sin_ref[...]
    lane = lax.broadcasted_iota(jnp.int32, cos.shape, 1)
    first_half = (lane % ATTN_HEAD_DIM) < (ATTN_HEAD_DIM // 2)

    def rope(t):
        fwd = pltpu.roll(t, ATTN_WIDTH - ATTN_HEAD_DIM // 2, axis=1)
        bwd = pltpu.roll(t, ATTN_HEAD_DIM // 2, axis=1)
        return t * cos + jnp.where(first_half, fwd, bwd) * sin

    zqkv = _dot(xb, wqkv_ref[...])
    q = rope(zqkv[:, :ATTN_WIDTH]) * (ATTN_HEAD_DIM ** -0.5 * LOG2_E)
    k = rope(zqkv[:, ATTN_WIDTH:2 * ATTN_WIDTH])
    v = zqkv[:, 2 * ATTN_WIDTH:]
    km_ref[0] = jnp.mean(k, axis=0, keepdims=True)
    qt = q.T
    vt = v.T
    for h in range(ATTN_HEADS):
        sl = slice(h * ATTN_HEAD_DIM, (h + 1) * ATTN_HEAD_DIM)
        q_ref[0, h] = qt[sl, :].astype(BF16)
        k_ref[0, h] = k[:, sl].astype(BF16)
        v_ref[0, h] = vt[sl, :].astype(BF16)

    zuvo = _dot(xb, wuvo_ref[...])
    u_ref[...] = zuvo[:, :MLSTM_WIDTH]
    vm_ref[...] = zuvo[:, MLSTM_WIDTH:2 * MLSTM_WIDTH].T.astype(BF16)
    o_ref[...] = zuvo[:, 2 * MLSTM_WIDTH:].T

    ifc_ref[...] = _dot(xb, wif_ref[...])
    ift_ref[...] = _dot_nt(wift_ref[...], xb)

    zg = _dot(xb, wg_ref[...])
    d = ga_ref.shape[1]
    ga_ref[...] = jax.nn.sigmoid(zg[:, :d]).astype(BF16)
    gm_ref[...] = jax.nn.sigmoid(zg[:, d:]).astype(BF16)
    xn_ref[...] = xn


def _inproj(x2, ln_g, ln_b, wqkv, wuvo, wif, wift, wg, cos, sin, batch, seq):
    n, d = x2.shape
    tm = ROW_TILE
    nsb = seq // tm
    hd = ATTN_HEAD_DIM
    row = lambda w: pl.BlockSpec((tm, w), lambda i: (i, 0))
    col = lambda h: pl.BlockSpec((h, tm), lambda i: (0, i))
    head = pl.BlockSpec((1, ATTN_HEADS, tm, hd), lambda i: (i // nsb, 0, i % nsb, 0))
    head_t = pl.BlockSpec((1, ATTN_HEADS, hd, tm), lambda i: (i // nsb, 0, 0, i % nsb))
    tab = pl.BlockSpec((tm, ATTN_WIDTH), lambda i: (i % nsb, 0))
    head_shape = jax.ShapeDtypeStruct((batch, ATTN_HEADS, seq, hd), BF16)
    head_t_shape = jax.ShapeDtypeStruct((batch, ATTN_HEADS, hd, seq), BF16)
    out_shape = (
        head_t_shape, head_shape, head_t_shape,
        jax.ShapeDtypeStruct((n // tm, 1, ATTN_WIDTH), F32),
        jax.ShapeDtypeStruct((n, MLSTM_WIDTH), F32),
        jax.ShapeDtypeStruct((MLSTM_WIDTH, n), BF16),
        jax.ShapeDtypeStruct((MLSTM_WIDTH, n), F32),
        jax.ShapeDtypeStruct((n, LANES), F32),
        jax.ShapeDtypeStruct((SUBLANES, n), F32),
        jax.ShapeDtypeStruct((n, d), BF16),
        jax.ShapeDtypeStruct((n, d), BF16),
        jax.ShapeDtypeStruct((n, d), F32),
    )
    out_specs = (
        head_t, head, head_t,
        pl.BlockSpec((1, 1, ATTN_WIDTH), lambda i: (i, 0, 0)),
        row(MLSTM_WIDTH), col(MLSTM_WIDTH), col(MLSTM_WIDTH),
        row(LANES),
        col(SUBLANES),
        row(d), row(d), row(d),
    )
    in_specs = [row(d), _full(ln_g.shape), _full(ln_b.shape), _full(wqkv.shape), _full(wuvo.shape),
                _full(wif.shape), _full(wift.shape), _full(wg.shape), tab, tab]
    return pl.pallas_call(
        _inproj_kernel, grid=(n // tm,), in_specs=in_specs, out_specs=out_specs, out_shape=out_shape,
        compiler_params=_params("parallel"), name="inproj",
    )(x2, ln_g, ln_b, wqkv, wuvo, wif, wift, wg, cos, sin)


def _moba_kernel(qt_ref, k_ref, vt_ref, km_ref, o_ref, bias_ref, m_ref, l_ref, acc_ref, s_ref):
    i = pl.program_id(1)
    blk = MOBA_BLOCK
    hd = ATTN_HEAD_DIM
    heads = ATTN_HEADS
    nb = k_ref.shape[2] // blk
    blk_id = lax.broadcasted_iota(jnp.int32, (nb, blk), 0)
    key_pos = lax.broadcasted_iota(jnp.int32, (blk, blk), 0)
    qry_pos = lax.broadcasted_iota(jnp.int32, (blk, blk), 1)
    causal = key_pos <= qry_pos

    for h in range(heads):
        qt = qt_ref[0, h]
        km = km_ref[0, h]
        km_hi = km.astype(BF16)
        km_lo = (km - km_hi.astype(F32)).astype(BF16)
        gate = _dot(km_hi, qt) + _dot(km_lo, qt)
        gate = jnp.where(blk_id < i, gate, NEG_INF)
        for j in range(nb - 1):
            row = gate[j:j + 1, :]
            beats = (gate > row) | ((gate == row) & (blk_id < j))
            cnt = jnp.sum(jnp.where(beats, 1.0, 0.0), axis=0, keepdims=True)
            sel = (cnt < float(MOBA_TOPK)) & (row > NEG_INF)
            bias_ref[j * heads + h] = jnp.where(sel, 0.0, NEG_INF)
    for h in range(heads):
        bias_ref[i * heads + h] = jnp.zeros((1, blk), F32)

    def scores(h, j, own_block):
        qt = qt_ref[0, h]
        half = blk // 2
        m_tile = None
        for c in range(2):
            rows = slice(c * half, (c + 1) * half)
            s = _dot(k_ref[0, h, pl.ds(pl.multiple_of(j * blk + c * half, half), half), :], qt)
            if own_block:
                s = jnp.where(causal[rows], s, NEG_INF)
            s_ref[j * heads + h, rows, :] = s
            m_c = jnp.max(s, axis=0, keepdims=True)
            m_tile = m_c if m_tile is None else jnp.maximum(m_tile, m_c)
        return m_tile

    for h in range(heads):
        m_ref[h] = scores(h, i, True)

    def past_scores(j, _):
        for h in range(heads):
            m_ref[h] = jnp.maximum(m_ref[h], scores(h, j, False) + bias_ref[j * heads + h])
        return 0

    lax.fori_loop(0, i, past_scores, 0)

    l_ref[...] = jnp.zeros_like(l_ref)
    acc_ref[...] = jnp.zeros_like(acc_ref)

    def accumulate(j, _):
        off = pl.multiple_of(j * blk, blk)
        for h in range(heads):
            p = jnp.exp2(s_ref[j * heads + h] - (m_ref[h] - bias_ref[j * heads + h]))
            l_ref[h] += jnp.sum(p, axis=0, keepdims=True)
            acc_ref[h] += _dot(vt_ref[0, h, :, pl.ds(off, blk)], p.astype(BF16))
        return 0

    lax.fori_loop(0, i + 1, accumulate, 0)
    yt = acc_ref[...] / l_ref[...]
    o_ref[0] = yt.reshape(heads * hd, blk).T.astype(BF16)


def _moba(qt, k, vt, km):
    batch, heads, seq, hd = k.shape
    blk = MOBA_BLOCK
    nb = seq // blk
    return pl.pallas_call(
        _moba_kernel, grid=(batch, nb),
        in_specs=[
            pl.BlockSpec((1, heads, hd, blk), lambda b, i: (b, 0, 0, i)),
            pl.BlockSpec((1, heads, seq, hd), lambda b, i: (b, 0, 0, 0)),
            pl.BlockSpec((1, heads, hd, seq), lambda b, i: (b, 0, 0, 0)),
            pl.BlockSpec((1, heads, nb, hd), lambda b, i: (b, 0, 0, 0)),
        ],
        out_specs=pl.BlockSpec((1, blk, heads * hd), lambda b, i: (b, i, 0)),
        out_shape=jax.ShapeDtypeStruct((batch, seq, heads * hd), BF16),
        scratch_shapes=[pltpu.VMEM((nb * heads, 1, blk), F32), pltpu.VMEM((heads, 1, blk), F32),
                        pltpu.VMEM((heads, 1, blk), F32), pltpu.VMEM((heads, hd, blk), F32),
                        pltpu.VMEM((nb * heads, blk, blk), F32)],
        compiler_params=_params("parallel", "arbitrary"), name="moba",
    )(qt, k, vt, km)


def _mlstm_kernel(u_ref, vmt_ref, ot_ref, ifc_ref, ift_ref, cw_ref, cb_ref, wqt_ref, wk_ref, brow_ref, bcol_ref,
                  gn_ref, skip_ref, y_ref, ext_ref, c_ref, n_ref, m_ref, yt_ref):
    tm = u_ref.shape[0]
    hd = MLSTM_HEAD_DIM
    halo = SUBLANES

    @pl.when(pl.program_id(1) == 0)
    def _():
        ext_ref[0:halo, :] = jnp.zeros((halo, MLSTM_WIDTH), F32)
        c_ref[...] = jnp.zeros_like(c_ref)
        n_ref[...] = jnp.zeros_like(n_ref)
        m_ref[...] = jnp.zeros_like(m_ref)

    u = u_ref[...]
    ext_ref[halo:halo + tm, :] = u
    acc = jnp.broadcast_to(cb_ref[...], u.shape)
    for j in range(MLSTM_CONV):
        acc = acc + cw_ref[j:j + 1, :] * ext_ref[halo - (MLSTM_CONV - 1) + j:halo - (MLSTM_CONV - 1) + j + tm, :]
    ext_ref[0:halo, :] = u[tm - halo:, :]
    uc = acc * jax.nn.sigmoid(acc)

    gc = ifc_ref[...] + brow_ref[...]
    gr = ift_ref[...] + bcol_ref[...]
    rows = lax.broadcasted_iota(jnp.int32, (tm, tm), 0)
    cols = lax.broadcasted_iota(jnp.int32, (tm, tm), 1)
    causal_t = rows <= cols
    tril = jnp.where(cols <= rows, 1.0, 0.0).astype(BF16)
    triu = jnp.where(causal_t, 1.0, 0.0).astype(BF16)
    c1, c2, c3 = _split3(_log_sigmoid(gc))
    bcum_c = _dot(tril, c1) + _dot(tril, c2) + _dot(tril, c3)
    r1, r2, r3 = _split3(_log_sigmoid(gr))
    bcum_r = _dot(r1, triu) + _dot(r2, triu) + _dot(r3, triu)

    uct = uc.T
    for h in range(MLSTM_HEADS):
        hs = slice(h * hd, (h + 1) * hd)
        fl = MLSTM_HEADS + h
        b_row = bcum_r[fl:fl + 1, :]
        key_row = gr[h:h + 1, :] - b_row
        key_col = gc[:, h:h + 1] - bcum_c[:, fl:fl + 1]
        m_prev = m_ref[h][:, 0:1]
        dlog = jnp.where(causal_t, key_col + b_row, NEG_INF)
        inter = b_row + m_prev
        m_t = jnp.maximum(inter, jnp.max(dlog, axis=0, keepdims=True))
        w_intra = jnp.exp(dlog - m_t)
        w_inter = jnp.exp(inter - m_t)

        uct_h = uct[hs, :]
        qtb = _dot(wqt_ref[h], uct_h.astype(BF16)).astype(BF16)
        k = _dot(uc[:, hs].astype(BF16), wk_ref[h]) * (hd ** -0.5)
        vtb = vmt_ref[hs, :]
        s = _dot(k.astype(BF16), qtb) * w_intra
        ct_prev = c_ref[h]
        n_prev = n_ref[h]
        n_hi = n_prev.astype(BF16)
        n_lo = (n_prev - n_hi.astype(F32)).astype(BF16)
        qn = (_dot(n_hi, qtb) + _dot(n_lo, qtb))[0:1, :]
        num = w_inter * _dot(ct_prev.astype(BF16), qtb) + _dot(vtb, s.astype(BF16))
        den = w_inter * qn + jnp.sum(s, axis=0, keepdims=True)
        hh = num / jnp.maximum(jnp.abs(den), jnp.exp(-m_t))

        b_end = b_row[:, tm - 1:tm]
        m_new = jnp.maximum(b_end + m_prev, jnp.max(b_end + key_row, axis=1, keepdims=True))
        decay = jnp.exp(b_end + m_prev - m_new)
        kw = k * jnp.exp(b_end + key_col - m_new)
        c_ref[h] = decay * ct_prev + _dot(vtb, kw.astype(BF16))
        n_ref[h] = decay * n_prev + jnp.broadcast_to(jnp.sum(kw, axis=0, keepdims=True), n_prev.shape)
        m_ref[h] = jnp.broadcast_to(m_new, (1, LANES))

        hh = jax.nn.sigmoid(ot_ref[hs, :]) * hh
        mu = jnp.mean(hh, axis=0, keepdims=True)
        hc = hh - mu
        var = jnp.mean(hc * hc, axis=0, keepdims=True)
        yt_ref[hs, :] = hc * lax.rsqrt(var + GN_EPS) * gn_ref[hs, :] + skip_ref[hs, :] * uct_h
    y_ref[...] = yt_ref[...].T.astype(BF16)


def _mlstm(u, vmt, ot, ifc, ift, conv_w, conv_b, wqt, wk, brow, bcol, gn_g, skip, batch, seq):
    n = u.shape[0]
    tm = ROW_TILE
    nc = seq // tm
    row = lambda w: pl.BlockSpec((tm, w), lambda b, c: (b * nc + c, 0))
    col = lambda h: pl.BlockSpec((h, tm), lambda b, c: (0, b * nc + c))
    in_specs = [row(MLSTM_WIDTH), col(MLSTM_WIDTH), col(MLSTM_WIDTH), row(LANES), col(SUBLANES),
                _full(conv_w.shape), _full(conv_b.shape), _full(wqt.shape), _full(wk.shape),
                _full(brow.shape), _full(bcol.shape), _full(gn_g.shape), _full(skip.shape)]
    return pl.pallas_call(
        _mlstm_kernel, grid=(batch, nc), in_specs=in_specs, out_specs=row(MLSTM_WIDTH),
        out_shape=jax.ShapeDtypeStruct((n, MLSTM_WIDTH), BF16),
        scratch_shapes=[pltpu.VMEM((SUBLANES + tm, MLSTM_WIDTH), F32),
                        pltpu.VMEM((MLSTM_HEADS, MLSTM_HEAD_DIM, MLSTM_HEAD_DIM), F32),
                        pltpu.VMEM((MLSTM_HEADS, SUBLANES, MLSTM_HEAD_DIM), F32),
                        pltpu.VMEM((MLSTM_HEADS, 1, LANES), F32),
                        pltpu.VMEM((MLSTM_WIDTH, tm), F32)],
        compiler_params=_params("parallel", "arbitrary"), name="mlstm",
    )(u, vmt, ot, ifc, ift, conv_w, conv_b, wqt, wk, brow, bcol, gn_g, skip)


def _mix_kernel(xn_ref, ya_ref, ym_ref, ga_ref, gm_ref, wau_ref, wmu_ref, wout_ref,
                g1_ref, b1_ref, wrc_ref, br_ref,
                x1_ref, ri_ref, rw_ref, cnt_out_ref, cnt_ref):
    @pl.when(pl.program_id(0) == 0)
    def _():
        cnt_ref[...] = jnp.zeros_like(cnt_ref)

    tm = ROW_TILE
    sub = lax.broadcasted_iota(jnp.int32, (LANES, tm), 0).astype(F32)
    big = float(4 * LANES)

    def up_and_mix(st, rs):
        a_up = _dot(ya_ref[rs, :], wau_ref[...])
        m_up = _dot(ym_ref[rs, :], wmu_ref[...])
        mix = ga_ref[rs, :].astype(F32) * a_up + gm_ref[rs, :].astype(F32) * m_up
        st["mix"] = mix.astype(BF16)

    def out_and_norm(st, rs):
        x1 = _layer_norm(DEEPNORM_ALPHA * xn_ref[rs, :] + _dot(st.pop("mix"), wout_ref[...]), g1_ref[...], b1_ref[...])
        x1_ref[rs, :] = x1
        st["x1"] = x1

    def router_logits(st, rs):
        x1 = st.pop("x1")
        x_hi = x1.astype(BF16)
        x_lo = (x1 - x_hi.astype(F32)).astype(BF16)
        both = _dot_nt(wrc_ref[...], x_hi)
        st["logits"] = both[:LANES] + both[LANES:] + _dot_nt(wrc_ref[:LANES, :], x_lo) + br_ref[...]

    def route(st, rs):
        logits = st.pop("logits")
        is_g = (sub >= float(MOE_EXPERTS)) & (sub < float(MOE_EXPERTS + MOE_GROUPS))
        gl = jnp.where(is_g, logits, NEG_INF)
        ge = jnp.exp(gl - jnp.max(gl, axis=0, keepdims=True))
        gp = ge / jnp.sum(ge, axis=0, keepdims=True)
        g_w = jnp.max(gp, axis=0, keepdims=True)
        g_idx = jnp.min(jnp.where((gp == g_w) & is_g, sub - float(MOE_EXPERTS), big), axis=0, keepdims=True)
        lo = g_idx * float(MOE_EXPERTS_PER_GROUP)
        in_grp = (sub >= lo) & (sub < lo + float(MOE_EXPERTS_PER_GROUP))
        el = jnp.where(in_grp, logits, NEG_INF)
        v1 = jnp.max(el, axis=0, keepdims=True)
        i1 = jnp.min(jnp.where((el == v1) & in_grp, sub, big), axis=0, keepdims=True)
        el2 = jnp.where(sub == i1, NEG_INF, el)
        v2 = jnp.max(el2, axis=0, keepdims=True)
        i2 = jnp.min(jnp.where((el2 == v2) & in_grp & (sub != i1), sub, big), axis=0, keepdims=True)
        e2 = jnp.exp(v2 - v1)
        w0 = g_w / (1.0 + e2)
        w1 = g_w * e2 / (1.0 + e2)
        rw_ref[rs, :] = jnp.where(sub == 0.0, w0, jnp.where(sub == 1.0, w1, 0.0)).T
        st["i1"], st["i2"] = i1, i2

    def rank(st, rs):
        i1, i2 = st.pop("i1"), st.pop("i2")
        is1 = sub == i1
        is2 = sub == i2
        onehot = jnp.where(is1 | is2, 1.0, 0.0)
        rows = lax.broadcasted_iota(jnp.int32, (tm, tm), 0)
        cols = lax.broadcasted_iota(jnp.int32, (tm, tm), 1)
        earlier = jnp.where(rows < cols, 1.0, 0.0).astype(BF16)
        before = _dot(onehot.astype(BF16), earlier) + cnt_ref[...]
        r0 = jnp.sum(jnp.where(is1, before, 0.0), axis=0, keepdims=True)
        r1 = jnp.sum(jnp.where(is2, before, 0.0), axis=0, keepdims=True)
        total = cnt_ref[...] + jnp.sum(onehot, axis=1, keepdims=True)
        cnt_ref[...] = total
        cnt_out_ref[...] = total
        ri_t = jnp.where(sub == 0.0, i1, jnp.where(sub == 1.0, i2, jnp.where(sub == 2.0, r0, jnp.where(sub == 3.0, r1, 0.0))))
        ri_ref[rs, :] = ri_t.T.astype(jnp.int32)

    phases = (up_and_mix, out_and_norm, router_logits, route, rank)
    chains = xn_ref.shape[0] // tm
    states = [dict() for _ in range(chains)]
    for t in range(chains + len(phases) - 1):
        for c in range(chains):
            if 0 <= t - c < len(phases):
                phases[t - c](states[c], slice(c * tm, (c + 1) * tm))


def _mix(xn, ya, ym, ga, gm, wau, wmu, wout, g1, b1, wrc, br):
    n, d = xn.shape
    tm = MIX_CHAINS * ROW_TILE
    row = lambda w: pl.BlockSpec((tm, w), lambda i: (i, 0))
    in_specs = [row(d), row(ATTN_WIDTH), row(MLSTM_WIDTH), row(d), row(d),
                _full(wau.shape), _full(wmu.shape), _full(wout.shape), _full(g1.shape), _full(b1.shape),
                _full(wrc.shape), _full(br.shape)]
    out_shape = (jax.ShapeDtypeStruct((n, d), F32), jax.ShapeDtypeStruct((n, LANES), jnp.int32),
                 jax.ShapeDtypeStruct((n, LANES), F32), jax.ShapeDtypeStruct((LANES, 1), F32))
    out_specs = (row(d), row(LANES), row(LANES), _full((LANES, 1)))
    return pl.pallas_call(
        _mix_kernel, grid=(n // tm,), in_specs=in_specs, out_specs=out_specs, out_shape=out_shape,
        scratch_shapes=[pltpu.VMEM((LANES, 1), F32)],
        compiler_params=_params("arbitrary"), name="mix",
    )(xn, ya, ym, ga, gm, wau, wmu, wout, g1, b1, wrc, br)


def _to_token_tiles(dst_ref, x):
    nch = x.shape[1] // LANES
    for c in range(nch):
        dst_ref[pl.ds(c, x.shape[0], stride=nch), :] = x[:, c * LANES:(c + 1) * LANES]


def _from_token_tiles(src_ref, rows, nch):
    return [src_ref[pl.ds(c, rows, stride=nch), :] for c in range(nch)]


def _token_copy(src, src_tok, dst, dst_tok, nch, sem):
    s0 = pl.multiple_of(src_tok * nch, nch)
    d0 = pl.multiple_of(dst_tok * nch, nch)
    return pltpu.make_async_copy(src.at[pl.ds(s0, nch), :], dst.at[pl.ds(d0, nch), :], sem)


def _slot(er_ref, ps_ref, r, k):
    return ps_ref[er_ref[4 * r + k]] + er_ref[4 * r + 2 + k]


def _dispatch_kernel(er_ref, ps_ref, last_ref, x_ref, xs_ref, scr_ref, zero_ref, sem, zsem):
    tm, d = x_ref.shape
    nch = d // LANES
    tb = zero_ref.shape[0] // nch

    @pl.when(pl.program_id(0) == 0)
    def _():
        zero_ref[...] = jnp.zeros_like(zero_ref)

        def desc(tok):
            off = pl.multiple_of(jnp.maximum(tok, 0) * nch, nch)
            return pltpu.make_async_copy(zero_ref, xs_ref.at[pl.ds(off, tb * nch), :], zsem)

        def zstart(e, _):
            @pl.when(last_ref[e] >= 0)
            def _():
                desc(last_ref[e]).start()
            return 0

        def zwait(e, _):
            @pl.when(last_ref[e] >= 0)
            def _():
                desc(last_ref[e]).wait()
            return 0

        lax.fori_loop(0, MOE_EXPERTS, zstart, 0)
        nused = last_ref[MOE_EXPERTS]
        nblk = xs_ref.shape[0] // (tb * nch)
        lax.fori_loop(nused, nblk, lambda b, _: (desc(b * tb).start(), 0)[1], 0)
        lax.fori_loop(0, MOE_EXPERTS, zwait, 0)
        lax.fori_loop(nused, nblk, lambda b, _: (desc(b * tb).wait(), 0)[1], 0)

    step = pl.program_id(0)
    slot = step % 2
    scr = scr_ref.at[slot]
    _to_token_tiles(scr, x_ref[...])

    def start(r, _):
        for k in range(2):
            _token_copy(scr, r, xs_ref, _slot(er_ref, ps_ref, r, k), nch, sem.at[slot]).start(priority=k)
        return 0

    def drain(which):
        def wait(r, _):
            for k in range(2):
                _token_copy(scr_ref.at[which], 0, xs_ref, 0, nch, sem.at[which]).wait()
            return 0
        lax.fori_loop(0, tm, wait, 0, unroll=8)

    lax.fori_loop(0, tm, start, 0, unroll=8)

    @pl.when(step > 0)
    def _():
        drain(1 - slot)

    @pl.when(step == pl.num_programs(0) - 1)
    def _():
        drain(slot)


def _dispatch(er, pad_start, last_blk, x1, n_rows):
    n, d = x1.shape
    tm = ROW_TILE
    nch = d // LANES
    smem = lambda: pl.BlockSpec(memory_space=pltpu.SMEM)
    return pl.pallas_call(
        _dispatch_kernel, grid=(n // tm,),
        in_specs=[pl.BlockSpec((4 * tm,), lambda i: (i,), memory_space=pltpu.SMEM), smem(), smem(),
                  pl.BlockSpec((tm, d), lambda i: (i, 0))],
        out_specs=pl.BlockSpec(memory_space=pl.ANY),
        out_shape=jax.ShapeDtypeStruct((n_rows * nch, LANES), F32),
        scratch_shapes=[pltpu.VMEM((2, tm * nch, LANES), F32), pltpu.VMEM((EXPERT_TILE * nch, LANES), F32),
                        pltpu.SemaphoreType.DMA((2,)), pltpu.SemaphoreType.DMA(())],
        compiler_params=_params("arbitrary"), name="dispatch",
    )(er, pad_start, last_blk, x1)


def _expert_kernel(first_ref, count_ref, widx_ref, nused_ref, wg_ref, wu_ref, wd_ref, xs_ref, ys_ref,
                   wgb_ref, wub_ref, wdb_ref, xbuf_ref, ybuf_ref, in_sem, out_sem):
    del widx_ref
    e = pl.program_id(0)
    nused = nused_ref[0]
    d = wg_ref.shape[1]
    nch = d // LANES
    rows = xbuf_ref.shape[1]
    tb = rows // nch
    nblk = xs_ref.shape[0] // rows

    def blk(ref, b):
        return ref.at[pl.ds(pl.multiple_of(b * rows, rows), rows), :]

    def in_copy(b, slot):
        return pltpu.make_async_copy(blk(xs_ref, b), xbuf_ref.at[slot], in_sem.at[slot])

    def out_copy(b, slot):
        return pltpu.make_async_copy(ybuf_ref.at[slot], blk(ys_ref, b), out_sem.at[slot])

    n_in = xbuf_ref.shape[0]

    @pl.when(e == 0)
    def _():
        for b0 in range(n_in - 1):
            @pl.when(b0 < nused)
            def _():
                in_copy(b0, b0).start()

    @pl.when(count_ref[e] > 0)
    def _():
        wgb_ref[...] = wg_ref[0].astype(BF16)
        wub_ref[...] = wu_ref[0].astype(BF16)
        wdb_ref[...] = wd_ref[0].astype(BF16)

    def body(b, _):
        slot = b % n_in
        oslot = b % 2
        in_copy(b, slot).wait()

        @pl.when(b + n_in - 1 < nused)
        def _():
            in_copy(b + n_in - 1, (b + n_in - 1) % n_in).start()

        @pl.when(b >= 2)
        def _():
            out_copy(b - 2, oslot).wait()

        xb = jnp.concatenate([c.astype(BF16) for c in _from_token_tiles(xbuf_ref.at[slot], tb, nch)], axis=1)
        g = _dot(xb, wgb_ref[...])
        u = _dot(xb, wub_ref[...])
        hmid = g * jax.nn.sigmoid(g) * u
        _to_token_tiles(ybuf_ref.at[oslot], _dot(hmid.astype(BF16), wdb_ref[...]))
        out_copy(b, oslot).start()
        return 0

    lax.fori_loop(first_ref[e], first_ref[e] + count_ref[e], body, 0)

    @pl.when(e == pl.num_programs(0) - 1)
    def _():
        for back in (2, 1):
            @pl.when(nused >= back)
            def _():
                out_copy(nused - back, (nused - back) % 2).wait()

        ybuf_ref[0] = jnp.zeros(ybuf_ref.shape[1:], F32)
        lax.fori_loop(nused, nblk, lambda b, _: (out_copy(b, 0).start(), 0)[1], 0)
        lax.fori_loop(nused, nblk, lambda b, _: (out_copy(b, 0).wait(), 0)[1], 0)


def _experts(first_blk, blk_count, w_idx, nused, xs, w_gate, w_up, w_down):
    n_exp, d, dff = w_gate.shape
    nch = d // LANES
    rows = EXPERT_TILE * nch
    w_spec = lambda shape: pl.BlockSpec(shape, lambda e, fb, bc, wi, nu: (wi[e], 0, 0))
    any_spec = pl.BlockSpec(memory_space=pl.ANY)
    grid_spec = pltpu.PrefetchScalarGridSpec(
        num_scalar_prefetch=4, grid=(n_exp,),
        in_specs=[w_spec((1, d, dff)), w_spec((1, d, dff)), w_spec((1, dff, d)), any_spec],
        out_specs=any_spec,
        scratch_shapes=[pltpu.VMEM((d, dff), BF16), pltpu.VMEM((d, dff), BF16), pltpu.VMEM((dff, d), BF16),
                        pltpu.VMEM((EXPERT_IN_SLOTS, rows, LANES), F32), pltpu.VMEM((2, rows, LANES), F32),
                        pltpu.SemaphoreType.DMA((EXPERT_IN_SLOTS,)), pltpu.SemaphoreType.DMA((2,))],
    )
    return pl.pallas_call(
        _expert_kernel, grid_spec=grid_spec, out_shape=jax.ShapeDtypeStruct(xs.shape, F32),
        compiler_params=_params("arbitrary"), name="experts",
    )(first_blk, blk_count, w_idx, nused, w_gate, w_up, w_down, xs)


def _combine_kernel(er_ref, er_next_ref, ps_ref, x1_ref, rw_ref, g_ref, b_ref, ys_ref, o_ref, buf_ref, sem):
    tm, d = x1_ref.shape
    nch = d // LANES
    step = pl.program_id(0)
    slot = step % 2

    def gather(idx_ref, which):
        def start(r, _):
            for k in range(2):
                _token_copy(ys_ref, _slot(idx_ref, ps_ref, r, k), buf_ref.at[which, k], r, nch,
                            sem.at[which]).start(priority=k)
            return 0
        lax.fori_loop(0, tm, start, 0, unroll=8)

    @pl.when(step == 0)
    def _():
        gather(er_ref, 0)

    @pl.when(step + 1 < pl.num_programs(0))
    def _():
        gather(er_next_ref, 1 - slot)

    def wait(r, _):
        for k in range(2):
            _token_copy(ys_ref, 0, buf_ref.at[slot, k], 0, nch, sem.at[slot]).wait()
        return 0

    lax.fori_loop(0, tm, wait, 0, unroll=8)
    rw = rw_ref[...]
    y0 = jnp.concatenate(_from_token_tiles(buf_ref.at[slot, 0], tm, nch), axis=1)
    y1 = jnp.concatenate(_from_token_tiles(buf_ref.at[slot, 1], tm, nch), axis=1)
    ffn = rw[:, 0:1] * y0 + rw[:, 1:2] * y1
    o_ref[...] = _layer_norm(DEEPNORM_ALPHA * x1_ref[...] + ffn, g_ref[...], b_ref[...])


def _combine(er, pad_start, x1, rw, ln_g, ln_b, ys):
    n, d = x1.shape
    tm = ROW_TILE
    nch = d // LANES
    last = n // tm - 1
    row = lambda w: pl.BlockSpec((tm, w), lambda i: (i, 0))
    return pl.pallas_call(
        _combine_kernel, grid=(n // tm,),
        in_specs=[pl.BlockSpec((4 * tm,), lambda i: (i,), memory_space=pltpu.SMEM),
                  pl.BlockSpec((4 * tm,), lambda i: (jnp.minimum(i + 1, last),), memory_space=pltpu.SMEM),
                  pl.BlockSpec(memory_space=pltpu.SMEM),
                  row(d), row(LANES), _full(ln_g.shape), _full(ln_b.shape),
                  pl.BlockSpec(memory_space=pl.ANY)],
        out_specs=row(d),
        out_shape=jax.ShapeDtypeStruct((n, d), F32),
        scratch_shapes=[pltpu.VMEM((2, 2, tm * nch, LANES), F32), pltpu.SemaphoreType.DMA((2,))],
        compiler_params=_params("arbitrary"), name="combine",
    )(er, er, pad_start, x1, rw, ln_g, ln_b, ys)


def _rope_tables(seq):
    half = ATTN_HEAD_DIM // 2
    inv_freq = ROPE_THETA ** (-jnp.arange(half, dtype=F32) / half)
    ang = jnp.arange(seq, dtype=F32)[:, None] * inv_freq[None, :]
    cos = jnp.cos(ang)
    sin = jnp.sin(ang)
    cos_h = jnp.concatenate([cos, cos], axis=1)
    sin_h = jnp.concatenate([-sin, sin], axis=1)
    return jnp.tile(cos_h, (1, ATTN_HEADS)), jnp.tile(sin_h, (1, ATTN_HEADS))


def _pad_lanes(a, width=LANES):
    return jnp.pad(a, ((0, 0), (0, width - a.shape[1])))


def kernel(x, ln0_g, ln0_b, w_in, conv_w, conv_b, w_mq, w_mk, b_i, b_f, gn_g, skip, w_attn_up, w_mlstm_up, w_out,
           ln1_g, ln1_b, w_router_group, b_router_group, w_router_expert, b_router_expert, w_gate, w_up, w_down,
           ln2_g, ln2_b):
    batch, seq, d = x.shape
    n = batch * seq
    assert seq % ROW_TILE == 0 and ROW_TILE == MOBA_BLOCK and w_in.shape[0] == DEPTH
    x2 = x.reshape(n, d)
    vec = lambda a: a.reshape(1, -1).astype(F32)

    w = w_in[0]
    c_if = 3 * ATTN_WIDTH + 3 * MLSTM_WIDTH
    c_g = c_if + 2 * MLSTM_HEADS
    wqkv = w[:, :3 * ATTN_WIDTH].astype(BF16)
    wuvo = w[:, 3 * ATTN_WIDTH:c_if].astype(BF16)
    w_if = w[:, c_if:c_g]
    wif = _pad_lanes(w_if).astype(BF16)
    wift = w_if.T.astype(BF16)
    wg = w[:, c_g:].astype(BF16)
    cos, sin = _rope_tables(seq)

    q, k, v, kmean, u, vm, o, ifc, ift, ga, gm, xn = _inproj(
        x2, vec(ln0_g), vec(ln0_b), wqkv, wuvo, wif, wift, wg, cos, sin, batch, seq)

    nb = seq // MOBA_BLOCK
    km = kmean.reshape(batch, nb, ATTN_HEADS, ATTN_HEAD_DIM).transpose(0, 2, 1, 3)
    ya = _moba(q, k, v, km).reshape(n, ATTN_WIDTH)

    b_if = jnp.concatenate([b_i[0], b_f[0]]).astype(F32)
    ym = _mlstm(u, vm, o, ifc, ift, conv_w[0], vec(conv_b[0]), w_mq[0].transpose(0, 2, 1).astype(BF16),
                w_mk[0].astype(BF16), _pad_lanes(b_if[None, :]), b_if[:, None],
                gn_g[0].astype(F32)[:, None], skip[0].astype(F32)[:, None], batch, seq)

    w_r = _pad_lanes(jnp.concatenate([w_router_expert[0], w_router_group[0]], axis=1))
    w_r_hi = w_r.astype(BF16)
    w_r_lo = (w_r - w_r_hi.astype(F32)).astype(BF16)
    w_rc = jnp.concatenate([w_r_hi.T, w_r_lo.T], axis=0)
    b_r = _pad_lanes(jnp.concatenate([b_router_expert[0], b_router_group[0]])[None, :]).T
    x1, ri, rw, counts = _mix(
        xn, ya, ym, ga, gm, w_attn_up[0].astype(BF16), w_mlstm_up[0].astype(BF16),
        w_out[0].astype(BF16), vec(ln1_g[0]), vec(ln1_b[0]), w_rc, b_r)

    tb = EXPERT_TILE
    nblk = (2 * n) // tb + MOE_EXPERTS
    cnt = counts[:MOE_EXPERTS, 0].astype(jnp.int32)
    nblk_e = (cnt + tb - 1) // tb
    blk_end = jnp.cumsum(nblk_e)
    pad_start = (blk_end - nblk_e) * tb
    nused = blk_end[-1:]
    ids = jnp.arange(MOE_EXPERTS, dtype=jnp.int32)
    prev_used = jnp.max(jnp.where((ids[None, :] <= ids[:, None]) & (nblk_e[None, :] > 0), ids[None, :], -1), axis=1)
    first_used = jnp.min(jnp.where(nblk_e > 0, ids, MOE_EXPERTS - 1))
    w_idx = jnp.where(prev_used >= 0, prev_used, first_used).astype(jnp.int32)
    last_blk = jnp.where(nblk_e > 0, (blk_end - 1) * tb, -1)
    last_blk = jnp.concatenate([last_blk, nused]).astype(jnp.int32)
    er = ri[:, :4].reshape(4 * n)

    xs = _dispatch(er, pad_start, last_blk, x1, nblk * tb)
    ys = _experts((blk_end - nblk_e).astype(jnp.int32), nblk_e.astype(jnp.int32), w_idx, nused.astype(jnp.int32),
                  xs, w_gate[0], w_up[0], w_down[0])
    out = _combine(er, pad_start, x1, rw, vec(ln2_g[0]), vec(ln2_b[0]), ys)
    return out.reshape(batch, seq, d)
```

```python
import functools
import math

import jax
import jax.numpy as jnp
from jax import lax
from jax.experimental import pallas as pl
from jax.experimental.pallas import tpu as pltpu

F32 = jnp.float32
BF16 = jnp.bfloat16

ATTN_HEADS = 8
ATTN_HEAD_DIM = 64
ATTN_WIDTH = ATTN_HEADS * ATTN_HEAD_DIM
MOBA_BLOCK = 256
MOBA_TOPK = 3
ROPE_THETA = 10000.0
MLSTM_HEADS = 4
MLSTM_HEAD_DIM = 128
MLSTM_WIDTH = MLSTM_HEADS * MLSTM_HEAD_DIM
MLSTM_CONV = 4
MOE_GROUPS = 8
MOE_EXPERTS_PER_GROUP = 8
MOE_EXPERTS = MOE_GROUPS * MOE_EXPERTS_PER_GROUP
MOE_D_FF = 512
LN_EPS = 1e-5
GN_EPS = 1e-6
DEPTH = 1
DEEPNORM_ALPHA = (2 * DEPTH) ** 0.25

LANES = 128
SUBLANES = 8
ROW_TILE = 256
EXPERT_TILE = 256
EXPERT_IN_SLOTS = 4
INPROJ_CHAINS = 2
SLOT_TILE = 2048
MIX_CHAINS = 4
VMEM_LIMIT = 48 * 1024 * 1024
LOG2_E = math.log2(math.e)

NEG_INF = float("-inf")


def _params(*sem):
    return pltpu.CompilerParams(dimension_semantics=sem, vmem_limit_bytes=VMEM_LIMIT)


def _dot(a, b):
    return jnp.dot(a, b, preferred_element_type=F32)


def _dot_nt(a, b):
    return lax.dot_general(a, b, (((1,), (1,)), ((), ())), preferred_element_type=F32)


def _dot_tn(a, b):
    return lax.dot_general(a, b, (((0,), (0,)), ((), ())), preferred_element_type=F32)


def _split3(x):
    x1 = x.astype(BF16)
    r1 = x - x1.astype(F32)
    x2 = r1.astype(BF16)
    r2 = r1 - x2.astype(F32)
    return x1, x2, r2.astype(BF16)


def _layer_norm(x, g, b):
    mu = jnp.mean(x, axis=-1, keepdims=True)
    xc = x - mu
    var = jnp.mean(xc * xc, axis=-1, keepdims=True)
    return xc * lax.rsqrt(var + LN_EPS) * g + b


def _log_sigmoid(x):
    return jnp.minimum(x, 0.0) - jnp.log1p(jnp.exp(-jnp.abs(x)))


def _full(shape):
    nd = len(shape)
    return pl.BlockSpec(shape, lambda *_: (0,) * nd)


def _run_skewed(phases, chains, rows):
    states = [dict() for _ in range(chains)]
    for t in range(chains + len(phases) - 1):
        for c in range(chains):
            if 0 <= t - c < len(phases):
                phases[t - c](states[c], c, slice(c * rows, (c + 1) * rows))


def _inproj_kernel(x_ref, g_ref, b_ref, wqkv_ref, wuvo_ref, wif_ref, wift_ref, wg_ref, cos_ref, sin_ref,
                   q_ref, k_ref, v_ref, km_ref, u_ref, vm_ref, o_ref, ifc_ref, ift_ref, ga_ref, gm_ref, xn_ref):
    tm = ROW_TILE
    lane = lax.broadcasted_iota(jnp.int32, (tm, ATTN_WIDTH), 1)
    first_half = (lane % ATTN_HEAD_DIM) < (ATTN_HEAD_DIM // 2)

    def norm(st, c, rs):
        xn = _layer_norm(x_ref[rs, :], g_ref[...], b_ref[...])
        xn_ref[rs, :] = xn
        st["xb"] = xn.astype(BF16)

    def qkv_matmul(st, c, rs):
        st["zqkv"] = _dot(st["xb"], wqkv_ref[...])

    def attn_outputs(st, c, rs):
        zqkv = st.pop("zqkv")
        cos = cos_ref[rs, :]
        sin = sin_ref[rs, :]

        def rope(t):
            fwd = pltpu.roll(t, ATTN_WIDTH - ATTN_HEAD_DIM // 2, axis=1)
            bwd = pltpu.roll(t, ATTN_HEAD_DIM // 2, axis=1)
            return t * cos + jnp.where(first_half, fwd, bwd) * sin

        q = rope(zqkv[:, :ATTN_WIDTH]) * (ATTN_HEAD_DIM ** -0.5 * LOG2_E)
        k = rope(zqkv[:, ATTN_WIDTH:2 * ATTN_WIDTH])
        v = zqkv[:, 2 * ATTN_WIDTH:]
        km_ref[c] = jnp.mean(k, axis=0, keepdims=True)
        qt = q.T
        vt = v.T
        for h in range(ATTN_HEADS):
            sl = slice(h * ATTN_HEAD_DIM, (h + 1) * ATTN_HEAD_DIM)
            q_ref[0, h, :, rs] = qt[sl, :].astype(BF16)
            k_ref[0, h, rs, :] = k[:, sl].astype(BF16)
            v_ref[0, h, :, rs] = vt[sl, :].astype(BF16)

    def uvo_matmul(st, c, rs):
        st["zuvo"] = _dot(st["xb"], wuvo_ref[...])

    def mlstm_outputs(st, c, rs):
        zuvo = st.pop("zuvo")
        u_ref[rs, :] = zuvo[:, :MLSTM_WIDTH]
        vm_ref[:, rs] = zuvo[:, MLSTM_WIDTH:2 * MLSTM_WIDTH].T.astype(BF16)
        o_ref[:, rs] = zuvo[:, 2 * MLSTM_WIDTH:].T
        ifc_ref[rs, :] = _dot(st["xb"], wif_ref[...])
        ift_ref[:, rs] = _dot_nt(wift_ref[...], st["xb"])

    def gate_matmul(st, c, rs):
        st["zg"] = _dot(st.pop("xb"), wg_ref[...])

    def gate_outputs(st, c, rs):
        zg = st.pop("zg")
        d = ga_ref.shape[1]
        ga_ref[rs, :] = jax.nn.sigmoid(zg[:, :d]).astype(BF16)
        gm_ref[rs, :] = jax.nn.sigmoid(zg[:, d:]).astype(BF16)

    _run_skewed((norm, qkv_matmul, attn_outputs, uvo_matmul, mlstm_outputs, gate_matmul, gate_outputs),
                x_ref.shape[0] // tm, tm)


def _inproj(x2, ln_g, ln_b, wqkv, wuvo, wif, wift, wg, cos, sin, batch, seq):
    n, d = x2.shape
    chains = INPROJ_CHAINS
    tm = chains * ROW_TILE
    assert seq % tm == 0
    nsb = seq // tm
    hd = ATTN_HEAD_DIM
    row = lambda w: pl.BlockSpec((tm, w), lambda i: (i, 0))
    col = lambda h: pl.BlockSpec((h, tm), lambda i: (0, i))
    head = pl.BlockSpec((1, ATTN_HEADS, tm, hd), lambda i: (i // nsb, 0, i % nsb, 0))
    head_t = pl.BlockSpec((1, ATTN_HEADS, hd, tm), lambda i: (i // nsb, 0, 0, i % nsb))
    tab = pl.BlockSpec((tm, ATTN_WIDTH), lambda i: (i % nsb, 0))
    head_shape = jax.ShapeDtypeStruct((batch, ATTN_HEADS, seq, hd), BF16)
    head_t_shape = jax.ShapeDtypeStruct((batch, ATTN_HEADS, hd, seq), BF16)
    out_shape = (
        head_t_shape, head_shape, head_t_shape,
        jax.ShapeDtypeStruct((n // ROW_TILE, 1, ATTN_WIDTH), F32),
        jax.ShapeDtypeStruct((n, MLSTM_WIDTH), F32),
        jax.ShapeDtypeStruct((MLSTM_WIDTH, n), BF16),
        jax.ShapeDtypeStruct((MLSTM_WIDTH, n), F32),
        jax.ShapeDtypeStruct((n, LANES), F32),
        jax.ShapeDtypeStruct((SUBLANES, n), F32),
        jax.ShapeDtypeStruct((n, d), BF16),
        jax.ShapeDtypeStruct((n, d), BF16),
        jax.ShapeDtypeStruct((n, d), F32),
    )
    out_specs = (
        head_t, head, head_t,
        pl.BlockSpec((chains, 1, ATTN_WIDTH), lambda i: (i, 0, 0)),
        row(MLSTM_WIDTH), col(MLSTM_WIDTH), col(MLSTM_WIDTH),
        row(LANES),
        col(SUBLANES),
        row(d), row(d), row(d),
    )
    in_specs = [row(d), _full(ln_g.shape), _full(ln_b.shape), _full(wqkv.shape), _full(wuvo.shape),
                _full(wif.shape), _full(wift.shape), _full(wg.shape), tab, tab]
    return pl.pallas_call(
        _inproj_kernel, grid=(n // tm,), in_specs=in_specs, out_specs=out_specs, out_shape=out_shape,
        compiler_params=_params("parallel"), name="inproj",
    )(x2, ln_g, ln_b, wqkv, wuvo, wif, wift, wg, cos, sin)


def _moba_kernel(qt_ref, k_ref, vt_ref, km_ref, o_ref, bias_ref, m_ref, l_ref, acc_ref, s_ref):
    i = pl.program_id(1)
    blk = MOBA_BLOCK
    hd = ATTN_HEAD_DIM
    heads = ATTN_HEADS
    nb = k_ref.shape[2] // blk
    blk_id = lax.broadcasted_iota(jnp.int32, (nb, blk), 0)
    key_pos = lax.broadcasted_iota(jnp.int32, (blk, blk), 0)
    qry_pos = lax.broadcasted_iota(jnp.int32, (blk, blk), 1)
    causal = key_pos <= qry_pos

    for h in range(heads):
        qt = qt_ref[0, h]
        km = km_ref[0, h]
        km_hi = km.astype(BF16)
        km_lo = (km - km_hi.astype(F32)).astype(BF16)
        gate = _dot(km_hi, qt) + _dot(km_lo, qt)
        gate = jnp.where(blk_id < i, gate, NEG_INF)
        for j in range(nb - 1):
            row = gate[j:j + 1, :]
            beats = (gate > row) | ((gate == row) & (blk_id < j))
            cnt = jnp.sum(jnp.where(beats, 1.0, 0.0), axis=0, keepdims=True)
            sel = (cnt < float(MOBA_TOPK)) & (row > NEG_INF)
            bias_ref[j * heads + h] = jnp.where(sel, 0.0, NEG_INF)
    for h in range(heads):
        bias_ref[i * heads + h] = jnp.zeros((1, blk), F32)

    def scores(h, j, own_block):
        qt = qt_ref[0, h]
        half = blk // 2
        m_tile = None
        for c in range(2):
            rows = slice(c * half, (c + 1) * half)
            s = _dot(k_ref[0, h, pl.ds(pl.multiple_of(j * blk + c * half, half), half), :], qt)
            if own_block:
                s = jnp.where(causal[rows], s, NEG_INF)
            s_ref[j * heads + h, rows, :] = s
            m_c = jnp.max(s, axis=0, keepdims=True)
            m_tile = m_c if m_tile is None else jnp.maximum(m_tile, m_c)
        return m_tile

    for h in range(heads):
        m_ref[h] = scores(h, i, True)

    def past_scores(j, _):
        for h in range(heads):
            m_ref[h] = jnp.maximum(m_ref[h], scores(h, j, False) + bias_ref[j * heads + h])
        return 0

    lax.fori_loop(0, i, past_scores, 0)

    l_ref[...] = jnp.zeros_like(l_ref)
    acc_ref[...] = jnp.zeros_like(acc_ref)

    def accumulate(j, _):
        off = pl.multiple_of(j * blk, blk)
        for h in range(heads):
            p = jnp.exp2(s_ref[j * heads + h] - (m_ref[h] - bias_ref[j * heads + h]))
            l_ref[h] += jnp.sum(p, axis=0, keepdims=True)
            acc_ref[h] += _dot(vt_ref[0, h, :, pl.ds(off, blk)], p.astype(BF16))
        return 0

    lax.fori_loop(0, i + 1, accumulate, 0)
    yt = acc_ref[...] / l_ref[...]
    o_ref[0] = yt.reshape(heads * hd, blk).T.astype(BF16)


def _moba(qt, k, vt, km):
    batch, heads, seq, hd = k.shape
    blk = MOBA_BLOCK
    nb = seq // blk
    return pl.pallas_call(
        _moba_kernel, grid=(batch, nb),
        in_specs=[
            pl.BlockSpec((1, heads, hd, blk), lambda b, i: (b, 0, 0, i)),
            pl.BlockSpec((1, heads, seq, hd), lambda b, i: (b, 0, 0, 0)),
            pl.BlockSpec((1, heads, hd, seq), lambda b, i: (b, 0, 0, 0)),
            pl.BlockSpec((1, heads, nb, hd), lambda b, i: (b, 0, 0, 0)),
        ],
        out_specs=pl.BlockSpec((1, blk, heads * hd), lambda b, i: (b, i, 0)),
        out_shape=jax.ShapeDtypeStruct((batch, seq, heads * hd), BF16),
        scratch_shapes=[pltpu.VMEM((nb * heads, 1, blk), F32), pltpu.VMEM((heads, 1, blk), F32),
                        pltpu.VMEM((heads, 1, blk), F32), pltpu.VMEM((heads, hd, blk), F32),
                        pltpu.VMEM((nb * heads, blk, blk), F32)],
        compiler_params=_params("parallel", "arbitrary"), name="moba",
    )(qt, k, vt, km)


def _mlstm_kernel(u_ref, vmt_ref, ot_ref, ifc_ref, ift_ref, cw_ref, cb_ref, wqt_ref, wk_ref, brow_ref, bcol_ref,
                  gn_ref, skip_ref, y_ref, ext_ref, c_ref, n_ref, m_ref, yt_ref):
    tm = u_ref.shape[0]
    hd = MLSTM_HEAD_DIM
    halo = SUBLANES

    @pl.when(pl.program_id(1) == 0)
    def _():
        ext_ref[0:halo, :] = jnp.zeros((halo, MLSTM_WIDTH), F32)
        c_ref[...] = jnp.zeros_like(c_ref)
        n_ref[...] = jnp.zeros_like(n_ref)
        m_ref[...] = jnp.zeros_like(m_ref)

    u = u_ref[...]
    ext_ref[halo:halo + tm, :] = u
    acc = jnp.broadcast_to(cb_ref[...], u.shape)
    for j in range(MLSTM_CONV):
        acc = acc + cw_ref[j:j + 1, :] * ext_ref[halo - (MLSTM_CONV - 1) + j:halo - (MLSTM_CONV - 1) + j + tm, :]
    ext_ref[0:halo, :] = u[tm - halo:, :]
    uc = acc * jax.nn.sigmoid(acc)

    gc = ifc_ref[...] + brow_ref[...]
    gr = ift_ref[...] + bcol_ref[...]
    rows = lax.broadcasted_iota(jnp.int32, (tm, tm), 0)
    cols = lax.broadcasted_iota(jnp.int32, (tm, tm), 1)
    causal_t = rows <= cols
    tril = jnp.where(cols <= rows, 1.0, 0.0).astype(BF16)
    triu = jnp.where(causal_t, 1.0, 0.0).astype(BF16)
    c1, c2, c3 = _split3(_log_sigmoid(gc))
    bcum_c = _dot(tril, c1) + _dot(tril, c2) + _dot(tril, c3)
    r1, r2, r3 = _split3(_log_sigmoid(gr))
    bcum_r = _dot(r1, triu) + _dot(r2, triu) + _dot(r3, triu)

    uct = uc.T
    for h in range(MLSTM_HEADS):
        hs = slice(h * hd, (h + 1) * hd)
        fl = MLSTM_HEADS + h
        b_row = bcum_r[fl:fl + 1, :]
        key_row = gr[h:h + 1, :] - b_row
        key_col = gc[:, h:h + 1] - bcum_c[:, fl:fl + 1]
        m_prev = m_ref[h][:, 0:1]
        dlog = jnp.where(causal_t, key_col + b_row, NEG_INF)
        inter = b_row + m_prev
        m_t = jnp.maximum(inter, jnp.max(dlog, axis=0, keepdims=True))
        w_intra = jnp.exp(dlog - m_t)
        w_inter = jnp.exp(inter - m_t)

        uct_h = uct[hs, :]
        qtb = _dot(wqt_ref[h], uct_h.astype(BF16)).astype(BF16)
        k = _dot(uc[:, hs].astype(BF16), wk_ref[h]) * (hd ** -0.5)
        vtb = vmt_ref[hs, :]
        s = _dot(k.astype(BF16), qtb) * w_intra
        ct_prev = c_ref[h]
        n_prev = n_ref[h]
        n_hi = n_prev.astype(BF16)
        n_lo = (n_prev - n_hi.astype(F32)).astype(BF16)
        qn = (_dot(n_hi, qtb) + _dot(n_lo, qtb))[0:1, :]
        num = w_inter * _dot(ct_prev.astype(BF16), qtb) + _dot(vtb, s.astype(BF16))
        den = w_inter * qn + jnp.sum(s, axis=0, keepdims=True)
        hh = num / jnp.maximum(jnp.abs(den), jnp.exp(-m_t))

        b_end = b_row[:, tm - 1:tm]
        m_new = jnp.maximum(b_end + m_prev, jnp.max(b_end + key_row, axis=1, keepdims=True))
        decay = jnp.exp(b_end + m_prev - m_new)
        kw = k * jnp.exp(b_end + key_col - m_new)
        c_ref[h] = decay * ct_prev + _dot(vtb, kw.astype(BF16))
        n_ref[h] = decay * n_prev + jnp.broadcast_to(jnp.sum(kw, axis=0, keepdims=True), n_prev.shape)
        m_ref[h] = jnp.broadcast_to(m_new, (1, LANES))

        hh = jax.nn.sigmoid(ot_ref[hs, :]) * hh
        mu = jnp.mean(hh, axis=0, keepdims=True)
        hc = hh - mu
        var = jnp.mean(hc * hc, axis=0, keepdims=True)
        yt_ref[hs, :] = hc * lax.rsqrt(var + GN_EPS) * gn_ref[hs, :] + skip_ref[hs, :] * uct_h
    y_ref[...] = yt_ref[...].T.astype(BF16)


def _mlstm(u, vmt, ot, ifc, ift, conv_w, conv_b, wqt, wk, brow, bcol, gn_g, skip, batch, seq):
    n = u.shape[0]
    tm = ROW_TILE
    nc = seq // tm
    row = lambda w: pl.BlockSpec((tm, w), lambda b, c: (b * nc + c, 0))
    col = lambda h: pl.BlockSpec((h, tm), lambda b, c: (0, b * nc + c))
    in_specs = [row(MLSTM_WIDTH), col(MLSTM_WIDTH), col(MLSTM_WIDTH), row(LANES), col(SUBLANES),
                _full(conv_w.shape), _full(conv_b.shape), _full(wqt.shape), _full(wk.shape),
                _full(brow.shape), _full(bcol.shape), _full(gn_g.shape), _full(skip.shape)]
    return pl.pallas_call(
        _mlstm_kernel, grid=(batch, nc), in_specs=in_specs, out_specs=row(MLSTM_WIDTH),
        out_shape=jax.ShapeDtypeStruct((n, MLSTM_WIDTH), BF16),
        scratch_shapes=[pltpu.VMEM((SUBLANES + tm, MLSTM_WIDTH), F32),
                        pltpu.VMEM((MLSTM_HEADS, MLSTM_HEAD_DIM, MLSTM_HEAD_DIM), F32),
                        pltpu.VMEM((MLSTM_HEADS, SUBLANES, MLSTM_HEAD_DIM), F32),
                        pltpu.VMEM((MLSTM_HEADS, 1, LANES), F32),
                        pltpu.VMEM((MLSTM_WIDTH, tm), F32)],
        compiler_params=_params("parallel", "arbitrary"), name="mlstm",
    )(u, vmt, ot, ifc, ift, conv_w, conv_b, wqt, wk, brow, bcol, gn_g, skip)


def _mix_kernel(xn_ref, ya_ref, ym_ref, ga_ref, gm_ref, wau_ref, wmu_ref, wout_ref,
                g1_ref, b1_ref, wrc_ref, br_ref,
                x1_ref, ri_ref, rw_ref, cnt_out_ref, cnt_ref):
    @pl.when(pl.program_id(0) == 0)
    def _():
        cnt_ref[...] = jnp.zeros_like(cnt_ref)

    tm = ROW_TILE
    sub = lax.broadcasted_iota(jnp.int32, (LANES, tm), 0).astype(F32)
    big = float(4 * LANES)

    def up_and_mix(st, c, rs):
        a_up = _dot(ya_ref[rs, :], wau_ref[...])
        m_up = _dot(ym_ref[rs, :], wmu_ref[...])
        mix = ga_ref[rs, :].astype(F32) * a_up + gm_ref[rs, :].astype(F32) * m_up
        st["mix"] = mix.astype(BF16)

    def out_and_norm(st, c, rs):
        x1 = _layer_norm(DEEPNORM_ALPHA * xn_ref[rs, :] + _dot(st.pop("mix"), wout_ref[...]), g1_ref[...], b1_ref[...])
        x1_ref[rs, :] = x1
        st["x1"] = x1

    def router_logits(st, c, rs):
        x1 = st.pop("x1")
        x_hi = x1.astype(BF16)
        x_lo = (x1 - x_hi.astype(F32)).astype(BF16)
        both = _dot_nt(wrc_ref[...], x_hi)
        st["logits"] = both[:LANES] + both[LANES:] + _dot_nt(wrc_ref[:LANES, :], x_lo) + br_ref[...]

    def route(st, c, rs):
        logits = st.pop("logits")
        is_g = (sub >= float(MOE_EXPERTS)) & (sub < float(MOE_EXPERTS + MOE_GROUPS))
        gl = jnp.where(is_g, logits, NEG_INF)
        ge = jnp.exp(gl - jnp.max(gl, axis=0, keepdims=True))
        gp = ge / jnp.sum(ge, axis=0, keepdims=True)
        g_w = jnp.max(gp, axis=0, keepdims=True)
        g_idx = jnp.min(jnp.where((gp == g_w) & is_g, sub - float(MOE_EXPERTS), big), axis=0, keepdims=True)
        lo = g_idx * float(MOE_EXPERTS_PER_GROUP)
        in_grp = (sub >= lo) & (sub < lo + float(MOE_EXPERTS_PER_GROUP))
        el = jnp.where(in_grp, logits, NEG_INF)
        v1 = jnp.max(el, axis=0, keepdims=True)
        i1 = jnp.min(jnp.where((el == v1) & in_grp, sub, big), axis=0, keepdims=True)
        el2 = jnp.where(sub == i1, NEG_INF, el)
        v2 = jnp.max(el2, axis=0, keepdims=True)
        i2 = jnp.min(jnp.where((el2 == v2) & in_grp & (sub != i1), sub, big), axis=0, keepdims=True)
        e2 = jnp.exp(v2 - v1)
        w0 = g_w / (1.0 + e2)
        w1 = g_w * e2 / (1.0 + e2)
        rw_ref[rs, :] = jnp.where(sub == 0.0, w0, jnp.where(sub == 1.0, w1, 0.0)).T
        st["i1"], st["i2"] = i1, i2

    def rank(st, c, rs):
        i1, i2 = st.pop("i1"), st.pop("i2")
        is1 = sub == i1
        is2 = sub == i2
        onehot = jnp.where(is1 | is2, 1.0, 0.0)
        rows = lax.broadcasted_iota(jnp.int32, (tm, tm), 0)
        cols = lax.broadcasted_iota(jnp.int32, (tm, tm), 1)
        earlier = jnp.where(rows < cols, 1.0, 0.0).astype(BF16)
        before = _dot(onehot.astype(BF16), earlier) + cnt_ref[...]
        r0 = jnp.sum(jnp.where(is1, before, 0.0), axis=0, keepdims=True)
        r1 = jnp.sum(jnp.where(is2, before, 0.0), axis=0, keepdims=True)
        total = cnt_ref[...] + jnp.sum(onehot, axis=1, keepdims=True)
        cnt_ref[...] = total
        cnt_out_ref[...] = total
        ri_t = jnp.where(sub == 0.0, i1, jnp.where(sub == 1.0, i2, jnp.where(sub == 2.0, r0, jnp.where(sub == 3.0, r1, 0.0))))
        ri_ref[rs, :] = ri_t.T.astype(jnp.int32)

    _run_skewed((up_and_mix, out_and_norm, router_logits, route, rank), xn_ref.shape[0] // tm, tm)


def _mix(xn, ya, ym, ga, gm, wau, wmu, wout, g1, b1, wrc, br):
    n, d = xn.shape
    tm = MIX_CHAINS * ROW_TILE
    row = lambda w: pl.BlockSpec((tm, w), lambda i: (i, 0))
    in_specs = [row(d), row(ATTN_WIDTH), row(MLSTM_WIDTH), row(d), row(d),
                _full(wau.shape), _full(wmu.shape), _full(wout.shape), _full(g1.shape), _full(b1.shape),
                _full(wrc.shape), _full(br.shape)]
    out_shape = (jax.ShapeDtypeStruct((n, d), F32), jax.ShapeDtypeStruct((n, LANES), jnp.int32),
                 jax.ShapeDtypeStruct((n, LANES), F32), jax.ShapeDtypeStruct((LANES, 1), F32))
    out_specs = (row(d), row(LANES), row(LANES), _full((LANES, 1)))
    return pl.pallas_call(
        _mix_kernel, grid=(n // tm,), in_specs=in_specs, out_specs=out_specs, out_shape=out_shape,
        scratch_shapes=[pltpu.VMEM((LANES, 1), F32)],
        compiler_params=_params("arbitrary"), name="mix",
    )(xn, ya, ym, ga, gm, wau, wmu, wout, g1, b1, wrc, br)


def _to_token_tiles(dst_ref, x):
    nch = x.shape[1] // LANES
    for c in range(nch):
        dst_ref[pl.ds(c, x.shape[0], stride=nch), :] = x[:, c * LANES:(c + 1) * LANES]


def _from_token_tiles(src_ref, rows, nch):
    return [src_ref[pl.ds(c, rows, stride=nch), :] for c in range(nch)]


def _token_copy(src, src_tok, dst, dst_tok, nch, sem):
    s0 = pl.multiple_of(src_tok * nch, nch)
    d0 = pl.multiple_of(dst_tok * nch, nch)
    return pltpu.make_async_copy(src.at[pl.ds(s0, nch), :], dst.at[pl.ds(d0, nch), :], sem)


def _slots_kernel(ri_ref, ps_ref, o_ref):
    ri = ri_ref[...].astype(F32)
    lane = lax.broadcasted_iota(jnp.int32, ri.shape, 1).astype(F32)
    ps = ps_ref[...]
    out = jnp.zeros(ri.shape, F32)
    for k in range(2):
        start = jnp.sum(jnp.where(lane == ri[:, k:k + 1], ps, 0.0), axis=1, keepdims=True)
        out = jnp.where(lane == float(k), start + ri[:, 2 + k:3 + k], out)
    o_ref[...] = out.astype(jnp.int32)


def _slots(ri, pad_start_row):
    n = ri.shape[0]
    tm = SLOT_TILE
    row = pl.BlockSpec((tm, LANES), lambda i: (i, 0))
    return pl.pallas_call(
        _slots_kernel, grid=(n // tm,), in_specs=[row, _full(pad_start_row.shape)], out_specs=row,
        out_shape=jax.ShapeDtypeStruct((n, LANES), jnp.int32),
        compiler_params=_params("parallel"), name="slots",
    )(ri, pad_start_row)


def _slot(dest_ref, r, k):
    return dest_ref[2 * r + k]


def _dispatch_kernel(dest_ref, last_ref, x_ref, xs_ref, scr_ref, zero_ref, sem, zsem):
    tm, d = x_ref.shape
    nch = d // LANES
    tb = zero_ref.shape[0] // nch

    @pl.when(pl.program_id(0) == 0)
    def _():
        zero_ref[...] = jnp.zeros_like(zero_ref)

        def desc(tok):
            off = pl.multiple_of(jnp.maximum(tok, 0) * nch, nch)
            return pltpu.make_async_copy(zero_ref, xs_ref.at[pl.ds(off, tb * nch), :], zsem)

        def zstart(e, _):
            @pl.when(last_ref[e] >= 0)
            def _():
                desc(last_ref[e]).start()
            return 0

        def zwait(e, _):
            @pl.when(last_ref[e] >= 0)
            def _():
                desc(last_ref[e]).wait()
            return 0

        lax.fori_loop(0, MOE_EXPERTS, zstart, 0)
        nused = last_ref[MOE_EXPERTS]
        nblk = xs_ref.shape[0] // (tb * nch)
        lax.fori_loop(nused, nblk, lambda b, _: (desc(b * tb).start(), 0)[1], 0)
        lax.fori_loop(0, MOE_EXPERTS, zwait, 0)
        lax.fori_loop(nused, nblk, lambda b, _: (desc(b * tb).wait(), 0)[1], 0)

    step = pl.program_id(0)
    slot = step % 2
    scr = scr_ref.at[slot]
    _to_token_tiles(scr, x_ref[...])

    def start(r, _):
        for k in range(2):
            _token_copy(scr, r, xs_ref, _slot(dest_ref, r, k), nch, sem.at[slot]).start(priority=k)
        return 0

    def drain(which):
        def wait(r, _):
            for k in range(2):
                _token_copy(scr_ref.at[which], 0, xs_ref, 0, nch, sem.at[which]).wait()
            return 0
        lax.fori_loop(0, tm, wait, 0, unroll=8)

    lax.fori_loop(0, tm, start, 0, unroll=8)

    @pl.when(step > 0)
    def _():
        drain(1 - slot)

    @pl.when(step == pl.num_programs(0) - 1)
    def _():
        drain(slot)


def _dispatch(dest, last_blk, x1, n_rows):
    n, d = x1.shape
    tm = ROW_TILE
    nch = d // LANES
    return pl.pallas_call(
        _dispatch_kernel, grid=(n // tm,),
        in_specs=[pl.BlockSpec((2 * tm,), lambda i: (i,), memory_space=pltpu.SMEM),
                  pl.BlockSpec(memory_space=pltpu.SMEM),
                  pl.BlockSpec((tm, d), lambda i: (i, 0))],
        out_specs=pl.BlockSpec(memory_space=pl.ANY),
        out_shape=jax.ShapeDtypeStruct((n_rows * nch, LANES), F32),
        scratch_shapes=[pltpu.VMEM((2, tm * nch, LANES), F32), pltpu.VMEM((EXPERT_TILE * nch, LANES), F32),
                        pltpu.SemaphoreType.DMA((2,)), pltpu.SemaphoreType.DMA(())],
        compiler_params=_params("arbitrary"), name="dispatch",
    )(dest, last_blk, x1)


def _expert_kernel(first_ref, count_ref, widx_ref, nused_ref, wg_ref, wu_ref, wd_ref, xs_ref, ys_ref,
                   wgb_ref, wub_ref, wdb_ref, xbuf_ref, ybuf_ref, in_sem, out_sem):
    del widx_ref
    e = pl.program_id(0)
    nused = nused_ref[0]
    d = wg_ref.shape[1]
    nch = d // LANES
    rows = xbuf_ref.shape[1]
    tb = rows // nch
    nblk = xs_ref.shape[0] // rows

    def blk(ref, b):
        return ref.at[pl.ds(pl.multiple_of(b * rows, rows), rows), :]

    def in_copy(b, slot):
        return pltpu.make_async_copy(blk(xs_ref, b), xbuf_ref.at[slot], in_sem.at[slot])

    def out_copy(b, slot):
        return pltpu.make_async_copy(ybuf_ref.at[slot], blk(ys_ref, b), out_sem.at[slot])

    n_in = xbuf_ref.shape[0]

    @pl.when(e == 0)
    def _():
        for b0 in range(n_in - 1):
            @pl.when(b0 < nused)
            def _():
                in_copy(b0, b0).start()

    @pl.when(count_ref[e] > 0)
    def _():
        wgb_ref[...] = wg_ref[0].astype(BF16)
        wub_ref[...] = wu_ref[0].astype(BF16)
        wdb_ref[...] = wd_ref[0].astype(BF16)

    def body(b, _):
        slot = b % n_in
        oslot = b % 2
        in_copy(b, slot).wait()

        @pl.when(b + n_in - 1 < nused)
        def _():
            in_copy(b + n_in - 1, (b + n_in - 1) % n_in).start()

        @pl.when(b >= 2)
        def _():
            out_copy(b - 2, oslot).wait()

        xb = jnp.concatenate([c.astype(BF16) for c in _from_token_tiles(xbuf_ref.at[slot], tb, nch)], axis=1)
        g = _dot(xb, wgb_ref[...])
        u = _dot(xb, wub_ref[...])
        hmid = g * jax.nn.sigmoid(g) * u
        _to_token_tiles(ybuf_ref.at[oslot], _dot(hmid.astype(BF16), wdb_ref[...]))
        out_copy(b, oslot).start()
        return 0

    lax.fori_loop(first_ref[e], first_ref[e] + count_ref[e], body, 0)

    @pl.when(e == pl.num_programs(0) - 1)
    def _():
        for back in (2, 1):
            @pl.when(nused >= back)
            def _():
                out_copy(nused - back, (nused - back) % 2).wait()

        ybuf_ref[0] = jnp.zeros(ybuf_ref.shape[1:], F32)
        lax.fori_loop(nused, nblk, lambda b, _: (out_copy(b, 0).start(), 0)[1], 0)
        lax.fori_loop(nused, nblk, lambda b, _: (out_copy(b, 0).wait(), 0)[1], 0)


def _experts(first_blk, blk_count, w_idx, nused, xs, w_gate, w_up, w_down):
    n_exp, d, dff = w_gate.shape
    nch = d // LANES
    rows = EXPERT_TILE * nch
    w_spec = lambda shape: pl.BlockSpec(shape, lambda e, fb, bc, wi, nu: (wi[e], 0, 0))
    any_spec = pl.BlockSpec(memory_space=pl.ANY)
    grid_spec = pltpu.PrefetchScalarGridSpec(
        num_scalar_prefetch=4, grid=(n_exp,),
        in_specs=[w_spec((1, d, dff)), w_spec((1, d, dff)), w_spec((1, dff, d)), any_spec],
        out_specs=any_spec,
        scratch_shapes=[pltpu.VMEM((d, dff), BF16), pltpu.VMEM((d, dff), BF16), pltpu.VMEM((dff, d), BF16),
                        pltpu.VMEM((EXPERT_IN_SLOTS, rows, LANES), F32), pltpu.VMEM((2, rows, LANES), F32),
                        pltpu.SemaphoreType.DMA((EXPERT_IN_SLOTS,)), pltpu.SemaphoreType.DMA((2,))],
    )
    return pl.pallas_call(
        _expert_kernel, grid_spec=grid_spec, out_shape=jax.ShapeDtypeStruct(xs.shape, F32),
        compiler_params=_params("arbitrary"), name="experts",
    )(first_blk, blk_count, w_idx, nused, w_gate, w_up, w_down, xs)


def _combine_kernel(dest_ref, dest_next_ref, x1_ref, rw_ref, g_ref, b_ref, ys_ref, o_ref, buf_ref, sem):
    tm, d = x1_ref.shape
    nch = d // LANES
    step = pl.program_id(0)
    slot = step % 2

    def gather(idx_ref, which):
        def start(r, _):
            for k in range(2):
                _token_copy(ys_ref, _slot(idx_ref, r, k), buf_ref.at[which, k], r, nch,
                            sem.at[which]).start(priority=k)
            return 0
        lax.fori_loop(0, tm, start, 0, unroll=8)

    @pl.when(step == 0)
    def _():
        gather(dest_ref, 0)

    @pl.when(step + 1 < pl.num_programs(0))
    def _():
        gather(dest_next_ref, 1 - slot)

    def wait(r, _):
        for k in range(2):
            _token_copy(ys_ref, 0, buf_ref.at[slot, k], 0, nch, sem.at[slot]).wait()
        return 0

    lax.fori_loop(0, tm, wait, 0, unroll=8)
    rw = rw_ref[...]
    y0 = jnp.concatenate(_from_token_tiles(buf_ref.at[slot, 0], tm, nch), axis=1)
    y1 = jnp.concatenate(_from_token_tiles(buf_ref.at[slot, 1], tm, nch), axis=1)
    ffn = rw[:, 0:1] * y0 + rw[:, 1:2] * y1
    o_ref[...] = _layer_norm(DEEPNORM_ALPHA * x1_ref[...] + ffn, g_ref[...], b_ref[...])


def _combine(dest, x1, rw, ln_g, ln_b, ys):
    n, d = x1.shape
    tm = ROW_TILE
    nch = d // LANES
    last = n // tm - 1
    row = lambda w: pl.BlockSpec((tm, w), lambda i: (i, 0))
    return pl.pallas_call(
        _combine_kernel, grid=(n // tm,),
        in_specs=[pl.BlockSpec((2 * tm,), lambda i: (i,), memory_space=pltpu.SMEM),
                  pl.BlockSpec((2 * tm,), lambda i: (jnp.minimum(i + 1, last),), memory_space=pltpu.SMEM),
                  row(d), row(LANES), _full(ln_g.shape), _full(ln_b.shape),
                  pl.BlockSpec(memory_space=pl.ANY)],
        out_specs=row(d),
        out_shape=jax.ShapeDtypeStruct((n, d), F32),
        scratch_shapes=[pltpu.VMEM((2, 2, tm * nch, LANES), F32), pltpu.SemaphoreType.DMA((2,))],
        compiler_params=_params("arbitrary"), name="combine",
    )(dest, dest, x1, rw, ln_g, ln_b, ys)


def _rope_tables(seq):
    half = ATTN_HEAD_DIM // 2
    inv_freq = ROPE_THETA ** (-jnp.arange(half, dtype=F32) / half)
    ang = jnp.arange(seq, dtype=F32)[:, None] * inv_freq[None, :]
    cos = jnp.cos(ang)
    sin = jnp.sin(ang)
    cos_h = jnp.concatenate([cos, cos], axis=1)
    sin_h = jnp.concatenate([-sin, sin], axis=1)
    return jnp.tile(cos_h, (1, ATTN_HEADS)), jnp.tile(sin_h, (1, ATTN_HEADS))


def _pad_lanes(a, width=LANES):
    return jnp.pad(a, ((0, 0), (0, width - a.shape[1])))


def kernel(x, ln0_g, ln0_b, w_in, conv_w, conv_b, w_mq, w_mk, b_i, b_f, gn_g, skip, w_attn_up, w_mlstm_up, w_out,
           ln1_g, ln1_b, w_router_group, b_router_group, w_router_expert, b_router_expert, w_gate, w_up, w_down,
           ln2_g, ln2_b):
    batch, seq, d = x.shape
    n = batch * seq
    assert seq % ROW_TILE == 0 and ROW_TILE == MOBA_BLOCK and w_in.shape[0] == DEPTH
    x2 = x.reshape(n, d)
    vec = lambda a: a.reshape(1, -1).astype(F32)

    w = w_in[0]
    c_if = 3 * ATTN_WIDTH + 3 * MLSTM_WIDTH
    c_g = c_if + 2 * MLSTM_HEADS
    wqkv = w[:, :3 * ATTN_WIDTH].astype(BF16)
    wuvo = w[:, 3 * ATTN_WIDTH:c_if].astype(BF16)
    w_if = w[:, c_if:c_g]
    wif = _pad_lanes(w_if).astype(BF16)
    wift = w_if.T.astype(BF16)
    wg = w[:, c_g:].astype(BF16)
    cos, sin = _rope_tables(seq)

    q, k, v, kmean, u, vm, o, ifc, ift, ga, gm, xn = _inproj(
        x2, vec(ln0_g), vec(ln0_b), wqkv, wuvo, wif, wift, wg, cos, sin, batch, seq)

    nb = seq // MOBA_BLOCK
    km = kmean.reshape(batch, nb, ATTN_HEADS, ATTN_HEAD_DIM).transpose(0, 2, 1, 3)
    ya = _moba(q, k, v, km).reshape(n, ATTN_WIDTH)

    b_if = jnp.concatenate([b_i[0], b_f[0]]).astype(F32)
    ym = _mlstm(u, vm, o, ifc, ift, conv_w[0], vec(conv_b[0]), w_mq[0].transpose(0, 2, 1).astype(BF16),
                w_mk[0].astype(BF16), _pad_lanes(b_if[None, :]), b_if[:, None],
                gn_g[0].astype(F32)[:, None], skip[0].astype(F32)[:, None], batch, seq)

    w_r = _pad_lanes(jnp.concatenate([w_router_expert[0], w_router_group[0]], axis=1))
    w_r_hi = w_r.astype(BF16)
    w_r_lo = (w_r - w_r_hi.astype(F32)).astype(BF16)
    w_rc = jnp.concatenate([w_r_hi.T, w_r_lo.T], axis=0)
    b_r = _pad_lanes(jnp.concatenate([b_router_expert[0], b_router_group[0]])[None, :]).T
    x1, ri, rw, counts = _mix(
        xn, ya, ym, ga, gm, w_attn_up[0].astype(BF16), w_mlstm_up[0].astype(BF16),
        w_out[0].astype(BF16), vec(ln1_g[0]), vec(ln1_b[0]), w_rc, b_r)

    tb = EXPERT_TILE
    nblk = (2 * n) // tb + MOE_EXPERTS
    cnt = counts[:MOE_EXPERTS, 0].astype(jnp.int32)
    nblk_e = (cnt + tb - 1) // tb
    blk_end = jnp.cumsum(nblk_e)
    pad_start = (blk_end - nblk_e) * tb
    nused = blk_end[-1:]
    ids = jnp.arange(MOE_EXPERTS, dtype=jnp.int32)
    prev_used = jnp.max(jnp.where((ids[None, :] <= ids[:, None]) & (nblk_e[None, :] > 0), ids[None, :], -1), axis=1)
    first_used = jnp.min(jnp.where(nblk_e > 0, ids, MOE_EXPERTS - 1))
    w_idx = jnp.where(prev_used >= 0, prev_used, first_used).astype(jnp.int32)
    last_blk = jnp.where(nblk_e > 0, (blk_end - 1) * tb, -1)
    last_blk = jnp.concatenate([last_blk, nused]).astype(jnp.int32)
    ps_row = _pad_lanes(pad_start[None, :].astype(F32))
    dest = _slots(ri, ps_row)[:, :2].reshape(2 * n)

    xs = _dispatch(dest, last_blk, x1, nblk * tb)
    ys = _experts((blk_end - nblk_e).astype(jnp.int32), nblk_e.astype(jnp.int32), w_idx, nused.astype(jnp.int32),
                  xs, w_gate[0], w_up[0], w_down[0])
    out = _combine(dest, x1, rw, vec(ln2_g[0]), vec(ln2_b[0]), ys)
    return out.reshape(batch, seq, d)
```

```python
import functools
import math

import jax
import jax.numpy as jnp
from jax import lax
from jax.experimental import pallas as pl
from jax.experimental.pallas import tpu as pltpu

F32 = jnp.float32
BF16 = jnp.bfloat16

ATTN_HEADS = 8
ATTN_HEAD_DIM = 64
ATTN_WIDTH = ATTN_HEADS * ATTN_HEAD_DIM
MOBA_BLOCK = 256
MOBA_TOPK = 3
ROPE_THETA = 10000.0
MLSTM_HEADS = 4
MLSTM_HEAD_DIM = 128
MLSTM_WIDTH = MLSTM_HEADS * MLSTM_HEAD_DIM
MLSTM_CONV = 4
MOE_GROUPS = 8
MOE_EXPERTS_PER_GROUP = 8
MOE_EXPERTS = MOE_GROUPS * MOE_EXPERTS_PER_GROUP
MOE_D_FF = 512
LN_EPS = 1e-5
GN_EPS = 1e-6
DEPTH = 1
DEEPNORM_ALPHA = (2 * DEPTH) ** 0.25

LANES = 128
SUBLANES = 8
ROW_TILE = 256
EXPERT_TILE = 256
EXPERT_IN_SLOTS = 4
INPROJ_CHAINS = 2
SLOT_TILE = 2048
MIX_CHAINS = 4
VMEM_LIMIT = 48 * 1024 * 1024
LOG2_E = math.log2(math.e)

NEG_INF = float("-inf")


def _params(*sem):
    return pltpu.CompilerParams(dimension_semantics=sem, vmem_limit_bytes=VMEM_LIMIT)


def _dot(a, b):
    return jnp.dot(a, b, preferred_element_type=F32)


def _dot_nt(a, b):
    return lax.dot_general(a, b, (((1,), (1,)), ((), ())), preferred_element_type=F32)


def _dot_tn(a, b):
    return lax.dot_general(a, b, (((0,), (0,)), ((), ())), preferred_element_type=F32)


def _split3(x):
    x1 = x.astype(BF16)
    r1 = x - x1.astype(F32)
    x2 = r1.astype(BF16)
    r2 = r1 - x2.astype(F32)
    return x1, x2, r2.astype(BF16)


def _layer_norm(x, g, b):
    mu = jnp.mean(x, axis=-1, keepdims=True)
    xc = x - mu
    var = jnp.mean(xc * xc, axis=-1, keepdims=True)
    return xc * lax.rsqrt(var + LN_EPS) * g + b


def _log_sigmoid(x):
    return jnp.minimum(x, 0.0) - jnp.log1p(jnp.exp(-jnp.abs(x)))


def _full(shape):
    nd = len(shape)
    return pl.BlockSpec(shape, lambda *_: (0,) * nd)


def _run_skewed(phases, chains, rows):
    states = [dict() for _ in range(chains)]
    for t in range(chains + len(phases) - 1):
        for c in range(chains):
            if 0 <= t - c < len(phases):
                phases[t - c](states[c], c, slice(c * rows, (c + 1) * rows))


def _inproj_kernel(x_ref, g_ref, b_ref, wqkv_ref, wuvo_ref, wif_ref, wift_ref, wg_ref, cos_ref, sin_ref,
                   q_ref, k_ref, v_ref, km_ref, u_ref, vm_ref, o_ref, ifc_ref, ift_ref, ga_ref, gm_ref, xn_ref):
    tm = ROW_TILE
    lane = lax.broadcasted_iota(jnp.int32, (tm, ATTN_WIDTH), 1)
    first_half = (lane % ATTN_HEAD_DIM) < (ATTN_HEAD_DIM // 2)

    def norm(st, c, rs):
        xn = _layer_norm(x_ref[rs, :], g_ref[...], b_ref[...])
        xn_ref[rs, :] = xn
        st["xb"] = xn.astype(BF16)

    def qkv_matmul(st, c, rs):
        st["zqkv"] = _dot(st["xb"], wqkv_ref[...])

    def attn_outputs(st, c, rs):
        zqkv = st.pop("zqkv")
        cos = cos_ref[rs, :]
        sin = sin_ref[rs, :]

        def rope(t):
            fwd = pltpu.roll(t, ATTN_WIDTH - ATTN_HEAD_DIM // 2, axis=1)
            bwd = pltpu.roll(t, ATTN_HEAD_DIM // 2, axis=1)
            return t * cos + jnp.where(first_half, fwd, bwd) * sin

        q = rope(zqkv[:, :ATTN_WIDTH]) * (ATTN_HEAD_DIM ** -0.5 * LOG2_E)
        k = rope(zqkv[:, ATTN_WIDTH:2 * ATTN_WIDTH])
        v = zqkv[:, 2 * ATTN_WIDTH:]
        km_ref[c] = jnp.mean(k, axis=0, keepdims=True)
        qt = q.T
        vt = v.T
        for h in range(ATTN_HEADS):
            sl = slice(h * ATTN_HEAD_DIM, (h + 1) * ATTN_HEAD_DIM)
            q_ref[0, h, :, rs] = qt[sl, :].astype(BF16)
            k_ref[0, h, rs, :] = k[:, sl].astype(BF16)
            v_ref[0, h, :, rs] = vt[sl, :].astype(BF16)

    def uvo_matmul(st, c, rs):
        st["zuvo"] = _dot(st["xb"], wuvo_ref[...])

    def mlstm_outputs(st, c, rs):
        zuvo = st.pop("zuvo")
        u_ref[rs, :] = zuvo[:, :MLSTM_WIDTH]
        vm_ref[:, rs] = zuvo[:, MLSTM_WIDTH:2 * MLSTM_WIDTH].T.astype(BF16)
        o_ref[:, rs] = zuvo[:, 2 * MLSTM_WIDTH:].T
        ifc_ref[rs, :] = _dot(st["xb"], wif_ref[...])
        ift_ref[:, rs] = _dot_nt(wift_ref[...], st["xb"])

    def gate_matmul(st, c, rs):
        st["zg"] = _dot(st.pop("xb"), wg_ref[...])

    def gate_outputs(st, c, rs):
        zg = st.pop("zg")
        d = ga_ref.shape[1]
        ga_ref[rs, :] = jax.nn.sigmoid(zg[:, :d]).astype(BF16)
        gm_ref[rs, :] = jax.nn.sigmoid(zg[:, d:]).astype(BF16)

    _run_skewed((norm, qkv_matmul, attn_outputs, uvo_matmul, mlstm_outputs, gate_matmul, gate_outputs),
                x_ref.shape[0] // tm, tm)


def _inproj(x2, ln_g, ln_b, wqkv, wuvo, wif, wift, wg, cos, sin, batch, seq):
    n, d = x2.shape
    chains = INPROJ_CHAINS
    tm = chains * ROW_TILE
    assert seq % tm == 0
    nsb = seq // tm
    hd = ATTN_HEAD_DIM
    row = lambda w: pl.BlockSpec((tm, w), lambda i: (i, 0))
    col = lambda h: pl.BlockSpec((h, tm), lambda i: (0, i))
    head = pl.BlockSpec((1, ATTN_HEADS, tm, hd), lambda i: (i // nsb, 0, i % nsb, 0))
    head_t = pl.BlockSpec((1, ATTN_HEADS, hd, tm), lambda i: (i // nsb, 0, 0, i % nsb))
    tab = pl.BlockSpec((tm, ATTN_WIDTH), lambda i: (i % nsb, 0))
    head_shape = jax.ShapeDtypeStruct((batch, ATTN_HEADS, seq, hd), BF16)
    head_t_shape = jax.ShapeDtypeStruct((batch, ATTN_HEADS, hd, seq), BF16)
    out_shape = (
        head_t_shape, head_shape, head_t_shape,
        jax.ShapeDtypeStruct((n // ROW_TILE, 1, ATTN_WIDTH), F32),
        jax.ShapeDtypeStruct((n, MLSTM_WIDTH), F32),
        jax.ShapeDtypeStruct((MLSTM_WIDTH, n), BF16),
        jax.ShapeDtypeStruct((MLSTM_WIDTH, n), F32),
        jax.ShapeDtypeStruct((n, LANES), F32),
        jax.ShapeDtypeStruct((SUBLANES, n), F32),
        jax.ShapeDtypeStruct((n, d), BF16),
        jax.ShapeDtypeStruct((n, d), BF16),
        jax.ShapeDtypeStruct((n, d), F32),
    )
    out_specs = (
        head_t, head, head_t,
        pl.BlockSpec((chains, 1, ATTN_WIDTH), lambda i: (i, 0, 0)),
        row(MLSTM_WIDTH), col(MLSTM_WIDTH), col(MLSTM_WIDTH),
        row(LANES),
        col(SUBLANES),
        row(d), row(d), row(d),
    )
    in_specs = [row(d), _full(ln_g.shape), _full(ln_b.shape), _full(wqkv.shape), _full(wuvo.shape),
                _full(wif.shape), _full(wift.shape), _full(wg.shape), tab, tab]
    return pl.pallas_call(
        _inproj_kernel, grid=(n // tm,), in_specs=in_specs, out_specs=out_specs, out_shape=out_shape,
        compiler_params=_params("parallel"), name="inproj",
    )(x2, ln_g, ln_b, wqkv, wuvo, wif, wift, wg, cos, sin)


def _moba_kernel(qt_ref, k_ref, vt_ref, km_ref, o_ref, bias_ref, m_ref, l_ref, acc_ref, s_ref):
    i = pl.program_id(1)
    blk = MOBA_BLOCK
    hd = ATTN_HEAD_DIM
    heads = ATTN_HEADS
    nb = k_ref.shape[2] // blk
    blk_id = lax.broadcasted_iota(jnp.int32, (nb, blk), 0)
    key_pos = lax.broadcasted_iota(jnp.int32, (blk, blk), 0)
    qry_pos = lax.broadcasted_iota(jnp.int32, (blk, blk), 1)
    causal = key_pos <= qry_pos

    for h in range(heads):
        qt = qt_ref[0, h]
        km = km_ref[0, h]
        km_hi = km.astype(BF16)
        km_lo = (km - km_hi.astype(F32)).astype(BF16)
        gate = _dot(km_hi, qt) + _dot(km_lo, qt)
        gate = jnp.where(blk_id < i, gate, NEG_INF)
        for j in range(nb - 1):
            row = gate[j:j + 1, :]
            beats = (gate > row) | ((gate == row) & (blk_id < j))
            cnt = jnp.sum(jnp.where(beats, 1.0, 0.0), axis=0, keepdims=True)
            sel = (cnt < float(MOBA_TOPK)) & (row > NEG_INF)
            bias_ref[j * heads + h] = jnp.where(sel, 0.0, NEG_INF)
    for h in range(heads):
        bias_ref[i * heads + h] = jnp.zeros((1, blk), F32)

    def scores(h, j, own_block):
        qt = qt_ref[0, h]
        half = blk // 2
        m_tile = None
        for c in range(2):
            rows = slice(c * half, (c + 1) * half)
            s = _dot(k_ref[0, h, pl.ds(pl.multiple_of(j * blk + c * half, half), half), :], qt)
            if own_block:
                s = jnp.where(causal[rows], s, NEG_INF)
            s_ref[j * heads + h, rows, :] = s
            m_c = jnp.max(s, axis=0, keepdims=True)
            m_tile = m_c if m_tile is None else jnp.maximum(m_tile, m_c)
        return m_tile

    for h in range(heads):
        m_ref[h] = scores(h, i, True)

    def past_scores(j, _):
        for h in range(heads):
            m_ref[h] = jnp.maximum(m_ref[h], scores(h, j, False) + bias_ref[j * heads + h])
        return 0

    lax.fori_loop(0, i, past_scores, 0)

    l_ref[...] = jnp.zeros_like(l_ref)
    acc_ref[...] = jnp.zeros_like(acc_ref)

    def accumulate(j, _):
        off = pl.multiple_of(j * blk, blk)
        for h in range(heads):
            p = jnp.exp2(s_ref[j * heads + h] - (m_ref[h] - bias_ref[j * heads + h]))
            l_ref[h] += jnp.sum(p, axis=0, keepdims=True)
            acc_ref[h] += _dot(vt_ref[0, h, :, pl.ds(off, blk)], p.astype(BF16))
        return 0

    lax.fori_loop(0, i + 1, accumulate, 0)
    yt = acc_ref[...] / l_ref[...]
    o_ref[0] = yt.reshape(heads * hd, blk).T.astype(BF16)


def _moba(qt, k, vt, km):
    batch, heads, seq, hd = k.shape
    blk = MOBA_BLOCK
    nb = seq // blk
    return pl.pallas_call(
        _moba_kernel, grid=(batch, nb),
        in_specs=[
            pl.BlockSpec((1, heads, hd, blk), lambda b, i: (b, 0, 0, i)),
            pl.BlockSpec((1, heads, seq, hd), lambda b, i: (b, 0, 0, 0)),
            pl.BlockSpec((1, heads, hd, seq), lambda b, i: (b, 0, 0, 0)),
            pl.BlockSpec((1, heads, nb, hd), lambda b, i: (b, 0, 0, 0)),
        ],
        out_specs=pl.BlockSpec((1, blk, heads * hd), lambda b, i: (b, i, 0)),
        out_shape=jax.ShapeDtypeStruct((batch, seq, heads * hd), BF16),
        scratch_shapes=[pltpu.VMEM((nb * heads, 1, blk), F32), pltpu.VMEM((heads, 1, blk), F32),
                        pltpu.VMEM((heads, 1, blk), F32), pltpu.VMEM((heads, hd, blk), F32),
                        pltpu.VMEM((nb * heads, blk, blk), F32)],
        compiler_params=_params("parallel", "arbitrary"), name="moba",
    )(qt, k, vt, km)


def _mlstm_kernel(u_ref, vmt_ref, ot_ref, ifc_ref, ift_ref, cw_ref, cb_ref, wqt_ref, wk_ref, brow_ref, bcol_ref,
                  gn_ref, skip_ref, y_ref, ext_ref, c_ref, n_ref, m_ref, yt_ref):
    tm = u_ref.shape[0]
    hd = MLSTM_HEAD_DIM
    halo = SUBLANES

    @pl.when(pl.program_id(1) == 0)
    def _():
        ext_ref[0:halo, :] = jnp.zeros((halo, MLSTM_WIDTH), F32)
        c_ref[...] = jnp.zeros_like(c_ref)
        n_ref[...] = jnp.zeros_like(n_ref)
        m_ref[...] = jnp.zeros_like(m_ref)

    u = u_ref[...]
    ext_ref[halo:halo + tm, :] = u
    acc = jnp.broadcast_to(cb_ref[...], u.shape)
    for j in range(MLSTM_CONV):
        acc = acc + cw_ref[j:j + 1, :] * ext_ref[halo - (MLSTM_CONV - 1) + j:halo - (MLSTM_CONV - 1) + j + tm, :]
    ext_ref[0:halo, :] = u[tm - halo:, :]
    uc = acc * jax.nn.sigmoid(acc)

    gc = ifc_ref[...] + brow_ref[...]
    gr = ift_ref[...] + bcol_ref[...]
    rows = lax.broadcasted_iota(jnp.int32, (tm, tm), 0)
    cols = lax.broadcasted_iota(jnp.int32, (tm, tm), 1)
    causal_t = rows <= cols
    tril = jnp.where(cols <= rows, 1.0, 0.0).astype(BF16)
    triu = jnp.where(causal_t, 1.0, 0.0).astype(BF16)
    c1, c2, c3 = _split3(_log_sigmoid(gc))
    bcum_c = _dot(tril, c1) + _dot(tril, c2) + _dot(tril, c3)
    r1, r2, r3 = _split3(_log_sigmoid(gr))
    bcum_r = _dot(r1, triu) + _dot(r2, triu) + _dot(r3, triu)

    uct = uc.T

    def decay_weights(st, h, hs):
        fl = MLSTM_HEADS + h
        b_row = bcum_r[fl:fl + 1, :]
        st["key_row"] = gr[h:h + 1, :] - b_row
        st["key_col"] = gc[:, h:h + 1] - bcum_c[:, fl:fl + 1]
        m_prev = m_ref[h][:, 0:1]
        dlog = jnp.where(causal_t, st["key_col"] + b_row, NEG_INF)
        inter = b_row + m_prev
        m_t = jnp.maximum(inter, jnp.max(dlog, axis=0, keepdims=True))
        st["w_intra"] = jnp.exp(dlog - m_t)
        st["w_inter"] = jnp.exp(inter - m_t)
        st["m_t"], st["m_prev"], st["b_end"] = m_t, m_prev, b_row[:, tm - 1:tm]

    def project(st, h, hs):
        st["qtb"] = _dot(wqt_ref[h], uct[hs, :].astype(BF16)).astype(BF16)
        st["k"] = _dot(uc[:, hs].astype(BF16), wk_ref[h]) * (hd ** -0.5)

    def scores(st, h, hs):
        st["s"] = _dot(st["k"].astype(BF16), st["qtb"]) * st.pop("w_intra")

    def readout(st, h, hs):
        qtb, s, w_inter, m_t = st.pop("qtb"), st.pop("s"), st.pop("w_inter"), st.pop("m_t")
        n_prev = n_ref[h]
        n_hi = n_prev.astype(BF16)
        n_lo = (n_prev - n_hi.astype(F32)).astype(BF16)
        qn = (_dot(n_hi, qtb) + _dot(n_lo, qtb))[0:1, :]
        num = w_inter * _dot(c_ref[h].astype(BF16), qtb) + _dot(vmt_ref[hs, :], s.astype(BF16))
        den = w_inter * qn + jnp.sum(s, axis=0, keepdims=True)
        st["hh"] = num / jnp.maximum(jnp.abs(den), jnp.exp(-m_t))

    def update_state(st, h, hs):
        b_end, m_prev = st.pop("b_end"), st.pop("m_prev")
        m_new = jnp.maximum(b_end + m_prev, jnp.max(b_end + st.pop("key_row"), axis=1, keepdims=True))
        decay = jnp.exp(b_end + m_prev - m_new)
        kw = st.pop("k") * jnp.exp(b_end + st.pop("key_col") - m_new)
        n_prev = n_ref[h]
        c_ref[h] = decay * c_ref[h] + _dot(vmt_ref[hs, :], kw.astype(BF16))
        n_ref[h] = decay * n_prev + jnp.broadcast_to(jnp.sum(kw, axis=0, keepdims=True), n_prev.shape)
        m_ref[h] = jnp.broadcast_to(m_new, (1, LANES))

    def gate_and_norm(st, h, hs):
        hh = jax.nn.sigmoid(ot_ref[hs, :]) * st.pop("hh")
        mu = jnp.mean(hh, axis=0, keepdims=True)
        hc = hh - mu
        var = jnp.mean(hc * hc, axis=0, keepdims=True)
        yt_ref[hs, :] = hc * lax.rsqrt(var + GN_EPS) * gn_ref[hs, :] + skip_ref[hs, :] * uct[hs, :]

    _run_skewed((decay_weights, project, scores, readout, update_state, gate_and_norm), MLSTM_HEADS, hd)
    y_ref[...] = yt_ref[...].T.astype(BF16)


def _mlstm(u, vmt, ot, ifc, ift, conv_w, conv_b, wqt, wk, brow, bcol, gn_g, skip, batch, seq):
    n = u.shape[0]
    tm = ROW_TILE
    nc = seq // tm
    row = lambda w: pl.BlockSpec((tm, w), lambda b, c: (b * nc + c, 0))
    col = lambda h: pl.BlockSpec((h, tm), lambda b, c: (0, b * nc + c))
    in_specs = [row(MLSTM_WIDTH), col(MLSTM_WIDTH), col(MLSTM_WIDTH), row(LANES), col(SUBLANES),
                _full(conv_w.shape), _full(conv_b.shape), _full(wqt.shape), _full(wk.shape),
                _full(brow.shape), _full(bcol.shape), _full(gn_g.shape), _full(skip.shape)]
    return pl.pallas_call(
        _mlstm_kernel, grid=(batch, nc), in_specs=in_specs, out_specs=row(MLSTM_WIDTH),
        out_shape=jax.ShapeDtypeStruct((n, MLSTM_WIDTH), BF16),
        scratch_shapes=[pltpu.VMEM((SUBLANES + tm, MLSTM_WIDTH), F32),
                        pltpu.VMEM((MLSTM_HEADS, MLSTM_HEAD_DIM, MLSTM_HEAD_DIM), F32),
                        pltpu.VMEM((MLSTM_HEADS, SUBLANES, MLSTM_HEAD_DIM), F32),
                        pltpu.VMEM((MLSTM_HEADS, 1, LANES), F32),
                        pltpu.VMEM((MLSTM_WIDTH, tm), F32)],
        compiler_params=_params("parallel", "arbitrary"), name="mlstm",
    )(u, vmt, ot, ifc, ift, conv_w, conv_b, wqt, wk, brow, bcol, gn_g, skip)


def _mix_kernel(xn_ref, ya_ref, ym_ref, ga_ref, gm_ref, wau_ref, wmu_ref, wout_ref,
                g1_ref, b1_ref, wrc_ref, br_ref,
                x1_ref, ri_ref, rw_ref, cnt_out_ref, cnt_ref):
    @pl.when(pl.program_id(0) == 0)
    def _():
        cnt_ref[...] = jnp.zeros_like(cnt_ref)

    tm = ROW_TILE
    sub = lax.broadcasted_iota(jnp.int32, (LANES, tm), 0).astype(F32)
    big = float(4 * LANES)

    def up_and_mix(st, c, rs):
        a_up = _dot(ya_ref[rs, :], wau_ref[...])
        m_up = _dot(ym_ref[rs, :], wmu_ref[...])
        mix = ga_ref[rs, :].astype(F32) * a_up + gm_ref[rs, :].astype(F32) * m_up
        st["mix"] = mix.astype(BF16)

    def out_and_norm(st, c, rs):
        x1 = _layer_norm(DEEPNORM_ALPHA * xn_ref[rs, :] + _dot(st.pop("mix"), wout_ref[...]), g1_ref[...], b1_ref[...])
        x1_ref[rs, :] = x1
        st["x1"] = x1

    def router_logits(st, c, rs):
        x1 = st.pop("x1")
        x_hi = x1.astype(BF16)
        x_lo = (x1 - x_hi.astype(F32)).astype(BF16)
        both = _dot_nt(wrc_ref[...], x_hi)
        st["logits"] = both[:LANES] + both[LANES:] + _dot_nt(wrc_ref[:LANES, :], x_lo) + br_ref[...]

    def route(st, c, rs):
        logits = st.pop("logits")
        is_g = (sub >= float(MOE_EXPERTS)) & (sub < float(MOE_EXPERTS + MOE_GROUPS))
        gl = jnp.where(is_g, logits, NEG_INF)
        ge = jnp.exp(gl - jnp.max(gl, axis=0, keepdims=True))
        gp = ge / jnp.sum(ge, axis=0, keepdims=True)
        g_w = jnp.max(gp, axis=0, keepdims=True)
        g_idx = jnp.min(jnp.where((gp == g_w) & is_g, sub - float(MOE_EXPERTS), big), axis=0, keepdims=True)
        lo = g_idx * float(MOE_EXPERTS_PER_GROUP)
        in_grp = (sub >= lo) & (sub < lo + float(MOE_EXPERTS_PER_GROUP))
        el = jnp.where(in_grp, logits, NEG_INF)
        v1 = jnp.max(el, axis=0, keepdims=True)
        i1 = jnp.min(jnp.where((el == v1) & in_grp, sub, big), axis=0, keepdims=True)
        el2 = jnp.where(sub == i1, NEG_INF, el)
        v2 = jnp.max(el2, axis=0, keepdims=True)
        i2 = jnp.min(jnp.where((el2 == v2) & in_grp & (sub != i1), sub, big), axis=0, keepdims=True)
        e2 = jnp.exp(v2 - v1)
        w0 = g_w / (1.0 + e2)
        w1 = g_w * e2 / (1.0 + e2)
        rw_ref[rs, :] = jnp.where(sub == 0.0, w0, jnp.where(sub == 1.0, w1, 0.0)).T
        st["i1"], st["i2"] = i1, i2

    def rank(st, c, rs):
        i1, i2 = st.pop("i1"), st.pop("i2")
        is1 = sub == i1
        is2 = sub == i2
        onehot = jnp.where(is1 | is2, 1.0, 0.0)
        rows = lax.broadcasted_iota(jnp.int32, (tm, tm), 0)
        cols = lax.broadcasted_iota(jnp.int32, (tm, tm), 1)
        earlier = jnp.where(rows < cols, 1.0, 0.0).astype(BF16)
        before = _dot(onehot.astype(BF16), earlier) + cnt_ref[...]
        r0 = jnp.sum(jnp.where(is1, before, 0.0), axis=0, keepdims=True)
        r1 = jnp.sum(jnp.where(is2, before, 0.0), axis=0, keepdims=True)
        total = cnt_ref[...] + jnp.sum(onehot, axis=1, keepdims=True)
        cnt_ref[...] = total
        cnt_out_ref[...] = total
        ri_t = jnp.where(sub == 0.0, i1, jnp.where(sub == 1.0, i2, jnp.where(sub == 2.0, r0, jnp.where(sub == 3.0, r1, 0.0))))
        ri_ref[:, rs] = ri_t[:SUBLANES, :].astype(jnp.int32)

    _run_skewed((up_and_mix, out_and_norm, router_logits, route, rank), xn_ref.shape[0] // tm, tm)


def _mix(xn, ya, ym, ga, gm, wau, wmu, wout, g1, b1, wrc, br):
    n, d = xn.shape
    tm = MIX_CHAINS * ROW_TILE
    row = lambda w: pl.BlockSpec((tm, w), lambda i: (i, 0))
    in_specs = [row(d), row(ATTN_WIDTH), row(MLSTM_WIDTH), row(d), row(d),
                _full(wau.shape), _full(wmu.shape), _full(wout.shape), _full(g1.shape), _full(b1.shape),
                _full(wrc.shape), _full(br.shape)]
    out_shape = (jax.ShapeDtypeStruct((n, d), F32), jax.ShapeDtypeStruct((SUBLANES, n), jnp.int32),
                 jax.ShapeDtypeStruct((n, LANES), F32), jax.ShapeDtypeStruct((LANES, 1), F32))
    out_specs = (row(d), pl.BlockSpec((SUBLANES, tm), lambda i: (0, i)), row(LANES), _full((LANES, 1)))
    return pl.pallas_call(
        _mix_kernel, grid=(n // tm,), in_specs=in_specs, out_specs=out_specs, out_shape=out_shape,
        scratch_shapes=[pltpu.VMEM((LANES, 1), F32)],
        compiler_params=_params("arbitrary"), name="mix",
    )(xn, ya, ym, ga, gm, wau, wmu, wout, g1, b1, wrc, br)


def _token_rows(d):
    return d // LANES


def _to_token_tiles(dst_ref, x):
    rows, d = x.shape
    nch = _token_rows(d)
    for c in range(nch):
        dst_ref[pl.ds(c, rows, stride=nch), :] = x[:, c * LANES:(c + 1) * LANES]


def _from_token_tiles(src_ref, rows, d):
    nch = _token_rows(d)
    return jnp.concatenate([src_ref[pl.ds(c, rows, stride=nch), :] for c in range(nch)], axis=1)


def _token_copy(src, src_tok, dst, dst_tok, nch, sem):
    s0 = pl.multiple_of(src_tok * nch, nch)
    d0 = pl.multiple_of(dst_tok * nch, nch)
    return pltpu.make_async_copy(src.at[pl.ds(s0, nch), :], dst.at[pl.ds(d0, nch), :], sem)


def _slots_kernel(ri_ref, ps_ref, o_ref):
    ri = ri_ref[...].astype(F32)
    ps = ps_ref[...]
    expert = lax.broadcasted_iota(jnp.int32, (ps.shape[0], ri.shape[1]), 0).astype(F32)
    row_id = lax.broadcasted_iota(jnp.int32, ri.shape, 0)
    out = jnp.zeros(ri.shape, F32)
    for k in range(2):
        start = jnp.sum(jnp.where(expert == ri[k:k + 1, :], jnp.broadcast_to(ps, expert.shape), 0.0),
                        axis=0, keepdims=True)
        out = jnp.where(row_id == k, start + ri[2 + k:3 + k, :], out)
    o_ref[...] = out.astype(jnp.int32)


def _slots(ri, pad_start_col):
    n = ri.shape[1]
    tm = SLOT_TILE
    blk = pl.BlockSpec((SUBLANES, tm), lambda i: (0, i))
    return pl.pallas_call(
        _slots_kernel, grid=(n // tm,), in_specs=[blk, _full(pad_start_col.shape)], out_specs=blk,
        out_shape=jax.ShapeDtypeStruct((SUBLANES, n), jnp.int32),
        compiler_params=_params("parallel"), name="slots",
    )(ri, pad_start_col)


def _slot(dest_ref, r, k):
    return dest_ref[k * ROW_TILE + r]


def _dispatch_kernel(dest_ref, last_ref, x_ref, xs_ref, scr_ref, zero_ref, sem, zsem):
    tm, d = x_ref.shape
    nch = _token_rows(d)
    tb = zero_ref.shape[0] // nch

    @pl.when(pl.program_id(0) == 0)
    def _():
        zero_ref[...] = jnp.zeros_like(zero_ref)

        def desc(tok):
            off = pl.multiple_of(jnp.maximum(tok, 0) * nch, nch)
            return pltpu.make_async_copy(zero_ref, xs_ref.at[pl.ds(off, tb * nch), :], zsem)

        def zstart(e, _):
            @pl.when(last_ref[e] >= 0)
            def _():
                desc(last_ref[e]).start()
            return 0

        def zwait(e, _):
            @pl.when(last_ref[e] >= 0)
            def _():
                desc(last_ref[e]).wait()
            return 0

        lax.fori_loop(0, MOE_EXPERTS, zstart, 0)
        nused = last_ref[MOE_EXPERTS]
        nblk = xs_ref.shape[0] // (tb * nch)
        lax.fori_loop(nused, nblk, lambda b, _: (desc(b * tb).start(), 0)[1], 0)
        lax.fori_loop(0, MOE_EXPERTS, zwait, 0)
        lax.fori_loop(nused, nblk, lambda b, _: (desc(b * tb).wait(), 0)[1], 0)

    step = pl.program_id(0)
    slot = step % 2
    scr = scr_ref.at[slot]
    _to_token_tiles(scr, x_ref[...])

    def start(r, _):
        for k in range(2):
            _token_copy(scr, r, xs_ref, _slot(dest_ref, r, k), nch, sem.at[slot]).start(priority=k)
        return 0

    def drain(which):
        def wait(r, _):
            for k in range(2):
                _token_copy(scr_ref.at[which], 0, xs_ref, 0, nch, sem.at[which]).wait()
            return 0
        lax.fori_loop(0, tm, wait, 0, unroll=8)

    lax.fori_loop(0, tm, start, 0, unroll=8)

    @pl.when(step > 0)
    def _():
        drain(1 - slot)

    @pl.when(step == pl.num_programs(0) - 1)
    def _():
        drain(slot)


def _dispatch(dest, last_blk, x1, n_rows):
    n, d = x1.shape
    tm = ROW_TILE
    nch = _token_rows(d)
    return pl.pallas_call(
        _dispatch_kernel, grid=(n // tm,),
        in_specs=[pl.BlockSpec((2 * tm,), lambda i: (i,), memory_space=pltpu.SMEM),
                  pl.BlockSpec(memory_space=pltpu.SMEM),
                  pl.BlockSpec((tm, d), lambda i: (i, 0))],
        out_specs=pl.BlockSpec(memory_space=pl.ANY),
        out_shape=jax.ShapeDtypeStruct((n_rows * nch, LANES), F32),
        scratch_shapes=[pltpu.VMEM((2, tm * nch, LANES), F32), pltpu.VMEM((EXPERT_TILE * nch, LANES), F32),
                        pltpu.SemaphoreType.DMA((2,)), pltpu.SemaphoreType.DMA(())],
        compiler_params=_params("arbitrary"), name="dispatch",
    )(dest, last_blk, x1)


def _expert_kernel(first_ref, count_ref, widx_ref, nused_ref, wg_ref, wu_ref, wd_ref, xs_ref, ys_ref,
                   wgb_ref, wub_ref, wdb_ref, xbuf_ref, ybuf_ref, in_sem, out_sem):
    del widx_ref
    e = pl.program_id(0)
    nused = nused_ref[0]
    d = wg_ref.shape[1]
    nch = _token_rows(d)
    rows = xbuf_ref.shape[1]
    tb = rows // nch
    nblk = xs_ref.shape[0] // rows

    def blk(ref, b):
        return ref.at[pl.ds(pl.multiple_of(b * rows, rows), rows), :]

    def in_copy(b, slot):
        return pltpu.make_async_copy(blk(xs_ref, b), xbuf_ref.at[slot], in_sem.at[slot])

    def out_copy(b, slot):
        return pltpu.make_async_copy(ybuf_ref.at[slot], blk(ys_ref, b), out_sem.at[slot])

    n_in = xbuf_ref.shape[0]

    @pl.when(e == 0)
    def _():
        for b0 in range(n_in - 1):
            @pl.when(b0 < nused)
            def _():
                in_copy(b0, b0).start()

    @pl.when(count_ref[e] > 0)
    def _():
        wgb_ref[...] = wg_ref[0].astype(BF16)
        wub_ref[...] = wu_ref[0].astype(BF16)
        wdb_ref[...] = wd_ref[0].astype(BF16)

    def body(b, _):
        slot = b % n_in
        oslot = b % 2
        in_copy(b, slot).wait()

        @pl.when(b + n_in - 1 < nused)
        def _():
            in_copy(b + n_in - 1, (b + n_in - 1) % n_in).start()

        @pl.when(b >= 2)
        def _():
            out_copy(b - 2, oslot).wait()

        xb = _from_token_tiles(xbuf_ref.at[slot], tb, d).astype(BF16)
        g = _dot(xb, wgb_ref[...])
        u = _dot(xb, wub_ref[...])
        hmid = g * jax.nn.sigmoid(g) * u
        _to_token_tiles(ybuf_ref.at[oslot], _dot(hmid.astype(BF16), wdb_ref[...]))
        out_copy(b, oslot).start()
        return 0

    lax.fori_loop(first_ref[e], first_ref[e] + count_ref[e], body, 0)

    @pl.when(e == pl.num_programs(0) - 1)
    def _():
        for back in (2, 1):
            @pl.when(nused >= back)
            def _():
                out_copy(nused - back, (nused - back) % 2).wait()

        ybuf_ref[0] = jnp.zeros(ybuf_ref.shape[1:], F32)
        lax.fori_loop(nused, nblk, lambda b, _: (out_copy(b, 0).start(), 0)[1], 0)
        lax.fori_loop(nused, nblk, lambda b, _: (out_copy(b, 0).wait(), 0)[1], 0)


def _experts(first_blk, blk_count, w_idx, nused, xs, w_gate, w_up, w_down):
    n_exp, d, dff = w_gate.shape
    nch = _token_rows(d)
    rows = EXPERT_TILE * nch
    w_spec = lambda shape: pl.BlockSpec(shape, lambda e, fb, bc, wi, nu: (wi[e], 0, 0))
    any_spec = pl.BlockSpec(memory_space=pl.ANY)
    grid_spec = pltpu.PrefetchScalarGridSpec(
        num_scalar_prefetch=4, grid=(n_exp,),
        in_specs=[w_spec((1, d, dff)), w_spec((1, d, dff)), w_spec((1, dff, d)), any_spec],
        out_specs=any_spec,
        scratch_shapes=[pltpu.VMEM((d, dff), BF16), pltpu.VMEM((d, dff), BF16), pltpu.VMEM((dff, d), BF16),
                        pltpu.VMEM((EXPERT_IN_SLOTS, rows, LANES), F32), pltpu.VMEM((2, rows, LANES), F32),
                        pltpu.SemaphoreType.DMA((EXPERT_IN_SLOTS,)), pltpu.SemaphoreType.DMA((2,))],
    )
    return pl.pallas_call(
        _expert_kernel, grid_spec=grid_spec, out_shape=jax.ShapeDtypeStruct(xs.shape, F32),
        compiler_params=_params("arbitrary"), name="experts",
    )(first_blk, blk_count, w_idx, nused, w_gate, w_up, w_down, xs)


def _combine_kernel(dest_ref, dest_next_ref, x1_ref, rw_ref, g_ref, b_ref, ys_ref, o_ref, buf_ref, sem):
    tm, d = x1_ref.shape
    nch = _token_rows(d)
    step = pl.program_id(0)
    slot = step % 2

    def gather(idx_ref, which):
        def start(r, _):
            for k in range(2):
                _token_copy(ys_ref, _slot(idx_ref, r, k), buf_ref.at[which, k], r, nch,
                            sem.at[which]).start(priority=k)
            return 0
        lax.fori_loop(0, tm, start, 0, unroll=8)

    @pl.when(step == 0)
    def _():
        gather(dest_ref, 0)

    @pl.when(step + 1 < pl.num_programs(0))
    def _():
        gather(dest_next_ref, 1 - slot)

    def wait(r, _):
        for k in range(2):
            _token_copy(ys_ref, 0, buf_ref.at[slot, k], 0, nch, sem.at[slot]).wait()
        return 0

    lax.fori_loop(0, tm, wait, 0, unroll=8)
    rw = rw_ref[...]
    y0 = _from_token_tiles(buf_ref.at[slot, 0], tm, d)
    y1 = _from_token_tiles(buf_ref.at[slot, 1], tm, d)
    ffn = rw[:, 0:1] * y0 + rw[:, 1:2] * y1
    o_ref[...] = _layer_norm(DEEPNORM_ALPHA * x1_ref[...] + ffn, g_ref[...], b_ref[...])


def _combine(dest, x1, rw, ln_g, ln_b, ys):
    n, d = x1.shape
    tm = ROW_TILE
    nch = _token_rows(d)
    last = n // tm - 1
    row = lambda w: pl.BlockSpec((tm, w), lambda i: (i, 0))
    return pl.pallas_call(
        _combine_kernel, grid=(n // tm,),
        in_specs=[pl.BlockSpec((2 * tm,), lambda i: (i,), memory_space=pltpu.SMEM),
                  pl.BlockSpec((2 * tm,), lambda i: (jnp.minimum(i + 1, last),), memory_space=pltpu.SMEM),
                  row(d), row(LANES), _full(ln_g.shape), _full(ln_b.shape),
                  pl.BlockSpec(memory_space=pl.ANY)],
        out_specs=row(d),
        out_shape=jax.ShapeDtypeStruct((n, d), F32),
        scratch_shapes=[pltpu.VMEM((2, 2, tm * nch, LANES), F32), pltpu.SemaphoreType.DMA((2,))],
        compiler_params=_params("arbitrary"), name="combine",
    )(dest, dest, x1, rw, ln_g, ln_b, ys)


def _rope_tables(seq):
    half = ATTN_HEAD_DIM // 2
    inv_freq = ROPE_THETA ** (-jnp.arange(half, dtype=F32) / half)
    ang = jnp.arange(seq, dtype=F32)[:, None] * inv_freq[None, :]
    cos = jnp.cos(ang)
    sin = jnp.sin(ang)
    cos_h = jnp.concatenate([cos, cos], axis=1)
    sin_h = jnp.concatenate([-sin, sin], axis=1)
    return jnp.tile(cos_h, (1, ATTN_HEADS)), jnp.tile(sin_h, (1, ATTN_HEADS))


def _pad_lanes(a, width=LANES):
    return jnp.pad(a, ((0, 0), (0, width - a.shape[1])))


def kernel(x, ln0_g, ln0_b, w_in, conv_w, conv_b, w_mq, w_mk, b_i, b_f, gn_g, skip, w_attn_up, w_mlstm_up, w_out,
           ln1_g, ln1_b, w_router_group, b_router_group, w_router_expert, b_router_expert, w_gate, w_up, w_down,
           ln2_g, ln2_b):
    batch, seq, d = x.shape
    n = batch * seq
    assert seq % ROW_TILE == 0 and ROW_TILE == MOBA_BLOCK and w_in.shape[0] == DEPTH
    x2 = x.reshape(n, d)
    vec = lambda a: a.reshape(1, -1).astype(F32)

    w = w_in[0]
    c_if = 3 * ATTN_WIDTH + 3 * MLSTM_WIDTH
    c_g = c_if + 2 * MLSTM_HEADS
    wqkv = w[:, :3 * ATTN_WIDTH].astype(BF16)
    wuvo = w[:, 3 * ATTN_WIDTH:c_if].astype(BF16)
    w_if = w[:, c_if:c_g]
    wif = _pad_lanes(w_if).astype(BF16)
    wift = w_if.T.astype(BF16)
    wg = w[:, c_g:].astype(BF16)
    cos, sin = _rope_tables(seq)

    q, k, v, kmean, u, vm, o, ifc, ift, ga, gm, xn = _inproj(
        x2, vec(ln0_g), vec(ln0_b), wqkv, wuvo, wif, wift, wg, cos, sin, batch, seq)

    nb = seq // MOBA_BLOCK
    km = kmean.reshape(batch, nb, ATTN_HEADS, ATTN_HEAD_DIM).transpose(0, 2, 1, 3)
    ya = _moba(q, k, v, km).reshape(n, ATTN_WIDTH)

    b_if = jnp.concatenate([b_i[0], b_f[0]]).astype(F32)
    ym = _mlstm(u, vm, o, ifc, ift, conv_w[0], vec(conv_b[0]), w_mq[0].transpose(0, 2, 1).astype(BF16),
                w_mk[0].astype(BF16), _pad_lanes(b_if[None, :]), b_if[:, None],
                gn_g[0].astype(F32)[:, None], skip[0].astype(F32)[:, None], batch, seq)

    w_r = _pad_lanes(jnp.concatenate([w_router_expert[0], w_router_group[0]], axis=1))
    w_r_hi = w_r.astype(BF16)
    w_r_lo = (w_r - w_r_hi.astype(F32)).astype(BF16)
    w_rc = jnp.concatenate([w_r_hi.T, w_r_lo.T], axis=0)
    b_r = _pad_lanes(jnp.concatenate([b_router_expert[0], b_router_group[0]])[None, :]).T
    x1, ri, rw, counts = _mix(
        xn, ya, ym, ga, gm, w_attn_up[0].astype(BF16), w_mlstm_up[0].astype(BF16),
        w_out[0].astype(BF16), vec(ln1_g[0]), vec(ln1_b[0]), w_rc, b_r)

    tb = EXPERT_TILE
    nblk = (2 * n) // tb + MOE_EXPERTS
    cnt = counts[:MOE_EXPERTS, 0].astype(jnp.int32)
    nblk_e = (cnt + tb - 1) // tb
    blk_end = jnp.cumsum(nblk_e)
    pad_start = (blk_end - nblk_e) * tb
    nused = blk_end[-1:]
    ids = jnp.arange(MOE_EXPERTS, dtype=jnp.int32)
    prev_used = jnp.max(jnp.where((ids[None, :] <= ids[:, None]) & (nblk_e[None, :] > 0), ids[None, :], -1), axis=1)
    first_used = jnp.min(jnp.where(nblk_e > 0, ids, MOE_EXPERTS - 1))
    w_idx = jnp.where(prev_used >= 0, prev_used, first_used).astype(jnp.int32)
    last_blk = jnp.where(nblk_e > 0, (blk_end - 1) * tb, -1)
    last_blk = jnp.concatenate([last_blk, nused]).astype(jnp.int32)
    dest = _slots(ri, pad_start.astype(F32)[:, None])
    dest = dest[:2].reshape(2, n // ROW_TILE, ROW_TILE).transpose(1, 0, 2).reshape(2 * n)

    xs = _dispatch(dest, last_blk, x1, nblk * tb)
    ys = _experts((blk_end - nblk_e).astype(jnp.int32), nblk_e.astype(jnp.int32), w_idx, nused.astype(jnp.int32),
                  xs, w_gate[0], w_up[0], w_down[0])
    out = _combine(dest, x1, rw, vec(ln2_g[0]), vec(ln2_b[0]), ys)
    return out.reshape(batch, seq, d)
```

```python
import functools
import math

import jax
import jax.numpy as jnp
from jax import lax
from jax.experimental import pallas as pl
from jax.experimental.pallas import tpu as pltpu

F32 = jnp.float32
BF16 = jnp.bfloat16

ATTN_HEADS = 8
ATTN_HEAD_DIM = 64
ATTN_WIDTH = ATTN_HEADS * ATTN_HEAD_DIM
MOBA_BLOCK = 256
MOBA_TOPK = 3
ROPE_THETA = 10000.0
MLSTM_HEADS = 4
MLSTM_HEAD_DIM = 128
MLSTM_WIDTH = MLSTM_HEADS * MLSTM_HEAD_DIM
MLSTM_CONV = 4
MOE_GROUPS = 8
MOE_EXPERTS_PER_GROUP = 8
MOE_EXPERTS = MOE_GROUPS * MOE_EXPERTS_PER_GROUP
MOE_D_FF = 512
LN_EPS = 1e-5
GN_EPS = 1e-6
DEPTH = 1
DEEPNORM_ALPHA = (2 * DEPTH) ** 0.25

LANES = 128
SUBLANES = 8
ROW_TILE = 256
EXPERT_TILE = 256
EXPERT_IN_SLOTS = 4
COMBINE_CHUNKS = 8
INPROJ_CHAINS = 2
SLOT_TILE = 2048
MIX_CHAINS = 4
VMEM_LIMIT = 48 * 1024 * 1024
LOG2_E = math.log2(math.e)

NEG_INF = float("-inf")


def _params(*sem):
    return pltpu.CompilerParams(dimension_semantics=sem, vmem_limit_bytes=VMEM_LIMIT)


def _dot(a, b):
    return jnp.dot(a, b, preferred_element_type=F32)


def _dot_nt(a, b):
    return lax.dot_general(a, b, (((1,), (1,)), ((), ())), preferred_element_type=F32)


def _dot_tn(a, b):
    return lax.dot_general(a, b, (((0,), (0,)), ((), ())), preferred_element_type=F32)


def _split3(x):
    x1 = x.astype(BF16)
    r1 = x - x1.astype(F32)
    x2 = r1.astype(BF16)
    r2 = r1 - x2.astype(F32)
    return x1, x2, r2.astype(BF16)


def _layer_norm(x, g, b):
    mu = jnp.mean(x, axis=-1, keepdims=True)
    xc = x - mu
    var = jnp.mean(xc * xc, axis=-1, keepdims=True)
    return xc * lax.rsqrt(var + LN_EPS) * g + b


def _log_sigmoid(x):
    return jnp.minimum(x, 0.0) - jnp.log1p(jnp.exp(-jnp.abs(x)))


def _full(shape):
    nd = len(shape)
    return pl.BlockSpec(shape, lambda *_: (0,) * nd)


def _run_skewed(phases, chains, rows):
    states = [dict() for _ in range(chains)]
    for t in range(chains + len(phases) - 1):
        for c in range(chains):
            if 0 <= t - c < len(phases):
                phases[t - c](states[c], c, slice(c * rows, (c + 1) * rows))


def _inproj_kernel(x_ref, g_ref, b_ref, wqkv_ref, wuvo_ref, wif_ref, wift_ref, wg_ref, cos_ref, sin_ref,
                   q_ref, k_ref, v_ref, km_ref, u_ref, vm_ref, o_ref, ifc_ref, ift_ref, ga_ref, gm_ref, xn_ref):
    tm = ROW_TILE
    lane = lax.broadcasted_iota(jnp.int32, (tm, ATTN_WIDTH), 1)
    first_half = (lane % ATTN_HEAD_DIM) < (ATTN_HEAD_DIM // 2)

    def norm(st, c, rs):
        xn = _layer_norm(x_ref[rs, :], g_ref[...], b_ref[...])
        xn_ref[rs, :] = xn
        st["xb"] = xn.astype(BF16)

    def qkv_matmul(st, c, rs):
        st["zqkv"] = _dot(st["xb"], wqkv_ref[...])

    def attn_outputs(st, c, rs):
        zqkv = st.pop("zqkv")
        cos = cos_ref[rs, :]
        sin = sin_ref[rs, :]

        def rope(t):
            fwd = pltpu.roll(t, ATTN_WIDTH - ATTN_HEAD_DIM // 2, axis=1)
            bwd = pltpu.roll(t, ATTN_HEAD_DIM // 2, axis=1)
            return t * cos + jnp.where(first_half, fwd, bwd) * sin

        q = rope(zqkv[:, :ATTN_WIDTH]) * (ATTN_HEAD_DIM ** -0.5 * LOG2_E)
        k = rope(zqkv[:, ATTN_WIDTH:2 * ATTN_WIDTH])
        v = zqkv[:, 2 * ATTN_WIDTH:]
        km_ref[c] = jnp.mean(k, axis=0, keepdims=True)
        qt = q.T
        vt = v.T
        for h in range(ATTN_HEADS):
            sl = slice(h * ATTN_HEAD_DIM, (h + 1) * ATTN_HEAD_DIM)
            q_ref[0, h, :, rs] = qt[sl, :].astype(BF16)
            k_ref[0, h, rs, :] = k[:, sl].astype(BF16)
            v_ref[0, h, :, rs] = vt[sl, :].astype(BF16)

    def uvo_matmul(st, c, rs):
        st["zuvo"] = _dot(st["xb"], wuvo_ref[...])

    def mlstm_outputs(st, c, rs):
        zuvo = st.pop("zuvo")
        u_ref[rs, :] = zuvo[:, :MLSTM_WIDTH]
        vm_ref[:, rs] = zuvo[:, MLSTM_WIDTH:2 * MLSTM_WIDTH].T.astype(BF16)
        o_ref[:, rs] = zuvo[:, 2 * MLSTM_WIDTH:].T
        ifc_ref[rs, :] = _dot(st["xb"], wif_ref[...])
        ift_ref[:, rs] = _dot_nt(wift_ref[...], st["xb"])

    def gate_matmul(st, c, rs):
        st["zg"] = _dot(st.pop("xb"), wg_ref[...])

    def gate_outputs(st, c, rs):
        zg = st.pop("zg")
        d = ga_ref.shape[1]
        ga_ref[rs, :] = jax.nn.sigmoid(zg[:, :d]).astype(BF16)
        gm_ref[rs, :] = jax.nn.sigmoid(zg[:, d:]).astype(BF16)

    _run_skewed((norm, qkv_matmul, attn_outputs, uvo_matmul, mlstm_outputs, gate_matmul, gate_outputs),
                x_ref.shape[0] // tm, tm)


def _inproj(x2, ln_g, ln_b, wqkv, wuvo, wif, wift, wg, cos, sin, batch, seq):
    n, d = x2.shape
    chains = INPROJ_CHAINS
    tm = chains * ROW_TILE
    assert seq % tm == 0
    nsb = seq // tm
    hd = ATTN_HEAD_DIM
    row = lambda w: pl.BlockSpec((tm, w), lambda i: (i, 0))
    col = lambda h: pl.BlockSpec((h, tm), lambda i: (0, i))
    head = pl.BlockSpec((1, ATTN_HEADS, tm, hd), lambda i: (i // nsb, 0, i % nsb, 0))
    head_t = pl.BlockSpec((1, ATTN_HEADS, hd, tm), lambda i: (i // nsb, 0, 0, i % nsb))
    tab = pl.BlockSpec((tm, ATTN_WIDTH), lambda i: (i % nsb, 0))
    head_shape = jax.ShapeDtypeStruct((batch, ATTN_HEADS, seq, hd), BF16)
    head_t_shape = jax.ShapeDtypeStruct((batch, ATTN_HEADS, hd, seq), BF16)
    out_shape = (
        head_t_shape, head_shape, head_t_shape,
        jax.ShapeDtypeStruct((n // ROW_TILE, 1, ATTN_WIDTH), F32),
        jax.ShapeDtypeStruct((n, MLSTM_WIDTH), F32),
        jax.ShapeDtypeStruct((MLSTM_WIDTH, n), BF16),
        jax.ShapeDtypeStruct((MLSTM_WIDTH, n), F32),
        jax.ShapeDtypeStruct((n, LANES), F32),
        jax.ShapeDtypeStruct((SUBLANES, n), F32),
        jax.ShapeDtypeStruct((n, d), BF16),
        jax.ShapeDtypeStruct((n, d), BF16),
        jax.ShapeDtypeStruct((n, d), F32),
    )
    out_specs = (
        head_t, head, head_t,
        pl.BlockSpec((chains, 1, ATTN_WIDTH), lambda i: (i, 0, 0)),
        row(MLSTM_WIDTH), col(MLSTM_WIDTH), col(MLSTM_WIDTH),
        row(LANES),
        col(SUBLANES),
        row(d), row(d), row(d),
    )
    in_specs = [row(d), _full(ln_g.shape), _full(ln_b.shape), _full(wqkv.shape), _full(wuvo.shape),
                _full(wif.shape), _full(wift.shape), _full(wg.shape), tab, tab]
    return pl.pallas_call(
        _inproj_kernel, grid=(n // tm,), in_specs=in_specs, out_specs=out_specs, out_shape=out_shape,
        compiler_params=_params("parallel"), name="inproj",
    )(x2, ln_g, ln_b, wqkv, wuvo, wif, wift, wg, cos, sin)


def _moba_kernel(qt_ref, k_ref, vt_ref, km_ref, o_ref, bias_ref, m_ref, l_ref, acc_ref, s_ref):
    i = pl.program_id(1)
    blk = MOBA_BLOCK
    hd = ATTN_HEAD_DIM
    heads = ATTN_HEADS
    nb = k_ref.shape[2] // blk
    blk_id = lax.broadcasted_iota(jnp.int32, (nb, blk), 0)
    key_pos = lax.broadcasted_iota(jnp.int32, (blk, blk), 0)
    qry_pos = lax.broadcasted_iota(jnp.int32, (blk, blk), 1)
    causal = key_pos <= qry_pos

    for h in range(heads):
        qt = qt_ref[0, h]
        km = km_ref[0, h]
        km_hi = km.astype(BF16)
        km_lo = (km - km_hi.astype(F32)).astype(BF16)
        gate = _dot(km_hi, qt) + _dot(km_lo, qt)
        gate = jnp.where(blk_id < i, gate, NEG_INF)
        for j in range(nb - 1):
            row = gate[j:j + 1, :]
            beats = (gate > row) | ((gate == row) & (blk_id < j))
            cnt = jnp.sum(jnp.where(beats, 1.0, 0.0), axis=0, keepdims=True)
            sel = (cnt < float(MOBA_TOPK)) & (row > NEG_INF)
            bias_ref[j * heads + h] = jnp.where(sel, 0.0, NEG_INF)
    for h in range(heads):
        bias_ref[i * heads + h] = jnp.zeros((1, blk), F32)

    def scores(h, j, own_block):
        qt = qt_ref[0, h]
        half = blk // 2
        m_tile = None
        for c in range(2):
            rows = slice(c * half, (c + 1) * half)
            s = _dot(k_ref[0, h, pl.ds(pl.multiple_of(j * blk + c * half, half), half), :], qt)
            if own_block:
                s = jnp.where(causal[rows], s, NEG_INF)
            s_ref[j * heads + h, rows, :] = s
            m_c = jnp.max(s, axis=0, keepdims=True)
            m_tile = m_c if m_tile is None else jnp.maximum(m_tile, m_c)
        return m_tile

    for h in range(heads):
        m_ref[h] = scores(h, i, True)

    def past_scores(j, _):
        for h in range(heads):
            m_ref[h] = jnp.maximum(m_ref[h], scores(h, j, False) + bias_ref[j * heads + h])
        return 0

    lax.fori_loop(0, i, past_scores, 0)

    l_ref[...] = jnp.zeros_like(l_ref)
    acc_ref[...] = jnp.zeros_like(acc_ref)

    def accumulate(j, _):
        off = pl.multiple_of(j * blk, blk)
        for h in range(heads):
            p = jnp.exp2(s_ref[j * heads + h] - (m_ref[h] - bias_ref[j * heads + h]))
            l_ref[h] += jnp.sum(p, axis=0, keepdims=True)
            acc_ref[h] += _dot(vt_ref[0, h, :, pl.ds(off, blk)], p.astype(BF16))
        return 0

    lax.fori_loop(0, i + 1, accumulate, 0)
    yt = acc_ref[...] / l_ref[...]
    o_ref[0] = yt.reshape(heads * hd, blk).T.astype(BF16)


def _moba(qt, k, vt, km):
    batch, heads, seq, hd = k.shape
    blk = MOBA_BLOCK
    nb = seq // blk
    return pl.pallas_call(
        _moba_kernel, grid=(batch, nb),
        in_specs=[
            pl.BlockSpec((1, heads, hd, blk), lambda b, i: (b, 0, 0, i)),
            pl.BlockSpec((1, heads, seq, hd), lambda b, i: (b, 0, 0, 0)),
            pl.BlockSpec((1, heads, hd, seq), lambda b, i: (b, 0, 0, 0)),
            pl.BlockSpec((1, heads, nb, hd), lambda b, i: (b, 0, 0, 0)),
        ],
        out_specs=pl.BlockSpec((1, blk, heads * hd), lambda b, i: (b, i, 0)),
        out_shape=jax.ShapeDtypeStruct((batch, seq, heads * hd), BF16),
        scratch_shapes=[pltpu.VMEM((nb * heads, 1, blk), F32), pltpu.VMEM((heads, 1, blk), F32),
                        pltpu.VMEM((heads, 1, blk), F32), pltpu.VMEM((heads, hd, blk), F32),
                        pltpu.VMEM((nb * heads, blk, blk), F32)],
        compiler_params=_params("parallel", "arbitrary"), name="moba",
    )(qt, k, vt, km)


def _mlstm_kernel(u_ref, vmt_ref, ot_ref, ifc_ref, ift_ref, cw_ref, cb_ref, wqt_ref, wk_ref, brow_ref, bcol_ref,
                  gn_ref, skip_ref, y_ref, ext_ref, c_ref, n_ref, m_ref, yt_ref):
    tm = u_ref.shape[0]
    hd = MLSTM_HEAD_DIM
    halo = SUBLANES

    @pl.when(pl.program_id(1) == 0)
    def _():
        ext_ref[0:halo, :] = jnp.zeros((halo, MLSTM_WIDTH), F32)
        c_ref[...] = jnp.zeros_like(c_ref)
        n_ref[...] = jnp.zeros_like(n_ref)
        m_ref[...] = jnp.zeros_like(m_ref)

    u = u_ref[...]
    ext_ref[halo:halo + tm, :] = u
    acc = jnp.broadcast_to(cb_ref[...], u.shape)
    for j in range(MLSTM_CONV):
        acc = acc + cw_ref[j:j + 1, :] * ext_ref[halo - (MLSTM_CONV - 1) + j:halo - (MLSTM_CONV - 1) + j + tm, :]
    ext_ref[0:halo, :] = u[tm - halo:, :]
    uc = acc * jax.nn.sigmoid(acc)

    gc = ifc_ref[...] + brow_ref[...]
    gr = ift_ref[...] + bcol_ref[...]
    rows = lax.broadcasted_iota(jnp.int32, (tm, tm), 0)
    cols = lax.broadcasted_iota(jnp.int32, (tm, tm), 1)
    causal_t = rows <= cols
    tril = jnp.where(cols <= rows, 1.0, 0.0).astype(BF16)
    triu = jnp.where(causal_t, 1.0, 0.0).astype(BF16)
    c1, c2, c3 = _split3(_log_sigmoid(gc))
    bcum_c = _dot(tril, c1) + _dot(tril, c2) + _dot(tril, c3)
    r1, r2, r3 = _split3(_log_sigmoid(gr))
    bcum_r = _dot(r1, triu) + _dot(r2, triu) + _dot(r3, triu)

    uct = uc.T

    def decay_weights(st, h, hs):
        fl = MLSTM_HEADS + h
        b_row = bcum_r[fl:fl + 1, :]
        st["key_row"] = gr[h:h + 1, :] - b_row
        st["key_col"] = gc[:, h:h + 1] - bcum_c[:, fl:fl + 1]
        m_prev = m_ref[h][:, 0:1]
        dlog = jnp.where(causal_t, st["key_col"] + b_row, NEG_INF)
        inter = b_row + m_prev
        m_t = jnp.maximum(inter, jnp.max(dlog, axis=0, keepdims=True))
        st["w_intra"] = jnp.exp(dlog - m_t)
        st["w_inter"] = jnp.exp(inter - m_t)
        st["m_t"], st["m_prev"], st["b_end"] = m_t, m_prev, b_row[:, tm - 1:tm]

    def project(st, h, hs):
        st["qtb"] = _dot(wqt_ref[h], uct[hs, :].astype(BF16)).astype(BF16)
        st["k"] = _dot(uc[:, hs].astype(BF16), wk_ref[h]) * (hd ** -0.5)

    def scores(st, h, hs):
        st["s"] = _dot(st["k"].astype(BF16), st["qtb"]) * st.pop("w_intra")

    def readout(st, h, hs):
        qtb, s, w_inter, m_t = st.pop("qtb"), st.pop("s"), st.pop("w_inter"), st.pop("m_t")
        n_prev = n_ref[h]
        n_hi = n_prev.astype(BF16)
        n_lo = (n_prev - n_hi.astype(F32)).astype(BF16)
        qn = (_dot(n_hi, qtb) + _dot(n_lo, qtb))[0:1, :]
        num = w_inter * _dot(c_ref[h].astype(BF16), qtb) + _dot(vmt_ref[hs, :], s.astype(BF16))
        den = w_inter * qn + jnp.sum(s, axis=0, keepdims=True)
        st["hh"] = num / jnp.maximum(jnp.abs(den), jnp.exp(-m_t))

    def update_state(st, h, hs):
        b_end, m_prev = st.pop("b_end"), st.pop("m_prev")
        m_new = jnp.maximum(b_end + m_prev, jnp.max(b_end + st.pop("key_row"), axis=1, keepdims=True))
        decay = jnp.exp(b_end + m_prev - m_new)
        kw = st.pop("k") * jnp.exp(b_end + st.pop("key_col") - m_new)
        n_prev = n_ref[h]
        c_ref[h] = decay * c_ref[h] + _dot(vmt_ref[hs, :], kw.astype(BF16))
        n_ref[h] = decay * n_prev + jnp.broadcast_to(jnp.sum(kw, axis=0, keepdims=True), n_prev.shape)
        m_ref[h] = jnp.broadcast_to(m_new, (1, LANES))

    def gate_and_norm(st, h, hs):
        hh = jax.nn.sigmoid(ot_ref[hs, :]) * st.pop("hh")
        mu = jnp.mean(hh, axis=0, keepdims=True)
        hc = hh - mu
        var = jnp.mean(hc * hc, axis=0, keepdims=True)
        yt_ref[hs, :] = hc * lax.rsqrt(var + GN_EPS) * gn_ref[hs, :] + skip_ref[hs, :] * uct[hs, :]

    _run_skewed((decay_weights, project, scores, readout, update_state, gate_and_norm), MLSTM_HEADS, hd)
    y_ref[...] = yt_ref[...].T.astype(BF16)


def _mlstm(u, vmt, ot, ifc, ift, conv_w, conv_b, wqt, wk, brow, bcol, gn_g, skip, batch, seq):
    n = u.shape[0]
    tm = ROW_TILE
    nc = seq // tm
    row = lambda w: pl.BlockSpec((tm, w), lambda b, c: (b * nc + c, 0))
    col = lambda h: pl.BlockSpec((h, tm), lambda b, c: (0, b * nc + c))
    in_specs = [row(MLSTM_WIDTH), col(MLSTM_WIDTH), col(MLSTM_WIDTH), row(LANES), col(SUBLANES),
                _full(conv_w.shape), _full(conv_b.shape), _full(wqt.shape), _full(wk.shape),
                _full(brow.shape), _full(bcol.shape), _full(gn_g.shape), _full(skip.shape)]
    return pl.pallas_call(
        _mlstm_kernel, grid=(batch, nc), in_specs=in_specs, out_specs=row(MLSTM_WIDTH),
        out_shape=jax.ShapeDtypeStruct((n, MLSTM_WIDTH), BF16),
        scratch_shapes=[pltpu.VMEM((SUBLANES + tm, MLSTM_WIDTH), F32),
                        pltpu.VMEM((MLSTM_HEADS, MLSTM_HEAD_DIM, MLSTM_HEAD_DIM), F32),
                        pltpu.VMEM((MLSTM_HEADS, SUBLANES, MLSTM_HEAD_DIM), F32),
                        pltpu.VMEM((MLSTM_HEADS, 1, LANES), F32),
                        pltpu.VMEM((MLSTM_WIDTH, tm), F32)],
        compiler_params=_params("parallel", "arbitrary"), name="mlstm",
    )(u, vmt, ot, ifc, ift, conv_w, conv_b, wqt, wk, brow, bcol, gn_g, skip)


def _mix_kernel(xn_ref, ya_ref, ym_ref, ga_ref, gm_ref, wau_ref, wmu_ref, wout_ref,
                g1_ref, b1_ref, wrc_ref, br_ref,
                x1_ref, ri_ref, rw_ref, cnt_out_ref, cnt_ref):
    @pl.when(pl.program_id(0) == 0)
    def _():
        cnt_ref[...] = jnp.zeros_like(cnt_ref)

    tm = ROW_TILE
    sub = lax.broadcasted_iota(jnp.int32, (LANES, tm), 0).astype(F32)
    big = float(4 * LANES)

    def up_and_mix(st, c, rs):
        a_up = _dot(ya_ref[rs, :], wau_ref[...])
        m_up = _dot(ym_ref[rs, :], wmu_ref[...])
        mix = ga_ref[rs, :].astype(F32) * a_up + gm_ref[rs, :].astype(F32) * m_up
        st["mix"] = mix.astype(BF16)

    def out_and_norm(st, c, rs):
        x1 = _layer_norm(DEEPNORM_ALPHA * xn_ref[rs, :] + _dot(st.pop("mix"), wout_ref[...]), g1_ref[...], b1_ref[...])
        x1_ref[rs, :] = x1
        st["x1"] = x1

    def router_logits(st, c, rs):
        x1 = st.pop("x1")
        x_hi = x1.astype(BF16)
        x_lo = (x1 - x_hi.astype(F32)).astype(BF16)
        both = _dot_nt(wrc_ref[...], x_hi)
        st["logits"] = both[:LANES] + both[LANES:] + _dot_nt(wrc_ref[:LANES, :], x_lo) + br_ref[...]

    def route(st, c, rs):
        logits = st.pop("logits")
        is_g = (sub >= float(MOE_EXPERTS)) & (sub < float(MOE_EXPERTS + MOE_GROUPS))
        gl = jnp.where(is_g, logits, NEG_INF)
        ge = jnp.exp(gl - jnp.max(gl, axis=0, keepdims=True))
        gp = ge / jnp.sum(ge, axis=0, keepdims=True)
        g_w = jnp.max(gp, axis=0, keepdims=True)
        g_idx = jnp.min(jnp.where((gp == g_w) & is_g, sub - float(MOE_EXPERTS), big), axis=0, keepdims=True)
        lo = g_idx * float(MOE_EXPERTS_PER_GROUP)
        in_grp = (sub >= lo) & (sub < lo + float(MOE_EXPERTS_PER_GROUP))
        el = jnp.where(in_grp, logits, NEG_INF)
        v1 = jnp.max(el, axis=0, keepdims=True)
        i1 = jnp.min(jnp.where((el == v1) & in_grp, sub, big), axis=0, keepdims=True)
        el2 = jnp.where(sub == i1, NEG_INF, el)
        v2 = jnp.max(el2, axis=0, keepdims=True)
        i2 = jnp.min(jnp.where((el2 == v2) & in_grp & (sub != i1), sub, big), axis=0, keepdims=True)
        e2 = jnp.exp(v2 - v1)
        w0 = g_w / (1.0 + e2)
        w1 = g_w * e2 / (1.0 + e2)
        rw_ref[rs, :] = jnp.where(sub == 0.0, w0, jnp.where(sub == 1.0, w1, 0.0)).T
        st["i1"], st["i2"] = i1, i2

    def rank(st, c, rs):
        i1, i2 = st.pop("i1"), st.pop("i2")
        is1 = sub == i1
        is2 = sub == i2
        onehot = jnp.where(is1 | is2, 1.0, 0.0)
        rows = lax.broadcasted_iota(jnp.int32, (tm, tm), 0)
        cols = lax.broadcasted_iota(jnp.int32, (tm, tm), 1)
        earlier = jnp.where(rows < cols, 1.0, 0.0).astype(BF16)
        before = _dot(onehot.astype(BF16), earlier) + cnt_ref[...]
        r0 = jnp.sum(jnp.where(is1, before, 0.0), axis=0, keepdims=True)
        r1 = jnp.sum(jnp.where(is2, before, 0.0), axis=0, keepdims=True)
        total = cnt_ref[...] + jnp.sum(onehot, axis=1, keepdims=True)
        cnt_ref[...] = total
        cnt_out_ref[...] = total
        ri_t = jnp.where(sub == 0.0, i1, jnp.where(sub == 1.0, i2, jnp.where(sub == 2.0, r0, jnp.where(sub == 3.0, r1, 0.0))))
        ri_ref[:, rs] = ri_t[:SUBLANES, :].astype(jnp.int32)

    _run_skewed((up_and_mix, out_and_norm, router_logits, route, rank), xn_ref.shape[0] // tm, tm)


def _mix(xn, ya, ym, ga, gm, wau, wmu, wout, g1, b1, wrc, br):
    n, d = xn.shape
    tm = MIX_CHAINS * ROW_TILE
    row = lambda w: pl.BlockSpec((tm, w), lambda i: (i, 0))
    in_specs = [row(d), row(ATTN_WIDTH), row(MLSTM_WIDTH), row(d), row(d),
                _full(wau.shape), _full(wmu.shape), _full(wout.shape), _full(g1.shape), _full(b1.shape),
                _full(wrc.shape), _full(br.shape)]
    out_shape = (jax.ShapeDtypeStruct((n, d), F32), jax.ShapeDtypeStruct((SUBLANES, n), jnp.int32),
                 jax.ShapeDtypeStruct((n, LANES), F32), jax.ShapeDtypeStruct((LANES, 1), F32))
    out_specs = (row(d), pl.BlockSpec((SUBLANES, tm), lambda i: (0, i)), row(LANES), _full((LANES, 1)))
    return pl.pallas_call(
        _mix_kernel, grid=(n // tm,), in_specs=in_specs, out_specs=out_specs, out_shape=out_shape,
        scratch_shapes=[pltpu.VMEM((LANES, 1), F32)],
        compiler_params=_params("arbitrary"), name="mix",
    )(xn, ya, ym, ga, gm, wau, wmu, wout, g1, b1, wrc, br)


def _token_rows(d):
    return d // LANES


def _to_token_tiles(dst_ref, x):
    rows, d = x.shape
    nch = _token_rows(d)
    for c in range(nch):
        dst_ref[pl.ds(c, rows, stride=nch), :] = x[:, c * LANES:(c + 1) * LANES]


def _from_token_tiles(src_ref, rows, d):
    nch = _token_rows(d)
    return jnp.concatenate([src_ref[pl.ds(c, rows, stride=nch), :] for c in range(nch)], axis=1)


def _token_copy(src, src_tok, dst, dst_tok, nch, sem):
    s0 = pl.multiple_of(src_tok * nch, nch)
    d0 = pl.multiple_of(dst_tok * nch, nch)
    return pltpu.make_async_copy(src.at[pl.ds(s0, nch), :], dst.at[pl.ds(d0, nch), :], sem)


def _slots_kernel(ri_ref, ps_ref, o_ref):
    ri = ri_ref[...].astype(F32)
    ps = ps_ref[...]
    expert = lax.broadcasted_iota(jnp.int32, (ps.shape[0], ri.shape[1]), 0).astype(F32)
    row_id = lax.broadcasted_iota(jnp.int32, ri.shape, 0)
    out = jnp.zeros(ri.shape, F32)
    for k in range(2):
        start = jnp.sum(jnp.where(expert == ri[k:k + 1, :], jnp.broadcast_to(ps, expert.shape), 0.0),
                        axis=0, keepdims=True)
        out = jnp.where(row_id == k, start + ri[2 + k:3 + k, :], out)
    o_ref[...] = out.astype(jnp.int32)


def _slots(ri, pad_start_col):
    n = ri.shape[1]
    tm = SLOT_TILE
    blk = pl.BlockSpec((SUBLANES, tm), lambda i: (0, i))
    return pl.pallas_call(
        _slots_kernel, grid=(n // tm,), in_specs=[blk, _full(pad_start_col.shape)], out_specs=blk,
        out_shape=jax.ShapeDtypeStruct((SUBLANES, n), jnp.int32),
        compiler_params=_params("parallel"), name="slots",
    )(ri, pad_start_col)


def _slot(dest_ref, r, k):
    return dest_ref[k * ROW_TILE + r]


def _dispatch_kernel(dest_ref, last_ref, x_ref, xs_ref, scr_ref, zero_ref, sem, zsem):
    tm, d = x_ref.shape
    nch = _token_rows(d)
    tb = zero_ref.shape[0] // nch

    @pl.when(pl.program_id(0) == 0)
    def _():
        zero_ref[...] = jnp.zeros_like(zero_ref)

        def desc(tok):
            off = pl.multiple_of(jnp.maximum(tok, 0) * nch, nch)
            return pltpu.make_async_copy(zero_ref, xs_ref.at[pl.ds(off, tb * nch), :], zsem)

        def zstart(e, _):
            @pl.when(last_ref[e] >= 0)
            def _():
                desc(last_ref[e]).start()
            return 0

        def zwait(e, _):
            @pl.when(last_ref[e] >= 0)
            def _():
                desc(last_ref[e]).wait()
            return 0

        lax.fori_loop(0, MOE_EXPERTS, zstart, 0)
        nused = last_ref[MOE_EXPERTS]
        nblk = xs_ref.shape[0] // (tb * nch)
        lax.fori_loop(nused, nblk, lambda b, _: (desc(b * tb).start(), 0)[1], 0)
        lax.fori_loop(0, MOE_EXPERTS, zwait, 0)
        lax.fori_loop(nused, nblk, lambda b, _: (desc(b * tb).wait(), 0)[1], 0)

    step = pl.program_id(0)
    slot = step % 2
    scr = scr_ref.at[slot]
    _to_token_tiles(scr, x_ref[...])

    def start(r, _):
        for k in range(2):
            _token_copy(scr, r, xs_ref, _slot(dest_ref, r, k), nch, sem.at[slot]).start(priority=k)
        return 0

    def drain(which):
        def wait(r, _):
            for k in range(2):
                _token_copy(scr_ref.at[which], 0, xs_ref, 0, nch, sem.at[which]).wait()
            return 0
        lax.fori_loop(0, tm, wait, 0, unroll=8)

    lax.fori_loop(0, tm, start, 0, unroll=8)

    @pl.when(step > 0)
    def _():
        drain(1 - slot)

    @pl.when(step == pl.num_programs(0) - 1)
    def _():
        drain(slot)


def _dispatch(dest, last_blk, x1, n_rows):
    n, d = x1.shape
    tm = ROW_TILE
    nch = _token_rows(d)
    return pl.pallas_call(
        _dispatch_kernel, grid=(n // tm,),
        in_specs=[pl.BlockSpec((2 * tm,), lambda i: (i,), memory_space=pltpu.SMEM),
                  pl.BlockSpec(memory_space=pltpu.SMEM),
                  pl.BlockSpec((tm, d), lambda i: (i, 0))],
        out_specs=pl.BlockSpec(memory_space=pl.ANY),
        out_shape=jax.ShapeDtypeStruct((n_rows * nch, LANES), F32),
        scratch_shapes=[pltpu.VMEM((2, tm * nch, LANES), F32), pltpu.VMEM((EXPERT_TILE * nch, LANES), F32),
                        pltpu.SemaphoreType.DMA((2,)), pltpu.SemaphoreType.DMA(())],
        compiler_params=_params("arbitrary"), name="dispatch",
    )(dest, last_blk, x1)


def _expert_kernel(first_ref, count_ref, widx_ref, nused_ref, wg_ref, wu_ref, wd_ref, xs_ref, ys_ref,
                   wgb_ref, wub_ref, wdb_ref, xbuf_ref, ybuf_ref, in_sem, out_sem):
    del widx_ref
    e = pl.program_id(0)
    nused = nused_ref[0]
    d = wg_ref.shape[1]
    nch = _token_rows(d)
    rows = xbuf_ref.shape[1]
    tb = rows // nch
    nblk = xs_ref.shape[0] // rows

    def blk(ref, b):
        return ref.at[pl.ds(pl.multiple_of(b * rows, rows), rows), :]

    def in_copy(b, slot):
        return pltpu.make_async_copy(blk(xs_ref, b), xbuf_ref.at[slot], in_sem.at[slot])

    def out_copy(b, slot):
        return pltpu.make_async_copy(ybuf_ref.at[slot], blk(ys_ref, b), out_sem.at[slot])

    n_in = xbuf_ref.shape[0]

    @pl.when(e == 0)
    def _():
        for b0 in range(n_in - 1):
            @pl.when(b0 < nused)
            def _():
                in_copy(b0, b0).start()

    @pl.when(count_ref[e] > 0)
    def _():
        wgb_ref[...] = wg_ref[0].astype(BF16)
        wub_ref[...] = wu_ref[0].astype(BF16)
        wdb_ref[...] = wd_ref[0].astype(BF16)

    def body(b, _):
        slot = b % n_in
        oslot = b % 2
        in_copy(b, slot).wait()

        @pl.when(b + n_in - 1 < nused)
        def _():
            in_copy(b + n_in - 1, (b + n_in - 1) % n_in).start()

        @pl.when(b >= 2)
        def _():
            out_copy(b - 2, oslot).wait()

        xb = _from_token_tiles(xbuf_ref.at[slot], tb, d).astype(BF16)
        g = _dot(xb, wgb_ref[...])
        u = _dot(xb, wub_ref[...])
        hmid = g * jax.nn.sigmoid(g) * u
        _to_token_tiles(ybuf_ref.at[oslot], _dot(hmid.astype(BF16), wdb_ref[...]))
        out_copy(b, oslot).start()
        return 0

    lax.fori_loop(first_ref[e], first_ref[e] + count_ref[e], body, 0)

    @pl.when(e == pl.num_programs(0) - 1)
    def _():
        for back in (2, 1):
            @pl.when(nused >= back)
            def _():
                out_copy(nused - back, (nused - back) % 2).wait()

        ybuf_ref[0] = jnp.zeros(ybuf_ref.shape[1:], F32)
        lax.fori_loop(nused, nblk, lambda b, _: (out_copy(b, 0).start(), 0)[1], 0)
        lax.fori_loop(nused, nblk, lambda b, _: (out_copy(b, 0).wait(), 0)[1], 0)


def _experts(first_blk, blk_count, w_idx, nused, xs, w_gate, w_up, w_down):
    n_exp, d, dff = w_gate.shape
    nch = _token_rows(d)
    rows = EXPERT_TILE * nch
    w_spec = lambda shape: pl.BlockSpec(shape, lambda e, fb, bc, wi, nu: (wi[e], 0, 0))
    any_spec = pl.BlockSpec(memory_space=pl.ANY)
    grid_spec = pltpu.PrefetchScalarGridSpec(
        num_scalar_prefetch=4, grid=(n_exp,),
        in_specs=[w_spec((1, d, dff)), w_spec((1, d, dff)), w_spec((1, dff, d)), any_spec],
        out_specs=any_spec,
        scratch_shapes=[pltpu.VMEM((d, dff), BF16), pltpu.VMEM((d, dff), BF16), pltpu.VMEM((dff, d), BF16),
                        pltpu.VMEM((EXPERT_IN_SLOTS, rows, LANES), F32), pltpu.VMEM((2, rows, LANES), F32),
                        pltpu.SemaphoreType.DMA((EXPERT_IN_SLOTS,)), pltpu.SemaphoreType.DMA((2,))],
    )
    return pl.pallas_call(
        _expert_kernel, grid_spec=grid_spec, out_shape=jax.ShapeDtypeStruct(xs.shape, F32),
        compiler_params=_params("arbitrary"), name="experts",
    )(first_blk, blk_count, w_idx, nused, w_gate, w_up, w_down, xs)


def _combine_kernel(dest_ref, dest_next_ref, x1_ref, rw_ref, g_ref, b_ref, ys_ref, o_ref, buf_ref, sem):
    tm, d = x1_ref.shape
    nch = _token_rows(d)
    step = pl.program_id(0)
    slot = step % 2

    def start(idx_ref, which, r):
        for k in range(2):
            _token_copy(ys_ref, _slot(idx_ref, r, k), buf_ref.at[which, k], r, nch, sem.at[which]).start(priority=k)

    def drain(which):
        def wait(r, _):
            for k in range(2):
                _token_copy(ys_ref, 0, buf_ref.at[which, k], 0, nch, sem.at[which]).wait()
            return 0
        lax.fori_loop(0, tm, wait, 0, unroll=8)

    @pl.when(step == 0)
    def _():
        lax.fori_loop(0, tm, lambda r, _: (start(dest_ref, 0, r), 0)[1], 0, unroll=8)

    drain(slot)
    rows = tm // COMBINE_CHUNKS
    for ch in range(COMBINE_CHUNKS):
        rs = slice(ch * rows, (ch + 1) * rows)
        tiles = pl.ds(ch * rows * nch, rows * nch)
        y0 = _from_token_tiles(buf_ref.at[slot, 0, tiles], rows, d)
        y1 = _from_token_tiles(buf_ref.at[slot, 1, tiles], rows, d)
        rw = rw_ref[rs, :]
        ffn = rw[:, 0:1] * y0 + rw[:, 1:2] * y1
        o_ref[rs, :] = _layer_norm(DEEPNORM_ALPHA * x1_ref[rs, :] + ffn, g_ref[...], b_ref[...])
        for r in range(ch * rows, (ch + 1) * rows):
            start(dest_next_ref, 1 - slot, r)

    @pl.when(step == pl.num_programs(0) - 1)
    def _():
        drain(1 - slot)


def _combine(dest, x1, rw, ln_g, ln_b, ys):
    n, d = x1.shape
    tm = ROW_TILE
    nch = _token_rows(d)
    last = n // tm - 1
    row = lambda w: pl.BlockSpec((tm, w), lambda i: (i, 0))
    return pl.pallas_call(
        _combine_kernel, grid=(n // tm,),
        in_specs=[pl.BlockSpec((2 * tm,), lambda i: (i,), memory_space=pltpu.SMEM),
                  pl.BlockSpec((2 * tm,), lambda i: (jnp.minimum(i + 1, last),), memory_space=pltpu.SMEM),
                  row(d), row(LANES), _full(ln_g.shape), _full(ln_b.shape),
                  pl.BlockSpec(memory_space=pl.ANY)],
        out_specs=row(d),
        out_shape=jax.ShapeDtypeStruct((n, d), F32),
        scratch_shapes=[pltpu.VMEM((2, 2, tm * nch, LANES), F32), pltpu.SemaphoreType.DMA((2,))],
        compiler_params=_params("arbitrary"), name="combine",
    )(dest, dest, x1, rw, ln_g, ln_b, ys)


def _rope_tables(seq):
    half = ATTN_HEAD_DIM // 2
    inv_freq = ROPE_THETA ** (-jnp.arange(half, dtype=F32) / half)
    ang = jnp.arange(seq, dtype=F32)[:, None] * inv_freq[None, :]
    cos = jnp.cos(ang)
    sin = jnp.sin(ang)
    cos_h = jnp.concatenate([cos, cos], axis=1)
    sin_h = jnp.concatenate([-sin, sin], axis=1)
    return jnp.tile(cos_h, (1, ATTN_HEADS)), jnp.tile(sin_h, (1, ATTN_HEADS))


def _pad_lanes(a, width=LANES):
    return jnp.pad(a, ((0, 0), (0, width - a.shape[1])))


def kernel(x, ln0_g, ln0_b, w_in, conv_w, conv_b, w_mq, w_mk, b_i, b_f, gn_g, skip, w_attn_up, w_mlstm_up, w_out,
           ln1_g, ln1_b, w_router_group, b_router_group, w_router_expert, b_router_expert, w_gate, w_up, w_down,
           ln2_g, ln2_b):
    batch, seq, d = x.shape
    n = batch * seq
    assert seq % ROW_TILE == 0 and ROW_TILE == MOBA_BLOCK and w_in.shape[0] == DEPTH
    x2 = x.reshape(n, d)
    vec = lambda a: a.reshape(1, -1).astype(F32)

    w = w_in[0]
    c_if = 3 * ATTN_WIDTH + 3 * MLSTM_WIDTH
    c_g = c_if + 2 * MLSTM_HEADS
    wqkv = w[:, :3 * ATTN_WIDTH].astype(BF16)
    wuvo = w[:, 3 * ATTN_WIDTH:c_if].astype(BF16)
    w_if = w[:, c_if:c_g]
    wif = _pad_lanes(w_if).astype(BF16)
    wift = w_if.T.astype(BF16)
    wg = w[:, c_g:].astype(BF16)
    cos, sin = _rope_tables(seq)

    q, k, v, kmean, u, vm, o, ifc, ift, ga, gm, xn = _inproj(
        x2, vec(ln0_g), vec(ln0_b), wqkv, wuvo, wif, wift, wg, cos, sin, batch, seq)

    nb = seq // MOBA_BLOCK
    km = kmean.reshape(batch, nb, ATTN_HEADS, ATTN_HEAD_DIM).transpose(0, 2, 1, 3)
    ya = _moba(q, k, v, km).reshape(n, ATTN_WIDTH)

    b_if = jnp.concatenate([b_i[0], b_f[0]]).astype(F32)
    ym = _mlstm(u, vm, o, ifc, ift, conv_w[0], vec(conv_b[0]), w_mq[0].transpose(0, 2, 1).astype(BF16),
                w_mk[0].astype(BF16), _pad_lanes(b_if[None, :]), b_if[:, None],
                gn_g[0].astype(F32)[:, None], skip[0].astype(F32)[:, None], batch, seq)

    w_r = _pad_lanes(jnp.concatenate([w_router_expert[0], w_router_group[0]], axis=1))
    w_r_hi = w_r.astype(BF16)
    w_r_lo = (w_r - w_r_hi.astype(F32)).astype(BF16)
    w_rc = jnp.concatenate([w_r_hi.T, w_r_lo.T], axis=0)
    b_r = _pad_lanes(jnp.concatenate([b_router_expert[0], b_router_group[0]])[None, :]).T
    x1, ri, rw, counts = _mix(
        xn, ya, ym, ga, gm, w_attn_up[0].astype(BF16), w_mlstm_up[0].astype(BF16),
        w_out[0].astype(BF16), vec(ln1_g[0]), vec(ln1_b[0]), w_rc, b_r)

    tb = EXPERT_TILE
    nblk = (2 * n) // tb + MOE_EXPERTS
    cnt = counts[:MOE_EXPERTS, 0].astype(jnp.int32)
    nblk_e = (cnt + tb - 1) // tb
    blk_end = jnp.cumsum(nblk_e)
    pad_start = (blk_end - nblk_e) * tb
    nused = blk_end[-1:]
    ids = jnp.arange(MOE_EXPERTS, dtype=jnp.int32)
    prev_used = jnp.max(jnp.where((ids[None, :] <= ids[:, None]) & (nblk_e[None, :] > 0), ids[None, :], -1), axis=1)
    first_used = jnp.min(jnp.where(nblk_e > 0, ids, MOE_EXPERTS - 1))
    w_idx = jnp.where(prev_used >= 0, prev_used, first_used).astype(jnp.int32)
    last_blk = jnp.where(nblk_e > 0, (blk_end - 1) * tb, -1)
    last_blk = jnp.concatenate([last_blk, nused]).astype(jnp.int32)
    dest = _slots(ri, pad_start.astype(F32)[:, None])
    dest = dest[:2].reshape(2, n // ROW_TILE, ROW_TILE).transpose(1, 0, 2).reshape(2 * n)

    xs = _dispatch(dest, last_blk, x1, nblk * tb)
    ys = _experts((blk_end - nblk_e).astype(jnp.int32), nblk_e.astype(jnp.int32), w_idx, nused.astype(jnp.int32),
                  xs, w_gate[0], w_up[0], w_down[0])
    out = _combine(dest, x1, rw, vec(ln2_g[0]), vec(ln2_b[0]), ys)
    return out.reshape(batch, seq, d)
```

```python
import functools
import math

import jax
import jax.numpy as jnp
from jax import lax
from jax.experimental import pallas as pl
from jax.experimental.pallas import tpu as pltpu

F32 = jnp.float32
BF16 = jnp.bfloat16

ATTN_HEADS = 8
ATTN_HEAD_DIM = 64
ATTN_WIDTH = ATTN_HEADS * ATTN_HEAD_DIM
MOBA_BLOCK = 256
MOBA_TOPK = 3
ROPE_THETA = 10000.0
MLSTM_HEADS = 4
MLSTM_HEAD_DIM = 128
MLSTM_WIDTH = MLSTM_HEADS * MLSTM_HEAD_DIM
MLSTM_CONV = 4
MOE_GROUPS = 8
MOE_EXPERTS_PER_GROUP = 8
MOE_EXPERTS = MOE_GROUPS * MOE_EXPERTS_PER_GROUP
MOE_D_FF = 512
LN_EPS = 1e-5
GN_EPS = 1e-6
DEPTH = 1
DEEPNORM_ALPHA = (2 * DEPTH) ** 0.25

LANES = 128
SUBLANES = 8
ROW_TILE = 256
EXPERT_TILE = 256
EXPERT_IN_SLOTS = 4
INPROJ_CHAINS = 2
SLOT_TILE = 2048
MIX_CHAINS = 4
VMEM_LIMIT = 48 * 1024 * 1024
LOG2_E = math.log2(math.e)

NEG_INF = float("-inf")


def _params(*sem):
    return pltpu.CompilerParams(dimension_semantics=sem, vmem_limit_bytes=VMEM_LIMIT)


def _dot(a, b):
    return jnp.dot(a, b, preferred_element_type=F32)


def _dot_nt(a, b):
    return lax.dot_general(a, b, (((1,), (1,)), ((), ())), preferred_element_type=F32)


def _dot_tn(a, b):
    return lax.dot_general(a, b, (((0,), (0,)), ((), ())), preferred_element_type=F32)


def _split3(x):
    x1 = x.astype(BF16)
    r1 = x - x1.astype(F32)
    x2 = r1.astype(BF16)
    r2 = r1 - x2.astype(F32)
    return x1, x2, r2.astype(BF16)


def _layer_norm(x, g, b):
    mu = jnp.mean(x, axis=-1, keepdims=True)
    xc = x - mu
    var = jnp.mean(xc * xc, axis=-1, keepdims=True)
    return xc * lax.rsqrt(var + LN_EPS) * g + b


def _log_sigmoid(x):
    return jnp.minimum(x, 0.0) - jnp.log1p(jnp.exp(-jnp.abs(x)))


def _full(shape):
    nd = len(shape)
    return pl.BlockSpec(shape, lambda *_: (0,) * nd)


def _run_skewed(phases, chains, rows):
    states = [dict() for _ in range(chains)]
    for t in range(chains + len(phases) - 1):
        for c in range(chains):
            if 0 <= t - c < len(phases):
                phases[t - c](states[c], c, slice(c * rows, (c + 1) * rows))


def _inproj_kernel(x_ref, g_ref, b_ref, wqkv_ref, wuvo_ref, wift_ref, wg_ref, cos_ref, sin_ref,
                   q_ref, k_ref, v_ref, km_ref, u_ref, vm_ref, o_ref, ift_ref, ga_ref, gm_ref, xn_ref):
    tm = ROW_TILE
    lane = lax.broadcasted_iota(jnp.int32, (tm, ATTN_WIDTH), 1)
    first_half = (lane % ATTN_HEAD_DIM) < (ATTN_HEAD_DIM // 2)

    def norm(st, c, rs):
        xn = _layer_norm(x_ref[rs, :], g_ref[...], b_ref[...])
        xn_ref[rs, :] = xn
        st["xb"] = xn.astype(BF16)

    def qkv_matmul(st, c, rs):
        st["zqkv"] = _dot(st["xb"], wqkv_ref[...])

    def attn_outputs(st, c, rs):
        zqkv = st.pop("zqkv")
        cos = cos_ref[rs, :]
        sin = sin_ref[rs, :]

        def rope(t):
            fwd = pltpu.roll(t, ATTN_WIDTH - ATTN_HEAD_DIM // 2, axis=1)
            bwd = pltpu.roll(t, ATTN_HEAD_DIM // 2, axis=1)
            return t * cos + jnp.where(first_half, fwd, bwd) * sin

        q = rope(zqkv[:, :ATTN_WIDTH]) * (ATTN_HEAD_DIM ** -0.5 * LOG2_E)
        k = rope(zqkv[:, ATTN_WIDTH:2 * ATTN_WIDTH])
        v = zqkv[:, 2 * ATTN_WIDTH:]
        km_ref[c] = jnp.mean(k, axis=0, keepdims=True)
        qt = q.T
        vt = v.T
        for h in range(ATTN_HEADS):
            sl = slice(h * ATTN_HEAD_DIM, (h + 1) * ATTN_HEAD_DIM)
            q_ref[0, h, :, rs] = qt[sl, :].astype(BF16)
            k_ref[0, h, rs, :] = k[:, sl].astype(BF16)
            v_ref[0, h, :, rs] = vt[sl, :].astype(BF16)

    def uvo_matmul(st, c, rs):
        st["zuvo"] = _dot(st["xb"], wuvo_ref[...])

    def mlstm_outputs(st, c, rs):
        zuvo = st.pop("zuvo")
        u_ref[rs, :] = zuvo[:, :MLSTM_WIDTH]
        vm_ref[:, rs] = zuvo[:, MLSTM_WIDTH:2 * MLSTM_WIDTH].T.astype(BF16)
        o_ref[:, rs] = zuvo[:, 2 * MLSTM_WIDTH:].T
        ift_ref[:, rs] = _dot_nt(wift_ref[...], st["xb"])

    def gate_matmul(st, c, rs):
        st["zg"] = _dot(st.pop("xb"), wg_ref[...])

    def gate_outputs(st, c, rs):
        zg = st.pop("zg")
        d = ga_ref.shape[1]
        ga_ref[rs, :] = jax.nn.sigmoid(zg[:, :d]).astype(BF16)
        gm_ref[rs, :] = jax.nn.sigmoid(zg[:, d:]).astype(BF16)

    _run_skewed((norm, qkv_matmul, attn_outputs, uvo_matmul, mlstm_outputs, gate_matmul, gate_outputs),
                x_ref.shape[0] // tm, tm)


def _inproj(x2, ln_g, ln_b, wqkv, wuvo, wift, wg, cos, sin, batch, seq):
    n, d = x2.shape
    chains = INPROJ_CHAINS
    tm = chains * ROW_TILE
    assert seq % tm == 0
    nsb = seq // tm
    hd = ATTN_HEAD_DIM
    row = lambda w: pl.BlockSpec((tm, w), lambda i: (i, 0))
    col = lambda h: pl.BlockSpec((h, tm), lambda i: (0, i))
    head = pl.BlockSpec((1, ATTN_HEADS, tm, hd), lambda i: (i // nsb, 0, i % nsb, 0))
    head_t = pl.BlockSpec((1, ATTN_HEADS, hd, tm), lambda i: (i // nsb, 0, 0, i % nsb))
    tab = pl.BlockSpec((tm, ATTN_WIDTH), lambda i: (i % nsb, 0))
    head_shape = jax.ShapeDtypeStruct((batch, ATTN_HEADS, seq, hd), BF16)
    head_t_shape = jax.ShapeDtypeStruct((batch, ATTN_HEADS, hd, seq), BF16)
    out_shape = (
        head_t_shape, head_shape, head_t_shape,
        jax.ShapeDtypeStruct((n // ROW_TILE, 1, ATTN_WIDTH), F32),
        jax.ShapeDtypeStruct((n, MLSTM_WIDTH), F32),
        jax.ShapeDtypeStruct((MLSTM_WIDTH, n), BF16),
        jax.ShapeDtypeStruct((MLSTM_WIDTH, n), F32),
        jax.ShapeDtypeStruct((SUBLANES, n), F32),
        jax.ShapeDtypeStruct((n, d), BF16),
        jax.ShapeDtypeStruct((n, d), BF16),
        jax.ShapeDtypeStruct((n, d), F32),
    )
    out_specs = (
        head_t, head, head_t,
        pl.BlockSpec((chains, 1, ATTN_WIDTH), lambda i: (i, 0, 0)),
        row(MLSTM_WIDTH), col(MLSTM_WIDTH), col(MLSTM_WIDTH),
        col(SUBLANES),
        row(d), row(d), row(d),
    )
    in_specs = [row(d), _full(ln_g.shape), _full(ln_b.shape), _full(wqkv.shape), _full(wuvo.shape),
                _full(wift.shape), _full(wg.shape), tab, tab]
    return pl.pallas_call(
        _inproj_kernel, grid=(n // tm,), in_specs=in_specs, out_specs=out_specs, out_shape=out_shape,
        compiler_params=_params("parallel"), name="inproj",
    )(x2, ln_g, ln_b, wqkv, wuvo, wift, wg, cos, sin)


def _moba_kernel(qt_ref, k_ref, vt_ref, km_ref, o_ref, bias_ref, m_ref, l_ref, acc_ref, s_ref):
    i = pl.program_id(1)
    blk = MOBA_BLOCK
    hd = ATTN_HEAD_DIM
    heads = ATTN_HEADS
    nb = k_ref.shape[2] // blk
    blk_id = lax.broadcasted_iota(jnp.int32, (nb, blk), 0)
    key_pos = lax.broadcasted_iota(jnp.int32, (blk, blk), 0)
    qry_pos = lax.broadcasted_iota(jnp.int32, (blk, blk), 1)
    causal = key_pos <= qry_pos

    for h in range(heads):
        qt = qt_ref[0, h]
        km = km_ref[0, h]
        km_hi = km.astype(BF16)
        km_lo = (km - km_hi.astype(F32)).astype(BF16)
        gate = _dot(km_hi, qt) + _dot(km_lo, qt)
        gate = jnp.where(blk_id < i, gate, NEG_INF)
        for j in range(nb - 1):
            row = gate[j:j + 1, :]
            beats = (gate > row) | ((gate == row) & (blk_id < j))
            cnt = jnp.sum(jnp.where(beats, 1.0, 0.0), axis=0, keepdims=True)
            sel = (cnt < float(MOBA_TOPK)) & (row > NEG_INF)
            bias_ref[j * heads + h] = jnp.where(sel, 0.0, NEG_INF)
    for h in range(heads):
        bias_ref[i * heads + h] = jnp.zeros((1, blk), F32)

    def scores(h, j, own_block):
        qt = qt_ref[0, h]
        half = blk // 2
        m_tile = None
        for c in range(2):
            rows = slice(c * half, (c + 1) * half)
            s = _dot(k_ref[0, h, pl.ds(pl.multiple_of(j * blk + c * half, half), half), :], qt)
            if own_block:
                s = jnp.where(causal[rows], s, NEG_INF)
            s_ref[j * heads + h, rows, :] = s
            m_c = jnp.max(s, axis=0, keepdims=True)
            m_tile = m_c if m_tile is None else jnp.maximum(m_tile, m_c)
        return m_tile

    for h in range(heads):
        m_ref[h] = scores(h, i, True)

    def past_scores(j, _):
        for h in range(heads):
            m_ref[h] = jnp.maximum(m_ref[h], scores(h, j, False) + bias_ref[j * heads + h])
        return 0

    lax.fori_loop(0, i, past_scores, 0)

    l_ref[...] = jnp.zeros_like(l_ref)
    acc_ref[...] = jnp.zeros_like(acc_ref)

    def accumulate(j, _):
        off = pl.multiple_of(j * blk, blk)
        for h in range(heads):
            p = jnp.exp2(s_ref[j * heads + h] - (m_ref[h] - bias_ref[j * heads + h]))
            l_ref[h] += jnp.sum(p, axis=0, keepdims=True)
            acc_ref[h] += _dot(vt_ref[0, h, :, pl.ds(off, blk)], p.astype(BF16))
        return 0

    lax.fori_loop(0, i + 1, accumulate, 0)
    yt = acc_ref[...] / l_ref[...]
    o_ref[0] = yt.reshape(heads * hd, blk).T.astype(BF16)


def _moba(qt, k, vt, km):
    batch, heads, seq, hd = k.shape
    blk = MOBA_BLOCK
    nb = seq // blk
    return pl.pallas_call(
        _moba_kernel, grid=(batch, nb),
        in_specs=[
            pl.BlockSpec((1, heads, hd, blk), lambda b, i: (b, 0, 0, i)),
            pl.BlockSpec((1, heads, seq, hd), lambda b, i: (b, 0, 0, 0)),
            pl.BlockSpec((1, heads, hd, seq), lambda b, i: (b, 0, 0, 0)),
            pl.BlockSpec((1, heads, nb, hd), lambda b, i: (b, 0, 0, 0)),
        ],
        out_specs=pl.BlockSpec((1, blk, heads * hd), lambda b, i: (b, i, 0)),
        out_shape=jax.ShapeDtypeStruct((batch, seq, heads * hd), BF16),
        scratch_shapes=[pltpu.VMEM((nb * heads, 1, blk), F32), pltpu.VMEM((heads, 1, blk), F32),
                        pltpu.VMEM((heads, 1, blk), F32), pltpu.VMEM((heads, hd, blk), F32),
                        pltpu.VMEM((nb * heads, blk, blk), F32)],
        compiler_params=_params("parallel", "arbitrary"), name="moba",
    )(qt, k, vt, km)


def _mlstm_kernel(u_ref, vmt_ref, ot_ref, ift_ref, cw_ref, cb_ref, wqt_ref, wk_ref, bcol_ref,
                  gn_ref, skip_ref, y_ref, ext_ref, c_ref, n_ref, m_ref, yt_ref):
    tm = u_ref.shape[0]
    hd = MLSTM_HEAD_DIM
    halo = SUBLANES

    @pl.when(pl.program_id(1) == 0)
    def _():
        ext_ref[0:halo, :] = jnp.zeros((halo, MLSTM_WIDTH), F32)
        c_ref[...] = jnp.zeros_like(c_ref)
        n_ref[...] = jnp.zeros_like(n_ref)
        m_ref[...] = jnp.zeros_like(m_ref)

    u = u_ref[...]
    ext_ref[halo:halo + tm, :] = u
    acc = jnp.broadcast_to(cb_ref[...], u.shape)
    for j in range(MLSTM_CONV):
        acc = acc + cw_ref[j:j + 1, :] * ext_ref[halo - (MLSTM_CONV - 1) + j:halo - (MLSTM_CONV - 1) + j + tm, :]
    ext_ref[0:halo, :] = u[tm - halo:, :]
    uc = acc * jax.nn.sigmoid(acc)

    gr = ift_ref[...] + bcol_ref[...]
    rows = lax.broadcasted_iota(jnp.int32, (tm, tm), 0)
    cols = lax.broadcasted_iota(jnp.int32, (tm, tm), 1)
    causal_t = rows <= cols
    triu = jnp.where(causal_t, 1.0, 0.0).astype(BF16)
    r1, r2, r3 = _split3(_log_sigmoid(gr))
    bcum_r = _dot(r1, triu) + _dot(r2, triu) + _dot(r3, triu)
    key_rows = gr[:MLSTM_HEADS, :] - bcum_r[MLSTM_HEADS:, :]
    key_cols = jnp.concatenate([key_rows, jnp.zeros((LANES - MLSTM_HEADS, tm), F32)], axis=0).T

    uct = uc.T

    def decay_weights(st, h, hs):
        fl = MLSTM_HEADS + h
        b_row = bcum_r[fl:fl + 1, :]
        st["key_row"] = key_rows[h:h + 1, :]
        st["key_col"] = key_cols[:, h:h + 1]
        m_prev = m_ref[h][:, 0:1]
        dlog = jnp.where(causal_t, st["key_col"] + b_row, NEG_INF)
        inter = b_row + m_prev
        m_t = jnp.maximum(inter, jnp.max(dlog, axis=0, keepdims=True))
        st["w_intra"] = jnp.exp(dlog - m_t)
        st["w_inter"] = jnp.exp(inter - m_t)
        st["m_t"], st["m_prev"], st["b_end"] = m_t, m_prev, b_row[:, tm - 1:tm]

    def project(st, h, hs):
        st["qtb"] = _dot(wqt_ref[h], uct[hs, :].astype(BF16)).astype(BF16)
        st["k"] = _dot(uc[:, hs].astype(BF16), wk_ref[h]) * (hd ** -0.5)

    def scores(st, h, hs):
        st["s"] = _dot(st["k"].astype(BF16), st["qtb"]) * st.pop("w_intra")

    def readout(st, h, hs):
        qtb, s, w_inter, m_t = st.pop("qtb"), st.pop("s"), st.pop("w_inter"), st.pop("m_t")
        n_prev = n_ref[h]
        n_hi = n_prev.astype(BF16)
        n_lo = (n_prev - n_hi.astype(F32)).astype(BF16)
        qn = (_dot(n_hi, qtb) + _dot(n_lo, qtb))[0:1, :]
        num = w_inter * _dot(c_ref[h].astype(BF16), qtb) + _dot(vmt_ref[hs, :], s.astype(BF16))
        den = w_inter * qn + jnp.sum(s, axis=0, keepdims=True)
        st["hh"] = num / jnp.maximum(jnp.abs(den), jnp.exp(-m_t))

    def update_state(st, h, hs):
        b_end, m_prev = st.pop("b_end"), st.pop("m_prev")
        m_new = jnp.maximum(b_end + m_prev, jnp.max(b_end + st.pop("key_row"), axis=1, keepdims=True))
        decay = jnp.exp(b_end + m_prev - m_new)
        kw = st.pop("k") * jnp.exp(b_end + st.pop("key_col") - m_new)
        n_prev = n_ref[h]
        c_ref[h] = decay * c_ref[h] + _dot(vmt_ref[hs, :], kw.astype(BF16))
        n_ref[h] = decay * n_prev + jnp.broadcast_to(jnp.sum(kw, axis=0, keepdims=True), n_prev.shape)
        m_ref[h] = jnp.broadcast_to(m_new, (1, LANES))

    def gate_and_norm(st, h, hs):
        hh = jax.nn.sigmoid(ot_ref[hs, :]) * st.pop("hh")
        mu = jnp.mean(hh, axis=0, keepdims=True)
        hc = hh - mu
        var = jnp.mean(hc * hc, axis=0, keepdims=True)
        yt_ref[hs, :] = hc * lax.rsqrt(var + GN_EPS) * gn_ref[hs, :] + skip_ref[hs, :] * uct[hs, :]

    _run_skewed((decay_weights, project, scores, readout, update_state, gate_and_norm), MLSTM_HEADS, hd)
    y_ref[...] = yt_ref[...].T.astype(BF16)


def _mlstm(u, vmt, ot, ift, conv_w, conv_b, wqt, wk, bcol, gn_g, skip, batch, seq):
    n = u.shape[0]
    tm = ROW_TILE
    nc = seq // tm
    row = lambda w: pl.BlockSpec((tm, w), lambda b, c: (b * nc + c, 0))
    col = lambda h: pl.BlockSpec((h, tm), lambda b, c: (0, b * nc + c))
    in_specs = [row(MLSTM_WIDTH), col(MLSTM_WIDTH), col(MLSTM_WIDTH), col(SUBLANES),
                _full(conv_w.shape), _full(conv_b.shape), _full(wqt.shape), _full(wk.shape),
                _full(bcol.shape), _full(gn_g.shape), _full(skip.shape)]
    return pl.pallas_call(
        _mlstm_kernel, grid=(batch, nc), in_specs=in_specs, out_specs=row(MLSTM_WIDTH),
        out_shape=jax.ShapeDtypeStruct((n, MLSTM_WIDTH), BF16),
        scratch_shapes=[pltpu.VMEM((SUBLANES + tm, MLSTM_WIDTH), F32),
                        pltpu.VMEM((MLSTM_HEADS, MLSTM_HEAD_DIM, MLSTM_HEAD_DIM), F32),
                        pltpu.VMEM((MLSTM_HEADS, SUBLANES, MLSTM_HEAD_DIM), F32),
                        pltpu.VMEM((MLSTM_HEADS, 1, LANES), F32),
                        pltpu.VMEM((MLSTM_WIDTH, tm), F32)],
        compiler_params=_params("parallel", "arbitrary"), name="mlstm",
    )(u, vmt, ot, ift, conv_w, conv_b, wqt, wk, bcol, gn_g, skip)


def _mix_kernel(xn_ref, ya_ref, ym_ref, ga_ref, gm_ref, wau_ref, wmu_ref, wout_ref,
                g1_ref, b1_ref, wrc_ref, br_ref,
                x1_ref, ri_ref, rw_ref, cnt_out_ref, cnt_ref):
    @pl.when(pl.program_id(0) == 0)
    def _():
        cnt_ref[...] = jnp.zeros_like(cnt_ref)

    tm = ROW_TILE
    sub = lax.broadcasted_iota(jnp.int32, (LANES, tm), 0).astype(F32)
    big = float(4 * LANES)

    def up_and_mix(st, c, rs):
        a_up = _dot(ya_ref[rs, :], wau_ref[...])
        m_up = _dot(ym_ref[rs, :], wmu_ref[...])
        mix = ga_ref[rs, :].astype(F32) * a_up + gm_ref[rs, :].astype(F32) * m_up
        st["mix"] = mix.astype(BF16)

    def out_and_norm(st, c, rs):
        x1 = _layer_norm(DEEPNORM_ALPHA * xn_ref[rs, :] + _dot(st.pop("mix"), wout_ref[...]), g1_ref[...], b1_ref[...])
        x1_ref[rs, :] = x1
        st["x1"] = x1

    def router_logits(st, c, rs):
        x1 = st.pop("x1")
        x_hi = x1.astype(BF16)
        x_lo = (x1 - x_hi.astype(F32)).astype(BF16)
        both = _dot_nt(wrc_ref[...], x_hi)
        st["logits"] = both[:LANES] + both[LANES:] + _dot_nt(wrc_ref[:LANES, :], x_lo) + br_ref[...]

    def route(st, c, rs):
        logits = st.pop("logits")
        is_g = (sub >= float(MOE_EXPERTS)) & (sub < float(MOE_EXPERTS + MOE_GROUPS))
        gl = jnp.where(is_g, logits, NEG_INF)
        ge = jnp.exp(gl - jnp.max(gl, axis=0, keepdims=True))
        gp = ge / jnp.sum(ge, axis=0, keepdims=True)
        g_w = jnp.max(gp, axis=0, keepdims=True)
        g_idx = jnp.min(jnp.where((gp == g_w) & is_g, sub - float(MOE_EXPERTS), big), axis=0, keepdims=True)
        lo = g_idx * float(MOE_EXPERTS_PER_GROUP)
        in_grp = (sub >= lo) & (sub < lo + float(MOE_EXPERTS_PER_GROUP))
        el = jnp.where(in_grp, logits, NEG_INF)
        v1 = jnp.max(el, axis=0, keepdims=True)
        i1 = jnp.min(jnp.where((el == v1) & in_grp, sub, big), axis=0, keepdims=True)
        el2 = jnp.where(sub == i1, NEG_INF, el)
        v2 = jnp.max(el2, axis=0, keepdims=True)
        i2 = jnp.min(jnp.where((el2 == v2) & in_grp & (sub != i1), sub, big), axis=0, keepdims=True)
        e2 = jnp.exp(v2 - v1)
        w0 = g_w / (1.0 + e2)
        w1 = g_w * e2 / (1.0 + e2)
        rw_ref[rs, :] = jnp.where(sub == 0.0, w0, jnp.where(sub == 1.0, w1, 0.0)).T
        st["i1"], st["i2"] = i1, i2

    def rank(st, c, rs):
        i1, i2 = st.pop("i1"), st.pop("i2")
        is1 = sub == i1
        is2 = sub == i2
        onehot = jnp.where(is1 | is2, 1.0, 0.0)
        rows = lax.broadcasted_iota(jnp.int32, (tm, tm), 0)
        cols = lax.broadcasted_iota(jnp.int32, (tm, tm), 1)
        earlier = jnp.where(rows < cols, 1.0, 0.0).astype(BF16)
        before = _dot(onehot.astype(BF16), earlier) + cnt_ref[...]
        r0 = jnp.sum(jnp.where(is1, before, 0.0), axis=0, keepdims=True)
        r1 = jnp.sum(jnp.where(is2, before, 0.0), axis=0, keepdims=True)
        total = cnt_ref[...] + jnp.sum(onehot, axis=1, keepdims=True)
        cnt_ref[...] = total
        cnt_out_ref[...] = total
        ri_t = jnp.where(sub == 0.0, i1, jnp.where(sub == 1.0, i2, jnp.where(sub == 2.0, r0, jnp.where(sub == 3.0, r1, 0.0))))
        ri_ref[:, rs] = ri_t[:SUBLANES, :].astype(jnp.int32)

    _run_skewed((up_and_mix, out_and_norm, router_logits, route, rank), xn_ref.shape[0] // tm, tm)


def _mix(xn, ya, ym, ga, gm, wau, wmu, wout, g1, b1, wrc, br):
    n, d = xn.shape
    tm = MIX_CHAINS * ROW_TILE
    row = lambda w: pl.BlockSpec((tm, w), lambda i: (i, 0))
    in_specs = [row(d), row(ATTN_WIDTH), row(MLSTM_WIDTH), row(d), row(d),
                _full(wau.shape), _full(wmu.shape), _full(wout.shape), _full(g1.shape), _full(b1.shape),
                _full(wrc.shape), _full(br.shape)]
    out_shape = (jax.ShapeDtypeStruct((n, d), F32), jax.ShapeDtypeStruct((SUBLANES, n), jnp.int32),
                 jax.ShapeDtypeStruct((n, LANES), F32), jax.ShapeDtypeStruct((LANES, 1), F32))
    out_specs = (row(d), pl.BlockSpec((SUBLANES, tm), lambda i: (0, i)), row(LANES), _full((LANES, 1)))
    return pl.pallas_call(
        _mix_kernel, grid=(n // tm,), in_specs=in_specs, out_specs=out_specs, out_shape=out_shape,
        scratch_shapes=[pltpu.VMEM((LANES, 1), F32)],
        compiler_params=_params("arbitrary"), name="mix",
    )(xn, ya, ym, ga, gm, wau, wmu, wout, g1, b1, wrc, br)


def _token_rows(d):
    return d // LANES


def _to_token_tiles(dst_ref, x):
    rows, d = x.shape
    nch = _token_rows(d)
    for c in range(nch):
        dst_ref[pl.ds(c, rows, stride=nch), :] = x[:, c * LANES:(c + 1) * LANES]


def _from_token_tiles(src_ref, rows, d):
    nch = _token_rows(d)
    return jnp.concatenate([src_ref[pl.ds(c, rows, stride=nch), :] for c in range(nch)], axis=1)


def _token_copy(src, src_tok, dst, dst_tok, nch, sem):
    s0 = pl.multiple_of(src_tok * nch, nch)
    d0 = pl.multiple_of(dst_tok * nch, nch)
    return pltpu.make_async_copy(src.at[pl.ds(s0, nch), :], dst.at[pl.ds(d0, nch), :], sem)


def _slots_kernel(ri_ref, ps_ref, o_ref):
    ri = ri_ref[...].astype(F32)
    ps = ps_ref[...]
    expert = lax.broadcasted_iota(jnp.int32, (ps.shape[0], ri.shape[1]), 0).astype(F32)
    row_id = lax.broadcasted_iota(jnp.int32, ri.shape, 0)
    out = jnp.zeros(ri.shape, F32)
    for k in range(2):
        start = jnp.sum(jnp.where(expert == ri[k:k + 1, :], jnp.broadcast_to(ps, expert.shape), 0.0),
                        axis=0, keepdims=True)
        out = jnp.where(row_id == k, start + ri[2 + k:3 + k, :], out)
    o_ref[...] = out.astype(jnp.int32)


def _slots(ri, pad_start_col):
    n = ri.shape[1]
    tm = SLOT_TILE
    blk = pl.BlockSpec((SUBLANES, tm), lambda i: (0, i))
    return pl.pallas_call(
        _slots_kernel, grid=(n // tm,), in_specs=[blk, _full(pad_start_col.shape)], out_specs=blk,
        out_shape=jax.ShapeDtypeStruct((SUBLANES, n), jnp.int32),
        compiler_params=_params("parallel"), name="slots",
    )(ri, pad_start_col)


def _slot(dest_ref, r, k):
    return dest_ref[k * ROW_TILE + r]


def _dispatch_kernel(dest_ref, last_ref, x_ref, xs_ref, scr_ref, zero_ref, sem, zsem):
    tm, d = x_ref.shape
    nch = _token_rows(d)
    tb = zero_ref.shape[0] // nch

    @pl.when(pl.program_id(0) == 0)
    def _():
        zero_ref[...] = jnp.zeros_like(zero_ref)

        def desc(tok):
            off = pl.multiple_of(jnp.maximum(tok, 0) * nch, nch)
            return pltpu.make_async_copy(zero_ref, xs_ref.at[pl.ds(off, tb * nch), :], zsem)

        def zstart(e, _):
            @pl.when(last_ref[e] >= 0)
            def _():
                desc(last_ref[e]).start()
            return 0

        def zwait(e, _):
            @pl.when(last_ref[e] >= 0)
            def _():
                desc(last_ref[e]).wait()
            return 0

        lax.fori_loop(0, MOE_EXPERTS, zstart, 0)
        nused = last_ref[MOE_EXPERTS]
        nblk = xs_ref.shape[0] // (tb * nch)
        lax.fori_loop(nused, nblk, lambda b, _: (desc(b * tb).start(), 0)[1], 0)
        lax.fori_loop(0, MOE_EXPERTS, zwait, 0)
        lax.fori_loop(nused, nblk, lambda b, _: (desc(b * tb).wait(), 0)[1], 0)

    step = pl.program_id(0)
    slot = step % 2
    scr = scr_ref.at[slot]
    _to_token_tiles(scr, x_ref[...])

    def start(r, _):
        for k in range(2):
            _token_copy(scr, r, xs_ref, _slot(dest_ref, r, k), nch, sem.at[slot]).start(priority=k)
        return 0

    def drain(which):
        def wait(r, _):
            for k in range(2):
                _token_copy(scr_ref.at[which], 0, xs_ref, 0, nch, sem.at[which]).wait()
            return 0
        lax.fori_loop(0, tm, wait, 0, unroll=8)

    lax.fori_loop(0, tm, start, 0, unroll=8)

    @pl.when(step > 0)
    def _():
        drain(1 - slot)

    @pl.when(step == pl.num_programs(0) - 1)
    def _():
        drain(slot)


def _dispatch(dest, last_blk, x1, n_rows):
    n, d = x1.shape
    tm = ROW_TILE
    nch = _token_rows(d)
    return pl.pallas_call(
        _dispatch_kernel, grid=(n // tm,),
        in_specs=[pl.BlockSpec((2 * tm,), lambda i: (i,), memory_space=pltpu.SMEM),
                  pl.BlockSpec(memory_space=pltpu.SMEM),
                  pl.BlockSpec((tm, d), lambda i: (i, 0))],
        out_specs=pl.BlockSpec(memory_space=pl.ANY),
        out_shape=jax.ShapeDtypeStruct((n_rows * nch, LANES), F32),
        scratch_shapes=[pltpu.VMEM((2, tm * nch, LANES), F32), pltpu.VMEM((EXPERT_TILE * nch, LANES), F32),
                        pltpu.SemaphoreType.DMA((2,)), pltpu.SemaphoreType.DMA(())],
        compiler_params=_params("arbitrary"), name="dispatch",
    )(dest, last_blk, x1)


def _expert_kernel(first_ref, count_ref, widx_ref, nused_ref, wg_ref, wu_ref, wd_ref, xs_ref, ys_ref,
                   wgb_ref, wub_ref, wdb_ref, xbuf_ref, ybuf_ref, in_sem, out_sem):
    del widx_ref
    e = pl.program_id(0)
    nused = nused_ref[0]
    d = wg_ref.shape[1]
    nch = _token_rows(d)
    rows = xbuf_ref.shape[1]
    tb = rows // nch
    nblk = xs_ref.shape[0] // rows

    def blk(ref, b):
        return ref.at[pl.ds(pl.multiple_of(b * rows, rows), rows), :]

    def in_copy(b, slot):
        return pltpu.make_async_copy(blk(xs_ref, b), xbuf_ref.at[slot], in_sem.at[slot])

    def out_copy(b, slot):
        return pltpu.make_async_copy(ybuf_ref.at[slot], blk(ys_ref, b), out_sem.at[slot])

    n_in = xbuf_ref.shape[0]

    @pl.when(e == 0)
    def _():
        for b0 in range(n_in - 1):
            @pl.when(b0 < nused)
            def _():
                in_copy(b0, b0).start()

    @pl.when(count_ref[e] > 0)
    def _():
        wgb_ref[...] = wg_ref[0].astype(BF16)
        wub_ref[...] = wu_ref[0].astype(BF16)
        wdb_ref[...] = wd_ref[0].astype(BF16)

    def body(b, _):
        slot = b % n_in
        oslot = b % 2
        in_copy(b, slot).wait()

        @pl.when(b + n_in - 1 < nused)
        def _():
            in_copy(b + n_in - 1, (b + n_in - 1) % n_in).start()

        @pl.when(b >= 2)
        def _():
            out_copy(b - 2, oslot).wait()

        xb = _from_token_tiles(xbuf_ref.at[slot], tb, d).astype(BF16)
        g = _dot(xb, wgb_ref[...])
        u = _dot(xb, wub_ref[...])
        hmid = g * jax.nn.sigmoid(g) * u
        _to_token_tiles(ybuf_ref.at[oslot], _dot(hmid.astype(BF16), wdb_ref[...]))
        out_copy(b, oslot).start()
        return 0

    lax.fori_loop(first_ref[e], first_ref[e] + count_ref[e], body, 0)

    @pl.when(e == pl.num_programs(0) - 1)
    def _():
        for back in (2, 1):
            @pl.when(nused >= back)
            def _():
                out_copy(nused - back, (nused - back) % 2).wait()

        ybuf_ref[0] = jnp.zeros(ybuf_ref.shape[1:], F32)
        lax.fori_loop(nused, nblk, lambda b, _: (out_copy(b, 0).start(), 0)[1], 0)
        lax.fori_loop(nused, nblk, lambda b, _: (out_copy(b, 0).wait(), 0)[1], 0)


def _experts(first_blk, blk_count, w_idx, nused, xs, w_gate, w_up, w_down):
    n_exp, d, dff = w_gate.shape
    nch = _token_rows(d)
    rows = EXPERT_TILE * nch
    w_spec = lambda shape: pl.BlockSpec(shape, lambda e, fb, bc, wi, nu: (wi[e], 0, 0))
    any_spec = pl.BlockSpec(memory_space=pl.ANY)
    grid_spec = pltpu.PrefetchScalarGridSpec(
        num_scalar_prefetch=4, grid=(n_exp,),
        in_specs=[w_spec((1, d, dff)), w_spec((1, d, dff)), w_spec((1, dff, d)), any_spec],
        out_specs=any_spec,
        scratch_shapes=[pltpu.VMEM((d, dff), BF16), pltpu.VMEM((d, dff), BF16), pltpu.VMEM((dff, d), BF16),
                        pltpu.VMEM((EXPERT_IN_SLOTS, rows, LANES), F32), pltpu.VMEM((2, rows, LANES), F32),
                        pltpu.SemaphoreType.DMA((EXPERT_IN_SLOTS,)), pltpu.SemaphoreType.DMA((2,))],
    )
    return pl.pallas_call(
        _expert_kernel, grid_spec=grid_spec, out_shape=jax.ShapeDtypeStruct(xs.shape, F32),
        compiler_params=_params("arbitrary"), name="experts",
    )(first_blk, blk_count, w_idx, nused, w_gate, w_up, w_down, xs)


def _combine_kernel(dest_ref, dest_next_ref, x1_ref, rw_ref, g_ref, b_ref, ys_ref, o_ref, buf_ref, sem):
    tm, d = x1_ref.shape
    nch = _token_rows(d)
    step = pl.program_id(0)
    slot = step % 2

    def gather(idx_ref, which):
        def start(r, _):
            for k in range(2):
                _token_copy(ys_ref, _slot(idx_ref, r, k), buf_ref.at[which, k], r, nch,
                            sem.at[which]).start(priority=k)
            return 0
        lax.fori_loop(0, tm, start, 0, unroll=8)

    @pl.when(step == 0)
    def _():
        gather(dest_ref, 0)

    @pl.when(step + 1 < pl.num_programs(0))
    def _():
        gather(dest_next_ref, 1 - slot)

    def wait(r, _):
        for k in range(2):
            _token_copy(ys_ref, 0, buf_ref.at[slot, k], 0, nch, sem.at[slot]).wait()
        return 0

    lax.fori_loop(0, tm, wait, 0, unroll=8)
    rw = rw_ref[...]
    y0 = _from_token_tiles(buf_ref.at[slot, 0], tm, d)
    y1 = _from_token_tiles(buf_ref.at[slot, 1], tm, d)
    ffn = rw[:, 0:1] * y0 + rw[:, 1:2] * y1
    o_ref[...] = _layer_norm(DEEPNORM_ALPHA * x1_ref[...] + ffn, g_ref[...], b_ref[...])


def _combine(dest, x1, rw, ln_g, ln_b, ys):
    n, d = x1.shape
    tm = ROW_TILE
    nch = _token_rows(d)
    last = n // tm - 1
    row = lambda w: pl.BlockSpec((tm, w), lambda i: (i, 0))
    return pl.pallas_call(
        _combine_kernel, grid=(n // tm,),
        in_specs=[pl.BlockSpec((2 * tm,), lambda i: (i,), memory_space=pltpu.SMEM),
                  pl.BlockSpec((2 * tm,), lambda i: (jnp.minimum(i + 1, last),), memory_space=pltpu.SMEM),
                  row(d), row(LANES), _full(ln_g.shape), _full(ln_b.shape),
                  pl.BlockSpec(memory_space=pl.ANY)],
        out_specs=row(d),
        out_shape=jax.ShapeDtypeStruct((n, d), F32),
        scratch_shapes=[pltpu.VMEM((2, 2, tm * nch, LANES), F32), pltpu.SemaphoreType.DMA((2,))],
        compiler_params=_params("arbitrary"), name="combine",
    )(dest, dest, x1, rw, ln_g, ln_b, ys)


def _rope_tables(seq):
    half = ATTN_HEAD_DIM // 2
    inv_freq = ROPE_THETA ** (-jnp.arange(half, dtype=F32) / half)
    ang = jnp.arange(seq, dtype=F32)[:, None] * inv_freq[None, :]
    cos = jnp.cos(ang)
    sin = jnp.sin(ang)
    cos_h = jnp.concatenate([cos, cos], axis=1)
    sin_h = jnp.concatenate([-sin, sin], axis=1)
    return jnp.tile(cos_h, (1, ATTN_HEADS)), jnp.tile(sin_h, (1, ATTN_HEADS))


def _pad_lanes(a, width=LANES):
    return jnp.pad(a, ((0, 0), (0, width - a.shape[1])))


def kernel(x, ln0_g, ln0_b, w_in, conv_w, conv_b, w_mq, w_mk, b_i, b_f, gn_g, skip, w_attn_up, w_mlstm_up, w_out,
           ln1_g, ln1_b, w_router_group, b_router_group, w_router_expert, b_router_expert, w_gate, w_up, w_down,
           ln2_g, ln2_b):
    batch, seq, d = x.shape
    n = batch * seq
    assert seq % ROW_TILE == 0 and ROW_TILE == MOBA_BLOCK and w_in.shape[0] == DEPTH
    x2 = x.reshape(n, d)
    vec = lambda a: a.reshape(1, -1).astype(F32)

    w = w_in[0]
    c_if = 3 * ATTN_WIDTH + 3 * MLSTM_WIDTH
    c_g = c_if + 2 * MLSTM_HEADS
    wqkv = w[:, :3 * ATTN_WIDTH].astype(BF16)
    wuvo = w[:, 3 * ATTN_WIDTH:c_if].astype(BF16)
    wift = w[:, c_if:c_g].T.astype(BF16)
    wg = w[:, c_g:].astype(BF16)
    cos, sin = _rope_tables(seq)

    q, k, v, kmean, u, vm, o, ift, ga, gm, xn = _inproj(
        x2, vec(ln0_g), vec(ln0_b), wqkv, wuvo, wift, wg, cos, sin, batch, seq)

    nb = seq // MOBA_BLOCK
    km = kmean.reshape(batch, nb, ATTN_HEADS, ATTN_HEAD_DIM).transpose(0, 2, 1, 3)
    ya = _moba(q, k, v, km).reshape(n, ATTN_WIDTH)

    b_if = jnp.concatenate([b_i[0], b_f[0]]).astype(F32)
    ym = _mlstm(u, vm, o, ift, conv_w[0], vec(conv_b[0]), w_mq[0].transpose(0, 2, 1).astype(BF16),
                w_mk[0].astype(BF16), b_if[:, None],
                gn_g[0].astype(F32)[:, None], skip[0].astype(F32)[:, None], batch, seq)

    w_r = _pad_lanes(jnp.concatenate([w_router_expert[0], w_router_group[0]], axis=1))
    w_r_hi = w_r.astype(BF16)
    w_r_lo = (w_r - w_r_hi.astype(F32)).astype(BF16)
    w_rc = jnp.concatenate([w_r_hi.T, w_r_lo.T], axis=0)
    b_r = _pad_lanes(jnp.concatenate([b_router_expert[0], b_router_group[0]])[None, :]).T
    x1, ri, rw, counts = _mix(
        xn, ya, ym, ga, gm, w_attn_up[0].astype(BF16), w_mlstm_up[0].astype(BF16),
        w_out[0].astype(BF16), vec(ln1_g[0]), vec(ln1_b[0]), w_rc, b_r)

    tb = EXPERT_TILE
    nblk = (2 * n) // tb + MOE_EXPERTS
    cnt = counts[:MOE_EXPERTS, 0].astype(jnp.int32)
    nblk_e = (cnt + tb - 1) // tb
    blk_end = jnp.cumsum(nblk_e)
    pad_start = (blk_end - nblk_e) * tb
    nused = blk_end[-1:]
    ids = jnp.arange(MOE_EXPERTS, dtype=jnp.int32)
    prev_used = jnp.max(jnp.where((ids[None, :] <= ids[:, None]) & (nblk_e[None, :] > 0), ids[None, :], -1), axis=1)
    first_used = jnp.min(jnp.where(nblk_e > 0, ids, MOE_EXPERTS - 1))
    w_idx = jnp.where(prev_used >= 0, prev_used, first_used).astype(jnp.int32)
    last_blk = jnp.where(nblk_e > 0, (blk_end - 1) * tb, -1)
    last_blk = jnp.concatenate([last_blk, nused]).astype(jnp.int32)
    dest = _slots(ri, pad_start.astype(F32)[:, None])
    dest = dest[:2].reshape(2, n // ROW_TILE, ROW_TILE).transpose(1, 0, 2).reshape(2 * n)

    xs = _dispatch(dest, last_blk, x1, nblk * tb)
    ys = _experts((blk_end - nblk_e).astype(jnp.int32), nblk_e.astype(jnp.int32), w_idx, nused.astype(jnp.int32),
                  xs, w_gate[0], w_up[0], w_down[0])
    out = _combine(dest, x1, rw, vec(ln2_g[0]), vec(ln2_b[0]), ys)
    return out.reshape(batch, seq, d)
```

```python
import functools
import math

import jax
import jax.numpy as jnp
from jax import lax
from jax.experimental import pallas as pl
from jax.experimental.pallas import tpu as pltpu

F32 = jnp.float32
BF16 = jnp.bfloat16

ATTN_HEADS = 8
ATTN_HEAD_DIM = 64
ATTN_WIDTH = ATTN_HEADS * ATTN_HEAD_DIM
MOBA_BLOCK = 256
MOBA_TOPK = 3
ROPE_THETA = 10000.0
MLSTM_HEADS = 4
MLSTM_HEAD_DIM = 128
MLSTM_WIDTH = MLSTM_HEADS * MLSTM_HEAD_DIM
MLSTM_CONV = 4
MOE_GROUPS = 8
MOE_EXPERTS_PER_GROUP = 8
MOE_EXPERTS = MOE_GROUPS * MOE_EXPERTS_PER_GROUP
MOE_D_FF = 512
LN_EPS = 1e-5
GN_EPS = 1e-6
DEPTH = 1
DEEPNORM_ALPHA = (2 * DEPTH) ** 0.25

LANES = 128
SUBLANES = 8
ROW_TILE = 256
EXPERT_TILE = 256
MOE_TILE = 512
EXPERT_IN_SLOTS = 4
INPROJ_CHAINS = 2
SLOT_TILE = 2048
MIX_CHAINS = 4
VMEM_LIMIT = 48 * 1024 * 1024
LOG2_E = math.log2(math.e)

NEG_INF = float("-inf")


def _params(*sem):
    return pltpu.CompilerParams(dimension_semantics=sem, vmem_limit_bytes=VMEM_LIMIT)


def _dot(a, b):
    return jnp.dot(a, b, preferred_element_type=F32)


def _dot_nt(a, b):
    return lax.dot_general(a, b, (((1,), (1,)), ((), ())), preferred_element_type=F32)


def _dot_tn(a, b):
    return lax.dot_general(a, b, (((0,), (0,)), ((), ())), preferred_element_type=F32)


def _split3(x):
    x1 = x.astype(BF16)
    r1 = x - x1.astype(F32)
    x2 = r1.astype(BF16)
    r2 = r1 - x2.astype(F32)
    return x1, x2, r2.astype(BF16)


def _layer_norm(x, g, b):
    mu = jnp.mean(x, axis=-1, keepdims=True)
    xc = x - mu
    var = jnp.mean(xc * xc, axis=-1, keepdims=True)
    return xc * lax.rsqrt(var + LN_EPS) * g + b


def _log_sigmoid(x):
    return jnp.minimum(x, 0.0) - jnp.log1p(jnp.exp(-jnp.abs(x)))


def _full(shape):
    nd = len(shape)
    return pl.BlockSpec(shape, lambda *_: (0,) * nd)


def _run_skewed(phases, chains, rows):
    states = [dict() for _ in range(chains)]
    for t in range(chains + len(phases) - 1):
        for c in range(chains):
            if 0 <= t - c < len(phases):
                phases[t - c](states[c], c, slice(c * rows, (c + 1) * rows))


def _inproj_kernel(x_ref, g_ref, b_ref, wqkv_ref, wuvo_ref, wift_ref, wg_ref, cos_ref, sin_ref,
                   q_ref, k_ref, v_ref, km_ref, u_ref, vm_ref, o_ref, ift_ref, ga_ref, gm_ref, xn_ref):
    tm = ROW_TILE
    lane = lax.broadcasted_iota(jnp.int32, (tm, ATTN_WIDTH), 1)
    first_half = (lane % ATTN_HEAD_DIM) < (ATTN_HEAD_DIM // 2)

    def norm(st, c, rs):
        xn = _layer_norm(x_ref[rs, :], g_ref[...], b_ref[...])
        xn_ref[rs, :] = xn
        st["xb"] = xn.astype(BF16)

    def qkv_matmul(st, c, rs):
        st["zqkv"] = _dot(st["xb"], wqkv_ref[...])

    def attn_outputs(st, c, rs):
        zqkv = st.pop("zqkv")
        cos = cos_ref[rs, :]
        sin = sin_ref[rs, :]

        def rope(t):
            fwd = pltpu.roll(t, ATTN_WIDTH - ATTN_HEAD_DIM // 2, axis=1)
            bwd = pltpu.roll(t, ATTN_HEAD_DIM // 2, axis=1)
            return t * cos + jnp.where(first_half, fwd, bwd) * sin

        q = rope(zqkv[:, :ATTN_WIDTH]) * (ATTN_HEAD_DIM ** -0.5 * LOG2_E)
        k = rope(zqkv[:, ATTN_WIDTH:2 * ATTN_WIDTH])
        v = zqkv[:, 2 * ATTN_WIDTH:]
        km_ref[c] = jnp.mean(k, axis=0, keepdims=True)
        qt = q.T
        vt = v.T
        for h in range(ATTN_HEADS):
            sl = slice(h * ATTN_HEAD_DIM, (h + 1) * ATTN_HEAD_DIM)
            q_ref[0, h, :, rs] = qt[sl, :].astype(BF16)
            k_ref[0, h, rs, :] = k[:, sl].astype(BF16)
            v_ref[0, h, :, rs] = vt[sl, :].astype(BF16)

    def uvo_matmul(st, c, rs):
        st["zuvo"] = _dot(st["xb"], wuvo_ref[...])

    def mlstm_outputs(st, c, rs):
        zuvo = st.pop("zuvo")
        u_ref[rs, :] = zuvo[:, :MLSTM_WIDTH]
        vm_ref[:, rs] = zuvo[:, MLSTM_WIDTH:2 * MLSTM_WIDTH].T.astype(BF16)
        o_ref[:, rs] = zuvo[:, 2 * MLSTM_WIDTH:].T
        ift_ref[:, rs] = _dot_nt(wift_ref[...], st["xb"])

    def gate_matmul(st, c, rs):
        st["zg"] = _dot(st.pop("xb"), wg_ref[...])

    def gate_outputs(st, c, rs):
        zg = st.pop("zg")
        d = ga_ref.shape[1]
        ga_ref[rs, :] = jax.nn.sigmoid(zg[:, :d]).astype(BF16)
        gm_ref[rs, :] = jax.nn.sigmoid(zg[:, d:]).astype(BF16)

    _run_skewed((norm, qkv_matmul, attn_outputs, uvo_matmul, mlstm_outputs, gate_matmul, gate_outputs),
                x_ref.shape[0] // tm, tm)


def _inproj(x2, ln_g, ln_b, wqkv, wuvo, wift, wg, cos, sin, batch, seq):
    n, d = x2.shape
    chains = INPROJ_CHAINS
    tm = chains * ROW_TILE
    assert seq % tm == 0
    nsb = seq // tm
    hd = ATTN_HEAD_DIM
    row = lambda w: pl.BlockSpec((tm, w), lambda i: (i, 0))
    col = lambda h: pl.BlockSpec((h, tm), lambda i: (0, i))
    head = pl.BlockSpec((1, ATTN_HEADS, tm, hd), lambda i: (i // nsb, 0, i % nsb, 0))
    head_t = pl.BlockSpec((1, ATTN_HEADS, hd, tm), lambda i: (i // nsb, 0, 0, i % nsb))
    tab = pl.BlockSpec((tm, ATTN_WIDTH), lambda i: (i % nsb, 0))
    head_shape = jax.ShapeDtypeStruct((batch, ATTN_HEADS, seq, hd), BF16)
    head_t_shape = jax.ShapeDtypeStruct((batch, ATTN_HEADS, hd, seq), BF16)
    out_shape = (
        head_t_shape, head_shape, head_t_shape,
        jax.ShapeDtypeStruct((n // ROW_TILE, 1, ATTN_WIDTH), F32),
        jax.ShapeDtypeStruct((n, MLSTM_WIDTH), F32),
        jax.ShapeDtypeStruct((MLSTM_WIDTH, n), BF16),
        jax.ShapeDtypeStruct((MLSTM_WIDTH, n), F32),
        jax.ShapeDtypeStruct((SUBLANES, n), F32),
        jax.ShapeDtypeStruct((n, d), BF16),
        jax.ShapeDtypeStruct((n, d), BF16),
        jax.ShapeDtypeStruct((n, d), F32),
    )
    out_specs = (
        head_t, head, head_t,
        pl.BlockSpec((chains, 1, ATTN_WIDTH), lambda i: (i, 0, 0)),
        row(MLSTM_WIDTH), col(MLSTM_WIDTH), col(MLSTM_WIDTH),
        col(SUBLANES),
        row(d), row(d), row(d),
    )
    in_specs = [row(d), _full(ln_g.shape), _full(ln_b.shape), _full(wqkv.shape), _full(wuvo.shape),
                _full(wift.shape), _full(wg.shape), tab, tab]
    return pl.pallas_call(
        _inproj_kernel, grid=(n // tm,), in_specs=in_specs, out_specs=out_specs, out_shape=out_shape,
        compiler_params=_params("parallel"), name="inproj",
    )(x2, ln_g, ln_b, wqkv, wuvo, wift, wg, cos, sin)


def _moba_kernel(qt_ref, k_ref, vt_ref, km_ref, o_ref, bias_ref, m_ref, l_ref, acc_ref, s_ref):
    i = pl.program_id(1)
    blk = MOBA_BLOCK
    hd = ATTN_HEAD_DIM
    heads = ATTN_HEADS
    nb = k_ref.shape[2] // blk
    blk_id = lax.broadcasted_iota(jnp.int32, (nb, blk), 0)
    key_pos = lax.broadcasted_iota(jnp.int32, (blk, blk), 0)
    qry_pos = lax.broadcasted_iota(jnp.int32, (blk, blk), 1)
    causal = key_pos <= qry_pos

    for h in range(heads):
        qt = qt_ref[0, h]
        km = km_ref[0, h]
        km_hi = km.astype(BF16)
        km_lo = (km - km_hi.astype(F32)).astype(BF16)
        gate = _dot(km_hi, qt) + _dot(km_lo, qt)
        gate = jnp.where(blk_id < i, gate, NEG_INF)
        for j in range(nb - 1):
            row = gate[j:j + 1, :]
            beats = (gate > row) | ((gate == row) & (blk_id < j))
            cnt = jnp.sum(jnp.where(beats, 1.0, 0.0), axis=0, keepdims=True)
            sel = (cnt < float(MOBA_TOPK)) & (row > NEG_INF)
            bias_ref[j * heads + h] = jnp.where(sel, 0.0, NEG_INF)
    for h in range(heads):
        bias_ref[i * heads + h] = jnp.zeros((1, blk), F32)

    def scores(h, j, own_block):
        qt = qt_ref[0, h]
        half = blk // 2
        m_tile = None
        for c in range(2):
            rows = slice(c * half, (c + 1) * half)
            s = _dot(k_ref[0, h, pl.ds(pl.multiple_of(j * blk + c * half, half), half), :], qt)
            if own_block:
                s = jnp.where(causal[rows], s, NEG_INF)
            s_ref[j * heads + h, rows, :] = s
            m_c = jnp.max(s, axis=0, keepdims=True)
            m_tile = m_c if m_tile is None else jnp.maximum(m_tile, m_c)
        return m_tile

    for h in range(heads):
        m_ref[h] = scores(h, i, True)

    def past_scores(j, _):
        for h in range(heads):
            m_ref[h] = jnp.maximum(m_ref[h], scores(h, j, False) + bias_ref[j * heads + h])
        return 0

    lax.fori_loop(0, i, past_scores, 0)

    l_ref[...] = jnp.zeros_like(l_ref)
    acc_ref[...] = jnp.zeros_like(acc_ref)

    def accumulate(j, _):
        off = pl.multiple_of(j * blk, blk)
        for h in range(heads):
            p = jnp.exp2(s_ref[j * heads + h] - (m_ref[h] - bias_ref[j * heads + h]))
            l_ref[h] += jnp.sum(p, axis=0, keepdims=True)
            acc_ref[h] += _dot(vt_ref[0, h, :, pl.ds(off, blk)], p.astype(BF16))
        return 0

    lax.fori_loop(0, i + 1, accumulate, 0)
    yt = acc_ref[...] / l_ref[...]
    o_ref[0] = yt.reshape(heads * hd, blk).T.astype(BF16)


def _moba(qt, k, vt, km):
    batch, heads, seq, hd = k.shape
    blk = MOBA_BLOCK
    nb = seq // blk
    return pl.pallas_call(
        _moba_kernel, grid=(batch, nb),
        in_specs=[
            pl.BlockSpec((1, heads, hd, blk), lambda b, i: (b, 0, 0, i)),
            pl.BlockSpec((1, heads, seq, hd), lambda b, i: (b, 0, 0, 0)),
            pl.BlockSpec((1, heads, hd, seq), lambda b, i: (b, 0, 0, 0)),
            pl.BlockSpec((1, heads, nb, hd), lambda b, i: (b, 0, 0, 0)),
        ],
        out_specs=pl.BlockSpec((1, blk, heads * hd), lambda b, i: (b, i, 0)),
        out_shape=jax.ShapeDtypeStruct((batch, seq, heads * hd), BF16),
        scratch_shapes=[pltpu.VMEM((nb * heads, 1, blk), F32), pltpu.VMEM((heads, 1, blk), F32),
                        pltpu.VMEM((heads, 1, blk), F32), pltpu.VMEM((heads, hd, blk), F32),
                        pltpu.VMEM((nb * heads, blk, blk), F32)],
        compiler_params=_params("parallel", "arbitrary"), name="moba",
    )(qt, k, vt, km)


def _mlstm_kernel(u_ref, vmt_ref, ot_ref, ift_ref, cw_ref, cb_ref, wqt_ref, wk_ref, bcol_ref,
                  gn_ref, skip_ref, y_ref, ext_ref, c_ref, n_ref, m_ref, yt_ref):
    tm = u_ref.shape[0]
    hd = MLSTM_HEAD_DIM
    halo = SUBLANES

    @pl.when(pl.program_id(1) == 0)
    def _():
        ext_ref[0:halo, :] = jnp.zeros((halo, MLSTM_WIDTH), F32)
        c_ref[...] = jnp.zeros_like(c_ref)
        n_ref[...] = jnp.zeros_like(n_ref)
        m_ref[...] = jnp.zeros_like(m_ref)

    u = u_ref[...]
    ext_ref[halo:halo + tm, :] = u
    acc = jnp.broadcast_to(cb_ref[...], u.shape)
    for j in range(MLSTM_CONV):
        acc = acc + cw_ref[j:j + 1, :] * ext_ref[halo - (MLSTM_CONV - 1) + j:halo - (MLSTM_CONV - 1) + j + tm, :]
    ext_ref[0:halo, :] = u[tm - halo:, :]
    uc = acc * jax.nn.sigmoid(acc)

    gr = ift_ref[...] + bcol_ref[...]
    rows = lax.broadcasted_iota(jnp.int32, (tm, tm), 0)
    cols = lax.broadcasted_iota(jnp.int32, (tm, tm), 1)
    causal_t = rows <= cols
    triu = jnp.where(causal_t, 1.0, 0.0).astype(BF16)
    r1, r2, r3 = _split3(_log_sigmoid(gr))
    bcum_r = _dot(r1, triu) + _dot(r2, triu) + _dot(r3, triu)
    key_rows = gr[:MLSTM_HEADS, :] - bcum_r[MLSTM_HEADS:, :]
    key_cols = jnp.concatenate([key_rows, jnp.zeros((LANES - MLSTM_HEADS, tm), F32)], axis=0).T

    uct = uc.T

    def decay_weights(st, h, hs):
        fl = MLSTM_HEADS + h
        b_row = bcum_r[fl:fl + 1, :]
        st["key_row"] = key_rows[h:h + 1, :]
        st["key_col"] = key_cols[:, h:h + 1]
        m_prev = m_ref[h][:, 0:1]
        dlog = jnp.where(causal_t, st["key_col"] + b_row, NEG_INF)
        inter = b_row + m_prev
        m_t = jnp.maximum(inter, jnp.max(dlog, axis=0, keepdims=True))
        st["w_intra"] = jnp.exp(dlog - m_t)
        st["w_inter"] = jnp.exp(inter - m_t)
        st["m_t"], st["m_prev"], st["b_end"] = m_t, m_prev, b_row[:, tm - 1:tm]

    def project(st, h, hs):
        st["qtb"] = _dot(wqt_ref[h], uct[hs, :].astype(BF16)).astype(BF16)
        st["k"] = _dot(uc[:, hs].astype(BF16), wk_ref[h]) * (hd ** -0.5)

    def scores(st, h, hs):
        st["s"] = _dot(st["k"].astype(BF16), st["qtb"]) * st.pop("w_intra")

    def readout(st, h, hs):
        qtb, s, w_inter, m_t = st.pop("qtb"), st.pop("s"), st.pop("w_inter"), st.pop("m_t")
        n_prev = n_ref[h]
        n_hi = n_prev.astype(BF16)
        n_lo = (n_prev - n_hi.astype(F32)).astype(BF16)
        qn = (_dot(n_hi, qtb) + _dot(n_lo, qtb))[0:1, :]
        num = w_inter * _dot(c_ref[h].astype(BF16), qtb) + _dot(vmt_ref[hs, :], s.astype(BF16))
        den = w_inter * qn + jnp.sum(s, axis=0, keepdims=True)
        st["hh"] = num / jnp.maximum(jnp.abs(den), jnp.exp(-m_t))

    def update_state(st, h, hs):
        b_end, m_prev = st.pop("b_end"), st.pop("m_prev")
        m_new = jnp.maximum(b_end + m_prev, jnp.max(b_end + st.pop("key_row"), axis=1, keepdims=True))
        decay = jnp.exp(b_end + m_prev - m_new)
        kw = st.pop("k") * jnp.exp(b_end + st.pop("key_col") - m_new)
        n_prev = n_ref[h]
        c_ref[h] = decay * c_ref[h] + _dot(vmt_ref[hs, :], kw.astype(BF16))
        n_ref[h] = decay * n_prev + jnp.broadcast_to(jnp.sum(kw, axis=0, keepdims=True), n_prev.shape)
        m_ref[h] = jnp.broadcast_to(m_new, (1, LANES))

    def gate_and_norm(st, h, hs):
        hh = jax.nn.sigmoid(ot_ref[hs, :]) * st.pop("hh")
        mu = jnp.mean(hh, axis=0, keepdims=True)
        hc = hh - mu
        var = jnp.mean(hc * hc, axis=0, keepdims=True)
        yt_ref[hs, :] = hc * lax.rsqrt(var + GN_EPS) * gn_ref[hs, :] + skip_ref[hs, :] * uct[hs, :]

    _run_skewed((decay_weights, project, scores, readout, update_state, gate_and_norm), MLSTM_HEADS, hd)
    y_ref[...] = yt_ref[...].T.astype(BF16)


def _mlstm(u, vmt, ot, ift, conv_w, conv_b, wqt, wk, bcol, gn_g, skip, batch, seq):
    n = u.shape[0]
    tm = ROW_TILE
    nc = seq // tm
    row = lambda w: pl.BlockSpec((tm, w), lambda b, c: (b * nc + c, 0))
    col = lambda h: pl.BlockSpec((h, tm), lambda b, c: (0, b * nc + c))
    in_specs = [row(MLSTM_WIDTH), col(MLSTM_WIDTH), col(MLSTM_WIDTH), col(SUBLANES),
                _full(conv_w.shape), _full(conv_b.shape), _full(wqt.shape), _full(wk.shape),
                _full(bcol.shape), _full(gn_g.shape), _full(skip.shape)]
    return pl.pallas_call(
        _mlstm_kernel, grid=(batch, nc), in_specs=in_specs, out_specs=row(MLSTM_WIDTH),
        out_shape=jax.ShapeDtypeStruct((n, MLSTM_WIDTH), BF16),
        scratch_shapes=[pltpu.VMEM((SUBLANES + tm, MLSTM_WIDTH), F32),
                        pltpu.VMEM((MLSTM_HEADS, MLSTM_HEAD_DIM, MLSTM_HEAD_DIM), F32),
                        pltpu.VMEM((MLSTM_HEADS, SUBLANES, MLSTM_HEAD_DIM), F32),
                        pltpu.VMEM((MLSTM_HEADS, 1, LANES), F32),
                        pltpu.VMEM((MLSTM_WIDTH, tm), F32)],
        compiler_params=_params("parallel", "arbitrary"), name="mlstm",
    )(u, vmt, ot, ift, conv_w, conv_b, wqt, wk, bcol, gn_g, skip)


def _mix_kernel(xn_ref, ya_ref, ym_ref, ga_ref, gm_ref, wau_ref, wmu_ref, wout_ref,
                g1_ref, b1_ref, wrc_ref, br_ref,
                x1_ref, ri_ref, rw_ref, cnt_out_ref, cnt_ref):
    @pl.when(pl.program_id(0) == 0)
    def _():
        cnt_ref[...] = jnp.zeros_like(cnt_ref)

    tm = ROW_TILE
    sub = lax.broadcasted_iota(jnp.int32, (LANES, tm), 0).astype(F32)
    big = float(4 * LANES)

    def up_and_mix(st, c, rs):
        a_up = _dot(ya_ref[rs, :], wau_ref[...])
        m_up = _dot(ym_ref[rs, :], wmu_ref[...])
        mix = ga_ref[rs, :].astype(F32) * a_up + gm_ref[rs, :].astype(F32) * m_up
        st["mix"] = mix.astype(BF16)

    def out_and_norm(st, c, rs):
        x1 = _layer_norm(DEEPNORM_ALPHA * xn_ref[rs, :] + _dot(st.pop("mix"), wout_ref[...]), g1_ref[...], b1_ref[...])
        x1_ref[rs, :] = x1
        st["x1"] = x1

    def router_logits(st, c, rs):
        x1 = st.pop("x1")
        x_hi = x1.astype(BF16)
        x_lo = (x1 - x_hi.astype(F32)).astype(BF16)
        both = _dot_nt(wrc_ref[...], x_hi)
        st["logits"] = both[:LANES] + both[LANES:] + _dot_nt(wrc_ref[:LANES, :], x_lo) + br_ref[...]

    def route(st, c, rs):
        logits = st.pop("logits")
        is_g = (sub >= float(MOE_EXPERTS)) & (sub < float(MOE_EXPERTS + MOE_GROUPS))
        gl = jnp.where(is_g, logits, NEG_INF)
        ge = jnp.exp(gl - jnp.max(gl, axis=0, keepdims=True))
        gp = ge / jnp.sum(ge, axis=0, keepdims=True)
        g_w = jnp.max(gp, axis=0, keepdims=True)
        g_idx = jnp.min(jnp.where((gp == g_w) & is_g, sub - float(MOE_EXPERTS), big), axis=0, keepdims=True)
        lo = g_idx * float(MOE_EXPERTS_PER_GROUP)
        in_grp = (sub >= lo) & (sub < lo + float(MOE_EXPERTS_PER_GROUP))
        el = jnp.where(in_grp, logits, NEG_INF)
        v1 = jnp.max(el, axis=0, keepdims=True)
        i1 = jnp.min(jnp.where((el == v1) & in_grp, sub, big), axis=0, keepdims=True)
        el2 = jnp.where(sub == i1, NEG_INF, el)
        v2 = jnp.max(el2, axis=0, keepdims=True)
        i2 = jnp.min(jnp.where((el2 == v2) & in_grp & (sub != i1), sub, big), axis=0, keepdims=True)
        e2 = jnp.exp(v2 - v1)
        w0 = g_w / (1.0 + e2)
        w1 = g_w * e2 / (1.0 + e2)
        rw_ref[rs, :] = jnp.where(sub == 0.0, w0, jnp.where(sub == 1.0, w1, 0.0)).T
        st["i1"], st["i2"] = i1, i2

    def rank(st, c, rs):
        i1, i2 = st.pop("i1"), st.pop("i2")
        is1 = sub == i1
        is2 = sub == i2
        onehot = jnp.where(is1 | is2, 1.0, 0.0)
        rows = lax.broadcasted_iota(jnp.int32, (tm, tm), 0)
        cols = lax.broadcasted_iota(jnp.int32, (tm, tm), 1)
        earlier = jnp.where(rows < cols, 1.0, 0.0).astype(BF16)
        before = _dot(onehot.astype(BF16), earlier) + cnt_ref[...]
        r0 = jnp.sum(jnp.where(is1, before, 0.0), axis=0, keepdims=True)
        r1 = jnp.sum(jnp.where(is2, before, 0.0), axis=0, keepdims=True)
        total = cnt_ref[...] + jnp.sum(onehot, axis=1, keepdims=True)
        cnt_ref[...] = total
        cnt_out_ref[...] = total
        ri_t = jnp.where(sub == 0.0, i1, jnp.where(sub == 1.0, i2, jnp.where(sub == 2.0, r0, jnp.where(sub == 3.0, r1, 0.0))))
        ri_ref[:, rs] = ri_t[:SUBLANES, :].astype(jnp.int32)

    _run_skewed((up_and_mix, out_and_norm, router_logits, route, rank), xn_ref.shape[0] // tm, tm)


def _mix(xn, ya, ym, ga, gm, wau, wmu, wout, g1, b1, wrc, br):
    n, d = xn.shape
    tm = MIX_CHAINS * ROW_TILE
    row = lambda w: pl.BlockSpec((tm, w), lambda i: (i, 0))
    in_specs = [row(d), row(ATTN_WIDTH), row(MLSTM_WIDTH), row(d), row(d),
                _full(wau.shape), _full(wmu.shape), _full(wout.shape), _full(g1.shape), _full(b1.shape),
                _full(wrc.shape), _full(br.shape)]
    out_shape = (jax.ShapeDtypeStruct((n, d), F32), jax.ShapeDtypeStruct((SUBLANES, n), jnp.int32),
                 jax.ShapeDtypeStruct((n, LANES), F32), jax.ShapeDtypeStruct((LANES, 1), F32))
    out_specs = (row(d), pl.BlockSpec((SUBLANES, tm), lambda i: (0, i)), row(LANES), _full((LANES, 1)))
    return pl.pallas_call(
        _mix_kernel, grid=(n // tm,), in_specs=in_specs, out_specs=out_specs, out_shape=out_shape,
        scratch_shapes=[pltpu.VMEM((LANES, 1), F32)],
        compiler_params=_params("arbitrary"), name="mix",
    )(xn, ya, ym, ga, gm, wau, wmu, wout, g1, b1, wrc, br)


def _token_rows(d):
    return d // LANES


def _to_token_tiles(dst_ref, x):
    rows, d = x.shape
    nch = _token_rows(d)
    for c in range(nch):
        dst_ref[pl.ds(c, rows, stride=nch), :] = x[:, c * LANES:(c + 1) * LANES]


def _from_token_tiles(src_ref, rows, d):
    nch = _token_rows(d)
    return jnp.concatenate([src_ref[pl.ds(c, rows, stride=nch), :] for c in range(nch)], axis=1)


def _token_copy(src, src_tok, dst, dst_tok, nch, sem):
    s0 = pl.multiple_of(src_tok * nch, nch)
    d0 = pl.multiple_of(dst_tok * nch, nch)
    return pltpu.make_async_copy(src.at[pl.ds(s0, nch), :], dst.at[pl.ds(d0, nch), :], sem)


def _slots_kernel(ri_ref, ps_ref, o_ref):
    ri = ri_ref[...].astype(F32)
    ps = ps_ref[...]
    expert = lax.broadcasted_iota(jnp.int32, (ps.shape[0], ri.shape[1]), 0).astype(F32)
    row_id = lax.broadcasted_iota(jnp.int32, ri.shape, 0)
    out = jnp.zeros(ri.shape, F32)
    for k in range(2):
        start = jnp.sum(jnp.where(expert == ri[k:k + 1, :], jnp.broadcast_to(ps, expert.shape), 0.0),
                        axis=0, keepdims=True)
        out = jnp.where(row_id == k, start + ri[2 + k:3 + k, :], out)
    o_ref[...] = out.astype(jnp.int32)


def _slots(ri, pad_start_col):
    n = ri.shape[1]
    tm = SLOT_TILE
    blk = pl.BlockSpec((SUBLANES, tm), lambda i: (0, i))
    return pl.pallas_call(
        _slots_kernel, grid=(n // tm,), in_specs=[blk, _full(pad_start_col.shape)], out_specs=blk,
        out_shape=jax.ShapeDtypeStruct((SUBLANES, n), jnp.int32),
        compiler_params=_params("parallel"), name="slots",
    )(ri, pad_start_col)


def _slot(dest_ref, r, k):
    return dest_ref[k * MOE_TILE + r]


def _dispatch_kernel(dest_ref, last_ref, x_ref, xs_ref, scr_ref, zero_ref, sem, zsem):
    tm, d = x_ref.shape
    nch = _token_rows(d)
    tb = zero_ref.shape[0] // nch

    @pl.when(pl.program_id(0) == 0)
    def _():
        zero_ref[...] = jnp.zeros_like(zero_ref)

        def desc(tok):
            off = pl.multiple_of(jnp.maximum(tok, 0) * nch, nch)
            return pltpu.make_async_copy(zero_ref, xs_ref.at[pl.ds(off, tb * nch), :], zsem)

        def zstart(e, _):
            @pl.when(last_ref[e] >= 0)
            def _():
                desc(last_ref[e]).start()
            return 0

        def zwait(e, _):
            @pl.when(last_ref[e] >= 0)
            def _():
                desc(last_ref[e]).wait()
            return 0

        lax.fori_loop(0, MOE_EXPERTS, zstart, 0)
        nused = last_ref[MOE_EXPERTS]
        nblk = xs_ref.shape[0] // (tb * nch)
        lax.fori_loop(nused, nblk, lambda b, _: (desc(b * tb).start(), 0)[1], 0)
        lax.fori_loop(0, MOE_EXPERTS, zwait, 0)
        lax.fori_loop(nused, nblk, lambda b, _: (desc(b * tb).wait(), 0)[1], 0)

    step = pl.program_id(0)
    slot = step % 2
    scr = scr_ref.at[slot]
    _to_token_tiles(scr, x_ref[...])

    def start(r, _):
        for k in range(2):
            _token_copy(scr, r, xs_ref, _slot(dest_ref, r, k), nch, sem.at[slot]).start(priority=k)
        return 0

    def drain(which):
        def wait(r, _):
            for k in range(2):
                _token_copy(scr_ref.at[which], 0, xs_ref, 0, nch, sem.at[which]).wait()
            return 0
        lax.fori_loop(0, tm, wait, 0, unroll=8)

    lax.fori_loop(0, tm, start, 0, unroll=8)

    @pl.when(step > 0)
    def _():
        drain(1 - slot)

    @pl.when(step == pl.num_programs(0) - 1)
    def _():
        drain(slot)


def _dispatch(dest, last_blk, x1, n_rows):
    n, d = x1.shape
    tm = MOE_TILE
    nch = _token_rows(d)
    return pl.pallas_call(
        _dispatch_kernel, grid=(n // tm,),
        in_specs=[pl.BlockSpec((2 * tm,), lambda i: (i,), memory_space=pltpu.SMEM),
                  pl.BlockSpec(memory_space=pltpu.SMEM),
                  pl.BlockSpec((tm, d), lambda i: (i, 0))],
        out_specs=pl.BlockSpec(memory_space=pl.ANY),
        out_shape=jax.ShapeDtypeStruct((n_rows * nch, LANES), F32),
        scratch_shapes=[pltpu.VMEM((2, tm * nch, LANES), F32), pltpu.VMEM((EXPERT_TILE * nch, LANES), F32),
                        pltpu.SemaphoreType.DMA((2,)), pltpu.SemaphoreType.DMA(())],
        compiler_params=_params("arbitrary"), name="dispatch",
    )(dest, last_blk, x1)


def _expert_kernel(first_ref, count_ref, widx_ref, nused_ref, wg_ref, wu_ref, wd_ref, xs_ref, ys_ref,
                   wgb_ref, wub_ref, wdb_ref, xbuf_ref, ybuf_ref, in_sem, out_sem):
    del widx_ref
    e = pl.program_id(0)
    nused = nused_ref[0]
    d = wg_ref.shape[1]
    nch = _token_rows(d)
    rows = xbuf_ref.shape[1]
    tb = rows // nch
    nblk = xs_ref.shape[0] // rows

    def blk(ref, b):
        return ref.at[pl.ds(pl.multiple_of(b * rows, rows), rows), :]

    def in_copy(b, slot):
        return pltpu.make_async_copy(blk(xs_ref, b), xbuf_ref.at[slot], in_sem.at[slot])

    def out_copy(b, slot):
        return pltpu.make_async_copy(ybuf_ref.at[slot], blk(ys_ref, b), out_sem.at[slot])

    n_in = xbuf_ref.shape[0]

    @pl.when(e == 0)
    def _():
        for b0 in range(n_in - 1):
            @pl.when(b0 < nused)
            def _():
                in_copy(b0, b0).start()

    @pl.when(count_ref[e] > 0)
    def _():
        wgb_ref[...] = wg_ref[0].astype(BF16)
        wub_ref[...] = wu_ref[0].astype(BF16)
        wdb_ref[...] = wd_ref[0].astype(BF16)

    def body(b, _):
        slot = b % n_in
        oslot = b % 2
        in_copy(b, slot).wait()

        @pl.when(b + n_in - 1 < nused)
        def _():
            in_copy(b + n_in - 1, (b + n_in - 1) % n_in).start()

        @pl.when(b >= 2)
        def _():
            out_copy(b - 2, oslot).wait()

        xb = _from_token_tiles(xbuf_ref.at[slot], tb, d).astype(BF16)
        g = _dot(xb, wgb_ref[...])
        u = _dot(xb, wub_ref[...])
        hmid = g * jax.nn.sigmoid(g) * u
        _to_token_tiles(ybuf_ref.at[oslot], _dot(hmid.astype(BF16), wdb_ref[...]))
        out_copy(b, oslot).start()
        return 0

    lax.fori_loop(first_ref[e], first_ref[e] + count_ref[e], body, 0)

    @pl.when(e == pl.num_programs(0) - 1)
    def _():
        for back in (2, 1):
            @pl.when(nused >= back)
            def _():
                out_copy(nused - back, (nused - back) % 2).wait()

        ybuf_ref[0] = jnp.zeros(ybuf_ref.shape[1:], F32)
        lax.fori_loop(nused, nblk, lambda b, _: (out_copy(b, 0).start(), 0)[1], 0)
        lax.fori_loop(nused, nblk, lambda b, _: (out_copy(b, 0).wait(), 0)[1], 0)


def _experts(first_blk, blk_count, w_idx, nused, xs, w_gate, w_up, w_down):
    n_exp, d, dff = w_gate.shape
    nch = _token_rows(d)
    rows = EXPERT_TILE * nch
    w_spec = lambda shape: pl.BlockSpec(shape, lambda e, fb, bc, wi, nu: (wi[e], 0, 0))
    any_spec = pl.BlockSpec(memory_space=pl.ANY)
    grid_spec = pltpu.PrefetchScalarGridSpec(
        num_scalar_prefetch=4, grid=(n_exp,),
        in_specs=[w_spec((1, d, dff)), w_spec((1, d, dff)), w_spec((1, dff, d)), any_spec],
        out_specs=any_spec,
        scratch_shapes=[pltpu.VMEM((d, dff), BF16), pltpu.VMEM((d, dff), BF16), pltpu.VMEM((dff, d), BF16),
                        pltpu.VMEM((EXPERT_IN_SLOTS, rows, LANES), F32), pltpu.VMEM((2, rows, LANES), F32),
                        pltpu.SemaphoreType.DMA((EXPERT_IN_SLOTS,)), pltpu.SemaphoreType.DMA((2,))],
    )
    return pl.pallas_call(
        _expert_kernel, grid_spec=grid_spec, out_shape=jax.ShapeDtypeStruct(xs.shape, F32),
        compiler_params=_params("arbitrary"), name="experts",
    )(first_blk, blk_count, w_idx, nused, w_gate, w_up, w_down, xs)


def _combine_kernel(dest_ref, dest_next_ref, x1_ref, rw_ref, g_ref, b_ref, ys_ref, o_ref, buf_ref, sem):
    tm, d = x1_ref.shape
    nch = _token_rows(d)
    step = pl.program_id(0)
    slot = step % 2

    def gather(idx_ref, which):
        def start(r, _):
            for k in range(2):
                _token_copy(ys_ref, _slot(idx_ref, r, k), buf_ref.at[which, k], r, nch,
                            sem.at[which]).start(priority=k)
            return 0
        lax.fori_loop(0, tm, start, 0, unroll=8)

    @pl.when(step == 0)
    def _():
        gather(dest_ref, 0)

    @pl.when(step + 1 < pl.num_programs(0))
    def _():
        gather(dest_next_ref, 1 - slot)

    def wait(r, _):
        for k in range(2):
            _token_copy(ys_ref, 0, buf_ref.at[slot, k], 0, nch, sem.at[slot]).wait()
        return 0

    lax.fori_loop(0, tm, wait, 0, unroll=8)
    rw = rw_ref[...]
    y0 = _from_token_tiles(buf_ref.at[slot, 0], tm, d)
    y1 = _from_token_tiles(buf_ref.at[slot, 1], tm, d)
    ffn = rw[:, 0:1] * y0 + rw[:, 1:2] * y1
    o_ref[...] = _layer_norm(DEEPNORM_ALPHA * x1_ref[...] + ffn, g_ref[...], b_ref[...])


def _combine(dest, x1, rw, ln_g, ln_b, ys):
    n, d = x1.shape
    tm = MOE_TILE
    nch = _token_rows(d)
    last = n // tm - 1
    row = lambda w: pl.BlockSpec((tm, w), lambda i: (i, 0))
    return pl.pallas_call(
        _combine_kernel, grid=(n // tm,),
        in_specs=[pl.BlockSpec((2 * tm,), lambda i: (i,), memory_space=pltpu.SMEM),
                  pl.BlockSpec((2 * tm,), lambda i: (jnp.minimum(i + 1, last),), memory_space=pltpu.SMEM),
                  row(d), row(LANES), _full(ln_g.shape), _full(ln_b.shape),
                  pl.BlockSpec(memory_space=pl.ANY)],
        out_specs=row(d),
        out_shape=jax.ShapeDtypeStruct((n, d), F32),
        scratch_shapes=[pltpu.VMEM((2, 2, tm * nch, LANES), F32), pltpu.SemaphoreType.DMA((2,))],
        compiler_params=_params("arbitrary"), name="combine",
    )(dest, dest, x1, rw, ln_g, ln_b, ys)


def _rope_tables(seq):
    half = ATTN_HEAD_DIM // 2
    inv_freq = ROPE_THETA ** (-jnp.arange(half, dtype=F32) / half)
    ang = jnp.arange(seq, dtype=F32)[:, None] * inv_freq[None, :]
    cos = jnp.cos(ang)
    sin = jnp.sin(ang)
    cos_h = jnp.concatenate([cos, cos], axis=1)
    sin_h = jnp.concatenate([-sin, sin], axis=1)
    return jnp.tile(cos_h, (1, ATTN_HEADS)), jnp.tile(sin_h, (1, ATTN_HEADS))


def _pad_lanes(a, width=LANES):
    return jnp.pad(a, ((0, 0), (0, width - a.shape[1])))


def kernel(x, ln0_g, ln0_b, w_in, conv_w, conv_b, w_mq, w_mk, b_i, b_f, gn_g, skip, w_attn_up, w_mlstm_up, w_out,
           ln1_g, ln1_b, w_router_group, b_router_group, w_router_expert, b_router_expert, w_gate, w_up, w_down,
           ln2_g, ln2_b):
    batch, seq, d = x.shape
    n = batch * seq
    assert seq % ROW_TILE == 0 and ROW_TILE == MOBA_BLOCK and w_in.shape[0] == DEPTH
    x2 = x.reshape(n, d)
    vec = lambda a: a.reshape(1, -1).astype(F32)

    w = w_in[0]
    c_if = 3 * ATTN_WIDTH + 3 * MLSTM_WIDTH
    c_g = c_if + 2 * MLSTM_HEADS
    wqkv = w[:, :3 * ATTN_WIDTH].astype(BF16)
    wuvo = w[:, 3 * ATTN_WIDTH:c_if].astype(BF16)
    wift = w[:, c_if:c_g].T.astype(BF16)
    wg = w[:, c_g:].astype(BF16)
    cos, sin = _rope_tables(seq)

    q, k, v, kmean, u, vm, o, ift, ga, gm, xn = _inproj(
        x2, vec(ln0_g), vec(ln0_b), wqkv, wuvo, wift, wg, cos, sin, batch, seq)

    nb = seq // MOBA_BLOCK
    km = kmean.reshape(batch, nb, ATTN_HEADS, ATTN_HEAD_DIM).transpose(0, 2, 1, 3)
    ya = _moba(q, k, v, km).reshape(n, ATTN_WIDTH)

    b_if = jnp.concatenate([b_i[0], b_f[0]]).astype(F32)
    ym = _mlstm(u, vm, o, ift, conv_w[0], vec(conv_b[0]), w_mq[0].transpose(0, 2, 1).astype(BF16),
                w_mk[0].astype(BF16), b_if[:, None],
                gn_g[0].astype(F32)[:, None], skip[0].astype(F32)[:, None], batch, seq)

    w_r = _pad_lanes(jnp.concatenate([w_router_expert[0], w_router_group[0]], axis=1))
    w_r_hi = w_r.astype(BF16)
    w_r_lo = (w_r - w_r_hi.astype(F32)).astype(BF16)
    w_rc = jnp.concatenate([w_r_hi.T, w_r_lo.T], axis=0)
    b_r = _pad_lanes(jnp.concatenate([b_router_expert[0], b_router_group[0]])[None, :]).T
    x1, ri, rw, counts = _mix(
        xn, ya, ym, ga, gm, w_attn_up[0].astype(BF16), w_mlstm_up[0].astype(BF16),
        w_out[0].astype(BF16), vec(ln1_g[0]), vec(ln1_b[0]), w_rc, b_r)

    tb = EXPERT_TILE
    nblk = (2 * n) // tb + MOE_EXPERTS
    cnt = counts[:MOE_EXPERTS, 0].astype(jnp.int32)
    nblk_e = (cnt + tb - 1) // tb
    blk_end = jnp.cumsum(nblk_e)
    pad_start = (blk_end - nblk_e) * tb
    nused = blk_end[-1:]
    ids = jnp.arange(MOE_EXPERTS, dtype=jnp.int32)
    prev_used = jnp.max(jnp.where((ids[None, :] <= ids[:, None]) & (nblk_e[None, :] > 0), ids[None, :], -1), axis=1)
    first_used = jnp.min(jnp.where(nblk_e > 0, ids, MOE_EXPERTS - 1))
    w_idx = jnp.where(prev_used >= 0, prev_used, first_used).astype(jnp.int32)
    last_blk = jnp.where(nblk_e > 0, (blk_end - 1) * tb, -1)
    last_blk = jnp.concatenate([last_blk, nused]).astype(jnp.int32)
    dest = _slots(ri, pad_start.astype(F32)[:, None])
    dest = dest[:2].reshape(2, n // MOE_TILE, MOE_TILE).transpose(1, 0, 2).reshape(2 * n)

    xs = _dispatch(dest, last_blk, x1, nblk * tb)
    ys = _experts((blk_end - nblk_e).astype(jnp.int32), nblk_e.astype(jnp.int32), w_idx, nused.astype(jnp.int32),
                  xs, w_gate[0], w_up[0], w_down[0])
    out = _combine(dest, x1, rw, vec(ln2_g[0]), vec(ln2_b[0]), ys)
    return out.reshape(batch, seq, d)
```

```python
import functools
import math

import jax
import jax.numpy as jnp
from jax import lax
from jax.experimental import pallas as pl
from jax.experimental.pallas import tpu as pltpu

F32 = jnp.float32
BF16 = jnp.bfloat16

ATTN_HEADS = 8
ATTN_HEAD_DIM = 64
ATTN_WIDTH = ATTN_HEADS * ATTN_HEAD_DIM
MOBA_BLOCK = 256
MOBA_TOPK = 3
ROPE_THETA = 10000.0
MLSTM_HEADS = 4
MLSTM_HEAD_DIM = 128
MLSTM_WIDTH = MLSTM_HEADS * MLSTM_HEAD_DIM
MLSTM_CONV = 4
MOE_GROUPS = 8
MOE_EXPERTS_PER_GROUP = 8
MOE_EXPERTS = MOE_GROUPS * MOE_EXPERTS_PER_GROUP
MOE_D_FF = 512
LN_EPS = 1e-5
GN_EPS = 1e-6
DEPTH = 1
DEEPNORM_ALPHA = (2 * DEPTH) ** 0.25

LANES = 128
SUBLANES = 8
ROW_TILE = 256
EXPERT_TILE = 256
EXPERT_IN_SLOTS = 4
INPROJ_CHAINS = 2
SLOT_TILE = 2048
MIX_CHAINS = 4
VMEM_LIMIT = 48 * 1024 * 1024
LOG2_E = math.log2(math.e)

NEG_INF = float("-inf")


def _params(*sem):
    return pltpu.CompilerParams(dimension_semantics=sem, vmem_limit_bytes=VMEM_LIMIT)


def _dot(a, b):
    return jnp.dot(a, b, preferred_element_type=F32)


def _dot_nt(a, b):
    return lax.dot_general(a, b, (((1,), (1,)), ((), ())), preferred_element_type=F32)


def _dot_tn(a, b):
    return lax.dot_general(a, b, (((0,), (0,)), ((), ())), preferred_element_type=F32)


def _split3(x):
    x1 = x.astype(BF16)
    r1 = x - x1.astype(F32)
    x2 = r1.astype(BF16)
    r2 = r1 - x2.astype(F32)
    return x1, x2, r2.astype(BF16)


def _layer_norm(x, g, b):
    mu = jnp.mean(x, axis=-1, keepdims=True)
    xc = x - mu
    var = jnp.mean(xc * xc, axis=-1, keepdims=True)
    return xc * lax.rsqrt(var + LN_EPS) * g + b


def _log_sigmoid(x):
    return jnp.minimum(x, 0.0) - jnp.log1p(jnp.exp(-jnp.abs(x)))


def _full(shape):
    nd = len(shape)
    return pl.BlockSpec(shape, lambda *_: (0,) * nd)


def _run_skewed(phases, chains, rows):
    states = [dict() for _ in range(chains)]
    for t in range(chains + len(phases) - 1):
        for c in range(chains):
            if 0 <= t - c < len(phases):
                phases[t - c](states[c], c, slice(c * rows, (c + 1) * rows))


def _loop_pairs(count, body):
    def pair(jj, _):
        body(2 * jj)
        body(2 * jj + 1)
        return 0

    lax.fori_loop(0, count // 2, pair, 0)

    @pl.when(count % 2 == 1)
    def _():
        body(count - 1)


def _inproj_kernel(x_ref, g_ref, b_ref, wqkv_ref, wuvo_ref, wift_ref, wg_ref, cos_ref, sin_ref,
                   q_ref, k_ref, v_ref, km_ref, u_ref, vm_ref, o_ref, ift_ref, ga_ref, gm_ref, xn_ref):
    tm = ROW_TILE
    lane = lax.broadcasted_iota(jnp.int32, (tm, ATTN_WIDTH), 1)
    first_half = (lane % ATTN_HEAD_DIM) < (ATTN_HEAD_DIM // 2)

    def norm(st, c, rs):
        xn = _layer_norm(x_ref[rs, :], g_ref[...], b_ref[...])
        xn_ref[rs, :] = xn
        st["xb"] = xn.astype(BF16)

    def qkv_matmul(st, c, rs):
        st["zqkv"] = _dot(st["xb"], wqkv_ref[...])

    def attn_outputs(st, c, rs):
        zqkv = st.pop("zqkv")
        cos = cos_ref[rs, :]
        sin = sin_ref[rs, :]

        def rope(t):
            fwd = pltpu.roll(t, ATTN_WIDTH - ATTN_HEAD_DIM // 2, axis=1)
            bwd = pltpu.roll(t, ATTN_HEAD_DIM // 2, axis=1)
            return t * cos + jnp.where(first_half, fwd, bwd) * sin

        q = rope(zqkv[:, :ATTN_WIDTH]) * (ATTN_HEAD_DIM ** -0.5 * LOG2_E)
        k = rope(zqkv[:, ATTN_WIDTH:2 * ATTN_WIDTH])
        v = zqkv[:, 2 * ATTN_WIDTH:]
        km_ref[c] = jnp.mean(k, axis=0, keepdims=True)
        qt = q.T
        vt = v.T
        for h in range(ATTN_HEADS):
            sl = slice(h * ATTN_HEAD_DIM, (h + 1) * ATTN_HEAD_DIM)
            q_ref[0, h, :, rs] = qt[sl, :].astype(BF16)
            k_ref[0, h, rs, :] = k[:, sl].astype(BF16)
            v_ref[0, h, :, rs] = vt[sl, :].astype(BF16)

    def uvo_matmul(st, c, rs):
        st["zuvo"] = _dot(st["xb"], wuvo_ref[...])

    def mlstm_outputs(st, c, rs):
        zuvo = st.pop("zuvo")
        u_ref[rs, :] = zuvo[:, :MLSTM_WIDTH]
        vm_ref[:, rs] = zuvo[:, MLSTM_WIDTH:2 * MLSTM_WIDTH].T.astype(BF16)
        o_ref[:, rs] = zuvo[:, 2 * MLSTM_WIDTH:].T
        ift_ref[:, rs] = _dot_nt(wift_ref[...], st["xb"])

    def gate_matmul(st, c, rs):
        st["zg"] = _dot(st.pop("xb"), wg_ref[...])

    def gate_outputs(st, c, rs):
        zg = st.pop("zg")
        d = ga_ref.shape[1]
        ga_ref[rs, :] = jax.nn.sigmoid(zg[:, :d]).astype(BF16)
        gm_ref[rs, :] = jax.nn.sigmoid(zg[:, d:]).astype(BF16)

    _run_skewed((norm, qkv_matmul, attn_outputs, uvo_matmul, mlstm_outputs, gate_matmul, gate_outputs),
                x_ref.shape[0] // tm, tm)


def _inproj(x2, ln_g, ln_b, wqkv, wuvo, wift, wg, cos, sin, batch, seq):
    n, d = x2.shape
    chains = INPROJ_CHAINS
    tm = chains * ROW_TILE
    assert seq % tm == 0
    nsb = seq // tm
    hd = ATTN_HEAD_DIM
    row = lambda w: pl.BlockSpec((tm, w), lambda i: (i, 0))
    col = lambda h: pl.BlockSpec((h, tm), lambda i: (0, i))
    head = pl.BlockSpec((1, ATTN_HEADS, tm, hd), lambda i: (i // nsb, 0, i % nsb, 0))
    head_t = pl.BlockSpec((1, ATTN_HEADS, hd, tm), lambda i: (i // nsb, 0, 0, i % nsb))
    tab = pl.BlockSpec((tm, ATTN_WIDTH), lambda i: (i % nsb, 0))
    head_shape = jax.ShapeDtypeStruct((batch, ATTN_HEADS, seq, hd), BF16)
    head_t_shape = jax.ShapeDtypeStruct((batch, ATTN_HEADS, hd, seq), BF16)
    out_shape = (
        head_t_shape, head_shape, head_t_shape,
        jax.ShapeDtypeStruct((n // ROW_TILE, 1, ATTN_WIDTH), F32),
        jax.ShapeDtypeStruct((n, MLSTM_WIDTH), F32),
        jax.ShapeDtypeStruct((MLSTM_WIDTH, n), BF16),
        jax.ShapeDtypeStruct((MLSTM_WIDTH, n), F32),
        jax.ShapeDtypeStruct((SUBLANES, n), F32),
        jax.ShapeDtypeStruct((n, d), BF16),
        jax.ShapeDtypeStruct((n, d), BF16),
        jax.ShapeDtypeStruct((n, d), F32),
    )
    out_specs = (
        head_t, head, head_t,
        pl.BlockSpec((chains, 1, ATTN_WIDTH), lambda i: (i, 0, 0)),
        row(MLSTM_WIDTH), col(MLSTM_WIDTH), col(MLSTM_WIDTH),
        col(SUBLANES),
        row(d), row(d), row(d),
    )
    in_specs = [row(d), _full(ln_g.shape), _full(ln_b.shape), _full(wqkv.shape), _full(wuvo.shape),
                _full(wift.shape), _full(wg.shape), tab, tab]
    return pl.pallas_call(
        _inproj_kernel, grid=(n // tm,), in_specs=in_specs, out_specs=out_specs, out_shape=out_shape,
        compiler_params=_params("parallel"), name="inproj",
    )(x2, ln_g, ln_b, wqkv, wuvo, wift, wg, cos, sin)


def _moba_kernel(qt_ref, k_ref, vt_ref, km_ref, o_ref, bias_ref, m_ref, l_ref, acc_ref, s_ref):
    i = pl.program_id(1)
    blk = MOBA_BLOCK
    hd = ATTN_HEAD_DIM
    heads = ATTN_HEADS
    nb = k_ref.shape[2] // blk
    blk_id = lax.broadcasted_iota(jnp.int32, (nb, blk), 0)
    key_pos = lax.broadcasted_iota(jnp.int32, (blk, blk), 0)
    qry_pos = lax.broadcasted_iota(jnp.int32, (blk, blk), 1)
    causal = key_pos <= qry_pos

    for h in range(heads):
        qt = qt_ref[0, h]
        km = km_ref[0, h]
        km_hi = km.astype(BF16)
        km_lo = (km - km_hi.astype(F32)).astype(BF16)
        gate = _dot(km_hi, qt) + _dot(km_lo, qt)
        gate = jnp.where(blk_id < i, gate, NEG_INF)
        for j in range(nb - 1):
            row = gate[j:j + 1, :]
            beats = (gate > row) | ((gate == row) & (blk_id < j))
            cnt = jnp.sum(jnp.where(beats, 1.0, 0.0), axis=0, keepdims=True)
            sel = (cnt < float(MOBA_TOPK)) & (row > NEG_INF)
            bias_ref[j * heads + h] = jnp.where(sel, 0.0, NEG_INF)
    for h in range(heads):
        bias_ref[i * heads + h] = jnp.zeros((1, blk), F32)

    def scores(h, j, own_block):
        qt = qt_ref[0, h]
        half = blk // 2
        m_tile = None
        for c in range(2):
            rows = slice(c * half, (c + 1) * half)
            s = _dot(k_ref[0, h, pl.ds(pl.multiple_of(j * blk + c * half, half), half), :], qt)
            if own_block:
                s = jnp.where(causal[rows], s, NEG_INF)
            s_ref[j * heads + h, rows, :] = s
            m_c = jnp.max(s, axis=0, keepdims=True)
            m_tile = m_c if m_tile is None else jnp.maximum(m_tile, m_c)
        return m_tile

    for h in range(heads):
        m_ref[h] = scores(h, i, True)

    def past_scores(j):
        for h in range(heads):
            m_ref[h] = jnp.maximum(m_ref[h], scores(h, j, False) + bias_ref[j * heads + h])

    _loop_pairs(i, past_scores)

    l_ref[...] = jnp.zeros_like(l_ref)
    acc_ref[...] = jnp.zeros_like(acc_ref)

    def accumulate(j):
        off = pl.multiple_of(j * blk, blk)
        for h in range(heads):
            p = jnp.exp2(s_ref[j * heads + h] - (m_ref[h] - bias_ref[j * heads + h]))
            l_ref[h] += jnp.sum(p, axis=0, keepdims=True)
            acc_ref[h] += _dot(vt_ref[0, h, :, pl.ds(off, blk)], p.astype(BF16))

    _loop_pairs(i + 1, accumulate)
    yt = acc_ref[...] / l_ref[...]
    o_ref[0] = yt.reshape(heads * hd, blk).T.astype(BF16)


def _moba(qt, k, vt, km):
    batch, heads, seq, hd = k.shape
    blk = MOBA_BLOCK
    nb = seq // blk
    return pl.pallas_call(
        _moba_kernel, grid=(batch, nb),
        in_specs=[
            pl.BlockSpec((1, heads, hd, blk), lambda b, i: (b, 0, 0, i)),
            pl.BlockSpec((1, heads, seq, hd), lambda b, i: (b, 0, 0, 0)),
            pl.BlockSpec((1, heads, hd, seq), lambda b, i: (b, 0, 0, 0)),
            pl.BlockSpec((1, heads, nb, hd), lambda b, i: (b, 0, 0, 0)),
        ],
        out_specs=pl.BlockSpec((1, blk, heads * hd), lambda b, i: (b, i, 0)),
        out_shape=jax.ShapeDtypeStruct((batch, seq, heads * hd), BF16),
        scratch_shapes=[pltpu.VMEM((nb * heads, 1, blk), F32), pltpu.VMEM((heads, 1, blk), F32),
                        pltpu.VMEM((heads, 1, blk), F32), pltpu.VMEM((heads, hd, blk), F32),
                        pltpu.VMEM((nb * heads, blk, blk), F32)],
        compiler_params=_params("parallel", "arbitrary"), name="moba",
    )(qt, k, vt, km)


def _mlstm_kernel(u_ref, vmt_ref, ot_ref, ift_ref, cw_ref, cb_ref, wqt_ref, wk_ref, bcol_ref,
                  gn_ref, skip_ref, y_ref, ext_ref, c_ref, n_ref, m_ref, yt_ref):
    tm = u_ref.shape[0]
    hd = MLSTM_HEAD_DIM
    halo = SUBLANES

    @pl.when(pl.program_id(1) == 0)
    def _():
        ext_ref[0:halo, :] = jnp.zeros((halo, MLSTM_WIDTH), F32)
        c_ref[...] = jnp.zeros_like(c_ref)
        n_ref[...] = jnp.zeros_like(n_ref)
        m_ref[...] = jnp.zeros_like(m_ref)

    u = u_ref[...]
    ext_ref[halo:halo + tm, :] = u
    acc = jnp.broadcast_to(cb_ref[...], u.shape)
    for j in range(MLSTM_CONV):
        acc = acc + cw_ref[j:j + 1, :] * ext_ref[halo - (MLSTM_CONV - 1) + j:halo - (MLSTM_CONV - 1) + j + tm, :]
    ext_ref[0:halo, :] = u[tm - halo:, :]
    uc = acc * jax.nn.sigmoid(acc)

    gr = ift_ref[...] + bcol_ref[...]
    rows = lax.broadcasted_iota(jnp.int32, (tm, tm), 0)
    cols = lax.broadcasted_iota(jnp.int32, (tm, tm), 1)
    causal_t = rows <= cols
    triu = jnp.where(causal_t, 1.0, 0.0).astype(BF16)
    r1, r2, r3 = _split3(_log_sigmoid(gr))
    bcum_r = _dot(r1, triu) + _dot(r2, triu) + _dot(r3, triu)
    key_rows = gr[:MLSTM_HEADS, :] - bcum_r[MLSTM_HEADS:, :]
    key_cols = jnp.concatenate([key_rows, jnp.zeros((LANES - MLSTM_HEADS, tm), F32)], axis=0).T

    uct = uc.T

    def decay_weights(st, h, hs):
        fl = MLSTM_HEADS + h
        b_row = bcum_r[fl:fl + 1, :]
        st["key_row"] = key_rows[h:h + 1, :]
        st["key_col"] = key_cols[:, h:h + 1]
        m_prev = m_ref[h][:, 0:1]
        dlog = jnp.where(causal_t, st["key_col"] + b_row, NEG_INF)
        inter = b_row + m_prev
        m_t = jnp.maximum(inter, jnp.max(dlog, axis=0, keepdims=True))
        st["w_intra"] = jnp.exp(dlog - m_t)
        st["w_inter"] = jnp.exp(inter - m_t)
        st["m_t"], st["m_prev"], st["b_end"] = m_t, m_prev, b_row[:, tm - 1:tm]

    def project(st, h, hs):
        st["qtb"] = _dot(wqt_ref[h], uct[hs, :].astype(BF16)).astype(BF16)
        st["k"] = _dot(uc[:, hs].astype(BF16), wk_ref[h]) * (hd ** -0.5)

    def scores(st, h, hs):
        st["s"] = _dot(st["k"].astype(BF16), st["qtb"]) * st.pop("w_intra")

    def readout(st, h, hs):
        qtb, s, w_inter, m_t = st.pop("qtb"), st.pop("s"), st.pop("w_inter"), st.pop("m_t")
        n_prev = n_ref[h]
        n_hi = n_prev.astype(BF16)
        n_lo = (n_prev - n_hi.astype(F32)).astype(BF16)
        qn = (_dot(n_hi, qtb) + _dot(n_lo, qtb))[0:1, :]
        num = w_inter * _dot(c_ref[h].astype(BF16), qtb) + _dot(vmt_ref[hs, :], s.astype(BF16))
        den = w_inter * qn + jnp.sum(s, axis=0, keepdims=True)
        st["hh"] = num / jnp.maximum(jnp.abs(den), jnp.exp(-m_t))

    def update_state(st, h, hs):
        b_end, m_prev = st.pop("b_end"), st.pop("m_prev")
        m_new = jnp.maximum(b_end + m_prev, jnp.max(b_end + st.pop("key_row"), axis=1, keepdims=True))
        decay = jnp.exp(b_end + m_prev - m_new)
        kw = st.pop("k") * jnp.exp(b_end + st.pop("key_col") - m_new)
        n_prev = n_ref[h]
        c_ref[h] = decay * c_ref[h] + _dot(vmt_ref[hs, :], kw.astype(BF16))
        n_ref[h] = decay * n_prev + jnp.broadcast_to(jnp.sum(kw, axis=0, keepdims=True), n_prev.shape)
        m_ref[h] = jnp.broadcast_to(m_new, (1, LANES))

    def gate_and_norm(st, h, hs):
        hh = jax.nn.sigmoid(ot_ref[hs, :]) * st.pop("hh")
        mu = jnp.mean(hh, axis=0, keepdims=True)
        hc = hh - mu
        var = jnp.mean(hc * hc, axis=0, keepdims=True)
        yt_ref[hs, :] = hc * lax.rsqrt(var + GN_EPS) * gn_ref[hs, :] + skip_ref[hs, :] * uct[hs, :]

    _run_skewed((decay_weights, project, scores, readout, update_state, gate_and_norm), MLSTM_HEADS, hd)
    y_ref[...] = yt_ref[...].T.astype(BF16)


def _mlstm(u, vmt, ot, ift, conv_w, conv_b, wqt, wk, bcol, gn_g, skip, batch, seq):
    n = u.shape[0]
    tm = ROW_TILE
    nc = seq // tm
    row = lambda w: pl.BlockSpec((tm, w), lambda b, c: (b * nc + c, 0))
    col = lambda h: pl.BlockSpec((h, tm), lambda b, c: (0, b * nc + c))
    in_specs = [row(MLSTM_WIDTH), col(MLSTM_WIDTH), col(MLSTM_WIDTH), col(SUBLANES),
                _full(conv_w.shape), _full(conv_b.shape), _full(wqt.shape), _full(wk.shape),
                _full(bcol.shape), _full(gn_g.shape), _full(skip.shape)]
    return pl.pallas_call(
        _mlstm_kernel, grid=(batch, nc), in_specs=in_specs, out_specs=row(MLSTM_WIDTH),
        out_shape=jax.ShapeDtypeStruct((n, MLSTM_WIDTH), BF16),
        scratch_shapes=[pltpu.VMEM((SUBLANES + tm, MLSTM_WIDTH), F32),
                        pltpu.VMEM((MLSTM_HEADS, MLSTM_HEAD_DIM, MLSTM_HEAD_DIM), F32),
                        pltpu.VMEM((MLSTM_HEADS, SUBLANES, MLSTM_HEAD_DIM), F32),
                        pltpu.VMEM((MLSTM_HEADS, 1, LANES), F32),
                        pltpu.VMEM((MLSTM_WIDTH, tm), F32)],
        compiler_params=_params("parallel", "arbitrary"), name="mlstm",
    )(u, vmt, ot, ift, conv_w, conv_b, wqt, wk, bcol, gn_g, skip)


def _mix_kernel(xn_ref, ya_ref, ym_ref, ga_ref, gm_ref, wau_ref, wmu_ref, wout_ref,
                g1_ref, b1_ref, wrc_ref, br_ref,
                x1_ref, ri_ref, rw_ref, cnt_out_ref, cnt_ref):
    @pl.when(pl.program_id(0) == 0)
    def _():
        cnt_ref[...] = jnp.zeros_like(cnt_ref)

    tm = ROW_TILE
    sub = lax.broadcasted_iota(jnp.int32, (LANES, tm), 0).astype(F32)
    big = float(4 * LANES)

    def up_and_mix(st, c, rs):
        a_up = _dot(ya_ref[rs, :], wau_ref[...])
        m_up = _dot(ym_ref[rs, :], wmu_ref[...])
        mix = ga_ref[rs, :].astype(F32) * a_up + gm_ref[rs, :].astype(F32) * m_up
        st["mix"] = mix.astype(BF16)

    def out_and_norm(st, c, rs):
        x1 = _layer_norm(DEEPNORM_ALPHA * xn_ref[rs, :] + _dot(st.pop("mix"), wout_ref[...]), g1_ref[...], b1_ref[...])
        x1_ref[rs, :] = x1
        st["x1"] = x1

    def router_logits(st, c, rs):
        x1 = st.pop("x1")
        x_hi = x1.astype(BF16)
        x_lo = (x1 - x_hi.astype(F32)).astype(BF16)
        both = _dot_nt(wrc_ref[...], x_hi)
        st["logits"] = both[:LANES] + both[LANES:] + _dot_nt(wrc_ref[:LANES, :], x_lo) + br_ref[...]

    def route(st, c, rs):
        logits = st.pop("logits")
        is_g = (sub >= float(MOE_EXPERTS)) & (sub < float(MOE_EXPERTS + MOE_GROUPS))
        gl = jnp.where(is_g, logits, NEG_INF)
        ge = jnp.exp(gl - jnp.max(gl, axis=0, keepdims=True))
        gp = ge / jnp.sum(ge, axis=0, keepdims=True)
        g_w = jnp.max(gp, axis=0, keepdims=True)
        g_idx = jnp.min(jnp.where((gp == g_w) & is_g, sub - float(MOE_EXPERTS), big), axis=0, keepdims=True)
        lo = g_idx * float(MOE_EXPERTS_PER_GROUP)
        in_grp = (sub >= lo) & (sub < lo + float(MOE_EXPERTS_PER_GROUP))
        el = jnp.where(in_grp, logits, NEG_INF)
        v1 = jnp.max(el, axis=0, keepdims=True)
        i1 = jnp.min(jnp.where((el == v1) & in_grp, sub, big), axis=0, keepdims=True)
        el2 = jnp.where(sub == i1, NEG_INF, el)
        v2 = jnp.max(el2, axis=0, keepdims=True)
        i2 = jnp.min(jnp.where((el2 == v2) & in_grp & (sub != i1), sub, big), axis=0, keepdims=True)
        e2 = jnp.exp(v2 - v1)
        w0 = g_w / (1.0 + e2)
        w1 = g_w * e2 / (1.0 + e2)
        rw_ref[rs, :] = jnp.where(sub == 0.0, w0, jnp.where(sub == 1.0, w1, 0.0)).T
        st["i1"], st["i2"] = i1, i2

    def rank(st, c, rs):
        i1, i2 = st.pop("i1"), st.pop("i2")
        is1 = sub == i1
        is2 = sub == i2
        onehot = jnp.where(is1 | is2, 1.0, 0.0)
        rows = lax.broadcasted_iota(jnp.int32, (tm, tm), 0)
        cols = lax.broadcasted_iota(jnp.int32, (tm, tm), 1)
        earlier = jnp.where(rows < cols, 1.0, 0.0).astype(BF16)
        before = _dot(onehot.astype(BF16), earlier) + cnt_ref[...]
        r0 = jnp.sum(jnp.where(is1, before, 0.0), axis=0, keepdims=True)
        r1 = jnp.sum(jnp.where(is2, before, 0.0), axis=0, keepdims=True)
        total = cnt_ref[...] + jnp.sum(onehot, axis=1, keepdims=True)
        cnt_ref[...] = total
        cnt_out_ref[...] = total
        ri_t = jnp.where(sub == 0.0, i1, jnp.where(sub == 1.0, i2, jnp.where(sub == 2.0, r0, jnp.where(sub == 3.0, r1, 0.0))))
        ri_ref[:, rs] = ri_t[:SUBLANES, :].astype(jnp.int32)

    _run_skewed((up_and_mix, out_and_norm, router_logits, route, rank), xn_ref.shape[0] // tm, tm)


def _mix(xn, ya, ym, ga, gm, wau, wmu, wout, g1, b1, wrc, br):
    n, d = xn.shape
    tm = MIX_CHAINS * ROW_TILE
    row = lambda w: pl.BlockSpec((tm, w), lambda i: (i, 0))
    in_specs = [row(d), row(ATTN_WIDTH), row(MLSTM_WIDTH), row(d), row(d),
                _full(wau.shape), _full(wmu.shape), _full(wout.shape), _full(g1.shape), _full(b1.shape),
                _full(wrc.shape), _full(br.shape)]
    out_shape = (jax.ShapeDtypeStruct((n, d), F32), jax.ShapeDtypeStruct((SUBLANES, n), jnp.int32),
                 jax.ShapeDtypeStruct((n, LANES), F32), jax.ShapeDtypeStruct((LANES, 1), F32))
    out_specs = (row(d), pl.BlockSpec((SUBLANES, tm), lambda i: (0, i)), row(LANES), _full((LANES, 1)))
    return pl.pallas_call(
        _mix_kernel, grid=(n // tm,), in_specs=in_specs, out_specs=out_specs, out_shape=out_shape,
        scratch_shapes=[pltpu.VMEM((LANES, 1), F32)],
        compiler_params=_params("arbitrary"), name="mix",
    )(xn, ya, ym, ga, gm, wau, wmu, wout, g1, b1, wrc, br)


def _token_rows(d):
    return d // LANES


def _to_token_tiles(dst_ref, x):
    rows, d = x.shape
    nch = _token_rows(d)
    for c in range(nch):
        dst_ref[pl.ds(c, rows, stride=nch), :] = x[:, c * LANES:(c + 1) * LANES]


def _from_token_tiles(src_ref, rows, d):
    nch = _token_rows(d)
    return jnp.concatenate([src_ref[pl.ds(c, rows, stride=nch), :] for c in range(nch)], axis=1)


def _token_copy(src, src_tok, dst, dst_tok, nch, sem):
    s0 = pl.multiple_of(src_tok * nch, nch)
    d0 = pl.multiple_of(dst_tok * nch, nch)
    return pltpu.make_async_copy(src.at[pl.ds(s0, nch), :], dst.at[pl.ds(d0, nch), :], sem)


def _slots_kernel(ri_ref, ps_ref, o_ref):
    ri = ri_ref[...].astype(F32)
    ps = ps_ref[...]
    expert = lax.broadcasted_iota(jnp.int32, (ps.shape[0], ri.shape[1]), 0).astype(F32)
    row_id = lax.broadcasted_iota(jnp.int32, ri.shape, 0)
    out = jnp.zeros(ri.shape, F32)
    for k in range(2):
        start = jnp.sum(jnp.where(expert == ri[k:k + 1, :], jnp.broadcast_to(ps, expert.shape), 0.0),
                        axis=0, keepdims=True)
        out = jnp.where(row_id == k, start + ri[2 + k:3 + k, :], out)
    o_ref[...] = out.astype(jnp.int32)


def _slots(ri, pad_start_col):
    n = ri.shape[1]
    tm = SLOT_TILE
    blk = pl.BlockSpec((SUBLANES, tm), lambda i: (0, i))
    return pl.pallas_call(
        _slots_kernel, grid=(n // tm,), in_specs=[blk, _full(pad_start_col.shape)], out_specs=blk,
        out_shape=jax.ShapeDtypeStruct((SUBLANES, n), jnp.int32),
        compiler_params=_params("parallel"), name="slots",
    )(ri, pad_start_col)


def _slot(dest_ref, r, k):
    return dest_ref[k * ROW_TILE + r]


def _dispatch_kernel(dest_ref, last_ref, x_ref, xs_ref, scr_ref, zero_ref, sem, zsem):
    tm, d = x_ref.shape
    nch = _token_rows(d)
    tb = zero_ref.shape[0] // nch

    @pl.when(pl.program_id(0) == 0)
    def _():
        zero_ref[...] = jnp.zeros_like(zero_ref)

        def desc(tok):
            off = pl.multiple_of(jnp.maximum(tok, 0) * nch, nch)
            return pltpu.make_async_copy(zero_ref, xs_ref.at[pl.ds(off, tb * nch), :], zsem)

        def zstart(e, _):
            @pl.when(last_ref[e] >= 0)
            def _():
                desc(last_ref[e]).start()
            return 0

        def zwait(e, _):
            @pl.when(last_ref[e] >= 0)
            def _():
                desc(last_ref[e]).wait()
            return 0

        lax.fori_loop(0, MOE_EXPERTS, zstart, 0)
        nused = last_ref[MOE_EXPERTS]
        nblk = xs_ref.shape[0] // (tb * nch)
        lax.fori_loop(nused, nblk, lambda b, _: (desc(b * tb).start(), 0)[1], 0)
        lax.fori_loop(0, MOE_EXPERTS, zwait, 0)
        lax.fori_loop(nused, nblk, lambda b, _: (desc(b * tb).wait(), 0)[1], 0)

    step = pl.program_id(0)
    slot = step % 2
    scr = scr_ref.at[slot]
    _to_token_tiles(scr, x_ref[...])

    def start(r, _):
        for k in range(2):
            _token_copy(scr, r, xs_ref, _slot(dest_ref, r, k), nch, sem.at[slot]).start(priority=k)
        return 0

    def drain(which):
        def wait(r, _):
            for k in range(2):
                _token_copy(scr_ref.at[which], 0, xs_ref, 0, nch, sem.at[which]).wait()
            return 0
        lax.fori_loop(0, tm, wait, 0, unroll=8)

    lax.fori_loop(0, tm, start, 0, unroll=8)

    @pl.when(step > 0)
    def _():
        drain(1 - slot)

    @pl.when(step == pl.num_programs(0) - 1)
    def _():
        drain(slot)


def _dispatch(dest, last_blk, x1, n_rows):
    n, d = x1.shape
    tm = ROW_TILE
    nch = _token_rows(d)
    return pl.pallas_call(
        _dispatch_kernel, grid=(n // tm,),
        in_specs=[pl.BlockSpec((2 * tm,), lambda i: (i,), memory_space=pltpu.SMEM),
                  pl.BlockSpec(memory_space=pltpu.SMEM),
                  pl.BlockSpec((tm, d), lambda i: (i, 0))],
        out_specs=pl.BlockSpec(memory_space=pl.ANY),
        out_shape=jax.ShapeDtypeStruct((n_rows * nch, LANES), F32),
        scratch_shapes=[pltpu.VMEM((2, tm * nch, LANES), F32), pltpu.VMEM((EXPERT_TILE * nch, LANES), F32),
                        pltpu.SemaphoreType.DMA((2,)), pltpu.SemaphoreType.DMA(())],
        compiler_params=_params("arbitrary"), name="dispatch",
    )(dest, last_blk, x1)


def _expert_kernel(first_ref, count_ref, widx_ref, nused_ref, wg_ref, wu_ref, wd_ref, xs_ref, ys_ref,
                   wgb_ref, wub_ref, wdb_ref, xbuf_ref, ybuf_ref, in_sem, out_sem):
    del widx_ref
    e = pl.program_id(0)
    nused = nused_ref[0]
    d = wg_ref.shape[1]
    nch = _token_rows(d)
    rows = xbuf_ref.shape[1]
    tb = rows // nch

    def blk(ref, b):
        return ref.at[pl.ds(pl.multiple_of(b * rows, rows), rows), :]

    def in_copy(b, slot):
        return pltpu.make_async_copy(blk(xs_ref, b), xbuf_ref.at[slot], in_sem.at[slot])

    def out_copy(b, slot):
        return pltpu.make_async_copy(ybuf_ref.at[slot], blk(ys_ref, b), out_sem.at[slot])

    n_in = xbuf_ref.shape[0]

    @pl.when(e == 0)
    def _():
        for b0 in range(n_in - 1):
            @pl.when(b0 < nused)
            def _():
                in_copy(b0, b0).start()

    @pl.when(count_ref[e] > 0)
    def _():
        wgb_ref[...] = wg_ref[0].astype(BF16)
        wub_ref[...] = wu_ref[0].astype(BF16)
        wdb_ref[...] = wd_ref[0].astype(BF16)

    def body(b, _):
        slot = b % n_in
        oslot = b % 2
        in_copy(b, slot).wait()

        @pl.when(b + n_in - 1 < nused)
        def _():
            in_copy(b + n_in - 1, (b + n_in - 1) % n_in).start()

        @pl.when(b >= 2)
        def _():
            out_copy(b - 2, oslot).wait()

        xb = _from_token_tiles(xbuf_ref.at[slot], tb, d).astype(BF16)
        g = _dot(xb, wgb_ref[...])
        u = _dot(xb, wub_ref[...])
        hmid = g * jax.nn.sigmoid(g) * u
        _to_token_tiles(ybuf_ref.at[oslot], _dot(hmid.astype(BF16), wdb_ref[...]))
        out_copy(b, oslot).start()
        return 0

    lax.fori_loop(first_ref[e], first_ref[e] + count_ref[e], body, 0)

    @pl.when(e == pl.num_programs(0) - 1)
    def _():
        for back in (2, 1):
            @pl.when(nused >= back)
            def _():
                out_copy(nused - back, (nused - back) % 2).wait()


def _experts(first_blk, blk_count, w_idx, nused, xs, w_gate, w_up, w_down):
    n_exp, d, dff = w_gate.shape
    nch = _token_rows(d)
    rows = EXPERT_TILE * nch
    w_spec = lambda shape: pl.BlockSpec(shape, lambda e, fb, bc, wi, nu: (wi[e], 0, 0))
    any_spec = pl.BlockSpec(memory_space=pl.ANY)
    grid_spec = pltpu.PrefetchScalarGridSpec(
        num_scalar_prefetch=4, grid=(n_exp,),
        in_specs=[w_spec((1, d, dff)), w_spec((1, d, dff)), w_spec((1, dff, d)), any_spec],
        out_specs=any_spec,
        scratch_shapes=[pltpu.VMEM((d, dff), BF16), pltpu.VMEM((d, dff), BF16), pltpu.VMEM((dff, d), BF16),
                        pltpu.VMEM((EXPERT_IN_SLOTS, rows, LANES), F32), pltpu.VMEM((2, rows, LANES), F32),
                        pltpu.SemaphoreType.DMA((EXPERT_IN_SLOTS,)), pltpu.SemaphoreType.DMA((2,))],
    )
    return pl.pallas_call(
        _expert_kernel, grid_spec=grid_spec, out_shape=jax.ShapeDtypeStruct(xs.shape, F32),
        input_output_aliases={7: 0},
        compiler_params=_params("arbitrary"), name="experts",
    )(first_blk, blk_count, w_idx, nused, w_gate, w_up, w_down, xs)


def _combine_kernel(dest_ref, dest_next_ref, x1_ref, rw_ref, g_ref, b_ref, ys_ref, o_ref, buf_ref, sem):
    tm, d = x1_ref.shape
    nch = _token_rows(d)
    step = pl.program_id(0)
    slot = step % 2

    def gather(idx_ref, which):
        def start(r, _):
            for k in range(2):
                _token_copy(ys_ref, _slot(idx_ref, r, k), buf_ref.at[which, k], r, nch,
                            sem.at[which]).start(priority=k)
            return 0
        lax.fori_loop(0, tm, start, 0, unroll=8)

    @pl.when(step == 0)
    def _():
        gather(dest_ref, 0)

    @pl.when(step + 1 < pl.num_programs(0))
    def _():
        gather(dest_next_ref, 1 - slot)

    def wait(r, _):
        for k in range(2):
            _token_copy(ys_ref, 0, buf_ref.at[slot, k], 0, nch, sem.at[slot]).wait()
        return 0

    lax.fori_loop(0, tm, wait, 0, unroll=8)
    rw = rw_ref[...]
    y0 = _from_token_tiles(buf_ref.at[slot, 0], tm, d)
    y1 = _from_token_tiles(buf_ref.at[slot, 1], tm, d)
    ffn = rw[:, 0:1] * y0 + rw[:, 1:2] * y1
    o_ref[...] = _layer_norm(DEEPNORM_ALPHA * x1_ref[...] + ffn, g_ref[...], b_ref[...])


def _combine(dest, x1, rw, ln_g, ln_b, ys):
    n, d = x1.shape
    tm = ROW_TILE
    nch = _token_rows(d)
    last = n // tm - 1
    row = lambda w: pl.BlockSpec((tm, w), lambda i: (i, 0))
    return pl.pallas_call(
        _combine_kernel, grid=(n // tm,),
        in_specs=[pl.BlockSpec((2 * tm,), lambda i: (i,), memory_space=pltpu.SMEM),
                  pl.BlockSpec((2 * tm,), lambda i: (jnp.minimum(i + 1, last),), memory_space=pltpu.SMEM),
                  row(d), row(LANES), _full(ln_g.shape), _full(ln_b.shape),
                  pl.BlockSpec(memory_space=pl.ANY)],
        out_specs=row(d),
        out_shape=jax.ShapeDtypeStruct((n, d), F32),
        scratch_shapes=[pltpu.VMEM((2, 2, tm * nch, LANES), F32), pltpu.SemaphoreType.DMA((2,))],
        compiler_params=_params("arbitrary"), name="combine",
    )(dest, dest, x1, rw, ln_g, ln_b, ys)


def _rope_tables(seq):
    half = ATTN_HEAD_DIM // 2
    inv_freq = ROPE_THETA ** (-jnp.arange(half, dtype=F32) / half)
    ang = jnp.arange(seq, dtype=F32)[:, None] * inv_freq[None, :]
    cos = jnp.cos(ang)
    sin = jnp.sin(ang)
    cos_h = jnp.concatenate([cos, cos], axis=1)
    sin_h = jnp.concatenate([-sin, sin], axis=1)
    return jnp.tile(cos_h, (1, ATTN_HEADS)), jnp.tile(sin_h, (1, ATTN_HEADS))


def _pad_lanes(a, width=LANES):
    return jnp.pad(a, ((0, 0), (0, width - a.shape[1])))


def kernel(x, ln0_g, ln0_b, w_in, conv_w, conv_b, w_mq, w_mk, b_i, b_f, gn_g, skip, w_attn_up, w_mlstm_up, w_out,
           ln1_g, ln1_b, w_router_group, b_router_group, w_router_expert, b_router_expert, w_gate, w_up, w_down,
           ln2_g, ln2_b):
    batch, seq, d = x.shape
    n = batch * seq
    assert seq % ROW_TILE == 0 and ROW_TILE == MOBA_BLOCK and w_in.shape[0] == DEPTH
    x2 = x.reshape(n, d)
    vec = lambda a: a.reshape(1, -1).astype(F32)

    w = w_in[0]
    c_if = 3 * ATTN_WIDTH + 3 * MLSTM_WIDTH
    c_g = c_if + 2 * MLSTM_HEADS
    wqkv = w[:, :3 * ATTN_WIDTH].astype(BF16)
    wuvo = w[:, 3 * ATTN_WIDTH:c_if].astype(BF16)
    wift = w[:, c_if:c_g].T.astype(BF16)
    wg = w[:, c_g:].astype(BF16)
    cos, sin = _rope_tables(seq)

    q, k, v, kmean, u, vm, o, ift, ga, gm, xn = _inproj(
        x2, vec(ln0_g), vec(ln0_b), wqkv, wuvo, wift, wg, cos, sin, batch, seq)

    nb = seq // MOBA_BLOCK
    km = kmean.reshape(batch, nb, ATTN_HEADS, ATTN_HEAD_DIM).transpose(0, 2, 1, 3)
    ya = _moba(q, k, v, km).reshape(n, ATTN_WIDTH)

    b_if = jnp.concatenate([b_i[0], b_f[0]]).astype(F32)
    ym = _mlstm(u, vm, o, ift, conv_w[0], vec(conv_b[0]), w_mq[0].transpose(0, 2, 1).astype(BF16),
                w_mk[0].astype(BF16), b_if[:, None],
                gn_g[0].astype(F32)[:, None], skip[0].astype(F32)[:, None], batch, seq)

    w_r = _pad_lanes(jnp.concatenate([w_router_expert[0], w_router_group[0]], axis=1))
    w_r_hi = w_r.astype(BF16)
    w_r_lo = (w_r - w_r_hi.astype(F32)).astype(BF16)
    w_rc = jnp.concatenate([w_r_hi.T, w_r_lo.T], axis=0)
    b_r = _pad_lanes(jnp.concatenate([b_router_expert[0], b_router_group[0]])[None, :]).T
    x1, ri, rw, counts = _mix(
        xn, ya, ym, ga, gm, w_attn_up[0].astype(BF16), w_mlstm_up[0].astype(BF16),
        w_out[0].astype(BF16), vec(ln1_g[0]), vec(ln1_b[0]), w_rc, b_r)

    tb = EXPERT_TILE
    nblk = (2 * n) // tb + MOE_EXPERTS
    cnt = counts[:MOE_EXPERTS, 0].astype(jnp.int32)
    nblk_e = (cnt + tb - 1) // tb
    blk_end = jnp.cumsum(nblk_e)
    pad_start = (blk_end - nblk_e) * tb
    nused = blk_end[-1:]
    ids = jnp.arange(MOE_EXPERTS, dtype=jnp.int32)
    prev_used = jnp.max(jnp.where((ids[None, :] <= ids[:, None]) & (nblk_e[None, :] > 0), ids[None, :], -1), axis=1)
    first_used = jnp.min(jnp.where(nblk_e > 0, ids, MOE_EXPERTS - 1))
    w_idx = jnp.where(prev_used >= 0, prev_used, first_used).astype(jnp.int32)
    last_blk = jnp.where(nblk_e > 0, (blk_end - 1) * tb, -1)
    last_blk = jnp.concatenate([last_blk, nused]).astype(jnp.int32)
    dest = _slots(ri, pad_start.astype(F32)[:, None])
    dest = dest[:2].reshape(2, n // ROW_TILE, ROW_TILE).transpose(1, 0, 2).reshape(2 * n)

    xs = _dispatch(dest, last_blk, x1, nblk * tb)
    ys = _experts((blk_end - nblk_e).astype(jnp.int32), nblk_e.astype(jnp.int32), w_idx, nused.astype(jnp.int32),
                  xs, w_gate[0], w_up[0], w_down[0])
    out = _combine(dest, x1, rw, vec(ln2_g[0]), vec(ln2_b[0]), ys)
    return out.reshape(batch, seq, d)
```

```python
import functools
import math

import jax
import jax.numpy as jnp
from jax import lax
from jax.experimental import pallas as pl
from jax.experimental.pallas import tpu as pltpu

F32 = jnp.float32
BF16 = jnp.bfloat16

ATTN_HEADS = 8
ATTN_HEAD_DIM = 64
ATTN_WIDTH = ATTN_HEADS * ATTN_HEAD_DIM
MOBA_BLOCK = 256
MOBA_TOPK = 3
ROPE_THETA = 10000.0
MLSTM_HEADS = 4
MLSTM_HEAD_DIM = 128
MLSTM_WIDTH = MLSTM_HEADS * MLSTM_HEAD_DIM
MLSTM_CONV = 4
MOE_GROUPS = 8
MOE_EXPERTS_PER_GROUP = 8
MOE_EXPERTS = MOE_GROUPS * MOE_EXPERTS_PER_GROUP
MOE_D_FF = 512
LN_EPS = 1e-5
GN_EPS = 1e-6
DEPTH = 1
DEEPNORM_ALPHA = (2 * DEPTH) ** 0.25

LANES = 128
SUBLANES = 8
ROW_TILE = 256
EXPERT_TILE = 256
EXPERT_IN_SLOTS = 4
INPROJ_CHAINS = 2
MLSTM_CHUNKS = 2
SLOT_TILE = 2048
MIX_CHAINS = 4
VMEM_LIMIT = 48 * 1024 * 1024
LOG2_E = math.log2(math.e)

NEG_INF = float("-inf")


def _params(*sem):
    return pltpu.CompilerParams(dimension_semantics=sem, vmem_limit_bytes=VMEM_LIMIT)


def _dot(a, b):
    return jnp.dot(a, b, preferred_element_type=F32)


def _dot_nt(a, b):
    return lax.dot_general(a, b, (((1,), (1,)), ((), ())), preferred_element_type=F32)


def _dot_tn(a, b):
    return lax.dot_general(a, b, (((0,), (0,)), ((), ())), preferred_element_type=F32)


def _split3(x):
    x1 = x.astype(BF16)
    r1 = x - x1.astype(F32)
    x2 = r1.astype(BF16)
    r2 = r1 - x2.astype(F32)
    return x1, x2, r2.astype(BF16)


def _layer_norm(x, g, b):
    mu = jnp.mean(x, axis=-1, keepdims=True)
    xc = x - mu
    var = jnp.mean(xc * xc, axis=-1, keepdims=True)
    return xc * lax.rsqrt(var + LN_EPS) * g + b


def _log_sigmoid(x):
    return jnp.minimum(x, 0.0) - jnp.log1p(jnp.exp(-jnp.abs(x)))


def _full(shape):
    nd = len(shape)
    return pl.BlockSpec(shape, lambda *_: (0,) * nd)


def _run_skewed(phases, chains, rows):
    states = [dict() for _ in range(chains)]
    for t in range(chains + len(phases) - 1):
        for c in range(chains):
            if 0 <= t - c < len(phases):
                phases[t - c](states[c], c, slice(c * rows, (c + 1) * rows))


def _loop_groups(count, body, group=4):
    def trip(g, _):
        for d in range(group):
            body(g * group + d)
        return 0

    lax.fori_loop(0, count // group, trip, 0)
    done = (count // group) * group
    size = group // 2
    while size >= 1:
        take = ((count - done) // size) * size
        @pl.when(take > 0)
        def _(done=done, size=size):
            for d in range(size):
                body(done + d)
        done = done + take
        size //= 2


def _inproj_kernel(x_ref, g_ref, b_ref, wqkv_ref, wuvo_ref, wift_ref, wg_ref, cos_ref, sin_ref,
                   q_ref, k_ref, v_ref, km_ref, u_ref, vm_ref, o_ref, ift_ref, ga_ref, gm_ref, xn_ref):
    tm = ROW_TILE
    lane = lax.broadcasted_iota(jnp.int32, (tm, ATTN_WIDTH), 1)
    first_half = (lane % ATTN_HEAD_DIM) < (ATTN_HEAD_DIM // 2)

    def norm(st, c, rs):
        xn = _layer_norm(x_ref[rs, :], g_ref[...], b_ref[...])
        xn_ref[rs, :] = xn
        st["xb"] = xn.astype(BF16)

    def qkv_matmul(st, c, rs):
        st["zqkv"] = _dot(st["xb"], wqkv_ref[...])

    def attn_outputs(st, c, rs):
        zqkv = st.pop("zqkv")
        cos = cos_ref[rs, :]
        sin = sin_ref[rs, :]

        def rope(t):
            fwd = pltpu.roll(t, ATTN_WIDTH - ATTN_HEAD_DIM // 2, axis=1)
            bwd = pltpu.roll(t, ATTN_HEAD_DIM // 2, axis=1)
            return t * cos + jnp.where(first_half, fwd, bwd) * sin

        q = rope(zqkv[:, :ATTN_WIDTH]) * (ATTN_HEAD_DIM ** -0.5 * LOG2_E)
        k = rope(zqkv[:, ATTN_WIDTH:2 * ATTN_WIDTH])
        v = zqkv[:, 2 * ATTN_WIDTH:]
        km_ref[c] = jnp.mean(k, axis=0, keepdims=True)
        qt = q.T
        vt = v.T
        for h in range(ATTN_HEADS):
            sl = slice(h * ATTN_HEAD_DIM, (h + 1) * ATTN_HEAD_DIM)
            q_ref[0, h, :, rs] = qt[sl, :].astype(BF16)
            k_ref[0, h, rs, :] = k[:, sl].astype(BF16)
            v_ref[0, h, :, rs] = vt[sl, :].astype(BF16)

    def uvo_matmul(st, c, rs):
        st["zuvo"] = _dot(st["xb"], wuvo_ref[...])

    def mlstm_outputs(st, c, rs):
        zuvo = st.pop("zuvo")
        u_ref[rs, :] = zuvo[:, :MLSTM_WIDTH]
        vm_ref[:, rs] = zuvo[:, MLSTM_WIDTH:2 * MLSTM_WIDTH].T.astype(BF16)
        o_ref[:, rs] = zuvo[:, 2 * MLSTM_WIDTH:].T
        ift_ref[:, rs] = _dot_nt(wift_ref[...], st["xb"])

    def gate_matmul(st, c, rs):
        st["zg"] = _dot(st.pop("xb"), wg_ref[...])

    def gate_outputs(st, c, rs):
        zg = st.pop("zg")
        d = ga_ref.shape[1]
        ga_ref[rs, :] = jax.nn.sigmoid(zg[:, :d]).astype(BF16)
        gm_ref[rs, :] = jax.nn.sigmoid(zg[:, d:]).astype(BF16)

    _run_skewed((norm, qkv_matmul, attn_outputs, uvo_matmul, mlstm_outputs, gate_matmul, gate_outputs),
                x_ref.shape[0] // tm, tm)


def _inproj(x2, ln_g, ln_b, wqkv, wuvo, wift, wg, cos, sin, batch, seq):
    n, d = x2.shape
    chains = INPROJ_CHAINS
    tm = chains * ROW_TILE
    assert seq % tm == 0
    nsb = seq // tm
    hd = ATTN_HEAD_DIM
    row = lambda w: pl.BlockSpec((tm, w), lambda i: (i, 0))
    col = lambda h: pl.BlockSpec((h, tm), lambda i: (0, i))
    head = pl.BlockSpec((1, ATTN_HEADS, tm, hd), lambda i: (i // nsb, 0, i % nsb, 0))
    head_t = pl.BlockSpec((1, ATTN_HEADS, hd, tm), lambda i: (i // nsb, 0, 0, i % nsb))
    tab = pl.BlockSpec((tm, ATTN_WIDTH), lambda i: (i % nsb, 0))
    head_shape = jax.ShapeDtypeStruct((batch, ATTN_HEADS, seq, hd), BF16)
    head_t_shape = jax.ShapeDtypeStruct((batch, ATTN_HEADS, hd, seq), BF16)
    out_shape = (
        head_t_shape, head_shape, head_t_shape,
        jax.ShapeDtypeStruct((n // ROW_TILE, 1, ATTN_WIDTH), F32),
        jax.ShapeDtypeStruct((n, MLSTM_WIDTH), F32),
        jax.ShapeDtypeStruct((MLSTM_WIDTH, n), BF16),
        jax.ShapeDtypeStruct((MLSTM_WIDTH, n), F32),
        jax.ShapeDtypeStruct((SUBLANES, n), F32),
        jax.ShapeDtypeStruct((n, d), BF16),
        jax.ShapeDtypeStruct((n, d), BF16),
        jax.ShapeDtypeStruct((n, d), F32),
    )
    out_specs = (
        head_t, head, head_t,
        pl.BlockSpec((chains, 1, ATTN_WIDTH), lambda i: (i, 0, 0)),
        row(MLSTM_WIDTH), col(MLSTM_WIDTH), col(MLSTM_WIDTH),
        col(SUBLANES),
        row(d), row(d), row(d),
    )
    in_specs = [row(d), _full(ln_g.shape), _full(ln_b.shape), _full(wqkv.shape), _full(wuvo.shape),
                _full(wift.shape), _full(wg.shape), tab, tab]
    return pl.pallas_call(
        _inproj_kernel, grid=(n // tm,), in_specs=in_specs, out_specs=out_specs, out_shape=out_shape,
        compiler_params=_params("parallel"), name="inproj",
    )(x2, ln_g, ln_b, wqkv, wuvo, wift, wg, cos, sin)


def _moba_kernel(qt_ref, k_ref, vt_ref, km_ref, o_ref, bias_ref, m_ref, l_ref, acc_ref, s_ref):
    i = pl.program_id(1)
    blk = MOBA_BLOCK
    hd = ATTN_HEAD_DIM
    heads = ATTN_HEADS
    nb = k_ref.shape[2] // blk
    blk_id = lax.broadcasted_iota(jnp.int32, (nb, blk), 0)
    key_pos = lax.broadcasted_iota(jnp.int32, (blk, blk), 0)
    qry_pos = lax.broadcasted_iota(jnp.int32, (blk, blk), 1)
    causal = key_pos <= qry_pos

    for h in range(heads):
        qt = qt_ref[0, h]
        km = km_ref[0, h]
        km_hi = km.astype(BF16)
        km_lo = (km - km_hi.astype(F32)).astype(BF16)
        gate = _dot(km_hi, qt) + _dot(km_lo, qt)
        gate = jnp.where(blk_id < i, gate, NEG_INF)
        for j in range(nb - 1):
            row = gate[j:j + 1, :]
            beats = (gate > row) | ((gate == row) & (blk_id < j))
            cnt = jnp.sum(jnp.where(beats, 1.0, 0.0), axis=0, keepdims=True)
            sel = (cnt < float(MOBA_TOPK)) & (row > NEG_INF)
            bias_ref[j * heads + h] = jnp.where(sel, 0.0, NEG_INF)
    for h in range(heads):
        bias_ref[i * heads + h] = jnp.zeros((1, blk), F32)

    def scores(h, j, own_block):
        qt = qt_ref[0, h]
        half = blk // 2
        m_tile = None
        for c in range(2):
            rows = slice(c * half, (c + 1) * half)
            s = _dot(k_ref[0, h, pl.ds(pl.multiple_of(j * blk + c * half, half), half), :], qt)
            if own_block:
                s = jnp.where(causal[rows], s, NEG_INF)
            s_ref[j * heads + h, rows, :] = s
            m_c = jnp.max(s, axis=0, keepdims=True)
            m_tile = m_c if m_tile is None else jnp.maximum(m_tile, m_c)
        return m_tile

    for h in range(heads):
        m_ref[h] = scores(h, i, True)

    def past_scores(j):
        for h in range(heads):
            m_ref[h] = jnp.maximum(m_ref[h], scores(h, j, False) + bias_ref[j * heads + h])

    _loop_groups(i, past_scores)

    l_ref[...] = jnp.zeros_like(l_ref)
    acc_ref[...] = jnp.zeros_like(acc_ref)

    def accumulate(j):
        off = pl.multiple_of(j * blk, blk)
        for h in range(heads):
            p = jnp.exp2(s_ref[j * heads + h] - (m_ref[h] - bias_ref[j * heads + h]))
            l_ref[h] += jnp.sum(p, axis=0, keepdims=True)
            acc_ref[h] += _dot(vt_ref[0, h, :, pl.ds(off, blk)], p.astype(BF16))

    _loop_groups(i + 1, accumulate)
    yt = acc_ref[...] / l_ref[...]
    o_ref[0] = yt.reshape(heads * hd, blk).T.astype(BF16)


def _moba(qt, k, vt, km):
    batch, heads, seq, hd = k.shape
    blk = MOBA_BLOCK
    nb = seq // blk
    return pl.pallas_call(
        _moba_kernel, grid=(batch, nb),
        in_specs=[
            pl.BlockSpec((1, heads, hd, blk), lambda b, i: (b, 0, 0, i)),
            pl.BlockSpec((1, heads, seq, hd), lambda b, i: (b, 0, 0, 0)),
            pl.BlockSpec((1, heads, hd, seq), lambda b, i: (b, 0, 0, 0)),
            pl.BlockSpec((1, heads, nb, hd), lambda b, i: (b, 0, 0, 0)),
        ],
        out_specs=pl.BlockSpec((1, blk, heads * hd), lambda b, i: (b, i, 0)),
        out_shape=jax.ShapeDtypeStruct((batch, seq, heads * hd), BF16),
        scratch_shapes=[pltpu.VMEM((nb * heads, 1, blk), F32), pltpu.VMEM((heads, 1, blk), F32),
                        pltpu.VMEM((heads, 1, blk), F32), pltpu.VMEM((heads, hd, blk), F32),
                        pltpu.VMEM((nb * heads, blk, blk), F32)],
        compiler_params=_params("parallel", "arbitrary"), name="moba",
    )(qt, k, vt, km)


def _mlstm_kernel(u_ref, vmt_ref, ot_ref, ift_ref, cw_ref, cb_ref, wqt_ref, wk_ref, bcol_ref,
                  gn_ref, skip_ref, y_ref, ext_ref, c_ref, n_ref, m_ref, yt_ref):
    @pl.when(pl.program_id(1) == 0)
    def _():
        ext_ref[0:SUBLANES, :] = jnp.zeros((SUBLANES, MLSTM_WIDTH), F32)
        c_ref[...] = jnp.zeros_like(c_ref)
        n_ref[...] = jnp.zeros_like(n_ref)
        m_ref[...] = jnp.zeros_like(m_ref)

    for c in range(u_ref.shape[0] // ROW_TILE):
        _mlstm_chunk(slice(c * ROW_TILE, (c + 1) * ROW_TILE), u_ref, vmt_ref, ot_ref, ift_ref, cw_ref, cb_ref,
                     wqt_ref, wk_ref, bcol_ref, gn_ref, skip_ref, y_ref, ext_ref, c_ref, n_ref, m_ref, yt_ref)


def _mlstm_chunk(rs, u_ref, vmt_ref, ot_ref, ift_ref, cw_ref, cb_ref, wqt_ref, wk_ref, bcol_ref,
                 gn_ref, skip_ref, y_ref, ext_ref, c_ref, n_ref, m_ref, yt_ref):
    tm = ROW_TILE
    hd = MLSTM_HEAD_DIM
    halo = SUBLANES
    u = u_ref[rs, :]
    ext_ref[halo:halo + tm, :] = u
    acc = jnp.broadcast_to(cb_ref[...], u.shape)
    for j in range(MLSTM_CONV):
        acc = acc + cw_ref[j:j + 1, :] * ext_ref[halo - (MLSTM_CONV - 1) + j:halo - (MLSTM_CONV - 1) + j + tm, :]
    ext_ref[0:halo, :] = u[tm - halo:, :]
    uc = acc * jax.nn.sigmoid(acc)

    gr = ift_ref[:, rs] + bcol_ref[...]
    rows = lax.broadcasted_iota(jnp.int32, (tm, tm), 0)
    cols = lax.broadcasted_iota(jnp.int32, (tm, tm), 1)
    causal_t = rows <= cols
    triu = jnp.where(causal_t, 1.0, 0.0).astype(BF16)
    r1, r2, r3 = _split3(_log_sigmoid(gr))
    bcum_r = _dot(r1, triu) + _dot(r2, triu) + _dot(r3, triu)
    key_rows = gr[:MLSTM_HEADS, :] - bcum_r[MLSTM_HEADS:, :]
    key_cols = jnp.concatenate([key_rows, jnp.zeros((LANES - MLSTM_HEADS, tm), F32)], axis=0).T

    uct = uc.T

    def decay_weights(st, h, hs):
        fl = MLSTM_HEADS + h
        b_row = bcum_r[fl:fl + 1, :]
        st["key_row"] = key_rows[h:h + 1, :]
        st["key_col"] = key_cols[:, h:h + 1]
        m_prev = m_ref[h][:, 0:1]
        dlog = jnp.where(causal_t, st["key_col"] + b_row, NEG_INF)
        inter = b_row + m_prev
        m_t = jnp.maximum(inter, jnp.max(dlog, axis=0, keepdims=True))
        st["w_intra"] = jnp.exp(dlog - m_t)
        st["w_inter"] = jnp.exp(inter - m_t)
        st["m_t"], st["m_prev"], st["b_end"] = m_t, m_prev, b_row[:, tm - 1:tm]

    def project(st, h, hs):
        st["qtb"] = _dot(wqt_ref[h], uct[hs, :].astype(BF16)).astype(BF16)
        st["k"] = _dot(uc[:, hs].astype(BF16), wk_ref[h]) * (hd ** -0.5)

    def scores(st, h, hs):
        st["s"] = _dot(st["k"].astype(BF16), st["qtb"]) * st.pop("w_intra")

    def readout(st, h, hs):
        qtb, s, w_inter, m_t = st.pop("qtb"), st.pop("s"), st.pop("w_inter"), st.pop("m_t")
        n_prev = n_ref[h]
        n_hi = n_prev.astype(BF16)
        n_lo = (n_prev - n_hi.astype(F32)).astype(BF16)
        qn = (_dot(n_hi, qtb) + _dot(n_lo, qtb))[0:1, :]
        num = w_inter * _dot(c_ref[h].astype(BF16), qtb) + _dot(vmt_ref[hs, rs], s.astype(BF16))
        den = w_inter * qn + jnp.sum(s, axis=0, keepdims=True)
        st["hh"] = num / jnp.maximum(jnp.abs(den), jnp.exp(-m_t))

    def update_state(st, h, hs):
        b_end, m_prev = st.pop("b_end"), st.pop("m_prev")
        m_new = jnp.maximum(b_end + m_prev, jnp.max(b_end + st.pop("key_row"), axis=1, keepdims=True))
        decay = jnp.exp(b_end + m_prev - m_new)
        kw = st.pop("k") * jnp.exp(b_end + st.pop("key_col") - m_new)
        n_prev = n_ref[h]
        c_ref[h] = decay * c_ref[h] + _dot(vmt_ref[hs, rs], kw.astype(BF16))
        n_ref[h] = decay * n_prev + jnp.broadcast_to(jnp.sum(kw, axis=0, keepdims=True), n_prev.shape)
        m_ref[h] = jnp.broadcast_to(m_new, (1, LANES))

    def gate_and_norm(st, h, hs):
        hh = jax.nn.sigmoid(ot_ref[hs, rs]) * st.pop("hh")
        mu = jnp.mean(hh, axis=0, keepdims=True)
        hc = hh - mu
        var = jnp.mean(hc * hc, axis=0, keepdims=True)
        yt_ref[hs, :] = hc * lax.rsqrt(var + GN_EPS) * gn_ref[hs, :] + skip_ref[hs, :] * uct[hs, :]

    _run_skewed((decay_weights, project, scores, readout, update_state, gate_and_norm), MLSTM_HEADS, hd)
    y_ref[rs, :] = yt_ref[...].T.astype(BF16)


def _mlstm(u, vmt, ot, ift, conv_w, conv_b, wqt, wk, bcol, gn_g, skip, batch, seq):
    n = u.shape[0]
    tm = MLSTM_CHUNKS * ROW_TILE
    assert seq % tm == 0
    nc = seq // tm
    row = lambda w: pl.BlockSpec((tm, w), lambda b, c: (b * nc + c, 0))
    col = lambda h: pl.BlockSpec((h, tm), lambda b, c: (0, b * nc + c))
    in_specs = [row(MLSTM_WIDTH), col(MLSTM_WIDTH), col(MLSTM_WIDTH), col(SUBLANES),
                _full(conv_w.shape), _full(conv_b.shape), _full(wqt.shape), _full(wk.shape),
                _full(bcol.shape), _full(gn_g.shape), _full(skip.shape)]
    return pl.pallas_call(
        _mlstm_kernel, grid=(batch, nc), in_specs=in_specs, out_specs=row(MLSTM_WIDTH),
        out_shape=jax.ShapeDtypeStruct((n, MLSTM_WIDTH), BF16),
        scratch_shapes=[pltpu.VMEM((SUBLANES + ROW_TILE, MLSTM_WIDTH), F32),
                        pltpu.VMEM((MLSTM_HEADS, MLSTM_HEAD_DIM, MLSTM_HEAD_DIM), F32),
                        pltpu.VMEM((MLSTM_HEADS, SUBLANES, MLSTM_HEAD_DIM), F32),
                        pltpu.VMEM((MLSTM_HEADS, 1, LANES), F32),
                        pltpu.VMEM((MLSTM_WIDTH, ROW_TILE), F32)],
        compiler_params=_params("parallel", "arbitrary"), name="mlstm",
    )(u, vmt, ot, ift, conv_w, conv_b, wqt, wk, bcol, gn_g, skip)


def _mix_kernel(xn_ref, ya_ref, ym_ref, ga_ref, gm_ref, wau_ref, wmu_ref, wout_ref,
                g1_ref, b1_ref, wrc_ref, br_ref,
                x1_ref, ri_ref, rw_ref, cnt_out_ref, cnt_ref):
    @pl.when(pl.program_id(0) == 0)
    def _():
        cnt_ref[...] = jnp.zeros_like(cnt_ref)

    tm = ROW_TILE
    sub = lax.broadcasted_iota(jnp.int32, (LANES, tm), 0).astype(F32)
    big = float(4 * LANES)

    def up_and_mix(st, c, rs):
        a_up = _dot(ya_ref[rs, :], wau_ref[...])
        m_up = _dot(ym_ref[rs, :], wmu_ref[...])
        mix = ga_ref[rs, :].astype(F32) * a_up + gm_ref[rs, :].astype(F32) * m_up
        st["mix"] = mix.astype(BF16)

    def out_and_norm(st, c, rs):
        x1 = _layer_norm(DEEPNORM_ALPHA * xn_ref[rs, :] + _dot(st.pop("mix"), wout_ref[...]), g1_ref[...], b1_ref[...])
        x1_ref[rs, :] = x1
        st["x1"] = x1

    def router_logits(st, c, rs):
        x1 = st.pop("x1")
        x_hi = x1.astype(BF16)
        x_lo = (x1 - x_hi.astype(F32)).astype(BF16)
        both = _dot_nt(wrc_ref[...], x_hi)
        st["logits"] = both[:LANES] + both[LANES:] + _dot_nt(wrc_ref[:LANES, :], x_lo) + br_ref[...]

    def route(st, c, rs):
        logits = st.pop("logits")
        is_g = (sub >= float(MOE_EXPERTS)) & (sub < float(MOE_EXPERTS + MOE_GROUPS))
        gl = jnp.where(is_g, logits, NEG_INF)
        ge = jnp.exp(gl - jnp.max(gl, axis=0, keepdims=True))
        gp = ge / jnp.sum(ge, axis=0, keepdims=True)
        g_w = jnp.max(gp, axis=0, keepdims=True)
        g_idx = jnp.min(jnp.where((gp == g_w) & is_g, sub - float(MOE_EXPERTS), big), axis=0, keepdims=True)
        lo = g_idx * float(MOE_EXPERTS_PER_GROUP)
        in_grp = (sub >= lo) & (sub < lo + float(MOE_EXPERTS_PER_GROUP))
        el = jnp.where(in_grp, logits, NEG_INF)
        v1 = jnp.max(el, axis=0, keepdims=True)
        i1 = jnp.min(jnp.where((el == v1) & in_grp, sub, big), axis=0, keepdims=True)
        el2 = jnp.where(sub == i1, NEG_INF, el)
        v2 = jnp.max(el2, axis=0, keepdims=True)
        i2 = jnp.min(jnp.where((el2 == v2) & in_grp & (sub != i1), sub, big), axis=0, keepdims=True)
        e2 = jnp.exp(v2 - v1)
        w0 = g_w / (1.0 + e2)
        w1 = g_w * e2 / (1.0 + e2)
        rw_ref[rs, :] = jnp.where(sub == 0.0, w0, jnp.where(sub == 1.0, w1, 0.0)).T
        st["i1"], st["i2"] = i1, i2

    def rank(st, c, rs):
        i1, i2 = st.pop("i1"), st.pop("i2")
        is1 = sub == i1
        is2 = sub == i2
        onehot = jnp.where(is1 | is2, 1.0, 0.0)
        rows = lax.broadcasted_iota(jnp.int32, (tm, tm), 0)
        cols = lax.broadcasted_iota(jnp.int32, (tm, tm), 1)
        earlier = jnp.where(rows < cols, 1.0, 0.0).astype(BF16)
        before = _dot(onehot.astype(BF16), earlier) + cnt_ref[...]
        r0 = jnp.sum(jnp.where(is1, before, 0.0), axis=0, keepdims=True)
        r1 = jnp.sum(jnp.where(is2, before, 0.0), axis=0, keepdims=True)
        total = cnt_ref[...] + jnp.sum(onehot, axis=1, keepdims=True)
        cnt_ref[...] = total
        cnt_out_ref[...] = total
        ri_t = jnp.where(sub == 0.0, i1, jnp.where(sub == 1.0, i2, jnp.where(sub == 2.0, r0, jnp.where(sub == 3.0, r1, 0.0))))
        ri_ref[:, rs] = ri_t[:SUBLANES, :].astype(jnp.int32)

    _run_skewed((up_and_mix, out_and_norm, router_logits, route, rank), xn_ref.shape[0] // tm, tm)


def _mix(xn, ya, ym, ga, gm, wau, wmu, wout, g1, b1, wrc, br):
    n, d = xn.shape
    tm = MIX_CHAINS * ROW_TILE
    row = lambda w: pl.BlockSpec((tm, w), lambda i: (i, 0))
    in_specs = [row(d), row(ATTN_WIDTH), row(MLSTM_WIDTH), row(d), row(d),
                _full(wau.shape), _full(wmu.shape), _full(wout.shape), _full(g1.shape), _full(b1.shape),
                _full(wrc.shape), _full(br.shape)]
    out_shape = (jax.ShapeDtypeStruct((n, d), F32), jax.ShapeDtypeStruct((SUBLANES, n), jnp.int32),
                 jax.ShapeDtypeStruct((n, LANES), F32), jax.ShapeDtypeStruct((LANES, 1), F32))
    out_specs = (row(d), pl.BlockSpec((SUBLANES, tm), lambda i: (0, i)), row(LANES), _full((LANES, 1)))
    return pl.pallas_call(
        _mix_kernel, grid=(n // tm,), in_specs=in_specs, out_specs=out_specs, out_shape=out_shape,
        scratch_shapes=[pltpu.VMEM((LANES, 1), F32)],
        compiler_params=_params("arbitrary"), name="mix",
    )(xn, ya, ym, ga, gm, wau, wmu, wout, g1, b1, wrc, br)


def _token_rows(d):
    return d // LANES


def _to_token_tiles(dst_ref, x):
    rows, d = x.shape
    nch = _token_rows(d)
    for c in range(nch):
        dst_ref[pl.ds(c, rows, stride=nch), :] = x[:, c * LANES:(c + 1) * LANES]


def _from_token_tiles(src_ref, rows, d):
    nch = _token_rows(d)
    return jnp.concatenate([src_ref[pl.ds(c, rows, stride=nch), :] for c in range(nch)], axis=1)


def _token_copy(src, src_tok, dst, dst_tok, nch, sem):
    s0 = pl.multiple_of(src_tok * nch, nch)
    d0 = pl.multiple_of(dst_tok * nch, nch)
    return pltpu.make_async_copy(src.at[pl.ds(s0, nch), :], dst.at[pl.ds(d0, nch), :], sem)


def _slots_kernel(ri_ref, ps_ref, o_ref):
    ri = ri_ref[...].astype(F32)
    ps = ps_ref[...]
    expert = lax.broadcasted_iota(jnp.int32, (ps.shape[0], ri.shape[1]), 0).astype(F32)
    row_id = lax.broadcasted_iota(jnp.int32, ri.shape, 0)
    out = jnp.zeros(ri.shape, F32)
    for k in range(2):
        start = jnp.sum(jnp.where(expert == ri[k:k + 1, :], jnp.broadcast_to(ps, expert.shape), 0.0),
                        axis=0, keepdims=True)
        out = jnp.where(row_id == k, start + ri[2 + k:3 + k, :], out)
    o_ref[...] = out.astype(jnp.int32)


def _slots(ri, pad_start_col):
    n = ri.shape[1]
    tm = SLOT_TILE
    blk = pl.BlockSpec((SUBLANES, tm), lambda i: (0, i))
    return pl.pallas_call(
        _slots_kernel, grid=(n // tm,), in_specs=[blk, _full(pad_start_col.shape)], out_specs=blk,
        out_shape=jax.ShapeDtypeStruct((SUBLANES, n), jnp.int32),
        compiler_params=_params("parallel"), name="slots",
    )(ri, pad_start_col)


def _slot(dest_ref, r, k):
    return dest_ref[k * ROW_TILE + r]


def _dispatch_kernel(dest_ref, last_ref, x_ref, xs_ref, scr_ref, zero_ref, sem, zsem):
    tm, d = x_ref.shape
    nch = _token_rows(d)
    tb = zero_ref.shape[0] // nch

    @pl.when(pl.program_id(0) == 0)
    def _():
        zero_ref[...] = jnp.zeros_like(zero_ref)

        def desc(tok):
            off = pl.multiple_of(jnp.maximum(tok, 0) * nch, nch)
            return pltpu.make_async_copy(zero_ref, xs_ref.at[pl.ds(off, tb * nch), :], zsem)

        def zstart(e, _):
            @pl.when(last_ref[e] >= 0)
            def _():
                desc(last_ref[e]).start()
            return 0

        def zwait(e, _):
            @pl.when(last_ref[e] >= 0)
            def _():
                desc(last_ref[e]).wait()
            return 0

        lax.fori_loop(0, MOE_EXPERTS, zstart, 0)
        nused = last_ref[MOE_EXPERTS]
        nblk = xs_ref.shape[0] // (tb * nch)
        lax.fori_loop(nused, nblk, lambda b, _: (desc(b * tb).start(), 0)[1], 0)
        lax.fori_loop(0, MOE_EXPERTS, zwait, 0)
        lax.fori_loop(nused, nblk, lambda b, _: (desc(b * tb).wait(), 0)[1], 0)

    step = pl.program_id(0)
    slot = step % 2
    scr = scr_ref.at[slot]
    _to_token_tiles(scr, x_ref[...])

    def start(r, _):
        for k in range(2):
            _token_copy(scr, r, xs_ref, _slot(dest_ref, r, k), nch, sem.at[slot]).start(priority=k)
        return 0

    def drain(which):
        def wait(r, _):
            for k in range(2):
                _token_copy(scr_ref.at[which], 0, xs_ref, 0, nch, sem.at[which]).wait()
            return 0
        lax.fori_loop(0, tm, wait, 0, unroll=8)

    lax.fori_loop(0, tm, start, 0, unroll=8)

    @pl.when(step > 0)
    def _():
        drain(1 - slot)

    @pl.when(step == pl.num_programs(0) - 1)
    def _():
        drain(slot)


def _dispatch(dest, last_blk, x1, n_rows):
    n, d = x1.shape
    tm = ROW_TILE
    nch = _token_rows(d)
    return pl.pallas_call(
        _dispatch_kernel, grid=(n // tm,),
        in_specs=[pl.BlockSpec((2 * tm,), lambda i: (i,), memory_space=pltpu.SMEM),
                  pl.BlockSpec(memory_space=pltpu.SMEM),
                  pl.BlockSpec((tm, d), lambda i: (i, 0))],
        out_specs=pl.BlockSpec(memory_space=pl.ANY),
        out_shape=jax.ShapeDtypeStruct((n_rows * nch, LANES), F32),
        scratch_shapes=[pltpu.VMEM((2, tm * nch, LANES), F32), pltpu.VMEM((EXPERT_TILE * nch, LANES), F32),
                        pltpu.SemaphoreType.DMA((2,)), pltpu.SemaphoreType.DMA(())],
        compiler_params=_params("arbitrary"), name="dispatch",
    )(dest, last_blk, x1)


def _expert_kernel(first_ref, count_ref, widx_ref, nused_ref, wg_ref, wu_ref, wd_ref, xs_ref, ys_ref,
                   wgb_ref, wub_ref, wdb_ref, xbuf_ref, ybuf_ref, in_sem, out_sem):
    del widx_ref
    e = pl.program_id(0)
    nused = nused_ref[0]
    d = wg_ref.shape[1]
    nch = _token_rows(d)
    rows = xbuf_ref.shape[1]
    tb = rows // nch

    def blk(ref, b):
        return ref.at[pl.ds(pl.multiple_of(b * rows, rows), rows), :]

    def in_copy(b, slot):
        return pltpu.make_async_copy(blk(xs_ref, b), xbuf_ref.at[slot], in_sem.at[slot])

    def out_copy(b, slot):
        return pltpu.make_async_copy(ybuf_ref.at[slot], blk(ys_ref, b), out_sem.at[slot])

    n_in = xbuf_ref.shape[0]

    @pl.when(e == 0)
    def _():
        for b0 in range(n_in - 1):
            @pl.when(b0 < nused)
            def _():
                in_copy(b0, b0).start()

    @pl.when(count_ref[e] > 0)
    def _():
        wgb_ref[...] = wg_ref[0].astype(BF16)
        wub_ref[...] = wu_ref[0].astype(BF16)
        wdb_ref[...] = wd_ref[0].astype(BF16)

    def body(b, _):
        slot = b % n_in
        oslot = b % 2
        in_copy(b, slot).wait()

        @pl.when(b + n_in - 1 < nused)
        def _():
            in_copy(b + n_in - 1, (b + n_in - 1) % n_in).start()

        @pl.when(b >= 2)
        def _():
            out_copy(b - 2, oslot).wait()

        xb = _from_token_tiles(xbuf_ref.at[slot], tb, d).astype(BF16)
        g = _dot(xb, wgb_ref[...])
        u = _dot(xb, wub_ref[...])
        hmid = g * jax.nn.sigmoid(g) * u
        _to_token_tiles(ybuf_ref.at[oslot], _dot(hmid.astype(BF16), wdb_ref[...]))
        out_copy(b, oslot).start()
        return 0

    lax.fori_loop(first_ref[e], first_ref[e] + count_ref[e], body, 0)

    @pl.when(e == pl.num_programs(0) - 1)
    def _():
        for back in (2, 1):
            @pl.when(nused >= back)
            def _():
                out_copy(nused - back, (nused - back) % 2).wait()


def _experts(first_blk, blk_count, w_idx, nused, xs, w_gate, w_up, w_down):
    n_exp, d, dff = w_gate.shape
    nch = _token_rows(d)
    rows = EXPERT_TILE * nch
    w_spec = lambda shape: pl.BlockSpec(shape, lambda e, fb, bc, wi, nu: (wi[e], 0, 0))
    any_spec = pl.BlockSpec(memory_space=pl.ANY)
    grid_spec = pltpu.PrefetchScalarGridSpec(
        num_scalar_prefetch=4, grid=(n_exp,),
        in_specs=[w_spec((1, d, dff)), w_spec((1, d, dff)), w_spec((1, dff, d)), any_spec],
        out_specs=any_spec,
        scratch_shapes=[pltpu.VMEM((d, dff), BF16), pltpu.VMEM((d, dff), BF16), pltpu.VMEM((dff, d), BF16),
                        pltpu.VMEM((EXPERT_IN_SLOTS, rows, LANES), F32), pltpu.VMEM((2, rows, LANES), F32),
                        pltpu.SemaphoreType.DMA((EXPERT_IN_SLOTS,)), pltpu.SemaphoreType.DMA((2,))],
    )
    return pl.pallas_call(
        _expert_kernel, grid_spec=grid_spec, out_shape=jax.ShapeDtypeStruct(xs.shape, F32),
        input_output_aliases={7: 0},
        compiler_params=_params("arbitrary"), name="experts",
    )(first_blk, blk_count, w_idx, nused, w_gate, w_up, w_down, xs)


def _combine_kernel(dest_ref, dest_next_ref, x1_ref, rw_ref, g_ref, b_ref, ys_ref, o_ref, buf_ref, sem):
    tm, d = x1_ref.shape
    nch = _token_rows(d)
    step = pl.program_id(0)
    slot = step % 2

    def gather(idx_ref, which):
        def start(r, _):
            for k in range(2):
                _token_copy(ys_ref, _slot(idx_ref, r, k), buf_ref.at[which, k], r, nch,
                            sem.at[which]).start(priority=k)
            return 0
        lax.fori_loop(0, tm, start, 0, unroll=8)

    @pl.when(step == 0)
    def _():
        gather(dest_ref, 0)

    @pl.when(step + 1 < pl.num_programs(0))
    def _():
        gather(dest_next_ref, 1 - slot)

    def wait(r, _):
        for k in range(2):
            _token_copy(ys_ref, 0, buf_ref.at[slot, k], 0, nch, sem.at[slot]).wait()
        return 0

    lax.fori_loop(0, tm, wait, 0, unroll=8)
    rw = rw_ref[...]
    y0 = _from_token_tiles(buf_ref.at[slot, 0], tm, d)
    y1 = _from_token_tiles(buf_ref.at[slot, 1], tm, d)
    ffn = rw[:, 0:1] * y0 + rw[:, 1:2] * y1
    o_ref[...] = _layer_norm(DEEPNORM_ALPHA * x1_ref[...] + ffn, g_ref[...], b_ref[...])


def _combine(dest, x1, rw, ln_g, ln_b, ys):
    n, d = x1.shape
    tm = ROW_TILE
    nch = _token_rows(d)
    last = n // tm - 1
    row = lambda w: pl.BlockSpec((tm, w), lambda i: (i, 0))
    return pl.pallas_call(
        _combine_kernel, grid=(n // tm,),
        in_specs=[pl.BlockSpec((2 * tm,), lambda i: (i,), memory_space=pltpu.SMEM),
                  pl.BlockSpec((2 * tm,), lambda i: (jnp.minimum(i + 1, last),), memory_space=pltpu.SMEM),
                  row(d), row(LANES), _full(ln_g.shape), _full(ln_b.shape),
                  pl.BlockSpec(memory_space=pl.ANY)],
        out_specs=row(d),
        out_shape=jax.ShapeDtypeStruct((n, d), F32),
        scratch_shapes=[pltpu.VMEM((2, 2, tm * nch, LANES), F32), pltpu.SemaphoreType.DMA((2,))],
        compiler_params=_params("arbitrary"), name="combine",
    )(dest, dest, x1, rw, ln_g, ln_b, ys)


def _rope_tables(seq):
    half = ATTN_HEAD_DIM // 2
    inv_freq = ROPE_THETA ** (-jnp.arange(half, dtype=F32) / half)
    ang = jnp.arange(seq, dtype=F32)[:, None] * inv_freq[None, :]
    cos = jnp.cos(ang)
    sin = jnp.sin(ang)
    cos_h = jnp.concatenate([cos, cos], axis=1)
    sin_h = jnp.concatenate([-sin, sin], axis=1)
    return jnp.tile(cos_h, (1, ATTN_HEADS)), jnp.tile(sin_h, (1, ATTN_HEADS))


def _pad_lanes(a, width=LANES):
    return jnp.pad(a, ((0, 0), (0, width - a.shape[1])))


def kernel(x, ln0_g, ln0_b, w_in, conv_w, conv_b, w_mq, w_mk, b_i, b_f, gn_g, skip, w_attn_up, w_mlstm_up, w_out,
           ln1_g, ln1_b, w_router_group, b_router_group, w_router_expert, b_router_expert, w_gate, w_up, w_down,
           ln2_g, ln2_b):
    batch, seq, d = x.shape
    n = batch * seq
    assert seq % ROW_TILE == 0 and ROW_TILE == MOBA_BLOCK and w_in.shape[0] == DEPTH
    x2 = x.reshape(n, d)
    vec = lambda a: a.reshape(1, -1).astype(F32)

    w = w_in[0]
    c_if = 3 * ATTN_WIDTH + 3 * MLSTM_WIDTH
    c_g = c_if + 2 * MLSTM_HEADS
    wqkv = w[:, :3 * ATTN_WIDTH].astype(BF16)
    wuvo = w[:, 3 * ATTN_WIDTH:c_if].astype(BF16)
    wift = w[:, c_if:c_g].T.astype(BF16)
    wg = w[:, c_g:].astype(BF16)
    cos, sin = _rope_tables(seq)

    q, k, v, kmean, u, vm, o, ift, ga, gm, xn = _inproj(
        x2, vec(ln0_g), vec(ln0_b), wqkv, wuvo, wift, wg, cos, sin, batch, seq)

    nb = seq // MOBA_BLOCK
    km = kmean.reshape(batch, nb, ATTN_HEADS, ATTN_HEAD_DIM).transpose(0, 2, 1, 3)
    ya = _moba(q, k, v, km).reshape(n, ATTN_WIDTH)

    b_if = jnp.concatenate([b_i[0], b_f[0]]).astype(F32)
    ym = _mlstm(u, vm, o, ift, conv_w[0], vec(conv_b[0]), w_mq[0].transpose(0, 2, 1).astype(BF16),
                w_mk[0].astype(BF16), b_if[:, None],
                gn_g[0].astype(F32)[:, None], skip[0].astype(F32)[:, None], batch, seq)

    w_r = _pad_lanes(jnp.concatenate([w_router_expert[0], w_router_group[0]], axis=1))
    w_r_hi = w_r.astype(BF16)
    w_r_lo = (w_r - w_r_hi.astype(F32)).astype(BF16)
    w_rc = jnp.concatenate([w_r_hi.T, w_r_lo.T], axis=0)
    b_r = _pad_lanes(jnp.concatenate([b_router_expert[0], b_router_group[0]])[None, :]).T
    x1, ri, rw, counts = _mix(
        xn, ya, ym, ga, gm, w_attn_up[0].astype(BF16), w_mlstm_up[0].astype(BF16),
        w_out[0].astype(BF16), vec(ln1_g[0]), vec(ln1_b[0]), w_rc, b_r)

    tb = EXPERT_TILE
    nblk = (2 * n) // tb + MOE_EXPERTS
    cnt = counts[:MOE_EXPERTS, 0].astype(jnp.int32)
    nblk_e = (cnt + tb - 1) // tb
    blk_end = jnp.cumsum(nblk_e)
    pad_start = (blk_end - nblk_e) * tb
    nused = blk_end[-1:]
    ids = jnp.arange(MOE_EXPERTS, dtype=jnp.int32)
    prev_used = jnp.max(jnp.where((ids[None, :] <= ids[:, None]) & (nblk_e[None, :] > 0), ids[None, :], -1), axis=1)
    first_used = jnp.min(jnp.where(nblk_e > 0, ids, MOE_EXPERTS - 1))
    w_idx = jnp.where(prev_used >= 0, prev_used, first_used).astype(jnp.int32)
    last_blk = jnp.where(nblk_e > 0, (blk_end - 1) * tb, -1)
    last_blk = jnp.concatenate([last_blk, nused]).astype(jnp.int32)
    dest = _slots(ri, pad_start.astype(F32)[:, None])
    dest = dest[:2].reshape(2, n // ROW_TILE, ROW_TILE).transpose(1, 0, 2).reshape(2 * n)

    xs = _dispatch(dest, last_blk, x1, nblk * tb)
    ys = _experts((blk_end - nblk_e).astype(jnp.int32), nblk_e.astype(jnp.int32), w_idx, nused.astype(jnp.int32),
                  xs, w_gate[0], w_up[0], w_down[0])
    out = _combine(dest, x1, rw, vec(ln2_g[0]), vec(ln2_b[0]), ys)
    return out.reshape(batch, seq, d)
```

```python
import functools
import math

import jax
import jax.numpy as jnp
import numpy as np
from jax import lax
from jax.experimental import pallas as pl
from jax.experimental.pallas import tpu as pltpu

F32 = jnp.float32
BF16 = jnp.bfloat16

ATTN_HEADS = 8
ATTN_HEAD_DIM = 64
ATTN_WIDTH = ATTN_HEADS * ATTN_HEAD_DIM
MOBA_BLOCK = 256
MOBA_TOPK = 3
ROPE_THETA = 10000.0
MLSTM_HEADS = 4
MLSTM_HEAD_DIM = 128
MLSTM_WIDTH = MLSTM_HEADS * MLSTM_HEAD_DIM
MLSTM_CONV = 4
MOE_GROUPS = 8
MOE_EXPERTS_PER_GROUP = 8
MOE_EXPERTS = MOE_GROUPS * MOE_EXPERTS_PER_GROUP
MOE_D_FF = 512
LN_EPS = 1e-5
GN_EPS = 1e-6
DEPTH = 1
DEEPNORM_ALPHA = (2 * DEPTH) ** 0.25

LANES = 128
SUBLANES = 8
ROW_TILE = 256
EXPERT_TILE = 256
EXPERT_IN_SLOTS = 4
INPROJ_CHAINS = 2
MLSTM_CHUNKS = 2
SLOT_TILE = 2048
MIX_CHAINS = 4
VMEM_LIMIT = 48 * 1024 * 1024
LOG2_E = math.log2(math.e)

NEG_INF = float("-inf")


def _params(*sem):
    return pltpu.CompilerParams(dimension_semantics=sem, vmem_limit_bytes=VMEM_LIMIT)


def _dot(a, b):
    return jnp.dot(a, b, preferred_element_type=F32)


def _dot_nt(a, b):
    return lax.dot_general(a, b, (((1,), (1,)), ((), ())), preferred_element_type=F32)


def _dot_tn(a, b):
    return lax.dot_general(a, b, (((0,), (0,)), ((), ())), preferred_element_type=F32)


def _split3(x):
    x1 = x.astype(BF16)
    r1 = x - x1.astype(F32)
    x2 = r1.astype(BF16)
    r2 = r1 - x2.astype(F32)
    return x1, x2, r2.astype(BF16)


def _layer_norm(x, g, b):
    mu = jnp.mean(x, axis=-1, keepdims=True)
    xc = x - mu
    var = jnp.mean(xc * xc, axis=-1, keepdims=True)
    return xc * lax.rsqrt(var + LN_EPS) * g + b


def _log_sigmoid(x):
    return jnp.minimum(x, 0.0) - jnp.log1p(jnp.exp(-jnp.abs(x)))


def _full(shape):
    nd = len(shape)
    return pl.BlockSpec(shape, lambda *_: (0,) * nd)


def _run_skewed(phases, chains, rows):
    states = [dict() for _ in range(chains)]
    for t in range(chains + len(phases) - 1):
        for c in range(chains):
            if 0 <= t - c < len(phases):
                phases[t - c](states[c], c, slice(c * rows, (c + 1) * rows))


def _loop_groups(count, body, group=4):
    def trip(g, _):
        for d in range(group):
            body(g * group + d)
        return 0

    lax.fori_loop(0, count // group, trip, 0)
    done = (count // group) * group
    size = group // 2
    while size >= 1:
        take = ((count - done) // size) * size
        @pl.when(take > 0)
        def _(done=done, size=size):
            for d in range(size):
                body(done + d)
        done = done + take
        size //= 2


def _inproj_kernel(x_ref, g_ref, b_ref, wqkv_ref, wuvo_ref, wift_ref, wg_ref, cos_ref, sin_ref,
                   q_ref, k_ref, v_ref, km_ref, u_ref, vm_ref, o_ref, ift_ref, ga_ref, gm_ref, xn_ref):
    tm = ROW_TILE
    lane = lax.broadcasted_iota(jnp.int32, (tm, ATTN_WIDTH), 1)
    first_half = (lane % ATTN_HEAD_DIM) < (ATTN_HEAD_DIM // 2)

    def norm(st, c, rs):
        xn = _layer_norm(x_ref[rs, :], g_ref[...], b_ref[...])
        xn_ref[rs, :] = xn
        st["xb"] = xn.astype(BF16)

    def qkv_matmul(st, c, rs):
        st["zqkv"] = _dot(st["xb"], wqkv_ref[...])

    def attn_outputs(st, c, rs):
        zqkv = st.pop("zqkv")
        cos = cos_ref[rs, :]
        sin = sin_ref[rs, :]

        def rope(t):
            fwd = pltpu.roll(t, ATTN_WIDTH - ATTN_HEAD_DIM // 2, axis=1)
            bwd = pltpu.roll(t, ATTN_HEAD_DIM // 2, axis=1)
            return t * cos + jnp.where(first_half, fwd, bwd) * sin

        q = rope(zqkv[:, :ATTN_WIDTH]) * (ATTN_HEAD_DIM ** -0.5 * LOG2_E)
        k = rope(zqkv[:, ATTN_WIDTH:2 * ATTN_WIDTH])
        v = zqkv[:, 2 * ATTN_WIDTH:]
        km_ref[c] = jnp.mean(k, axis=0, keepdims=True)
        qt = q.T
        vt = v.T
        for h in range(ATTN_HEADS):
            sl = slice(h * ATTN_HEAD_DIM, (h + 1) * ATTN_HEAD_DIM)
            q_ref[0, h, :, rs] = qt[sl, :].astype(BF16)
            k_ref[0, h, rs, :] = k[:, sl].astype(BF16)
            v_ref[0, h, :, rs] = vt[sl, :].astype(BF16)

    def uvo_matmul(st, c, rs):
        st["zuvo"] = _dot(st["xb"], wuvo_ref[...])

    def mlstm_outputs(st, c, rs):
        zuvo = st.pop("zuvo")
        u_ref[rs, :] = zuvo[:, :MLSTM_WIDTH]
        vm_ref[:, rs] = zuvo[:, MLSTM_WIDTH:2 * MLSTM_WIDTH].T.astype(BF16)
        o_ref[:, rs] = zuvo[:, 2 * MLSTM_WIDTH:].T
        ift_ref[:, rs] = _dot_nt(wift_ref[...], st["xb"])

    def gate_matmul(st, c, rs):
        st["zg"] = _dot(st.pop("xb"), wg_ref[...])

    def gate_outputs(st, c, rs):
        zg = st.pop("zg")
        d = ga_ref.shape[1]
        ga_ref[rs, :] = jax.nn.sigmoid(zg[:, :d]).astype(BF16)
        gm_ref[rs, :] = jax.nn.sigmoid(zg[:, d:]).astype(BF16)

    _run_skewed((norm, qkv_matmul, attn_outputs, uvo_matmul, mlstm_outputs, gate_matmul, gate_outputs),
                x_ref.shape[0] // tm, tm)


def _inproj(x2, ln_g, ln_b, wqkv, wuvo, wift, wg, cos, sin, batch, seq):
    n, d = x2.shape
    chains = INPROJ_CHAINS
    tm = chains * ROW_TILE
    assert seq % tm == 0
    nsb = seq // tm
    hd = ATTN_HEAD_DIM
    row = lambda w: pl.BlockSpec((tm, w), lambda i: (i, 0))
    col = lambda h: pl.BlockSpec((h, tm), lambda i: (0, i))
    head = pl.BlockSpec((1, ATTN_HEADS, tm, hd), lambda i: (i // nsb, 0, i % nsb, 0))
    head_t = pl.BlockSpec((1, ATTN_HEADS, hd, tm), lambda i: (i // nsb, 0, 0, i % nsb))
    tab = pl.BlockSpec((tm, ATTN_WIDTH), lambda i: (i % nsb, 0))
    head_shape = jax.ShapeDtypeStruct((batch, ATTN_HEADS, seq, hd), BF16)
    head_t_shape = jax.ShapeDtypeStruct((batch, ATTN_HEADS, hd, seq), BF16)
    out_shape = (
        head_t_shape, head_shape, head_t_shape,
        jax.ShapeDtypeStruct((n // ROW_TILE, 1, ATTN_WIDTH), F32),
        jax.ShapeDtypeStruct((n, MLSTM_WIDTH), F32),
        jax.ShapeDtypeStruct((MLSTM_WIDTH, n), BF16),
        jax.ShapeDtypeStruct((MLSTM_WIDTH, n), F32),
        jax.ShapeDtypeStruct((SUBLANES, n), F32),
        jax.ShapeDtypeStruct((n, d), BF16),
        jax.ShapeDtypeStruct((n, d), BF16),
        jax.ShapeDtypeStruct((n, d), F32),
    )
    out_specs = (
        head_t, head, head_t,
        pl.BlockSpec((chains, 1, ATTN_WIDTH), lambda i: (i, 0, 0)),
        row(MLSTM_WIDTH), col(MLSTM_WIDTH), col(MLSTM_WIDTH),
        col(SUBLANES),
        row(d), row(d), row(d),
    )
    in_specs = [row(d), _full(ln_g.shape), _full(ln_b.shape), _full(wqkv.shape), _full(wuvo.shape),
                _full(wift.shape), _full(wg.shape), tab, tab]
    return pl.pallas_call(
        _inproj_kernel, grid=(n // tm,), in_specs=in_specs, out_specs=out_specs, out_shape=out_shape,
        compiler_params=_params("parallel"), name="inproj",
    )(x2, ln_g, ln_b, wqkv, wuvo, wift, wg, cos, sin)


def _moba_kernel(qt_ref, k_ref, vt_ref, km_ref, o_ref, bias_ref, m_ref, l_ref, acc_ref, s_ref):
    i = pl.program_id(1)
    blk = MOBA_BLOCK
    hd = ATTN_HEAD_DIM
    heads = ATTN_HEADS
    nb = k_ref.shape[2] // blk
    blk_id = lax.broadcasted_iota(jnp.int32, (nb, blk), 0)
    key_pos = lax.broadcasted_iota(jnp.int32, (blk, blk), 0)
    qry_pos = lax.broadcasted_iota(jnp.int32, (blk, blk), 1)
    causal = key_pos <= qry_pos

    for h in range(heads):
        qt = qt_ref[0, h]
        km = km_ref[0, h]
        km_hi = km.astype(BF16)
        km_lo = (km - km_hi.astype(F32)).astype(BF16)
        gate = _dot(km_hi, qt) + _dot(km_lo, qt)
        gate = jnp.where(blk_id < i, gate, NEG_INF)
        for j in range(nb - 1):
            row = gate[j:j + 1, :]
            beats = (gate > row) | ((gate == row) & (blk_id < j))
            cnt = jnp.sum(jnp.where(beats, 1.0, 0.0), axis=0, keepdims=True)
            sel = (cnt < float(MOBA_TOPK)) & (row > NEG_INF)
            bias_ref[j * heads + h] = jnp.where(sel, 0.0, NEG_INF)
    for h in range(heads):
        bias_ref[i * heads + h] = jnp.zeros((1, blk), F32)

    def scores(h, j, own_block):
        qt = qt_ref[0, h]
        half = blk // 2
        m_tile = None
        for c in range(2):
            rows = slice(c * half, (c + 1) * half)
            s = _dot(k_ref[0, h, pl.ds(pl.multiple_of(j * blk + c * half, half), half), :], qt)
            if own_block:
                s = jnp.where(causal[rows], s, NEG_INF)
            s_ref[j * heads + h, rows, :] = s
            m_c = jnp.max(s, axis=0, keepdims=True)
            m_tile = m_c if m_tile is None else jnp.maximum(m_tile, m_c)
        return m_tile

    for h in range(heads):
        m_ref[h] = scores(h, i, True)

    def past_scores(j):
        for h in range(heads):
            m_ref[h] = jnp.maximum(m_ref[h], scores(h, j, False) + bias_ref[j * heads + h])

    _loop_groups(i, past_scores)

    l_ref[...] = jnp.zeros_like(l_ref)
    acc_ref[...] = jnp.zeros_like(acc_ref)

    def accumulate(j):
        off = pl.multiple_of(j * blk, blk)
        for h in range(heads):
            p = jnp.exp2(s_ref[j * heads + h] - (m_ref[h] - bias_ref[j * heads + h]))
            l_ref[h] += jnp.sum(p, axis=0, keepdims=True)
            acc_ref[h] += _dot(vt_ref[0, h, :, pl.ds(off, blk)], p.astype(BF16))

    _loop_groups(i + 1, accumulate)
    yt = acc_ref[...] / l_ref[...]
    o_ref[0] = yt.reshape(heads * hd, blk).T.astype(BF16)


def _moba(qt, k, vt, km):
    batch, heads, seq, hd = k.shape
    blk = MOBA_BLOCK
    nb = seq // blk
    return pl.pallas_call(
        _moba_kernel, grid=(batch, nb),
        in_specs=[
            pl.BlockSpec((1, heads, hd, blk), lambda b, i: (b, 0, 0, i)),
            pl.BlockSpec((1, heads, seq, hd), lambda b, i: (b, 0, 0, 0)),
            pl.BlockSpec((1, heads, hd, seq), lambda b, i: (b, 0, 0, 0)),
            pl.BlockSpec((1, heads, nb, hd), lambda b, i: (b, 0, 0, 0)),
        ],
        out_specs=pl.BlockSpec((1, blk, heads * hd), lambda b, i: (b, i, 0)),
        out_shape=jax.ShapeDtypeStruct((batch, seq, heads * hd), BF16),
        scratch_shapes=[pltpu.VMEM((nb * heads, 1, blk), F32), pltpu.VMEM((heads, 1, blk), F32),
                        pltpu.VMEM((heads, 1, blk), F32), pltpu.VMEM((heads, hd, blk), F32),
                        pltpu.VMEM((nb * heads, blk, blk), F32)],
        compiler_params=_params("parallel", "arbitrary"), name="moba",
    )(qt, k, vt, km)


def _mlstm_kernel(u_ref, vmt_ref, ot_ref, ift_ref, cw_ref, cb_ref, wqt_ref, wk_ref, bcol_ref,
                  gn_ref, skip_ref, y_ref, ext_ref, c_ref, n_ref, m_ref, yt_ref):
    @pl.when(pl.program_id(1) == 0)
    def _():
        ext_ref[0:SUBLANES, :] = jnp.zeros((SUBLANES, MLSTM_WIDTH), F32)
        c_ref[...] = jnp.zeros_like(c_ref)
        n_ref[...] = jnp.zeros_like(n_ref)
        m_ref[...] = jnp.zeros_like(m_ref)

    for c in range(u_ref.shape[0] // ROW_TILE):
        _mlstm_chunk(slice(c * ROW_TILE, (c + 1) * ROW_TILE), u_ref, vmt_ref, ot_ref, ift_ref, cw_ref, cb_ref,
                     wqt_ref, wk_ref, bcol_ref, gn_ref, skip_ref, y_ref, ext_ref, c_ref, n_ref, m_ref, yt_ref)


def _mlstm_chunk(rs, u_ref, vmt_ref, ot_ref, ift_ref, cw_ref, cb_ref, wqt_ref, wk_ref, bcol_ref,
                 gn_ref, skip_ref, y_ref, ext_ref, c_ref, n_ref, m_ref, yt_ref):
    tm = ROW_TILE
    hd = MLSTM_HEAD_DIM
    halo = SUBLANES
    u = u_ref[rs, :]
    ext_ref[halo:halo + tm, :] = u
    acc = jnp.broadcast_to(cb_ref[...], u.shape)
    for j in range(MLSTM_CONV):
        acc = acc + cw_ref[j:j + 1, :] * ext_ref[halo - (MLSTM_CONV - 1) + j:halo - (MLSTM_CONV - 1) + j + tm, :]
    ext_ref[0:halo, :] = u[tm - halo:, :]
    uc = acc * jax.nn.sigmoid(acc)

    gr = ift_ref[:, rs] + bcol_ref[...]
    rows = lax.broadcasted_iota(jnp.int32, (tm, tm), 0)
    cols = lax.broadcasted_iota(jnp.int32, (tm, tm), 1)
    causal_t = rows <= cols
    triu = jnp.where(causal_t, 1.0, 0.0).astype(BF16)
    r1, r2, r3 = _split3(_log_sigmoid(gr))
    bcum_r = _dot(r1, triu) + _dot(r2, triu) + _dot(r3, triu)
    key_rows = gr[:MLSTM_HEADS, :] - bcum_r[MLSTM_HEADS:, :]
    key_cols = jnp.concatenate([key_rows, jnp.zeros((LANES - MLSTM_HEADS, tm), F32)], axis=0).T

    uct = uc.T

    def decay_weights(st, h, hs):
        fl = MLSTM_HEADS + h
        b_row = bcum_r[fl:fl + 1, :]
        st["key_row"] = key_rows[h:h + 1, :]
        st["key_col"] = key_cols[:, h:h + 1]
        m_prev = m_ref[h][:, 0:1]
        dlog = jnp.where(causal_t, st["key_col"] + b_row, NEG_INF)
        inter = b_row + m_prev
        m_t = jnp.maximum(inter, jnp.max(dlog, axis=0, keepdims=True))
        st["w_intra"] = jnp.exp(dlog - m_t)
        st["w_inter"] = jnp.exp(inter - m_t)
        st["m_t"], st["m_prev"], st["b_end"] = m_t, m_prev, b_row[:, tm - 1:tm]

    def project(st, h, hs):
        st["qtb"] = _dot(wqt_ref[h], uct[hs, :].astype(BF16)).astype(BF16)
        st["k"] = _dot(uc[:, hs].astype(BF16), wk_ref[h]) * (hd ** -0.5)

    def scores(st, h, hs):
        st["s"] = _dot(st["k"].astype(BF16), st["qtb"]) * st.pop("w_intra")

    def readout(st, h, hs):
        qtb, s, w_inter, m_t = st.pop("qtb"), st.pop("s"), st.pop("w_inter"), st.pop("m_t")
        n_prev = n_ref[h]
        n_hi = n_prev.astype(BF16)
        n_lo = (n_prev - n_hi.astype(F32)).astype(BF16)
        qn = (_dot(n_hi, qtb) + _dot(n_lo, qtb))[0:1, :]
        num = w_inter * _dot(c_ref[h].astype(BF16), qtb) + _dot(vmt_ref[hs, rs], s.astype(BF16))
        den = w_inter * qn + jnp.sum(s, axis=0, keepdims=True)
        st["hh"] = num / jnp.maximum(jnp.abs(den), jnp.exp(-m_t))

    def update_state(st, h, hs):
        b_end, m_prev = st.pop("b_end"), st.pop("m_prev")
        m_new = jnp.maximum(b_end + m_prev, jnp.max(b_end + st.pop("key_row"), axis=1, keepdims=True))
        decay = jnp.exp(b_end + m_prev - m_new)
        kw = st.pop("k") * jnp.exp(b_end + st.pop("key_col") - m_new)
        n_prev = n_ref[h]
        c_ref[h] = decay * c_ref[h] + _dot(vmt_ref[hs, rs], kw.astype(BF16))
        n_ref[h] = decay * n_prev + jnp.broadcast_to(jnp.sum(kw, axis=0, keepdims=True), n_prev.shape)
        m_ref[h] = jnp.broadcast_to(m_new, (1, LANES))

    def gate_and_norm(st, h, hs):
        hh = jax.nn.sigmoid(ot_ref[hs, rs]) * st.pop("hh")
        mu = jnp.mean(hh, axis=0, keepdims=True)
        hc = hh - mu
        var = jnp.mean(hc * hc, axis=0, keepdims=True)
        yt_ref[hs, :] = hc * lax.rsqrt(var + GN_EPS) * gn_ref[hs, :] + skip_ref[hs, :] * uct[hs, :]

    _run_skewed((decay_weights, project, scores, readout, update_state, gate_and_norm), MLSTM_HEADS, hd)
    y_ref[rs, :] = yt_ref[...].T.astype(BF16)


def _mlstm(u, vmt, ot, ift, conv_w, conv_b, wqt, wk, bcol, gn_g, skip, batch, seq):
    n = u.shape[0]
    tm = MLSTM_CHUNKS * ROW_TILE
    assert seq % tm == 0
    nc = seq // tm
    row = lambda w: pl.BlockSpec((tm, w), lambda b, c: (b * nc + c, 0))
    col = lambda h: pl.BlockSpec((h, tm), lambda b, c: (0, b * nc + c))
    in_specs = [row(MLSTM_WIDTH), col(MLSTM_WIDTH), col(MLSTM_WIDTH), col(SUBLANES),
                _full(conv_w.shape), _full(conv_b.shape), _full(wqt.shape), _full(wk.shape),
                _full(bcol.shape), _full(gn_g.shape), _full(skip.shape)]
    return pl.pallas_call(
        _mlstm_kernel, grid=(batch, nc), in_specs=in_specs, out_specs=row(MLSTM_WIDTH),
        out_shape=jax.ShapeDtypeStruct((n, MLSTM_WIDTH), BF16),
        scratch_shapes=[pltpu.VMEM((SUBLANES + ROW_TILE, MLSTM_WIDTH), F32),
                        pltpu.VMEM((MLSTM_HEADS, MLSTM_HEAD_DIM, MLSTM_HEAD_DIM), F32),
                        pltpu.VMEM((MLSTM_HEADS, SUBLANES, MLSTM_HEAD_DIM), F32),
                        pltpu.VMEM((MLSTM_HEADS, 1, LANES), F32),
                        pltpu.VMEM((MLSTM_WIDTH, ROW_TILE), F32)],
        compiler_params=_params("parallel", "arbitrary"), name="mlstm",
    )(u, vmt, ot, ift, conv_w, conv_b, wqt, wk, bcol, gn_g, skip)


def _mix_kernel(xn_ref, ya_ref, ym_ref, ga_ref, gm_ref, wau_ref, wmu_ref, wout_ref,
                g1_ref, b1_ref, wrc_ref, br_ref,
                x1_ref, ri_ref, rw_ref, cnt_out_ref, cnt_ref):
    @pl.when(pl.program_id(0) == 0)
    def _():
        cnt_ref[...] = jnp.zeros_like(cnt_ref)

    tm = ROW_TILE
    sub = lax.broadcasted_iota(jnp.int32, (LANES, tm), 0).astype(F32)
    big = float(4 * LANES)

    def up_and_mix(st, c, rs):
        a_up = _dot(ya_ref[rs, :], wau_ref[...])
        m_up = _dot(ym_ref[rs, :], wmu_ref[...])
        mix = ga_ref[rs, :].astype(F32) * a_up + gm_ref[rs, :].astype(F32) * m_up
        st["mix"] = mix.astype(BF16)

    def out_and_norm(st, c, rs):
        x1 = _layer_norm(DEEPNORM_ALPHA * xn_ref[rs, :] + _dot(st.pop("mix"), wout_ref[...]), g1_ref[...], b1_ref[...])
        x1_ref[rs, :] = x1
        st["x1"] = x1

    def router_logits(st, c, rs):
        x1 = st.pop("x1")
        x_hi = x1.astype(BF16)
        x_lo = (x1 - x_hi.astype(F32)).astype(BF16)
        both = _dot_nt(wrc_ref[...], x_hi)
        st["logits"] = both[:LANES] + both[LANES:] + _dot_nt(wrc_ref[:LANES, :], x_lo) + br_ref[...]

    def route(st, c, rs):
        logits = st.pop("logits")
        is_g = (sub >= float(MOE_EXPERTS)) & (sub < float(MOE_EXPERTS + MOE_GROUPS))
        gl = jnp.where(is_g, logits, NEG_INF)
        ge = jnp.exp(gl - jnp.max(gl, axis=0, keepdims=True))
        gp = ge / jnp.sum(ge, axis=0, keepdims=True)
        g_w = jnp.max(gp, axis=0, keepdims=True)
        g_idx = jnp.min(jnp.where((gp == g_w) & is_g, sub - float(MOE_EXPERTS), big), axis=0, keepdims=True)
        lo = g_idx * float(MOE_EXPERTS_PER_GROUP)
        in_grp = (sub >= lo) & (sub < lo + float(MOE_EXPERTS_PER_GROUP))
        el = jnp.where(in_grp, logits, NEG_INF)
        v1 = jnp.max(el, axis=0, keepdims=True)
        i1 = jnp.min(jnp.where((el == v1) & in_grp, sub, big), axis=0, keepdims=True)
        el2 = jnp.where(sub == i1, NEG_INF, el)
        v2 = jnp.max(el2, axis=0, keepdims=True)
        i2 = jnp.min(jnp.where((el2 == v2) & in_grp & (sub != i1), sub, big), axis=0, keepdims=True)
        e2 = jnp.exp(v2 - v1)
        w0 = g_w / (1.0 + e2)
        w1 = g_w * e2 / (1.0 + e2)
        rw_ref[rs, :] = jnp.where(sub == 0.0, w0, jnp.where(sub == 1.0, w1, 0.0)).T
        st["i1"], st["i2"] = i1, i2

    def rank(st, c, rs):
        i1, i2 = st.pop("i1"), st.pop("i2")
        is1 = sub == i1
        is2 = sub == i2
        onehot = jnp.where(is1 | is2, 1.0, 0.0)
        rows = lax.broadcasted_iota(jnp.int32, (tm, tm), 0)
        cols = lax.broadcasted_iota(jnp.int32, (tm, tm), 1)
        earlier = jnp.where(rows < cols, 1.0, 0.0).astype(BF16)
        before = _dot(onehot.astype(BF16), earlier) + cnt_ref[...]
        r0 = jnp.sum(jnp.where(is1, before, 0.0), axis=0, keepdims=True)
        r1 = jnp.sum(jnp.where(is2, before, 0.0), axis=0, keepdims=True)
        total = cnt_ref[...] + jnp.sum(onehot, axis=1, keepdims=True)
        cnt_ref[...] = total
        cnt_out_ref[...] = total
        ri_t = jnp.where(sub == 0.0, i1, jnp.where(sub == 1.0, i2, jnp.where(sub == 2.0, r0, jnp.where(sub == 3.0, r1, 0.0))))
        ri_ref[:, rs] = ri_t[:SUBLANES, :].astype(jnp.int32)

    _run_skewed((up_and_mix, out_and_norm, router_logits, route, rank), xn_ref.shape[0] // tm, tm)


def _mix(xn, ya, ym, ga, gm, wau, wmu, wout, g1, b1, wrc, br):
    n, d = xn.shape
    tm = MIX_CHAINS * ROW_TILE
    row = lambda w: pl.BlockSpec((tm, w), lambda i: (i, 0))
    in_specs = [row(d), row(ATTN_WIDTH), row(MLSTM_WIDTH), row(d), row(d),
                _full(wau.shape), _full(wmu.shape), _full(wout.shape), _full(g1.shape), _full(b1.shape),
                _full(wrc.shape), _full(br.shape)]
    out_shape = (jax.ShapeDtypeStruct((n, d), F32), jax.ShapeDtypeStruct((SUBLANES, n), jnp.int32),
                 jax.ShapeDtypeStruct((n, LANES), F32), jax.ShapeDtypeStruct((LANES, 1), F32))
    out_specs = (row(d), pl.BlockSpec((SUBLANES, tm), lambda i: (0, i)), row(LANES), _full((LANES, 1)))
    return pl.pallas_call(
        _mix_kernel, grid=(n // tm,), in_specs=in_specs, out_specs=out_specs, out_shape=out_shape,
        scratch_shapes=[pltpu.VMEM((LANES, 1), F32)],
        compiler_params=_params("arbitrary"), name="mix",
    )(xn, ya, ym, ga, gm, wau, wmu, wout, g1, b1, wrc, br)


def _token_rows(d):
    return d // LANES


def _to_token_tiles(dst_ref, x):
    rows, d = x.shape
    nch = _token_rows(d)
    for c in range(nch):
        dst_ref[pl.ds(c, rows, stride=nch), :] = x[:, c * LANES:(c + 1) * LANES]


def _from_token_tiles(src_ref, rows, d):
    nch = _token_rows(d)
    return jnp.concatenate([src_ref[pl.ds(c, rows, stride=nch), :] for c in range(nch)], axis=1)


def _token_copy(src, src_tok, dst, dst_tok, nch, sem):
    s0 = pl.multiple_of(src_tok * nch, nch)
    d0 = pl.multiple_of(dst_tok * nch, nch)
    return pltpu.make_async_copy(src.at[pl.ds(s0, nch), :], dst.at[pl.ds(d0, nch), :], sem)


def _slots_kernel(ri_ref, ps_ref, o_ref):
    ri = ri_ref[...].astype(F32)
    ps = ps_ref[...]
    expert = lax.broadcasted_iota(jnp.int32, (ps.shape[0], ri.shape[1]), 0).astype(F32)
    row_id = lax.broadcasted_iota(jnp.int32, ri.shape, 0)
    out = jnp.zeros(ri.shape, F32)
    for k in range(2):
        start = jnp.sum(jnp.where(expert == ri[k:k + 1, :], jnp.broadcast_to(ps, expert.shape), 0.0),
                        axis=0, keepdims=True)
        out = jnp.where(row_id == k, start + ri[2 + k:3 + k, :], out)
    o_ref[...] = out.astype(jnp.int32)


def _slots(ri, pad_start_col):
    n = ri.shape[1]
    tm = SLOT_TILE
    blk = pl.BlockSpec((SUBLANES, tm), lambda i: (0, i))
    return pl.pallas_call(
        _slots_kernel, grid=(n // tm,), in_specs=[blk, _full(pad_start_col.shape)], out_specs=blk,
        out_shape=jax.ShapeDtypeStruct((SUBLANES, n), jnp.int32),
        compiler_params=_params("parallel"), name="slots",
    )(ri, pad_start_col)


def _slot(dest_ref, r, k):
    return dest_ref[k * ROW_TILE + r]


def _dispatch_kernel(dest_ref, last_ref, x_ref, xs_ref, scr_ref, zero_ref, sem, zsem):
    tm, d = x_ref.shape
    nch = _token_rows(d)
    tb = zero_ref.shape[0] // nch

    @pl.when(pl.program_id(0) == 0)
    def _():
        zero_ref[...] = jnp.zeros_like(zero_ref)

        def desc(tok):
            off = pl.multiple_of(jnp.maximum(tok, 0) * nch, nch)
            return pltpu.make_async_copy(zero_ref, xs_ref.at[pl.ds(off, tb * nch), :], zsem)

        def zstart(e, _):
            @pl.when(last_ref[e] >= 0)
            def _():
                desc(last_ref[e]).start()
            return 0

        def zwait(e, _):
            @pl.when(last_ref[e] >= 0)
            def _():
                desc(last_ref[e]).wait()
            return 0

        lax.fori_loop(0, MOE_EXPERTS, zstart, 0)
        nused = last_ref[MOE_EXPERTS]
        nblk = xs_ref.shape[0] // (tb * nch)
        lax.fori_loop(nused, nblk, lambda b, _: (desc(b * tb).start(), 0)[1], 0)
        lax.fori_loop(0, MOE_EXPERTS, zwait, 0)
        lax.fori_loop(nused, nblk, lambda b, _: (desc(b * tb).wait(), 0)[1], 0)

    step = pl.program_id(0)
    slot = step % 2
    scr = scr_ref.at[slot]
    _to_token_tiles(scr, x_ref[...])

    def start(r, _):
        for k in range(2):
            _token_copy(scr, r, xs_ref, _slot(dest_ref, r, k), nch, sem.at[slot]).start(priority=k)
        return 0

    def drain(which):
        def wait(r, _):
            for k in range(2):
                _token_copy(scr_ref.at[which], 0, xs_ref, 0, nch, sem.at[which]).wait()
            return 0
        lax.fori_loop(0, tm, wait, 0, unroll=8)

    lax.fori_loop(0, tm, start, 0, unroll=8)

    @pl.when(step > 0)
    def _():
        drain(1 - slot)

    @pl.when(step == pl.num_programs(0) - 1)
    def _():
        drain(slot)


def _dispatch(dest, last_blk, x1, n_rows):
    n, d = x1.shape
    tm = ROW_TILE
    nch = _token_rows(d)
    return pl.pallas_call(
        _dispatch_kernel, grid=(n // tm,),
        in_specs=[pl.BlockSpec((2 * tm,), lambda i: (i,), memory_space=pltpu.SMEM),
                  pl.BlockSpec(memory_space=pltpu.SMEM),
                  pl.BlockSpec((tm, d), lambda i: (i, 0))],
        out_specs=pl.BlockSpec(memory_space=pl.ANY),
        out_shape=jax.ShapeDtypeStruct((n_rows * nch, LANES), F32),
        scratch_shapes=[pltpu.VMEM((2, tm * nch, LANES), F32), pltpu.VMEM((EXPERT_TILE * nch, LANES), F32),
                        pltpu.SemaphoreType.DMA((2,)), pltpu.SemaphoreType.DMA(())],
        compiler_params=_params("arbitrary"), name="dispatch",
    )(dest, last_blk, x1)


def _expert_kernel(first_ref, count_ref, widx_ref, nused_ref, wg_ref, wu_ref, wd_ref, xs_ref, ys_ref,
                   wgb_ref, wub_ref, wdb_ref, xbuf_ref, ybuf_ref, in_sem, out_sem):
    del widx_ref
    e = pl.program_id(0)
    nused = nused_ref[0]
    d = wg_ref.shape[1]
    nch = _token_rows(d)
    rows = xbuf_ref.shape[1]
    tb = rows // nch

    def blk(ref, b):
        return ref.at[pl.ds(pl.multiple_of(b * rows, rows), rows), :]

    def in_copy(b, slot):
        return pltpu.make_async_copy(blk(xs_ref, b), xbuf_ref.at[slot], in_sem.at[slot])

    def out_copy(b, slot):
        return pltpu.make_async_copy(ybuf_ref.at[slot], blk(ys_ref, b), out_sem.at[slot])

    n_in = xbuf_ref.shape[0]

    @pl.when(e == 0)
    def _():
        for b0 in range(n_in - 1):
            @pl.when(b0 < nused)
            def _():
                in_copy(b0, b0).start()

    @pl.when(count_ref[e] > 0)
    def _():
        wgb_ref[...] = wg_ref[0].astype(BF16)
        wub_ref[...] = wu_ref[0].astype(BF16)
        wdb_ref[...] = wd_ref[0].astype(BF16)

    def body(b, _):
        slot = b % n_in
        oslot = b % 2
        in_copy(b, slot).wait()

        @pl.when(b + n_in - 1 < nused)
        def _():
            in_copy(b + n_in - 1, (b + n_in - 1) % n_in).start()

        @pl.when(b >= 2)
        def _():
            out_copy(b - 2, oslot).wait()

        xb = _from_token_tiles(xbuf_ref.at[slot], tb, d).astype(BF16)
        g = _dot(xb, wgb_ref[...])
        u = _dot(xb, wub_ref[...])
        hmid = g * jax.nn.sigmoid(g) * u
        _to_token_tiles(ybuf_ref.at[oslot], _dot(hmid.astype(BF16), wdb_ref[...]))
        out_copy(b, oslot).start()
        return 0

    lax.fori_loop(first_ref[e], first_ref[e] + count_ref[e], body, 0)

    @pl.when(e == pl.num_programs(0) - 1)
    def _():
        for back in (2, 1):
            @pl.when(nused >= back)
            def _():
                out_copy(nused - back, (nused - back) % 2).wait()


def _experts(first_blk, blk_count, w_idx, nused, xs, w_gate, w_up, w_down):
    n_exp, d, dff = w_gate.shape
    nch = _token_rows(d)
    rows = EXPERT_TILE * nch
    w_spec = lambda shape: pl.BlockSpec(shape, lambda e, fb, bc, wi, nu: (wi[e], 0, 0))
    any_spec = pl.BlockSpec(memory_space=pl.ANY)
    grid_spec = pltpu.PrefetchScalarGridSpec(
        num_scalar_prefetch=4, grid=(n_exp,),
        in_specs=[w_spec((1, d, dff)), w_spec((1, d, dff)), w_spec((1, dff, d)), any_spec],
        out_specs=any_spec,
        scratch_shapes=[pltpu.VMEM((d, dff), BF16), pltpu.VMEM((d, dff), BF16), pltpu.VMEM((dff, d), BF16),
                        pltpu.VMEM((EXPERT_IN_SLOTS, rows, LANES), F32), pltpu.VMEM((2, rows, LANES), F32),
                        pltpu.SemaphoreType.DMA((EXPERT_IN_SLOTS,)), pltpu.SemaphoreType.DMA((2,))],
    )
    return pl.pallas_call(
        _expert_kernel, grid_spec=grid_spec, out_shape=jax.ShapeDtypeStruct(xs.shape, F32),
        input_output_aliases={7: 0},
        compiler_params=_params("arbitrary"), name="experts",
    )(first_blk, blk_count, w_idx, nused, w_gate, w_up, w_down, xs)


def _combine_kernel(dest_ref, dest_next_ref, x1_ref, rw_ref, g_ref, b_ref, ys_ref, o_ref, buf_ref, sem):
    tm, d = x1_ref.shape
    nch = _token_rows(d)
    step = pl.program_id(0)
    slot = step % 2

    def gather(idx_ref, which):
        def start(r, _):
            for k in range(2):
                _token_copy(ys_ref, _slot(idx_ref, r, k), buf_ref.at[which, k], r, nch,
                            sem.at[which]).start(priority=k)
            return 0
        lax.fori_loop(0, tm, start, 0, unroll=8)

    @pl.when(step == 0)
    def _():
        gather(dest_ref, 0)

    @pl.when(step + 1 < pl.num_programs(0))
    def _():
        gather(dest_next_ref, 1 - slot)

    def wait(r, _):
        for k in range(2):
            _token_copy(ys_ref, 0, buf_ref.at[slot, k], 0, nch, sem.at[slot]).wait()
        return 0

    lax.fori_loop(0, tm, wait, 0, unroll=8)
    rw = rw_ref[...]
    y0 = _from_token_tiles(buf_ref.at[slot, 0], tm, d)
    y1 = _from_token_tiles(buf_ref.at[slot, 1], tm, d)
    ffn = rw[:, 0:1] * y0 + rw[:, 1:2] * y1
    o_ref[...] = _layer_norm(DEEPNORM_ALPHA * x1_ref[...] + ffn, g_ref[...], b_ref[...])


def _combine(dest, x1, rw, ln_g, ln_b, ys):
    n, d = x1.shape
    tm = ROW_TILE
    nch = _token_rows(d)
    last = n // tm - 1
    row = lambda w: pl.BlockSpec((tm, w), lambda i: (i, 0))
    return pl.pallas_call(
        _combine_kernel, grid=(n // tm,),
        in_specs=[pl.BlockSpec((2 * tm,), lambda i: (i,), memory_space=pltpu.SMEM),
                  pl.BlockSpec((2 * tm,), lambda i: (jnp.minimum(i + 1, last),), memory_space=pltpu.SMEM),
                  row(d), row(LANES), _full(ln_g.shape), _full(ln_b.shape),
                  pl.BlockSpec(memory_space=pl.ANY)],
        out_specs=row(d),
        out_shape=jax.ShapeDtypeStruct((n, d), F32),
        scratch_shapes=[pltpu.VMEM((2, 2, tm * nch, LANES), F32), pltpu.SemaphoreType.DMA((2,))],
        compiler_params=_params("arbitrary"), name="combine",
    )(dest, dest, x1, rw, ln_g, ln_b, ys)


def _rope_tables(seq):
    half = ATTN_HEAD_DIM // 2
    inv_freq = ROPE_THETA ** (-np.arange(half, dtype=np.float64) / half)
    ang = np.arange(seq, dtype=np.float64)[:, None] * inv_freq[None, :]
    cos = np.cos(ang)
    sin = np.sin(ang)
    cos_h = np.concatenate([cos, cos], axis=1)
    sin_h = np.concatenate([-sin, sin], axis=1)
    return (jnp.asarray(np.tile(cos_h, (1, ATTN_HEADS)), F32), jnp.asarray(np.tile(sin_h, (1, ATTN_HEADS)), F32))


def _pad_lanes(a, width=LANES):
    return jnp.pad(a, ((0, 0), (0, width - a.shape[1])))


def kernel(x, ln0_g, ln0_b, w_in, conv_w, conv_b, w_mq, w_mk, b_i, b_f, gn_g, skip, w_attn_up, w_mlstm_up, w_out,
           ln1_g, ln1_b, w_router_group, b_router_group, w_router_expert, b_router_expert, w_gate, w_up, w_down,
           ln2_g, ln2_b):
    batch, seq, d = x.shape
    n = batch * seq
    assert seq % ROW_TILE == 0 and ROW_TILE == MOBA_BLOCK and w_in.shape[0] == DEPTH
    x2 = x.reshape(n, d)
    vec = lambda a: a.reshape(1, -1).astype(F32)

    w = w_in[0]
    c_if = 3 * ATTN_WIDTH + 3 * MLSTM_WIDTH
    c_g = c_if + 2 * MLSTM_HEADS
    wqkv = w[:, :3 * ATTN_WIDTH].astype(BF16)
    wuvo = w[:, 3 * ATTN_WIDTH:c_if].astype(BF16)
    wift = w[:, c_if:c_g].T.astype(BF16)
    wg = w[:, c_g:].astype(BF16)
    cos, sin = _rope_tables(seq)

    q, k, v, kmean, u, vm, o, ift, ga, gm, xn = _inproj(
        x2, vec(ln0_g), vec(ln0_b), wqkv, wuvo, wift, wg, cos, sin, batch, seq)

    nb = seq // MOBA_BLOCK
    km = kmean.reshape(batch, nb, ATTN_HEADS, ATTN_HEAD_DIM).transpose(0, 2, 1, 3)
    ya = _moba(q, k, v, km).reshape(n, ATTN_WIDTH)

    b_if = jnp.concatenate([b_i[0], b_f[0]]).astype(F32)
    ym = _mlstm(u, vm, o, ift, conv_w[0], vec(conv_b[0]), w_mq[0].transpose(0, 2, 1).astype(BF16),
                w_mk[0].astype(BF16), b_if[:, None],
                gn_g[0].astype(F32)[:, None], skip[0].astype(F32)[:, None], batch, seq)

    w_r = _pad_lanes(jnp.concatenate([w_router_expert[0], w_router_group[0]], axis=1))
    w_r_hi = w_r.astype(BF16)
    w_r_lo = (w_r - w_r_hi.astype(F32)).astype(BF16)
    w_rc = jnp.concatenate([w_r_hi.T, w_r_lo.T], axis=0)
    b_r = _pad_lanes(jnp.concatenate([b_router_expert[0], b_router_group[0]])[None, :]).T
    x1, ri, rw, counts = _mix(
        xn, ya, ym, ga, gm, w_attn_up[0].astype(BF16), w_mlstm_up[0].astype(BF16),
        w_out[0].astype(BF16), vec(ln1_g[0]), vec(ln1_b[0]), w_rc, b_r)

    tb = EXPERT_TILE
    nblk = (2 * n) // tb + MOE_EXPERTS
    cnt = counts[:MOE_EXPERTS, 0].astype(jnp.int32)
    nblk_e = (cnt + tb - 1) // tb
    blk_end = jnp.cumsum(nblk_e)
    pad_start = (blk_end - nblk_e) * tb
    nused = blk_end[-1:]
    ids = jnp.arange(MOE_EXPERTS, dtype=jnp.int32)
    prev_used = jnp.max(jnp.where((ids[None, :] <= ids[:, None]) & (nblk_e[None, :] > 0), ids[None, :], -1), axis=1)
    first_used = jnp.min(jnp.where(nblk_e > 0, ids, MOE_EXPERTS - 1))
    w_idx = jnp.where(prev_used >= 0, prev_used, first_used).astype(jnp.int32)
    last_blk = jnp.where(nblk_e > 0, (blk_end - 1) * tb, -1)
    last_blk = jnp.concatenate([last_blk, nused]).astype(jnp.int32)
    dest = _slots(ri, pad_start.astype(F32)[:, None])
    dest = dest[:2].reshape(2, n // ROW_TILE, ROW_TILE).transpose(1, 0, 2).reshape(2 * n)

    xs = _dispatch(dest, last_blk, x1, nblk * tb)
    ys = _experts((blk_end - nblk_e).astype(jnp.int32), nblk_e.astype(jnp.int32), w_idx, nused.astype(jnp.int32),
                  xs, w_gate[0], w_up[0], w_down[0])
    out = _combine(dest, x1, rw, vec(ln2_g[0]), vec(ln2_b[0]), ys)
    return out.reshape(batch, seq, d)
```

```python
import functools
import math

import jax
import jax.numpy as jnp
import numpy as np
from jax import lax
from jax.experimental import pallas as pl
from jax.experimental.pallas import tpu as pltpu

F32 = jnp.float32
BF16 = jnp.bfloat16

ATTN_HEADS = 8
ATTN_HEAD_DIM = 64
ATTN_WIDTH = ATTN_HEADS * ATTN_HEAD_DIM
MOBA_BLOCK = 256
MOBA_TOPK = 3
ROPE_THETA = 10000.0
MLSTM_HEADS = 4
MLSTM_HEAD_DIM = 128
MLSTM_WIDTH = MLSTM_HEADS * MLSTM_HEAD_DIM
MLSTM_CONV = 4
MOE_GROUPS = 8
MOE_EXPERTS_PER_GROUP = 8
MOE_EXPERTS = MOE_GROUPS * MOE_EXPERTS_PER_GROUP
MOE_D_FF = 512
LN_EPS = 1e-5
GN_EPS = 1e-6
DEPTH = 1
DEEPNORM_ALPHA = (2 * DEPTH) ** 0.25

LANES = 128
SUBLANES = 8
ROW_TILE = 256
EXPERT_TILE = 256
EXPERT_IN_SLOTS = 4
INPROJ_CHAINS = 2
MLSTM_CHUNKS = 2
SLOT_TILE = 2048
MIX_CHAINS = 4
VMEM_LIMIT = 48 * 1024 * 1024
INPROJ_VMEM_LIMIT = 58 * 1024 * 1024
WEIGHT_CHUNK = 512
LOG2_E = math.log2(math.e)

NEG_INF = float("-inf")


def _params(*sem):
    return pltpu.CompilerParams(dimension_semantics=sem, vmem_limit_bytes=VMEM_LIMIT)


def _dot(a, b):
    return jnp.dot(a, b, preferred_element_type=F32)


def _dot_nt(a, b):
    return lax.dot_general(a, b, (((1,), (1,)), ((), ())), preferred_element_type=F32)


def _dot_tn(a, b):
    return lax.dot_general(a, b, (((0,), (0,)), ((), ())), preferred_element_type=F32)


def _split3(x):
    x1 = x.astype(BF16)
    r1 = x - x1.astype(F32)
    x2 = r1.astype(BF16)
    r2 = r1 - x2.astype(F32)
    return x1, x2, r2.astype(BF16)


def _layer_norm(x, g, b):
    mu = jnp.mean(x, axis=-1, keepdims=True)
    xc = x - mu
    var = jnp.mean(xc * xc, axis=-1, keepdims=True)
    return xc * lax.rsqrt(var + LN_EPS) * g + b


def _log_sigmoid(x):
    return jnp.minimum(x, 0.0) - jnp.log1p(jnp.exp(-jnp.abs(x)))


def _full(shape):
    nd = len(shape)
    return pl.BlockSpec(shape, lambda *_: (0,) * nd)


def _run_skewed(phases, chains, rows):
    states = [dict() for _ in range(chains)]
    for t in range(chains + len(phases) - 1):
        for c in range(chains):
            if 0 <= t - c < len(phases):
                phases[t - c](states[c], c, slice(c * rows, (c + 1) * rows))


def _loop_groups(count, body, group=4):
    def trip(g, _):
        for d in range(group):
            body(g * group + d)
        return 0

    lax.fori_loop(0, count // group, trip, 0)
    done = (count // group) * group
    size = group // 2
    while size >= 1:
        take = ((count - done) // size) * size
        @pl.when(take > 0)
        def _(done=done, size=size):
            for d in range(size):
                body(done + d)
        done = done + take
        size //= 2


def _inproj_kernel(x_ref, g_ref, b_ref, wt_ref, cos_ref, sin_ref,
                   q_ref, k_ref, v_ref, km_ref, u_ref, vm_ref, o_ref, ift_ref, ga_ref, gm_ref, xn_ref,
                   wqkv_ref, wuvo_ref, wift_ref, wg_ref):
    tm = ROW_TILE
    lane = lax.broadcasted_iota(jnp.int32, (tm, ATTN_WIDTH), 1)
    first_half = (lane % ATTN_HEAD_DIM) < (ATTN_HEAD_DIM // 2)

    @pl.when(pl.program_id(0) == 0)
    def _():
        def fill(dst_ref, row0):
            for c in range(dst_ref.shape[1] // WEIGHT_CHUNK):
                cols = slice(c * WEIGHT_CHUNK, (c + 1) * WEIGHT_CHUNK)
                rows = slice(row0 + c * WEIGHT_CHUNK, row0 + (c + 1) * WEIGHT_CHUNK)
                dst_ref[:, cols] = wt_ref[rows, :].T.astype(BF16)

        c_if = 3 * ATTN_WIDTH + 3 * MLSTM_WIDTH
        fill(wqkv_ref, 0)
        fill(wuvo_ref, 3 * ATTN_WIDTH)
        wift_ref[...] = wt_ref[c_if:c_if + 2 * MLSTM_HEADS, :].astype(BF16)
        fill(wg_ref, c_if + 2 * MLSTM_HEADS)

    def norm(st, c, rs):
        xn = _layer_norm(x_ref[rs, :], g_ref[...], b_ref[...])
        xn_ref[rs, :] = xn
        st["xb"] = xn.astype(BF16)

    def qkv_matmul(st, c, rs):
        st["zqkv"] = _dot(st["xb"], wqkv_ref[...])

    def attn_outputs(st, c, rs):
        zqkv = st.pop("zqkv")
        cos = cos_ref[rs, :]
        sin = sin_ref[rs, :]

        def rope(t):
            fwd = pltpu.roll(t, ATTN_WIDTH - ATTN_HEAD_DIM // 2, axis=1)
            bwd = pltpu.roll(t, ATTN_HEAD_DIM // 2, axis=1)
            return t * cos + jnp.where(first_half, fwd, bwd) * sin

        q = rope(zqkv[:, :ATTN_WIDTH]) * (ATTN_HEAD_DIM ** -0.5 * LOG2_E)
        k = rope(zqkv[:, ATTN_WIDTH:2 * ATTN_WIDTH])
        v = zqkv[:, 2 * ATTN_WIDTH:]
        km_ref[c] = jnp.mean(k, axis=0, keepdims=True)
        qt = q.T
        vt = v.T
        for h in range(ATTN_HEADS):
            sl = slice(h * ATTN_HEAD_DIM, (h + 1) * ATTN_HEAD_DIM)
            q_ref[0, h, :, rs] = qt[sl, :].astype(BF16)
            k_ref[0, h, rs, :] = k[:, sl].astype(BF16)
            v_ref[0, h, :, rs] = vt[sl, :].astype(BF16)

    def uvo_matmul(st, c, rs):
        st["zuvo"] = _dot(st["xb"], wuvo_ref[...])

    def mlstm_outputs(st, c, rs):
        zuvo = st.pop("zuvo")
        u_ref[rs, :] = zuvo[:, :MLSTM_WIDTH]
        vm_ref[:, rs] = zuvo[:, MLSTM_WIDTH:2 * MLSTM_WIDTH].T.astype(BF16)
        o_ref[:, rs] = zuvo[:, 2 * MLSTM_WIDTH:].T
        ift_ref[:, rs] = _dot_nt(wift_ref[...], st["xb"])

    def gate_matmul(st, c, rs):
        st["zg"] = _dot(st.pop("xb"), wg_ref[...])

    def gate_outputs(st, c, rs):
        zg = st.pop("zg")
        d = ga_ref.shape[1]
        ga_ref[rs, :] = jax.nn.sigmoid(zg[:, :d]).astype(BF16)
        gm_ref[rs, :] = jax.nn.sigmoid(zg[:, d:]).astype(BF16)

    _run_skewed((norm, qkv_matmul, attn_outputs, uvo_matmul, mlstm_outputs, gate_matmul, gate_outputs),
                x_ref.shape[0] // tm, tm)


def _inproj(x2, ln_g, ln_b, wt, cos, sin, batch, seq):
    n, d = x2.shape
    chains = INPROJ_CHAINS
    tm = chains * ROW_TILE
    assert seq % tm == 0
    nsb = seq // tm
    hd = ATTN_HEAD_DIM
    row = lambda w: pl.BlockSpec((tm, w), lambda i: (i, 0))
    col = lambda h: pl.BlockSpec((h, tm), lambda i: (0, i))
    head = pl.BlockSpec((1, ATTN_HEADS, tm, hd), lambda i: (i // nsb, 0, i % nsb, 0))
    head_t = pl.BlockSpec((1, ATTN_HEADS, hd, tm), lambda i: (i // nsb, 0, 0, i % nsb))
    tab = pl.BlockSpec((tm, ATTN_WIDTH), lambda i: (i % nsb, 0))
    head_shape = jax.ShapeDtypeStruct((batch, ATTN_HEADS, seq, hd), BF16)
    head_t_shape = jax.ShapeDtypeStruct((batch, ATTN_HEADS, hd, seq), BF16)
    out_shape = (
        head_t_shape, head_shape, head_t_shape,
        jax.ShapeDtypeStruct((n // ROW_TILE, 1, ATTN_WIDTH), F32),
        jax.ShapeDtypeStruct((n, MLSTM_WIDTH), F32),
        jax.ShapeDtypeStruct((MLSTM_WIDTH, n), BF16),
        jax.ShapeDtypeStruct((MLSTM_WIDTH, n), F32),
        jax.ShapeDtypeStruct((SUBLANES, n), F32),
        jax.ShapeDtypeStruct((n, d), BF16),
        jax.ShapeDtypeStruct((n, d), BF16),
        jax.ShapeDtypeStruct((n, d), F32),
    )
    out_specs = (
        head_t, head, head_t,
        pl.BlockSpec((chains, 1, ATTN_WIDTH), lambda i: (i, 0, 0)),
        row(MLSTM_WIDTH), col(MLSTM_WIDTH), col(MLSTM_WIDTH),
        col(SUBLANES),
        row(d), row(d), row(d),
    )
    wt_spec = pl.BlockSpec(wt.shape, lambda i: (0, 0), pipeline_mode=pl.Buffered(1))
    in_specs = [row(d), _full(ln_g.shape), _full(ln_b.shape), wt_spec, tab, tab]
    return pl.pallas_call(
        _inproj_kernel, grid=(n // tm,), in_specs=in_specs, out_specs=out_specs, out_shape=out_shape,
        scratch_shapes=[pltpu.VMEM((d, 3 * ATTN_WIDTH), BF16), pltpu.VMEM((d, 3 * MLSTM_WIDTH), BF16),
                        pltpu.VMEM((2 * MLSTM_HEADS, d), BF16), pltpu.VMEM((d, 2 * d), BF16)],
        compiler_params=pltpu.CompilerParams(dimension_semantics=("arbitrary",), vmem_limit_bytes=INPROJ_VMEM_LIMIT),
        name="inproj",
    )(x2, ln_g, ln_b, wt, cos, sin)


def _moba_kernel(qt_ref, k_ref, vt_ref, km_ref, o_ref, bias_ref, m_ref, l_ref, acc_ref, s_ref):
    i = pl.program_id(1)
    blk = MOBA_BLOCK
    hd = ATTN_HEAD_DIM
    heads = ATTN_HEADS
    nb = k_ref.shape[2] // blk
    blk_id = lax.broadcasted_iota(jnp.int32, (nb, blk), 0)
    key_pos = lax.broadcasted_iota(jnp.int32, (blk, blk), 0)
    qry_pos = lax.broadcasted_iota(jnp.int32, (blk, blk), 1)
    causal = key_pos <= qry_pos

    for h in range(heads):
        qt = qt_ref[0, h]
        km = km_ref[0, h]
        km_hi = km.astype(BF16)
        km_lo = (km - km_hi.astype(F32)).astype(BF16)
        gate = _dot(km_hi, qt) + _dot(km_lo, qt)
        gate = jnp.where(blk_id < i, gate, NEG_INF)
        for j in range(nb - 1):
            row = gate[j:j + 1, :]
            beats = (gate > row) | ((gate == row) & (blk_id < j))
            cnt = jnp.sum(jnp.where(beats, 1.0, 0.0), axis=0, keepdims=True)
            sel = (cnt < float(MOBA_TOPK)) & (row > NEG_INF)
            bias_ref[j * heads + h] = jnp.where(sel, 0.0, NEG_INF)
    for h in range(heads):
        bias_ref[i * heads + h] = jnp.zeros((1, blk), F32)

    def scores(h, j, own_block):
        qt = qt_ref[0, h]
        half = blk // 2
        m_tile = None
        for c in range(2):
            rows = slice(c * half, (c + 1) * half)
            s = _dot(k_ref[0, h, pl.ds(pl.multiple_of(j * blk + c * half, half), half), :], qt)
            if own_block:
                s = jnp.where(causal[rows], s, NEG_INF)
            s_ref[j * heads + h, rows, :] = s
            m_c = jnp.max(s, axis=0, keepdims=True)
            m_tile = m_c if m_tile is None else jnp.maximum(m_tile, m_c)
        return m_tile

    for h in range(heads):
        m_ref[h] = scores(h, i, True)

    def past_scores(j):
        for h in range(heads):
            m_ref[h] = jnp.maximum(m_ref[h], scores(h, j, False) + bias_ref[j * heads + h])

    _loop_groups(i, past_scores)

    l_ref[...] = jnp.zeros_like(l_ref)
    acc_ref[...] = jnp.zeros_like(acc_ref)

    def accumulate(j):
        off = pl.multiple_of(j * blk, blk)
        for h in range(heads):
            p = jnp.exp2(s_ref[j * heads + h] - (m_ref[h] - bias_ref[j * heads + h]))
            l_ref[h] += jnp.sum(p, axis=0, keepdims=True)
            acc_ref[h] += _dot(vt_ref[0, h, :, pl.ds(off, blk)], p.astype(BF16))

    _loop_groups(i + 1, accumulate)
    yt = acc_ref[...] / l_ref[...]
    o_ref[0] = yt.reshape(heads * hd, blk).T.astype(BF16)


def _moba(qt, k, vt, km):
    batch, heads, seq, hd = k.shape
    blk = MOBA_BLOCK
    nb = seq // blk
    return pl.pallas_call(
        _moba_kernel, grid=(batch, nb),
        in_specs=[
            pl.BlockSpec((1, heads, hd, blk), lambda b, i: (b, 0, 0, i)),
            pl.BlockSpec((1, heads, seq, hd), lambda b, i: (b, 0, 0, 0)),
            pl.BlockSpec((1, heads, hd, seq), lambda b, i: (b, 0, 0, 0)),
            pl.BlockSpec((1, heads, nb, hd), lambda b, i: (b, 0, 0, 0)),
        ],
        out_specs=pl.BlockSpec((1, blk, heads * hd), lambda b, i: (b, i, 0)),
        out_shape=jax.ShapeDtypeStruct((batch, seq, heads * hd), BF16),
        scratch_shapes=[pltpu.VMEM((nb * heads, 1, blk), F32), pltpu.VMEM((heads, 1, blk), F32),
                        pltpu.VMEM((heads, 1, blk), F32), pltpu.VMEM((heads, hd, blk), F32),
                        pltpu.VMEM((nb * heads, blk, blk), F32)],
        compiler_params=_params("parallel", "arbitrary"), name="moba",
    )(qt, k, vt, km)


def _mlstm_kernel(u_ref, vmt_ref, ot_ref, ift_ref, cw_ref, cb_ref, wqt_ref, wk_ref, bcol_ref,
                  gn_ref, skip_ref, y_ref, ext_ref, c_ref, n_ref, m_ref, yt_ref):
    @pl.when(pl.program_id(1) == 0)
    def _():
        ext_ref[0:SUBLANES, :] = jnp.zeros((SUBLANES, MLSTM_WIDTH), F32)
        c_ref[...] = jnp.zeros_like(c_ref)
        n_ref[...] = jnp.zeros_like(n_ref)
        m_ref[...] = jnp.zeros_like(m_ref)

    for c in range(u_ref.shape[0] // ROW_TILE):
        _mlstm_chunk(slice(c * ROW_TILE, (c + 1) * ROW_TILE), u_ref, vmt_ref, ot_ref, ift_ref, cw_ref, cb_ref,
                     wqt_ref, wk_ref, bcol_ref, gn_ref, skip_ref, y_ref, ext_ref, c_ref, n_ref, m_ref, yt_ref)


def _mlstm_chunk(rs, u_ref, vmt_ref, ot_ref, ift_ref, cw_ref, cb_ref, wqt_ref, wk_ref, bcol_ref,
                 gn_ref, skip_ref, y_ref, ext_ref, c_ref, n_ref, m_ref, yt_ref):
    tm = ROW_TILE
    hd = MLSTM_HEAD_DIM
    halo = SUBLANES
    u = u_ref[rs, :]
    ext_ref[halo:halo + tm, :] = u
    acc = jnp.broadcast_to(cb_ref[...], u.shape)
    for j in range(MLSTM_CONV):
        acc = acc + cw_ref[j:j + 1, :] * ext_ref[halo - (MLSTM_CONV - 1) + j:halo - (MLSTM_CONV - 1) + j + tm, :]
    ext_ref[0:halo, :] = u[tm - halo:, :]
    uc = acc * jax.nn.sigmoid(acc)

    gr = ift_ref[:, rs] + bcol_ref[...]
    rows = lax.broadcasted_iota(jnp.int32, (tm, tm), 0)
    cols = lax.broadcasted_iota(jnp.int32, (tm, tm), 1)
    causal_t = rows <= cols
    triu = jnp.where(causal_t, 1.0, 0.0).astype(BF16)
    r1, r2, r3 = _split3(_log_sigmoid(gr))
    bcum_r = _dot(r1, triu) + _dot(r2, triu) + _dot(r3, triu)
    key_rows = gr[:MLSTM_HEADS, :] - bcum_r[MLSTM_HEADS:, :]
    key_cols = jnp.concatenate([key_rows, jnp.zeros((LANES - MLSTM_HEADS, tm), F32)], axis=0).T

    uct = uc.T

    def decay_weights(st, h, hs):
        fl = MLSTM_HEADS + h
        b_row = bcum_r[fl:fl + 1, :]
        st["key_row"] = key_rows[h:h + 1, :]
        st["key_col"] = key_cols[:, h:h + 1]
        m_prev = m_ref[h][:, 0:1]
        dlog = jnp.where(causal_t, st["key_col"] + b_row, NEG_INF)
        inter = b_row + m_prev
        m_t = jnp.maximum(inter, jnp.max(dlog, axis=0, keepdims=True))
        st["w_intra"] = jnp.exp(dlog - m_t)
        st["w_inter"] = jnp.exp(inter - m_t)
        st["m_t"], st["m_prev"], st["b_end"] = m_t, m_prev, b_row[:, tm - 1:tm]

    def project(st, h, hs):
        st["qtb"] = _dot(wqt_ref[h], uct[hs, :].astype(BF16)).astype(BF16)
        st["k"] = _dot(uc[:, hs].astype(BF16), wk_ref[h]) * (hd ** -0.5)

    def scores(st, h, hs):
        st["s"] = _dot(st["k"].astype(BF16), st["qtb"]) * st.pop("w_intra")

    def readout(st, h, hs):
        qtb, s, w_inter, m_t = st.pop("qtb"), st.pop("s"), st.pop("w_inter"), st.pop("m_t")
        n_prev = n_ref[h]
        n_hi = n_prev.astype(BF16)
        n_lo = (n_prev - n_hi.astype(F32)).astype(BF16)
        qn = (_dot(n_hi, qtb) + _dot(n_lo, qtb))[0:1, :]
        num = w_inter * _dot(c_ref[h].astype(BF16), qtb) + _dot(vmt_ref[hs, rs], s.astype(BF16))
        den = w_inter * qn + jnp.sum(s, axis=0, keepdims=True)
        st["hh"] = num / jnp.maximum(jnp.abs(den), jnp.exp(-m_t))

    def update_state(st, h, hs):
        b_end, m_prev = st.pop("b_end"), st.pop("m_prev")
        m_new = jnp.maximum(b_end + m_prev, jnp.max(b_end + st.pop("key_row"), axis=1, keepdims=True))
        decay = jnp.exp(b_end + m_prev - m_new)
        kw = st.pop("k") * jnp.exp(b_end + st.pop("key_col") - m_new)
        n_prev = n_ref[h]
        c_ref[h] = decay * c_ref[h] + _dot(vmt_ref[hs, rs], kw.astype(BF16))
        n_ref[h] = decay * n_prev + jnp.broadcast_to(jnp.sum(kw, axis=0, keepdims=True), n_prev.shape)
        m_ref[h] = jnp.broadcast_to(m_new, (1, LANES))

    def gate_and_norm(st, h, hs):
        hh = jax.nn.sigmoid(ot_ref[hs, rs]) * st.pop("hh")
        mu = jnp.mean(hh, axis=0, keepdims=True)
        hc = hh - mu
        var = jnp.mean(hc * hc, axis=0, keepdims=True)
        yt_ref[hs, :] = hc * lax.rsqrt(var + GN_EPS) * gn_ref[hs, :] + skip_ref[hs, :] * uct[hs, :]

    _run_skewed((decay_weights, project, scores, readout, update_state, gate_and_norm), MLSTM_HEADS, hd)
    y_ref[rs, :] = yt_ref[...].T.astype(BF16)


def _mlstm(u, vmt, ot, ift, conv_w, conv_b, wqt, wk, bcol, gn_g, skip, batch, seq):
    n = u.shape[0]
    tm = MLSTM_CHUNKS * ROW_TILE
    assert seq % tm == 0
    nc = seq // tm
    row = lambda w: pl.BlockSpec((tm, w), lambda b, c: (b * nc + c, 0))
    col = lambda h: pl.BlockSpec((h, tm), lambda b, c: (0, b * nc + c))
    in_specs = [row(MLSTM_WIDTH), col(MLSTM_WIDTH), col(MLSTM_WIDTH), col(SUBLANES),
                _full(conv_w.shape), _full(conv_b.shape), _full(wqt.shape), _full(wk.shape),
                _full(bcol.shape), _full(gn_g.shape), _full(skip.shape)]
    return pl.pallas_call(
        _mlstm_kernel, grid=(batch, nc), in_specs=in_specs, out_specs=row(MLSTM_WIDTH),
        out_shape=jax.ShapeDtypeStruct((n, MLSTM_WIDTH), BF16),
        scratch_shapes=[pltpu.VMEM((SUBLANES + ROW_TILE, MLSTM_WIDTH), F32),
                        pltpu.VMEM((MLSTM_HEADS, MLSTM_HEAD_DIM, MLSTM_HEAD_DIM), F32),
                        pltpu.VMEM((MLSTM_HEADS, SUBLANES, MLSTM_HEAD_DIM), F32),
                        pltpu.VMEM((MLSTM_HEADS, 1, LANES), F32),
                        pltpu.VMEM((MLSTM_WIDTH, ROW_TILE), F32)],
        compiler_params=_params("parallel", "arbitrary"), name="mlstm",
    )(u, vmt, ot, ift, conv_w, conv_b, wqt, wk, bcol, gn_g, skip)


def _mix_kernel(xn_ref, ya_ref, ym_ref, ga_ref, gm_ref, wau_ref, wmu_ref, wout_ref,
                g1_ref, b1_ref, wrc_ref, br_ref,
                x1_ref, ri_ref, rw_ref, cnt_out_ref, cnt_ref):
    @pl.when(pl.program_id(0) == 0)
    def _():
        cnt_ref[...] = jnp.zeros_like(cnt_ref)

    tm = ROW_TILE
    sub = lax.broadcasted_iota(jnp.int32, (LANES, tm), 0).astype(F32)
    big = float(4 * LANES)

    def up_and_mix(st, c, rs):
        a_up = _dot(ya_ref[rs, :], wau_ref[...])
        m_up = _dot(ym_ref[rs, :], wmu_ref[...])
        mix = ga_ref[rs, :].astype(F32) * a_up + gm_ref[rs, :].astype(F32) * m_up
        st["mix"] = mix.astype(BF16)

    def out_and_norm(st, c, rs):
        x1 = _layer_norm(DEEPNORM_ALPHA * xn_ref[rs, :] + _dot(st.pop("mix"), wout_ref[...]), g1_ref[...], b1_ref[...])
        x1_ref[rs, :] = x1
        st["x1"] = x1

    def router_logits(st, c, rs):
        x1 = st.pop("x1")
        x_hi = x1.astype(BF16)
        x_lo = (x1 - x_hi.astype(F32)).astype(BF16)
        both = _dot_nt(wrc_ref[...], x_hi)
        st["logits"] = both[:LANES] + both[LANES:] + _dot_nt(wrc_ref[:LANES, :], x_lo) + br_ref[...]

    def route(st, c, rs):
        logits = st.pop("logits")
        is_g = (sub >= float(MOE_EXPERTS)) & (sub < float(MOE_EXPERTS + MOE_GROUPS))
        gl = jnp.where(is_g, logits, NEG_INF)
        ge = jnp.exp(gl - jnp.max(gl, axis=0, keepdims=True))
        gp = ge / jnp.sum(ge, axis=0, keepdims=True)
        g_w = jnp.max(gp, axis=0, keepdims=True)
        g_idx = jnp.min(jnp.where((gp == g_w) & is_g, sub - float(MOE_EXPERTS), big), axis=0, keepdims=True)
        lo = g_idx * float(MOE_EXPERTS_PER_GROUP)
        in_grp = (sub >= lo) & (sub < lo + float(MOE_EXPERTS_PER_GROUP))
        el = jnp.where(in_grp, logits, NEG_INF)
        v1 = jnp.max(el, axis=0, keepdims=True)
        i1 = jnp.min(jnp.where((el == v1) & in_grp, sub, big), axis=0, keepdims=True)
        el2 = jnp.where(sub == i1, NEG_INF, el)
        v2 = jnp.max(el2, axis=0, keepdims=True)
        i2 = jnp.min(jnp.where((el2 == v2) & in_grp & (sub != i1), sub, big), axis=0, keepdims=True)
        e2 = jnp.exp(v2 - v1)
        w0 = g_w / (1.0 + e2)
        w1 = g_w * e2 / (1.0 + e2)
        rw_ref[rs, :] = jnp.where(sub == 0.0, w0, jnp.where(sub == 1.0, w1, 0.0)).T
        st["i1"], st["i2"] = i1, i2

    def rank(st, c, rs):
        i1, i2 = st.pop("i1"), st.pop("i2")
        is1 = sub == i1
        is2 = sub == i2
        onehot = jnp.where(is1 | is2, 1.0, 0.0)
        rows = lax.broadcasted_iota(jnp.int32, (tm, tm), 0)
        cols = lax.broadcasted_iota(jnp.int32, (tm, tm), 1)
        earlier = jnp.where(rows < cols, 1.0, 0.0).astype(BF16)
        before = _dot(onehot.astype(BF16), earlier) + cnt_ref[...]
        r0 = jnp.sum(jnp.where(is1, before, 0.0), axis=0, keepdims=True)
        r1 = jnp.sum(jnp.where(is2, before, 0.0), axis=0, keepdims=True)
        total = cnt_ref[...] + jnp.sum(onehot, axis=1, keepdims=True)
        cnt_ref[...] = total
        cnt_out_ref[...] = total
        ri_t = jnp.where(sub == 0.0, i1, jnp.where(sub == 1.0, i2, jnp.where(sub == 2.0, r0, jnp.where(sub == 3.0, r1, 0.0))))
        ri_ref[:, rs] = ri_t[:SUBLANES, :].astype(jnp.int32)

    _run_skewed((up_and_mix, out_and_norm, router_logits, route, rank), xn_ref.shape[0] // tm, tm)


def _mix(xn, ya, ym, ga, gm, wau, wmu, wout, g1, b1, wrc, br):
    n, d = xn.shape
    tm = MIX_CHAINS * ROW_TILE
    row = lambda w: pl.BlockSpec((tm, w), lambda i: (i, 0))
    in_specs = [row(d), row(ATTN_WIDTH), row(MLSTM_WIDTH), row(d), row(d),
                _full(wau.shape), _full(wmu.shape), _full(wout.shape), _full(g1.shape), _full(b1.shape),
                _full(wrc.shape), _full(br.shape)]
    out_shape = (jax.ShapeDtypeStruct((n, d), F32), jax.ShapeDtypeStruct((SUBLANES, n), jnp.int32),
                 jax.ShapeDtypeStruct((n, LANES), F32), jax.ShapeDtypeStruct((LANES, 1), F32))
    out_specs = (row(d), pl.BlockSpec((SUBLANES, tm), lambda i: (0, i)), row(LANES), _full((LANES, 1)))
    return pl.pallas_call(
        _mix_kernel, grid=(n // tm,), in_specs=in_specs, out_specs=out_specs, out_shape=out_shape,
        scratch_shapes=[pltpu.VMEM((LANES, 1), F32)],
        compiler_params=_params("arbitrary"), name="mix",
    )(xn, ya, ym, ga, gm, wau, wmu, wout, g1, b1, wrc, br)


def _token_rows(d):
    return d // LANES


def _to_token_tiles(dst_ref, x):
    rows, d = x.shape
    nch = _token_rows(d)
    for c in range(nch):
        dst_ref[pl.ds(c, rows, stride=nch), :] = x[:, c * LANES:(c + 1) * LANES]


def _from_token_tiles(src_ref, rows, d):
    nch = _token_rows(d)
    return jnp.concatenate([src_ref[pl.ds(c, rows, stride=nch), :] for c in range(nch)], axis=1)


def _token_copy(src, src_tok, dst, dst_tok, nch, sem):
    s0 = pl.multiple_of(src_tok * nch, nch)
    d0 = pl.multiple_of(dst_tok * nch, nch)
    return pltpu.make_async_copy(src.at[pl.ds(s0, nch), :], dst.at[pl.ds(d0, nch), :], sem)


def _slots_kernel(ri_ref, ps_ref, o_ref):
    ri = ri_ref[...].astype(F32)
    ps = ps_ref[...]
    expert = lax.broadcasted_iota(jnp.int32, (ps.shape[0], ri.shape[1]), 0).astype(F32)
    row_id = lax.broadcasted_iota(jnp.int32, ri.shape, 0)
    out = jnp.zeros(ri.shape, F32)
    for k in range(2):
        start = jnp.sum(jnp.where(expert == ri[k:k + 1, :], jnp.broadcast_to(ps, expert.shape), 0.0),
                        axis=0, keepdims=True)
        out = jnp.where(row_id == k, start + ri[2 + k:3 + k, :], out)
    o_ref[...] = out.astype(jnp.int32)


def _slots(ri, pad_start_col):
    n = ri.shape[1]
    tm = SLOT_TILE
    blk = pl.BlockSpec((SUBLANES, tm), lambda i: (0, i))
    return pl.pallas_call(
        _slots_kernel, grid=(n // tm,), in_specs=[blk, _full(pad_start_col.shape)], out_specs=blk,
        out_shape=jax.ShapeDtypeStruct((SUBLANES, n), jnp.int32),
        compiler_params=_params("parallel"), name="slots",
    )(ri, pad_start_col)


def _slot(dest_ref, r, k):
    return dest_ref[k * ROW_TILE + r]


def _dispatch_kernel(dest_ref, last_ref, x_ref, xs_ref, scr_ref, zero_ref, sem, zsem):
    tm, d = x_ref.shape
    nch = _token_rows(d)
    tb = zero_ref.shape[0] // nch

    @pl.when(pl.program_id(0) == 0)
    def _():
        zero_ref[...] = jnp.zeros_like(zero_ref)

        def desc(tok):
            off = pl.multiple_of(jnp.maximum(tok, 0) * nch, nch)
            return pltpu.make_async_copy(zero_ref, xs_ref.at[pl.ds(off, tb * nch), :], zsem)

        def zstart(e, _):
            @pl.when(last_ref[e] >= 0)
            def _():
                desc(last_ref[e]).start()
            return 0

        def zwait(e, _):
            @pl.when(last_ref[e] >= 0)
            def _():
                desc(last_ref[e]).wait()
            return 0

        lax.fori_loop(0, MOE_EXPERTS, zstart, 0)
        nused = last_ref[MOE_EXPERTS]
        nblk = xs_ref.shape[0] // (tb * nch)
        lax.fori_loop(nused, nblk, lambda b, _: (desc(b * tb).start(), 0)[1], 0)
        lax.fori_loop(0, MOE_EXPERTS, zwait, 0)
        lax.fori_loop(nused, nblk, lambda b, _: (desc(b * tb).wait(), 0)[1], 0)

    step = pl.program_id(0)
    slot = step % 2
    scr = scr_ref.at[slot]
    _to_token_tiles(scr, x_ref[...])

    def start(r, _):
        for k in range(2):
            _token_copy(scr, r, xs_ref, _slot(dest_ref, r, k), nch, sem.at[slot]).start(priority=k)
        return 0

    def drain(which):
        def wait(r, _):
            for k in range(2):
                _token_copy(scr_ref.at[which], 0, xs_ref, 0, nch, sem.at[which]).wait()
            return 0
        lax.fori_loop(0, tm, wait, 0, unroll=8)

    lax.fori_loop(0, tm, start, 0, unroll=8)

    @pl.when(step > 0)
    def _():
        drain(1 - slot)

    @pl.when(step == pl.num_programs(0) - 1)
    def _():
        drain(slot)


def _dispatch(dest, last_blk, x1, n_rows):
    n, d = x1.shape
    tm = ROW_TILE
    nch = _token_rows(d)
    return pl.pallas_call(
        _dispatch_kernel, grid=(n // tm,),
        in_specs=[pl.BlockSpec((2 * tm,), lambda i: (i,), memory_space=pltpu.SMEM),
                  pl.BlockSpec(memory_space=pltpu.SMEM),
                  pl.BlockSpec((tm, d), lambda i: (i, 0))],
        out_specs=pl.BlockSpec(memory_space=pl.ANY),
        out_shape=jax.ShapeDtypeStruct((n_rows * nch, LANES), F32),
        scratch_shapes=[pltpu.VMEM((2, tm * nch, LANES), F32), pltpu.VMEM((EXPERT_TILE * nch, LANES), F32),
                        pltpu.SemaphoreType.DMA((2,)), pltpu.SemaphoreType.DMA(())],
        compiler_params=_params("arbitrary"), name="dispatch",
    )(dest, last_blk, x1)


def _expert_kernel(first_ref, count_ref, widx_ref, nused_ref, wg_ref, wu_ref, wd_ref, xs_ref, ys_ref,
                   wgb_ref, wub_ref, wdb_ref, xbuf_ref, ybuf_ref, in_sem, out_sem):
    del widx_ref
    e = pl.program_id(0)
    nused = nused_ref[0]
    d = wg_ref.shape[1]
    nch = _token_rows(d)
    rows = xbuf_ref.shape[1]
    tb = rows // nch

    def blk(ref, b):
        return ref.at[pl.ds(pl.multiple_of(b * rows, rows), rows), :]

    def in_copy(b, slot):
        return pltpu.make_async_copy(blk(xs_ref, b), xbuf_ref.at[slot], in_sem.at[slot])

    def out_copy(b, slot):
        return pltpu.make_async_copy(ybuf_ref.at[slot], blk(ys_ref, b), out_sem.at[slot])

    n_in = xbuf_ref.shape[0]

    @pl.when(e == 0)
    def _():
        for b0 in range(n_in - 1):
            @pl.when(b0 < nused)
            def _():
                in_copy(b0, b0).start()

    @pl.when(count_ref[e] > 0)
    def _():
        wgb_ref[...] = wg_ref[0].astype(BF16)
        wub_ref[...] = wu_ref[0].astype(BF16)
        wdb_ref[...] = wd_ref[0].astype(BF16)

    def body(b, _):
        slot = b % n_in
        oslot = b % 2
        in_copy(b, slot).wait()

        @pl.when(b + n_in - 1 < nused)
        def _():
            in_copy(b + n_in - 1, (b + n_in - 1) % n_in).start()

        @pl.when(b >= 2)
        def _():
            out_copy(b - 2, oslot).wait()

        xb = _from_token_tiles(xbuf_ref.at[slot], tb, d).astype(BF16)
        g = _dot(xb, wgb_ref[...])
        u = _dot(xb, wub_ref[...])
        hmid = g * jax.nn.sigmoid(g) * u
        _to_token_tiles(ybuf_ref.at[oslot], _dot(hmid.astype(BF16), wdb_ref[...]))
        out_copy(b, oslot).start()
        return 0

    lax.fori_loop(first_ref[e], first_ref[e] + count_ref[e], body, 0)

    @pl.when(e == pl.num_programs(0) - 1)
    def _():
        for back in (2, 1):
            @pl.when(nused >= back)
            def _():
                out_copy(nused - back, (nused - back) % 2).wait()


def _experts(first_blk, blk_count, w_idx, nused, xs, w_gate, w_up, w_down):
    n_exp, d, dff = w_gate.shape
    nch = _token_rows(d)
    rows = EXPERT_TILE * nch
    w_spec = lambda shape: pl.BlockSpec(shape, lambda e, fb, bc, wi, nu: (wi[e], 0, 0))
    any_spec = pl.BlockSpec(memory_space=pl.ANY)
    grid_spec = pltpu.PrefetchScalarGridSpec(
        num_scalar_prefetch=4, grid=(n_exp,),
        in_specs=[w_spec((1, d, dff)), w_spec((1, d, dff)), w_spec((1, dff, d)), any_spec],
        out_specs=any_spec,
        scratch_shapes=[pltpu.VMEM((d, dff), BF16), pltpu.VMEM((d, dff), BF16), pltpu.VMEM((dff, d), BF16),
                        pltpu.VMEM((EXPERT_IN_SLOTS, rows, LANES), F32), pltpu.VMEM((2, rows, LANES), F32),
                        pltpu.SemaphoreType.DMA((EXPERT_IN_SLOTS,)), pltpu.SemaphoreType.DMA((2,))],
    )
    return pl.pallas_call(
        _expert_kernel, grid_spec=grid_spec, out_shape=jax.ShapeDtypeStruct(xs.shape, F32),
        input_output_aliases={7: 0},
        compiler_params=_params("arbitrary"), name="experts",
    )(first_blk, blk_count, w_idx, nused, w_gate, w_up, w_down, xs)


def _combine_kernel(dest_ref, dest_next_ref, x1_ref, rw_ref, g_ref, b_ref, ys_ref, o_ref, buf_ref, sem):
    tm, d = x1_ref.shape
    nch = _token_rows(d)
    step = pl.program_id(0)
    slot = step % 2

    def gather(idx_ref, which):
        def start(r, _):
            for k in range(2):
                _token_copy(ys_ref, _slot(idx_ref, r, k), buf_ref.at[which, k], r, nch,
                            sem.at[which]).start(priority=k)
            return 0
        lax.fori_loop(0, tm, start, 0, unroll=8)

    @pl.when(step == 0)
    def _():
        gather(dest_ref, 0)

    @pl.when(step + 1 < pl.num_programs(0))
    def _():
        gather(dest_next_ref, 1 - slot)

    def wait(r, _):
        for k in range(2):
            _token_copy(ys_ref, 0, buf_ref.at[slot, k], 0, nch, sem.at[slot]).wait()
        return 0

    lax.fori_loop(0, tm, wait, 0, unroll=8)
    rw = rw_ref[...]
    y0 = _from_token_tiles(buf_ref.at[slot, 0], tm, d)
    y1 = _from_token_tiles(buf_ref.at[slot, 1], tm, d)
    ffn = rw[:, 0:1] * y0 + rw[:, 1:2] * y1
    o_ref[...] = _layer_norm(DEEPNORM_ALPHA * x1_ref[...] + ffn, g_ref[...], b_ref[...])


def _combine(dest, x1, rw, ln_g, ln_b, ys):
    n, d = x1.shape
    tm = ROW_TILE
    nch = _token_rows(d)
    last = n // tm - 1
    row = lambda w: pl.BlockSpec((tm, w), lambda i: (i, 0))
    return pl.pallas_call(
        _combine_kernel, grid=(n // tm,),
        in_specs=[pl.BlockSpec((2 * tm,), lambda i: (i,), memory_space=pltpu.SMEM),
                  pl.BlockSpec((2 * tm,), lambda i: (jnp.minimum(i + 1, last),), memory_space=pltpu.SMEM),
                  row(d), row(LANES), _full(ln_g.shape), _full(ln_b.shape),
                  pl.BlockSpec(memory_space=pl.ANY)],
        out_specs=row(d),
        out_shape=jax.ShapeDtypeStruct((n, d), F32),
        scratch_shapes=[pltpu.VMEM((2, 2, tm * nch, LANES), F32), pltpu.SemaphoreType.DMA((2,))],
        compiler_params=_params("arbitrary"), name="combine",
    )(dest, dest, x1, rw, ln_g, ln_b, ys)


def _rope_tables(seq):
    half = ATTN_HEAD_DIM // 2
    inv_freq = ROPE_THETA ** (-np.arange(half, dtype=np.float64) / half)
    ang = np.arange(seq, dtype=np.float64)[:, None] * inv_freq[None, :]
    cos = np.cos(ang)
    sin = np.sin(ang)
    cos_h = np.concatenate([cos, cos], axis=1)
    sin_h = np.concatenate([-sin, sin], axis=1)
    return (jnp.asarray(np.tile(cos_h, (1, ATTN_HEADS)), F32), jnp.asarray(np.tile(sin_h, (1, ATTN_HEADS)), F32))


def _pad_lanes(a, width=LANES):
    return jnp.pad(a, ((0, 0), (0, width - a.shape[1])))


def kernel(x, ln0_g, ln0_b, w_in, conv_w, conv_b, w_mq, w_mk, b_i, b_f, gn_g, skip, w_attn_up, w_mlstm_up, w_out,
           ln1_g, ln1_b, w_router_group, b_router_group, w_router_expert, b_router_expert, w_gate, w_up, w_down,
           ln2_g, ln2_b):
    batch, seq, d = x.shape
    n = batch * seq
    assert seq % ROW_TILE == 0 and ROW_TILE == MOBA_BLOCK and w_in.shape[0] == DEPTH
    x2 = x.reshape(n, d)
    vec = lambda a: a.reshape(1, -1).astype(F32)

    wt = w_in[0].T
    cos, sin = _rope_tables(seq)

    q, k, v, kmean, u, vm, o, ift, ga, gm, xn = _inproj(
        x2, vec(ln0_g), vec(ln0_b), wt, cos, sin, batch, seq)

    nb = seq // MOBA_BLOCK
    km = kmean.reshape(batch, nb, ATTN_HEADS, ATTN_HEAD_DIM).transpose(0, 2, 1, 3)
    ya = _moba(q, k, v, km).reshape(n, ATTN_WIDTH)

    b_if = jnp.concatenate([b_i[0], b_f[0]]).astype(F32)
    ym = _mlstm(u, vm, o, ift, conv_w[0], vec(conv_b[0]), w_mq[0].transpose(0, 2, 1).astype(BF16),
                w_mk[0].astype(BF16), b_if[:, None],
                gn_g[0].astype(F32)[:, None], skip[0].astype(F32)[:, None], batch, seq)

    w_r = _pad_lanes(jnp.concatenate([w_router_expert[0], w_router_group[0]], axis=1))
    w_r_hi = w_r.astype(BF16)
    w_r_lo = (w_r - w_r_hi.astype(F32)).astype(BF16)
    w_rc = jnp.concatenate([w_r_hi.T, w_r_lo.T], axis=0)
    b_r = _pad_lanes(jnp.concatenate([b_router_expert[0], b_router_group[0]])[None, :]).T
    x1, ri, rw, counts = _mix(
        xn, ya, ym, ga, gm, w_attn_up[0].astype(BF16), w_mlstm_up[0].astype(BF16),
        w_out[0].astype(BF16), vec(ln1_g[0]), vec(ln1_b[0]), w_rc, b_r)

    tb = EXPERT_TILE
    nblk = (2 * n) // tb + MOE_EXPERTS
    cnt = counts[:MOE_EXPERTS, 0].astype(jnp.int32)
    nblk_e = (cnt + tb - 1) // tb
    blk_end = jnp.cumsum(nblk_e)
    pad_start = (blk_end - nblk_e) * tb
    nused = blk_end[-1:]
    ids = jnp.arange(MOE_EXPERTS, dtype=jnp.int32)
    prev_used = jnp.max(jnp.where((ids[None, :] <= ids[:, None]) & (nblk_e[None, :] > 0), ids[None, :], -1), axis=1)
    first_used = jnp.min(jnp.where(nblk_e > 0, ids, MOE_EXPERTS - 1))
    w_idx = jnp.where(prev_used >= 0, prev_used, first_used).astype(jnp.int32)
    last_blk = jnp.where(nblk_e > 0, (blk_end - 1) * tb, -1)
    last_blk = jnp.concatenate([last_blk, nused]).astype(jnp.int32)
    dest = _slots(ri, pad_start.astype(F32)[:, None])
    dest = dest[:2].reshape(2, n // ROW_TILE, ROW_TILE).transpose(1, 0, 2).reshape(2 * n)

    xs = _dispatch(dest, last_blk, x1, nblk * tb)
    ys = _experts((blk_end - nblk_e).astype(jnp.int32), nblk_e.astype(jnp.int32), w_idx, nused.astype(jnp.int32),
                  xs, w_gate[0], w_up[0], w_down[0])
    out = _combine(dest, x1, rw, vec(ln2_g[0]), vec(ln2_b[0]), ys)
    return out.reshape(batch, seq, d)
```

```python
import functools
import math

import jax
import jax.numpy as jnp
import numpy as np
from jax import lax
from jax.experimental import pallas as pl
from jax.experimental.pallas import tpu as pltpu

F32 = jnp.float32
BF16 = jnp.bfloat16

ATTN_HEADS = 8
ATTN_HEAD_DIM = 64
ATTN_WIDTH = ATTN_HEADS * ATTN_HEAD_DIM
MOBA_BLOCK = 256
MOBA_TOPK = 3
ROPE_THETA = 10000.0
MLSTM_HEADS = 4
MLSTM_HEAD_DIM = 128
MLSTM_WIDTH = MLSTM_HEADS * MLSTM_HEAD_DIM
MLSTM_CONV = 4
MOE_GROUPS = 8
MOE_EXPERTS_PER_GROUP = 8
MOE_EXPERTS = MOE_GROUPS * MOE_EXPERTS_PER_GROUP
MOE_D_FF = 512
LN_EPS = 1e-5
GN_EPS = 1e-6
DEPTH = 1
DEEPNORM_ALPHA = (2 * DEPTH) ** 0.25

LANES = 128
SUBLANES = 8
ROW_TILE = 256
EXPERT_TILE = 256
EXPERT_IN_SLOTS = 4
INPROJ_CHAINS = 2
MLSTM_CHUNKS = 2
SLOT_TILE = 2048
MIX_CHAINS = 4
VMEM_LIMIT = 48 * 1024 * 1024
INPROJ_VMEM_LIMIT = 58 * 1024 * 1024
WEIGHT_CHUNK = 512
LOG2_E = math.log2(math.e)

NEG_INF = float("-inf")


def _params(*sem):
    return pltpu.CompilerParams(dimension_semantics=sem, vmem_limit_bytes=VMEM_LIMIT)


def _dot(a, b):
    return jnp.dot(a, b, preferred_element_type=F32)


def _dot_nt(a, b):
    return lax.dot_general(a, b, (((1,), (1,)), ((), ())), preferred_element_type=F32)


def _dot_tn(a, b):
    return lax.dot_general(a, b, (((0,), (0,)), ((), ())), preferred_element_type=F32)


def _split3(x):
    x1 = x.astype(BF16)
    r1 = x - x1.astype(F32)
    x2 = r1.astype(BF16)
    r2 = r1 - x2.astype(F32)
    return x1, x2, r2.astype(BF16)


def _layer_norm(x, g, b):
    mu = jnp.mean(x, axis=-1, keepdims=True)
    xc = x - mu
    var = jnp.mean(xc * xc, axis=-1, keepdims=True)
    return xc * lax.rsqrt(var + LN_EPS) * g + b


def _log_sigmoid(x):
    return jnp.minimum(x, 0.0) - jnp.log1p(jnp.exp(-jnp.abs(x)))


def _full(shape):
    nd = len(shape)
    return pl.BlockSpec(shape, lambda *_: (0,) * nd)


def _run_skewed(phases, chains, rows):
    states = [dict() for _ in range(chains)]
    for t in range(chains + len(phases) - 1):
        for c in range(chains):
            if 0 <= t - c < len(phases):
                phases[t - c](states[c], c, slice(c * rows, (c + 1) * rows))


def _loop_groups(count, body, group=4):
    def trip(g, _):
        for d in range(group):
            body(g * group + d)
        return 0

    lax.fori_loop(0, count // group, trip, 0)
    done = (count // group) * group
    size = group // 2
    while size >= 1:
        take = ((count - done) // size) * size
        @pl.when(take > 0)
        def _(done=done, size=size):
            for d in range(size):
                body(done + d)
        done = done + take
        size //= 2


def _inproj_kernel(x_ref, g_ref, b_ref, wt_ref, cos_ref, sin_ref,
                   q_ref, k_ref, v_ref, km_ref, u_ref, vm_ref, o_ref, ift_ref, ga_ref, gm_ref, xn_ref,
                   wqkv_ref, wuvo_ref, wift_ref, wg_ref):
    tm = ROW_TILE
    lane = lax.broadcasted_iota(jnp.int32, (tm, ATTN_WIDTH), 1)
    first_half = (lane % ATTN_HEAD_DIM) < (ATTN_HEAD_DIM // 2)

    @pl.when(pl.program_id(0) == 0)
    def _():
        def fill(dst_ref, row0):
            for c in range(dst_ref.shape[1] // WEIGHT_CHUNK):
                cols = slice(c * WEIGHT_CHUNK, (c + 1) * WEIGHT_CHUNK)
                rows = slice(row0 + c * WEIGHT_CHUNK, row0 + (c + 1) * WEIGHT_CHUNK)
                dst_ref[:, cols] = wt_ref[rows, :].T.astype(BF16)

        c_if = 3 * ATTN_WIDTH + 3 * MLSTM_WIDTH
        fill(wqkv_ref, 0)
        fill(wuvo_ref, 3 * ATTN_WIDTH)
        wift_ref[...] = wt_ref[c_if:c_if + 2 * MLSTM_HEADS, :].astype(BF16)
        fill(wg_ref, c_if + 2 * MLSTM_HEADS)

    def norm(st, c, rs):
        xn = _layer_norm(x_ref[rs, :], g_ref[...], b_ref[...])
        xn_ref[rs, :] = xn
        st["xb"] = xn.astype(BF16)

    def qkv_matmul(st, c, rs):
        st["zqkv"] = _dot(st["xb"], wqkv_ref[...])

    def attn_outputs(st, c, rs):
        zqkv = st.pop("zqkv")
        cos = cos_ref[rs, :]
        sin = sin_ref[rs, :]

        def rope(t):
            fwd = pltpu.roll(t, ATTN_WIDTH - ATTN_HEAD_DIM // 2, axis=1)
            bwd = pltpu.roll(t, ATTN_HEAD_DIM // 2, axis=1)
            return t * cos + jnp.where(first_half, fwd, bwd) * sin

        q = rope(zqkv[:, :ATTN_WIDTH]) * (ATTN_HEAD_DIM ** -0.5 * LOG2_E)
        k = rope(zqkv[:, ATTN_WIDTH:2 * ATTN_WIDTH])
        v = zqkv[:, 2 * ATTN_WIDTH:]
        km_ref[c] = jnp.mean(k, axis=0, keepdims=True)
        qt = q.T
        vt = v.T
        for h in range(ATTN_HEADS):
            sl = slice(h * ATTN_HEAD_DIM, (h + 1) * ATTN_HEAD_DIM)
            q_ref[0, h, :, rs] = qt[sl, :].astype(BF16)
            k_ref[0, h, rs, :] = k[:, sl].astype(BF16)
            v_ref[0, h, :, rs] = vt[sl, :].astype(BF16)

    def uvo_matmul(st, c, rs):
        st["zuvo"] = _dot(st["xb"], wuvo_ref[...])

    def mlstm_outputs(st, c, rs):
        zuvo = st.pop("zuvo")
        u_ref[rs, :] = zuvo[:, :MLSTM_WIDTH]
        vm_ref[:, rs] = zuvo[:, MLSTM_WIDTH:2 * MLSTM_WIDTH].T.astype(BF16)
        o_ref[:, rs] = zuvo[:, 2 * MLSTM_WIDTH:].T
        ift_ref[:, rs] = _dot_nt(wift_ref[...], st["xb"])

    def gate_matmul(st, c, rs):
        st["zg"] = _dot(st.pop("xb"), wg_ref[...])

    def gate_outputs(st, c, rs):
        zg = st.pop("zg")
        d = ga_ref.shape[1]
        ga_ref[rs, :] = jax.nn.sigmoid(zg[:, :d]).astype(BF16)
        gm_ref[rs, :] = jax.nn.sigmoid(zg[:, d:]).astype(BF16)

    _run_skewed((norm, qkv_matmul, attn_outputs, uvo_matmul, mlstm_outputs, gate_matmul, gate_outputs),
                x_ref.shape[0] // tm, tm)


def _inproj(x2, ln_g, ln_b, wt, cos, sin, batch, seq):
    n, d = x2.shape
    chains = INPROJ_CHAINS
    tm = chains * ROW_TILE
    assert seq % tm == 0
    nsb = seq // tm
    hd = ATTN_HEAD_DIM
    row = lambda w: pl.BlockSpec((tm, w), lambda i: (i, 0))
    col = lambda h: pl.BlockSpec((h, tm), lambda i: (0, i))
    head = pl.BlockSpec((1, ATTN_HEADS, tm, hd), lambda i: (i // nsb, 0, i % nsb, 0))
    head_t = pl.BlockSpec((1, ATTN_HEADS, hd, tm), lambda i: (i // nsb, 0, 0, i % nsb))
    tab = pl.BlockSpec((tm, ATTN_WIDTH), lambda i: (i % nsb, 0))
    head_shape = jax.ShapeDtypeStruct((batch, ATTN_HEADS, seq, hd), BF16)
    head_t_shape = jax.ShapeDtypeStruct((batch, ATTN_HEADS, hd, seq), BF16)
    out_shape = (
        head_t_shape, head_shape, head_t_shape,
        jax.ShapeDtypeStruct((n // ROW_TILE, 1, ATTN_WIDTH), F32),
        jax.ShapeDtypeStruct((n, MLSTM_WIDTH), F32),
        jax.ShapeDtypeStruct((MLSTM_WIDTH, n), BF16),
        jax.ShapeDtypeStruct((MLSTM_WIDTH, n), F32),
        jax.ShapeDtypeStruct((SUBLANES, n), F32),
        jax.ShapeDtypeStruct((n, d), BF16),
        jax.ShapeDtypeStruct((n, d), BF16),
        jax.ShapeDtypeStruct((n, d), F32),
    )
    out_specs = (
        head_t, head, head_t,
        pl.BlockSpec((chains, 1, ATTN_WIDTH), lambda i: (i, 0, 0)),
        row(MLSTM_WIDTH), col(MLSTM_WIDTH), col(MLSTM_WIDTH),
        col(SUBLANES),
        row(d), row(d), row(d),
    )
    wt_spec = pl.BlockSpec(wt.shape, lambda i: (0, 0), pipeline_mode=pl.Buffered(1))
    in_specs = [row(d), _full(ln_g.shape), _full(ln_b.shape), wt_spec, tab, tab]
    return pl.pallas_call(
        _inproj_kernel, grid=(n // tm,), in_specs=in_specs, out_specs=out_specs, out_shape=out_shape,
        scratch_shapes=[pltpu.VMEM((d, 3 * ATTN_WIDTH), BF16), pltpu.VMEM((d, 3 * MLSTM_WIDTH), BF16),
                        pltpu.VMEM((2 * MLSTM_HEADS, d), BF16), pltpu.VMEM((d, 2 * d), BF16)],
        compiler_params=pltpu.CompilerParams(dimension_semantics=("arbitrary",), vmem_limit_bytes=INPROJ_VMEM_LIMIT),
        name="inproj",
    )(x2, ln_g, ln_b, wt, cos, sin)


def _moba_kernel(qt_ref, k_ref, vt_ref, km_ref, o_ref, bias_ref, m_ref, l_ref, acc_ref, s_ref):
    i = pl.program_id(1)
    blk = MOBA_BLOCK
    hd = ATTN_HEAD_DIM
    heads = ATTN_HEADS
    nb = k_ref.shape[2] // blk
    blk_id = lax.broadcasted_iota(jnp.int32, (nb, blk), 0)
    key_pos = lax.broadcasted_iota(jnp.int32, (blk, blk), 0)
    qry_pos = lax.broadcasted_iota(jnp.int32, (blk, blk), 1)
    causal = key_pos <= qry_pos

    for h in range(heads):
        qt = qt_ref[0, h]
        km = km_ref[0, h]
        km_hi = km.astype(BF16)
        km_lo = (km - km_hi.astype(F32)).astype(BF16)
        gate = _dot(km_hi, qt) + _dot(km_lo, qt)
        gate = jnp.where(blk_id < i, gate, NEG_INF)
        for j in range(nb - 1):
            row = gate[j:j + 1, :]
            beats = (gate > row) | ((gate == row) & (blk_id < j))
            cnt = jnp.sum(jnp.where(beats, 1.0, 0.0), axis=0, keepdims=True)
            sel = (cnt < float(MOBA_TOPK)) & (row > NEG_INF)
            bias_ref[j * heads + h] = jnp.where(sel, 0.0, NEG_INF)
    for h in range(heads):
        bias_ref[i * heads + h] = jnp.zeros((1, blk), F32)

    def scores(h, j, own_block):
        qt = qt_ref[0, h]
        half = blk // 2
        m_tile = None
        for c in range(2):
            rows = slice(c * half, (c + 1) * half)
            s = _dot(k_ref[0, h, pl.ds(pl.multiple_of(j * blk + c * half, half), half), :], qt)
            if own_block:
                s = jnp.where(causal[rows], s, NEG_INF)
            s_ref[j * heads + h, rows, :] = s
            m_c = jnp.max(s, axis=0, keepdims=True)
            m_tile = m_c if m_tile is None else jnp.maximum(m_tile, m_c)
        return m_tile

    for h in range(heads):
        m_ref[h] = scores(h, i, True)

    def past_scores(j):
        for h in range(heads):
            m_ref[h] = jnp.maximum(m_ref[h], scores(h, j, False) + bias_ref[j * heads + h])

    _loop_groups(i, past_scores)

    l_ref[...] = jnp.zeros_like(l_ref)
    acc_ref[...] = jnp.zeros_like(acc_ref)

    def accumulate(j):
        off = pl.multiple_of(j * blk, blk)
        for h in range(heads):
            p = jnp.exp2(s_ref[j * heads + h] - (m_ref[h] - bias_ref[j * heads + h]))
            l_ref[h] += jnp.sum(p, axis=0, keepdims=True)
            acc_ref[h] += _dot(vt_ref[0, h, :, pl.ds(off, blk)], p.astype(BF16))

    _loop_groups(i + 1, accumulate)
    yt = acc_ref[...] / l_ref[...]
    o_ref[0] = yt.reshape(heads * hd, blk).T.astype(BF16)


def _moba(qt, k, vt, km):
    batch, heads, seq, hd = k.shape
    blk = MOBA_BLOCK
    nb = seq // blk
    return pl.pallas_call(
        _moba_kernel, grid=(batch, nb),
        in_specs=[
            pl.BlockSpec((1, heads, hd, blk), lambda b, i: (b, 0, 0, i)),
            pl.BlockSpec((1, heads, seq, hd), lambda b, i: (b, 0, 0, 0)),
            pl.BlockSpec((1, heads, hd, seq), lambda b, i: (b, 0, 0, 0)),
            pl.BlockSpec((1, heads, nb, hd), lambda b, i: (b, 0, 0, 0)),
        ],
        out_specs=pl.BlockSpec((1, blk, heads * hd), lambda b, i: (b, i, 0)),
        out_shape=jax.ShapeDtypeStruct((batch, seq, heads * hd), BF16),
        scratch_shapes=[pltpu.VMEM((nb * heads, 1, blk), F32), pltpu.VMEM((heads, 1, blk), F32),
                        pltpu.VMEM((heads, 1, blk), F32), pltpu.VMEM((heads, hd, blk), F32),
                        pltpu.VMEM((nb * heads, blk, blk), F32)],
        compiler_params=_params("parallel", "arbitrary"), name="moba",
    )(qt, k, vt, km)


def _mlstm_kernel(u_ref, vmt_ref, ot_ref, ift_ref, cw_ref, cb_ref, wqt_ref, wk_ref, bcol_ref,
                  gn_ref, skip_ref, y_ref, ext_ref, c_ref, n_ref, m_ref, yt_ref):
    @pl.when(pl.program_id(1) == 0)
    def _():
        ext_ref[0:SUBLANES, :] = jnp.zeros((SUBLANES, MLSTM_WIDTH), F32)
        c_ref[...] = jnp.zeros_like(c_ref)
        n_ref[...] = jnp.zeros_like(n_ref)
        m_ref[...] = jnp.zeros_like(m_ref)

    for c in range(u_ref.shape[0] // ROW_TILE):
        _mlstm_chunk(slice(c * ROW_TILE, (c + 1) * ROW_TILE), u_ref, vmt_ref, ot_ref, ift_ref, cw_ref, cb_ref,
                     wqt_ref, wk_ref, bcol_ref, gn_ref, skip_ref, y_ref, ext_ref, c_ref, n_ref, m_ref, yt_ref)


def _mlstm_chunk(rs, u_ref, vmt_ref, ot_ref, ift_ref, cw_ref, cb_ref, wqt_ref, wk_ref, bcol_ref,
                 gn_ref, skip_ref, y_ref, ext_ref, c_ref, n_ref, m_ref, yt_ref):
    tm = ROW_TILE
    hd = MLSTM_HEAD_DIM
    halo = SUBLANES
    u = u_ref[rs, :]
    ext_ref[halo:halo + tm, :] = u
    acc = jnp.broadcast_to(cb_ref[...], u.shape)
    for j in range(MLSTM_CONV):
        acc = acc + cw_ref[j:j + 1, :] * ext_ref[halo - (MLSTM_CONV - 1) + j:halo - (MLSTM_CONV - 1) + j + tm, :]
    ext_ref[0:halo, :] = u[tm - halo:, :]
    uc = acc * jax.nn.sigmoid(acc)

    gr = ift_ref[:, rs] + bcol_ref[...]
    rows = lax.broadcasted_iota(jnp.int32, (tm, tm), 0)
    cols = lax.broadcasted_iota(jnp.int32, (tm, tm), 1)
    causal_t = rows <= cols
    triu = jnp.where(causal_t, 1.0, 0.0).astype(BF16)
    r1, r2, r3 = _split3(_log_sigmoid(gr))
    bcum_r = _dot(r1, triu) + _dot(r2, triu) + _dot(r3, triu)
    key_rows = gr[:MLSTM_HEADS, :] - bcum_r[MLSTM_HEADS:, :]
    key_cols = jnp.concatenate([key_rows, jnp.zeros((LANES - MLSTM_HEADS, tm), F32)], axis=0).T

    uct = uc.T

    def decay_weights(st, h, hs):
        fl = MLSTM_HEADS + h
        b_row = bcum_r[fl:fl + 1, :]
        st["key_row"] = key_rows[h:h + 1, :]
        st["key_col"] = key_cols[:, h:h + 1]
        m_prev = m_ref[h][:, 0:1]
        dlog = jnp.where(causal_t, st["key_col"] + b_row, NEG_INF)
        inter = b_row + m_prev
        m_t = jnp.maximum(inter, jnp.max(dlog, axis=0, keepdims=True))
        st["w_intra"] = jnp.exp(dlog - m_t)
        st["w_inter"] = jnp.exp(inter - m_t)
        st["m_t"], st["m_prev"], st["b_end"] = m_t, m_prev, b_row[:, tm - 1:tm]

    def project(st, h, hs):
        st["qtb"] = _dot(wqt_ref[h], uct[hs, :].astype(BF16)).astype(BF16)
        st["k"] = _dot(uc[:, hs].astype(BF16), wk_ref[h]) * (hd ** -0.5)

    def scores(st, h, hs):
        st["s"] = _dot(st["k"].astype(BF16), st["qtb"]) * st.pop("w_intra")

    def readout(st, h, hs):
        qtb, s, w_inter, m_t = st.pop("qtb"), st.pop("s"), st.pop("w_inter"), st.pop("m_t")
        n_prev = n_ref[h]
        n_hi = n_prev.astype(BF16)
        n_lo = (n_prev - n_hi.astype(F32)).astype(BF16)
        qn = (_dot(n_hi, qtb) + _dot(n_lo, qtb))[0:1, :]
        num = w_inter * _dot(c_ref[h].astype(BF16), qtb) + _dot(vmt_ref[hs, rs], s.astype(BF16))
        den = w_inter * qn + jnp.sum(s, axis=0, keepdims=True)
        st["hh"] = num / jnp.maximum(jnp.abs(den), jnp.exp(-m_t))

    def update_state(st, h, hs):
        b_end, m_prev = st.pop("b_end"), st.pop("m_prev")
        m_new = jnp.maximum(b_end + m_prev, jnp.max(b_end + st.pop("key_row"), axis=1, keepdims=True))
        decay = jnp.exp(b_end + m_prev - m_new)
        kw = st.pop("k") * jnp.exp(b_end + st.pop("key_col") - m_new)
        n_prev = n_ref[h]
        c_ref[h] = decay * c_ref[h] + _dot(vmt_ref[hs, rs], kw.astype(BF16))
        n_ref[h] = decay * n_prev + jnp.broadcast_to(jnp.sum(kw, axis=0, keepdims=True), n_prev.shape)
        m_ref[h] = jnp.broadcast_to(m_new, (1, LANES))

    def gate_and_norm(st, h, hs):
        hh = jax.nn.sigmoid(ot_ref[hs, rs]) * st.pop("hh")
        mu = jnp.mean(hh, axis=0, keepdims=True)
        hc = hh - mu
        var = jnp.mean(hc * hc, axis=0, keepdims=True)
        yt_ref[hs, :] = hc * lax.rsqrt(var + GN_EPS) * gn_ref[hs, :] + skip_ref[hs, :] * uct[hs, :]

    _run_skewed((decay_weights, project, scores, readout, update_state, gate_and_norm), MLSTM_HEADS, hd)
    y_ref[rs, :] = yt_ref[...].T.astype(BF16)


def _mlstm(u, vmt, ot, ift, conv_w, conv_b, wqt, wk, bcol, gn_g, skip, batch, seq):
    n = u.shape[0]
    tm = MLSTM_CHUNKS * ROW_TILE
    assert seq % tm == 0
    nc = seq // tm
    row = lambda w: pl.BlockSpec((tm, w), lambda b, c: (b * nc + c, 0))
    col = lambda h: pl.BlockSpec((h, tm), lambda b, c: (0, b * nc + c))
    in_specs = [row(MLSTM_WIDTH), col(MLSTM_WIDTH), col(MLSTM_WIDTH), col(SUBLANES),
                _full(conv_w.shape), _full(conv_b.shape), _full(wqt.shape), _full(wk.shape),
                _full(bcol.shape), _full(gn_g.shape), _full(skip.shape)]
    return pl.pallas_call(
        _mlstm_kernel, grid=(batch, nc), in_specs=in_specs, out_specs=row(MLSTM_WIDTH),
        out_shape=jax.ShapeDtypeStruct((n, MLSTM_WIDTH), BF16),
        scratch_shapes=[pltpu.VMEM((SUBLANES + ROW_TILE, MLSTM_WIDTH), F32),
                        pltpu.VMEM((MLSTM_HEADS, MLSTM_HEAD_DIM, MLSTM_HEAD_DIM), F32),
                        pltpu.VMEM((MLSTM_HEADS, SUBLANES, MLSTM_HEAD_DIM), F32),
                        pltpu.VMEM((MLSTM_HEADS, 1, LANES), F32),
                        pltpu.VMEM((MLSTM_WIDTH, ROW_TILE), F32)],
        compiler_params=_params("parallel", "arbitrary"), name="mlstm",
    )(u, vmt, ot, ift, conv_w, conv_b, wqt, wk, bcol, gn_g, skip)


def _mix_kernel(xn_ref, ya_ref, ym_ref, ga_ref, gm_ref, wau_ref, wmu_ref, wout_ref,
                g1_ref, b1_ref, wrc_ref, br_ref,
                x1_ref, x1t_ref, ri_ref, rw_ref, cnt_out_ref, cnt_ref):
    @pl.when(pl.program_id(0) == 0)
    def _():
        cnt_ref[...] = jnp.zeros_like(cnt_ref)

    tm = ROW_TILE
    sub = lax.broadcasted_iota(jnp.int32, (LANES, tm), 0).astype(F32)
    big = float(4 * LANES)

    def up_and_mix(st, c, rs):
        a_up = _dot(ya_ref[rs, :], wau_ref[...])
        m_up = _dot(ym_ref[rs, :], wmu_ref[...])
        mix = ga_ref[rs, :].astype(F32) * a_up + gm_ref[rs, :].astype(F32) * m_up
        st["mix"] = mix.astype(BF16)

    def out_and_norm(st, c, rs):
        x1 = _layer_norm(DEEPNORM_ALPHA * xn_ref[rs, :] + _dot(st.pop("mix"), wout_ref[...]), g1_ref[...], b1_ref[...])
        x1_ref[rs, :] = x1
        nch = _token_rows(x1.shape[1])
        _to_token_tiles(x1t_ref.at[pl.ds(rs.start * nch, (rs.stop - rs.start) * nch), :], x1)
        st["x1"] = x1

    def router_logits(st, c, rs):
        x1 = st.pop("x1")
        x_hi = x1.astype(BF16)
        x_lo = (x1 - x_hi.astype(F32)).astype(BF16)
        both = _dot_nt(wrc_ref[...], x_hi)
        st["logits"] = both[:LANES] + both[LANES:] + _dot_nt(wrc_ref[:LANES, :], x_lo) + br_ref[...]

    def route(st, c, rs):
        logits = st.pop("logits")
        is_g = (sub >= float(MOE_EXPERTS)) & (sub < float(MOE_EXPERTS + MOE_GROUPS))
        gl = jnp.where(is_g, logits, NEG_INF)
        ge = jnp.exp(gl - jnp.max(gl, axis=0, keepdims=True))
        gp = ge / jnp.sum(ge, axis=0, keepdims=True)
        g_w = jnp.max(gp, axis=0, keepdims=True)
        g_idx = jnp.min(jnp.where((gp == g_w) & is_g, sub - float(MOE_EXPERTS), big), axis=0, keepdims=True)
        lo = g_idx * float(MOE_EXPERTS_PER_GROUP)
        in_grp = (sub >= lo) & (sub < lo + float(MOE_EXPERTS_PER_GROUP))
        el = jnp.where(in_grp, logits, NEG_INF)
        v1 = jnp.max(el, axis=0, keepdims=True)
        i1 = jnp.min(jnp.where((el == v1) & in_grp, sub, big), axis=0, keepdims=True)
        el2 = jnp.where(sub == i1, NEG_INF, el)
        v2 = jnp.max(el2, axis=0, keepdims=True)
        i2 = jnp.min(jnp.where((el2 == v2) & in_grp & (sub != i1), sub, big), axis=0, keepdims=True)
        e2 = jnp.exp(v2 - v1)
        w0 = g_w / (1.0 + e2)
        w1 = g_w * e2 / (1.0 + e2)
        rw_ref[rs, :] = jnp.where(sub == 0.0, w0, jnp.where(sub == 1.0, w1, 0.0)).T
        st["i1"], st["i2"] = i1, i2

    def rank(st, c, rs):
        i1, i2 = st.pop("i1"), st.pop("i2")
        is1 = sub == i1
        is2 = sub == i2
        onehot = jnp.where(is1 | is2, 1.0, 0.0)
        rows = lax.broadcasted_iota(jnp.int32, (tm, tm), 0)
        cols = lax.broadcasted_iota(jnp.int32, (tm, tm), 1)
        earlier = jnp.where(rows < cols, 1.0, 0.0).astype(BF16)
        before = _dot(onehot.astype(BF16), earlier) + cnt_ref[...]
        r0 = jnp.sum(jnp.where(is1, before, 0.0), axis=0, keepdims=True)
        r1 = jnp.sum(jnp.where(is2, before, 0.0), axis=0, keepdims=True)
        total = cnt_ref[...] + jnp.sum(onehot, axis=1, keepdims=True)
        cnt_ref[...] = total
        cnt_out_ref[...] = total
        ri_t = jnp.where(sub == 0.0, i1, jnp.where(sub == 1.0, i2, jnp.where(sub == 2.0, r0, jnp.where(sub == 3.0, r1, 0.0))))
        ri_ref[:, rs] = ri_t[:SUBLANES, :].astype(jnp.int32)

    _run_skewed((up_and_mix, out_and_norm, router_logits, route, rank), xn_ref.shape[0] // tm, tm)


def _mix(xn, ya, ym, ga, gm, wau, wmu, wout, g1, b1, wrc, br):
    n, d = xn.shape
    tm = MIX_CHAINS * ROW_TILE
    row = lambda w: pl.BlockSpec((tm, w), lambda i: (i, 0))
    in_specs = [row(d), row(ATTN_WIDTH), row(MLSTM_WIDTH), row(d), row(d),
                _full(wau.shape), _full(wmu.shape), _full(wout.shape), _full(g1.shape), _full(b1.shape),
                _full(wrc.shape), _full(br.shape)]
    nch = _token_rows(d)
    out_shape = (jax.ShapeDtypeStruct((n, d), F32), jax.ShapeDtypeStruct((n * nch, LANES), F32),
                 jax.ShapeDtypeStruct((SUBLANES, n), jnp.int32),
                 jax.ShapeDtypeStruct((n, LANES), F32), jax.ShapeDtypeStruct((LANES, 1), F32))
    out_specs = (row(d), pl.BlockSpec((tm * nch, LANES), lambda i: (i, 0)),
                 pl.BlockSpec((SUBLANES, tm), lambda i: (0, i)), row(LANES), _full((LANES, 1)))
    return pl.pallas_call(
        _mix_kernel, grid=(n // tm,), in_specs=in_specs, out_specs=out_specs, out_shape=out_shape,
        scratch_shapes=[pltpu.VMEM((LANES, 1), F32)],
        compiler_params=_params("arbitrary"), name="mix",
    )(xn, ya, ym, ga, gm, wau, wmu, wout, g1, b1, wrc, br)


def _token_rows(d):
    return d // LANES


def _to_token_tiles(dst_ref, x):
    rows, d = x.shape
    nch = _token_rows(d)
    for c in range(nch):
        dst_ref[pl.ds(c, rows, stride=nch), :] = x[:, c * LANES:(c + 1) * LANES]


def _from_token_tiles(src_ref, rows, d):
    nch = _token_rows(d)
    return jnp.concatenate([src_ref[pl.ds(c, rows, stride=nch), :] for c in range(nch)], axis=1)


def _token_copy(src, src_tok, dst, dst_tok, nch, sem):
    s0 = pl.multiple_of(src_tok * nch, nch)
    d0 = pl.multiple_of(dst_tok * nch, nch)
    return pltpu.make_async_copy(src.at[pl.ds(s0, nch), :], dst.at[pl.ds(d0, nch), :], sem)


def _slots_kernel(ri_ref, ps_ref, o_ref):
    ri = ri_ref[...].astype(F32)
    ps = ps_ref[...]
    expert = lax.broadcasted_iota(jnp.int32, (ps.shape[0], ri.shape[1]), 0).astype(F32)
    row_id = lax.broadcasted_iota(jnp.int32, ri.shape, 0)
    out = jnp.zeros(ri.shape, F32)
    for k in range(2):
        start = jnp.sum(jnp.where(expert == ri[k:k + 1, :], jnp.broadcast_to(ps, expert.shape), 0.0),
                        axis=0, keepdims=True)
        out = jnp.where(row_id == k, start + ri[2 + k:3 + k, :], out)
    o_ref[...] = out.astype(jnp.int32)


def _slots(ri, pad_start_col):
    n = ri.shape[1]
    tm = SLOT_TILE
    blk = pl.BlockSpec((SUBLANES, tm), lambda i: (0, i))
    return pl.pallas_call(
        _slots_kernel, grid=(n // tm,), in_specs=[blk, _full(pad_start_col.shape)], out_specs=blk,
        out_shape=jax.ShapeDtypeStruct((SUBLANES, n), jnp.int32),
        compiler_params=_params("parallel"), name="slots",
    )(ri, pad_start_col)


def _slot(dest_ref, r, k):
    return dest_ref[k * ROW_TILE + r]


def _dispatch_kernel(dest_ref, last_ref, x1t_ref, xs_ref, zero_ref, sem, zsem):
    tm = ROW_TILE
    tb = EXPERT_TILE
    nch = zero_ref.shape[0] // tb

    @pl.when(pl.program_id(0) == 0)
    def _():
        zero_ref[...] = jnp.zeros_like(zero_ref)

        def desc(tok):
            off = pl.multiple_of(jnp.maximum(tok, 0) * nch, nch)
            return pltpu.make_async_copy(zero_ref, xs_ref.at[pl.ds(off, tb * nch), :], zsem)

        def zstart(e, _):
            @pl.when(last_ref[e] >= 0)
            def _():
                desc(last_ref[e]).start()
            return 0

        def zwait(e, _):
            @pl.when(last_ref[e] >= 0)
            def _():
                desc(last_ref[e]).wait()
            return 0

        lax.fori_loop(0, MOE_EXPERTS, zstart, 0)
        nused = last_ref[MOE_EXPERTS]
        nblk = xs_ref.shape[0] // (tb * nch)
        lax.fori_loop(nused, nblk, lambda b, _: (desc(b * tb).start(), 0)[1], 0)
        lax.fori_loop(0, MOE_EXPERTS, zwait, 0)
        lax.fori_loop(nused, nblk, lambda b, _: (desc(b * tb).wait(), 0)[1], 0)

    step = pl.program_id(0)
    slot = step % 2

    def start(r, _):
        for k in range(2):
            _token_copy(x1t_ref, step * tm + r, xs_ref, _slot(dest_ref, r, k), nch, sem.at[slot]).start(priority=k)
        return 0

    def drain(which):
        def wait(r, _):
            for k in range(2):
                _token_copy(x1t_ref, 0, xs_ref, 0, nch, sem.at[which]).wait()
            return 0
        lax.fori_loop(0, tm, wait, 0, unroll=8)

    lax.fori_loop(0, tm, start, 0, unroll=8)

    @pl.when(step > 0)
    def _():
        drain(1 - slot)

    @pl.when(step == pl.num_programs(0) - 1)
    def _():
        drain(slot)


def _dispatch(dest, last_blk, x1t, n, d, n_rows):
    tm = ROW_TILE
    nch = _token_rows(d)
    return pl.pallas_call(
        _dispatch_kernel, grid=(n // tm,),
        in_specs=[pl.BlockSpec((2 * tm,), lambda i: (i,), memory_space=pltpu.SMEM),
                  pl.BlockSpec(memory_space=pltpu.SMEM),
                  pl.BlockSpec(memory_space=pl.ANY)],
        out_specs=pl.BlockSpec(memory_space=pl.ANY),
        out_shape=jax.ShapeDtypeStruct((n_rows * nch, LANES), F32),
        scratch_shapes=[pltpu.VMEM((EXPERT_TILE * nch, LANES), F32),
                        pltpu.SemaphoreType.DMA((2,)), pltpu.SemaphoreType.DMA(())],
        compiler_params=_params("arbitrary"), name="dispatch",
    )(dest, last_blk, x1t)


def _expert_kernel(first_ref, count_ref, widx_ref, nused_ref, wg_ref, wu_ref, wd_ref, xs_ref, ys_ref,
                   wgb_ref, wub_ref, wdb_ref, xbuf_ref, ybuf_ref, in_sem, out_sem):
    del widx_ref
    e = pl.program_id(0)
    nused = nused_ref[0]
    d = wg_ref.shape[1]
    nch = _token_rows(d)
    rows = xbuf_ref.shape[1]
    tb = rows // nch

    def blk(ref, b):
        return ref.at[pl.ds(pl.multiple_of(b * rows, rows), rows), :]

    def in_copy(b, slot):
        return pltpu.make_async_copy(blk(xs_ref, b), xbuf_ref.at[slot], in_sem.at[slot])

    def out_copy(b, slot):
        return pltpu.make_async_copy(ybuf_ref.at[slot], blk(ys_ref, b), out_sem.at[slot])

    n_in = xbuf_ref.shape[0]

    @pl.when(e == 0)
    def _():
        for b0 in range(n_in - 1):
            @pl.when(b0 < nused)
            def _():
                in_copy(b0, b0).start()

    @pl.when(count_ref[e] > 0)
    def _():
        wgb_ref[...] = wg_ref[0].astype(BF16)
        wub_ref[...] = wu_ref[0].astype(BF16)
        wdb_ref[...] = wd_ref[0].astype(BF16)

    def body(b, _):
        slot = b % n_in
        oslot = b % 2
        in_copy(b, slot).wait()

        @pl.when(b + n_in - 1 < nused)
        def _():
            in_copy(b + n_in - 1, (b + n_in - 1) % n_in).start()

        @pl.when(b >= 2)
        def _():
            out_copy(b - 2, oslot).wait()

        xb = _from_token_tiles(xbuf_ref.at[slot], tb, d).astype(BF16)
        g = _dot(xb, wgb_ref[...])
        u = _dot(xb, wub_ref[...])
        hmid = g * jax.nn.sigmoid(g) * u
        _to_token_tiles(ybuf_ref.at[oslot], _dot(hmid.astype(BF16), wdb_ref[...]))
        out_copy(b, oslot).start()
        return 0

    lax.fori_loop(first_ref[e], first_ref[e] + count_ref[e], body, 0)

    @pl.when(e == pl.num_programs(0) - 1)
    def _():
        for back in (2, 1):
            @pl.when(nused >= back)
            def _():
                out_copy(nused - back, (nused - back) % 2).wait()


def _experts(first_blk, blk_count, w_idx, nused, xs, w_gate, w_up, w_down):
    n_exp, d, dff = w_gate.shape
    nch = _token_rows(d)
    rows = EXPERT_TILE * nch
    w_spec = lambda shape: pl.BlockSpec(shape, lambda e, fb, bc, wi, nu: (wi[e], 0, 0))
    any_spec = pl.BlockSpec(memory_space=pl.ANY)
    grid_spec = pltpu.PrefetchScalarGridSpec(
        num_scalar_prefetch=4, grid=(n_exp,),
        in_specs=[w_spec((1, d, dff)), w_spec((1, d, dff)), w_spec((1, dff, d)), any_spec],
        out_specs=any_spec,
        scratch_shapes=[pltpu.VMEM((d, dff), BF16), pltpu.VMEM((d, dff), BF16), pltpu.VMEM((dff, d), BF16),
                        pltpu.VMEM((EXPERT_IN_SLOTS, rows, LANES), F32), pltpu.VMEM((2, rows, LANES), F32),
                        pltpu.SemaphoreType.DMA((EXPERT_IN_SLOTS,)), pltpu.SemaphoreType.DMA((2,))],
    )
    return pl.pallas_call(
        _expert_kernel, grid_spec=grid_spec, out_shape=jax.ShapeDtypeStruct(xs.shape, F32),
        input_output_aliases={7: 0},
        compiler_params=_params("arbitrary"), name="experts",
    )(first_blk, blk_count, w_idx, nused, w_gate, w_up, w_down, xs)


def _combine_kernel(dest_ref, dest_next_ref, x1_ref, rw_ref, g_ref, b_ref, ys_ref, o_ref, buf_ref, sem):
    tm, d = x1_ref.shape
    nch = _token_rows(d)
    step = pl.program_id(0)
    slot = step % 2

    def gather(idx_ref, which):
        def start(r, _):
            for k in range(2):
                _token_copy(ys_ref, _slot(idx_ref, r, k), buf_ref.at[which, k], r, nch,
                            sem.at[which]).start(priority=k)
            return 0
        lax.fori_loop(0, tm, start, 0, unroll=8)

    @pl.when(step == 0)
    def _():
        gather(dest_ref, 0)

    @pl.when(step + 1 < pl.num_programs(0))
    def _():
        gather(dest_next_ref, 1 - slot)

    def wait(r, _):
        for k in range(2):
            _token_copy(ys_ref, 0, buf_ref.at[slot, k], 0, nch, sem.at[slot]).wait()
        return 0

    lax.fori_loop(0, tm, wait, 0, unroll=8)
    rw = rw_ref[...]
    y0 = _from_token_tiles(buf_ref.at[slot, 0], tm, d)
    y1 = _from_token_tiles(buf_ref.at[slot, 1], tm, d)
    ffn = rw[:, 0:1] * y0 + rw[:, 1:2] * y1
    o_ref[...] = _layer_norm(DEEPNORM_ALPHA * x1_ref[...] + ffn, g_ref[...], b_ref[...])


def _combine(dest, x1, rw, ln_g, ln_b, ys):
    n, d = x1.shape
    tm = ROW_TILE
    nch = _token_rows(d)
    last = n // tm - 1
    row = lambda w: pl.BlockSpec((tm, w), lambda i: (i, 0))
    return pl.pallas_call(
        _combine_kernel, grid=(n // tm,),
        in_specs=[pl.BlockSpec((2 * tm,), lambda i: (i,), memory_space=pltpu.SMEM),
                  pl.BlockSpec((2 * tm,), lambda i: (jnp.minimum(i + 1, last),), memory_space=pltpu.SMEM),
                  row(d), row(LANES), _full(ln_g.shape), _full(ln_b.shape),
                  pl.BlockSpec(memory_space=pl.ANY)],
        out_specs=row(d),
        out_shape=jax.ShapeDtypeStruct((n, d), F32),
        scratch_shapes=[pltpu.VMEM((2, 2, tm * nch, LANES), F32), pltpu.SemaphoreType.DMA((2,))],
        compiler_params=_params("arbitrary"), name="combine",
    )(dest, dest, x1, rw, ln_g, ln_b, ys)


def _rope_tables(seq):
    half = ATTN_HEAD_DIM // 2
    inv_freq = ROPE_THETA ** (-np.arange(half, dtype=np.float64) / half)
    ang = np.arange(seq, dtype=np.float64)[:, None] * inv_freq[None, :]
    cos = np.cos(ang)
    sin = np.sin(ang)
    cos_h = np.concatenate([cos, cos], axis=1)
    sin_h = np.concatenate([-sin, sin], axis=1)
    return (jnp.asarray(np.tile(cos_h, (1, ATTN_HEADS)), F32), jnp.asarray(np.tile(sin_h, (1, ATTN_HEADS)), F32))


def _pad_lanes(a, width=LANES):
    return jnp.pad(a, ((0, 0), (0, width - a.shape[1])))


def kernel(x, ln0_g, ln0_b, w_in, conv_w, conv_b, w_mq, w_mk, b_i, b_f, gn_g, skip, w_attn_up, w_mlstm_up, w_out,
           ln1_g, ln1_b, w_router_group, b_router_group, w_router_expert, b_router_expert, w_gate, w_up, w_down,
           ln2_g, ln2_b):
    batch, seq, d = x.shape
    n = batch * seq
    assert seq % ROW_TILE == 0 and ROW_TILE == MOBA_BLOCK and w_in.shape[0] == DEPTH
    x2 = x.reshape(n, d)
    vec = lambda a: a.reshape(1, -1).astype(F32)

    wt = w_in[0].T
    cos, sin = _rope_tables(seq)

    q, k, v, kmean, u, vm, o, ift, ga, gm, xn = _inproj(
        x2, vec(ln0_g), vec(ln0_b), wt, cos, sin, batch, seq)

    nb = seq // MOBA_BLOCK
    km = kmean.reshape(batch, nb, ATTN_HEADS, ATTN_HEAD_DIM).transpose(0, 2, 1, 3)
    ya = _moba(q, k, v, km).reshape(n, ATTN_WIDTH)

    b_if = jnp.concatenate([b_i[0], b_f[0]]).astype(F32)
    ym = _mlstm(u, vm, o, ift, conv_w[0], vec(conv_b[0]), w_mq[0].transpose(0, 2, 1).astype(BF16),
                w_mk[0].astype(BF16), b_if[:, None],
                gn_g[0].astype(F32)[:, None], skip[0].astype(F32)[:, None], batch, seq)

    w_r = _pad_lanes(jnp.concatenate([w_router_expert[0], w_router_group[0]], axis=1))
    w_r_hi = w_r.astype(BF16)
    w_r_lo = (w_r - w_r_hi.astype(F32)).astype(BF16)
    w_rc = jnp.concatenate([w_r_hi.T, w_r_lo.T], axis=0)
    b_r = _pad_lanes(jnp.concatenate([b_router_expert[0], b_router_group[0]])[None, :]).T
    x1, x1t, ri, rw, counts = _mix(
        xn, ya, ym, ga, gm, w_attn_up[0].astype(BF16), w_mlstm_up[0].astype(BF16),
        w_out[0].astype(BF16), vec(ln1_g[0]), vec(ln1_b[0]), w_rc, b_r)

    tb = EXPERT_TILE
    nblk = (2 * n) // tb + MOE_EXPERTS
    cnt = counts[:MOE_EXPERTS, 0].astype(jnp.int32)
    nblk_e = (cnt + tb - 1) // tb
    blk_end = jnp.cumsum(nblk_e)
    pad_start = (blk_end - nblk_e) * tb
    nused = blk_end[-1:]
    ids = jnp.arange(MOE_EXPERTS, dtype=jnp.int32)
    prev_used = jnp.max(jnp.where((ids[None, :] <= ids[:, None]) & (nblk_e[None, :] > 0), ids[None, :], -1), axis=1)
    first_used = jnp.min(jnp.where(nblk_e > 0, ids, MOE_EXPERTS - 1))
    w_idx = jnp.where(prev_used >= 0, prev_used, first_used).astype(jnp.int32)
    last_blk = jnp.where(nblk_e > 0, (blk_end - 1) * tb, -1)
    last_blk = jnp.concatenate([last_blk, nused]).astype(jnp.int32)
    dest = _slots(ri, pad_start.astype(F32)[:, None])
    dest = dest[:2].reshape(2, n // ROW_TILE, ROW_TILE).transpose(1, 0, 2).reshape(2 * n)

    xs = _dispatch(dest, last_blk, x1t, n, d, nblk * tb)
    ys = _experts((blk_end - nblk_e).astype(jnp.int32), nblk_e.astype(jnp.int32), w_idx, nused.astype(jnp.int32),
                  xs, w_gate[0], w_up[0], w_down[0])
    out = _combine(dest, x1, rw, vec(ln2_g[0]), vec(ln2_b[0]), ys)
    return out.reshape(batch, seq, d)
```

```python
import math

import jax
import jax.numpy as jnp
import numpy as np
from jax import lax
from jax.experimental import pallas as pl
from jax.experimental.pallas import tpu as pltpu

F32 = jnp.float32
BF16 = jnp.bfloat16

ATTN_HEADS = 8
ATTN_HEAD_DIM = 64
ATTN_WIDTH = ATTN_HEADS * ATTN_HEAD_DIM
MOBA_BLOCK = 256
MOBA_TOPK = 3
ROPE_THETA = 10000.0
MLSTM_HEADS = 4
MLSTM_HEAD_DIM = 128
MLSTM_WIDTH = MLSTM_HEADS * MLSTM_HEAD_DIM
MLSTM_CONV = 4
MOE_GROUPS = 8
MOE_EXPERTS_PER_GROUP = 8
MOE_EXPERTS = MOE_GROUPS * MOE_EXPERTS_PER_GROUP
LN_EPS = 1e-5
GN_EPS = 1e-6
DEPTH = 1
DEEPNORM_ALPHA = (2 * DEPTH) ** 0.25

LANES = 128
SUBLANES = 8
ROW_TILE = 256
EXPERT_TILE = 256
EXPERT_IN_SLOTS = 4
INPROJ_CHAINS = 2
MLSTM_CHUNKS = 2
SLOT_TILE = 2048
MIX_CHAINS = 4
VMEM_LIMIT = 48 * 1024 * 1024
INPROJ_VMEM_LIMIT = 58 * 1024 * 1024
WEIGHT_CHUNK = 512
LOG2_E = math.log2(math.e)

NEG_INF = float("-inf")


def _params(*sem):
    return pltpu.CompilerParams(dimension_semantics=sem, vmem_limit_bytes=VMEM_LIMIT)


def _dot(a, b):
    return jnp.dot(a, b, preferred_element_type=F32)


def _dot_nt(a, b):
    return lax.dot_general(a, b, (((1,), (1,)), ((), ())), preferred_element_type=F32)


def _split3(x):
    x1 = x.astype(BF16)
    r1 = x - x1.astype(F32)
    x2 = r1.astype(BF16)
    r2 = r1 - x2.astype(F32)
    return x1, x2, r2.astype(BF16)


def _layer_norm(x, g, b):
    mu = jnp.mean(x, axis=-1, keepdims=True)
    xc = x - mu
    var = jnp.mean(xc * xc, axis=-1, keepdims=True)
    return xc * lax.rsqrt(var + LN_EPS) * g + b


def _log_sigmoid(x):
    return jnp.minimum(x, 0.0) - jnp.log1p(jnp.exp(-jnp.abs(x)))


def _full(shape):
    nd = len(shape)
    return pl.BlockSpec(shape, lambda *_: (0,) * nd)


def _run_skewed(phases, chains, rows):
    states = [dict() for _ in range(chains)]
    for t in range(chains + len(phases) - 1):
        for c in range(chains):
            if 0 <= t - c < len(phases):
                phases[t - c](states[c], c, slice(c * rows, (c + 1) * rows))


def _loop_groups(count, body, group=4):
    def trip(g, _):
        for d in range(group):
            body(g * group + d)
        return 0

    lax.fori_loop(0, count // group, trip, 0)
    done = (count // group) * group
    size = group // 2
    while size >= 1:
        take = ((count - done) // size) * size
        @pl.when(take > 0)
        def _(done=done, size=size):
            for d in range(size):
                body(done + d)
        done = done + take
        size //= 2


def _inproj_kernel(x_ref, g_ref, b_ref, wt_ref, cos_ref, sin_ref,
                   q_ref, k_ref, v_ref, km_ref, u_ref, vm_ref, o_ref, ift_ref, ga_ref, gm_ref, xn_ref,
                   wqkv_ref, wuvo_ref, wift_ref, wg_ref):
    tm = ROW_TILE
    lane = lax.broadcasted_iota(jnp.int32, (tm, ATTN_WIDTH), 1)
    first_half = (lane % ATTN_HEAD_DIM) < (ATTN_HEAD_DIM // 2)

    @pl.when(pl.program_id(0) == 0)
    def _():
        def fill(dst_ref, row0):
            for c in range(dst_ref.shape[1] // WEIGHT_CHUNK):
                cols = slice(c * WEIGHT_CHUNK, (c + 1) * WEIGHT_CHUNK)
                rows = slice(row0 + c * WEIGHT_CHUNK, row0 + (c + 1) * WEIGHT_CHUNK)
                dst_ref[:, cols] = wt_ref[rows, :].T.astype(BF16)

        c_if = 3 * ATTN_WIDTH + 3 * MLSTM_WIDTH
        fill(wqkv_ref, 0)
        fill(wuvo_ref, 3 * ATTN_WIDTH)
        wift_ref[...] = wt_ref[c_if:c_if + 2 * MLSTM_HEADS, :].astype(BF16)
        fill(wg_ref, c_if + 2 * MLSTM_HEADS)

    def norm(st, c, rs):
        xn = _layer_norm(x_ref[rs, :], g_ref[...], b_ref[...])
        xn_ref[rs, :] = xn
        st["xb"] = xn.astype(BF16)

    def qkv_matmul(st, c, rs):
        st["zqkv"] = _dot(st["xb"], wqkv_ref[...])

    def attn_outputs(st, c, rs):
        zqkv = st.pop("zqkv")
        cos = cos_ref[rs, :]
        sin = sin_ref[rs, :]

        def rope(t):
            fwd = pltpu.roll(t, ATTN_WIDTH - ATTN_HEAD_DIM // 2, axis=1)
            bwd = pltpu.roll(t, ATTN_HEAD_DIM // 2, axis=1)
            return t * cos + jnp.where(first_half, fwd, bwd) * sin

        q = rope(zqkv[:, :ATTN_WIDTH]) * (ATTN_HEAD_DIM ** -0.5 * LOG2_E)
        k = rope(zqkv[:, ATTN_WIDTH:2 * ATTN_WIDTH])
        v = zqkv[:, 2 * ATTN_WIDTH:]
        km_ref[c] = jnp.mean(k, axis=0, keepdims=True)
        qt = q.T
        vt = v.T
        for h in range(ATTN_HEADS):
            sl = slice(h * ATTN_HEAD_DIM, (h + 1) * ATTN_HEAD_DIM)
            q_ref[0, h, :, rs] = qt[sl, :].astype(BF16)
            k_ref[0, h, rs, :] = k[:, sl].astype(BF16)
            v_ref[0, h, :, rs] = vt[sl, :].astype(BF16)

    def uvo_matmul(st, c, rs):
        st["zuvo"] = _dot(st["xb"], wuvo_ref[...])

    def mlstm_outputs(st, c, rs):
        zuvo = st.pop("zuvo")
        u_ref[rs, :] = zuvo[:, :MLSTM_WIDTH]
        vm_ref[:, rs] = zuvo[:, MLSTM_WIDTH:2 * MLSTM_WIDTH].T.astype(BF16)
        o_ref[:, rs] = zuvo[:, 2 * MLSTM_WIDTH:].T
        ift_ref[:, rs] = _dot_nt(wift_ref[...], st["xb"])

    def gate_matmul(st, c, rs):
        st["zg"] = _dot(st.pop("xb"), wg_ref[...])

    def gate_outputs(st, c, rs):
        zg = st.pop("zg")
        d = ga_ref.shape[1]
        ga_ref[rs, :] = jax.nn.sigmoid(zg[:, :d]).astype(BF16)
        gm_ref[rs, :] = jax.nn.sigmoid(zg[:, d:]).astype(BF16)

    _run_skewed((norm, qkv_matmul, attn_outputs, uvo_matmul, mlstm_outputs, gate_matmul, gate_outputs),
                x_ref.shape[0] // tm, tm)


def _inproj(x2, ln_g, ln_b, wt, cos, sin, batch, seq):
    n, d = x2.shape
    chains = INPROJ_CHAINS
    tm = chains * ROW_TILE
    assert seq % tm == 0
    nsb = seq // tm
    hd = ATTN_HEAD_DIM
    row = lambda w: pl.BlockSpec((tm, w), lambda i: (i, 0))
    col = lambda h: pl.BlockSpec((h, tm), lambda i: (0, i))
    head = pl.BlockSpec((1, ATTN_HEADS, tm, hd), lambda i: (i // nsb, 0, i % nsb, 0))
    head_t = pl.BlockSpec((1, ATTN_HEADS, hd, tm), lambda i: (i // nsb, 0, 0, i % nsb))
    tab = pl.BlockSpec((tm, ATTN_WIDTH), lambda i: (i % nsb, 0))
    head_shape = jax.ShapeDtypeStruct((batch, ATTN_HEADS, seq, hd), BF16)
    head_t_shape = jax.ShapeDtypeStruct((batch, ATTN_HEADS, hd, seq), BF16)
    out_shape = (
        head_t_shape, head_shape, head_t_shape,
        jax.ShapeDtypeStruct((n // ROW_TILE, 1, ATTN_WIDTH), F32),
        jax.ShapeDtypeStruct((n, MLSTM_WIDTH), F32),
        jax.ShapeDtypeStruct((MLSTM_WIDTH, n), BF16),
        jax.ShapeDtypeStruct((MLSTM_WIDTH, n), F32),
        jax.ShapeDtypeStruct((SUBLANES, n), F32),
        jax.ShapeDtypeStruct((n, d), BF16),
        jax.ShapeDtypeStruct((n, d), BF16),
        jax.ShapeDtypeStruct((n, d), F32),
    )
    out_specs = (
        head_t, head, head_t,
        pl.BlockSpec((chains, 1, ATTN_WIDTH), lambda i: (i, 0, 0)),
        row(MLSTM_WIDTH), col(MLSTM_WIDTH), col(MLSTM_WIDTH),
        col(SUBLANES),
        row(d), row(d), row(d),
    )
    wt_spec = pl.BlockSpec(wt.shape, lambda i: (0, 0), pipeline_mode=pl.Buffered(1))
    in_specs = [row(d), _full(ln_g.shape), _full(ln_b.shape), wt_spec, tab, tab]
    return pl.pallas_call(
        _inproj_kernel, grid=(n // tm,), in_specs=in_specs, out_specs=out_specs, out_shape=out_shape,
        scratch_shapes=[pltpu.VMEM((d, 3 * ATTN_WIDTH), BF16), pltpu.VMEM((d, 3 * MLSTM_WIDTH), BF16),
                        pltpu.VMEM((2 * MLSTM_HEADS, d), BF16), pltpu.VMEM((d, 2 * d), BF16)],
        compiler_params=pltpu.CompilerParams(dimension_semantics=("arbitrary",), vmem_limit_bytes=INPROJ_VMEM_LIMIT),
        name="inproj",
    )(x2, ln_g, ln_b, wt, cos, sin)


def _moba_kernel(qt_ref, k_ref, vt_ref, km_ref, o_ref, bias_ref, m_ref, l_ref, acc_ref, s_ref):
    i = pl.program_id(1)
    blk = MOBA_BLOCK
    hd = ATTN_HEAD_DIM
    heads = ATTN_HEADS
    nb = k_ref.shape[2] // blk
    blk_id = lax.broadcasted_iota(jnp.int32, (nb, blk), 0)
    key_pos = lax.broadcasted_iota(jnp.int32, (blk, blk), 0)
    qry_pos = lax.broadcasted_iota(jnp.int32, (blk, blk), 1)
    causal = key_pos <= qry_pos

    for h in range(heads):
        qt = qt_ref[0, h]
        km = km_ref[0, h]
        km_hi = km.astype(BF16)
        km_lo = (km - km_hi.astype(F32)).astype(BF16)
        gate = _dot(km_hi, qt) + _dot(km_lo, qt)
        gate = jnp.where(blk_id < i, gate, NEG_INF)
        for j in range(nb - 1):
            row = gate[j:j + 1, :]
            beats = (gate > row) | ((gate == row) & (blk_id < j))
            cnt = jnp.sum(jnp.where(beats, 1.0, 0.0), axis=0, keepdims=True)
            sel = (cnt < float(MOBA_TOPK)) & (row > NEG_INF)
            bias_ref[j * heads + h] = jnp.where(sel, 0.0, NEG_INF)
    for h in range(heads):
        bias_ref[i * heads + h] = jnp.zeros((1, blk), F32)

    def scores(h, j, own_block):
        qt = qt_ref[0, h]
        half = blk // 2
        m_tile = None
        for c in range(2):
            rows = slice(c * half, (c + 1) * half)
            s = _dot(k_ref[0, h, pl.ds(pl.multiple_of(j * blk + c * half, half), half), :], qt)
            if own_block:
                s = jnp.where(causal[rows], s, NEG_INF)
            s_ref[j * heads + h, rows, :] = s
            m_c = jnp.max(s, axis=0, keepdims=True)
            m_tile = m_c if m_tile is None else jnp.maximum(m_tile, m_c)
        return m_tile

    for h in range(heads):
        m_ref[h] = scores(h, i, True)

    def past_scores(j):
        for h in range(heads):
            m_ref[h] = jnp.maximum(m_ref[h], scores(h, j, False) + bias_ref[j * heads + h])

    _loop_groups(i, past_scores)

    l_ref[...] = jnp.zeros_like(l_ref)
    acc_ref[...] = jnp.zeros_like(acc_ref)

    def accumulate(j):
        off = pl.multiple_of(j * blk, blk)
        for h in range(heads):
            p = jnp.exp2(s_ref[j * heads + h] - (m_ref[h] - bias_ref[j * heads + h]))
            l_ref[h] += jnp.sum(p, axis=0, keepdims=True)
            acc_ref[h] += _dot(vt_ref[0, h, :, pl.ds(off, blk)], p.astype(BF16))

    _loop_groups(i + 1, accumulate)
    yt = acc_ref[...] / l_ref[...]
    o_ref[0] = yt.reshape(heads * hd, blk).T.astype(BF16)


def _moba(qt, k, vt, km):
    batch, heads, seq, hd = k.shape
    blk = MOBA_BLOCK
    nb = seq // blk
    return pl.pallas_call(
        _moba_kernel, grid=(batch, nb),
        in_specs=[
            pl.BlockSpec((1, heads, hd, blk), lambda b, i: (b, 0, 0, i)),
            pl.BlockSpec((1, heads, seq, hd), lambda b, i: (b, 0, 0, 0)),
            pl.BlockSpec((1, heads, hd, seq), lambda b, i: (b, 0, 0, 0)),
            pl.BlockSpec((1, heads, nb, hd), lambda b, i: (b, 0, 0, 0)),
        ],
        out_specs=pl.BlockSpec((1, blk, heads * hd), lambda b, i: (b, i, 0)),
        out_shape=jax.ShapeDtypeStruct((batch, seq, heads * hd), BF16),
        scratch_shapes=[pltpu.VMEM((nb * heads, 1, blk), F32), pltpu.VMEM((heads, 1, blk), F32),
                        pltpu.VMEM((heads, 1, blk), F32), pltpu.VMEM((heads, hd, blk), F32),
                        pltpu.VMEM((nb * heads, blk, blk), F32)],
        compiler_params=_params("parallel", "arbitrary"), name="moba",
    )(qt, k, vt, km)


def _mlstm_kernel(u_ref, vmt_ref, ot_ref, ift_ref, cw_ref, cb_ref, wqt_ref, wk_ref, bcol_ref,
                  gn_ref, skip_ref, y_ref, ext_ref, c_ref, n_ref, m_ref, yt_ref):
    @pl.when(pl.program_id(1) == 0)
    def _():
        ext_ref[0:SUBLANES, :] = jnp.zeros((SUBLANES, MLSTM_WIDTH), F32)
        c_ref[...] = jnp.zeros_like(c_ref)
        n_ref[...] = jnp.zeros_like(n_ref)
        m_ref[...] = jnp.zeros_like(m_ref)

    for c in range(u_ref.shape[0] // ROW_TILE):
        _mlstm_chunk(slice(c * ROW_TILE, (c + 1) * ROW_TILE), u_ref, vmt_ref, ot_ref, ift_ref, cw_ref, cb_ref,
                     wqt_ref, wk_ref, bcol_ref, gn_ref, skip_ref, y_ref, ext_ref, c_ref, n_ref, m_ref, yt_ref)


def _mlstm_chunk(rs, u_ref, vmt_ref, ot_ref, ift_ref, cw_ref, cb_ref, wqt_ref, wk_ref, bcol_ref,
                 gn_ref, skip_ref, y_ref, ext_ref, c_ref, n_ref, m_ref, yt_ref):
    tm = ROW_TILE
    hd = MLSTM_HEAD_DIM
    halo = SUBLANES
    u = u_ref[rs, :]
    ext_ref[halo:halo + tm, :] = u
    acc = jnp.broadcast_to(cb_ref[...], u.shape)
    for j in range(MLSTM_CONV):
        acc = acc + cw_ref[j:j + 1, :] * ext_ref[halo - (MLSTM_CONV - 1) + j:halo - (MLSTM_CONV - 1) + j + tm, :]
    ext_ref[0:halo, :] = u[tm - halo:, :]
    uc = acc * jax.nn.sigmoid(acc)

    gr = ift_ref[:, rs] + bcol_ref[...]
    rows = lax.broadcasted_iota(jnp.int32, (tm, tm), 0)
    cols = lax.broadcasted_iota(jnp.int32, (tm, tm), 1)
    causal_t = rows <= cols
    triu = jnp.where(causal_t, 1.0, 0.0).astype(BF16)
    r1, r2, r3 = _split3(_log_sigmoid(gr))
    bcum_r = _dot(r1, triu) + _dot(r2, triu) + _dot(r3, triu)
    key_rows = gr[:MLSTM_HEADS, :] - bcum_r[MLSTM_HEADS:, :]
    key_cols = jnp.concatenate([key_rows, jnp.zeros((LANES - MLSTM_HEADS, tm), F32)], axis=0).T

    uct = uc.T

    def decay_weights(st, h, hs):
        fl = MLSTM_HEADS + h
        b_row = bcum_r[fl:fl + 1, :]
        st["key_row"] = key_rows[h:h + 1, :]
        st["key_col"] = key_cols[:, h:h + 1]
        m_prev = m_ref[h][:, 0:1]
        dlog = jnp.where(causal_t, st["key_col"] + b_row, NEG_INF)
        inter = b_row + m_prev
        m_t = jnp.maximum(inter, jnp.max(dlog, axis=0, keepdims=True))
        st["w_intra"] = jnp.exp(dlog - m_t)
        st["w_inter"] = jnp.exp(inter - m_t)
        st["m_t"], st["m_prev"], st["b_end"] = m_t, m_prev, b_row[:, tm - 1:tm]

    def project(st, h, hs):
        st["qtb"] = _dot(wqt_ref[h], uct[hs, :].astype(BF16)).astype(BF16)
        st["k"] = _dot(uc[:, hs].astype(BF16), wk_ref[h]) * (hd ** -0.5)

    def scores(st, h, hs):
        st["s"] = _dot(st["k"].astype(BF16), st["qtb"]) * st.pop("w_intra")

    def readout(st, h, hs):
        qtb, s, w_inter, m_t = st.pop("qtb"), st.pop("s"), st.pop("w_inter"), st.pop("m_t")
        n_prev = n_ref[h]
        n_hi = n_prev.astype(BF16)
        n_lo = (n_prev - n_hi.astype(F32)).astype(BF16)
        qn = (_dot(n_hi, qtb) + _dot(n_lo, qtb))[0:1, :]
        num = w_inter * _dot(c_ref[h].astype(BF16), qtb) + _dot(vmt_ref[hs, rs], s.astype(BF16))
        den = w_inter * qn + jnp.sum(s, axis=0, keepdims=True)
        st["hh"] = num / jnp.maximum(jnp.abs(den), jnp.exp(-m_t))

    def update_state(st, h, hs):
        b_end, m_prev = st.pop("b_end"), st.pop("m_prev")
        m_new = jnp.maximum(b_end + m_prev, jnp.max(b_end + st.pop("key_row"), axis=1, keepdims=True))
        decay = jnp.exp(b_end + m_prev - m_new)
        kw = st.pop("k") * jnp.exp(b_end + st.pop("key_col") - m_new)
        n_prev = n_ref[h]
        c_ref[h] = decay * c_ref[h] + _dot(vmt_ref[hs, rs], kw.astype(BF16))
        n_ref[h] = decay * n_prev + jnp.broadcast_to(jnp.sum(kw, axis=0, keepdims=True), n_prev.shape)
        m_ref[h] = jnp.broadcast_to(m_new, (1, LANES))

    def gate_and_norm(st, h, hs):
        hh = jax.nn.sigmoid(ot_ref[hs, rs]) * st.pop("hh")
        mu = jnp.mean(hh, axis=0, keepdims=True)
        hc = hh - mu
        var = jnp.mean(hc * hc, axis=0, keepdims=True)
        yt_ref[hs, :] = hc * lax.rsqrt(var + GN_EPS) * gn_ref[hs, :] + skip_ref[hs, :] * uct[hs, :]

    _run_skewed((decay_weights, project, scores, readout, update_state, gate_and_norm), MLSTM_HEADS, hd)
    y_ref[rs, :] = yt_ref[...].T.astype(BF16)


def _mlstm(u, vmt, ot, ift, conv_w, conv_b, wqt, wk, bcol, gn_g, skip, batch, seq):
    n = u.shape[0]
    tm = MLSTM_CHUNKS * ROW_TILE
    assert seq % tm == 0
    nc = seq // tm
    row = lambda w: pl.BlockSpec((tm, w), lambda b, c: (b * nc + c, 0))
    col = lambda h: pl.BlockSpec((h, tm), lambda b, c: (0, b * nc + c))
    in_specs = [row(MLSTM_WIDTH), col(MLSTM_WIDTH), col(MLSTM_WIDTH), col(SUBLANES),
                _full(conv_w.shape), _full(conv_b.shape), _full(wqt.shape), _full(wk.shape),
                _full(bcol.shape), _full(gn_g.shape), _full(skip.shape)]
    return pl.pallas_call(
        _mlstm_kernel, grid=(batch, nc), in_specs=in_specs, out_specs=row(MLSTM_WIDTH),
        out_shape=jax.ShapeDtypeStruct((n, MLSTM_WIDTH), BF16),
        scratch_shapes=[pltpu.VMEM((SUBLANES + ROW_TILE, MLSTM_WIDTH), F32),
                        pltpu.VMEM((MLSTM_HEADS, MLSTM_HEAD_DIM, MLSTM_HEAD_DIM), F32),
                        pltpu.VMEM((MLSTM_HEADS, SUBLANES, MLSTM_HEAD_DIM), F32),
                        pltpu.VMEM((MLSTM_HEADS, 1, LANES), F32),
                        pltpu.VMEM((MLSTM_WIDTH, ROW_TILE), F32)],
        compiler_params=_params("parallel", "arbitrary"), name="mlstm",
    )(u, vmt, ot, ift, conv_w, conv_b, wqt, wk, bcol, gn_g, skip)


def _mix_kernel(xn_ref, ya_ref, ym_ref, ga_ref, gm_ref, wau_ref, wmu_ref, wout_ref,
                g1_ref, b1_ref, wrc_ref, br_ref,
                x1_ref, ri_ref, rw_ref, cnt_out_ref, cnt_ref):
    @pl.when(pl.program_id(0) == 0)
    def _():
        cnt_ref[...] = jnp.zeros_like(cnt_ref)

    tm = ROW_TILE
    sub = lax.broadcasted_iota(jnp.int32, (LANES, tm), 0).astype(F32)
    big = float(4 * LANES)

    def up_and_mix(st, c, rs):
        a_up = _dot(ya_ref[rs, :], wau_ref[...])
        m_up = _dot(ym_ref[rs, :], wmu_ref[...])
        mix = ga_ref[rs, :].astype(F32) * a_up + gm_ref[rs, :].astype(F32) * m_up
        st["mix"] = mix.astype(BF16)

    def out_and_norm(st, c, rs):
        x1 = _layer_norm(DEEPNORM_ALPHA * xn_ref[rs, :] + _dot(st.pop("mix"), wout_ref[...]), g1_ref[...], b1_ref[...])
        x1_ref[rs, :] = x1
        st["x1"] = x1

    def router_logits(st, c, rs):
        x1 = st.pop("x1")
        x_hi = x1.astype(BF16)
        x_lo = (x1 - x_hi.astype(F32)).astype(BF16)
        both = _dot_nt(wrc_ref[...], x_hi)
        st["logits"] = both[:LANES] + both[LANES:] + _dot_nt(wrc_ref[:LANES, :], x_lo) + br_ref[...]

    def route(st, c, rs):
        logits = st.pop("logits")
        is_g = (sub >= float(MOE_EXPERTS)) & (sub < float(MOE_EXPERTS + MOE_GROUPS))
        gl = jnp.where(is_g, logits, NEG_INF)
        ge = jnp.exp(gl - jnp.max(gl, axis=0, keepdims=True))
        gp = ge / jnp.sum(ge, axis=0, keepdims=True)
        g_w = jnp.max(gp, axis=0, keepdims=True)
        g_idx = jnp.min(jnp.where((gp == g_w) & is_g, sub - float(MOE_EXPERTS), big), axis=0, keepdims=True)
        lo = g_idx * float(MOE_EXPERTS_PER_GROUP)
        in_grp = (sub >= lo) & (sub < lo + float(MOE_EXPERTS_PER_GROUP))
        el = jnp.where(in_grp, logits, NEG_INF)
        v1 = jnp.max(el, axis=0, keepdims=True)
        i1 = jnp.min(jnp.where((el == v1) & in_grp, sub, big), axis=0, keepdims=True)
        el2 = jnp.where(sub == i1, NEG_INF, el)
        v2 = jnp.max(el2, axis=0, keepdims=True)
        i2 = jnp.min(jnp.where((el2 == v2) & in_grp & (sub != i1), sub, big), axis=0, keepdims=True)
        e2 = jnp.exp(v2 - v1)
        w0 = g_w / (1.0 + e2)
        w1 = g_w * e2 / (1.0 + e2)
        rw_ref[rs, :] = jnp.where(sub == 0.0, w0, jnp.where(sub == 1.0, w1, 0.0)).T
        st["i1"], st["i2"] = i1, i2

    def rank(st, c, rs):
        i1, i2 = st.pop("i1"), st.pop("i2")
        is1 = sub == i1
        is2 = sub == i2
        onehot = jnp.where(is1 | is2, 1.0, 0.0)
        rows = lax.broadcasted_iota(jnp.int32, (tm, tm), 0)
        cols = lax.broadcasted_iota(jnp.int32, (tm, tm), 1)
        earlier = jnp.where(rows < cols, 1.0, 0.0).astype(BF16)
        before = _dot(onehot.astype(BF16), earlier) + cnt_ref[...]
        r0 = jnp.sum(jnp.where(is1, before, 0.0), axis=0, keepdims=True)
        r1 = jnp.sum(jnp.where(is2, before, 0.0), axis=0, keepdims=True)
        total = cnt_ref[...] + jnp.sum(onehot, axis=1, keepdims=True)
        cnt_ref[...] = total
        cnt_out_ref[...] = total
        ri_t = jnp.where(sub == 0.0, i1, jnp.where(sub == 1.0, i2, jnp.where(sub == 2.0, r0, jnp.where(sub == 3.0, r1, 0.0))))
        ri_ref[:, rs] = ri_t[:SUBLANES, :].astype(jnp.int32)

    _run_skewed((up_and_mix, out_and_norm, router_logits, route, rank), xn_ref.shape[0] // tm, tm)


def _mix(xn, ya, ym, ga, gm, wau, wmu, wout, g1, b1, wrc, br):
    n, d = xn.shape
    tm = MIX_CHAINS * ROW_TILE
    row = lambda w: pl.BlockSpec((tm, w), lambda i: (i, 0))
    in_specs = [row(d), row(ATTN_WIDTH), row(MLSTM_WIDTH), row(d), row(d),
                _full(wau.shape), _full(wmu.shape), _full(wout.shape), _full(g1.shape), _full(b1.shape),
                _full(wrc.shape), _full(br.shape)]
    out_shape = (jax.ShapeDtypeStruct((n, d), F32), jax.ShapeDtypeStruct((SUBLANES, n), jnp.int32),
                 jax.ShapeDtypeStruct((n, LANES), F32), jax.ShapeDtypeStruct((LANES, 1), F32))
    out_specs = (row(d), pl.BlockSpec((SUBLANES, tm), lambda i: (0, i)), row(LANES), _full((LANES, 1)))
    return pl.pallas_call(
        _mix_kernel, grid=(n // tm,), in_specs=in_specs, out_specs=out_specs, out_shape=out_shape,
        scratch_shapes=[pltpu.VMEM((LANES, 1), F32)],
        compiler_params=_params("arbitrary"), name="mix",
    )(xn, ya, ym, ga, gm, wau, wmu, wout, g1, b1, wrc, br)


def _token_rows(d):
    return d // LANES


def _to_token_tiles(dst_ref, x):
    rows, d = x.shape
    nch = _token_rows(d)
    for c in range(nch):
        dst_ref[pl.ds(c, rows, stride=nch), :] = x[:, c * LANES:(c + 1) * LANES]


def _from_token_tiles(src_ref, rows, d):
    nch = _token_rows(d)
    return jnp.concatenate([src_ref[pl.ds(c, rows, stride=nch), :] for c in range(nch)], axis=1)


def _token_copy(src, src_tok, dst, dst_tok, nch, sem):
    s0 = pl.multiple_of(src_tok * nch, nch)
    d0 = pl.multiple_of(dst_tok * nch, nch)
    return pltpu.make_async_copy(src.at[pl.ds(s0, nch), :], dst.at[pl.ds(d0, nch), :], sem)


def _slots_kernel(ri_ref, ps_ref, o_ref):
    ri = ri_ref[...].astype(F32)
    ps = ps_ref[...]
    expert = lax.broadcasted_iota(jnp.int32, (ps.shape[0], ri.shape[1]), 0).astype(F32)
    row_id = lax.broadcasted_iota(jnp.int32, ri.shape, 0)
    out = jnp.zeros(ri.shape, F32)
    for k in range(2):
        start = jnp.sum(jnp.where(expert == ri[k:k + 1, :], jnp.broadcast_to(ps, expert.shape), 0.0),
                        axis=0, keepdims=True)
        out = jnp.where(row_id == k, start + ri[2 + k:3 + k, :], out)
    o_ref[...] = out.astype(jnp.int32)


def _slots(ri, pad_start_col):
    n = ri.shape[1]
    tm = SLOT_TILE
    blk = pl.BlockSpec((SUBLANES, tm), lambda i: (0, i))
    return pl.pallas_call(
        _slots_kernel, grid=(n // tm,), in_specs=[blk, _full(pad_start_col.shape)], out_specs=blk,
        out_shape=jax.ShapeDtypeStruct((SUBLANES, n), jnp.int32),
        compiler_params=_params("parallel"), name="slots",
    )(ri, pad_start_col)


def _slot(dest_ref, r, k):
    return dest_ref[k * ROW_TILE + r]


def _dispatch_kernel(dest_ref, last_ref, x_ref, xs_ref, scr_ref, zero_ref, sem, zsem):
    tm, d = x_ref.shape
    nch = _token_rows(d)
    tb = zero_ref.shape[0] // nch

    @pl.when(pl.program_id(0) == 0)
    def _():
        zero_ref[...] = jnp.zeros_like(zero_ref)

        def desc(tok):
            off = pl.multiple_of(jnp.maximum(tok, 0) * nch, nch)
            return pltpu.make_async_copy(zero_ref, xs_ref.at[pl.ds(off, tb * nch), :], zsem)

        def zstart(e, _):
            @pl.when(last_ref[e] >= 0)
            def _():
                desc(last_ref[e]).start()
            return 0

        def zwait(e, _):
            @pl.when(last_ref[e] >= 0)
            def _():
                desc(last_ref[e]).wait()
            return 0

        lax.fori_loop(0, MOE_EXPERTS, zstart, 0)
        nused = last_ref[MOE_EXPERTS]
        nblk = xs_ref.shape[0] // (tb * nch)
        lax.fori_loop(nused, nblk, lambda b, _: (desc(b * tb).start(), 0)[1], 0)
        lax.fori_loop(0, MOE_EXPERTS, zwait, 0)
        lax.fori_loop(nused, nblk, lambda b, _: (desc(b * tb).wait(), 0)[1], 0)

    step = pl.program_id(0)
    slot = step % 2
    scr = scr_ref.at[slot]
    _to_token_tiles(scr, x_ref[...])

    def start(r, _):
        for k in range(2):
            _token_copy(scr, r, xs_ref, _slot(dest_ref, r, k), nch, sem.at[slot]).start(priority=k)
        return 0

    def drain(which):
        def wait(r, _):
            for k in range(2):
                _token_copy(scr_ref.at[which], 0, xs_ref, 0, nch, sem.at[which]).wait()
            return 0
        lax.fori_loop(0, tm, wait, 0, unroll=8)

    lax.fori_loop(0, tm, start, 0, unroll=8)

    @pl.when(step > 0)
    def _():
        drain(1 - slot)

    @pl.when(step == pl.num_programs(0) - 1)
    def _():
        drain(slot)


def _dispatch(dest, last_blk, x1, n_rows):
    n, d = x1.shape
    tm = ROW_TILE
    nch = _token_rows(d)
    return pl.pallas_call(
        _dispatch_kernel, grid=(n // tm,),
        in_specs=[pl.BlockSpec((2 * tm,), lambda i: (i,), memory_space=pltpu.SMEM),
                  pl.BlockSpec(memory_space=pltpu.SMEM),
                  pl.BlockSpec((tm, d), lambda i: (i, 0))],
        out_specs=pl.BlockSpec(memory_space=pl.ANY),
        out_shape=jax.ShapeDtypeStruct((n_rows * nch, LANES), F32),
        scratch_shapes=[pltpu.VMEM((2, tm * nch, LANES), F32), pltpu.VMEM((EXPERT_TILE * nch, LANES), F32),
                        pltpu.SemaphoreType.DMA((2,)), pltpu.SemaphoreType.DMA(())],
        compiler_params=_params("arbitrary"), name="dispatch",
    )(dest, last_blk, x1)


def _expert_kernel(first_ref, count_ref, widx_ref, nused_ref, wg_ref, wu_ref, wd_ref, xs_ref, ys_ref,
                   wgb_ref, wub_ref, wdb_ref, xbuf_ref, ybuf_ref, in_sem, out_sem):
    del widx_ref
    e = pl.program_id(0)
    nused = nused_ref[0]
    d = wg_ref.shape[1]
    nch = _token_rows(d)
    rows = xbuf_ref.shape[1]
    tb = rows // nch

    def blk(ref, b):
        return ref.at[pl.ds(pl.multiple_of(b * rows, rows), rows), :]

    def in_copy(b, slot):
        return pltpu.make_async_copy(blk(xs_ref, b), xbuf_ref.at[slot], in_sem.at[slot])

    def out_copy(b, slot):
        return pltpu.make_async_copy(ybuf_ref.at[slot], blk(ys_ref, b), out_sem.at[slot])

    n_in = xbuf_ref.shape[0]

    @pl.when(e == 0)
    def _():
        for b0 in range(n_in - 1):
            @pl.when(b0 < nused)
            def _():
                in_copy(b0, b0).start()

    @pl.when(count_ref[e] > 0)
    def _():
        wgb_ref[...] = wg_ref[0].astype(BF16)
        wub_ref[...] = wu_ref[0].astype(BF16)
        wdb_ref[...] = wd_ref[0].astype(BF16)

    def body(b, _):
        slot = b % n_in
        oslot = b % 2
        in_copy(b, slot).wait()

        @pl.when(b + n_in - 1 < nused)
        def _():
            in_copy(b + n_in - 1, (b + n_in - 1) % n_in).start()

        @pl.when(b >= 2)
        def _():
            out_copy(b - 2, oslot).wait()

        xb = _from_token_tiles(xbuf_ref.at[slot], tb, d).astype(BF16)
        g = _dot(xb, wgb_ref[...])
        u = _dot(xb, wub_ref[...])
        hmid = g * jax.nn.sigmoid(g) * u
        _to_token_tiles(ybuf_ref.at[oslot], _dot(hmid.astype(BF16), wdb_ref[...]))
        out_copy(b, oslot).start()
        return 0

    lax.fori_loop(first_ref[e], first_ref[e] + count_ref[e], body, 0)

    @pl.when(e == pl.num_programs(0) - 1)
    def _():
        for back in (2, 1):
            @pl.when(nused >= back)
            def _():
                out_copy(nused - back, (nused - back) % 2).wait()


def _experts(first_blk, blk_count, w_idx, nused, xs, w_gate, w_up, w_down):
    n_exp, d, dff = w_gate.shape
    nch = _token_rows(d)
    rows = EXPERT_TILE * nch
    w_spec = lambda shape: pl.BlockSpec(shape, lambda e, fb, bc, wi, nu: (wi[e], 0, 0))
    any_spec = pl.BlockSpec(memory_space=pl.ANY)
    grid_spec = pltpu.PrefetchScalarGridSpec(
        num_scalar_prefetch=4, grid=(n_exp,),
        in_specs=[w_spec((1, d, dff)), w_spec((1, d, dff)), w_spec((1, dff, d)), any_spec],
        out_specs=any_spec,
        scratch_shapes=[pltpu.VMEM((d, dff), BF16), pltpu.VMEM((d, dff), BF16), pltpu.VMEM((dff, d), BF16),
                        pltpu.VMEM((EXPERT_IN_SLOTS, rows, LANES), F32), pltpu.VMEM((2, rows, LANES), F32),
                        pltpu.SemaphoreType.DMA((EXPERT_IN_SLOTS,)), pltpu.SemaphoreType.DMA((2,))],
    )
    return pl.pallas_call(
        _expert_kernel, grid_spec=grid_spec, out_shape=jax.ShapeDtypeStruct(xs.shape, F32),
        input_output_aliases={7: 0},
        compiler_params=_params("arbitrary"), name="experts",
    )(first_blk, blk_count, w_idx, nused, w_gate, w_up, w_down, xs)


def _combine_kernel(dest_ref, dest_next_ref, x1_ref, rw_ref, g_ref, b_ref, ys_ref, o_ref, buf_ref, sem):
    tm, d = x1_ref.shape
    nch = _token_rows(d)
    step = pl.program_id(0)
    slot = step % 2

    def gather(idx_ref, which):
        def start(r, _):
            for k in range(2):
                _token_copy(ys_ref, _slot(idx_ref, r, k), buf_ref.at[which, k], r, nch,
                            sem.at[which]).start(priority=k)
            return 0
        lax.fori_loop(0, tm, start, 0, unroll=8)

    @pl.when(step == 0)
    def _():
        gather(dest_ref, 0)

    @pl.when(step + 1 < pl.num_programs(0))
    def _():
        gather(dest_next_ref, 1 - slot)

    def wait(r, _):
        for k in range(2):
            _token_copy(ys_ref, 0, buf_ref.at[slot, k], 0, nch, sem.at[slot]).wait()
        return 0

    lax.fori_loop(0, tm, wait, 0, unroll=8)
    rw = rw_ref[...]
    y0 = _from_token_tiles(buf_ref.at[slot, 0], tm, d)
    y1 = _from_token_tiles(buf_ref.at[slot, 1], tm, d)
    ffn = rw[:, 0:1] * y0 + rw[:, 1:2] * y1
    o_ref[...] = _layer_norm(DEEPNORM_ALPHA * x1_ref[...] + ffn, g_ref[...], b_ref[...])


def _combine(dest, x1, rw, ln_g, ln_b, ys):
    n, d = x1.shape
    tm = ROW_TILE
    nch = _token_rows(d)
    last = n // tm - 1
    row = lambda w: pl.BlockSpec((tm, w), lambda i: (i, 0))
    return pl.pallas_call(
        _combine_kernel, grid=(n // tm,),
        in_specs=[pl.BlockSpec((2 * tm,), lambda i: (i,), memory_space=pltpu.SMEM),
                  pl.BlockSpec((2 * tm,), lambda i: (jnp.minimum(i + 1, last),), memory_space=pltpu.SMEM),
                  row(d), row(LANES), _full(ln_g.shape), _full(ln_b.shape),
                  pl.BlockSpec(memory_space=pl.ANY)],
        out_specs=row(d),
        out_shape=jax.ShapeDtypeStruct((n, d), F32),
        scratch_shapes=[pltpu.VMEM((2, 2, tm * nch, LANES), F32), pltpu.SemaphoreType.DMA((2,))],
        compiler_params=_params("arbitrary"), name="combine",
    )(dest, dest, x1, rw, ln_g, ln_b, ys)


def _rope_tables(seq):
    half = ATTN_HEAD_DIM // 2
    inv_freq = ROPE_THETA ** (-np.arange(half, dtype=np.float64) / half)
    ang = np.arange(seq, dtype=np.float64)[:, None] * inv_freq[None, :]
    cos = np.cos(ang)
    sin = np.sin(ang)
    cos_h = np.concatenate([cos, cos], axis=1)
    sin_h = np.concatenate([-sin, sin], axis=1)
    return (jnp.asarray(np.tile(cos_h, (1, ATTN_HEADS)), F32), jnp.asarray(np.tile(sin_h, (1, ATTN_HEADS)), F32))


def _pad_lanes(a, width=LANES):
    return jnp.pad(a, ((0, 0), (0, width - a.shape[1])))


def kernel(x, ln0_g, ln0_b, w_in, conv_w, conv_b, w_mq, w_mk, b_i, b_f, gn_g, skip, w_attn_up, w_mlstm_up, w_out,
           ln1_g, ln1_b, w_router_group, b_router_group, w_router_expert, b_router_expert, w_gate, w_up, w_down,
           ln2_g, ln2_b):
    batch, seq, d = x.shape
    n = batch * seq
    assert seq % ROW_TILE == 0 and ROW_TILE == MOBA_BLOCK and w_in.shape[0] == DEPTH
    x2 = x.reshape(n, d)
    vec = lambda a: a.reshape(1, -1).astype(F32)

    wt = w_in[0].T
    cos, sin = _rope_tables(seq)

    q, k, v, kmean, u, vm, o, ift, ga, gm, xn = _inproj(
        x2, vec(ln0_g), vec(ln0_b), wt, cos, sin, batch, seq)

    nb = seq // MOBA_BLOCK
    km = kmean.reshape(batch, nb, ATTN_HEADS, ATTN_HEAD_DIM).transpose(0, 2, 1, 3)
    ya = _moba(q, k, v, km).reshape(n, ATTN_WIDTH)

    b_if = jnp.concatenate([b_i[0], b_f[0]]).astype(F32)
    ym = _mlstm(u, vm, o, ift, conv_w[0], vec(conv_b[0]), w_mq[0].transpose(0, 2, 1).astype(BF16),
                w_mk[0].astype(BF16), b_if[:, None],
                gn_g[0].astype(F32)[:, None], skip[0].astype(F32)[:, None], batch, seq)

    w_r = _pad_lanes(jnp.concatenate([w_router_expert[0], w_router_group[0]], axis=1))
    w_r_hi = w_r.astype(BF16)
    w_r_lo = (w_r - w_r_hi.astype(F32)).astype(BF16)
    w_rc = jnp.concatenate([w_r_hi.T, w_r_lo.T], axis=0)
    b_r = _pad_lanes(jnp.concatenate([b_router_expert[0], b_router_group[0]])[None, :]).T
    x1, ri, rw, counts = _mix(
        xn, ya, ym, ga, gm, w_attn_up[0].astype(BF16), w_mlstm_up[0].astype(BF16),
        w_out[0].astype(BF16), vec(ln1_g[0]), vec(ln1_b[0]), w_rc, b_r)

    tb = EXPERT_TILE
    nblk = (2 * n) // tb + MOE_EXPERTS
    cnt = counts[:MOE_EXPERTS, 0].astype(jnp.int32)
    nblk_e = (cnt + tb - 1) // tb
    blk_end = jnp.cumsum(nblk_e)
    pad_start = (blk_end - nblk_e) * tb
    nused = blk_end[-1:]
    ids = jnp.arange(MOE_EXPERTS, dtype=jnp.int32)
    prev_used = jnp.max(jnp.where((ids[None, :] <= ids[:, None]) & (nblk_e[None, :] > 0), ids[None, :], -1), axis=1)
    first_used = jnp.min(jnp.where(nblk_e > 0, ids, MOE_EXPERTS - 1))
    w_idx = jnp.where(prev_used >= 0, prev_used, first_used).astype(jnp.int32)
    last_blk = jnp.where(nblk_e > 0, (blk_end - 1) * tb, -1)
    last_blk = jnp.concatenate([last_blk, nused]).astype(jnp.int32)
    dest = _slots(ri, pad_start.astype(F32)[:, None])
    dest = dest[:2].reshape(2, n // ROW_TILE, ROW_TILE).transpose(1, 0, 2).reshape(2 * n)

    xs = _dispatch(dest, last_blk, x1, nblk * tb)
    ys = _experts((blk_end - nblk_e).astype(jnp.int32), nblk_e.astype(jnp.int32), w_idx, nused.astype(jnp.int32),
                  xs, w_gate[0], w_up[0], w_down[0])
    out = _combine(dest, x1, rw, vec(ln2_g[0]), vec(ln2_b[0]), ys)
    return out.reshape(batch, seq, d)
```

```python
import math

import jax
import jax.numpy as jnp
import numpy as np
from jax import lax
from jax.experimental import pallas as pl
from jax.experimental.pallas import tpu as pltpu

F32 = jnp.float32
BF16 = jnp.bfloat16

ATTN_HEADS = 8
ATTN_HEAD_DIM = 64
ATTN_WIDTH = ATTN_HEADS * ATTN_HEAD_DIM
MOBA_BLOCK = 256
MOBA_TOPK = 3
ROPE_THETA = 10000.0
MLSTM_HEADS = 4
MLSTM_HEAD_DIM = 128
MLSTM_WIDTH = MLSTM_HEADS * MLSTM_HEAD_DIM
MLSTM_CONV = 4
MOE_GROUPS = 8
MOE_EXPERTS_PER_GROUP = 8
MOE_EXPERTS = MOE_GROUPS * MOE_EXPERTS_PER_GROUP
LN_EPS = 1e-5
GN_EPS = 1e-6
DEPTH = 1
DEEPNORM_ALPHA = (2 * DEPTH) ** 0.25

LANES = 128
SUBLANES = 8
ROW_TILE = 256
EXPERT_TILE = 256
EXPERT_IN_SLOTS = 6
INPROJ_CHAINS = 2
MLSTM_CHUNKS = 2
SLOT_TILE = 2048
MIX_CHAINS = 4
VMEM_LIMIT = 48 * 1024 * 1024
INPROJ_VMEM_LIMIT = 58 * 1024 * 1024
WEIGHT_CHUNK = 512
LOG2_E = math.log2(math.e)

NEG_INF = float("-inf")


def _params(*sem):
    return pltpu.CompilerParams(dimension_semantics=sem, vmem_limit_bytes=VMEM_LIMIT)


def _dot(a, b):
    return jnp.dot(a, b, preferred_element_type=F32)


def _dot_nt(a, b):
    return lax.dot_general(a, b, (((1,), (1,)), ((), ())), preferred_element_type=F32)


def _split3(x):
    x1 = x.astype(BF16)
    r1 = x - x1.astype(F32)
    x2 = r1.astype(BF16)
    r2 = r1 - x2.astype(F32)
    return x1, x2, r2.astype(BF16)


def _layer_norm(x, g, b):
    mu = jnp.mean(x, axis=-1, keepdims=True)
    xc = x - mu
    var = jnp.mean(xc * xc, axis=-1, keepdims=True)
    return xc * lax.rsqrt(var + LN_EPS) * g + b


def _log_sigmoid(x):
    return jnp.minimum(x, 0.0) - jnp.log1p(jnp.exp(-jnp.abs(x)))


def _full(shape):
    nd = len(shape)
    return pl.BlockSpec(shape, lambda *_: (0,) * nd)


def _run_skewed(phases, chains, rows):
    states = [dict() for _ in range(chains)]
    for t in range(chains + len(phases) - 1):
        for c in range(chains):
            if 0 <= t - c < len(phases):
                phases[t - c](states[c], c, slice(c * rows, (c + 1) * rows))


def _loop_groups(count, body, group=4):
    def trip(g, _):
        for d in range(group):
            body(g * group + d)
        return 0

    lax.fori_loop(0, count // group, trip, 0)
    done = (count // group) * group
    size = group // 2
    while size >= 1:
        take = ((count - done) // size) * size
        @pl.when(take > 0)
        def _(done=done, size=size):
            for d in range(size):
                body(done + d)
        done = done + take
        size //= 2


def _inproj_kernel(x_ref, g_ref, b_ref, wt_ref, cos_ref, sin_ref,
                   q_ref, k_ref, v_ref, km_ref, u_ref, vm_ref, o_ref, ift_ref, ga_ref, gm_ref, xn_ref,
                   wqkv_ref, wuvo_ref, wift_ref, wg_ref):
    tm = ROW_TILE
    lane = lax.broadcasted_iota(jnp.int32, (tm, ATTN_WIDTH), 1)
    first_half = (lane % ATTN_HEAD_DIM) < (ATTN_HEAD_DIM // 2)

    @pl.when(pl.program_id(0) == 0)
    def _():
        def fill(dst_ref, row0):
            for c in range(dst_ref.shape[1] // WEIGHT_CHUNK):
                cols = slice(c * WEIGHT_CHUNK, (c + 1) * WEIGHT_CHUNK)
                rows = slice(row0 + c * WEIGHT_CHUNK, row0 + (c + 1) * WEIGHT_CHUNK)
                dst_ref[:, cols] = wt_ref[rows, :].T.astype(BF16)

        c_if = 3 * ATTN_WIDTH + 3 * MLSTM_WIDTH
        fill(wqkv_ref, 0)
        fill(wuvo_ref, 3 * ATTN_WIDTH)
        wift_ref[...] = wt_ref[c_if:c_if + 2 * MLSTM_HEADS, :].astype(BF16)
        fill(wg_ref, c_if + 2 * MLSTM_HEADS)

    def norm(st, c, rs):
        xn = _layer_norm(x_ref[rs, :], g_ref[...], b_ref[...])
        xn_ref[rs, :] = xn
        st["xb"] = xn.astype(BF16)

    def qkv_matmul(st, c, rs):
        st["zqkv"] = _dot(st["xb"], wqkv_ref[...])

    def attn_outputs(st, c, rs):
        zqkv = st.pop("zqkv")
        cos = cos_ref[rs, :]
        sin = sin_ref[rs, :]

        def rope(t):
            fwd = pltpu.roll(t, ATTN_WIDTH - ATTN_HEAD_DIM // 2, axis=1)
            bwd = pltpu.roll(t, ATTN_HEAD_DIM // 2, axis=1)
            return t * cos + jnp.where(first_half, fwd, bwd) * sin

        q = rope(zqkv[:, :ATTN_WIDTH]) * (ATTN_HEAD_DIM ** -0.5 * LOG2_E)
        k = rope(zqkv[:, ATTN_WIDTH:2 * ATTN_WIDTH])
        v = zqkv[:, 2 * ATTN_WIDTH:]
        km_ref[c] = jnp.mean(k, axis=0, keepdims=True)
        qt = q.T
        vt = v.T
        for h in range(ATTN_HEADS):
            sl = slice(h * ATTN_HEAD_DIM, (h + 1) * ATTN_HEAD_DIM)
            q_ref[0, h, :, rs] = qt[sl, :].astype(BF16)
            k_ref[0, h, rs, :] = k[:, sl].astype(BF16)
            v_ref[0, h, :, rs] = vt[sl, :].astype(BF16)

    def uvo_matmul(st, c, rs):
        st["zuvo"] = _dot(st["xb"], wuvo_ref[...])

    def mlstm_outputs(st, c, rs):
        zuvo = st.pop("zuvo")
        u_ref[rs, :] = zuvo[:, :MLSTM_WIDTH]
        vm_ref[:, rs] = zuvo[:, MLSTM_WIDTH:2 * MLSTM_WIDTH].T.astype(BF16)
        o_ref[:, rs] = zuvo[:, 2 * MLSTM_WIDTH:].T
        ift_ref[:, rs] = _dot_nt(wift_ref[...], st["xb"])

    def gate_matmul(st, c, rs):
        st["zg"] = _dot(st.pop("xb"), wg_ref[...])

    def gate_outputs(st, c, rs):
        zg = st.pop("zg")
        d = ga_ref.shape[1]
        ga_ref[rs, :] = jax.nn.sigmoid(zg[:, :d]).astype(BF16)
        gm_ref[rs, :] = jax.nn.sigmoid(zg[:, d:]).astype(BF16)

    _run_skewed((norm, qkv_matmul, attn_outputs, uvo_matmul, mlstm_outputs, gate_matmul, gate_outputs),
                x_ref.shape[0] // tm, tm)


def _inproj(x2, ln_g, ln_b, wt, cos, sin, batch, seq):
    n, d = x2.shape
    chains = INPROJ_CHAINS
    tm = chains * ROW_TILE
    assert seq % tm == 0
    nsb = seq // tm
    hd = ATTN_HEAD_DIM
    row = lambda w: pl.BlockSpec((tm, w), lambda i: (i, 0))
    col = lambda h: pl.BlockSpec((h, tm), lambda i: (0, i))
    head = pl.BlockSpec((1, ATTN_HEADS, tm, hd), lambda i: (i // nsb, 0, i % nsb, 0))
    head_t = pl.BlockSpec((1, ATTN_HEADS, hd, tm), lambda i: (i // nsb, 0, 0, i % nsb))
    tab = pl.BlockSpec((tm, ATTN_WIDTH), lambda i: (i % nsb, 0))
    head_shape = jax.ShapeDtypeStruct((batch, ATTN_HEADS, seq, hd), BF16)
    head_t_shape = jax.ShapeDtypeStruct((batch, ATTN_HEADS, hd, seq), BF16)
    out_shape = (
        head_t_shape, head_shape, head_t_shape,
        jax.ShapeDtypeStruct((n // ROW_TILE, 1, ATTN_WIDTH), F32),
        jax.ShapeDtypeStruct((n, MLSTM_WIDTH), F32),
        jax.ShapeDtypeStruct((MLSTM_WIDTH, n), BF16),
        jax.ShapeDtypeStruct((MLSTM_WIDTH, n), F32),
        jax.ShapeDtypeStruct((SUBLANES, n), F32),
        jax.ShapeDtypeStruct((n, d), BF16),
        jax.ShapeDtypeStruct((n, d), BF16),
        jax.ShapeDtypeStruct((n, d), F32),
    )
    out_specs = (
        head_t, head, head_t,
        pl.BlockSpec((chains, 1, ATTN_WIDTH), lambda i: (i, 0, 0)),
        row(MLSTM_WIDTH), col(MLSTM_WIDTH), col(MLSTM_WIDTH),
        col(SUBLANES),
        row(d), row(d), row(d),
    )
    wt_spec = pl.BlockSpec(wt.shape, lambda i: (0, 0), pipeline_mode=pl.Buffered(1))
    in_specs = [row(d), _full(ln_g.shape), _full(ln_b.shape), wt_spec, tab, tab]
    return pl.pallas_call(
        _inproj_kernel, grid=(n // tm,), in_specs=in_specs, out_specs=out_specs, out_shape=out_shape,
        scratch_shapes=[pltpu.VMEM((d, 3 * ATTN_WIDTH), BF16), pltpu.VMEM((d, 3 * MLSTM_WIDTH), BF16),
                        pltpu.VMEM((2 * MLSTM_HEADS, d), BF16), pltpu.VMEM((d, 2 * d), BF16)],
        compiler_params=pltpu.CompilerParams(dimension_semantics=("arbitrary",), vmem_limit_bytes=INPROJ_VMEM_LIMIT),
        name="inproj",
    )(x2, ln_g, ln_b, wt, cos, sin)


def _moba_kernel(qt_ref, k_ref, vt_ref, km_ref, o_ref, bias_ref, m_ref, l_ref, acc_ref, s_ref):
    i = pl.program_id(1)
    blk = MOBA_BLOCK
    hd = ATTN_HEAD_DIM
    heads = ATTN_HEADS
    nb = k_ref.shape[2] // blk
    blk_id = lax.broadcasted_iota(jnp.int32, (nb, blk), 0)
    key_pos = lax.broadcasted_iota(jnp.int32, (blk, blk), 0)
    qry_pos = lax.broadcasted_iota(jnp.int32, (blk, blk), 1)
    causal = key_pos <= qry_pos

    for h in range(heads):
        qt = qt_ref[0, h]
        km = km_ref[0, h]
        km_hi = km.astype(BF16)
        km_lo = (km - km_hi.astype(F32)).astype(BF16)
        gate = _dot(km_hi, qt) + _dot(km_lo, qt)
        gate = jnp.where(blk_id < i, gate, NEG_INF)
        for j in range(nb - 1):
            row = gate[j:j + 1, :]
            beats = (gate > row) | ((gate == row) & (blk_id < j))
            cnt = jnp.sum(jnp.where(beats, 1.0, 0.0), axis=0, keepdims=True)
            sel = (cnt < float(MOBA_TOPK)) & (row > NEG_INF)
            bias_ref[j * heads + h] = jnp.where(sel, 0.0, NEG_INF)
    for h in range(heads):
        bias_ref[i * heads + h] = jnp.zeros((1, blk), F32)

    def scores(h, j, own_block):
        qt = qt_ref[0, h]
        half = blk // 2
        m_tile = None
        for c in range(2):
            rows = slice(c * half, (c + 1) * half)
            s = _dot(k_ref[0, h, pl.ds(pl.multiple_of(j * blk + c * half, half), half), :], qt)
            if own_block:
                s = jnp.where(causal[rows], s, NEG_INF)
            s_ref[j * heads + h, rows, :] = s
            m_c = jnp.max(s, axis=0, keepdims=True)
            m_tile = m_c if m_tile is None else jnp.maximum(m_tile, m_c)
        return m_tile

    for h in range(heads):
        m_ref[h] = scores(h, i, True)

    def past_scores(j):
        for h in range(heads):
            m_ref[h] = jnp.maximum(m_ref[h], scores(h, j, False) + bias_ref[j * heads + h])

    _loop_groups(i, past_scores)

    l_ref[...] = jnp.zeros_like(l_ref)
    acc_ref[...] = jnp.zeros_like(acc_ref)

    def accumulate(j):
        off = pl.multiple_of(j * blk, blk)
        for h in range(heads):
            p = jnp.exp2(s_ref[j * heads + h] - (m_ref[h] - bias_ref[j * heads + h]))
            l_ref[h] += jnp.sum(p, axis=0, keepdims=True)
            acc_ref[h] += _dot(vt_ref[0, h, :, pl.ds(off, blk)], p.astype(BF16))

    _loop_groups(i + 1, accumulate)
    yt = acc_ref[...] / l_ref[...]
    o_ref[0] = yt.reshape(heads * hd, blk).T.astype(BF16)


def _moba(qt, k, vt, km):
    batch, heads, seq, hd = k.shape
    blk = MOBA_BLOCK
    nb = seq // blk
    return pl.pallas_call(
        _moba_kernel, grid=(batch, nb),
        in_specs=[
            pl.BlockSpec((1, heads, hd, blk), lambda b, i: (b, 0, 0, i)),
            pl.BlockSpec((1, heads, seq, hd), lambda b, i: (b, 0, 0, 0)),
            pl.BlockSpec((1, heads, hd, seq), lambda b, i: (b, 0, 0, 0)),
            pl.BlockSpec((1, heads, nb, hd), lambda b, i: (b, 0, 0, 0)),
        ],
        out_specs=pl.BlockSpec((1, blk, heads * hd), lambda b, i: (b, i, 0)),
        out_shape=jax.ShapeDtypeStruct((batch, seq, heads * hd), BF16),
        scratch_shapes=[pltpu.VMEM((nb * heads, 1, blk), F32), pltpu.VMEM((heads, 1, blk), F32),
                        pltpu.VMEM((heads, 1, blk), F32), pltpu.VMEM((heads, hd, blk), F32),
                        pltpu.VMEM((nb * heads, blk, blk), F32)],
        compiler_params=_params("parallel", "arbitrary"), name="moba",
    )(qt, k, vt, km)


def _mlstm_kernel(u_ref, vmt_ref, ot_ref, ift_ref, cw_ref, cb_ref, wqt_ref, wk_ref, bcol_ref,
                  gn_ref, skip_ref, y_ref, ext_ref, c_ref, n_ref, m_ref, yt_ref):
    @pl.when(pl.program_id(1) == 0)
    def _():
        ext_ref[0:SUBLANES, :] = jnp.zeros((SUBLANES, MLSTM_WIDTH), F32)
        c_ref[...] = jnp.zeros_like(c_ref)
        n_ref[...] = jnp.zeros_like(n_ref)
        m_ref[...] = jnp.zeros_like(m_ref)

    for c in range(u_ref.shape[0] // ROW_TILE):
        _mlstm_chunk(slice(c * ROW_TILE, (c + 1) * ROW_TILE), u_ref, vmt_ref, ot_ref, ift_ref, cw_ref, cb_ref,
                     wqt_ref, wk_ref, bcol_ref, gn_ref, skip_ref, y_ref, ext_ref, c_ref, n_ref, m_ref, yt_ref)


def _mlstm_chunk(rs, u_ref, vmt_ref, ot_ref, ift_ref, cw_ref, cb_ref, wqt_ref, wk_ref, bcol_ref,
                 gn_ref, skip_ref, y_ref, ext_ref, c_ref, n_ref, m_ref, yt_ref):
    tm = ROW_TILE
    hd = MLSTM_HEAD_DIM
    halo = SUBLANES
    u = u_ref[rs, :]
    ext_ref[halo:halo + tm, :] = u
    acc = jnp.broadcast_to(cb_ref[...], u.shape)
    for j in range(MLSTM_CONV):
        acc = acc + cw_ref[j:j + 1, :] * ext_ref[halo - (MLSTM_CONV - 1) + j:halo - (MLSTM_CONV - 1) + j + tm, :]
    ext_ref[0:halo, :] = u[tm - halo:, :]
    uc = acc * jax.nn.sigmoid(acc)

    gr = ift_ref[:, rs] + bcol_ref[...]
    rows = lax.broadcasted_iota(jnp.int32, (tm, tm), 0)
    cols = lax.broadcasted_iota(jnp.int32, (tm, tm), 1)
    causal_t = rows <= cols
    triu = jnp.where(causal_t, 1.0, 0.0).astype(BF16)
    r1, r2, r3 = _split3(_log_sigmoid(gr))
    bcum_r = _dot(r1, triu) + _dot(r2, triu) + _dot(r3, triu)
    key_rows = gr[:MLSTM_HEADS, :] - bcum_r[MLSTM_HEADS:, :]
    key_cols = jnp.concatenate([key_rows, jnp.zeros((LANES - MLSTM_HEADS, tm), F32)], axis=0).T

    uct = uc.T

    def decay_weights(st, h, hs):
        fl = MLSTM_HEADS + h
        b_row = bcum_r[fl:fl + 1, :]
        st["key_row"] = key_rows[h:h + 1, :]
        st["key_col"] = key_cols[:, h:h + 1]
        m_prev = m_ref[h][:, 0:1]
        dlog = jnp.where(causal_t, st["key_col"] + b_row, NEG_INF)
        inter = b_row + m_prev
        m_t = jnp.maximum(inter, jnp.max(dlog, axis=0, keepdims=True))
        st["w_intra"] = jnp.exp(dlog - m_t)
        st["w_inter"] = jnp.exp(inter - m_t)
        st["m_t"], st["m_prev"], st["b_end"] = m_t, m_prev, b_row[:, tm - 1:tm]

    def project(st, h, hs):
        st["qtb"] = _dot(wqt_ref[h], uct[hs, :].astype(BF16)).astype(BF16)
        st["k"] = _dot(uc[:, hs].astype(BF16), wk_ref[h]) * (hd ** -0.5)

    def scores(st, h, hs):
        st["s"] = _dot(st["k"].astype(BF16), st["qtb"]) * st.pop("w_intra")

    def readout(st, h, hs):
        qtb, s, w_inter, m_t = st.pop("qtb"), st.pop("s"), st.pop("w_inter"), st.pop("m_t")
        n_prev = n_ref[h]
        n_hi = n_prev.astype(BF16)
        n_lo = (n_prev - n_hi.astype(F32)).astype(BF16)
        qn = (_dot(n_hi, qtb) + _dot(n_lo, qtb))[0:1, :]
        num = w_inter * _dot(c_ref[h].astype(BF16), qtb) + _dot(vmt_ref[hs, rs], s.astype(BF16))
        den = w_inter * qn + jnp.sum(s, axis=0, keepdims=True)
        st["hh"] = num / jnp.maximum(jnp.abs(den), jnp.exp(-m_t))

    def update_state(st, h, hs):
        b_end, m_prev = st.pop("b_end"), st.pop("m_prev")
        m_new = jnp.maximum(b_end + m_prev, jnp.max(b_end + st.pop("key_row"), axis=1, keepdims=True))
        decay = jnp.exp(b_end + m_prev - m_new)
        kw = st.pop("k") * jnp.exp(b_end + st.pop("key_col") - m_new)
        n_prev = n_ref[h]
        c_ref[h] = decay * c_ref[h] + _dot(vmt_ref[hs, rs], kw.astype(BF16))
        n_ref[h] = decay * n_prev + jnp.broadcast_to(jnp.sum(kw, axis=0, keepdims=True), n_prev.shape)
        m_ref[h] = jnp.broadcast_to(m_new, (1, LANES))

    def gate_and_norm(st, h, hs):
        hh = jax.nn.sigmoid(ot_ref[hs, rs]) * st.pop("hh")
        mu = jnp.mean(hh, axis=0, keepdims=True)
        hc = hh - mu
        var = jnp.mean(hc * hc, axis=0, keepdims=True)
        yt_ref[hs, :] = hc * lax.rsqrt(var + GN_EPS) * gn_ref[hs, :] + skip_ref[hs, :] * uct[hs, :]

    _run_skewed((decay_weights, project, scores, readout, update_state, gate_and_norm), MLSTM_HEADS, hd)
    y_ref[rs, :] = yt_ref[...].T.astype(BF16)


def _mlstm(u, vmt, ot, ift, conv_w, conv_b, wqt, wk, bcol, gn_g, skip, batch, seq):
    n = u.shape[0]
    tm = MLSTM_CHUNKS * ROW_TILE
    assert seq % tm == 0
    nc = seq // tm
    row = lambda w: pl.BlockSpec((tm, w), lambda b, c: (b * nc + c, 0))
    col = lambda h: pl.BlockSpec((h, tm), lambda b, c: (0, b * nc + c))
    in_specs = [row(MLSTM_WIDTH), col(MLSTM_WIDTH), col(MLSTM_WIDTH), col(SUBLANES),
                _full(conv_w.shape), _full(conv_b.shape), _full(wqt.shape), _full(wk.shape),
                _full(bcol.shape), _full(gn_g.shape), _full(skip.shape)]
    return pl.pallas_call(
        _mlstm_kernel, grid=(batch, nc), in_specs=in_specs, out_specs=row(MLSTM_WIDTH),
        out_shape=jax.ShapeDtypeStruct((n, MLSTM_WIDTH), BF16),
        scratch_shapes=[pltpu.VMEM((SUBLANES + ROW_TILE, MLSTM_WIDTH), F32),
                        pltpu.VMEM((MLSTM_HEADS, MLSTM_HEAD_DIM, MLSTM_HEAD_DIM), F32),
                        pltpu.VMEM((MLSTM_HEADS, SUBLANES, MLSTM_HEAD_DIM), F32),
                        pltpu.VMEM((MLSTM_HEADS, 1, LANES), F32),
                        pltpu.VMEM((MLSTM_WIDTH, ROW_TILE), F32)],
        compiler_params=_params("parallel", "arbitrary"), name="mlstm",
    )(u, vmt, ot, ift, conv_w, conv_b, wqt, wk, bcol, gn_g, skip)


def _mix_kernel(xn_ref, ya_ref, ym_ref, ga_ref, gm_ref, wau_ref, wmu_ref, wout_ref,
                g1_ref, b1_ref, wrc_ref, br_ref,
                x1_ref, ri_ref, rw_ref, cnt_out_ref, cnt_ref):
    @pl.when(pl.program_id(0) == 0)
    def _():
        cnt_ref[...] = jnp.zeros_like(cnt_ref)

    tm = ROW_TILE
    sub = lax.broadcasted_iota(jnp.int32, (LANES, tm), 0).astype(F32)
    big = float(4 * LANES)

    def up_and_mix(st, c, rs):
        a_up = _dot(ya_ref[rs, :], wau_ref[...])
        m_up = _dot(ym_ref[rs, :], wmu_ref[...])
        mix = ga_ref[rs, :].astype(F32) * a_up + gm_ref[rs, :].astype(F32) * m_up
        st["mix"] = mix.astype(BF16)

    def out_and_norm(st, c, rs):
        x1 = _layer_norm(DEEPNORM_ALPHA * xn_ref[rs, :] + _dot(st.pop("mix"), wout_ref[...]), g1_ref[...], b1_ref[...])
        x1_ref[rs, :] = x1
        st["x1"] = x1

    def router_logits(st, c, rs):
        x1 = st.pop("x1")
        x_hi = x1.astype(BF16)
        x_lo = (x1 - x_hi.astype(F32)).astype(BF16)
        both = _dot_nt(wrc_ref[...], x_hi)
        st["logits"] = both[:LANES] + both[LANES:] + _dot_nt(wrc_ref[:LANES, :], x_lo) + br_ref[...]

    def route(st, c, rs):
        logits = st.pop("logits")
        is_g = (sub >= float(MOE_EXPERTS)) & (sub < float(MOE_EXPERTS + MOE_GROUPS))
        gl = jnp.where(is_g, logits, NEG_INF)
        ge = jnp.exp(gl - jnp.max(gl, axis=0, keepdims=True))
        gp = ge / jnp.sum(ge, axis=0, keepdims=True)
        g_w = jnp.max(gp, axis=0, keepdims=True)
        g_idx = jnp.min(jnp.where((gp == g_w) & is_g, sub - float(MOE_EXPERTS), big), axis=0, keepdims=True)
        lo = g_idx * float(MOE_EXPERTS_PER_GROUP)
        in_grp = (sub >= lo) & (sub < lo + float(MOE_EXPERTS_PER_GROUP))
        el = jnp.where(in_grp, logits, NEG_INF)
        v1 = jnp.max(el, axis=0, keepdims=True)
        i1 = jnp.min(jnp.where((el == v1) & in_grp, sub, big), axis=0, keepdims=True)
        el2 = jnp.where(sub == i1, NEG_INF, el)
        v2 = jnp.max(el2, axis=0, keepdims=True)
        i2 = jnp.min(jnp.where((el2 == v2) & in_grp & (sub != i1), sub, big), axis=0, keepdims=True)
        e2 = jnp.exp(v2 - v1)
        w0 = g_w / (1.0 + e2)
        w1 = g_w * e2 / (1.0 + e2)
        rw_ref[rs, :] = jnp.where(sub == 0.0, w0, jnp.where(sub == 1.0, w1, 0.0)).T
        st["i1"], st["i2"] = i1, i2

    def rank(st, c, rs):
        i1, i2 = st.pop("i1"), st.pop("i2")
        is1 = sub == i1
        is2 = sub == i2
        onehot = jnp.where(is1 | is2, 1.0, 0.0)
        rows = lax.broadcasted_iota(jnp.int32, (tm, tm), 0)
        cols = lax.broadcasted_iota(jnp.int32, (tm, tm), 1)
        earlier = jnp.where(rows < cols, 1.0, 0.0).astype(BF16)
        before = _dot(onehot.astype(BF16), earlier) + cnt_ref[...]
        r0 = jnp.sum(jnp.where(is1, before, 0.0), axis=0, keepdims=True)
        r1 = jnp.sum(jnp.where(is2, before, 0.0), axis=0, keepdims=True)
        total = cnt_ref[...] + jnp.sum(onehot, axis=1, keepdims=True)
        cnt_ref[...] = total
        cnt_out_ref[...] = total
        ri_t = jnp.where(sub == 0.0, i1, jnp.where(sub == 1.0, i2, jnp.where(sub == 2.0, r0, jnp.where(sub == 3.0, r1, 0.0))))
        ri_ref[:, rs] = ri_t[:SUBLANES, :].astype(jnp.int32)

    _run_skewed((up_and_mix, out_and_norm, router_logits, route, rank), xn_ref.shape[0] // tm, tm)


def _mix(xn, ya, ym, ga, gm, wau, wmu, wout, g1, b1, wrc, br):
    n, d = xn.shape
    tm = MIX_CHAINS * ROW_TILE
    row = lambda w: pl.BlockSpec((tm, w), lambda i: (i, 0))
    in_specs = [row(d), row(ATTN_WIDTH), row(MLSTM_WIDTH), row(d), row(d),
                _full(wau.shape), _full(wmu.shape), _full(wout.shape), _full(g1.shape), _full(b1.shape),
                _full(wrc.shape), _full(br.shape)]
    out_shape = (jax.ShapeDtypeStruct((n, d), F32), jax.ShapeDtypeStruct((SUBLANES, n), jnp.int32),
                 jax.ShapeDtypeStruct((n, LANES), F32), jax.ShapeDtypeStruct((LANES, 1), F32))
    out_specs = (row(d), pl.BlockSpec((SUBLANES, tm), lambda i: (0, i)), row(LANES), _full((LANES, 1)))
    return pl.pallas_call(
        _mix_kernel, grid=(n // tm,), in_specs=in_specs, out_specs=out_specs, out_shape=out_shape,
        scratch_shapes=[pltpu.VMEM((LANES, 1), F32)],
        compiler_params=_params("arbitrary"), name="mix",
    )(xn, ya, ym, ga, gm, wau, wmu, wout, g1, b1, wrc, br)


def _token_rows(d):
    return d // LANES


def _to_token_tiles(dst_ref, x):
    rows, d = x.shape
    nch = _token_rows(d)
    for c in range(nch):
        dst_ref[pl.ds(c, rows, stride=nch), :] = x[:, c * LANES:(c + 1) * LANES]


def _from_token_tiles(src_ref, rows, d):
    nch = _token_rows(d)
    return jnp.concatenate([src_ref[pl.ds(c, rows, stride=nch), :] for c in range(nch)], axis=1)


def _token_copy(src, src_tok, dst, dst_tok, nch, sem):
    s0 = pl.multiple_of(src_tok * nch, nch)
    d0 = pl.multiple_of(dst_tok * nch, nch)
    return pltpu.make_async_copy(src.at[pl.ds(s0, nch), :], dst.at[pl.ds(d0, nch), :], sem)


def _slots_kernel(ri_ref, ps_ref, o_ref):
    ri = ri_ref[...].astype(F32)
    ps = ps_ref[...]
    expert = lax.broadcasted_iota(jnp.int32, (ps.shape[0], ri.shape[1]), 0).astype(F32)
    row_id = lax.broadcasted_iota(jnp.int32, ri.shape, 0)
    out = jnp.zeros(ri.shape, F32)
    for k in range(2):
        start = jnp.sum(jnp.where(expert == ri[k:k + 1, :], jnp.broadcast_to(ps, expert.shape), 0.0),
                        axis=0, keepdims=True)
        out = jnp.where(row_id == k, start + ri[2 + k:3 + k, :], out)
    o_ref[...] = out.astype(jnp.int32)


def _slots(ri, pad_start_col):
    n = ri.shape[1]
    tm = SLOT_TILE
    blk = pl.BlockSpec((SUBLANES, tm), lambda i: (0, i))
    return pl.pallas_call(
        _slots_kernel, grid=(n // tm,), in_specs=[blk, _full(pad_start_col.shape)], out_specs=blk,
        out_shape=jax.ShapeDtypeStruct((SUBLANES, n), jnp.int32),
        compiler_params=_params("parallel"), name="slots",
    )(ri, pad_start_col)


def _slot(dest_ref, r, k):
    return dest_ref[k * ROW_TILE + r]


def _dispatch_kernel(dest_ref, last_ref, x_ref, xs_ref, scr_ref, zero_ref, sem, zsem):
    tm, d = x_ref.shape
    nch = _token_rows(d)
    tb = zero_ref.shape[0] // nch

    @pl.when(pl.program_id(0) == 0)
    def _():
        zero_ref[...] = jnp.zeros_like(zero_ref)

        def desc(tok):
            off = pl.multiple_of(jnp.maximum(tok, 0) * nch, nch)
            return pltpu.make_async_copy(zero_ref, xs_ref.at[pl.ds(off, tb * nch), :], zsem)

        def zstart(e, _):
            @pl.when(last_ref[e] >= 0)
            def _():
                desc(last_ref[e]).start()
            return 0

        def zwait(e, _):
            @pl.when(last_ref[e] >= 0)
            def _():
                desc(last_ref[e]).wait()
            return 0

        lax.fori_loop(0, MOE_EXPERTS, zstart, 0)
        nused = last_ref[MOE_EXPERTS]
        nblk = xs_ref.shape[0] // (tb * nch)
        lax.fori_loop(nused, nblk, lambda b, _: (desc(b * tb).start(), 0)[1], 0)
        lax.fori_loop(0, MOE_EXPERTS, zwait, 0)
        lax.fori_loop(nused, nblk, lambda b, _: (desc(b * tb).wait(), 0)[1], 0)

    step = pl.program_id(0)
    slot = step % 2
    scr = scr_ref.at[slot]
    _to_token_tiles(scr, x_ref[...])

    def start(r, _):
        for k in range(2):
            _token_copy(scr, r, xs_ref, _slot(dest_ref, r, k), nch, sem.at[slot]).start(priority=k)
        return 0

    def drain(which):
        def wait(r, _):
            for k in range(2):
                _token_copy(scr_ref.at[which], 0, xs_ref, 0, nch, sem.at[which]).wait()
            return 0
        lax.fori_loop(0, tm, wait, 0, unroll=8)

    lax.fori_loop(0, tm, start, 0, unroll=8)

    @pl.when(step > 0)
    def _():
        drain(1 - slot)

    @pl.when(step == pl.num_programs(0) - 1)
    def _():
        drain(slot)


def _dispatch(dest, last_blk, x1, n_rows):
    n, d = x1.shape
    tm = ROW_TILE
    nch = _token_rows(d)
    return pl.pallas_call(
        _dispatch_kernel, grid=(n // tm,),
        in_specs=[pl.BlockSpec((2 * tm,), lambda i: (i,), memory_space=pltpu.SMEM),
                  pl.BlockSpec(memory_space=pltpu.SMEM),
                  pl.BlockSpec((tm, d), lambda i: (i, 0))],
        out_specs=pl.BlockSpec(memory_space=pl.ANY),
        out_shape=jax.ShapeDtypeStruct((n_rows * nch, LANES), F32),
        scratch_shapes=[pltpu.VMEM((2, tm * nch, LANES), F32), pltpu.VMEM((EXPERT_TILE * nch, LANES), F32),
                        pltpu.SemaphoreType.DMA((2,)), pltpu.SemaphoreType.DMA(())],
        compiler_params=_params("arbitrary"), name="dispatch",
    )(dest, last_blk, x1)


def _expert_kernel(first_ref, count_ref, widx_ref, nused_ref, wg_ref, wu_ref, wd_ref, xs_ref, ys_ref,
                   wgb_ref, wub_ref, wdb_ref, xbuf_ref, ybuf_ref, in_sem, out_sem):
    del widx_ref
    e = pl.program_id(0)
    nused = nused_ref[0]
    d = wg_ref.shape[1]
    nch = _token_rows(d)
    rows = xbuf_ref.shape[1]
    tb = rows // nch

    def blk(ref, b):
        return ref.at[pl.ds(pl.multiple_of(b * rows, rows), rows), :]

    def in_copy(b, slot):
        return pltpu.make_async_copy(blk(xs_ref, b), xbuf_ref.at[slot], in_sem.at[slot])

    def out_copy(b, slot):
        return pltpu.make_async_copy(ybuf_ref.at[slot], blk(ys_ref, b), out_sem.at[slot])

    n_in = xbuf_ref.shape[0]

    @pl.when(e == 0)
    def _():
        for b0 in range(n_in - 1):
            @pl.when(b0 < nused)
            def _():
                in_copy(b0, b0).start()

    @pl.when(count_ref[e] > 0)
    def _():
        wgb_ref[...] = wg_ref[0].astype(BF16)
        wub_ref[...] = wu_ref[0].astype(BF16)
        wdb_ref[...] = wd_ref[0].astype(BF16)

    def body(b, _):
        slot = b % n_in
        oslot = b % 2
        in_copy(b, slot).wait()

        @pl.when(b + n_in - 1 < nused)
        def _():
            in_copy(b + n_in - 1, (b + n_in - 1) % n_in).start()

        @pl.when(b >= 2)
        def _():
            out_copy(b - 2, oslot).wait()

        xb = _from_token_tiles(xbuf_ref.at[slot], tb, d).astype(BF16)
        g = _dot(xb, wgb_ref[...])
        u = _dot(xb, wub_ref[...])
        hmid = g * jax.nn.sigmoid(g) * u
        _to_token_tiles(ybuf_ref.at[oslot], _dot(hmid.astype(BF16), wdb_ref[...]))
        out_copy(b, oslot).start()
        return 0

    lax.fori_loop(first_ref[e], first_ref[e] + count_ref[e], body, 0)

    @pl.when(e == pl.num_programs(0) - 1)
    def _():
        for back in (2, 1):
            @pl.when(nused >= back)
            def _():
                out_copy(nused - back, (nused - back) % 2).wait()


def _experts(first_blk, blk_count, w_idx, nused, xs, w_gate, w_up, w_down):
    n_exp, d, dff = w_gate.shape
    nch = _token_rows(d)
    rows = EXPERT_TILE * nch
    w_spec = lambda shape: pl.BlockSpec(shape, lambda e, fb, bc, wi, nu: (wi[e], 0, 0))
    any_spec = pl.BlockSpec(memory_space=pl.ANY)
    grid_spec = pltpu.PrefetchScalarGridSpec(
        num_scalar_prefetch=4, grid=(n_exp,),
        in_specs=[w_spec((1, d, dff)), w_spec((1, d, dff)), w_spec((1, dff, d)), any_spec],
        out_specs=any_spec,
        scratch_shapes=[pltpu.VMEM((d, dff), BF16), pltpu.VMEM((d, dff), BF16), pltpu.VMEM((dff, d), BF16),
                        pltpu.VMEM((EXPERT_IN_SLOTS, rows, LANES), F32), pltpu.VMEM((2, rows, LANES), F32),
                        pltpu.SemaphoreType.DMA((EXPERT_IN_SLOTS,)), pltpu.SemaphoreType.DMA((2,))],
    )
    return pl.pallas_call(
        _expert_kernel, grid_spec=grid_spec, out_shape=jax.ShapeDtypeStruct(xs.shape, F32),
        input_output_aliases={7: 0},
        compiler_params=_params("arbitrary"), name="experts",
    )(first_blk, blk_count, w_idx, nused, w_gate, w_up, w_down, xs)


def _combine_kernel(dest_ref, dest_next_ref, x1_ref, rw_ref, g_ref, b_ref, ys_ref, o_ref, buf_ref, sem):
    tm, d = x1_ref.shape
    nch = _token_rows(d)
    step = pl.program_id(0)
    slot = step % 2

    def gather(idx_ref, which):
        def start(r, _):
            for k in range(2):
                _token_copy(ys_ref, _slot(idx_ref, r, k), buf_ref.at[which, k], r, nch,
                            sem.at[which]).start(priority=k)
            return 0
        lax.fori_loop(0, tm, start, 0, unroll=8)

    @pl.when(step == 0)
    def _():
        gather(dest_ref, 0)

    @pl.when(step + 1 < pl.num_programs(0))
    def _():
        gather(dest_next_ref, 1 - slot)

    def wait(r, _):
        for k in range(2):
            _token_copy(ys_ref, 0, buf_ref.at[slot, k], 0, nch, sem.at[slot]).wait()
        return 0

    lax.fori_loop(0, tm, wait, 0, unroll=8)
    rw = rw_ref[...]
    y0 = _from_token_tiles(buf_ref.at[slot, 0], tm, d)
    y1 = _from_token_tiles(buf_ref.at[slot, 1], tm, d)
    ffn = rw[:, 0:1] * y0 + rw[:, 1:2] * y1
    o_ref[...] = _layer_norm(DEEPNORM_ALPHA * x1_ref[...] + ffn, g_ref[...], b_ref[...])


def _combine(dest, x1, rw, ln_g, ln_b, ys):
    n, d = x1.shape
    tm = ROW_TILE
    nch = _token_rows(d)
    last = n // tm - 1
    row = lambda w: pl.BlockSpec((tm, w), lambda i: (i, 0))
    return pl.pallas_call(
        _combine_kernel, grid=(n // tm,),
        in_specs=[pl.BlockSpec((2 * tm,), lambda i: (i,), memory_space=pltpu.SMEM),
                  pl.BlockSpec((2 * tm,), lambda i: (jnp.minimum(i + 1, last),), memory_space=pltpu.SMEM),
                  row(d), row(LANES), _full(ln_g.shape), _full(ln_b.shape),
                  pl.BlockSpec(memory_space=pl.ANY)],
        out_specs=row(d),
        out_shape=jax.ShapeDtypeStruct((n, d), F32),
        scratch_shapes=[pltpu.VMEM((2, 2, tm * nch, LANES), F32), pltpu.SemaphoreType.DMA((2,))],
        compiler_params=_params("arbitrary"), name="combine",
    )(dest, dest, x1, rw, ln_g, ln_b, ys)


def _rope_tables(seq):
    half = ATTN_HEAD_DIM // 2
    inv_freq = ROPE_THETA ** (-np.arange(half, dtype=np.float64) / half)
    ang = np.arange(seq, dtype=np.float64)[:, None] * inv_freq[None, :]
    cos = np.cos(ang)
    sin = np.sin(ang)
    cos_h = np.concatenate([cos, cos], axis=1)
    sin_h = np.concatenate([-sin, sin], axis=1)
    return (jnp.asarray(np.tile(cos_h, (1, ATTN_HEADS)), F32), jnp.asarray(np.tile(sin_h, (1, ATTN_HEADS)), F32))


def _pad_lanes(a, width=LANES):
    return jnp.pad(a, ((0, 0), (0, width - a.shape[1])))


def kernel(x, ln0_g, ln0_b, w_in, conv_w, conv_b, w_mq, w_mk, b_i, b_f, gn_g, skip, w_attn_up, w_mlstm_up, w_out,
           ln1_g, ln1_b, w_router_group, b_router_group, w_router_expert, b_router_expert, w_gate, w_up, w_down,
           ln2_g, ln2_b):
    batch, seq, d = x.shape
    n = batch * seq
    assert seq % ROW_TILE == 0 and ROW_TILE == MOBA_BLOCK and w_in.shape[0] == DEPTH
    x2 = x.reshape(n, d)
    vec = lambda a: a.reshape(1, -1).astype(F32)

    wt = w_in[0].T
    cos, sin = _rope_tables(seq)

    q, k, v, kmean, u, vm, o, ift, ga, gm, xn = _inproj(
        x2, vec(ln0_g), vec(ln0_b), wt, cos, sin, batch, seq)

    nb = seq // MOBA_BLOCK
    km = kmean.reshape(batch, nb, ATTN_HEADS, ATTN_HEAD_DIM).transpose(0, 2, 1, 3)
    ya = _moba(q, k, v, km).reshape(n, ATTN_WIDTH)

    b_if = jnp.concatenate([b_i[0], b_f[0]]).astype(F32)
    ym = _mlstm(u, vm, o, ift, conv_w[0], vec(conv_b[0]), w_mq[0].transpose(0, 2, 1).astype(BF16),
                w_mk[0].astype(BF16), b_if[:, None],
                gn_g[0].astype(F32)[:, None], skip[0].astype(F32)[:, None], batch, seq)

    w_r = _pad_lanes(jnp.concatenate([w_router_expert[0], w_router_group[0]], axis=1))
    w_r_hi = w_r.astype(BF16)
    w_r_lo = (w_r - w_r_hi.astype(F32)).astype(BF16)
    w_rc = jnp.concatenate([w_r_hi.T, w_r_lo.T], axis=0)
    b_r = _pad_lanes(jnp.concatenate([b_router_expert[0], b_router_group[0]])[None, :]).T
    x1, ri, rw, counts = _mix(
        xn, ya, ym, ga, gm, w_attn_up[0].astype(BF16), w_mlstm_up[0].astype(BF16),
        w_out[0].astype(BF16), vec(ln1_g[0]), vec(ln1_b[0]), w_rc, b_r)

    tb = EXPERT_TILE
    nblk = (2 * n) // tb + MOE_EXPERTS
    cnt = counts[:MOE_EXPERTS, 0].astype(jnp.int32)
    nblk_e = (cnt + tb - 1) // tb
    blk_end = jnp.cumsum(nblk_e)
    pad_start = (blk_end - nblk_e) * tb
    nused = blk_end[-1:]
    ids = jnp.arange(MOE_EXPERTS, dtype=jnp.int32)
    prev_used = jnp.max(jnp.where((ids[None, :] <= ids[:, None]) & (nblk_e[None, :] > 0), ids[None, :], -1), axis=1)
    first_used = jnp.min(jnp.where(nblk_e > 0, ids, MOE_EXPERTS - 1))
    w_idx = jnp.where(prev_used >= 0, prev_used, first_used).astype(jnp.int32)
    last_blk = jnp.where(nblk_e > 0, (blk_end - 1) * tb, -1)
    last_blk = jnp.concatenate([last_blk, nused]).astype(jnp.int32)
    dest = _slots(ri, pad_start.astype(F32)[:, None])
    dest = dest[:2].reshape(2, n // ROW_TILE, ROW_TILE).transpose(1, 0, 2).reshape(2 * n)

    xs = _dispatch(dest, last_blk, x1, nblk * tb)
    ys = _experts((blk_end - nblk_e).astype(jnp.int32), nblk_e.astype(jnp.int32), w_idx, nused.astype(jnp.int32),
                  xs, w_gate[0], w_up[0], w_down[0])
    out = _combine(dest, x1, rw, vec(ln2_g[0]), vec(ln2_b[0]), ys)
    return out.reshape(batch, seq, d)
```

```python
import math

import jax
import jax.numpy as jnp
import numpy as np
from jax import lax
from jax.experimental import pallas as pl
from jax.experimental.pallas import tpu as pltpu

F32 = jnp.float32
BF16 = jnp.bfloat16

ATTN_HEADS = 8
ATTN_HEAD_DIM = 64
ATTN_WIDTH = ATTN_HEADS * ATTN_HEAD_DIM
MOBA_BLOCK = 256
MOBA_TOPK = 3
ROPE_THETA = 10000.0
MLSTM_HEADS = 4
MLSTM_HEAD_DIM = 128
MLSTM_WIDTH = MLSTM_HEADS * MLSTM_HEAD_DIM
MLSTM_CONV = 4
MOE_GROUPS = 8
MOE_EXPERTS_PER_GROUP = 8
MOE_EXPERTS = MOE_GROUPS * MOE_EXPERTS_PER_GROUP
LN_EPS = 1e-5
GN_EPS = 1e-6
DEPTH = 1
DEEPNORM_ALPHA = (2 * DEPTH) ** 0.25

LANES = 128
SUBLANES = 8
ROW_TILE = 256
EXPERT_TILE = 256
EXPERT_IN_SLOTS = 4
INPROJ_CHAINS = 2
MLSTM_CHUNKS = 2
SLOT_TILE = 2048
MIX_CHAINS = 4
VMEM_LIMIT = 48 * 1024 * 1024
INPROJ_VMEM_LIMIT = 58 * 1024 * 1024
WEIGHT_CHUNK = 512
LOG2_E = math.log2(math.e)

NEG_INF = float("-inf")


def _params(*sem):
    return pltpu.CompilerParams(dimension_semantics=sem, vmem_limit_bytes=VMEM_LIMIT)


def _dot(a, b):
    return jnp.dot(a, b, preferred_element_type=F32)


def _dot_nt(a, b):
    return lax.dot_general(a, b, (((1,), (1,)), ((), ())), preferred_element_type=F32)


def _split3(x):
    x1 = x.astype(BF16)
    r1 = x - x1.astype(F32)
    x2 = r1.astype(BF16)
    r2 = r1 - x2.astype(F32)
    return x1, x2, r2.astype(BF16)


def _layer_norm(x, g, b):
    mu = jnp.mean(x, axis=-1, keepdims=True)
    xc = x - mu
    var = jnp.mean(xc * xc, axis=-1, keepdims=True)
    return xc * lax.rsqrt(var + LN_EPS) * g + b


def _log_sigmoid(x):
    return jnp.minimum(x, 0.0) - jnp.log1p(jnp.exp(-jnp.abs(x)))


def _full(shape):
    nd = len(shape)
    return pl.BlockSpec(shape, lambda *_: (0,) * nd)


def _run_skewed(phases, chains, rows):
    states = [dict() for _ in range(chains)]
    for t in range(chains + len(phases) - 1):
        for c in range(chains):
            if 0 <= t - c < len(phases):
                phases[t - c](states[c], c, slice(c * rows, (c + 1) * rows))


def _loop_groups(count, body, group=4):
    def trip(g, _):
        for d in range(group):
            body(g * group + d)
        return 0

    lax.fori_loop(0, count // group, trip, 0)
    done = (count // group) * group
    size = group // 2
    while size >= 1:
        take = ((count - done) // size) * size
        @pl.when(take > 0)
        def _(done=done, size=size):
            for d in range(size):
                body(done + d)
        done = done + take
        size //= 2


def _inproj_kernel(x_ref, g_ref, b_ref, wt_ref, cos_ref, sin_ref,
                   q_ref, k_ref, v_ref, km_ref, u_ref, vm_ref, o_ref, ift_ref, ga_ref, gm_ref, xn_ref,
                   wqkv_ref, wuvo_ref, wift_ref, wg_ref):
    tm = ROW_TILE
    lane = lax.broadcasted_iota(jnp.int32, (tm, ATTN_WIDTH), 1)
    first_half = (lane % ATTN_HEAD_DIM) < (ATTN_HEAD_DIM // 2)

    @pl.when(pl.program_id(0) == 0)
    def _():
        def fill(dst_ref, row0):
            for c in range(dst_ref.shape[1] // WEIGHT_CHUNK):
                cols = slice(c * WEIGHT_CHUNK, (c + 1) * WEIGHT_CHUNK)
                rows = slice(row0 + c * WEIGHT_CHUNK, row0 + (c + 1) * WEIGHT_CHUNK)
                dst_ref[:, cols] = wt_ref[rows, :].T.astype(BF16)

        c_if = 3 * ATTN_WIDTH + 3 * MLSTM_WIDTH
        fill(wqkv_ref, 0)
        fill(wuvo_ref, 3 * ATTN_WIDTH)
        wift_ref[...] = wt_ref[c_if:c_if + 2 * MLSTM_HEADS, :].astype(BF16)
        fill(wg_ref, c_if + 2 * MLSTM_HEADS)

    def norm(st, c, rs):
        xn = _layer_norm(x_ref[rs, :], g_ref[...], b_ref[...])
        xn_ref[rs, :] = xn
        st["xb"] = xn.astype(BF16)

    def qkv_matmul(st, c, rs):
        st["zqkv"] = _dot(st["xb"], wqkv_ref[...])

    def attn_outputs(st, c, rs):
        zqkv = st.pop("zqkv")
        cos = cos_ref[rs, :]
        sin = sin_ref[rs, :]

        def rope(t):
            fwd = pltpu.roll(t, ATTN_WIDTH - ATTN_HEAD_DIM // 2, axis=1)
            bwd = pltpu.roll(t, ATTN_HEAD_DIM // 2, axis=1)
            return t * cos + jnp.where(first_half, fwd, bwd) * sin

        q = rope(zqkv[:, :ATTN_WIDTH]) * (ATTN_HEAD_DIM ** -0.5 * LOG2_E)
        k = rope(zqkv[:, ATTN_WIDTH:2 * ATTN_WIDTH])
        v = zqkv[:, 2 * ATTN_WIDTH:]
        km_ref[c] = jnp.mean(k, axis=0, keepdims=True)
        qt = q.T
        vt = v.T
        for h in range(ATTN_HEADS):
            sl = slice(h * ATTN_HEAD_DIM, (h + 1) * ATTN_HEAD_DIM)
            q_ref[0, h, :, rs] = qt[sl, :].astype(BF16)
            k_ref[0, h, rs, :] = k[:, sl].astype(BF16)
            v_ref[0, h, :, rs] = vt[sl, :].astype(BF16)

    def uvo_matmul(st, c, rs):
        st["zuvo"] = _dot(st["xb"], wuvo_ref[...])

    def mlstm_outputs(st, c, rs):
        zuvo = st.pop("zuvo")
        u_ref[rs, :] = zuvo[:, :MLSTM_WIDTH]
        vm_ref[:, rs] = zuvo[:, MLSTM_WIDTH:2 * MLSTM_WIDTH].T.astype(BF16)
        o_ref[:, rs] = zuvo[:, 2 * MLSTM_WIDTH:].T
        ift_ref[:, rs] = _dot_nt(wift_ref[...], st["xb"])

    def gate_matmul(st, c, rs):
        st["zg"] = _dot(st.pop("xb"), wg_ref[...])

    def gate_outputs(st, c, rs):
        zg = st.pop("zg")
        d = ga_ref.shape[1]
        ga_ref[rs, :] = jax.nn.sigmoid(zg[:, :d]).astype(BF16)
        gm_ref[rs, :] = jax.nn.sigmoid(zg[:, d:]).astype(BF16)

    _run_skewed((norm, qkv_matmul, attn_outputs, uvo_matmul, mlstm_outputs, gate_matmul, gate_outputs),
                x_ref.shape[0] // tm, tm)


def _inproj(x2, ln_g, ln_b, wt, cos, sin, batch, seq):
    n, d = x2.shape
    chains = INPROJ_CHAINS
    tm = chains * ROW_TILE
    assert seq % tm == 0
    nsb = seq // tm
    hd = ATTN_HEAD_DIM
    row = lambda w: pl.BlockSpec((tm, w), lambda i: (i, 0))
    col = lambda h: pl.BlockSpec((h, tm), lambda i: (0, i))
    head = pl.BlockSpec((1, ATTN_HEADS, tm, hd), lambda i: (i // nsb, 0, i % nsb, 0))
    head_t = pl.BlockSpec((1, ATTN_HEADS, hd, tm), lambda i: (i // nsb, 0, 0, i % nsb))
    tab = pl.BlockSpec((tm, ATTN_WIDTH), lambda i: (i % nsb, 0))
    head_shape = jax.ShapeDtypeStruct((batch, ATTN_HEADS, seq, hd), BF16)
    head_t_shape = jax.ShapeDtypeStruct((batch, ATTN_HEADS, hd, seq), BF16)
    out_shape = (
        head_t_shape, head_shape, head_t_shape,
        jax.ShapeDtypeStruct((n // ROW_TILE, 1, ATTN_WIDTH), F32),
        jax.ShapeDtypeStruct((n, MLSTM_WIDTH), F32),
        jax.ShapeDtypeStruct((MLSTM_WIDTH, n), BF16),
        jax.ShapeDtypeStruct((MLSTM_WIDTH, n), F32),
        jax.ShapeDtypeStruct((SUBLANES, n), F32),
        jax.ShapeDtypeStruct((n, d), BF16),
        jax.ShapeDtypeStruct((n, d), BF16),
        jax.ShapeDtypeStruct((n, d), F32),
    )
    out_specs = (
        head_t, head, head_t,
        pl.BlockSpec((chains, 1, ATTN_WIDTH), lambda i: (i, 0, 0)),
        row(MLSTM_WIDTH), col(MLSTM_WIDTH), col(MLSTM_WIDTH),
        col(SUBLANES),
        row(d), row(d), row(d),
    )
    wt_spec = pl.BlockSpec(wt.shape, lambda i: (0, 0), pipeline_mode=pl.Buffered(1))
    in_specs = [row(d), _full(ln_g.shape), _full(ln_b.shape), wt_spec, tab, tab]
    return pl.pallas_call(
        _inproj_kernel, grid=(n // tm,), in_specs=in_specs, out_specs=out_specs, out_shape=out_shape,
        scratch_shapes=[pltpu.VMEM((d, 3 * ATTN_WIDTH), BF16), pltpu.VMEM((d, 3 * MLSTM_WIDTH), BF16),
                        pltpu.VMEM((2 * MLSTM_HEADS, d), BF16), pltpu.VMEM((d, 2 * d), BF16)],
        compiler_params=pltpu.CompilerParams(dimension_semantics=("arbitrary",), vmem_limit_bytes=INPROJ_VMEM_LIMIT),
        name="inproj",
    )(x2, ln_g, ln_b, wt, cos, sin)


def _moba_kernel(qt_ref, k_ref, vt_ref, km_ref, o_ref, bias_ref, m_ref, l_ref, acc_ref, s_ref):
    i = pl.program_id(1)
    blk = MOBA_BLOCK
    hd = ATTN_HEAD_DIM
    heads = ATTN_HEADS
    nb = k_ref.shape[2] // blk
    blk_id = lax.broadcasted_iota(jnp.int32, (nb, blk), 0)
    key_pos = lax.broadcasted_iota(jnp.int32, (blk, blk), 0)
    qry_pos = lax.broadcasted_iota(jnp.int32, (blk, blk), 1)
    causal = key_pos <= qry_pos

    for h in range(heads):
        qt = qt_ref[0, h]
        km = km_ref[0, h]
        km_hi = km.astype(BF16)
        km_lo = (km - km_hi.astype(F32)).astype(BF16)
        gate = _dot(km_hi, qt) + _dot(km_lo, qt)
        gate = jnp.where(blk_id < i, gate, NEG_INF)
        for j in range(nb - 1):
            row = gate[j:j + 1, :]
            beats = (gate > row) | ((gate == row) & (blk_id < j))
            cnt = jnp.sum(jnp.where(beats, 1.0, 0.0), axis=0, keepdims=True)
            sel = (cnt < float(MOBA_TOPK)) & (row > NEG_INF)
            bias_ref[j * heads + h] = jnp.where(sel, 0.0, NEG_INF)
    for h in range(heads):
        bias_ref[i * heads + h] = jnp.zeros((1, blk), F32)

    def scores(h, j, own_block):
        qt = qt_ref[0, h]
        half = blk // 2
        m_tile = None
        for c in range(2):
            rows = slice(c * half, (c + 1) * half)
            s = _dot(k_ref[0, h, pl.ds(pl.multiple_of(j * blk + c * half, half), half), :], qt)
            if own_block:
                s = jnp.where(causal[rows], s, NEG_INF)
            s_ref[j * heads + h, rows, :] = s
            m_c = jnp.max(s, axis=0, keepdims=True)
            m_tile = m_c if m_tile is None else jnp.maximum(m_tile, m_c)
        return m_tile

    for h in range(heads):
        m_ref[h] = scores(h, i, True)

    def past_scores(j):
        for h in range(heads):
            m_ref[h] = jnp.maximum(m_ref[h], scores(h, j, False) + bias_ref[j * heads + h])

    _loop_groups(i, past_scores)

    l_ref[...] = jnp.zeros_like(l_ref)
    acc_ref[...] = jnp.zeros_like(acc_ref)

    def accumulate(j):
        off = pl.multiple_of(j * blk, blk)
        for h in range(heads):
            p = jnp.exp2(s_ref[j * heads + h] - (m_ref[h] - bias_ref[j * heads + h]))
            l_ref[h] += jnp.sum(p, axis=0, keepdims=True)
            acc_ref[h] += _dot(vt_ref[0, h, :, pl.ds(off, blk)], p.astype(BF16))

    _loop_groups(i + 1, accumulate)
    yt = acc_ref[...] / l_ref[...]
    o_ref[0] = yt.reshape(heads * hd, blk).T.astype(BF16)


def _moba(qt, k, vt, km):
    batch, heads, seq, hd = k.shape
    blk = MOBA_BLOCK
    nb = seq // blk
    return pl.pallas_call(
        _moba_kernel, grid=(batch, nb),
        in_specs=[
            pl.BlockSpec((1, heads, hd, blk), lambda b, i: (b, 0, 0, i)),
            pl.BlockSpec((1, heads, seq, hd), lambda b, i: (b, 0, 0, 0)),
            pl.BlockSpec((1, heads, hd, seq), lambda b, i: (b, 0, 0, 0)),
            pl.BlockSpec((1, heads, nb, hd), lambda b, i: (b, 0, 0, 0)),
        ],
        out_specs=pl.BlockSpec((1, blk, heads * hd), lambda b, i: (b, i, 0)),
        out_shape=jax.ShapeDtypeStruct((batch, seq, heads * hd), BF16),
        scratch_shapes=[pltpu.VMEM((nb * heads, 1, blk), F32), pltpu.VMEM((heads, 1, blk), F32),
                        pltpu.VMEM((heads, 1, blk), F32), pltpu.VMEM((heads, hd, blk), F32),
                        pltpu.VMEM((nb * heads, blk, blk), F32)],
        compiler_params=_params("parallel", "arbitrary"), name="moba",
    )(qt, k, vt, km)


def _mlstm_kernel(u_ref, vmt_ref, ot_ref, ift_ref, cw_ref, cb_ref, wqt_ref, wk_ref, bcol_ref,
                  gn_ref, skip_ref, y_ref, ext_ref, c_ref, n_ref, m_ref, yt_ref):
    @pl.when(pl.program_id(1) == 0)
    def _():
        ext_ref[0:SUBLANES, :] = jnp.zeros((SUBLANES, MLSTM_WIDTH), F32)
        c_ref[...] = jnp.zeros_like(c_ref)
        n_ref[...] = jnp.zeros_like(n_ref)
        m_ref[...] = jnp.zeros_like(m_ref)

    for c in range(u_ref.shape[0] // ROW_TILE):
        _mlstm_chunk(slice(c * ROW_TILE, (c + 1) * ROW_TILE), u_ref, vmt_ref, ot_ref, ift_ref, cw_ref, cb_ref,
                     wqt_ref, wk_ref, bcol_ref, gn_ref, skip_ref, y_ref, ext_ref, c_ref, n_ref, m_ref, yt_ref)


def _mlstm_chunk(rs, u_ref, vmt_ref, ot_ref, ift_ref, cw_ref, cb_ref, wqt_ref, wk_ref, bcol_ref,
                 gn_ref, skip_ref, y_ref, ext_ref, c_ref, n_ref, m_ref, yt_ref):
    tm = ROW_TILE
    hd = MLSTM_HEAD_DIM
    halo = SUBLANES
    u = u_ref[rs, :]
    ext_ref[halo:halo + tm, :] = u
    acc = jnp.broadcast_to(cb_ref[...], u.shape)
    for j in range(MLSTM_CONV):
        acc = acc + cw_ref[j:j + 1, :] * ext_ref[halo - (MLSTM_CONV - 1) + j:halo - (MLSTM_CONV - 1) + j + tm, :]
    ext_ref[0:halo, :] = u[tm - halo:, :]
    uc = acc * jax.nn.sigmoid(acc)

    gr = ift_ref[:, rs] + bcol_ref[...]
    rows = lax.broadcasted_iota(jnp.int32, (tm, tm), 0)
    cols = lax.broadcasted_iota(jnp.int32, (tm, tm), 1)
    causal_t = rows <= cols
    triu = jnp.where(causal_t, 1.0, 0.0).astype(BF16)
    r1, r2, r3 = _split3(_log_sigmoid(gr))
    bcum_r = _dot(r1, triu) + _dot(r2, triu) + _dot(r3, triu)
    key_rows = gr[:MLSTM_HEADS, :] - bcum_r[MLSTM_HEADS:, :]
    key_cols = jnp.concatenate([key_rows, jnp.zeros((LANES - MLSTM_HEADS, tm), F32)], axis=0).T

    uct = uc.T

    def decay_weights(st, h, hs):
        fl = MLSTM_HEADS + h
        b_row = bcum_r[fl:fl + 1, :]
        st["key_row"] = key_rows[h:h + 1, :]
        st["key_col"] = key_cols[:, h:h + 1]
        m_prev = m_ref[h][:, 0:1]
        dlog = jnp.where(causal_t, st["key_col"] + b_row, NEG_INF)
        inter = b_row + m_prev
        m_t = jnp.maximum(inter, jnp.max(dlog, axis=0, keepdims=True))
        st["w_intra"] = jnp.exp(dlog - m_t)
        st["w_inter"] = jnp.exp(inter - m_t)
        st["m_t"], st["m_prev"], st["b_end"] = m_t, m_prev, b_row[:, tm - 1:tm]

    def project(st, h, hs):
        st["qtb"] = _dot(wqt_ref[h], uct[hs, :].astype(BF16)).astype(BF16)
        st["k"] = _dot(uc[:, hs].astype(BF16), wk_ref[h]) * (hd ** -0.5)

    def scores(st, h, hs):
        st["s"] = _dot(st["k"].astype(BF16), st["qtb"]) * st.pop("w_intra")

    def readout(st, h, hs):
        qtb, s, w_inter, m_t = st.pop("qtb"), st.pop("s"), st.pop("w_inter"), st.pop("m_t")
        n_prev = n_ref[h]
        n_hi = n_prev.astype(BF16)
        n_lo = (n_prev - n_hi.astype(F32)).astype(BF16)
        qn = (_dot(n_hi, qtb) + _dot(n_lo, qtb))[0:1, :]
        num = w_inter * _dot(c_ref[h].astype(BF16), qtb) + _dot(vmt_ref[hs, rs], s.astype(BF16))
        den = w_inter * qn + jnp.sum(s, axis=0, keepdims=True)
        st["hh"] = num / jnp.maximum(jnp.abs(den), jnp.exp(-m_t))

    def update_state(st, h, hs):
        b_end, m_prev = st.pop("b_end"), st.pop("m_prev")
        m_new = jnp.maximum(b_end + m_prev, jnp.max(b_end + st.pop("key_row"), axis=1, keepdims=True))
        decay = jnp.exp(b_end + m_prev - m_new)
        kw = st.pop("k") * jnp.exp(b_end + st.pop("key_col") - m_new)
        n_prev = n_ref[h]
        c_ref[h] = decay * c_ref[h] + _dot(vmt_ref[hs, rs], kw.astype(BF16))
        n_ref[h] = decay * n_prev + jnp.broadcast_to(jnp.sum(kw, axis=0, keepdims=True), n_prev.shape)
        m_ref[h] = jnp.broadcast_to(m_new, (1, LANES))

    def gate_and_norm(st, h, hs):
        hh = jax.nn.sigmoid(ot_ref[hs, rs]) * st.pop("hh")
        mu = jnp.mean(hh, axis=0, keepdims=True)
        hc = hh - mu
        var = jnp.mean(hc * hc, axis=0, keepdims=True)
        yt_ref[hs, :] = hc * lax.rsqrt(var + GN_EPS) * gn_ref[hs, :] + skip_ref[hs, :] * uct[hs, :]

    _run_skewed((decay_weights, project, scores, readout, update_state, gate_and_norm), MLSTM_HEADS, hd)
    y_ref[rs, :] = yt_ref[...].T.astype(BF16)


def _mlstm(u, vmt, ot, ift, conv_w, conv_b, wqt, wk, bcol, gn_g, skip, batch, seq):
    n = u.shape[0]
    tm = MLSTM_CHUNKS * ROW_TILE
    assert seq % tm == 0
    nc = seq // tm
    row = lambda w: pl.BlockSpec((tm, w), lambda b, c: (b * nc + c, 0))
    col = lambda h: pl.BlockSpec((h, tm), lambda b, c: (0, b * nc + c))
    in_specs = [row(MLSTM_WIDTH), col(MLSTM_WIDTH), col(MLSTM_WIDTH), col(SUBLANES),
                _full(conv_w.shape), _full(conv_b.shape), _full(wqt.shape), _full(wk.shape),
                _full(bcol.shape), _full(gn_g.shape), _full(skip.shape)]
    return pl.pallas_call(
        _mlstm_kernel, grid=(batch, nc), in_specs=in_specs, out_specs=row(MLSTM_WIDTH),
        out_shape=jax.ShapeDtypeStruct((n, MLSTM_WIDTH), BF16),
        scratch_shapes=[pltpu.VMEM((SUBLANES + ROW_TILE, MLSTM_WIDTH), F32),
                        pltpu.VMEM((MLSTM_HEADS, MLSTM_HEAD_DIM, MLSTM_HEAD_DIM), F32),
                        pltpu.VMEM((MLSTM_HEADS, SUBLANES, MLSTM_HEAD_DIM), F32),
                        pltpu.VMEM((MLSTM_HEADS, 1, LANES), F32),
                        pltpu.VMEM((MLSTM_WIDTH, ROW_TILE), F32)],
        compiler_params=_params("parallel", "arbitrary"), name="mlstm",
    )(u, vmt, ot, ift, conv_w, conv_b, wqt, wk, bcol, gn_g, skip)


def _mix_kernel(xn_ref, ya_ref, ym_ref, ga_ref, gm_ref, wau_ref, wmu_ref, wout_ref,
                g1_ref, b1_ref, wrc_ref, br_ref,
                x1_ref, x1t_ref, ri_ref, rw_ref, cnt_out_ref, cnt_ref):
    @pl.when(pl.program_id(0) == 0)
    def _():
        cnt_ref[...] = jnp.zeros_like(cnt_ref)

    tm = ROW_TILE
    sub = lax.broadcasted_iota(jnp.int32, (LANES, tm), 0).astype(F32)
    big = float(4 * LANES)

    def up_and_mix(st, c, rs):
        a_up = _dot(ya_ref[rs, :], wau_ref[...])
        m_up = _dot(ym_ref[rs, :], wmu_ref[...])
        mix = ga_ref[rs, :].astype(F32) * a_up + gm_ref[rs, :].astype(F32) * m_up
        st["mix"] = mix.astype(BF16)

    def out_and_norm(st, c, rs):
        x1 = _layer_norm(DEEPNORM_ALPHA * xn_ref[rs, :] + _dot(st.pop("mix"), wout_ref[...]), g1_ref[...], b1_ref[...])
        x1_ref[rs, :] = x1
        nch = _token_rows(x1.shape[1])
        _to_token_tiles(x1t_ref.at[pl.ds(rs.start * nch, (rs.stop - rs.start) * nch), :], x1)
        st["x1"] = x1

    def router_logits(st, c, rs):
        x1 = st.pop("x1")
        x_hi = x1.astype(BF16)
        x_lo = (x1 - x_hi.astype(F32)).astype(BF16)
        both = _dot_nt(wrc_ref[...], x_hi)
        st["logits"] = both[:LANES] + both[LANES:] + _dot_nt(wrc_ref[:LANES, :], x_lo) + br_ref[...]

    def route(st, c, rs):
        logits = st.pop("logits")
        is_g = (sub >= float(MOE_EXPERTS)) & (sub < float(MOE_EXPERTS + MOE_GROUPS))
        gl = jnp.where(is_g, logits, NEG_INF)
        ge = jnp.exp(gl - jnp.max(gl, axis=0, keepdims=True))
        gp = ge / jnp.sum(ge, axis=0, keepdims=True)
        g_w = jnp.max(gp, axis=0, keepdims=True)
        g_idx = jnp.min(jnp.where((gp == g_w) & is_g, sub - float(MOE_EXPERTS), big), axis=0, keepdims=True)
        lo = g_idx * float(MOE_EXPERTS_PER_GROUP)
        in_grp = (sub >= lo) & (sub < lo + float(MOE_EXPERTS_PER_GROUP))
        el = jnp.where(in_grp, logits, NEG_INF)
        v1 = jnp.max(el, axis=0, keepdims=True)
        i1 = jnp.min(jnp.where((el == v1) & in_grp, sub, big), axis=0, keepdims=True)
        el2 = jnp.where(sub == i1, NEG_INF, el)
        v2 = jnp.max(el2, axis=0, keepdims=True)
        i2 = jnp.min(jnp.where((el2 == v2) & in_grp & (sub != i1), sub, big), axis=0, keepdims=True)
        e2 = jnp.exp(v2 - v1)
        w0 = g_w / (1.0 + e2)
        w1 = g_w * e2 / (1.0 + e2)
        rw_ref[rs, :] = jnp.where(sub == 0.0, w0, jnp.where(sub == 1.0, w1, 0.0)).T
        st["i1"], st["i2"] = i1, i2

    def rank(st, c, rs):
        i1, i2 = st.pop("i1"), st.pop("i2")
        is1 = sub == i1
        is2 = sub == i2
        onehot = jnp.where(is1 | is2, 1.0, 0.0)
        rows = lax.broadcasted_iota(jnp.int32, (tm, tm), 0)
        cols = lax.broadcasted_iota(jnp.int32, (tm, tm), 1)
        earlier = jnp.where(rows < cols, 1.0, 0.0).astype(BF16)
        before = _dot(onehot.astype(BF16), earlier) + cnt_ref[...]
        r0 = jnp.sum(jnp.where(is1, before, 0.0), axis=0, keepdims=True)
        r1 = jnp.sum(jnp.where(is2, before, 0.0), axis=0, keepdims=True)
        total = cnt_ref[...] + jnp.sum(onehot, axis=1, keepdims=True)
        cnt_ref[...] = total
        cnt_out_ref[...] = total
        ri_t = jnp.where(sub == 0.0, i1, jnp.where(sub == 1.0, i2, jnp.where(sub == 2.0, r0, jnp.where(sub == 3.0, r1, 0.0))))
        ri_ref[:, rs] = ri_t[:SUBLANES, :].astype(jnp.int32)

    _run_skewed((up_and_mix, out_and_norm, router_logits, route, rank), xn_ref.shape[0] // tm, tm)


def _mix(xn, ya, ym, ga, gm, wau, wmu, wout, g1, b1, wrc, br):
    n, d = xn.shape
    tm = MIX_CHAINS * ROW_TILE
    row = lambda w: pl.BlockSpec((tm, w), lambda i: (i, 0))
    in_specs = [row(d), row(ATTN_WIDTH), row(MLSTM_WIDTH), row(d), row(d),
                _full(wau.shape), _full(wmu.shape), _full(wout.shape), _full(g1.shape), _full(b1.shape),
                _full(wrc.shape), _full(br.shape)]
    nch = _token_rows(d)
    out_shape = (jax.ShapeDtypeStruct((n, d), F32), jax.ShapeDtypeStruct((n * nch, LANES), F32),
                 jax.ShapeDtypeStruct((SUBLANES, n), jnp.int32),
                 jax.ShapeDtypeStruct((n, LANES), F32), jax.ShapeDtypeStruct((LANES, 1), F32))
    out_specs = (row(d), pl.BlockSpec((tm * nch, LANES), lambda i: (i, 0)),
                 pl.BlockSpec((SUBLANES, tm), lambda i: (0, i)), row(LANES), _full((LANES, 1)))
    return pl.pallas_call(
        _mix_kernel, grid=(n // tm,), in_specs=in_specs, out_specs=out_specs, out_shape=out_shape,
        scratch_shapes=[pltpu.VMEM((LANES, 1), F32)],
        compiler_params=_params("arbitrary"), name="mix",
    )(xn, ya, ym, ga, gm, wau, wmu, wout, g1, b1, wrc, br)


def _token_rows(d):
    return d // LANES


def _to_token_tiles(dst_ref, x):
    rows, d = x.shape
    nch = _token_rows(d)
    for c in range(nch):
        dst_ref[pl.ds(c, rows, stride=nch), :] = x[:, c * LANES:(c + 1) * LANES]


def _from_token_tiles(src_ref, rows, d):
    nch = _token_rows(d)
    return jnp.concatenate([src_ref[pl.ds(c, rows, stride=nch), :] for c in range(nch)], axis=1)


def _token_copy(src, src_tok, dst, dst_tok, nch, sem):
    s0 = pl.multiple_of(src_tok * nch, nch)
    d0 = pl.multiple_of(dst_tok * nch, nch)
    return pltpu.make_async_copy(src.at[pl.ds(s0, nch), :], dst.at[pl.ds(d0, nch), :], sem)


def _slots_kernel(ri_ref, ps_ref, o_ref):
    ri = ri_ref[...].astype(F32)
    ps = ps_ref[...]
    expert = lax.broadcasted_iota(jnp.int32, (ps.shape[0], ri.shape[1]), 0).astype(F32)
    row_id = lax.broadcasted_iota(jnp.int32, ri.shape, 0)
    out = jnp.zeros(ri.shape, F32)
    for k in range(2):
        start = jnp.sum(jnp.where(expert == ri[k:k + 1, :], jnp.broadcast_to(ps, expert.shape), 0.0),
                        axis=0, keepdims=True)
        out = jnp.where(row_id == k, start + ri[2 + k:3 + k, :], out)
    o_ref[...] = out.astype(jnp.int32)


def _slots(ri, pad_start_col):
    n = ri.shape[1]
    tm = SLOT_TILE
    blk = pl.BlockSpec((SUBLANES, tm), lambda i: (0, i))
    return pl.pallas_call(
        _slots_kernel, grid=(n // tm,), in_specs=[blk, _full(pad_start_col.shape)], out_specs=blk,
        out_shape=jax.ShapeDtypeStruct((SUBLANES, n), jnp.int32),
        compiler_params=_params("parallel"), name="slots",
    )(ri, pad_start_col)


def _slot(dest_ref, r, k):
    return dest_ref[k * ROW_TILE + r]


def _dispatch_kernel(dest_ref, last_ref, xt_ref, xs_ref, zero_ref, sem, zsem):
    tm = ROW_TILE
    tb = EXPERT_TILE
    nch = zero_ref.shape[0] // tb

    @pl.when(pl.program_id(0) == 0)
    def _():
        zero_ref[...] = jnp.zeros_like(zero_ref)

        def desc(tok):
            off = pl.multiple_of(jnp.maximum(tok, 0) * nch, nch)
            return pltpu.make_async_copy(zero_ref, xs_ref.at[pl.ds(off, tb * nch), :], zsem)

        def zstart(e, _):
            @pl.when(last_ref[e] >= 0)
            def _():
                desc(last_ref[e]).start()
            return 0

        def zwait(e, _):
            @pl.when(last_ref[e] >= 0)
            def _():
                desc(last_ref[e]).wait()
            return 0

        lax.fori_loop(0, MOE_EXPERTS, zstart, 0)
        nused = last_ref[MOE_EXPERTS]
        nblk = xs_ref.shape[0] // (tb * nch)
        lax.fori_loop(nused, nblk, lambda b, _: (desc(b * tb).start(), 0)[1], 0)
        lax.fori_loop(0, MOE_EXPERTS, zwait, 0)
        lax.fori_loop(nused, nblk, lambda b, _: (desc(b * tb).wait(), 0)[1], 0)

    def start(r, _):
        for k in range(2):
            _token_copy(xt_ref, r, xs_ref, _slot(dest_ref, r, k), nch, sem).start(priority=k)
        return 0

    def wait(r, _):
        for k in range(2):
            _token_copy(xt_ref, 0, xs_ref, 0, nch, sem).wait()
        return 0

    lax.fori_loop(0, tm, start, 0, unroll=8)
    lax.fori_loop(0, tm, wait, 0, unroll=8)


def _dispatch(dest, last_blk, x1t, d, n_rows):
    tm = ROW_TILE
    nch = _token_rows(d)
    n = x1t.shape[0] // nch
    return pl.pallas_call(
        _dispatch_kernel, grid=(n // tm,),
        in_specs=[pl.BlockSpec((2 * tm,), lambda i: (i,), memory_space=pltpu.SMEM),
                  pl.BlockSpec(memory_space=pltpu.SMEM),
                  pl.BlockSpec((tm * nch, LANES), lambda i: (i, 0))],
        out_specs=pl.BlockSpec(memory_space=pl.ANY),
        out_shape=jax.ShapeDtypeStruct((n_rows * nch, LANES), F32),
        scratch_shapes=[pltpu.VMEM((EXPERT_TILE * nch, LANES), F32),
                        pltpu.SemaphoreType.DMA(()), pltpu.SemaphoreType.DMA(())],
        compiler_params=_params("arbitrary"), name="dispatch",
    )(dest, last_blk, x1t)


def _expert_kernel(first_ref, count_ref, widx_ref, nused_ref, wg_ref, wu_ref, wd_ref, xs_ref, ys_ref,
                   wgb_ref, wub_ref, wdb_ref, xbuf_ref, ybuf_ref, in_sem, out_sem):
    del widx_ref
    e = pl.program_id(0)
    nused = nused_ref[0]
    d = wg_ref.shape[1]
    nch = _token_rows(d)
    rows = xbuf_ref.shape[1]
    tb = rows // nch

    def blk(ref, b):
        return ref.at[pl.ds(pl.multiple_of(b * rows, rows), rows), :]

    def in_copy(b, slot):
        return pltpu.make_async_copy(blk(xs_ref, b), xbuf_ref.at[slot], in_sem.at[slot])

    def out_copy(b, slot):
        return pltpu.make_async_copy(ybuf_ref.at[slot], blk(ys_ref, b), out_sem.at[slot])

    n_in = xbuf_ref.shape[0]

    @pl.when(e == 0)
    def _():
        for b0 in range(n_in - 1):
            @pl.when(b0 < nused)
            def _():
                in_copy(b0, b0).start()

    @pl.when(count_ref[e] > 0)
    def _():
        wgb_ref[...] = wg_ref[0].astype(BF16)
        wub_ref[...] = wu_ref[0].astype(BF16)
        wdb_ref[...] = wd_ref[0].astype(BF16)

    def body(b, _):
        slot = b % n_in
        oslot = b % 2
        in_copy(b, slot).wait()

        @pl.when(b + n_in - 1 < nused)
        def _():
            in_copy(b + n_in - 1, (b + n_in - 1) % n_in).start()

        @pl.when(b >= 2)
        def _():
            out_copy(b - 2, oslot).wait()

        xb = _from_token_tiles(xbuf_ref.at[slot], tb, d).astype(BF16)
        g = _dot(xb, wgb_ref[...])
        u = _dot(xb, wub_ref[...])
        hmid = g * jax.nn.sigmoid(g) * u
        _to_token_tiles(ybuf_ref.at[oslot], _dot(hmid.astype(BF16), wdb_ref[...]))
        out_copy(b, oslot).start()
        return 0

    lax.fori_loop(first_ref[e], first_ref[e] + count_ref[e], body, 0)

    @pl.when(e == pl.num_programs(0) - 1)
    def _():
        for back in (2, 1):
            @pl.when(nused >= back)
            def _():
                out_copy(nused - back, (nused - back) % 2).wait()


def _experts(first_blk, blk_count, w_idx, nused, xs, w_gate, w_up, w_down):
    n_exp, d, dff = w_gate.shape
    nch = _token_rows(d)
    rows = EXPERT_TILE * nch
    w_spec = lambda shape: pl.BlockSpec(shape, lambda e, fb, bc, wi, nu: (wi[e], 0, 0))
    any_spec = pl.BlockSpec(memory_space=pl.ANY)
    grid_spec = pltpu.PrefetchScalarGridSpec(
        num_scalar_prefetch=4, grid=(n_exp,),
        in_specs=[w_spec((1, d, dff)), w_spec((1, d, dff)), w_spec((1, dff, d)), any_spec],
        out_specs=any_spec,
        scratch_shapes=[pltpu.VMEM((d, dff), BF16), pltpu.VMEM((d, dff), BF16), pltpu.VMEM((dff, d), BF16),
                        pltpu.VMEM((EXPERT_IN_SLOTS, rows, LANES), F32), pltpu.VMEM((2, rows, LANES), F32),
                        pltpu.SemaphoreType.DMA((EXPERT_IN_SLOTS,)), pltpu.SemaphoreType.DMA((2,))],
    )
    return pl.pallas_call(
        _expert_kernel, grid_spec=grid_spec, out_shape=jax.ShapeDtypeStruct(xs.shape, F32),
        input_output_aliases={7: 0},
        compiler_params=_params("arbitrary"), name="experts",
    )(first_blk, blk_count, w_idx, nused, w_gate, w_up, w_down, xs)


def _combine_kernel(dest_ref, dest_next_ref, x1_ref, rw_ref, g_ref, b_ref, ys_ref, o_ref, buf_ref, sem):
    tm, d = x1_ref.shape
    nch = _token_rows(d)
    step = pl.program_id(0)
    slot = step % 2

    def gather(idx_ref, which):
        def start(r, _):
            for k in range(2):
                _token_copy(ys_ref, _slot(idx_ref, r, k), buf_ref.at[which, k], r, nch,
                            sem.at[which]).start(priority=k)
            return 0
        lax.fori_loop(0, tm, start, 0, unroll=8)

    @pl.when(step == 0)
    def _():
        gather(dest_ref, 0)

    @pl.when(step + 1 < pl.num_programs(0))
    def _():
        gather(dest_next_ref, 1 - slot)

    def wait(r, _):
        for k in range(2):
            _token_copy(ys_ref, 0, buf_ref.at[slot, k], 0, nch, sem.at[slot]).wait()
        return 0

    lax.fori_loop(0, tm, wait, 0, unroll=8)
    rw = rw_ref[...]
    y0 = _from_token_tiles(buf_ref.at[slot, 0], tm, d)
    y1 = _from_token_tiles(buf_ref.at[slot, 1], tm, d)
    ffn = rw[:, 0:1] * y0 + rw[:, 1:2] * y1
    o_ref[...] = _layer_norm(DEEPNORM_ALPHA * x1_ref[...] + ffn, g_ref[...], b_ref[...])


def _combine(dest, x1, rw, ln_g, ln_b, ys):
    n, d = x1.shape
    tm = ROW_TILE
    nch = _token_rows(d)
    last = n // tm - 1
    row = lambda w: pl.BlockSpec((tm, w), lambda i: (i, 0))
    return pl.pallas_call(
        _combine_kernel, grid=(n // tm,),
        in_specs=[pl.BlockSpec((2 * tm,), lambda i: (i,), memory_space=pltpu.SMEM),
                  pl.BlockSpec((2 * tm,), lambda i: (jnp.minimum(i + 1, last),), memory_space=pltpu.SMEM),
                  row(d), row(LANES), _full(ln_g.shape), _full(ln_b.shape),
                  pl.BlockSpec(memory_space=pl.ANY)],
        out_specs=row(d),
        out_shape=jax.ShapeDtypeStruct((n, d), F32),
        scratch_shapes=[pltpu.VMEM((2, 2, tm * nch, LANES), F32), pltpu.SemaphoreType.DMA((2,))],
        compiler_params=_params("arbitrary"), name="combine",
    )(dest, dest, x1, rw, ln_g, ln_b, ys)


def _rope_tables(seq):
    half = ATTN_HEAD_DIM // 2
    inv_freq = ROPE_THETA ** (-np.arange(half, dtype=np.float64) / half)
    ang = np.arange(seq, dtype=np.float64)[:, None] * inv_freq[None, :]
    cos = np.cos(ang)
    sin = np.sin(ang)
    cos_h = np.concatenate([cos, cos], axis=1)
    sin_h = np.concatenate([-sin, sin], axis=1)
    return (jnp.asarray(np.tile(cos_h, (1, ATTN_HEADS)), F32), jnp.asarray(np.tile(sin_h, (1, ATTN_HEADS)), F32))


def _pad_lanes(a, width=LANES):
    return jnp.pad(a, ((0, 0), (0, width - a.shape[1])))


def kernel(x, ln0_g, ln0_b, w_in, conv_w, conv_b, w_mq, w_mk, b_i, b_f, gn_g, skip, w_attn_up, w_mlstm_up, w_out,
           ln1_g, ln1_b, w_router_group, b_router_group, w_router_expert, b_router_expert, w_gate, w_up, w_down,
           ln2_g, ln2_b):
    batch, seq, d = x.shape
    n = batch * seq
    assert seq % ROW_TILE == 0 and ROW_TILE == MOBA_BLOCK and w_in.shape[0] == DEPTH
    x2 = x.reshape(n, d)
    vec = lambda a: a.reshape(1, -1).astype(F32)

    wt = w_in[0].T
    cos, sin = _rope_tables(seq)

    q, k, v, kmean, u, vm, o, ift, ga, gm, xn = _inproj(
        x2, vec(ln0_g), vec(ln0_b), wt, cos, sin, batch, seq)

    nb = seq // MOBA_BLOCK
    km = kmean.reshape(batch, nb, ATTN_HEADS, ATTN_HEAD_DIM).transpose(0, 2, 1, 3)
    ya = _moba(q, k, v, km).reshape(n, ATTN_WIDTH)

    b_if = jnp.concatenate([b_i[0], b_f[0]]).astype(F32)
    ym = _mlstm(u, vm, o, ift, conv_w[0], vec(conv_b[0]), w_mq[0].transpose(0, 2, 1).astype(BF16),
                w_mk[0].astype(BF16), b_if[:, None],
                gn_g[0].astype(F32)[:, None], skip[0].astype(F32)[:, None], batch, seq)

    w_r = _pad_lanes(jnp.concatenate([w_router_expert[0], w_router_group[0]], axis=1))
    w_r_hi = w_r.astype(BF16)
    w_r_lo = (w_r - w_r_hi.astype(F32)).astype(BF16)
    w_rc = jnp.concatenate([w_r_hi.T, w_r_lo.T], axis=0)
    b_r = _pad_lanes(jnp.concatenate([b_router_expert[0], b_router_group[0]])[None, :]).T
    x1, x1t, ri, rw, counts = _mix(
        xn, ya, ym, ga, gm, w_attn_up[0].astype(BF16), w_mlstm_up[0].astype(BF16),
        w_out[0].astype(BF16), vec(ln1_g[0]), vec(ln1_b[0]), w_rc, b_r)

    tb = EXPERT_TILE
    nblk = (2 * n) // tb + MOE_EXPERTS
    cnt = counts[:MOE_EXPERTS, 0].astype(jnp.int32)
    nblk_e = (cnt + tb - 1) // tb
    blk_end = jnp.cumsum(nblk_e)
    pad_start = (blk_end - nblk_e) * tb
    nused = blk_end[-1:]
    ids = jnp.arange(MOE_EXPERTS, dtype=jnp.int32)
    prev_used = jnp.max(jnp.where((ids[None, :] <= ids[:, None]) & (nblk_e[None, :] > 0), ids[None, :], -1), axis=1)
    first_used = jnp.min(jnp.where(nblk_e > 0, ids, MOE_EXPERTS - 1))
    w_idx = jnp.where(prev_used >= 0, prev_used, first_used).astype(jnp.int32)
    last_blk = jnp.where(nblk_e > 0, (blk_end - 1) * tb, -1)
    last_blk = jnp.concatenate([last_blk, nused]).astype(jnp.int32)
    dest = _slots(ri, pad_start.astype(F32)[:, None])
    dest = dest[:2].reshape(2, n // ROW_TILE, ROW_TILE).transpose(1, 0, 2).reshape(2 * n)

    xs = _dispatch(dest, last_blk, x1t, d, nblk * tb)
    ys = _experts((blk_end - nblk_e).astype(jnp.int32), nblk_e.astype(jnp.int32), w_idx, nused.astype(jnp.int32),
                  xs, w_gate[0], w_up[0], w_down[0])
    out = _combine(dest, x1, rw, vec(ln2_g[0]), vec(ln2_b[0]), ys)
    return out.reshape(batch, seq, d)
```

```python
import math

import jax
import jax.numpy as jnp
import numpy as np
from jax import lax
from jax.experimental import pallas as pl
from jax.experimental.pallas import tpu as pltpu

F32 = jnp.float32
BF16 = jnp.bfloat16

ATTN_HEADS = 8
ATTN_HEAD_DIM = 64
ATTN_WIDTH = ATTN_HEADS * ATTN_HEAD_DIM
MOBA_BLOCK = 256
MOBA_TOPK = 3
ROPE_THETA = 10000.0
MLSTM_HEADS = 4
MLSTM_HEAD_DIM = 128
MLSTM_WIDTH = MLSTM_HEADS * MLSTM_HEAD_DIM
MLSTM_CONV = 4
MOE_GROUPS = 8
MOE_EXPERTS_PER_GROUP = 8
MOE_EXPERTS = MOE_GROUPS * MOE_EXPERTS_PER_GROUP
LN_EPS = 1e-5
GN_EPS = 1e-6
DEPTH = 1
DEEPNORM_ALPHA = (2 * DEPTH) ** 0.25

LANES = 128
SUBLANES = 8
ROW_TILE = 256
EXPERT_TILE = 256
EXPERT_IN_SLOTS = 4
SLOT_TILE = 2048
MIX_CHAINS = 4
VMEM_LIMIT = 48 * 1024 * 1024
INPROJ_VMEM_LIMIT = 58 * 1024 * 1024
WEIGHT_CHUNK = 512
LOG2_E = math.log2(math.e)

NEG_INF = float("-inf")


def _params(*sem):
    return pltpu.CompilerParams(dimension_semantics=sem, vmem_limit_bytes=VMEM_LIMIT)


def _dot(a, b):
    return jnp.dot(a, b, preferred_element_type=F32)


def _dot_nt(a, b):
    return lax.dot_general(a, b, (((1,), (1,)), ((), ())), preferred_element_type=F32)


def _split3(x):
    x1 = x.astype(BF16)
    r1 = x - x1.astype(F32)
    x2 = r1.astype(BF16)
    r2 = r1 - x2.astype(F32)
    return x1, x2, r2.astype(BF16)


def _layer_norm(x, g, b):
    mu = jnp.mean(x, axis=-1, keepdims=True)
    xc = x - mu
    var = jnp.mean(xc * xc, axis=-1, keepdims=True)
    return xc * lax.rsqrt(var + LN_EPS) * g + b


def _log_sigmoid(x):
    return jnp.minimum(x, 0.0) - jnp.log1p(jnp.exp(-jnp.abs(x)))


def _full(shape):
    nd = len(shape)
    return pl.BlockSpec(shape, lambda *_: (0,) * nd)


def _skewed(phases, chains, rows):
    states = [dict() for _ in range(chains)]

    def thunk(t, c):
        return lambda: phases[t - c](states[c], c, slice(c * rows, (c + 1) * rows))

    return [thunk(t, c) for t in range(chains + len(phases) - 1) for c in range(chains) if 0 <= t - c < len(phases)]


def _run_skewed(phases, chains, rows):
    for thunk in _skewed(phases, chains, rows):
        thunk()


def _interleave(a, b):
    ia = ib = 0
    while ia < len(a) or ib < len(b):
        if ib >= len(b) or (ia < len(a) and ia * len(b) <= ib * len(a)):
            a[ia]()
            ia += 1
        else:
            b[ib]()
            ib += 1


def _loop_groups(count, body, group=4):
    def trip(g, _):
        for d in range(group):
            body(g * group + d)
        return 0

    lax.fori_loop(0, count // group, trip, 0)
    done = (count // group) * group
    size = group // 2
    while size >= 1:
        take = ((count - done) // size) * size
        @pl.when(take > 0)
        def _(done=done, size=size):
            for d in range(size):
                body(done + d)
        done = done + take
        size //= 2


def _front_kernel(x_ref, g_ref, b_ref, wt_ref, cos_ref, sin_ref,
                  cw_ref, cb_ref, wqt_ref, wk_ref, bcol_ref, gn_ref, skip_ref,
                  q_ref, k_ref, v_ref, km_ref, ga_ref, gm_ref, xn_ref, y_ref,
                  wch_ref, wift_ref,
                  u_s, vmt_s, ot_s, ift_s, ext_ref, c_ref, n_ref, m_ref, yt_ref, *, chunks_per_seq):
    tm = ROW_TILE
    hd = MLSTM_HEAD_DIM
    halo = SUBLANES
    step = pl.program_id(0)
    wr = step % 2
    rd = 1 - wr

    @pl.when(step == 0)
    def _():
        c_if = 3 * ATTN_WIDTH + 3 * MLSTM_WIDTH
        for c in range(wch_ref.shape[0]):
            row0 = c * WEIGHT_CHUNK + (2 * MLSTM_HEADS if c * WEIGHT_CHUNK >= c_if else 0)
            wch_ref[c] = wt_ref[row0:row0 + WEIGHT_CHUNK, :].T.astype(BF16)
        wift_ref[...] = wt_ref[c_if:c_if + 2 * MLSTM_HEADS, :].astype(BF16)
        u_s[...] = jnp.zeros_like(u_s)
        vmt_s[...] = jnp.zeros_like(vmt_s)
        ot_s[...] = jnp.zeros_like(ot_s)
        ift_s[...] = jnp.zeros_like(ift_s)

    @pl.when(lax.rem(jnp.maximum(step - 1, 0), chunks_per_seq) == 0)
    def _():
        ext_ref[0:halo, :] = jnp.zeros((halo, MLSTM_WIDTH), F32)
        c_ref[...] = jnp.zeros_like(c_ref)
        n_ref[...] = jnp.zeros_like(n_ref)
        m_ref[...] = jnp.zeros_like(m_ref)

    lane = lax.broadcasted_iota(jnp.int32, (tm, ATTN_WIDTH), 1)
    first_half = (lane % ATTN_HEAD_DIM) < (ATTN_HEAD_DIM // 2)
    ps = {}

    def norm():
        xn = _layer_norm(x_ref[...], g_ref[...], b_ref[...])
        xn_ref[...] = xn
        ps["xb"] = xn.astype(BF16)

    def piece(key, chunk):
        def run():
            ps[key] = _dot(ps["xb"], wch_ref[chunk])
        return run

    def attn_outputs():
        cos = cos_ref[...]
        sin = sin_ref[...]

        def rope(t):
            fwd = pltpu.roll(t, ATTN_WIDTH - ATTN_HEAD_DIM // 2, axis=1)
            bwd = pltpu.roll(t, ATTN_HEAD_DIM // 2, axis=1)
            return t * cos + jnp.where(first_half, fwd, bwd) * sin

        q = rope(ps.pop("zq")) * (ATTN_HEAD_DIM ** -0.5 * LOG2_E)
        k = rope(ps.pop("zk"))
        km_ref[0] = jnp.mean(k, axis=0, keepdims=True)
        qt = q.T
        vt = ps.pop("zv").T
        for h in range(ATTN_HEADS):
            sl = slice(h * ATTN_HEAD_DIM, (h + 1) * ATTN_HEAD_DIM)
            q_ref[0, h] = qt[sl, :].astype(BF16)
            k_ref[0, h] = k[:, sl].astype(BF16)
            v_ref[0, h] = vt[sl, :].astype(BF16)

    def stage():
        u_s[wr] = ps.pop("zu")
        vmt_s[wr] = ps.pop("zvm").T.astype(BF16)
        ot_s[wr] = ps.pop("zo").T
        ift_s[wr] = _dot_nt(wift_ref[...], ps["xb"])

    d = ga_ref.shape[1]
    gw = d // 2

    def gate_out(key, ref, lo):
        def run():
            ref[:, lo:lo + gw] = jax.nn.sigmoid(ps.pop(key)).astype(BF16)
        return run

    assert ATTN_WIDTH == MLSTM_WIDTH == gw == WEIGHT_CHUNK
    project = [
        norm,
        piece("zq", 0), piece("zk", 1), piece("zv", 2),
        attn_outputs,
        piece("zu", 3), piece("zvm", 4), piece("zo", 5),
        stage,
        piece("g0", 6), piece("g1", 7), gate_out("g0", ga_ref, 0),
        piece("g2", 8), gate_out("g1", ga_ref, gw),
        piece("g3", 9), gate_out("g2", gm_ref, 0), gate_out("g3", gm_ref, gw),
    ]

    ms = {}
    rows = lax.broadcasted_iota(jnp.int32, (tm, tm), 0)
    cols = lax.broadcasted_iota(jnp.int32, (tm, tm), 1)
    causal_t = rows <= cols

    def prologue():
        u = u_s[rd]
        ext_ref[halo:halo + tm, :] = u
        acc = jnp.broadcast_to(cb_ref[...], u.shape)
        for j in range(MLSTM_CONV):
            acc = acc + cw_ref[j:j + 1, :] * ext_ref[halo - (MLSTM_CONV - 1) + j:halo - (MLSTM_CONV - 1) + j + tm, :]
        ext_ref[0:halo, :] = u[tm - halo:, :]
        uc = acc * jax.nn.sigmoid(acc)
        gr = ift_s[rd] + bcol_ref[...]
        triu = jnp.where(causal_t, 1.0, 0.0).astype(BF16)
        r1, r2, r3 = _split3(_log_sigmoid(gr))
        bcum_r = _dot(r1, triu) + _dot(r2, triu) + _dot(r3, triu)
        key_rows = gr[:MLSTM_HEADS, :] - bcum_r[MLSTM_HEADS:, :]
        ms["key_cols"] = jnp.concatenate([key_rows, jnp.zeros((LANES - MLSTM_HEADS, tm), F32)], axis=0).T
        ms["key_rows"], ms["bcum_r"], ms["uc"], ms["uct"] = key_rows, bcum_r, uc, uc.T

    def decay_weights(st, h, hs):
        b_row = ms["bcum_r"][MLSTM_HEADS + h:MLSTM_HEADS + h + 1, :]
        st["key_row"] = ms["key_rows"][h:h + 1, :]
        st["key_col"] = ms["key_cols"][:, h:h + 1]
        m_prev = m_ref[h][:, 0:1]
        dlog = jnp.where(causal_t, st["key_col"] + b_row, NEG_INF)
        inter = b_row + m_prev
        m_t = jnp.maximum(inter, jnp.max(dlog, axis=0, keepdims=True))
        st["w_intra"] = jnp.exp(dlog - m_t)
        st["w_inter"] = jnp.exp(inter - m_t)
        st["m_t"], st["m_prev"], st["b_end"] = m_t, m_prev, b_row[:, tm - 1:tm]

    def project_qk(st, h, hs):
        st["qtb"] = _dot(wqt_ref[h], ms["uct"][hs, :].astype(BF16)).astype(BF16)
        st["k"] = _dot(ms["uc"][:, hs].astype(BF16), wk_ref[h]) * (hd ** -0.5)

    def scores(st, h, hs):
        st["s"] = _dot(st["k"].astype(BF16), st["qtb"]) * st.pop("w_intra")

    def readout(st, h, hs):
        qtb, s, w_inter, m_t = st.pop("qtb"), st.pop("s"), st.pop("w_inter"), st.pop("m_t")
        n_prev = n_ref[h]
        n_hi = n_prev.astype(BF16)
        n_lo = (n_prev - n_hi.astype(F32)).astype(BF16)
        qn = (_dot(n_hi, qtb) + _dot(n_lo, qtb))[0:1, :]
        num = w_inter * _dot(c_ref[h].astype(BF16), qtb) + _dot(vmt_s[rd, hs, :], s.astype(BF16))
        den = w_inter * qn + jnp.sum(s, axis=0, keepdims=True)
        st["hh"] = num / jnp.maximum(jnp.abs(den), jnp.exp(-m_t))

    def update_state(st, h, hs):
        b_end, m_prev = st.pop("b_end"), st.pop("m_prev")
        m_new = jnp.maximum(b_end + m_prev, jnp.max(b_end + st.pop("key_row"), axis=1, keepdims=True))
        decay = jnp.exp(b_end + m_prev - m_new)
        kw = st.pop("k") * jnp.exp(b_end + st.pop("key_col") - m_new)
        n_prev = n_ref[h]
        c_ref[h] = decay * c_ref[h] + _dot(vmt_s[rd, hs, :], kw.astype(BF16))
        n_ref[h] = decay * n_prev + jnp.broadcast_to(jnp.sum(kw, axis=0, keepdims=True), n_prev.shape)
        m_ref[h] = jnp.broadcast_to(m_new, (1, LANES))

    def gate_and_norm(st, h, hs):
        hh = jax.nn.sigmoid(ot_s[rd, hs, :]) * st.pop("hh")
        mu = jnp.mean(hh, axis=0, keepdims=True)
        hc = hh - mu
        var = jnp.mean(hc * hc, axis=0, keepdims=True)
        yt_ref[hs, :] = hc * lax.rsqrt(var + GN_EPS) * gn_ref[hs, :] + skip_ref[hs, :] * ms["uct"][hs, :]

    def emit_y():
        y_ref[...] = yt_ref[...].T.astype(BF16)

    heads = _skewed((decay_weights, project_qk, scores, readout, update_state, gate_and_norm), MLSTM_HEADS, hd)
    _interleave(project, [prologue] + heads + [emit_y])


def _front(x2, ln_g, ln_b, wt, cos, sin, conv_w, conv_b, wqt, wk, bcol, gn_g, skip, batch, seq):
    n, d = x2.shape
    tm = ROW_TILE
    nsb = seq // tm
    nchunks = n // tm
    hd = ATTN_HEAD_DIM
    cur = lambda s: jnp.minimum(s, nchunks - 1)
    prev = lambda s: jnp.maximum(s - 1, 0)
    row = lambda w: pl.BlockSpec((tm, w), lambda s: (cur(s), 0))
    head = pl.BlockSpec((1, ATTN_HEADS, tm, hd), lambda s: (cur(s) // nsb, 0, cur(s) % nsb, 0))
    head_t = pl.BlockSpec((1, ATTN_HEADS, hd, tm), lambda s: (cur(s) // nsb, 0, 0, cur(s) % nsb))
    tab = pl.BlockSpec((tm, ATTN_WIDTH), lambda s: (cur(s) % nsb, 0))
    head_shape = jax.ShapeDtypeStruct((batch, ATTN_HEADS, seq, hd), BF16)
    head_t_shape = jax.ShapeDtypeStruct((batch, ATTN_HEADS, hd, seq), BF16)
    out_shape = (
        head_t_shape, head_shape, head_t_shape,
        jax.ShapeDtypeStruct((nchunks, 1, ATTN_WIDTH), F32),
        jax.ShapeDtypeStruct((n, d), BF16),
        jax.ShapeDtypeStruct((n, d), BF16),
        jax.ShapeDtypeStruct((n, d), F32),
        jax.ShapeDtypeStruct((n, MLSTM_WIDTH), BF16),
    )
    out_specs = (
        head_t, head, head_t,
        pl.BlockSpec((1, 1, ATTN_WIDTH), lambda s: (cur(s), 0, 0)),
        row(d), row(d), row(d),
        pl.BlockSpec((tm, MLSTM_WIDTH), lambda s: (prev(s), 0)),
    )
    mconsts = (conv_w, conv_b, wqt, wk, bcol, gn_g, skip)
    wt_spec = pl.BlockSpec(wt.shape, lambda s: (0, 0), pipeline_mode=pl.Buffered(1))
    in_specs = ([row(d), _full(ln_g.shape), _full(ln_b.shape), wt_spec, tab, tab] + [_full(a.shape) for a in mconsts])
    return pl.pallas_call(
        lambda *refs: _front_kernel(*refs, chunks_per_seq=nsb),
        grid=(nchunks + 1,), in_specs=in_specs, out_specs=out_specs, out_shape=out_shape,
        scratch_shapes=[pltpu.VMEM(((3 * ATTN_WIDTH + 3 * MLSTM_WIDTH + 2 * d) // WEIGHT_CHUNK, d, WEIGHT_CHUNK), BF16),
                        pltpu.VMEM((2 * MLSTM_HEADS, d), BF16),
                        pltpu.VMEM((2, tm, MLSTM_WIDTH), F32), pltpu.VMEM((2, MLSTM_WIDTH, tm), BF16),
                        pltpu.VMEM((2, MLSTM_WIDTH, tm), F32), pltpu.VMEM((2, SUBLANES, tm), F32),
                        pltpu.VMEM((SUBLANES + tm, MLSTM_WIDTH), F32),
                        pltpu.VMEM((MLSTM_HEADS, MLSTM_HEAD_DIM, MLSTM_HEAD_DIM), F32),
                        pltpu.VMEM((MLSTM_HEADS, SUBLANES, MLSTM_HEAD_DIM), F32),
                        pltpu.VMEM((MLSTM_HEADS, 1, LANES), F32),
                        pltpu.VMEM((MLSTM_WIDTH, tm), F32)],
        compiler_params=pltpu.CompilerParams(dimension_semantics=("arbitrary",), vmem_limit_bytes=INPROJ_VMEM_LIMIT),
        name="front",
    )(x2, ln_g, ln_b, wt, cos, sin, *mconsts)


def _moba_kernel(qt_ref, k_ref, vt_ref, km_ref, o_ref, bias_ref, m_ref, l_ref, acc_ref, s_ref):
    i = pl.program_id(1)
    blk = MOBA_BLOCK
    hd = ATTN_HEAD_DIM
    heads = ATTN_HEADS
    nb = k_ref.shape[2] // blk
    blk_id = lax.broadcasted_iota(jnp.int32, (nb, blk), 0)
    key_pos = lax.broadcasted_iota(jnp.int32, (blk, blk), 0)
    qry_pos = lax.broadcasted_iota(jnp.int32, (blk, blk), 1)
    causal = key_pos <= qry_pos

    for h in range(heads):
        qt = qt_ref[0, h]
        km = km_ref[0, h]
        km_hi = km.astype(BF16)
        km_lo = (km - km_hi.astype(F32)).astype(BF16)
        gate = _dot(km_hi, qt) + _dot(km_lo, qt)
        gate = jnp.where(blk_id < i, gate, NEG_INF)
        for j in range(nb - 1):
            row = gate[j:j + 1, :]
            beats = (gate > row) | ((gate == row) & (blk_id < j))
            cnt = jnp.sum(jnp.where(beats, 1.0, 0.0), axis=0, keepdims=True)
            sel = (cnt < float(MOBA_TOPK)) & (row > NEG_INF)
            bias_ref[j * heads + h] = jnp.where(sel, 0.0, NEG_INF)
    for h in range(heads):
        bias_ref[i * heads + h] = jnp.zeros((1, blk), F32)

    def scores(h, j, own_block):
        qt = qt_ref[0, h]
        half = blk // 2
        m_tile = None
        for c in range(2):
            rows = slice(c * half, (c + 1) * half)
            s = _dot(k_ref[0, h, pl.ds(pl.multiple_of(j * blk + c * half, half), half), :], qt)
            if own_block:
                s = jnp.where(causal[rows], s, NEG_INF)
            s_ref[j * heads + h, rows, :] = s
            m_c = jnp.max(s, axis=0, keepdims=True)
            m_tile = m_c if m_tile is None else jnp.maximum(m_tile, m_c)
        return m_tile

    for h in range(heads):
        m_ref[h] = scores(h, i, True)

    def past_scores(j):
        for h in range(heads):
            m_ref[h] = jnp.maximum(m_ref[h], scores(h, j, False) + bias_ref[j * heads + h])

    _loop_groups(i, past_scores)

    l_ref[...] = jnp.zeros_like(l_ref)
    acc_ref[...] = jnp.zeros_like(acc_ref)

    def accumulate(j):
        off = pl.multiple_of(j * blk, blk)
        for h in range(heads):
            p = jnp.exp2(s_ref[j * heads + h] - (m_ref[h] - bias_ref[j * heads + h]))
            l_ref[h] += jnp.sum(p, axis=0, keepdims=True)
            acc_ref[h] += _dot(vt_ref[0, h, :, pl.ds(off, blk)], p.astype(BF16))

    _loop_groups(i + 1, accumulate)
    yt = acc_ref[...] / l_ref[...]
    o_ref[0] = yt.reshape(heads * hd, blk).T.astype(BF16)


def _moba(qt, k, vt, km):
    batch, heads, seq, hd = k.shape
    blk = MOBA_BLOCK
    nb = seq // blk
    return pl.pallas_call(
        _moba_kernel, grid=(batch, nb),
        in_specs=[
            pl.BlockSpec((1, heads, hd, blk), lambda b, i: (b, 0, 0, i)),
            pl.BlockSpec((1, heads, seq, hd), lambda b, i: (b, 0, 0, 0)),
            pl.BlockSpec((1, heads, hd, seq), lambda b, i: (b, 0, 0, 0)),
            pl.BlockSpec((1, heads, nb, hd), lambda b, i: (b, 0, 0, 0)),
        ],
        out_specs=pl.BlockSpec((1, blk, heads * hd), lambda b, i: (b, i, 0)),
        out_shape=jax.ShapeDtypeStruct((batch, seq, heads * hd), BF16),
        scratch_shapes=[pltpu.VMEM((nb * heads, 1, blk), F32), pltpu.VMEM((heads, 1, blk), F32),
                        pltpu.VMEM((heads, 1, blk), F32), pltpu.VMEM((heads, hd, blk), F32),
                        pltpu.VMEM((nb * heads, blk, blk), F32)],
        compiler_params=_params("parallel", "arbitrary"), name="moba",
    )(qt, k, vt, km)


def _mix_kernel(xn_ref, ya_ref, ym_ref, ga_ref, gm_ref, wau_ref, wmu_ref, wout_ref,
                g1_ref, b1_ref, wrc_ref, br_ref,
                x1_ref, ri_ref, rw_ref, cnt_out_ref, cnt_ref):
    @pl.when(pl.program_id(0) == 0)
    def _():
        cnt_ref[...] = jnp.zeros_like(cnt_ref)

    tm = ROW_TILE
    sub = lax.broadcasted_iota(jnp.int32, (LANES, tm), 0).astype(F32)
    big = float(4 * LANES)

    def up_and_mix(st, c, rs):
        a_up = _dot(ya_ref[rs, :], wau_ref[...])
        m_up = _dot(ym_ref[rs, :], wmu_ref[...])
        mix = ga_ref[rs, :].astype(F32) * a_up + gm_ref[rs, :].astype(F32) * m_up
        st["mix"] = mix.astype(BF16)

    def out_and_norm(st, c, rs):
        x1 = _layer_norm(DEEPNORM_ALPHA * xn_ref[rs, :] + _dot(st.pop("mix"), wout_ref[...]), g1_ref[...], b1_ref[...])
        x1_ref[rs, :] = x1
        st["x1"] = x1

    def router_logits(st, c, rs):
        x1 = st.pop("x1")
        x_hi = x1.astype(BF16)
        x_lo = (x1 - x_hi.astype(F32)).astype(BF16)
        both = _dot_nt(wrc_ref[...], x_hi)
        st["logits"] = both[:LANES] + both[LANES:] + _dot_nt(wrc_ref[:LANES, :], x_lo) + br_ref[...]

    def route(st, c, rs):
        logits = st.pop("logits")
        is_g = (sub >= float(MOE_EXPERTS)) & (sub < float(MOE_EXPERTS + MOE_GROUPS))
        gl = jnp.where(is_g, logits, NEG_INF)
        ge = jnp.exp(gl - jnp.max(gl, axis=0, keepdims=True))
        gp = ge / jnp.sum(ge, axis=0, keepdims=True)
        g_w = jnp.max(gp, axis=0, keepdims=True)
        g_idx = jnp.min(jnp.where((gp == g_w) & is_g, sub - float(MOE_EXPERTS), big), axis=0, keepdims=True)
        lo = g_idx * float(MOE_EXPERTS_PER_GROUP)
        in_grp = (sub >= lo) & (sub < lo + float(MOE_EXPERTS_PER_GROUP))
        el = jnp.where(in_grp, logits, NEG_INF)
        v1 = jnp.max(el, axis=0, keepdims=True)
        i1 = jnp.min(jnp.where((el == v1) & in_grp, sub, big), axis=0, keepdims=True)
        el2 = jnp.where(sub == i1, NEG_INF, el)
        v2 = jnp.max(el2, axis=0, keepdims=True)
        i2 = jnp.min(jnp.where((el2 == v2) & in_grp & (sub != i1), sub, big), axis=0, keepdims=True)
        e2 = jnp.exp(v2 - v1)
        w0 = g_w / (1.0 + e2)
        w1 = g_w * e2 / (1.0 + e2)
        rw_ref[rs, :] = jnp.where(sub == 0.0, w0, jnp.where(sub == 1.0, w1, 0.0)).T
        st["i1"], st["i2"] = i1, i2

    def rank(st, c, rs):
        i1, i2 = st.pop("i1"), st.pop("i2")
        is1 = sub == i1
        is2 = sub == i2
        onehot = jnp.where(is1 | is2, 1.0, 0.0)
        rows = lax.broadcasted_iota(jnp.int32, (tm, tm), 0)
        cols = lax.broadcasted_iota(jnp.int32, (tm, tm), 1)
        earlier = jnp.where(rows < cols, 1.0, 0.0).astype(BF16)
        before = _dot(onehot.astype(BF16), earlier) + cnt_ref[...]
        r0 = jnp.sum(jnp.where(is1, before, 0.0), axis=0, keepdims=True)
        r1 = jnp.sum(jnp.where(is2, before, 0.0), axis=0, keepdims=True)
        total = cnt_ref[...] + jnp.sum(onehot, axis=1, keepdims=True)
        cnt_ref[...] = total
        cnt_out_ref[...] = total
        ri_t = jnp.where(sub == 0.0, i1, jnp.where(sub == 1.0, i2, jnp.where(sub == 2.0, r0, jnp.where(sub == 3.0, r1, 0.0))))
        ri_ref[:, rs] = ri_t[:SUBLANES, :].astype(jnp.int32)

    _run_skewed((up_and_mix, out_and_norm, router_logits, route, rank), xn_ref.shape[0] // tm, tm)


def _mix(xn, ya, ym, ga, gm, wau, wmu, wout, g1, b1, wrc, br):
    n, d = xn.shape
    tm = MIX_CHAINS * ROW_TILE
    row = lambda w: pl.BlockSpec((tm, w), lambda i: (i, 0))
    in_specs = [row(d), row(ATTN_WIDTH), row(MLSTM_WIDTH), row(d), row(d),
                _full(wau.shape), _full(wmu.shape), _full(wout.shape), _full(g1.shape), _full(b1.shape),
                _full(wrc.shape), _full(br.shape)]
    out_shape = (jax.ShapeDtypeStruct((n, d), F32), jax.ShapeDtypeStruct((SUBLANES, n), jnp.int32),
                 jax.ShapeDtypeStruct((n, LANES), F32), jax.ShapeDtypeStruct((LANES, 1), F32))
    out_specs = (row(d), pl.BlockSpec((SUBLANES, tm), lambda i: (0, i)), row(LANES), _full((LANES, 1)))
    return pl.pallas_call(
        _mix_kernel, grid=(n // tm,), in_specs=in_specs, out_specs=out_specs, out_shape=out_shape,
        scratch_shapes=[pltpu.VMEM((LANES, 1), F32)],
        compiler_params=_params("arbitrary"), name="mix",
    )(xn, ya, ym, ga, gm, wau, wmu, wout, g1, b1, wrc, br)


def _token_rows(d):
    return d // LANES


def _to_token_tiles(dst_ref, x):
    rows, d = x.shape
    nch = _token_rows(d)
    for c in range(nch):
        dst_ref[pl.ds(c, rows, stride=nch), :] = x[:, c * LANES:(c + 1) * LANES]


def _from_token_tiles(src_ref, rows, d):
    nch = _token_rows(d)
    return jnp.concatenate([src_ref[pl.ds(c, rows, stride=nch), :] for c in range(nch)], axis=1)


def _token_copy(src, src_tok, dst, dst_tok, nch, sem):
    s0 = pl.multiple_of(src_tok * nch, nch)
    d0 = pl.multiple_of(dst_tok * nch, nch)
    return pltpu.make_async_copy(src.at[pl.ds(s0, nch), :], dst.at[pl.ds(d0, nch), :], sem)


def _slots_kernel(ri_ref, ps_ref, o_ref):
    ri = ri_ref[...].astype(F32)
    ps = ps_ref[...]
    expert = lax.broadcasted_iota(jnp.int32, (ps.shape[0], ri.shape[1]), 0).astype(F32)
    row_id = lax.broadcasted_iota(jnp.int32, ri.shape, 0)
    out = jnp.zeros(ri.shape, F32)
    for k in range(2):
        start = jnp.sum(jnp.where(expert == ri[k:k + 1, :], jnp.broadcast_to(ps, expert.shape), 0.0),
                        axis=0, keepdims=True)
        out = jnp.where(row_id == k, start + ri[2 + k:3 + k, :], out)
    o_ref[...] = out.astype(jnp.int32)


def _slots(ri, pad_start_col):
    n = ri.shape[1]
    tm = SLOT_TILE
    blk = pl.BlockSpec((SUBLANES, tm), lambda i: (0, i))
    return pl.pallas_call(
        _slots_kernel, grid=(n // tm,), in_specs=[blk, _full(pad_start_col.shape)], out_specs=blk,
        out_shape=jax.ShapeDtypeStruct((SUBLANES, n), jnp.int32),
        compiler_params=_params("parallel"), name="slots",
    )(ri, pad_start_col)


def _slot(dest_ref, r, k):
    return dest_ref[k * ROW_TILE + r]


def _dispatch_kernel(dest_ref, last_ref, x_ref, xs_ref, scr_ref, zero_ref, sem, zsem):
    tm, d = x_ref.shape
    nch = _token_rows(d)
    tb = zero_ref.shape[0] // nch

    @pl.when(pl.program_id(0) == 0)
    def _():
        zero_ref[...] = jnp.zeros_like(zero_ref)

        def desc(tok):
            off = pl.multiple_of(jnp.maximum(tok, 0) * nch, nch)
            return pltpu.make_async_copy(zero_ref, xs_ref.at[pl.ds(off, tb * nch), :], zsem)

        def zstart(e, _):
            @pl.when(last_ref[e] >= 0)
            def _():
                desc(last_ref[e]).start()
            return 0

        def zwait(e, _):
            @pl.when(last_ref[e] >= 0)
            def _():
                desc(last_ref[e]).wait()
            return 0

        lax.fori_loop(0, MOE_EXPERTS, zstart, 0)
        nused = last_ref[MOE_EXPERTS]
        nblk = xs_ref.shape[0] // (tb * nch)
        lax.fori_loop(nused, nblk, lambda b, _: (desc(b * tb).start(), 0)[1], 0)
        lax.fori_loop(0, MOE_EXPERTS, zwait, 0)
        lax.fori_loop(nused, nblk, lambda b, _: (desc(b * tb).wait(), 0)[1], 0)

    step = pl.program_id(0)
    slot = step % 2
    scr = scr_ref.at[slot]
    _to_token_tiles(scr, x_ref[...])

    def start(r, _):
        for k in range(2):
            _token_copy(scr, r, xs_ref, _slot(dest_ref, r, k), nch, sem.at[slot]).start(priority=k)
        return 0

    def drain(which):
        def wait(r, _):
            for k in range(2):
                _token_copy(scr_ref.at[which], 0, xs_ref, 0, nch, sem.at[which]).wait()
            return 0
        lax.fori_loop(0, tm, wait, 0, unroll=8)

    lax.fori_loop(0, tm, start, 0, unroll=8)

    @pl.when(step > 0)
    def _():
        drain(1 - slot)

    @pl.when(step == pl.num_programs(0) - 1)
    def _():
        drain(slot)


def _dispatch(dest, last_blk, x1, n_rows):
    n, d = x1.shape
    tm = ROW_TILE
    nch = _token_rows(d)
    return pl.pallas_call(
        _dispatch_kernel, grid=(n // tm,),
        in_specs=[pl.BlockSpec((2 * tm,), lambda i: (i,), memory_space=pltpu.SMEM),
                  pl.BlockSpec(memory_space=pltpu.SMEM),
                  pl.BlockSpec((tm, d), lambda i: (i, 0))],
        out_specs=pl.BlockSpec(memory_space=pl.ANY),
        out_shape=jax.ShapeDtypeStruct((n_rows * nch, LANES), F32),
        scratch_shapes=[pltpu.VMEM((2, tm * nch, LANES), F32), pltpu.VMEM((EXPERT_TILE * nch, LANES), F32),
                        pltpu.SemaphoreType.DMA((2,)), pltpu.SemaphoreType.DMA(())],
        compiler_params=_params("arbitrary"), name="dispatch",
    )(dest, last_blk, x1)


def _expert_kernel(first_ref, count_ref, widx_ref, nused_ref, wg_ref, wu_ref, wd_ref, xs_ref, ys_ref,
                   wgb_ref, wub_ref, wdb_ref, xbuf_ref, ybuf_ref, in_sem, out_sem):
    del widx_ref
    e = pl.program_id(0)
    nused = nused_ref[0]
    d = wg_ref.shape[1]
    nch = _token_rows(d)
    rows = xbuf_ref.shape[1]
    tb = rows // nch

    def blk(ref, b):
        return ref.at[pl.ds(pl.multiple_of(b * rows, rows), rows), :]

    def in_copy(b, slot):
        return pltpu.make_async_copy(blk(xs_ref, b), xbuf_ref.at[slot], in_sem.at[slot])

    def out_copy(b, slot):
        return pltpu.make_async_copy(ybuf_ref.at[slot], blk(ys_ref, b), out_sem.at[slot])

    n_in = xbuf_ref.shape[0]

    @pl.when(e == 0)
    def _():
        for b0 in range(n_in - 1):
            @pl.when(b0 < nused)
            def _():
                in_copy(b0, b0).start()

    @pl.when(count_ref[e] > 0)
    def _():
        wgb_ref[...] = wg_ref[0].astype(BF16)
        wub_ref[...] = wu_ref[0].astype(BF16)
        wdb_ref[...] = wd_ref[0].astype(BF16)

    def body(b, _):
        slot = b % n_in
        oslot = b % 2
        in_copy(b, slot).wait()

        @pl.when(b + n_in - 1 < nused)
        def _():
            in_copy(b + n_in - 1, (b + n_in - 1) % n_in).start()

        @pl.when(b >= 2)
        def _():
            out_copy(b - 2, oslot).wait()

        xb = _from_token_tiles(xbuf_ref.at[slot], tb, d).astype(BF16)
        g = _dot(xb, wgb_ref[...])
        u = _dot(xb, wub_ref[...])
        hmid = g * jax.nn.sigmoid(g) * u
        _to_token_tiles(ybuf_ref.at[oslot], _dot(hmid.astype(BF16), wdb_ref[...]))
        out_copy(b, oslot).start()
        return 0

    lax.fori_loop(first_ref[e], first_ref[e] + count_ref[e], body, 0)

    @pl.when(e == pl.num_programs(0) - 1)
    def _():
        for back in (2, 1):
            @pl.when(nused >= back)
            def _():
                out_copy(nused - back, (nused - back) % 2).wait()


def _experts(first_blk, blk_count, w_idx, nused, xs, w_gate, w_up, w_down):
    n_exp, d, dff = w_gate.shape
    nch = _token_rows(d)
    rows = EXPERT_TILE * nch
    w_spec = lambda shape: pl.BlockSpec(shape, lambda e, fb, bc, wi, nu: (wi[e], 0, 0))
    any_spec = pl.BlockSpec(memory_space=pl.ANY)
    grid_spec = pltpu.PrefetchScalarGridSpec(
        num_scalar_prefetch=4, grid=(n_exp,),
        in_specs=[w_spec((1, d, dff)), w_spec((1, d, dff)), w_spec((1, dff, d)), any_spec],
        out_specs=any_spec,
        scratch_shapes=[pltpu.VMEM((d, dff), BF16), pltpu.VMEM((d, dff), BF16), pltpu.VMEM((dff, d), BF16),
                        pltpu.VMEM((EXPERT_IN_SLOTS, rows, LANES), F32), pltpu.VMEM((2, rows, LANES), F32),
                        pltpu.SemaphoreType.DMA((EXPERT_IN_SLOTS,)), pltpu.SemaphoreType.DMA((2,))],
    )
    return pl.pallas_call(
        _expert_kernel, grid_spec=grid_spec, out_shape=jax.ShapeDtypeStruct(xs.shape, F32),
        input_output_aliases={7: 0},
        compiler_params=_params("arbitrary"), name="experts",
    )(first_blk, blk_count, w_idx, nused, w_gate, w_up, w_down, xs)


def _combine_kernel(dest_ref, dest_next_ref, x1_ref, rw_ref, g_ref, b_ref, ys_ref, o_ref, buf_ref, sem):
    tm, d = x1_ref.shape
    nch = _token_rows(d)
    step = pl.program_id(0)
    slot = step % 2

    def gather(idx_ref, which):
        def start(r, _):
            for k in range(2):
                _token_copy(ys_ref, _slot(idx_ref, r, k), buf_ref.at[which, k], r, nch,
                            sem.at[which]).start(priority=k)
            return 0
        lax.fori_loop(0, tm, start, 0, unroll=8)

    @pl.when(step == 0)
    def _():
        gather(dest_ref, 0)

    @pl.when(step + 1 < pl.num_programs(0))
    def _():
        gather(dest_next_ref, 1 - slot)

    def wait(r, _):
        for k in range(2):
            _token_copy(ys_ref, 0, buf_ref.at[slot, k], 0, nch, sem.at[slot]).wait()
        return 0

    lax.fori_loop(0, tm, wait, 0, unroll=8)
    rw = rw_ref[...]
    y0 = _from_token_tiles(buf_ref.at[slot, 0], tm, d)
    y1 = _from_token_tiles(buf_ref.at[slot, 1], tm, d)
    ffn = rw[:, 0:1] * y0 + rw[:, 1:2] * y1
    o_ref[...] = _layer_norm(DEEPNORM_ALPHA * x1_ref[...] + ffn, g_ref[...], b_ref[...])


def _combine(dest, x1, rw, ln_g, ln_b, ys):
    n, d = x1.shape
    tm = ROW_TILE
    nch = _token_rows(d)
    last = n // tm - 1
    row = lambda w: pl.BlockSpec((tm, w), lambda i: (i, 0))
    return pl.pallas_call(
        _combine_kernel, grid=(n // tm,),
        in_specs=[pl.BlockSpec((2 * tm,), lambda i: (i,), memory_space=pltpu.SMEM),
                  pl.BlockSpec((2 * tm,), lambda i: (jnp.minimum(i + 1, last),), memory_space=pltpu.SMEM),
                  row(d), row(LANES), _full(ln_g.shape), _full(ln_b.shape),
                  pl.BlockSpec(memory_space=pl.ANY)],
        out_specs=row(d),
        out_shape=jax.ShapeDtypeStruct((n, d), F32),
        scratch_shapes=[pltpu.VMEM((2, 2, tm * nch, LANES), F32), pltpu.SemaphoreType.DMA((2,))],
        compiler_params=_params("arbitrary"), name="combine",
    )(dest, dest, x1, rw, ln_g, ln_b, ys)


def _rope_tables(seq):
    half = ATTN_HEAD_DIM // 2
    inv_freq = ROPE_THETA ** (-np.arange(half, dtype=np.float64) / half)
    ang = np.arange(seq, dtype=np.float64)[:, None] * inv_freq[None, :]
    cos = np.cos(ang)
    sin = np.sin(ang)
    cos_h = np.concatenate([cos, cos], axis=1)
    sin_h = np.concatenate([-sin, sin], axis=1)
    return (jnp.asarray(np.tile(cos_h, (1, ATTN_HEADS)), F32), jnp.asarray(np.tile(sin_h, (1, ATTN_HEADS)), F32))


def _pad_lanes(a, width=LANES):
    return jnp.pad(a, ((0, 0), (0, width - a.shape[1])))


def kernel(x, ln0_g, ln0_b, w_in, conv_w, conv_b, w_mq, w_mk, b_i, b_f, gn_g, skip, w_attn_up, w_mlstm_up, w_out,
           ln1_g, ln1_b, w_router_group, b_router_group, w_router_expert, b_router_expert, w_gate, w_up, w_down,
           ln2_g, ln2_b):
    batch, seq, d = x.shape
    n = batch * seq
    assert seq % ROW_TILE == 0 and ROW_TILE == MOBA_BLOCK and w_in.shape[0] == DEPTH
    x2 = x.reshape(n, d)
    vec = lambda a: a.reshape(1, -1).astype(F32)

    wt = w_in[0].T
    cos, sin = _rope_tables(seq)

    b_if = jnp.concatenate([b_i[0], b_f[0]]).astype(F32)
    q, k, v, kmean, ga, gm, xn, ym = _front(
        x2, vec(ln0_g), vec(ln0_b), wt, cos, sin,
        conv_w[0], vec(conv_b[0]), w_mq[0].transpose(0, 2, 1).astype(BF16), w_mk[0].astype(BF16), b_if[:, None],
        gn_g[0].astype(F32)[:, None], skip[0].astype(F32)[:, None], batch, seq)

    nb = seq // MOBA_BLOCK
    km = kmean.reshape(batch, nb, ATTN_HEADS, ATTN_HEAD_DIM).transpose(0, 2, 1, 3)
    ya = _moba(q, k, v, km).reshape(n, ATTN_WIDTH)

    w_r = _pad_lanes(jnp.concatenate([w_router_expert[0], w_router_group[0]], axis=1))
    w_r_hi = w_r.astype(BF16)
    w_r_lo = (w_r - w_r_hi.astype(F32)).astype(BF16)
    w_rc = jnp.concatenate([w_r_hi.T, w_r_lo.T], axis=0)
    b_r = _pad_lanes(jnp.concatenate([b_router_expert[0], b_router_group[0]])[None, :]).T
    x1, ri, rw, counts = _mix(
        xn, ya, ym, ga, gm, w_attn_up[0].astype(BF16), w_mlstm_up[0].astype(BF16),
        w_out[0].astype(BF16), vec(ln1_g[0]), vec(ln1_b[0]), w_rc, b_r)

    tb = EXPERT_TILE
    nblk = (2 * n) // tb + MOE_EXPERTS
    cnt = counts[:MOE_EXPERTS, 0].astype(jnp.int32)
    nblk_e = (cnt + tb - 1) // tb
    blk_end = jnp.cumsum(nblk_e)
    pad_start = (blk_end - nblk_e) * tb
    nused = blk_end[-1:]
    ids = jnp.arange(MOE_EXPERTS, dtype=jnp.int32)
    prev_used = jnp.max(jnp.where((ids[None, :] <= ids[:, None]) & (nblk_e[None, :] > 0), ids[None, :], -1), axis=1)
    first_used = jnp.min(jnp.where(nblk_e > 0, ids, MOE_EXPERTS - 1))
    w_idx = jnp.where(prev_used >= 0, prev_used, first_used).astype(jnp.int32)
    last_blk = jnp.where(nblk_e > 0, (blk_end - 1) * tb, -1)
    last_blk = jnp.concatenate([last_blk, nused]).astype(jnp.int32)
    dest = _slots(ri, pad_start.astype(F32)[:, None])
    dest = dest[:2].reshape(2, n // ROW_TILE, ROW_TILE).transpose(1, 0, 2).reshape(2 * n)

    xs = _dispatch(dest, last_blk, x1, nblk * tb)
    ys = _experts((blk_end - nblk_e).astype(jnp.int32), nblk_e.astype(jnp.int32), w_idx, nused.astype(jnp.int32),
                  xs, w_gate[0], w_up[0], w_down[0])
    out = _combine(dest, x1, rw, vec(ln2_g[0]), vec(ln2_b[0]), ys)
    return out.reshape(batch, seq, d)
```

```python
import math

import jax
import jax.numpy as jnp
import numpy as np
from jax import lax
from jax.experimental import pallas as pl
from jax.experimental.pallas import tpu as pltpu

F32 = jnp.float32
BF16 = jnp.bfloat16

ATTN_HEADS = 8
ATTN_HEAD_DIM = 64
ATTN_WIDTH = ATTN_HEADS * ATTN_HEAD_DIM
MOBA_BLOCK = 256
MOBA_TOPK = 3
ROPE_THETA = 10000.0
MLSTM_HEADS = 4
MLSTM_HEAD_DIM = 128
MLSTM_WIDTH = MLSTM_HEADS * MLSTM_HEAD_DIM
MLSTM_CONV = 4
MOE_GROUPS = 8
MOE_EXPERTS_PER_GROUP = 8
MOE_EXPERTS = MOE_GROUPS * MOE_EXPERTS_PER_GROUP
LN_EPS = 1e-5
GN_EPS = 1e-6
DEPTH = 1
DEEPNORM_ALPHA = (2 * DEPTH) ** 0.25

LANES = 128
SUBLANES = 8
ROW_TILE = 256
EXPERT_TILE = 256
EXPERT_IN_SLOTS = 4
EXPERT_PARTS = 4
SLOT_TILE = 2048
MIX_CHAINS = 4
VMEM_LIMIT = 48 * 1024 * 1024
INPROJ_VMEM_LIMIT = 58 * 1024 * 1024
WEIGHT_CHUNK = 512
LOG2_E = math.log2(math.e)

NEG_INF = float("-inf")


def _params(*sem):
    return pltpu.CompilerParams(dimension_semantics=sem, vmem_limit_bytes=VMEM_LIMIT)


def _dot(a, b):
    return jnp.dot(a, b, preferred_element_type=F32)


def _dot_nt(a, b):
    return lax.dot_general(a, b, (((1,), (1,)), ((), ())), preferred_element_type=F32)


def _split3(x):
    x1 = x.astype(BF16)
    r1 = x - x1.astype(F32)
    x2 = r1.astype(BF16)
    r2 = r1 - x2.astype(F32)
    return x1, x2, r2.astype(BF16)


def _layer_norm(x, g, b):
    mu = jnp.mean(x, axis=-1, keepdims=True)
    xc = x - mu
    var = jnp.mean(xc * xc, axis=-1, keepdims=True)
    return xc * lax.rsqrt(var + LN_EPS) * g + b


def _log_sigmoid(x):
    return jnp.minimum(x, 0.0) - jnp.log1p(jnp.exp(-jnp.abs(x)))


def _full(shape):
    nd = len(shape)
    return pl.BlockSpec(shape, lambda *_: (0,) * nd)


def _skewed(phases, chains, rows):
    states = [dict() for _ in range(chains)]

    def thunk(t, c):
        return lambda: phases[t - c](states[c], c, slice(c * rows, (c + 1) * rows))

    return [thunk(t, c) for t in range(chains + len(phases) - 1) for c in range(chains) if 0 <= t - c < len(phases)]


def _run_skewed(phases, chains, rows):
    for thunk in _skewed(phases, chains, rows):
        thunk()


def _interleave(a, b):
    ia = ib = 0
    while ia < len(a) or ib < len(b):
        if ib >= len(b) or (ia < len(a) and ia * len(b) <= ib * len(a)):
            a[ia]()
            ia += 1
        else:
            b[ib]()
            ib += 1


def _loop_groups(count, body, group=4):
    def trip(g, _):
        for d in range(group):
            body(g * group + d)
        return 0

    lax.fori_loop(0, count // group, trip, 0)
    done = (count // group) * group
    size = group // 2
    while size >= 1:
        take = ((count - done) // size) * size
        @pl.when(take > 0)
        def _(done=done, size=size):
            for d in range(size):
                body(done + d)
        done = done + take
        size //= 2


def _front_kernel(x_ref, g_ref, b_ref, wt_ref, cos_ref, sin_ref,
                  cw_ref, cb_ref, wqt_ref, wk_ref, bcol_ref, gn_ref, skip_ref,
                  q_ref, k_ref, v_ref, km_ref, ga_ref, gm_ref, xn_ref, y_ref,
                  wch_ref, wift_ref,
                  u_s, vmt_s, ot_s, ift_s, ext_ref, c_ref, n_ref, m_ref, yt_ref, *, chunks_per_seq):
    tm = ROW_TILE
    hd = MLSTM_HEAD_DIM
    halo = SUBLANES
    step = pl.program_id(0)
    wr = step % 2
    rd = 1 - wr

    @pl.when(step == 0)
    def _():
        c_if = 3 * ATTN_WIDTH + 3 * MLSTM_WIDTH
        for c in range(wch_ref.shape[0]):
            row0 = c * WEIGHT_CHUNK + (2 * MLSTM_HEADS if c * WEIGHT_CHUNK >= c_if else 0)
            wch_ref[c] = wt_ref[row0:row0 + WEIGHT_CHUNK, :].T.astype(BF16)
        wift_ref[...] = wt_ref[c_if:c_if + 2 * MLSTM_HEADS, :].astype(BF16)
        u_s[...] = jnp.zeros_like(u_s)
        vmt_s[...] = jnp.zeros_like(vmt_s)
        ot_s[...] = jnp.zeros_like(ot_s)
        ift_s[...] = jnp.zeros_like(ift_s)

    @pl.when(lax.rem(jnp.maximum(step - 1, 0), chunks_per_seq) == 0)
    def _():
        ext_ref[0:halo, :] = jnp.zeros((halo, MLSTM_WIDTH), F32)
        c_ref[...] = jnp.zeros_like(c_ref)
        n_ref[...] = jnp.zeros_like(n_ref)
        m_ref[...] = jnp.zeros_like(m_ref)

    lane = lax.broadcasted_iota(jnp.int32, (tm, ATTN_WIDTH), 1)
    first_half = (lane % ATTN_HEAD_DIM) < (ATTN_HEAD_DIM // 2)
    ps = {}

    def norm():
        xn = _layer_norm(x_ref[...], g_ref[...], b_ref[...])
        xn_ref[...] = xn
        ps["xb"] = xn.astype(BF16)

    def piece(key, chunk):
        def run():
            ps[key] = _dot(ps["xb"], wch_ref[chunk])
        return run

    def attn_outputs():
        cos = cos_ref[...]
        sin = sin_ref[...]

        def rope(t):
            fwd = pltpu.roll(t, ATTN_WIDTH - ATTN_HEAD_DIM // 2, axis=1)
            bwd = pltpu.roll(t, ATTN_HEAD_DIM // 2, axis=1)
            return t * cos + jnp.where(first_half, fwd, bwd) * sin

        q = rope(ps.pop("zq")) * (ATTN_HEAD_DIM ** -0.5 * LOG2_E)
        k = rope(ps.pop("zk"))
        km_ref[0] = jnp.mean(k, axis=0, keepdims=True)
        qt = q.T
        vt = ps.pop("zv").T
        for h in range(ATTN_HEADS):
            sl = slice(h * ATTN_HEAD_DIM, (h + 1) * ATTN_HEAD_DIM)
            q_ref[0, h] = qt[sl, :].astype(BF16)
            k_ref[0, h] = k[:, sl].astype(BF16)
            v_ref[0, h] = vt[sl, :].astype(BF16)

    def stage():
        u_s[wr] = ps.pop("zu")
        vmt_s[wr] = ps.pop("zvm").T.astype(BF16)
        ot_s[wr] = ps.pop("zo").T
        ift_s[wr] = _dot_nt(wift_ref[...], ps["xb"])

    d = ga_ref.shape[1]
    gw = d // 2

    def gate_out(key, ref, lo):
        def run():
            ref[:, lo:lo + gw] = jax.nn.sigmoid(ps.pop(key)).astype(BF16)
        return run

    assert ATTN_WIDTH == MLSTM_WIDTH == gw == WEIGHT_CHUNK
    project = [
        norm,
        piece("zq", 0), piece("zk", 1), piece("zv", 2),
        attn_outputs,
        piece("zu", 3), piece("zvm", 4), piece("zo", 5),
        stage,
        piece("g0", 6), piece("g1", 7), gate_out("g0", ga_ref, 0),
        piece("g2", 8), gate_out("g1", ga_ref, gw),
        piece("g3", 9), gate_out("g2", gm_ref, 0), gate_out("g3", gm_ref, gw),
    ]

    ms = {}
    rows = lax.broadcasted_iota(jnp.int32, (tm, tm), 0)
    cols = lax.broadcasted_iota(jnp.int32, (tm, tm), 1)
    causal_t = rows <= cols

    def prologue():
        u = u_s[rd]
        ext_ref[halo:halo + tm, :] = u
        acc = jnp.broadcast_to(cb_ref[...], u.shape)
        for j in range(MLSTM_CONV):
            acc = acc + cw_ref[j:j + 1, :] * ext_ref[halo - (MLSTM_CONV - 1) + j:halo - (MLSTM_CONV - 1) + j + tm, :]
        ext_ref[0:halo, :] = u[tm - halo:, :]
        uc = acc * jax.nn.sigmoid(acc)
        gr = ift_s[rd] + bcol_ref[...]
        triu = jnp.where(causal_t, 1.0, 0.0).astype(BF16)
        r1, r2, r3 = _split3(_log_sigmoid(gr))
        bcum_r = _dot(r1, triu) + _dot(r2, triu) + _dot(r3, triu)
        key_rows = gr[:MLSTM_HEADS, :] - bcum_r[MLSTM_HEADS:, :]
        ms["key_cols"] = jnp.concatenate([key_rows, jnp.zeros((LANES - MLSTM_HEADS, tm), F32)], axis=0).T
        ms["key_rows"], ms["bcum_r"], ms["uc"], ms["uct"] = key_rows, bcum_r, uc, uc.T

    def decay_weights(st, h, hs):
        b_row = ms["bcum_r"][MLSTM_HEADS + h:MLSTM_HEADS + h + 1, :]
        st["key_row"] = ms["key_rows"][h:h + 1, :]
        st["key_col"] = ms["key_cols"][:, h:h + 1]
        m_prev = m_ref[h][:, 0:1]
        dlog = jnp.where(causal_t, st["key_col"] + b_row, NEG_INF)
        inter = b_row + m_prev
        m_t = jnp.maximum(inter, jnp.max(dlog, axis=0, keepdims=True))
        st["w_intra"] = jnp.exp(dlog - m_t)
        st["w_inter"] = jnp.exp(inter - m_t)
        st["m_t"], st["m_prev"], st["b_end"] = m_t, m_prev, b_row[:, tm - 1:tm]

    def project_qk(st, h, hs):
        st["qtb"] = _dot(wqt_ref[h], ms["uct"][hs, :].astype(BF16)).astype(BF16)
        st["k"] = _dot(ms["uc"][:, hs].astype(BF16), wk_ref[h]) * (hd ** -0.5)

    def scores(st, h, hs):
        st["s"] = _dot(st["k"].astype(BF16), st["qtb"]) * st.pop("w_intra")

    def readout(st, h, hs):
        qtb, s, w_inter, m_t = st.pop("qtb"), st.pop("s"), st.pop("w_inter"), st.pop("m_t")
        n_prev = n_ref[h]
        n_hi = n_prev.astype(BF16)
        n_lo = (n_prev - n_hi.astype(F32)).astype(BF16)
        qn = (_dot(n_hi, qtb) + _dot(n_lo, qtb))[0:1, :]
        num = w_inter * _dot(c_ref[h].astype(BF16), qtb) + _dot(vmt_s[rd, hs, :], s.astype(BF16))
        den = w_inter * qn + jnp.sum(s, axis=0, keepdims=True)
        st["hh"] = num / jnp.maximum(jnp.abs(den), jnp.exp(-m_t))

    def update_state(st, h, hs):
        b_end, m_prev = st.pop("b_end"), st.pop("m_prev")
        m_new = jnp.maximum(b_end + m_prev, jnp.max(b_end + st.pop("key_row"), axis=1, keepdims=True))
        decay = jnp.exp(b_end + m_prev - m_new)
        kw = st.pop("k") * jnp.exp(b_end + st.pop("key_col") - m_new)
        n_prev = n_ref[h]
        c_ref[h] = decay * c_ref[h] + _dot(vmt_s[rd, hs, :], kw.astype(BF16))
        n_ref[h] = decay * n_prev + jnp.broadcast_to(jnp.sum(kw, axis=0, keepdims=True), n_prev.shape)
        m_ref[h] = jnp.broadcast_to(m_new, (1, LANES))

    def gate_and_norm(st, h, hs):
        hh = jax.nn.sigmoid(ot_s[rd, hs, :]) * st.pop("hh")
        mu = jnp.mean(hh, axis=0, keepdims=True)
        hc = hh - mu
        var = jnp.mean(hc * hc, axis=0, keepdims=True)
        yt_ref[hs, :] = hc * lax.rsqrt(var + GN_EPS) * gn_ref[hs, :] + skip_ref[hs, :] * ms["uct"][hs, :]

    def emit_y():
        y_ref[...] = yt_ref[...].T.astype(BF16)

    heads = _skewed((decay_weights, project_qk, scores, readout, update_state, gate_and_norm), MLSTM_HEADS, hd)
    _interleave(project, [prologue] + heads + [emit_y])


def _front(x2, ln_g, ln_b, wt, cos, sin, conv_w, conv_b, wqt, wk, bcol, gn_g, skip, batch, seq):
    n, d = x2.shape
    tm = ROW_TILE
    nsb = seq // tm
    nchunks = n // tm
    hd = ATTN_HEAD_DIM
    cur = lambda s: jnp.minimum(s, nchunks - 1)
    prev = lambda s: jnp.maximum(s - 1, 0)
    row = lambda w: pl.BlockSpec((tm, w), lambda s: (cur(s), 0))
    head = pl.BlockSpec((1, ATTN_HEADS, tm, hd), lambda s: (cur(s) // nsb, 0, cur(s) % nsb, 0))
    head_t = pl.BlockSpec((1, ATTN_HEADS, hd, tm), lambda s: (cur(s) // nsb, 0, 0, cur(s) % nsb))
    tab = pl.BlockSpec((tm, ATTN_WIDTH), lambda s: (cur(s) % nsb, 0))
    head_shape = jax.ShapeDtypeStruct((batch, ATTN_HEADS, seq, hd), BF16)
    head_t_shape = jax.ShapeDtypeStruct((batch, ATTN_HEADS, hd, seq), BF16)
    out_shape = (
        head_t_shape, head_shape, head_t_shape,
        jax.ShapeDtypeStruct((nchunks, 1, ATTN_WIDTH), F32),
        jax.ShapeDtypeStruct((n, d), BF16),
        jax.ShapeDtypeStruct((n, d), BF16),
        jax.ShapeDtypeStruct((n, d), F32),
        jax.ShapeDtypeStruct((n, MLSTM_WIDTH), BF16),
    )
    out_specs = (
        head_t, head, head_t,
        pl.BlockSpec((1, 1, ATTN_WIDTH), lambda s: (cur(s), 0, 0)),
        row(d), row(d), row(d),
        pl.BlockSpec((tm, MLSTM_WIDTH), lambda s: (prev(s), 0)),
    )
    mconsts = (conv_w, conv_b, wqt, wk, bcol, gn_g, skip)
    wt_spec = pl.BlockSpec(wt.shape, lambda s: (0, 0), pipeline_mode=pl.Buffered(1))
    in_specs = ([row(d), _full(ln_g.shape), _full(ln_b.shape), wt_spec, tab, tab] + [_full(a.shape) for a in mconsts])
    return pl.pallas_call(
        lambda *refs: _front_kernel(*refs, chunks_per_seq=nsb),
        grid=(nchunks + 1,), in_specs=in_specs, out_specs=out_specs, out_shape=out_shape,
        scratch_shapes=[pltpu.VMEM(((3 * ATTN_WIDTH + 3 * MLSTM_WIDTH + 2 * d) // WEIGHT_CHUNK, d, WEIGHT_CHUNK), BF16),
                        pltpu.VMEM((2 * MLSTM_HEADS, d), BF16),
                        pltpu.VMEM((2, tm, MLSTM_WIDTH), F32), pltpu.VMEM((2, MLSTM_WIDTH, tm), BF16),
                        pltpu.VMEM((2, MLSTM_WIDTH, tm), F32), pltpu.VMEM((2, SUBLANES, tm), F32),
                        pltpu.VMEM((SUBLANES + tm, MLSTM_WIDTH), F32),
                        pltpu.VMEM((MLSTM_HEADS, MLSTM_HEAD_DIM, MLSTM_HEAD_DIM), F32),
                        pltpu.VMEM((MLSTM_HEADS, SUBLANES, MLSTM_HEAD_DIM), F32),
                        pltpu.VMEM((MLSTM_HEADS, 1, LANES), F32),
                        pltpu.VMEM((MLSTM_WIDTH, tm), F32)],
        compiler_params=pltpu.CompilerParams(dimension_semantics=("arbitrary",), vmem_limit_bytes=INPROJ_VMEM_LIMIT),
        name="front",
    )(x2, ln_g, ln_b, wt, cos, sin, *mconsts)


def _moba_kernel(qt_ref, k_ref, vt_ref, km_ref, o_ref, bias_ref, m_ref, l_ref, acc_ref, s_ref):
    i = pl.program_id(1)
    blk = MOBA_BLOCK
    hd = ATTN_HEAD_DIM
    heads = ATTN_HEADS
    nb = k_ref.shape[2] // blk
    blk_id = lax.broadcasted_iota(jnp.int32, (nb, blk), 0)
    key_pos = lax.broadcasted_iota(jnp.int32, (blk, blk), 0)
    qry_pos = lax.broadcasted_iota(jnp.int32, (blk, blk), 1)
    causal = key_pos <= qry_pos

    for h in range(heads):
        qt = qt_ref[0, h]
        km = km_ref[0, h]
        km_hi = km.astype(BF16)
        km_lo = (km - km_hi.astype(F32)).astype(BF16)
        gate = _dot(km_hi, qt) + _dot(km_lo, qt)
        gate = jnp.where(blk_id < i, gate, NEG_INF)
        for j in range(nb - 1):
            row = gate[j:j + 1, :]
            beats = (gate > row) | ((gate == row) & (blk_id < j))
            cnt = jnp.sum(jnp.where(beats, 1.0, 0.0), axis=0, keepdims=True)
            sel = (cnt < float(MOBA_TOPK)) & (row > NEG_INF)
            bias_ref[j * heads + h] = jnp.where(sel, 0.0, NEG_INF)
    for h in range(heads):
        bias_ref[i * heads + h] = jnp.zeros((1, blk), F32)

    def scores(h, j, own_block):
        qt = qt_ref[0, h]
        half = blk // 2
        m_tile = None
        for c in range(2):
            rows = slice(c * half, (c + 1) * half)
            s = _dot(k_ref[0, h, pl.ds(pl.multiple_of(j * blk + c * half, half), half), :], qt)
            if own_block:
                s = jnp.where(causal[rows], s, NEG_INF)
            s_ref[j * heads + h, rows, :] = s
            m_c = jnp.max(s, axis=0, keepdims=True)
            m_tile = m_c if m_tile is None else jnp.maximum(m_tile, m_c)
        return m_tile

    for h in range(heads):
        m_ref[h] = scores(h, i, True)

    def past_scores(j):
        for h in range(heads):
            m_ref[h] = jnp.maximum(m_ref[h], scores(h, j, False) + bias_ref[j * heads + h])

    _loop_groups(i, past_scores)

    l_ref[...] = jnp.zeros_like(l_ref)
    acc_ref[...] = jnp.zeros_like(acc_ref)

    def accumulate(j):
        off = pl.multiple_of(j * blk, blk)
        for h in range(heads):
            p = jnp.exp2(s_ref[j * heads + h] - (m_ref[h] - bias_ref[j * heads + h]))
            l_ref[h] += jnp.sum(p, axis=0, keepdims=True)
            acc_ref[h] += _dot(vt_ref[0, h, :, pl.ds(off, blk)], p.astype(BF16))

    _loop_groups(i + 1, accumulate)
    yt = acc_ref[...] / l_ref[...]
    o_ref[0] = yt.reshape(heads * hd, blk).T.astype(BF16)


def _moba(qt, k, vt, km):
    batch, heads, seq, hd = k.shape
    blk = MOBA_BLOCK
    nb = seq // blk
    return pl.pallas_call(
        _moba_kernel, grid=(batch, nb),
        in_specs=[
            pl.BlockSpec((1, heads, hd, blk), lambda b, i: (b, 0, 0, i)),
            pl.BlockSpec((1, heads, seq, hd), lambda b, i: (b, 0, 0, 0)),
            pl.BlockSpec((1, heads, hd, seq), lambda b, i: (b, 0, 0, 0)),
            pl.BlockSpec((1, heads, nb, hd), lambda b, i: (b, 0, 0, 0)),
        ],
        out_specs=pl.BlockSpec((1, blk, heads * hd), lambda b, i: (b, i, 0)),
        out_shape=jax.ShapeDtypeStruct((batch, seq, heads * hd), BF16),
        scratch_shapes=[pltpu.VMEM((nb * heads, 1, blk), F32), pltpu.VMEM((heads, 1, blk), F32),
                        pltpu.VMEM((heads, 1, blk), F32), pltpu.VMEM((heads, hd, blk), F32),
                        pltpu.VMEM((nb * heads, blk, blk), F32)],
        compiler_params=_params("parallel", "arbitrary"), name="moba",
    )(qt, k, vt, km)


def _mix_kernel(xn_ref, ya_ref, ym_ref, ga_ref, gm_ref, wau_ref, wmu_ref, wout_ref,
                g1_ref, b1_ref, wrc_ref, br_ref,
                x1_ref, ri_ref, rw_ref, cnt_out_ref, cnt_ref):
    @pl.when(pl.program_id(0) == 0)
    def _():
        cnt_ref[...] = jnp.zeros_like(cnt_ref)

    tm = ROW_TILE
    sub = lax.broadcasted_iota(jnp.int32, (LANES, tm), 0).astype(F32)
    big = float(4 * LANES)

    def up_and_mix(st, c, rs):
        a_up = _dot(ya_ref[rs, :], wau_ref[...])
        m_up = _dot(ym_ref[rs, :], wmu_ref[...])
        mix = ga_ref[rs, :].astype(F32) * a_up + gm_ref[rs, :].astype(F32) * m_up
        st["mix"] = mix.astype(BF16)

    def out_and_norm(st, c, rs):
        x1 = _layer_norm(DEEPNORM_ALPHA * xn_ref[rs, :] + _dot(st.pop("mix"), wout_ref[...]), g1_ref[...], b1_ref[...])
        x1_ref[rs, :] = x1
        st["x1"] = x1

    def router_logits(st, c, rs):
        x1 = st.pop("x1")
        x_hi = x1.astype(BF16)
        x_lo = (x1 - x_hi.astype(F32)).astype(BF16)
        both = _dot_nt(wrc_ref[...], x_hi)
        st["logits"] = both[:LANES] + both[LANES:] + _dot_nt(wrc_ref[:LANES, :], x_lo) + br_ref[...]

    def route(st, c, rs):
        logits = st.pop("logits")
        is_g = (sub >= float(MOE_EXPERTS)) & (sub < float(MOE_EXPERTS + MOE_GROUPS))
        gl = jnp.where(is_g, logits, NEG_INF)
        ge = jnp.exp(gl - jnp.max(gl, axis=0, keepdims=True))
        gp = ge / jnp.sum(ge, axis=0, keepdims=True)
        g_w = jnp.max(gp, axis=0, keepdims=True)
        g_idx = jnp.min(jnp.where((gp == g_w) & is_g, sub - float(MOE_EXPERTS), big), axis=0, keepdims=True)
        lo = g_idx * float(MOE_EXPERTS_PER_GROUP)
        in_grp = (sub >= lo) & (sub < lo + float(MOE_EXPERTS_PER_GROUP))
        el = jnp.where(in_grp, logits, NEG_INF)
        v1 = jnp.max(el, axis=0, keepdims=True)
        i1 = jnp.min(jnp.where((el == v1) & in_grp, sub, big), axis=0, keepdims=True)
        el2 = jnp.where(sub == i1, NEG_INF, el)
        v2 = jnp.max(el2, axis=0, keepdims=True)
        i2 = jnp.min(jnp.where((el2 == v2) & in_grp & (sub != i1), sub, big), axis=0, keepdims=True)
        e2 = jnp.exp(v2 - v1)
        w0 = g_w / (1.0 + e2)
        w1 = g_w * e2 / (1.0 + e2)
        rw_ref[rs, :] = jnp.where(sub == 0.0, w0, jnp.where(sub == 1.0, w1, 0.0)).T
        st["i1"], st["i2"] = i1, i2

    def rank(st, c, rs):
        i1, i2 = st.pop("i1"), st.pop("i2")
        is1 = sub == i1
        is2 = sub == i2
        onehot = jnp.where(is1 | is2, 1.0, 0.0)
        rows = lax.broadcasted_iota(jnp.int32, (tm, tm), 0)
        cols = lax.broadcasted_iota(jnp.int32, (tm, tm), 1)
        earlier = jnp.where(rows < cols, 1.0, 0.0).astype(BF16)
        before = _dot(onehot.astype(BF16), earlier) + cnt_ref[...]
        r0 = jnp.sum(jnp.where(is1, before, 0.0), axis=0, keepdims=True)
        r1 = jnp.sum(jnp.where(is2, before, 0.0), axis=0, keepdims=True)
        total = cnt_ref[...] + jnp.sum(onehot, axis=1, keepdims=True)
        cnt_ref[...] = total
        cnt_out_ref[...] = total
        ri_t = jnp.where(sub == 0.0, i1, jnp.where(sub == 1.0, i2, jnp.where(sub == 2.0, r0, jnp.where(sub == 3.0, r1, 0.0))))
        ri_ref[:, rs] = ri_t[:SUBLANES, :].astype(jnp.int32)

    _run_skewed((up_and_mix, out_and_norm, router_logits, route, rank), xn_ref.shape[0] // tm, tm)


def _mix(xn, ya, ym, ga, gm, wau, wmu, wout, g1, b1, wrc, br):
    n, d = xn.shape
    tm = MIX_CHAINS * ROW_TILE
    row = lambda w: pl.BlockSpec((tm, w), lambda i: (i, 0))
    in_specs = [row(d), row(ATTN_WIDTH), row(MLSTM_WIDTH), row(d), row(d),
                _full(wau.shape), _full(wmu.shape), _full(wout.shape), _full(g1.shape), _full(b1.shape),
                _full(wrc.shape), _full(br.shape)]
    out_shape = (jax.ShapeDtypeStruct((n, d), F32), jax.ShapeDtypeStruct((SUBLANES, n), jnp.int32),
                 jax.ShapeDtypeStruct((n, LANES), F32), jax.ShapeDtypeStruct((LANES, 1), F32))
    out_specs = (row(d), pl.BlockSpec((SUBLANES, tm), lambda i: (0, i)), row(LANES), _full((LANES, 1)))
    return pl.pallas_call(
        _mix_kernel, grid=(n // tm,), in_specs=in_specs, out_specs=out_specs, out_shape=out_shape,
        scratch_shapes=[pltpu.VMEM((LANES, 1), F32)],
        compiler_params=_params("arbitrary"), name="mix",
    )(xn, ya, ym, ga, gm, wau, wmu, wout, g1, b1, wrc, br)


def _token_rows(d):
    return d // LANES


def _to_token_tiles(dst_ref, x):
    rows, d = x.shape
    nch = _token_rows(d)
    for c in range(nch):
        dst_ref[pl.ds(c, rows, stride=nch), :] = x[:, c * LANES:(c + 1) * LANES]


def _from_token_tiles(src_ref, rows, d):
    nch = _token_rows(d)
    return jnp.concatenate([src_ref[pl.ds(c, rows, stride=nch), :] for c in range(nch)], axis=1)


def _token_copy(src, src_tok, dst, dst_tok, nch, sem):
    s0 = pl.multiple_of(src_tok * nch, nch)
    d0 = pl.multiple_of(dst_tok * nch, nch)
    return pltpu.make_async_copy(src.at[pl.ds(s0, nch), :], dst.at[pl.ds(d0, nch), :], sem)


def _slots_kernel(ri_ref, ps_ref, o_ref):
    ri = ri_ref[...].astype(F32)
    ps = ps_ref[...]
    expert = lax.broadcasted_iota(jnp.int32, (ps.shape[0], ri.shape[1]), 0).astype(F32)
    row_id = lax.broadcasted_iota(jnp.int32, ri.shape, 0)
    out = jnp.zeros(ri.shape, F32)
    for k in range(2):
        start = jnp.sum(jnp.where(expert == ri[k:k + 1, :], jnp.broadcast_to(ps, expert.shape), 0.0),
                        axis=0, keepdims=True)
        out = jnp.where(row_id == k, start + ri[2 + k:3 + k, :], out)
    o_ref[...] = out.astype(jnp.int32)


def _slots(ri, pad_start_col):
    n = ri.shape[1]
    tm = SLOT_TILE
    blk = pl.BlockSpec((SUBLANES, tm), lambda i: (0, i))
    return pl.pallas_call(
        _slots_kernel, grid=(n // tm,), in_specs=[blk, _full(pad_start_col.shape)], out_specs=blk,
        out_shape=jax.ShapeDtypeStruct((SUBLANES, n), jnp.int32),
        compiler_params=_params("parallel"), name="slots",
    )(ri, pad_start_col)


def _slot(dest_ref, r, k):
    return dest_ref[k * ROW_TILE + r]


def _dispatch_kernel(dest_ref, last_ref, x_ref, xs_ref, scr_ref, zero_ref, sem, zsem):
    tm, d = x_ref.shape
    nch = _token_rows(d)
    tb = zero_ref.shape[0] // nch

    @pl.when(pl.program_id(0) == 0)
    def _():
        zero_ref[...] = jnp.zeros_like(zero_ref)

        def desc(tok):
            off = pl.multiple_of(jnp.maximum(tok, 0) * nch, nch)
            return pltpu.make_async_copy(zero_ref, xs_ref.at[pl.ds(off, tb * nch), :], zsem)

        def zstart(e, _):
            @pl.when(last_ref[e] >= 0)
            def _():
                desc(last_ref[e]).start()
            return 0

        def zwait(e, _):
            @pl.when(last_ref[e] >= 0)
            def _():
                desc(last_ref[e]).wait()
            return 0

        lax.fori_loop(0, MOE_EXPERTS, zstart, 0)
        nused = last_ref[MOE_EXPERTS]
        nblk = xs_ref.shape[0] // (tb * nch)
        lax.fori_loop(nused, nblk, lambda b, _: (desc(b * tb).start(), 0)[1], 0)
        lax.fori_loop(0, MOE_EXPERTS, zwait, 0)
        lax.fori_loop(nused, nblk, lambda b, _: (desc(b * tb).wait(), 0)[1], 0)

    step = pl.program_id(0)
    slot = step % 2
    scr = scr_ref.at[slot]
    _to_token_tiles(scr, x_ref[...])

    def start(r, _):
        for k in range(2):
            _token_copy(scr, r, xs_ref, _slot(dest_ref, r, k), nch, sem.at[slot]).start(priority=k)
        return 0

    def drain(which):
        def wait(r, _):
            for k in range(2):
                _token_copy(scr_ref.at[which], 0, xs_ref, 0, nch, sem.at[which]).wait()
            return 0
        lax.fori_loop(0, tm, wait, 0, unroll=8)

    lax.fori_loop(0, tm, start, 0, unroll=8)

    @pl.when(step > 0)
    def _():
        drain(1 - slot)

    @pl.when(step == pl.num_programs(0) - 1)
    def _():
        drain(slot)


def _dispatch(dest, last_blk, x1, n_rows):
    n, d = x1.shape
    tm = ROW_TILE
    nch = _token_rows(d)
    return pl.pallas_call(
        _dispatch_kernel, grid=(n // tm,),
        in_specs=[pl.BlockSpec((2 * tm,), lambda i: (i,), memory_space=pltpu.SMEM),
                  pl.BlockSpec(memory_space=pltpu.SMEM),
                  pl.BlockSpec((tm, d), lambda i: (i, 0))],
        out_specs=pl.BlockSpec(memory_space=pl.ANY),
        out_shape=jax.ShapeDtypeStruct((n_rows * nch, LANES), F32),
        scratch_shapes=[pltpu.VMEM((2, tm * nch, LANES), F32), pltpu.VMEM((EXPERT_TILE * nch, LANES), F32),
                        pltpu.SemaphoreType.DMA((2,)), pltpu.SemaphoreType.DMA(())],
        compiler_params=_params("arbitrary"), name="dispatch",
    )(dest, last_blk, x1)


def _expert_kernel(first_ref, count_ref, widx_ref, nused_ref, parts_ref, wg_ref, wu_ref, wd_ref, xs_ref, ys_ref,
                   wgb_ref, wub_ref, wdb_ref, xbuf_ref, ybuf_ref, in_sem, out_sem):
    del widx_ref
    e = pl.program_id(0)
    nused = nused_ref[0]
    d = wg_ref.shape[1]
    nch = _token_rows(d)
    rows = xbuf_ref.shape[1]
    tb = rows // nch
    prow = rows // EXPERT_PARTS

    def in_part(b, slot, p):
        src = xs_ref.at[pl.ds(pl.multiple_of(b * rows + p * prow, prow), prow), :]
        return pltpu.make_async_copy(src, xbuf_ref.at[slot, pl.ds(p * prow, prow), :], in_sem.at[slot])

    def out_part(b, slot, p):
        dst = ys_ref.at[pl.ds(pl.multiple_of(b * rows + p * prow, prow), prow), :]
        return pltpu.make_async_copy(ybuf_ref.at[slot, pl.ds(p * prow, prow), :], dst, out_sem.at[slot])

    def for_parts(b, fn):
        used = parts_ref[b]
        for p in range(EXPERT_PARTS):
            @pl.when(p < used)
            def _(p=p):
                fn(p)

    class _Block:
        def __init__(self, part, b, slot):
            self.part, self.b, self.slot = part, b, slot

        def start(self):
            for_parts(self.b, lambda p: self.part(self.b, self.slot, p).start())

        def wait(self):
            for_parts(self.b, lambda p: self.part(self.b, self.slot, p).wait())

    def in_copy(b, slot):
        return _Block(in_part, b, slot)

    def out_copy(b, slot):
        return _Block(out_part, b, slot)

    n_in = xbuf_ref.shape[0]

    @pl.when(e == 0)
    def _():
        xbuf_ref[...] = jnp.zeros_like(xbuf_ref)
        for b0 in range(n_in - 1):
            @pl.when(b0 < nused)
            def _():
                in_copy(b0, b0).start()

    @pl.when(count_ref[e] > 0)
    def _():
        wgb_ref[...] = wg_ref[0].astype(BF16)
        wub_ref[...] = wu_ref[0].astype(BF16)
        wdb_ref[...] = wd_ref[0].astype(BF16)

    def body(b, _):
        slot = b % n_in
        oslot = b % 2
        in_copy(b, slot).wait()

        @pl.when(b + n_in - 1 < nused)
        def _():
            in_copy(b + n_in - 1, (b + n_in - 1) % n_in).start()

        @pl.when(b >= 2)
        def _():
            out_copy(b - 2, oslot).wait()

        xb = _from_token_tiles(xbuf_ref.at[slot], tb, d).astype(BF16)
        g = _dot(xb, wgb_ref[...])
        u = _dot(xb, wub_ref[...])
        hmid = g * jax.nn.sigmoid(g) * u
        _to_token_tiles(ybuf_ref.at[oslot], _dot(hmid.astype(BF16), wdb_ref[...]))
        out_copy(b, oslot).start()
        return 0

    lax.fori_loop(first_ref[e], first_ref[e] + count_ref[e], body, 0)

    @pl.when(e == pl.num_programs(0) - 1)
    def _():
        for back in (2, 1):
            @pl.when(nused >= back)
            def _():
                out_copy(nused - back, (nused - back) % 2).wait()


def _experts(first_blk, blk_count, w_idx, nused, blk_parts, xs, w_gate, w_up, w_down):
    n_exp, d, dff = w_gate.shape
    nch = _token_rows(d)
    rows = EXPERT_TILE * nch
    w_spec = lambda shape: pl.BlockSpec(shape, lambda e, fb, bc, wi, nu, bp: (wi[e], 0, 0))
    any_spec = pl.BlockSpec(memory_space=pl.ANY)
    grid_spec = pltpu.PrefetchScalarGridSpec(
        num_scalar_prefetch=5, grid=(n_exp,),
        in_specs=[w_spec((1, d, dff)), w_spec((1, d, dff)), w_spec((1, dff, d)), any_spec],
        out_specs=any_spec,
        scratch_shapes=[pltpu.VMEM((d, dff), BF16), pltpu.VMEM((d, dff), BF16), pltpu.VMEM((dff, d), BF16),
                        pltpu.VMEM((EXPERT_IN_SLOTS, rows, LANES), F32), pltpu.VMEM((2, rows, LANES), F32),
                        pltpu.SemaphoreType.DMA((EXPERT_IN_SLOTS,)), pltpu.SemaphoreType.DMA((2,))],
    )
    return pl.pallas_call(
        _expert_kernel, grid_spec=grid_spec, out_shape=jax.ShapeDtypeStruct(xs.shape, F32),
        input_output_aliases={8: 0},
        compiler_params=_params("arbitrary"), name="experts",
    )(first_blk, blk_count, w_idx, nused, blk_parts, w_gate, w_up, w_down, xs)


def _combine_kernel(dest_ref, dest_next_ref, x1_ref, rw_ref, g_ref, b_ref, ys_ref, o_ref, buf_ref, sem):
    tm, d = x1_ref.shape
    nch = _token_rows(d)
    step = pl.program_id(0)
    slot = step % 2

    def gather(idx_ref, which):
        def start(r, _):
            for k in range(2):
                _token_copy(ys_ref, _slot(idx_ref, r, k), buf_ref.at[which, k], r, nch,
                            sem.at[which]).start(priority=k)
            return 0
        lax.fori_loop(0, tm, start, 0, unroll=8)

    @pl.when(step == 0)
    def _():
        gather(dest_ref, 0)

    @pl.when(step + 1 < pl.num_programs(0))
    def _():
        gather(dest_next_ref, 1 - slot)

    def wait(r, _):
        for k in range(2):
            _token_copy(ys_ref, 0, buf_ref.at[slot, k], 0, nch, sem.at[slot]).wait()
        return 0

    lax.fori_loop(0, tm, wait, 0, unroll=8)
    rw = rw_ref[...]
    y0 = _from_token_tiles(buf_ref.at[slot, 0], tm, d)
    y1 = _from_token_tiles(buf_ref.at[slot, 1], tm, d)
    ffn = rw[:, 0:1] * y0 + rw[:, 1:2] * y1
    o_ref[...] = _layer_norm(DEEPNORM_ALPHA * x1_ref[...] + ffn, g_ref[...], b_ref[...])


def _combine(dest, x1, rw, ln_g, ln_b, ys):
    n, d = x1.shape
    tm = ROW_TILE
    nch = _token_rows(d)
    last = n // tm - 1
    row = lambda w: pl.BlockSpec((tm, w), lambda i: (i, 0))
    return pl.pallas_call(
        _combine_kernel, grid=(n // tm,),
        in_specs=[pl.BlockSpec((2 * tm,), lambda i: (i,), memory_space=pltpu.SMEM),
                  pl.BlockSpec((2 * tm,), lambda i: (jnp.minimum(i + 1, last),), memory_space=pltpu.SMEM),
                  row(d), row(LANES), _full(ln_g.shape), _full(ln_b.shape),
                  pl.BlockSpec(memory_space=pl.ANY)],
        out_specs=row(d),
        out_shape=jax.ShapeDtypeStruct((n, d), F32),
        scratch_shapes=[pltpu.VMEM((2, 2, tm * nch, LANES), F32), pltpu.SemaphoreType.DMA((2,))],
        compiler_params=_params("arbitrary"), name="combine",
    )(dest, dest, x1, rw, ln_g, ln_b, ys)


def _rope_tables(seq):
    half = ATTN_HEAD_DIM // 2
    inv_freq = ROPE_THETA ** (-np.arange(half, dtype=np.float64) / half)
    ang = np.arange(seq, dtype=np.float64)[:, None] * inv_freq[None, :]
    cos = np.cos(ang)
    sin = np.sin(ang)
    cos_h = np.concatenate([cos, cos], axis=1)
    sin_h = np.concatenate([-sin, sin], axis=1)
    return (jnp.asarray(np.tile(cos_h, (1, ATTN_HEADS)), F32), jnp.asarray(np.tile(sin_h, (1, ATTN_HEADS)), F32))


def _pad_lanes(a, width=LANES):
    return jnp.pad(a, ((0, 0), (0, width - a.shape[1])))


def kernel(x, ln0_g, ln0_b, w_in, conv_w, conv_b, w_mq, w_mk, b_i, b_f, gn_g, skip, w_attn_up, w_mlstm_up, w_out,
           ln1_g, ln1_b, w_router_group, b_router_group, w_router_expert, b_router_expert, w_gate, w_up, w_down,
           ln2_g, ln2_b):
    batch, seq, d = x.shape
    n = batch * seq
    assert seq % ROW_TILE == 0 and ROW_TILE == MOBA_BLOCK and w_in.shape[0] == DEPTH
    x2 = x.reshape(n, d)
    vec = lambda a: a.reshape(1, -1).astype(F32)

    wt = w_in[0].T
    cos, sin = _rope_tables(seq)

    b_if = jnp.concatenate([b_i[0], b_f[0]]).astype(F32)
    q, k, v, kmean, ga, gm, xn, ym = _front(
        x2, vec(ln0_g), vec(ln0_b), wt, cos, sin,
        conv_w[0], vec(conv_b[0]), w_mq[0].transpose(0, 2, 1).astype(BF16), w_mk[0].astype(BF16), b_if[:, None],
        gn_g[0].astype(F32)[:, None], skip[0].astype(F32)[:, None], batch, seq)

    nb = seq // MOBA_BLOCK
    km = kmean.reshape(batch, nb, ATTN_HEADS, ATTN_HEAD_DIM).transpose(0, 2, 1, 3)
    ya = _moba(q, k, v, km).reshape(n, ATTN_WIDTH)

    w_r = _pad_lanes(jnp.concatenate([w_router_expert[0], w_router_group[0]], axis=1))
    w_r_hi = w_r.astype(BF16)
    w_r_lo = (w_r - w_r_hi.astype(F32)).astype(BF16)
    w_rc = jnp.concatenate([w_r_hi.T, w_r_lo.T], axis=0)
    b_r = _pad_lanes(jnp.concatenate([b_router_expert[0], b_router_group[0]])[None, :]).T
    x1, ri, rw, counts = _mix(
        xn, ya, ym, ga, gm, w_attn_up[0].astype(BF16), w_mlstm_up[0].astype(BF16),
        w_out[0].astype(BF16), vec(ln1_g[0]), vec(ln1_b[0]), w_rc, b_r)

    tb = EXPERT_TILE
    nblk = (2 * n) // tb + MOE_EXPERTS
    cnt = counts[:MOE_EXPERTS, 0].astype(jnp.int32)
    nblk_e = (cnt + tb - 1) // tb
    blk_end = jnp.cumsum(nblk_e)
    pad_start = (blk_end - nblk_e) * tb
    nused = blk_end[-1:]
    ids = jnp.arange(MOE_EXPERTS, dtype=jnp.int32)
    prev_used = jnp.max(jnp.where((ids[None, :] <= ids[:, None]) & (nblk_e[None, :] > 0), ids[None, :], -1), axis=1)
    first_used = jnp.min(jnp.where(nblk_e > 0, ids, MOE_EXPERTS - 1))
    w_idx = jnp.where(prev_used >= 0, prev_used, first_used).astype(jnp.int32)
    last_blk = jnp.where(nblk_e > 0, (blk_end - 1) * tb, -1)
    last_blk = jnp.concatenate([last_blk, nused]).astype(jnp.int32)
    dest = _slots(ri, pad_start.astype(F32)[:, None])
    dest = dest[:2].reshape(2, n // ROW_TILE, ROW_TILE).transpose(1, 0, 2).reshape(2 * n)

    first_blk = blk_end - nblk_e
    blk_ids = jnp.arange(nblk, dtype=jnp.int32)[:, None]
    owner = (first_blk[None, :] <= blk_ids) & (blk_ids < blk_end[None, :])
    blk_tokens = jnp.sum(jnp.where(owner, jnp.clip(cnt[None, :] - (blk_ids - first_blk[None, :]) * tb, 0, tb), 0), axis=1)
    piece = tb // EXPERT_PARTS
    blk_parts = ((blk_tokens + piece - 1) // piece).astype(jnp.int32)

    xs = _dispatch(dest, last_blk, x1, nblk * tb)
    ys = _experts(first_blk.astype(jnp.int32), nblk_e.astype(jnp.int32), w_idx, nused.astype(jnp.int32), blk_parts,
                  xs, w_gate[0], w_up[0], w_down[0])
    out = _combine(dest, x1, rw, vec(ln2_g[0]), vec(ln2_b[0]), ys)
    return out.reshape(batch, seq, d)
```

```python
import math

import jax
import jax.numpy as jnp
import numpy as np
from jax import lax
from jax.experimental import pallas as pl
from jax.experimental.pallas import tpu as pltpu

F32 = jnp.float32
BF16 = jnp.bfloat16

ATTN_HEADS = 8
ATTN_HEAD_DIM = 64
ATTN_WIDTH = ATTN_HEADS * ATTN_HEAD_DIM
MOBA_BLOCK = 256
MOBA_TOPK = 3
ROPE_THETA = 10000.0
MLSTM_HEADS = 4
MLSTM_HEAD_DIM = 128
MLSTM_WIDTH = MLSTM_HEADS * MLSTM_HEAD_DIM
MLSTM_CONV = 4
MOE_GROUPS = 8
MOE_EXPERTS_PER_GROUP = 8
MOE_EXPERTS = MOE_GROUPS * MOE_EXPERTS_PER_GROUP
LN_EPS = 1e-5
GN_EPS = 1e-6
DEPTH = 1
DEEPNORM_ALPHA = (2 * DEPTH) ** 0.25

LANES = 128
SUBLANES = 8
ROW_TILE = 256
EXPERT_TILE = 128
EXPERT_IN_SLOTS = 6
EXPERT_PARTS = 4
SLOT_TILE = 2048
MIX_CHAINS = 4
VMEM_LIMIT = 48 * 1024 * 1024
INPROJ_VMEM_LIMIT = 58 * 1024 * 1024
WEIGHT_CHUNK = 512
LOG2_E = math.log2(math.e)

NEG_INF = float("-inf")


def _params(*sem):
    return pltpu.CompilerParams(dimension_semantics=sem, vmem_limit_bytes=VMEM_LIMIT)


def _dot(a, b):
    return jnp.dot(a, b, preferred_element_type=F32)


def _dot_nt(a, b):
    return lax.dot_general(a, b, (((1,), (1,)), ((), ())), preferred_element_type=F32)


def _split3(x):
    x1 = x.astype(BF16)
    r1 = x - x1.astype(F32)
    x2 = r1.astype(BF16)
    r2 = r1 - x2.astype(F32)
    return x1, x2, r2.astype(BF16)


def _layer_norm(x, g, b):
    mu = jnp.mean(x, axis=-1, keepdims=True)
    xc = x - mu
    var = jnp.mean(xc * xc, axis=-1, keepdims=True)
    return xc * lax.rsqrt(var + LN_EPS) * g + b


def _log_sigmoid(x):
    return jnp.minimum(x, 0.0) - jnp.log1p(jnp.exp(-jnp.abs(x)))


def _full(shape):
    nd = len(shape)
    return pl.BlockSpec(shape, lambda *_: (0,) * nd)


def _skewed(phases, chains, rows):
    states = [dict() for _ in range(chains)]

    def thunk(t, c):
        return lambda: phases[t - c](states[c], c, slice(c * rows, (c + 1) * rows))

    return [thunk(t, c) for t in range(chains + len(phases) - 1) for c in range(chains) if 0 <= t - c < len(phases)]


def _run_skewed(phases, chains, rows):
    for thunk in _skewed(phases, chains, rows):
        thunk()


def _interleave(a, b):
    ia = ib = 0
    while ia < len(a) or ib < len(b):
        if ib >= len(b) or (ia < len(a) and ia * len(b) <= ib * len(a)):
            a[ia]()
            ia += 1
        else:
            b[ib]()
            ib += 1


def _loop_groups(count, body, group=4):
    def trip(g, _):
        for d in range(group):
            body(g * group + d)
        return 0

    lax.fori_loop(0, count // group, trip, 0)
    done = (count // group) * group
    size = group // 2
    while size >= 1:
        take = ((count - done) // size) * size
        @pl.when(take > 0)
        def _(done=done, size=size):
            for d in range(size):
                body(done + d)
        done = done + take
        size //= 2


def _front_kernel(x_ref, g_ref, b_ref, wt_ref, cos_ref, sin_ref,
                  cw_ref, cb_ref, wqt_ref, wk_ref, bcol_ref, gn_ref, skip_ref,
                  q_ref, k_ref, v_ref, km_ref, ga_ref, gm_ref, xn_ref, y_ref,
                  wch_ref, wift_ref,
                  u_s, vmt_s, ot_s, ift_s, ext_ref, c_ref, n_ref, m_ref, yt_ref, *, chunks_per_seq):
    tm = ROW_TILE
    hd = MLSTM_HEAD_DIM
    halo = SUBLANES
    step = pl.program_id(0)
    wr = step % 2
    rd = 1 - wr

    @pl.when(step == 0)
    def _():
        c_if = 3 * ATTN_WIDTH + 3 * MLSTM_WIDTH
        for c in range(wch_ref.shape[0]):
            row0 = c * WEIGHT_CHUNK + (2 * MLSTM_HEADS if c * WEIGHT_CHUNK >= c_if else 0)
            wch_ref[c] = wt_ref[row0:row0 + WEIGHT_CHUNK, :].T.astype(BF16)
        wift_ref[...] = wt_ref[c_if:c_if + 2 * MLSTM_HEADS, :].astype(BF16)
        u_s[...] = jnp.zeros_like(u_s)
        vmt_s[...] = jnp.zeros_like(vmt_s)
        ot_s[...] = jnp.zeros_like(ot_s)
        ift_s[...] = jnp.zeros_like(ift_s)

    @pl.when(lax.rem(jnp.maximum(step - 1, 0), chunks_per_seq) == 0)
    def _():
        ext_ref[0:halo, :] = jnp.zeros((halo, MLSTM_WIDTH), F32)
        c_ref[...] = jnp.zeros_like(c_ref)
        n_ref[...] = jnp.zeros_like(n_ref)
        m_ref[...] = jnp.zeros_like(m_ref)

    lane = lax.broadcasted_iota(jnp.int32, (tm, ATTN_WIDTH), 1)
    first_half = (lane % ATTN_HEAD_DIM) < (ATTN_HEAD_DIM // 2)
    ps = {}

    def norm():
        xn = _layer_norm(x_ref[...], g_ref[...], b_ref[...])
        xn_ref[...] = xn
        ps["xb"] = xn.astype(BF16)

    def piece(key, chunk):
        def run():
            ps[key] = _dot(ps["xb"], wch_ref[chunk])
        return run

    def attn_outputs():
        cos = cos_ref[...]
        sin = sin_ref[...]

        def rope(t):
            fwd = pltpu.roll(t, ATTN_WIDTH - ATTN_HEAD_DIM // 2, axis=1)
            bwd = pltpu.roll(t, ATTN_HEAD_DIM // 2, axis=1)
            return t * cos + jnp.where(first_half, fwd, bwd) * sin

        q = rope(ps.pop("zq")) * (ATTN_HEAD_DIM ** -0.5 * LOG2_E)
        k = rope(ps.pop("zk"))
        km_ref[0] = jnp.mean(k, axis=0, keepdims=True)
        qt = q.T
        vt = ps.pop("zv").T
        for h in range(ATTN_HEADS):
            sl = slice(h * ATTN_HEAD_DIM, (h + 1) * ATTN_HEAD_DIM)
            q_ref[0, h] = qt[sl, :].astype(BF16)
            k_ref[0, h] = k[:, sl].astype(BF16)
            v_ref[0, h] = vt[sl, :].astype(BF16)

    def stage():
        u_s[wr] = ps.pop("zu")
        vmt_s[wr] = ps.pop("zvm").T.astype(BF16)
        ot_s[wr] = ps.pop("zo").T
        ift_s[wr] = _dot_nt(wift_ref[...], ps["xb"])

    d = ga_ref.shape[1]
    gw = d // 2

    def gate_out(key, ref, lo):
        def run():
            ref[:, lo:lo + gw] = jax.nn.sigmoid(ps.pop(key)).astype(BF16)
        return run

    assert ATTN_WIDTH == MLSTM_WIDTH == gw == WEIGHT_CHUNK
    project = [
        norm,
        piece("zq", 0), piece("zk", 1), piece("zv", 2),
        attn_outputs,
        piece("zu", 3), piece("zvm", 4), piece("zo", 5),
        stage,
        piece("g0", 6), piece("g1", 7), gate_out("g0", ga_ref, 0),
        piece("g2", 8), gate_out("g1", ga_ref, gw),
        piece("g3", 9), gate_out("g2", gm_ref, 0), gate_out("g3", gm_ref, gw),
    ]

    ms = {}
    rows = lax.broadcasted_iota(jnp.int32, (tm, tm), 0)
    cols = lax.broadcasted_iota(jnp.int32, (tm, tm), 1)
    causal_t = rows <= cols

    def prologue():
        u = u_s[rd]
        ext_ref[halo:halo + tm, :] = u
        acc = jnp.broadcast_to(cb_ref[...], u.shape)
        for j in range(MLSTM_CONV):
            acc = acc + cw_ref[j:j + 1, :] * ext_ref[halo - (MLSTM_CONV - 1) + j:halo - (MLSTM_CONV - 1) + j + tm, :]
        ext_ref[0:halo, :] = u[tm - halo:, :]
        uc = acc * jax.nn.sigmoid(acc)
        gr = ift_s[rd] + bcol_ref[...]
        triu = jnp.where(causal_t, 1.0, 0.0).astype(BF16)
        r1, r2, r3 = _split3(_log_sigmoid(gr))
        bcum_r = _dot(r1, triu) + _dot(r2, triu) + _dot(r3, triu)
        key_rows = gr[:MLSTM_HEADS, :] - bcum_r[MLSTM_HEADS:, :]
        ms["key_cols"] = jnp.concatenate([key_rows, jnp.zeros((LANES - MLSTM_HEADS, tm), F32)], axis=0).T
        ms["key_rows"], ms["bcum_r"], ms["uc"], ms["uct"] = key_rows, bcum_r, uc, uc.T

    def decay_weights(st, h, hs):
        b_row = ms["bcum_r"][MLSTM_HEADS + h:MLSTM_HEADS + h + 1, :]
        st["key_row"] = ms["key_rows"][h:h + 1, :]
        st["key_col"] = ms["key_cols"][:, h:h + 1]
        m_prev = m_ref[h][:, 0:1]
        dlog = jnp.where(causal_t, st["key_col"] + b_row, NEG_INF)
        inter = b_row + m_prev
        m_t = jnp.maximum(inter, jnp.max(dlog, axis=0, keepdims=True))
        st["w_intra"] = jnp.exp(dlog - m_t)
        st["w_inter"] = jnp.exp(inter - m_t)
        st["m_t"], st["m_prev"], st["b_end"] = m_t, m_prev, b_row[:, tm - 1:tm]

    def project_qk(st, h, hs):
        st["qtb"] = _dot(wqt_ref[h], ms["uct"][hs, :].astype(BF16)).astype(BF16)
        st["k"] = _dot(ms["uc"][:, hs].astype(BF16), wk_ref[h]) * (hd ** -0.5)

    def scores(st, h, hs):
        st["s"] = _dot(st["k"].astype(BF16), st["qtb"]) * st.pop("w_intra")

    def readout(st, h, hs):
        qtb, s, w_inter, m_t = st.pop("qtb"), st.pop("s"), st.pop("w_inter"), st.pop("m_t")
        n_prev = n_ref[h]
        n_hi = n_prev.astype(BF16)
        n_lo = (n_prev - n_hi.astype(F32)).astype(BF16)
        qn = (_dot(n_hi, qtb) + _dot(n_lo, qtb))[0:1, :]
        num = w_inter * _dot(c_ref[h].astype(BF16), qtb) + _dot(vmt_s[rd, hs, :], s.astype(BF16))
        den = w_inter * qn + jnp.sum(s, axis=0, keepdims=True)
        st["hh"] = num / jnp.maximum(jnp.abs(den), jnp.exp(-m_t))

    def update_state(st, h, hs):
        b_end, m_prev = st.pop("b_end"), st.pop("m_prev")
        m_new = jnp.maximum(b_end + m_prev, jnp.max(b_end + st.pop("key_row"), axis=1, keepdims=True))
        decay = jnp.exp(b_end + m_prev - m_new)
        kw = st.pop("k") * jnp.exp(b_end + st.pop("key_col") - m_new)
        n_prev = n_ref[h]
        c_ref[h] = decay * c_ref[h] + _dot(vmt_s[rd, hs, :], kw.astype(BF16))
        n_ref[h] = decay * n_prev + jnp.broadcast_to(jnp.sum(kw, axis=0, keepdims=True), n_prev.shape)
        m_ref[h] = jnp.broadcast_to(m_new, (1, LANES))

    def gate_and_norm(st, h, hs):
        hh = jax.nn.sigmoid(ot_s[rd, hs, :]) * st.pop("hh")
        mu = jnp.mean(hh, axis=0, keepdims=True)
        hc = hh - mu
        var = jnp.mean(hc * hc, axis=0, keepdims=True)
        yt_ref[hs, :] = hc * lax.rsqrt(var + GN_EPS) * gn_ref[hs, :] + skip_ref[hs, :] * ms["uct"][hs, :]

    def emit_y():
        y_ref[...] = yt_ref[...].T.astype(BF16)

    heads = _skewed((decay_weights, project_qk, scores, readout, update_state, gate_and_norm), MLSTM_HEADS, hd)
    _interleave(project, [prologue] + heads + [emit_y])


def _front(x2, ln_g, ln_b, wt, cos, sin, conv_w, conv_b, wqt, wk, bcol, gn_g, skip, batch, seq):
    n, d = x2.shape
    tm = ROW_TILE
    nsb = seq // tm
    nchunks = n // tm
    hd = ATTN_HEAD_DIM
    cur = lambda s: jnp.minimum(s, nchunks - 1)
    prev = lambda s: jnp.maximum(s - 1, 0)
    row = lambda w: pl.BlockSpec((tm, w), lambda s: (cur(s), 0))
    head = pl.BlockSpec((1, ATTN_HEADS, tm, hd), lambda s: (cur(s) // nsb, 0, cur(s) % nsb, 0))
    head_t = pl.BlockSpec((1, ATTN_HEADS, hd, tm), lambda s: (cur(s) // nsb, 0, 0, cur(s) % nsb))
    tab = pl.BlockSpec((tm, ATTN_WIDTH), lambda s: (cur(s) % nsb, 0))
    head_shape = jax.ShapeDtypeStruct((batch, ATTN_HEADS, seq, hd), BF16)
    head_t_shape = jax.ShapeDtypeStruct((batch, ATTN_HEADS, hd, seq), BF16)
    out_shape = (
        head_t_shape, head_shape, head_t_shape,
        jax.ShapeDtypeStruct((nchunks, 1, ATTN_WIDTH), F32),
        jax.ShapeDtypeStruct((n, d), BF16),
        jax.ShapeDtypeStruct((n, d), BF16),
        jax.ShapeDtypeStruct((n, d), F32),
        jax.ShapeDtypeStruct((n, MLSTM_WIDTH), BF16),
    )
    out_specs = (
        head_t, head, head_t,
        pl.BlockSpec((1, 1, ATTN_WIDTH), lambda s: (cur(s), 0, 0)),
        row(d), row(d), row(d),
        pl.BlockSpec((tm, MLSTM_WIDTH), lambda s: (prev(s), 0)),
    )
    mconsts = (conv_w, conv_b, wqt, wk, bcol, gn_g, skip)
    wt_spec = pl.BlockSpec(wt.shape, lambda s: (0, 0), pipeline_mode=pl.Buffered(1))
    in_specs = ([row(d), _full(ln_g.shape), _full(ln_b.shape), wt_spec, tab, tab] + [_full(a.shape) for a in mconsts])
    return pl.pallas_call(
        lambda *refs: _front_kernel(*refs, chunks_per_seq=nsb),
        grid=(nchunks + 1,), in_specs=in_specs, out_specs=out_specs, out_shape=out_shape,
        scratch_shapes=[pltpu.VMEM(((3 * ATTN_WIDTH + 3 * MLSTM_WIDTH + 2 * d) // WEIGHT_CHUNK, d, WEIGHT_CHUNK), BF16),
                        pltpu.VMEM((2 * MLSTM_HEADS, d), BF16),
                        pltpu.VMEM((2, tm, MLSTM_WIDTH), F32), pltpu.VMEM((2, MLSTM_WIDTH, tm), BF16),
                        pltpu.VMEM((2, MLSTM_WIDTH, tm), F32), pltpu.VMEM((2, SUBLANES, tm), F32),
                        pltpu.VMEM((SUBLANES + tm, MLSTM_WIDTH), F32),
                        pltpu.VMEM((MLSTM_HEADS, MLSTM_HEAD_DIM, MLSTM_HEAD_DIM), F32),
                        pltpu.VMEM((MLSTM_HEADS, SUBLANES, MLSTM_HEAD_DIM), F32),
                        pltpu.VMEM((MLSTM_HEADS, 1, LANES), F32),
                        pltpu.VMEM((MLSTM_WIDTH, tm), F32)],
        compiler_params=pltpu.CompilerParams(dimension_semantics=("arbitrary",), vmem_limit_bytes=INPROJ_VMEM_LIMIT),
        name="front",
    )(x2, ln_g, ln_b, wt, cos, sin, *mconsts)


def _moba_kernel(qt_ref, k_ref, vt_ref, km_ref, o_ref, bias_ref, m_ref, l_ref, acc_ref, s_ref):
    i = pl.program_id(1)
    blk = MOBA_BLOCK
    hd = ATTN_HEAD_DIM
    heads = ATTN_HEADS
    nb = k_ref.shape[2] // blk
    blk_id = lax.broadcasted_iota(jnp.int32, (nb, blk), 0)
    key_pos = lax.broadcasted_iota(jnp.int32, (blk, blk), 0)
    qry_pos = lax.broadcasted_iota(jnp.int32, (blk, blk), 1)
    causal = key_pos <= qry_pos

    for h in range(heads):
        qt = qt_ref[0, h]
        km = km_ref[0, h]
        km_hi = km.astype(BF16)
        km_lo = (km - km_hi.astype(F32)).astype(BF16)
        gate = _dot(km_hi, qt) + _dot(km_lo, qt)
        gate = jnp.where(blk_id < i, gate, NEG_INF)
        for j in range(nb - 1):
            row = gate[j:j + 1, :]
            beats = (gate > row) | ((gate == row) & (blk_id < j))
            cnt = jnp.sum(jnp.where(beats, 1.0, 0.0), axis=0, keepdims=True)
            sel = (cnt < float(MOBA_TOPK)) & (row > NEG_INF)
            bias_ref[j * heads + h] = jnp.where(sel, 0.0, NEG_INF)
    for h in range(heads):
        bias_ref[i * heads + h] = jnp.zeros((1, blk), F32)

    def scores(h, j, own_block):
        qt = qt_ref[0, h]
        half = blk // 2
        m_tile = None
        for c in range(2):
            rows = slice(c * half, (c + 1) * half)
            s = _dot(k_ref[0, h, pl.ds(pl.multiple_of(j * blk + c * half, half), half), :], qt)
            if own_block:
                s = jnp.where(causal[rows], s, NEG_INF)
            s_ref[j * heads + h, rows, :] = s
            m_c = jnp.max(s, axis=0, keepdims=True)
            m_tile = m_c if m_tile is None else jnp.maximum(m_tile, m_c)
        return m_tile

    for h in range(heads):
        m_ref[h] = scores(h, i, True)

    def past_scores(j):
        for h in range(heads):
            m_ref[h] = jnp.maximum(m_ref[h], scores(h, j, False) + bias_ref[j * heads + h])

    _loop_groups(i, past_scores)

    l_ref[...] = jnp.zeros_like(l_ref)
    acc_ref[...] = jnp.zeros_like(acc_ref)

    def accumulate(j):
        off = pl.multiple_of(j * blk, blk)
        for h in range(heads):
            p = jnp.exp2(s_ref[j * heads + h] - (m_ref[h] - bias_ref[j * heads + h]))
            l_ref[h] += jnp.sum(p, axis=0, keepdims=True)
            acc_ref[h] += _dot(vt_ref[0, h, :, pl.ds(off, blk)], p.astype(BF16))

    _loop_groups(i + 1, accumulate)
    yt = acc_ref[...] / l_ref[...]
    o_ref[0] = yt.reshape(heads * hd, blk).T.astype(BF16)


def _moba(qt, k, vt, km):
    batch, heads, seq, hd = k.shape
    blk = MOBA_BLOCK
    nb = seq // blk
    return pl.pallas_call(
        _moba_kernel, grid=(batch, nb),
        in_specs=[
            pl.BlockSpec((1, heads, hd, blk), lambda b, i: (b, 0, 0, i)),
            pl.BlockSpec((1, heads, seq, hd), lambda b, i: (b, 0, 0, 0)),
            pl.BlockSpec((1, heads, hd, seq), lambda b, i: (b, 0, 0, 0)),
            pl.BlockSpec((1, heads, nb, hd), lambda b, i: (b, 0, 0, 0)),
        ],
        out_specs=pl.BlockSpec((1, blk, heads * hd), lambda b, i: (b, i, 0)),
        out_shape=jax.ShapeDtypeStruct((batch, seq, heads * hd), BF16),
        scratch_shapes=[pltpu.VMEM((nb * heads, 1, blk), F32), pltpu.VMEM((heads, 1, blk), F32),
                        pltpu.VMEM((heads, 1, blk), F32), pltpu.VMEM((heads, hd, blk), F32),
                        pltpu.VMEM((nb * heads, blk, blk), F32)],
        compiler_params=_params("parallel", "arbitrary"), name="moba",
    )(qt, k, vt, km)


def _mix_kernel(xn_ref, ya_ref, ym_ref, ga_ref, gm_ref, wau_ref, wmu_ref, wout_ref,
                g1_ref, b1_ref, wrc_ref, br_ref,
                x1_ref, ri_ref, rw_ref, cnt_out_ref, cnt_ref):
    @pl.when(pl.program_id(0) == 0)
    def _():
        cnt_ref[...] = jnp.zeros_like(cnt_ref)

    tm = ROW_TILE
    sub = lax.broadcasted_iota(jnp.int32, (LANES, tm), 0).astype(F32)
    big = float(4 * LANES)

    def up_and_mix(st, c, rs):
        a_up = _dot(ya_ref[rs, :], wau_ref[...])
        m_up = _dot(ym_ref[rs, :], wmu_ref[...])
        mix = ga_ref[rs, :].astype(F32) * a_up + gm_ref[rs, :].astype(F32) * m_up
        st["mix"] = mix.astype(BF16)

    def out_and_norm(st, c, rs):
        x1 = _layer_norm(DEEPNORM_ALPHA * xn_ref[rs, :] + _dot(st.pop("mix"), wout_ref[...]), g1_ref[...], b1_ref[...])
        x1_ref[rs, :] = x1
        st["x1"] = x1

    def router_logits(st, c, rs):
        x1 = st.pop("x1")
        x_hi = x1.astype(BF16)
        x_lo = (x1 - x_hi.astype(F32)).astype(BF16)
        both = _dot_nt(wrc_ref[...], x_hi)
        st["logits"] = both[:LANES] + both[LANES:] + _dot_nt(wrc_ref[:LANES, :], x_lo) + br_ref[...]

    def route(st, c, rs):
        logits = st.pop("logits")
        is_g = (sub >= float(MOE_EXPERTS)) & (sub < float(MOE_EXPERTS + MOE_GROUPS))
        gl = jnp.where(is_g, logits, NEG_INF)
        ge = jnp.exp(gl - jnp.max(gl, axis=0, keepdims=True))
        gp = ge / jnp.sum(ge, axis=0, keepdims=True)
        g_w = jnp.max(gp, axis=0, keepdims=True)
        g_idx = jnp.min(jnp.where((gp == g_w) & is_g, sub - float(MOE_EXPERTS), big), axis=0, keepdims=True)
        lo = g_idx * float(MOE_EXPERTS_PER_GROUP)
        in_grp = (sub >= lo) & (sub < lo + float(MOE_EXPERTS_PER_GROUP))
        el = jnp.where(in_grp, logits, NEG_INF)
        v1 = jnp.max(el, axis=0, keepdims=True)
        i1 = jnp.min(jnp.where((el == v1) & in_grp, sub, big), axis=0, keepdims=True)
        el2 = jnp.where(sub == i1, NEG_INF, el)
        v2 = jnp.max(el2, axis=0, keepdims=True)
        i2 = jnp.min(jnp.where((el2 == v2) & in_grp & (sub != i1), sub, big), axis=0, keepdims=True)
        e2 = jnp.exp(v2 - v1)
        w0 = g_w / (1.0 + e2)
        w1 = g_w * e2 / (1.0 + e2)
        rw_ref[rs, :] = jnp.where(sub == 0.0, w0, jnp.where(sub == 1.0, w1, 0.0)).T
        st["i1"], st["i2"] = i1, i2

    def rank(st, c, rs):
        i1, i2 = st.pop("i1"), st.pop("i2")
        is1 = sub == i1
        is2 = sub == i2
        onehot = jnp.where(is1 | is2, 1.0, 0.0)
        rows = lax.broadcasted_iota(jnp.int32, (tm, tm), 0)
        cols = lax.broadcasted_iota(jnp.int32, (tm, tm), 1)
        earlier = jnp.where(rows < cols, 1.0, 0.0).astype(BF16)
        before = _dot(onehot.astype(BF16), earlier) + cnt_ref[...]
        r0 = jnp.sum(jnp.where(is1, before, 0.0), axis=0, keepdims=True)
        r1 = jnp.sum(jnp.where(is2, before, 0.0), axis=0, keepdims=True)
        total = cnt_ref[...] + jnp.sum(onehot, axis=1, keepdims=True)
        cnt_ref[...] = total
        cnt_out_ref[...] = total
        ri_t = jnp.where(sub == 0.0, i1, jnp.where(sub == 1.0, i2, jnp.where(sub == 2.0, r0, jnp.where(sub == 3.0, r1, 0.0))))
        ri_ref[:, rs] = ri_t[:SUBLANES, :].astype(jnp.int32)

    _run_skewed((up_and_mix, out_and_norm, router_logits, route, rank), xn_ref.shape[0] // tm, tm)


def _mix(xn, ya, ym, ga, gm, wau, wmu, wout, g1, b1, wrc, br):
    n, d = xn.shape
    tm = MIX_CHAINS * ROW_TILE
    row = lambda w: pl.BlockSpec((tm, w), lambda i: (i, 0))
    in_specs = [row(d), row(ATTN_WIDTH), row(MLSTM_WIDTH), row(d), row(d),
                _full(wau.shape), _full(wmu.shape), _full(wout.shape), _full(g1.shape), _full(b1.shape),
                _full(wrc.shape), _full(br.shape)]
    out_shape = (jax.ShapeDtypeStruct((n, d), F32), jax.ShapeDtypeStruct((SUBLANES, n), jnp.int32),
                 jax.ShapeDtypeStruct((n, LANES), F32), jax.ShapeDtypeStruct((LANES, 1), F32))
    out_specs = (row(d), pl.BlockSpec((SUBLANES, tm), lambda i: (0, i)), row(LANES), _full((LANES, 1)))
    return pl.pallas_call(
        _mix_kernel, grid=(n // tm,), in_specs=in_specs, out_specs=out_specs, out_shape=out_shape,
        scratch_shapes=[pltpu.VMEM((LANES, 1), F32)],
        compiler_params=_params("arbitrary"), name="mix",
    )(xn, ya, ym, ga, gm, wau, wmu, wout, g1, b1, wrc, br)


def _token_rows(d):
    return d // LANES


def _to_token_tiles(dst_ref, x):
    rows, d = x.shape
    nch = _token_rows(d)
    for c in range(nch):
        dst_ref[pl.ds(c, rows, stride=nch), :] = x[:, c * LANES:(c + 1) * LANES]


def _from_token_tiles(src_ref, rows, d):
    nch = _token_rows(d)
    return jnp.concatenate([src_ref[pl.ds(c, rows, stride=nch), :] for c in range(nch)], axis=1)


def _token_copy(src, src_tok, dst, dst_tok, nch, sem):
    s0 = pl.multiple_of(src_tok * nch, nch)
    d0 = pl.multiple_of(dst_tok * nch, nch)
    return pltpu.make_async_copy(src.at[pl.ds(s0, nch), :], dst.at[pl.ds(d0, nch), :], sem)


def _slots_kernel(ri_ref, ps_ref, o_ref):
    ri = ri_ref[...].astype(F32)
    ps = ps_ref[...]
    expert = lax.broadcasted_iota(jnp.int32, (ps.shape[0], ri.shape[1]), 0).astype(F32)
    row_id = lax.broadcasted_iota(jnp.int32, ri.shape, 0)
    out = jnp.zeros(ri.shape, F32)
    for k in range(2):
        start = jnp.sum(jnp.where(expert == ri[k:k + 1, :], jnp.broadcast_to(ps, expert.shape), 0.0),
                        axis=0, keepdims=True)
        out = jnp.where(row_id == k, start + ri[2 + k:3 + k, :], out)
    o_ref[...] = out.astype(jnp.int32)


def _slots(ri, pad_start_col):
    n = ri.shape[1]
    tm = SLOT_TILE
    blk = pl.BlockSpec((SUBLANES, tm), lambda i: (0, i))
    return pl.pallas_call(
        _slots_kernel, grid=(n // tm,), in_specs=[blk, _full(pad_start_col.shape)], out_specs=blk,
        out_shape=jax.ShapeDtypeStruct((SUBLANES, n), jnp.int32),
        compiler_params=_params("parallel"), name="slots",
    )(ri, pad_start_col)


def _slot(dest_ref, r, k):
    return dest_ref[k * ROW_TILE + r]


def _dispatch_kernel(dest_ref, last_ref, x_ref, xs_ref, scr_ref, zero_ref, sem, zsem):
    tm, d = x_ref.shape
    nch = _token_rows(d)
    tb = zero_ref.shape[0] // nch

    @pl.when(pl.program_id(0) == 0)
    def _():
        zero_ref[...] = jnp.zeros_like(zero_ref)

        def desc(tok):
            off = pl.multiple_of(jnp.maximum(tok, 0) * nch, nch)
            return pltpu.make_async_copy(zero_ref, xs_ref.at[pl.ds(off, tb * nch), :], zsem)

        def zstart(e, _):
            @pl.when(last_ref[e] >= 0)
            def _():
                desc(last_ref[e]).start()
            return 0

        def zwait(e, _):
            @pl.when(last_ref[e] >= 0)
            def _():
                desc(last_ref[e]).wait()
            return 0

        lax.fori_loop(0, MOE_EXPERTS, zstart, 0)
        nused = last_ref[MOE_EXPERTS]
        nblk = xs_ref.shape[0] // (tb * nch)
        lax.fori_loop(nused, nblk, lambda b, _: (desc(b * tb).start(), 0)[1], 0)
        lax.fori_loop(0, MOE_EXPERTS, zwait, 0)
        lax.fori_loop(nused, nblk, lambda b, _: (desc(b * tb).wait(), 0)[1], 0)

    step = pl.program_id(0)
    slot = step % 2
    scr = scr_ref.at[slot]
    _to_token_tiles(scr, x_ref[...])

    def start(r, _):
        for k in range(2):
            _token_copy(scr, r, xs_ref, _slot(dest_ref, r, k), nch, sem.at[slot]).start(priority=k)
        return 0

    def drain(which):
        def wait(r, _):
            for k in range(2):
                _token_copy(scr_ref.at[which], 0, xs_ref, 0, nch, sem.at[which]).wait()
            return 0
        lax.fori_loop(0, tm, wait, 0, unroll=8)

    lax.fori_loop(0, tm, start, 0, unroll=8)

    @pl.when(step > 0)
    def _():
        drain(1 - slot)

    @pl.when(step == pl.num_programs(0) - 1)
    def _():
        drain(slot)


def _dispatch(dest, last_blk, x1, n_rows):
    n, d = x1.shape
    tm = ROW_TILE
    nch = _token_rows(d)
    return pl.pallas_call(
        _dispatch_kernel, grid=(n // tm,),
        in_specs=[pl.BlockSpec((2 * tm,), lambda i: (i,), memory_space=pltpu.SMEM),
                  pl.BlockSpec(memory_space=pltpu.SMEM),
                  pl.BlockSpec((tm, d), lambda i: (i, 0))],
        out_specs=pl.BlockSpec(memory_space=pl.ANY),
        out_shape=jax.ShapeDtypeStruct((n_rows * nch, LANES), F32),
        scratch_shapes=[pltpu.VMEM((2, tm * nch, LANES), F32), pltpu.VMEM((EXPERT_TILE * nch, LANES), F32),
                        pltpu.SemaphoreType.DMA((2,)), pltpu.SemaphoreType.DMA(())],
        compiler_params=_params("arbitrary"), name="dispatch",
    )(dest, last_blk, x1)


def _expert_kernel(first_ref, count_ref, widx_ref, nused_ref, parts_ref, wg_ref, wu_ref, wd_ref, xs_ref, ys_ref,
                   wgb_ref, wub_ref, wdb_ref, xbuf_ref, ybuf_ref, in_sem, out_sem):
    del widx_ref
    e = pl.program_id(0)
    nused = nused_ref[0]
    d = wg_ref.shape[1]
    nch = _token_rows(d)
    rows = xbuf_ref.shape[1]
    tb = rows // nch
    prow = rows // EXPERT_PARTS

    def in_part(b, slot, p):
        src = xs_ref.at[pl.ds(pl.multiple_of(b * rows + p * prow, prow), prow), :]
        return pltpu.make_async_copy(src, xbuf_ref.at[slot, pl.ds(p * prow, prow), :], in_sem.at[slot])

    def out_part(b, slot, p):
        dst = ys_ref.at[pl.ds(pl.multiple_of(b * rows + p * prow, prow), prow), :]
        return pltpu.make_async_copy(ybuf_ref.at[slot, pl.ds(p * prow, prow), :], dst, out_sem.at[slot])

    def for_parts(b, fn):
        used = parts_ref[b]
        for p in range(EXPERT_PARTS):
            @pl.when(p < used)
            def _(p=p):
                fn(p)

    class _Block:
        def __init__(self, part, b, slot):
            self.part, self.b, self.slot = part, b, slot

        def start(self):
            for_parts(self.b, lambda p: self.part(self.b, self.slot, p).start())

        def wait(self):
            for_parts(self.b, lambda p: self.part(self.b, self.slot, p).wait())

    def in_copy(b, slot):
        return _Block(in_part, b, slot)

    def out_copy(b, slot):
        return _Block(out_part, b, slot)

    n_in = xbuf_ref.shape[0]

    @pl.when(e == 0)
    def _():
        xbuf_ref[...] = jnp.zeros_like(xbuf_ref)
        for b0 in range(n_in - 1):
            @pl.when(b0 < nused)
            def _():
                in_copy(b0, b0).start()

    @pl.when(count_ref[e] > 0)
    def _():
        wgb_ref[...] = wg_ref[0].astype(BF16)
        wub_ref[...] = wu_ref[0].astype(BF16)
        wdb_ref[...] = wd_ref[0].astype(BF16)

    def body(b, _):
        slot = b % n_in
        oslot = b % 2
        in_copy(b, slot).wait()

        @pl.when(b + n_in - 1 < nused)
        def _():
            in_copy(b + n_in - 1, (b + n_in - 1) % n_in).start()

        @pl.when(b >= 2)
        def _():
            out_copy(b - 2, oslot).wait()

        xb = _from_token_tiles(xbuf_ref.at[slot], tb, d).astype(BF16)
        g = _dot(xb, wgb_ref[...])
        u = _dot(xb, wub_ref[...])
        hmid = g * jax.nn.sigmoid(g) * u
        _to_token_tiles(ybuf_ref.at[oslot], _dot(hmid.astype(BF16), wdb_ref[...]))
        out_copy(b, oslot).start()
        return 0

    lax.fori_loop(first_ref[e], first_ref[e] + count_ref[e], body, 0)

    @pl.when(e == pl.num_programs(0) - 1)
    def _():
        for back in (2, 1):
            @pl.when(nused >= back)
            def _():
                out_copy(nused - back, (nused - back) % 2).wait()


def _experts(first_blk, blk_count, w_idx, nused, blk_parts, xs, w_gate, w_up, w_down):
    n_exp, d, dff = w_gate.shape
    nch = _token_rows(d)
    rows = EXPERT_TILE * nch
    w_spec = lambda shape: pl.BlockSpec(shape, lambda e, fb, bc, wi, nu, bp: (wi[e], 0, 0))
    any_spec = pl.BlockSpec(memory_space=pl.ANY)
    grid_spec = pltpu.PrefetchScalarGridSpec(
        num_scalar_prefetch=5, grid=(n_exp,),
        in_specs=[w_spec((1, d, dff)), w_spec((1, d, dff)), w_spec((1, dff, d)), any_spec],
        out_specs=any_spec,
        scratch_shapes=[pltpu.VMEM((d, dff), BF16), pltpu.VMEM((d, dff), BF16), pltpu.VMEM((dff, d), BF16),
                        pltpu.VMEM((EXPERT_IN_SLOTS, rows, LANES), F32), pltpu.VMEM((2, rows, LANES), F32),
                        pltpu.SemaphoreType.DMA((EXPERT_IN_SLOTS,)), pltpu.SemaphoreType.DMA((2,))],
    )
    return pl.pallas_call(
        _expert_kernel, grid_spec=grid_spec, out_shape=jax.ShapeDtypeStruct(xs.shape, F32),
        input_output_aliases={8: 0},
        compiler_params=_params("arbitrary"), name="experts",
    )(first_blk, blk_count, w_idx, nused, blk_parts, w_gate, w_up, w_down, xs)


def _combine_kernel(dest_ref, dest_next_ref, x1_ref, rw_ref, g_ref, b_ref, ys_ref, o_ref, buf_ref, sem):
    tm, d = x1_ref.shape
    nch = _token_rows(d)
    step = pl.program_id(0)
    slot = step % 2

    def gather(idx_ref, which):
        def start(r, _):
            for k in range(2):
                _token_copy(ys_ref, _slot(idx_ref, r, k), buf_ref.at[which, k], r, nch,
                            sem.at[which]).start(priority=k)
            return 0
        lax.fori_loop(0, tm, start, 0, unroll=8)

    @pl.when(step == 0)
    def _():
        gather(dest_ref, 0)

    @pl.when(step + 1 < pl.num_programs(0))
    def _():
        gather(dest_next_ref, 1 - slot)

    def wait(r, _):
        for k in range(2):
            _token_copy(ys_ref, 0, buf_ref.at[slot, k], 0, nch, sem.at[slot]).wait()
        return 0

    lax.fori_loop(0, tm, wait, 0, unroll=8)
    rw = rw_ref[...]
    y0 = _from_token_tiles(buf_ref.at[slot, 0], tm, d)
    y1 = _from_token_tiles(buf_ref.at[slot, 1], tm, d)
    ffn = rw[:, 0:1] * y0 + rw[:, 1:2] * y1
    o_ref[...] = _layer_norm(DEEPNORM_ALPHA * x1_ref[...] + ffn, g_ref[...], b_ref[...])


def _combine(dest, x1, rw, ln_g, ln_b, ys):
    n, d = x1.shape
    tm = ROW_TILE
    nch = _token_rows(d)
    last = n // tm - 1
    row = lambda w: pl.BlockSpec((tm, w), lambda i: (i, 0))
    return pl.pallas_call(
        _combine_kernel, grid=(n // tm,),
        in_specs=[pl.BlockSpec((2 * tm,), lambda i: (i,), memory_space=pltpu.SMEM),
                  pl.BlockSpec((2 * tm,), lambda i: (jnp.minimum(i + 1, last),), memory_space=pltpu.SMEM),
                  row(d), row(LANES), _full(ln_g.shape), _full(ln_b.shape),
                  pl.BlockSpec(memory_space=pl.ANY)],
        out_specs=row(d),
        out_shape=jax.ShapeDtypeStruct((n, d), F32),
        scratch_shapes=[pltpu.VMEM((2, 2, tm * nch, LANES), F32), pltpu.SemaphoreType.DMA((2,))],
        compiler_params=_params("arbitrary"), name="combine",
    )(dest, dest, x1, rw, ln_g, ln_b, ys)


def _rope_tables(seq):
    half = ATTN_HEAD_DIM // 2
    inv_freq = ROPE_THETA ** (-np.arange(half, dtype=np.float64) / half)
    ang = np.arange(seq, dtype=np.float64)[:, None] * inv_freq[None, :]
    cos = np.cos(ang)
    sin = np.sin(ang)
    cos_h = np.concatenate([cos, cos], axis=1)
    sin_h = np.concatenate([-sin, sin], axis=1)
    return (jnp.asarray(np.tile(cos_h, (1, ATTN_HEADS)), F32), jnp.asarray(np.tile(sin_h, (1, ATTN_HEADS)), F32))


def _pad_lanes(a, width=LANES):
    return jnp.pad(a, ((0, 0), (0, width - a.shape[1])))


def kernel(x, ln0_g, ln0_b, w_in, conv_w, conv_b, w_mq, w_mk, b_i, b_f, gn_g, skip, w_attn_up, w_mlstm_up, w_out,
           ln1_g, ln1_b, w_router_group, b_router_group, w_router_expert, b_router_expert, w_gate, w_up, w_down,
           ln2_g, ln2_b):
    batch, seq, d = x.shape
    n = batch * seq
    assert seq % ROW_TILE == 0 and ROW_TILE == MOBA_BLOCK and w_in.shape[0] == DEPTH
    x2 = x.reshape(n, d)
    vec = lambda a: a.reshape(1, -1).astype(F32)

    wt = w_in[0].T
    cos, sin = _rope_tables(seq)

    b_if = jnp.concatenate([b_i[0], b_f[0]]).astype(F32)
    q, k, v, kmean, ga, gm, xn, ym = _front(
        x2, vec(ln0_g), vec(ln0_b), wt, cos, sin,
        conv_w[0], vec(conv_b[0]), w_mq[0].transpose(0, 2, 1).astype(BF16), w_mk[0].astype(BF16), b_if[:, None],
        gn_g[0].astype(F32)[:, None], skip[0].astype(F32)[:, None], batch, seq)

    nb = seq // MOBA_BLOCK
    km = kmean.reshape(batch, nb, ATTN_HEADS, ATTN_HEAD_DIM).transpose(0, 2, 1, 3)
    ya = _moba(q, k, v, km).reshape(n, ATTN_WIDTH)

    w_r = _pad_lanes(jnp.concatenate([w_router_expert[0], w_router_group[0]], axis=1))
    w_r_hi = w_r.astype(BF16)
    w_r_lo = (w_r - w_r_hi.astype(F32)).astype(BF16)
    w_rc = jnp.concatenate([w_r_hi.T, w_r_lo.T], axis=0)
    b_r = _pad_lanes(jnp.concatenate([b_router_expert[0], b_router_group[0]])[None, :]).T
    x1, ri, rw, counts = _mix(
        xn, ya, ym, ga, gm, w_attn_up[0].astype(BF16), w_mlstm_up[0].astype(BF16),
        w_out[0].astype(BF16), vec(ln1_g[0]), vec(ln1_b[0]), w_rc, b_r)

    tb = EXPERT_TILE
    nblk = (2 * n) // tb + MOE_EXPERTS
    cnt = counts[:MOE_EXPERTS, 0].astype(jnp.int32)
    nblk_e = (cnt + tb - 1) // tb
    blk_end = jnp.cumsum(nblk_e)
    pad_start = (blk_end - nblk_e) * tb
    nused = blk_end[-1:]
    ids = jnp.arange(MOE_EXPERTS, dtype=jnp.int32)
    prev_used = jnp.max(jnp.where((ids[None, :] <= ids[:, None]) & (nblk_e[None, :] > 0), ids[None, :], -1), axis=1)
    first_used = jnp.min(jnp.where(nblk_e > 0, ids, MOE_EXPERTS - 1))
    w_idx = jnp.where(prev_used >= 0, prev_used, first_used).astype(jnp.int32)
    last_blk = jnp.where(nblk_e > 0, (blk_end - 1) * tb, -1)
    last_blk = jnp.concatenate([last_blk, nused]).astype(jnp.int32)
    dest = _slots(ri, pad_start.astype(F32)[:, None])
    dest = dest[:2].reshape(2, n // ROW_TILE, ROW_TILE).transpose(1, 0, 2).reshape(2 * n)

    first_blk = blk_end - nblk_e
    blk_ids = jnp.arange(nblk, dtype=jnp.int32)[:, None]
    owner = (first_blk[None, :] <= blk_ids) & (blk_ids < blk_end[None, :])
    blk_tokens = jnp.sum(jnp.where(owner, jnp.clip(cnt[None, :] - (blk_ids - first_blk[None, :]) * tb, 0, tb), 0), axis=1)
    piece = tb // EXPERT_PARTS
    blk_parts = ((blk_tokens + piece - 1) // piece).astype(jnp.int32)

    xs = _dispatch(dest, last_blk, x1, nblk * tb)
    ys = _experts(first_blk.astype(jnp.int32), nblk_e.astype(jnp.int32), w_idx, nused.astype(jnp.int32), blk_parts,
                  xs, w_gate[0], w_up[0], w_down[0])
    out = _combine(dest, x1, rw, vec(ln2_g[0]), vec(ln2_b[0]), ys)
    return out.reshape(batch, seq, d)
```

```python
import math

import jax
import jax.numpy as jnp
import numpy as np
from jax import lax
from jax.experimental import pallas as pl
from jax.experimental.pallas import tpu as pltpu

F32 = jnp.float32
BF16 = jnp.bfloat16

ATTN_HEADS = 8
ATTN_HEAD_DIM = 64
ATTN_WIDTH = ATTN_HEADS * ATTN_HEAD_DIM
MOBA_BLOCK = 256
MOBA_TOPK = 3
ROPE_THETA = 10000.0
MLSTM_HEADS = 4
MLSTM_HEAD_DIM = 128
MLSTM_WIDTH = MLSTM_HEADS * MLSTM_HEAD_DIM
MLSTM_CONV = 4
MOE_GROUPS = 8
MOE_EXPERTS_PER_GROUP = 8
MOE_EXPERTS = MOE_GROUPS * MOE_EXPERTS_PER_GROUP
LN_EPS = 1e-5
GN_EPS = 1e-6
DEPTH = 1
DEEPNORM_ALPHA = (2 * DEPTH) ** 0.25

LANES = 128
SUBLANES = 8
ROW_TILE = 256
EXPERT_TILE = 256
EXPERT_IN_SLOTS = 4
EXPERT_PARTS = 4
SLOT_TILE = 2048
MIX_CHAINS = 4
VMEM_LIMIT = 48 * 1024 * 1024
INPROJ_VMEM_LIMIT = 58 * 1024 * 1024
WEIGHT_CHUNK = 512
LOG2_E = math.log2(math.e)

NEG_INF = float("-inf")


def _params(*sem):
    return pltpu.CompilerParams(dimension_semantics=sem, vmem_limit_bytes=VMEM_LIMIT)


def _dot(a, b):
    return jnp.dot(a, b, preferred_element_type=F32)


def _dot_nt(a, b):
    return lax.dot_general(a, b, (((1,), (1,)), ((), ())), preferred_element_type=F32)


def _split3(x):
    x1 = x.astype(BF16)
    r1 = x - x1.astype(F32)
    x2 = r1.astype(BF16)
    r2 = r1 - x2.astype(F32)
    return x1, x2, r2.astype(BF16)


def _layer_norm(x, g, b):
    mu = jnp.mean(x, axis=-1, keepdims=True)
    xc = x - mu
    var = jnp.mean(xc * xc, axis=-1, keepdims=True)
    return xc * lax.rsqrt(var + LN_EPS) * g + b


def _log_sigmoid(x):
    return jnp.minimum(x, 0.0) - jnp.log1p(jnp.exp(-jnp.abs(x)))


def _full(shape):
    nd = len(shape)
    return pl.BlockSpec(shape, lambda *_: (0,) * nd)


def _skewed(phases, chains, rows):
    states = [dict() for _ in range(chains)]

    def thunk(t, c):
        return lambda: phases[t - c](states[c], c, slice(c * rows, (c + 1) * rows))

    return [thunk(t, c) for t in range(chains + len(phases) - 1) for c in range(chains) if 0 <= t - c < len(phases)]


def _run_skewed(phases, chains, rows):
    for thunk in _skewed(phases, chains, rows):
        thunk()


def _interleave(a, b):
    ia = ib = 0
    while ia < len(a) or ib < len(b):
        if ib >= len(b) or (ia < len(a) and ia * len(b) <= ib * len(a)):
            a[ia]()
            ia += 1
        else:
            b[ib]()
            ib += 1


def _loop_groups(count, body, group=4):
    def trip(g, _):
        for d in range(group):
            body(g * group + d)
        return 0

    lax.fori_loop(0, count // group, trip, 0)
    done = (count // group) * group
    size = group // 2
    while size >= 1:
        take = ((count - done) // size) * size
        @pl.when(take > 0)
        def _(done=done, size=size):
            for d in range(size):
                body(done + d)
        done = done + take
        size //= 2


def _front_kernel(x_ref, g_ref, b_ref, wt_ref, cos_ref, sin_ref,
                  cw_ref, cb_ref, wqt_ref, wk_ref, bcol_ref, gn_ref, skip_ref,
                  q_ref, k_ref, v_ref, km_ref, ga_ref, gm_ref, xn_ref, y_ref,
                  wch_ref, wift_ref,
                  u_s, vmt_s, ot_s, ift_s, ext_ref, c_ref, n_ref, m_ref, yt_ref, *, chunks_per_seq):
    tm = ROW_TILE
    hd = MLSTM_HEAD_DIM
    halo = SUBLANES
    step = pl.program_id(0)
    wr = step % 2
    rd = 1 - wr

    @pl.when(step == 0)
    def _():
        c_if = 3 * ATTN_WIDTH + 3 * MLSTM_WIDTH
        for c in range(wch_ref.shape[0]):
            row0 = c * WEIGHT_CHUNK + (2 * MLSTM_HEADS if c * WEIGHT_CHUNK >= c_if else 0)
            wch_ref[c] = wt_ref[row0:row0 + WEIGHT_CHUNK, :].T.astype(BF16)
        wift_ref[...] = wt_ref[c_if:c_if + 2 * MLSTM_HEADS, :].astype(BF16)
        u_s[...] = jnp.zeros_like(u_s)
        vmt_s[...] = jnp.zeros_like(vmt_s)
        ot_s[...] = jnp.zeros_like(ot_s)
        ift_s[...] = jnp.zeros_like(ift_s)

    @pl.when(lax.rem(jnp.maximum(step - 1, 0), chunks_per_seq) == 0)
    def _():
        ext_ref[0:halo, :] = jnp.zeros((halo, MLSTM_WIDTH), F32)
        c_ref[...] = jnp.zeros_like(c_ref)
        n_ref[...] = jnp.zeros_like(n_ref)
        m_ref[...] = jnp.zeros_like(m_ref)

    lane = lax.broadcasted_iota(jnp.int32, (tm, ATTN_WIDTH), 1)
    first_half = (lane % ATTN_HEAD_DIM) < (ATTN_HEAD_DIM // 2)
    ps = {}

    def norm():
        xn = _layer_norm(x_ref[...], g_ref[...], b_ref[...])
        xn_ref[...] = xn
        ps["xb"] = xn.astype(BF16)

    def piece(key, chunk):
        def run():
            ps[key] = _dot(ps["xb"], wch_ref[chunk])
        return run

    def attn_outputs():
        cos = cos_ref[...]
        sin = sin_ref[...]

        def rope(t):
            fwd = pltpu.roll(t, ATTN_WIDTH - ATTN_HEAD_DIM // 2, axis=1)
            bwd = pltpu.roll(t, ATTN_HEAD_DIM // 2, axis=1)
            return t * cos + jnp.where(first_half, fwd, bwd) * sin

        q = rope(ps.pop("zq")) * (ATTN_HEAD_DIM ** -0.5 * LOG2_E)
        k = rope(ps.pop("zk"))
        km_ref[0] = jnp.mean(k, axis=0, keepdims=True)
        qt = q.T
        vt = ps.pop("zv").T
        for h in range(ATTN_HEADS):
            sl = slice(h * ATTN_HEAD_DIM, (h + 1) * ATTN_HEAD_DIM)
            q_ref[0, h] = qt[sl, :].astype(BF16)
            k_ref[0, h] = k[:, sl].astype(BF16)
            v_ref[0, h] = vt[sl, :].astype(BF16)

    def stage():
        u_s[wr] = ps.pop("zu")
        vmt_s[wr] = ps.pop("zvm").T.astype(BF16)
        ot_s[wr] = ps.pop("zo").T
        ift_s[wr] = _dot_nt(wift_ref[...], ps["xb"])

    d = ga_ref.shape[1]
    gw = d // 2

    def gate_out(key, ref, lo):
        def run():
            ref[:, lo:lo + gw] = jax.nn.sigmoid(ps.pop(key)).astype(BF16)
        return run

    assert ATTN_WIDTH == MLSTM_WIDTH == gw == WEIGHT_CHUNK
    project = [
        norm,
        piece("zq", 0), piece("zk", 1), piece("zv", 2),
        attn_outputs,
        piece("zu", 3), piece("zvm", 4), piece("zo", 5),
        stage,
        piece("g0", 6), piece("g1", 7), gate_out("g0", ga_ref, 0),
        piece("g2", 8), gate_out("g1", ga_ref, gw),
        piece("g3", 9), gate_out("g2", gm_ref, 0), gate_out("g3", gm_ref, gw),
    ]

    ms = {}
    rows = lax.broadcasted_iota(jnp.int32, (tm, tm), 0)
    cols = lax.broadcasted_iota(jnp.int32, (tm, tm), 1)
    causal_t = rows <= cols

    def prologue():
        u = u_s[rd]
        ext_ref[halo:halo + tm, :] = u
        acc = jnp.broadcast_to(cb_ref[...], u.shape)
        for j in range(MLSTM_CONV):
            acc = acc + cw_ref[j:j + 1, :] * ext_ref[halo - (MLSTM_CONV - 1) + j:halo - (MLSTM_CONV - 1) + j + tm, :]
        ext_ref[0:halo, :] = u[tm - halo:, :]
        uc = acc * jax.nn.sigmoid(acc)
        gr = ift_s[rd] + bcol_ref[...]
        triu = jnp.where(causal_t, 1.0, 0.0).astype(BF16)
        r1, r2, r3 = _split3(_log_sigmoid(gr))
        bcum_r = _dot(r1, triu) + _dot(r2, triu) + _dot(r3, triu)
        key_rows = gr[:MLSTM_HEADS, :] - bcum_r[MLSTM_HEADS:, :]
        ms["key_cols"] = jnp.concatenate([key_rows, jnp.zeros((LANES - MLSTM_HEADS, tm), F32)], axis=0).T
        ms["key_rows"], ms["bcum_r"], ms["uc"], ms["uct"] = key_rows, bcum_r, uc, uc.T

    def decay_weights(st, h, hs):
        b_row = ms["bcum_r"][MLSTM_HEADS + h:MLSTM_HEADS + h + 1, :]
        st["key_row"] = ms["key_rows"][h:h + 1, :]
        st["key_col"] = ms["key_cols"][:, h:h + 1]
        m_prev = m_ref[h][:, 0:1]
        dlog = jnp.where(causal_t, st["key_col"] + b_row, NEG_INF)
        inter = b_row + m_prev
        m_t = jnp.maximum(inter, jnp.max(dlog, axis=0, keepdims=True))
        st["w_intra"] = jnp.exp(dlog - m_t)
        st["w_inter"] = jnp.exp(inter - m_t)
        st["m_t"], st["m_prev"], st["b_end"] = m_t, m_prev, b_row[:, tm - 1:tm]

    def project_qk(st, h, hs):
        st["qtb"] = _dot(wqt_ref[h], ms["uct"][hs, :].astype(BF16)).astype(BF16)
        st["k"] = _dot(ms["uc"][:, hs].astype(BF16), wk_ref[h]) * (hd ** -0.5)

    def scores(st, h, hs):
        st["s"] = _dot(st["k"].astype(BF16), st["qtb"]) * st.pop("w_intra")

    def readout(st, h, hs):
        qtb, s, w_inter, m_t = st.pop("qtb"), st.pop("s"), st.pop("w_inter"), st.pop("m_t")
        n_prev = n_ref[h]
        n_hi = n_prev.astype(BF16)
        n_lo = (n_prev - n_hi.astype(F32)).astype(BF16)
        qn = (_dot(n_hi, qtb) + _dot(n_lo, qtb))[0:1, :]
        num = w_inter * _dot(c_ref[h].astype(BF16), qtb) + _dot(vmt_s[rd, hs, :], s.astype(BF16))
        den = w_inter * qn + jnp.sum(s, axis=0, keepdims=True)
        st["hh"] = num / jnp.maximum(jnp.abs(den), jnp.exp(-m_t))

    def update_state(st, h, hs):
        b_end, m_prev = st.pop("b_end"), st.pop("m_prev")
        m_new = jnp.maximum(b_end + m_prev, jnp.max(b_end + st.pop("key_row"), axis=1, keepdims=True))
        decay = jnp.exp(b_end + m_prev - m_new)
        kw = st.pop("k") * jnp.exp(b_end + st.pop("key_col") - m_new)
        n_prev = n_ref[h]
        c_ref[h] = decay * c_ref[h] + _dot(vmt_s[rd, hs, :], kw.astype(BF16))
        n_ref[h] = decay * n_prev + jnp.broadcast_to(jnp.sum(kw, axis=0, keepdims=True), n_prev.shape)
        m_ref[h] = jnp.broadcast_to(m_new, (1, LANES))

    def gate_and_norm(st, h, hs):
        hh = jax.nn.sigmoid(ot_s[rd, hs, :]) * st.pop("hh")
        mu = jnp.mean(hh, axis=0, keepdims=True)
        hc = hh - mu
        var = jnp.mean(hc * hc, axis=0, keepdims=True)
        yt_ref[hs, :] = hc * lax.rsqrt(var + GN_EPS) * gn_ref[hs, :] + skip_ref[hs, :] * ms["uct"][hs, :]

    def emit_y():
        y_ref[...] = yt_ref[...].T.astype(BF16)

    heads = _skewed((decay_weights, project_qk, scores, readout, update_state, gate_and_norm), MLSTM_HEADS, hd)
    _interleave(project, [prologue] + heads + [emit_y])


def _front(x2, ln_g, ln_b, wt, cos, sin, conv_w, conv_b, wqt, wk, bcol, gn_g, skip, batch, seq):
    n, d = x2.shape
    tm = ROW_TILE
    nsb = seq // tm
    nchunks = n // tm
    hd = ATTN_HEAD_DIM
    cur = lambda s: jnp.minimum(s, nchunks - 1)
    prev = lambda s: jnp.maximum(s - 1, 0)
    row = lambda w: pl.BlockSpec((tm, w), lambda s: (cur(s), 0))
    head = pl.BlockSpec((1, ATTN_HEADS, tm, hd), lambda s: (cur(s) // nsb, 0, cur(s) % nsb, 0))
    head_t = pl.BlockSpec((1, ATTN_HEADS, hd, tm), lambda s: (cur(s) // nsb, 0, 0, cur(s) % nsb))
    tab = pl.BlockSpec((tm, ATTN_WIDTH), lambda s: (cur(s) % nsb, 0))
    head_shape = jax.ShapeDtypeStruct((batch, ATTN_HEADS, seq, hd), BF16)
    head_t_shape = jax.ShapeDtypeStruct((batch, ATTN_HEADS, hd, seq), BF16)
    out_shape = (
        head_t_shape, head_shape, head_t_shape,
        jax.ShapeDtypeStruct((nchunks, 1, ATTN_WIDTH), F32),
        jax.ShapeDtypeStruct((n, d), BF16),
        jax.ShapeDtypeStruct((n, d), BF16),
        jax.ShapeDtypeStruct((n, d), F32),
        jax.ShapeDtypeStruct((n, MLSTM_WIDTH), BF16),
    )
    out_specs = (
        head_t, head, head_t,
        pl.BlockSpec((1, 1, ATTN_WIDTH), lambda s: (cur(s), 0, 0)),
        row(d), row(d), row(d),
        pl.BlockSpec((tm, MLSTM_WIDTH), lambda s: (prev(s), 0)),
    )
    mconsts = (conv_w, conv_b, wqt, wk, bcol, gn_g, skip)
    wt_spec = pl.BlockSpec(wt.shape, lambda s: (0, 0), pipeline_mode=pl.Buffered(1))
    in_specs = ([row(d), _full(ln_g.shape), _full(ln_b.shape), wt_spec, tab, tab] + [_full(a.shape) for a in mconsts])
    return pl.pallas_call(
        lambda *refs: _front_kernel(*refs, chunks_per_seq=nsb),
        grid=(nchunks + 1,), in_specs=in_specs, out_specs=out_specs, out_shape=out_shape,
        scratch_shapes=[pltpu.VMEM(((3 * ATTN_WIDTH + 3 * MLSTM_WIDTH + 2 * d) // WEIGHT_CHUNK, d, WEIGHT_CHUNK), BF16),
                        pltpu.VMEM((2 * MLSTM_HEADS, d), BF16),
                        pltpu.VMEM((2, tm, MLSTM_WIDTH), F32), pltpu.VMEM((2, MLSTM_WIDTH, tm), BF16),
                        pltpu.VMEM((2, MLSTM_WIDTH, tm), F32), pltpu.VMEM((2, SUBLANES, tm), F32),
                        pltpu.VMEM((SUBLANES + tm, MLSTM_WIDTH), F32),
                        pltpu.VMEM((MLSTM_HEADS, MLSTM_HEAD_DIM, MLSTM_HEAD_DIM), F32),
                        pltpu.VMEM((MLSTM_HEADS, SUBLANES, MLSTM_HEAD_DIM), F32),
                        pltpu.VMEM((MLSTM_HEADS, 1, LANES), F32),
                        pltpu.VMEM((MLSTM_WIDTH, tm), F32)],
        compiler_params=pltpu.CompilerParams(dimension_semantics=("arbitrary",), vmem_limit_bytes=INPROJ_VMEM_LIMIT),
        name="front",
    )(x2, ln_g, ln_b, wt, cos, sin, *mconsts)


def _moba_kernel(qt_ref, k_ref, vt_ref, km_ref, o_ref, bias_ref, m_ref, l_ref, acc_ref, s_ref):
    i = pl.program_id(1)
    blk = MOBA_BLOCK
    hd = ATTN_HEAD_DIM
    heads = ATTN_HEADS
    nb = k_ref.shape[2] // blk
    blk_id = lax.broadcasted_iota(jnp.int32, (nb, blk), 0)
    key_pos = lax.broadcasted_iota(jnp.int32, (blk, blk), 0)
    qry_pos = lax.broadcasted_iota(jnp.int32, (blk, blk), 1)
    causal = key_pos <= qry_pos

    for h in range(heads):
        qt = qt_ref[0, h]
        km = km_ref[0, h]
        km_hi = km.astype(BF16)
        km_lo = (km - km_hi.astype(F32)).astype(BF16)
        gate = _dot(km_hi, qt) + _dot(km_lo, qt)
        gate = jnp.where(blk_id < i, gate, NEG_INF)
        for j in range(nb - 1):
            row = gate[j:j + 1, :]
            beats = (gate > row) | ((gate == row) & (blk_id < j))
            cnt = jnp.sum(jnp.where(beats, 1.0, 0.0), axis=0, keepdims=True)
            sel = (cnt < float(MOBA_TOPK)) & (row > NEG_INF)
            bias_ref[j * heads + h] = jnp.where(sel, 0.0, NEG_INF)
    for h in range(heads):
        bias_ref[i * heads + h] = jnp.zeros((1, blk), F32)

    def scores(h, j, own_block):
        qt = qt_ref[0, h]
        half = blk // 2
        m_tile = None
        for c in range(2):
            rows = slice(c * half, (c + 1) * half)
            s = _dot(k_ref[0, h, pl.ds(pl.multiple_of(j * blk + c * half, half), half), :], qt)
            if own_block:
                s = jnp.where(causal[rows], s, NEG_INF)
            s_ref[j * heads + h, rows, :] = s
            m_c = jnp.max(s, axis=0, keepdims=True)
            m_tile = m_c if m_tile is None else jnp.maximum(m_tile, m_c)
        return m_tile

    for h in range(heads):
        m_ref[h] = scores(h, i, True)

    def past_scores(j):
        for h in range(heads):
            m_ref[h] = jnp.maximum(m_ref[h], scores(h, j, False) + bias_ref[j * heads + h])

    _loop_groups(i, past_scores)

    l_ref[...] = jnp.zeros_like(l_ref)
    acc_ref[...] = jnp.zeros_like(acc_ref)

    def accumulate(j):
        off = pl.multiple_of(j * blk, blk)
        for h in range(heads):
            p = jnp.exp2(s_ref[j * heads + h] - (m_ref[h] - bias_ref[j * heads + h]))
            l_ref[h] += jnp.sum(p, axis=0, keepdims=True)
            acc_ref[h] += _dot(vt_ref[0, h, :, pl.ds(off, blk)], p.astype(BF16))

    _loop_groups(i + 1, accumulate)
    yt = acc_ref[...] / l_ref[...]
    o_ref[0] = yt.reshape(heads * hd, blk).T.astype(BF16)


def _moba(qt, k, vt, km):
    batch, heads, seq, hd = k.shape
    blk = MOBA_BLOCK
    nb = seq // blk
    return pl.pallas_call(
        _moba_kernel, grid=(batch, nb),
        in_specs=[
            pl.BlockSpec((1, heads, hd, blk), lambda b, i: (b, 0, 0, i)),
            pl.BlockSpec((1, heads, seq, hd), lambda b, i: (b, 0, 0, 0)),
            pl.BlockSpec((1, heads, hd, seq), lambda b, i: (b, 0, 0, 0)),
            pl.BlockSpec((1, heads, nb, hd), lambda b, i: (b, 0, 0, 0)),
        ],
        out_specs=pl.BlockSpec((1, blk, heads * hd), lambda b, i: (b, i, 0)),
        out_shape=jax.ShapeDtypeStruct((batch, seq, heads * hd), BF16),
        scratch_shapes=[pltpu.VMEM((nb * heads, 1, blk), F32), pltpu.VMEM((heads, 1, blk), F32),
                        pltpu.VMEM((heads, 1, blk), F32), pltpu.VMEM((heads, hd, blk), F32),
                        pltpu.VMEM((nb * heads, blk, blk), F32)],
        compiler_params=_params("parallel", "arbitrary"), name="moba",
    )(qt, k, vt, km)


def _mix_kernel(xn_ref, ya_ref, ym_ref, ga_ref, gm_ref, wau_ref, wmu_ref, wout_ref,
                g1_ref, b1_ref, wrc_ref, br_ref,
                x1_ref, ri_ref, rw_ref, cnt_out_ref, cnt_ref):
    @pl.when(pl.program_id(0) == 0)
    def _():
        cnt_ref[...] = jnp.zeros_like(cnt_ref)

    tm = ROW_TILE
    sub = lax.broadcasted_iota(jnp.int32, (LANES, tm), 0).astype(F32)
    big = float(4 * LANES)

    def up_and_mix(st, c, rs):
        a_up = _dot(ya_ref[rs, :], wau_ref[...])
        m_up = _dot(ym_ref[rs, :], wmu_ref[...])
        mix = ga_ref[rs, :].astype(F32) * a_up + gm_ref[rs, :].astype(F32) * m_up
        st["mix"] = mix.astype(BF16)

    def out_and_norm(st, c, rs):
        x1 = _layer_norm(DEEPNORM_ALPHA * xn_ref[rs, :] + _dot(st.pop("mix"), wout_ref[...]), g1_ref[...], b1_ref[...])
        x1_ref[rs, :] = x1
        st["x1"] = x1

    def router_logits(st, c, rs):
        x1 = st.pop("x1")
        x_hi = x1.astype(BF16)
        x_lo = (x1 - x_hi.astype(F32)).astype(BF16)
        both = _dot_nt(wrc_ref[...], x_hi)
        st["logits"] = both[:LANES] + both[LANES:] + _dot_nt(wrc_ref[:LANES, :], x_lo) + br_ref[...]

    def route(st, c, rs):
        logits = st.pop("logits")
        is_g = (sub >= float(MOE_EXPERTS)) & (sub < float(MOE_EXPERTS + MOE_GROUPS))
        gl = jnp.where(is_g, logits, NEG_INF)
        ge = jnp.exp(gl - jnp.max(gl, axis=0, keepdims=True))
        gp = ge / jnp.sum(ge, axis=0, keepdims=True)
        g_w = jnp.max(gp, axis=0, keepdims=True)
        g_idx = jnp.min(jnp.where((gp == g_w) & is_g, sub - float(MOE_EXPERTS), big), axis=0, keepdims=True)
        lo = g_idx * float(MOE_EXPERTS_PER_GROUP)
        in_grp = (sub >= lo) & (sub < lo + float(MOE_EXPERTS_PER_GROUP))
        el = jnp.where(in_grp, logits, NEG_INF)
        v1 = jnp.max(el, axis=0, keepdims=True)
        i1 = jnp.min(jnp.where((el == v1) & in_grp, sub, big), axis=0, keepdims=True)
        el2 = jnp.where(sub == i1, NEG_INF, el)
        v2 = jnp.max(el2, axis=0, keepdims=True)
        i2 = jnp.min(jnp.where((el2 == v2) & in_grp & (sub != i1), sub, big), axis=0, keepdims=True)
        e2 = jnp.exp(v2 - v1)
        w0 = g_w / (1.0 + e2)
        w1 = g_w * e2 / (1.0 + e2)
        rw_ref[rs, :] = jnp.where(sub == 0.0, w0, jnp.where(sub == 1.0, w1, 0.0)).T
        st["i1"], st["i2"] = i1, i2

    def rank(st, c, rs):
        i1, i2 = st.pop("i1"), st.pop("i2")
        is1 = sub == i1
        is2 = sub == i2
        onehot = jnp.where(is1 | is2, 1.0, 0.0)
        rows = lax.broadcasted_iota(jnp.int32, (tm, tm), 0)
        cols = lax.broadcasted_iota(jnp.int32, (tm, tm), 1)
        earlier = jnp.where(rows < cols, 1.0, 0.0).astype(BF16)
        before = _dot(onehot.astype(BF16), earlier) + cnt_ref[...]
        r0 = jnp.sum(jnp.where(is1, before, 0.0), axis=0, keepdims=True)
        r1 = jnp.sum(jnp.where(is2, before, 0.0), axis=0, keepdims=True)
        total = cnt_ref[...] + jnp.sum(onehot, axis=1, keepdims=True)
        cnt_ref[...] = total
        cnt_out_ref[...] = total
        ri_t = jnp.where(sub == 0.0, i1, jnp.where(sub == 1.0, i2, jnp.where(sub == 2.0, r0, jnp.where(sub == 3.0, r1, 0.0))))
        ri_ref[:, rs] = ri_t[:SUBLANES, :].astype(jnp.int32)

    _run_skewed((up_and_mix, out_and_norm, router_logits, route, rank), xn_ref.shape[0] // tm, tm)


def _mix(xn, ya, ym, ga, gm, wau, wmu, wout, g1, b1, wrc, br):
    n, d = xn.shape
    tm = MIX_CHAINS * ROW_TILE
    row = lambda w: pl.BlockSpec((tm, w), lambda i: (i, 0))
    in_specs = [row(d), row(ATTN_WIDTH), row(MLSTM_WIDTH), row(d), row(d),
                _full(wau.shape), _full(wmu.shape), _full(wout.shape), _full(g1.shape), _full(b1.shape),
                _full(wrc.shape), _full(br.shape)]
    out_shape = (jax.ShapeDtypeStruct((n, d), F32), jax.ShapeDtypeStruct((SUBLANES, n), jnp.int32),
                 jax.ShapeDtypeStruct((n, LANES), F32), jax.ShapeDtypeStruct((LANES, 1), F32))
    out_specs = (row(d), pl.BlockSpec((SUBLANES, tm), lambda i: (0, i)), row(LANES), _full((LANES, 1)))
    return pl.pallas_call(
        _mix_kernel, grid=(n // tm,), in_specs=in_specs, out_specs=out_specs, out_shape=out_shape,
        scratch_shapes=[pltpu.VMEM((LANES, 1), F32)],
        compiler_params=_params("arbitrary"), name="mix",
    )(xn, ya, ym, ga, gm, wau, wmu, wout, g1, b1, wrc, br)


def _token_rows(d):
    return d // LANES


def _to_token_tiles(dst_ref, x):
    rows, d = x.shape
    nch = _token_rows(d)
    for c in range(nch):
        dst_ref[pl.ds(c, rows, stride=nch), :] = x[:, c * LANES:(c + 1) * LANES]


def _from_token_tiles(src_ref, rows, d):
    nch = _token_rows(d)
    return jnp.concatenate([src_ref[pl.ds(c, rows, stride=nch), :] for c in range(nch)], axis=1)


def _token_copy(src, src_tok, dst, dst_tok, nch, sem):
    s0 = pl.multiple_of(src_tok * nch, nch)
    d0 = pl.multiple_of(dst_tok * nch, nch)
    return pltpu.make_async_copy(src.at[pl.ds(s0, nch), :], dst.at[pl.ds(d0, nch), :], sem)


def _slots_kernel(ri_ref, ps_ref, o_ref):
    ri = ri_ref[...].astype(F32)
    ps = ps_ref[...]
    expert = lax.broadcasted_iota(jnp.int32, (ps.shape[0], ri.shape[1]), 0).astype(F32)
    row_id = lax.broadcasted_iota(jnp.int32, ri.shape, 0)
    out = jnp.zeros(ri.shape, F32)
    for k in range(2):
        start = jnp.sum(jnp.where(expert == ri[k:k + 1, :], jnp.broadcast_to(ps, expert.shape), 0.0),
                        axis=0, keepdims=True)
        out = jnp.where(row_id == k, start + ri[2 + k:3 + k, :], out)
    o_ref[...] = out.astype(jnp.int32)


def _slots(ri, pad_start_col):
    n = ri.shape[1]
    tm = SLOT_TILE
    blk = pl.BlockSpec((SUBLANES, tm), lambda i: (0, i))
    return pl.pallas_call(
        _slots_kernel, grid=(n // tm,), in_specs=[blk, _full(pad_start_col.shape)], out_specs=blk,
        out_shape=jax.ShapeDtypeStruct((SUBLANES, n), jnp.int32),
        compiler_params=_params("parallel"), name="slots",
    )(ri, pad_start_col)


def _slot(dest_ref, r, k):
    return dest_ref[k * ROW_TILE + r]


def _dispatch_kernel(dest_ref, last_ref, x_ref, xs_ref, scr_ref, zero_ref, sem, zsem, tsem):
    tm, d = x_ref.shape
    nch = _token_rows(d)
    tb = zero_ref.shape[0] // nch
    nused = last_ref[MOE_EXPERTS]
    nblk = xs_ref.shape[0] // (tb * nch)

    def zero_block(tok, zero_sem):
        off = pl.multiple_of(jnp.maximum(tok, 0) * nch, nch)
        return pltpu.make_async_copy(zero_ref, xs_ref.at[pl.ds(off, tb * nch), :], zero_sem)

    @pl.when(pl.program_id(0) == 0)
    def _():
        zero_ref[...] = jnp.zeros_like(zero_ref)

        def zstart(e, _):
            @pl.when(last_ref[e] >= 0)
            def _():
                zero_block(last_ref[e], zsem).start()
            return 0

        def zwait(e, _):
            @pl.when(last_ref[e] >= 0)
            def _():
                zero_block(last_ref[e], zsem).wait()
            return 0

        lax.fori_loop(0, MOE_EXPERTS, zstart, 0)
        lax.fori_loop(nused, nblk, lambda b, _: (zero_block(b * tb, tsem).start(), 0)[1], 0)
        lax.fori_loop(0, MOE_EXPERTS, zwait, 0)

    step = pl.program_id(0)
    slot = step % 2
    scr = scr_ref.at[slot]
    _to_token_tiles(scr, x_ref[...])

    def start(r, _):
        for k in range(2):
            _token_copy(scr, r, xs_ref, _slot(dest_ref, r, k), nch, sem.at[slot]).start(priority=k)
        return 0

    def drain(which):
        def wait(r, _):
            for k in range(2):
                _token_copy(scr_ref.at[which], 0, xs_ref, 0, nch, sem.at[which]).wait()
            return 0
        lax.fori_loop(0, tm, wait, 0, unroll=8)

    lax.fori_loop(0, tm, start, 0, unroll=8)

    @pl.when(step > 0)
    def _():
        drain(1 - slot)

    @pl.when(step == pl.num_programs(0) - 1)
    def _():
        drain(slot)
        lax.fori_loop(nused, nblk, lambda b, _: (zero_block(b * tb, tsem).wait(), 0)[1], 0)


def _dispatch(dest, last_blk, x1, n_rows):
    n, d = x1.shape
    tm = ROW_TILE
    nch = _token_rows(d)
    return pl.pallas_call(
        _dispatch_kernel, grid=(n // tm,),
        in_specs=[pl.BlockSpec((2 * tm,), lambda i: (i,), memory_space=pltpu.SMEM),
                  pl.BlockSpec(memory_space=pltpu.SMEM),
                  pl.BlockSpec((tm, d), lambda i: (i, 0))],
        out_specs=pl.BlockSpec(memory_space=pl.ANY),
        out_shape=jax.ShapeDtypeStruct((n_rows * nch, LANES), F32),
        scratch_shapes=[pltpu.VMEM((2, tm * nch, LANES), F32), pltpu.VMEM((EXPERT_TILE * nch, LANES), F32),
                        pltpu.SemaphoreType.DMA((2,)), pltpu.SemaphoreType.DMA(()), pltpu.SemaphoreType.DMA(())],
        compiler_params=_params("arbitrary"), name="dispatch",
    )(dest, last_blk, x1)


def _expert_kernel(first_ref, count_ref, widx_ref, nused_ref, parts_ref, wg_ref, wu_ref, wd_ref, xs_ref, ys_ref,
                   wgb_ref, wub_ref, wdb_ref, xbuf_ref, ybuf_ref, in_sem, out_sem):
    del widx_ref
    e = pl.program_id(0)
    nused = nused_ref[0]
    d = wg_ref.shape[1]
    nch = _token_rows(d)
    rows = xbuf_ref.shape[1]
    tb = rows // nch
    prow = rows // EXPERT_PARTS

    def in_part(b, slot, p):
        src = xs_ref.at[pl.ds(pl.multiple_of(b * rows + p * prow, prow), prow), :]
        return pltpu.make_async_copy(src, xbuf_ref.at[slot, pl.ds(p * prow, prow), :], in_sem.at[slot])

    def out_part(b, slot, p):
        dst = ys_ref.at[pl.ds(pl.multiple_of(b * rows + p * prow, prow), prow), :]
        return pltpu.make_async_copy(ybuf_ref.at[slot, pl.ds(p * prow, prow), :], dst, out_sem.at[slot])

    def for_parts(b, fn):
        used = parts_ref[b]
        for p in range(EXPERT_PARTS):
            @pl.when(p < used)
            def _(p=p):
                fn(p)

    class _Block:
        def __init__(self, part, b, slot):
            self.part, self.b, self.slot = part, b, slot

        def start(self):
            for_parts(self.b, lambda p: self.part(self.b, self.slot, p).start())

        def wait(self):
            for_parts(self.b, lambda p: self.part(self.b, self.slot, p).wait())

    def in_copy(b, slot):
        return _Block(in_part, b, slot)

    def out_copy(b, slot):
        return _Block(out_part, b, slot)

    n_in = xbuf_ref.shape[0]

    @pl.when(e == 0)
    def _():
        xbuf_ref[...] = jnp.zeros_like(xbuf_ref)
        for b0 in range(n_in - 1):
            @pl.when(b0 < nused)
            def _():
                in_copy(b0, b0).start()

    @pl.when(count_ref[e] > 0)
    def _():
        wgb_ref[...] = wg_ref[0].astype(BF16)
        wub_ref[...] = wu_ref[0].astype(BF16)
        wdb_ref[...] = wd_ref[0].astype(BF16)

    def body(b, _):
        slot = b % n_in
        oslot = b % 2
        in_copy(b, slot).wait()

        @pl.when(b + n_in - 1 < nused)
        def _():
            in_copy(b + n_in - 1, (b + n_in - 1) % n_in).start()

        @pl.when(b >= 2)
        def _():
            out_copy(b - 2, oslot).wait()

        xb = _from_token_tiles(xbuf_ref.at[slot], tb, d).astype(BF16)
        g = _dot(xb, wgb_ref[...])
        u = _dot(xb, wub_ref[...])
        hmid = g * jax.nn.sigmoid(g) * u
        _to_token_tiles(ybuf_ref.at[oslot], _dot(hmid.astype(BF16), wdb_ref[...]))
        out_copy(b, oslot).start()
        return 0

    lax.fori_loop(first_ref[e], first_ref[e] + count_ref[e], body, 0)

    @pl.when(e == pl.num_programs(0) - 1)
    def _():
        for back in (2, 1):
            @pl.when(nused >= back)
            def _():
                out_copy(nused - back, (nused - back) % 2).wait()


def _experts(first_blk, blk_count, w_idx, nused, blk_parts, xs, w_gate, w_up, w_down):
    n_exp, d, dff = w_gate.shape
    nch = _token_rows(d)
    rows = EXPERT_TILE * nch
    w_spec = lambda shape: pl.BlockSpec(shape, lambda e, fb, bc, wi, nu, bp: (wi[e], 0, 0))
    any_spec = pl.BlockSpec(memory_space=pl.ANY)
    grid_spec = pltpu.PrefetchScalarGridSpec(
        num_scalar_prefetch=5, grid=(n_exp,),
        in_specs=[w_spec((1, d, dff)), w_spec((1, d, dff)), w_spec((1, dff, d)), any_spec],
        out_specs=any_spec,
        scratch_shapes=[pltpu.VMEM((d, dff), BF16), pltpu.VMEM((d, dff), BF16), pltpu.VMEM((dff, d), BF16),
                        pltpu.VMEM((EXPERT_IN_SLOTS, rows, LANES), F32), pltpu.VMEM((2, rows, LANES), F32),
                        pltpu.SemaphoreType.DMA((EXPERT_IN_SLOTS,)), pltpu.SemaphoreType.DMA((2,))],
    )
    return pl.pallas_call(
        _expert_kernel, grid_spec=grid_spec, out_shape=jax.ShapeDtypeStruct(xs.shape, F32),
        input_output_aliases={8: 0},
        compiler_params=_params("arbitrary"), name="experts",
    )(first_blk, blk_count, w_idx, nused, blk_parts, w_gate, w_up, w_down, xs)


def _combine_kernel(dest_ref, dest_next_ref, x1_ref, rw_ref, g_ref, b_ref, ys_ref, o_ref, buf_ref, sem):
    tm, d = x1_ref.shape
    nch = _token_rows(d)
    step = pl.program_id(0)
    slot = step % 2

    def gather(idx_ref, which):
        def start(r, _):
            for k in range(2):
                _token_copy(ys_ref, _slot(idx_ref, r, k), buf_ref.at[which, k], r, nch,
                            sem.at[which]).start(priority=k)
            return 0
        lax.fori_loop(0, tm, start, 0, unroll=8)

    @pl.when(step == 0)
    def _():
        gather(dest_ref, 0)

    @pl.when(step + 1 < pl.num_programs(0))
    def _():
        gather(dest_next_ref, 1 - slot)

    def wait(r, _):
        for k in range(2):
            _token_copy(ys_ref, 0, buf_ref.at[slot, k], 0, nch, sem.at[slot]).wait()
        return 0

    lax.fori_loop(0, tm, wait, 0, unroll=8)
    rw = rw_ref[...]
    y0 = _from_token_tiles(buf_ref.at[slot, 0], tm, d)
    y1 = _from_token_tiles(buf_ref.at[slot, 1], tm, d)
    ffn = rw[:, 0:1] * y0 + rw[:, 1:2] * y1
    o_ref[...] = _layer_norm(DEEPNORM_ALPHA * x1_ref[...] + ffn, g_ref[...], b_ref[...])


def _combine(dest, x1, rw, ln_g, ln_b, ys):
    n, d = x1.shape
    tm = ROW_TILE
    nch = _token_rows(d)
    last = n // tm - 1
    row = lambda w: pl.BlockSpec((tm, w), lambda i: (i, 0))
    return pl.pallas_call(
        _combine_kernel, grid=(n // tm,),
        in_specs=[pl.BlockSpec((2 * tm,), lambda i: (i,), memory_space=pltpu.SMEM),
                  pl.BlockSpec((2 * tm,), lambda i: (jnp.minimum(i + 1, last),), memory_space=pltpu.SMEM),
                  row(d), row(LANES), _full(ln_g.shape), _full(ln_b.shape),
                  pl.BlockSpec(memory_space=pl.ANY)],
        out_specs=row(d),
        out_shape=jax.ShapeDtypeStruct((n, d), F32),
        scratch_shapes=[pltpu.VMEM((2, 2, tm * nch, LANES), F32), pltpu.SemaphoreType.DMA((2,))],
        compiler_params=_params("arbitrary"), name="combine",
    )(dest, dest, x1, rw, ln_g, ln_b, ys)


def _rope_tables(seq):
    half = ATTN_HEAD_DIM // 2
    inv_freq = ROPE_THETA ** (-np.arange(half, dtype=np.float64) / half)
    ang = np.arange(seq, dtype=np.float64)[:, None] * inv_freq[None, :]
    cos = np.cos(ang)
    sin = np.sin(ang)
    cos_h = np.concatenate([cos, cos], axis=1)
    sin_h = np.concatenate([-sin, sin], axis=1)
    return (jnp.asarray(np.tile(cos_h, (1, ATTN_HEADS)), F32), jnp.asarray(np.tile(sin_h, (1, ATTN_HEADS)), F32))


def _pad_lanes(a, width=LANES):
    return jnp.pad(a, ((0, 0), (0, width - a.shape[1])))


def kernel(x, ln0_g, ln0_b, w_in, conv_w, conv_b, w_mq, w_mk, b_i, b_f, gn_g, skip, w_attn_up, w_mlstm_up, w_out,
           ln1_g, ln1_b, w_router_group, b_router_group, w_router_expert, b_router_expert, w_gate, w_up, w_down,
           ln2_g, ln2_b):
    batch, seq, d = x.shape
    n = batch * seq
    assert seq % ROW_TILE == 0 and ROW_TILE == MOBA_BLOCK and w_in.shape[0] == DEPTH
    x2 = x.reshape(n, d)
    vec = lambda a: a.reshape(1, -1).astype(F32)

    wt = w_in[0].T
    cos, sin = _rope_tables(seq)

    b_if = jnp.concatenate([b_i[0], b_f[0]]).astype(F32)
    q, k, v, kmean, ga, gm, xn, ym = _front(
        x2, vec(ln0_g), vec(ln0_b), wt, cos, sin,
        conv_w[0], vec(conv_b[0]), w_mq[0].transpose(0, 2, 1).astype(BF16), w_mk[0].astype(BF16), b_if[:, None],
        gn_g[0].astype(F32)[:, None], skip[0].astype(F32)[:, None], batch, seq)

    nb = seq // MOBA_BLOCK
    km = kmean.reshape(batch, nb, ATTN_HEADS, ATTN_HEAD_DIM).transpose(0, 2, 1, 3)
    ya = _moba(q, k, v, km).reshape(n, ATTN_WIDTH)

    w_r = _pad_lanes(jnp.concatenate([w_router_expert[0], w_router_group[0]], axis=1))
    w_r_hi = w_r.astype(BF16)
    w_r_lo = (w_r - w_r_hi.astype(F32)).astype(BF16)
    w_rc = jnp.concatenate([w_r_hi.T, w_r_lo.T], axis=0)
    b_r = _pad_lanes(jnp.concatenate([b_router_expert[0], b_router_group[0]])[None, :]).T
    x1, ri, rw, counts = _mix(
        xn, ya, ym, ga, gm, w_attn_up[0].astype(BF16), w_mlstm_up[0].astype(BF16),
        w_out[0].astype(BF16), vec(ln1_g[0]), vec(ln1_b[0]), w_rc, b_r)

    tb = EXPERT_TILE
    nblk = (2 * n) // tb + MOE_EXPERTS
    cnt = counts[:MOE_EXPERTS, 0].astype(jnp.int32)
    nblk_e = (cnt + tb - 1) // tb
    blk_end = jnp.cumsum(nblk_e)
    pad_start = (blk_end - nblk_e) * tb
    nused = blk_end[-1:]
    ids = jnp.arange(MOE_EXPERTS, dtype=jnp.int32)
    prev_used = jnp.max(jnp.where((ids[None, :] <= ids[:, None]) & (nblk_e[None, :] > 0), ids[None, :], -1), axis=1)
    first_used = jnp.min(jnp.where(nblk_e > 0, ids, MOE_EXPERTS - 1))
    w_idx = jnp.where(prev_used >= 0, prev_used, first_used).astype(jnp.int32)
    last_blk = jnp.where(nblk_e > 0, (blk_end - 1) * tb, -1)
    last_blk = jnp.concatenate([last_blk, nused]).astype(jnp.int32)
    dest = _slots(ri, pad_start.astype(F32)[:, None])
    dest = dest[:2].reshape(2, n // ROW_TILE, ROW_TILE).transpose(1, 0, 2).reshape(2 * n)

    first_blk = blk_end - nblk_e
    blk_ids = jnp.arange(nblk, dtype=jnp.int32)[:, None]
    owner = (first_blk[None, :] <= blk_ids) & (blk_ids < blk_end[None, :])
    blk_tokens = jnp.sum(jnp.where(owner, jnp.clip(cnt[None, :] - (blk_ids - first_blk[None, :]) * tb, 0, tb), 0), axis=1)
    piece = tb // EXPERT_PARTS
    blk_parts = ((blk_tokens + piece - 1) // piece).astype(jnp.int32)

    xs = _dispatch(dest, last_blk, x1, nblk * tb)
    ys = _experts(first_blk.astype(jnp.int32), nblk_e.astype(jnp.int32), w_idx, nused.astype(jnp.int32), blk_parts,
                  xs, w_gate[0], w_up[0], w_down[0])
    out = _combine(dest, x1, rw, vec(ln2_g[0]), vec(ln2_b[0]), ys)
    return out.reshape(batch, seq, d)
```
